```python
import jax, jax.numpy as jnp
from jax import lax
import numpy as np

D_MODEL = 1024
BATCH = 8
SEQ = 8192
DEPTH = 1

N_META = 16
RG_WIDTH = (4 * D_MODEL // 3) // 128 * 128
RG_BLOCKS = 16
RG_BLOCK = RG_WIDTH // RG_BLOCKS
RG_CONV = 4
RG_C = 8.0
HG_HEADS = 8
HG_DK = D_MODEL // HG_HEADS
HG_DV = D_MODEL // HG_HEADS
HG_WIDTH = HG_HEADS * HG_DV
HG_CHUNK = 64
D_FF = ((8 * D_MODEL // 3 + 255) // 256) * 256
NORM_EPS = 1e-6
SPLITS = [RG_WIDTH, RG_WIDTH, HG_WIDTH, HG_WIDTH, HG_WIDTH, HG_WIDTH, D_MODEL, D_MODEL]
D_IN = sum(SPLITS)

kernel_name = "hybrid_rglru_hgrn2_gated_block"


def _rmsnorm(x, g):
    xf = x.astype(jnp.float32)
    y = xf * lax.rsqrt(jnp.mean(xf * xf, axis=-1, keepdims=True) + NORM_EPS)
    return (y * g.astype(jnp.float32)).astype(x.dtype)


def _causal_depthwise_conv(x, w, b):
    k = w.shape[0]
    y = lax.conv_general_dilated(
        x, w[:, None, :], window_strides=(1,), padding=[(k - 1, 0)],
        dimension_numbers=("NWC", "WIO", "NWC"), feature_group_count=x.shape[-1])
    return y + b


def _rg_lru(x, w_a, b_a, w_x, b_x, lam):
    bsz, t, w = x.shape
    xb = x.reshape(bsz, t, RG_BLOCKS, RG_BLOCK)
    r = jax.nn.sigmoid(jnp.einsum("btnd,nde->btne", xb, w_a).reshape(bsz, t, w) + b_a)
    i = jax.nn.sigmoid(jnp.einsum("btnd,nde->btne", xb, w_x).reshape(bsz, t, w) + b_x)
    log_a = -RG_C * r.astype(jnp.float32) * jax.nn.softplus(-lam.astype(jnp.float32))
    a = jnp.exp(log_a)
    u = jnp.sqrt(-jnp.expm1(2.0 * log_a)) * (i * x).astype(jnp.float32)

    def combine(left, right):
        a1, b1 = left
        a2, b2 = right
        return a1 * a2, a2 * b1 + b2

    _, h = lax.associative_scan(combine, (a, u), axis=1)
    return h.astype(x.dtype)


def _hgrn2_chunkwise(q, k, v, log_f):
    bsz, nh, t, dk = q.shape
    dv = v.shape[-1]
    n = t // HG_CHUNK

    def to_chunks(z):
        z = z.astype(jnp.float32).reshape(bsz, nh, n, HG_CHUNK, z.shape[-1])
        return jnp.moveaxis(z, 2, 0)

    causal = jnp.tril(jnp.ones((HG_CHUNK, HG_CHUNK), dtype=bool))

    def step(s_prev, inp):
        qc, kc, vc, gc = inp
        bcum = jnp.cumsum(gc, axis=-2)
        o_inter = jnp.einsum("bhtk,bhkv->bhtv", qc * jnp.exp(bcum), s_prev)
        diff = bcum[:, :, :, None, :] - bcum[:, :, None, :, :]
        decay = jnp.exp(jnp.where(causal[:, :, None], diff, -jnp.inf))
        attn = jnp.einsum("bhtk,bhtsk,bhsk->bhts", qc, decay, kc)
        o_intra = jnp.einsum("bhts,bhsv->bhtv", attn, vc)
        b_last = bcum[:, :, -1:, :]
        k_dec = kc * jnp.exp(b_last - bcum)
        s_new = jnp.exp(b_last[:, :, 0, :])[..., None] * s_prev + jnp.einsum("bhsk,bhsv->bhkv", k_dec, vc)
        return s_new, o_inter + o_intra

    s0 = jnp.zeros((bsz, nh, dk, dv), jnp.float32)
    _, o = lax.scan(step, s0, (to_chunks(q), to_chunks(k), to_chunks(v), to_chunks(log_f)))
    o = jnp.moveaxis(o, 0, 2).reshape(bsz, nh, t, dv)
    return o


def _fwd_setup_inputs(seed: int = 0) -> dict:
    key = jax.random.key(seed)
    ks = jax.random.split(key, 24)
    f32 = jnp.float32

    def nrm(k, shape, scale):
        return jax.random.normal(k, shape, f32) * scale

    a_c = jax.random.uniform(ks[10], (DEPTH, RG_WIDTH), f32, 0.9, 0.999)
    p = a_c ** (1.0 / RG_C)
    rg_lambda = jnp.log(p) - jnp.log1p(-p)
    return {
        "x": nrm(ks[0], (BATCH, SEQ, D_MODEL), 1.0),
        "meta_tokens": nrm(ks[1], (N_META, D_MODEL), 1.0),
        "norm1_g": 1.0 + nrm(ks[2], (DEPTH, D_MODEL), 0.01),
        "w_in": nrm(ks[3], (DEPTH, D_MODEL, D_IN), D_MODEL ** -0.5),
        "conv_w": nrm(ks[4], (DEPTH, RG_CONV, RG_WIDTH), RG_CONV ** -0.5),
        "conv_b": nrm(ks[5], (DEPTH, RG_WIDTH), 0.01),
        "rg_wa": nrm(ks[6], (DEPTH, RG_BLOCKS, RG_BLOCK, RG_BLOCK), RG_BLOCK ** -0.5),
        "rg_ba": nrm(ks[7], (DEPTH, RG_WIDTH), 0.01),
        "rg_wx": nrm(ks[8], (DEPTH, RG_BLOCKS, RG_BLOCK, RG_BLOCK), RG_BLOCK ** -0.5),
        "rg_bx": nrm(ks[9], (DEPTH, RG_WIDTH), 0.01),
        "rg_lambda": rg_lambda,
        "hg_lb_logits": nrm(ks[11], (DEPTH + 1, HG_HEADS * HG_DK), 1.0),
        "hg_norm_g": 1.0 + nrm(ks[12], (DEPTH, HG_DV), 0.01),
        "w_proj_a": nrm(ks[13], (DEPTH, RG_WIDTH, D_MODEL), RG_WIDTH ** -0.5),
        "w_proj_b": nrm(ks[14], (DEPTH, HG_WIDTH, D_MODEL), HG_WIDTH ** -0.5),
        "w_out": nrm(ks[15], (DEPTH, D_MODEL, D_MODEL), D_MODEL ** -0.5),
        "norm2_g": 1.0 + nrm(ks[16], (DEPTH, D_MODEL), 0.01),
        "w_ffn_in": nrm(ks[17], (DEPTH, D_MODEL, 2 * D_FF), D_MODEL ** -0.5),
        "w_ffn_down": nrm(ks[18], (DEPTH, D_FF, D_MODEL), D_FF ** -0.5),
        "norm_f_g": 1.0 + nrm(ks[19], (D_MODEL,), 0.01),
    }


def _fwd_reference(x, meta_tokens, norm1_g, w_in, conv_w, conv_b, rg_wa, rg_ba, rg_wx, rg_bx,
              rg_lambda, hg_lb_logits, hg_norm_g, w_proj_a, w_proj_b, w_out, norm2_g,
              w_ffn_in, w_ffn_down, norm_f_g):
    bsz = x.shape[0]
    meta = jnp.broadcast_to(meta_tokens[None].astype(x.dtype), (bsz, N_META, D_MODEL))
    h = jnp.concatenate([meta, x], axis=1)
    t_total = h.shape[1]
    pad = HG_CHUNK - N_META

    lower_bounds = jnp.cumsum(jax.nn.softmax(hg_lb_logits.astype(jnp.float32), axis=0), axis=0)
    split_idx = list(np.cumsum(SPLITS)[:-1])

    for l in range(DEPTH):
        hn = _rmsnorm(h, norm1_g[l])
        proj = jnp.einsum("btd,de->bte", hn, w_in[l])
        a_x, a_gate, b_q, b_f, b_i, b_g, gate_a, gate_b = jnp.split(proj, split_idx, axis=-1)

        a_x = _causal_depthwise_conv(a_x, conv_w[l], conv_b[l])
        y_a = _rg_lru(a_x, rg_wa[l], rg_ba[l], rg_wx[l], rg_bx[l], rg_lambda[l]) * jax.nn.gelu(a_gate)

        lb = lower_bounds[l].astype(jnp.float32)
        f_logit = b_f.astype(jnp.float32)
        log_f = jnp.log(lb + (1.0 - lb) * jax.nn.sigmoid(f_logit))
        k_in = (1.0 - lb) * jax.nn.sigmoid(-f_logit)
        q_in = jax.nn.silu(b_q)

        def heads(z, d):
            z = z.reshape(bsz, t_total, HG_HEADS, d).transpose(0, 2, 1, 3)
            return jnp.pad(z, ((0, 0), (0, 0), (pad, 0), (0, 0)))

        o_b = _hgrn2_chunkwise(heads(q_in, HG_DK), heads(k_in, HG_DK),
                               heads(b_i, HG_DV), heads(log_f, HG_DK))
        o_b = o_b[:, :, pad:, :].transpose(0, 2, 1, 3)
        g_heads = b_g.reshape(bsz, t_total, HG_HEADS, HG_DV)
        y_b = (_rmsnorm(o_b, hg_norm_g[l]).astype(h.dtype) * jax.nn.silu(g_heads)).reshape(bsz, t_total, HG_WIDTH)

        merged = (jax.nn.sigmoid(gate_a) * jnp.einsum("btw,wd->btd", y_a, w_proj_a[l])
                  + jax.nn.sigmoid(gate_b) * jnp.einsum("btw,wd->btd", y_b, w_proj_b[l]))
        h = h + jnp.einsum("btd,de->bte", merged, w_out[l])

        hn = _rmsnorm(h, norm2_g[l])
        gu = jnp.einsum("btd,df->btf", hn, w_ffn_in[l])
        g_ff, u_ff = jnp.split(gu, [D_FF], axis=-1)
        h = h + jnp.einsum("btf,fd->btd", jax.nn.silu(g_ff) * u_ff, w_ffn_down[l])

    h = _rmsnorm(h, norm_f_g)
    return h[:, N_META:, :]


import jax as _jax
import jax.numpy as _jnp

TWIN_FORMAT = 'train_step'
FWD_PARAMS = ['x', 'meta_tokens', 'norm1_g', 'w_in', 'conv_w', 'conv_b', 'rg_wa', 'rg_ba', 'rg_wx', 'rg_bx', 'rg_lambda', 'hg_lb_logits', 'hg_norm_g', 'w_proj_a', 'w_proj_b', 'w_out', 'norm2_g', 'w_ffn_in', 'w_ffn_down', 'norm_f_g']
TWIN_WEIGHTS = ['meta_tokens', 'norm1_g', 'w_in', 'conv_w', 'conv_b', 'rg_wa', 'rg_ba', 'rg_wx', 'rg_bx', 'rg_lambda', 'hg_lb_logits', 'hg_norm_g', 'w_proj_a', 'w_proj_b', 'w_out', 'norm2_g', 'w_ffn_in', 'w_ffn_down', 'norm_f_g']
TWIN_DIFF_INPUT = 'x'
TWIN_INPUTS = ['x', 'meta_tokens', 'norm1_g', 'w_in', 'conv_w', 'conv_b', 'rg_wa', 'rg_ba', 'rg_wx', 'rg_bx', 'rg_lambda', 'hg_lb_logits', 'hg_norm_g', 'w_proj_a', 'w_proj_b', 'w_out', 'norm2_g', 'w_ffn_in', 'w_ffn_down', 'norm_f_g', 'loss_target', 'm_meta_tokens', 'm_norm1_g', 'm_w_in', 'm_conv_w', 'm_conv_b', 'm_rg_wa', 'm_rg_ba', 'm_rg_wx', 'm_rg_bx', 'm_rg_lambda', 'm_hg_lb_logits', 'm_hg_norm_g', 'm_w_proj_a', 'm_w_proj_b', 'm_w_out', 'm_norm2_g', 'm_w_ffn_in', 'm_w_ffn_down', 'm_norm_f_g', 'v_meta_tokens', 'v_norm1_g', 'v_w_in', 'v_conv_w', 'v_conv_b', 'v_rg_wa', 'v_rg_ba', 'v_rg_wx', 'v_rg_bx', 'v_rg_lambda', 'v_hg_lb_logits', 'v_hg_norm_g', 'v_w_proj_a', 'v_w_proj_b', 'v_w_out', 'v_norm2_g', 'v_w_ffn_in', 'v_w_ffn_down', 'v_norm_f_g']
TWIN_OUTPUTS = ['loss', 'grad_x', 'grad_meta_tokens', 'grad_norm1_g', 'grad_w_in', 'grad_conv_w', 'grad_conv_b', 'grad_rg_wa', 'grad_rg_ba', 'grad_rg_wx', 'grad_rg_bx', 'grad_rg_lambda', 'grad_hg_lb_logits', 'grad_hg_norm_g', 'grad_w_proj_a', 'grad_w_proj_b', 'grad_w_out', 'grad_norm2_g', 'grad_w_ffn_in', 'grad_w_ffn_down', 'grad_norm_f_g', 'delta_meta_tokens', 'delta_norm1_g', 'delta_w_in', 'delta_conv_w', 'delta_conv_b', 'delta_rg_wa', 'delta_rg_ba', 'delta_rg_wx', 'delta_rg_bx', 'delta_rg_lambda', 'delta_hg_lb_logits', 'delta_hg_norm_g', 'delta_w_proj_a', 'delta_w_proj_b', 'delta_w_out', 'delta_norm2_g', 'delta_w_ffn_in', 'delta_w_ffn_down', 'delta_norm_f_g', 'new_m_meta_tokens', 'new_m_norm1_g', 'new_m_w_in', 'new_m_conv_w', 'new_m_conv_b', 'new_m_rg_wa', 'new_m_rg_ba', 'new_m_rg_wx', 'new_m_rg_bx', 'new_m_rg_lambda', 'new_m_hg_lb_logits', 'new_m_hg_norm_g', 'new_m_w_proj_a', 'new_m_w_proj_b', 'new_m_w_out', 'new_m_norm2_g', 'new_m_w_ffn_in', 'new_m_w_ffn_down', 'new_m_norm_f_g', 'new_v_meta_tokens', 'new_v_norm1_g', 'new_v_w_in', 'new_v_conv_w', 'new_v_conv_b', 'new_v_rg_wa', 'new_v_rg_ba', 'new_v_rg_wx', 'new_v_rg_bx', 'new_v_rg_lambda', 'new_v_hg_lb_logits', 'new_v_hg_norm_g', 'new_v_w_proj_a', 'new_v_w_proj_b', 'new_v_w_out', 'new_v_norm2_g', 'new_v_w_ffn_in', 'new_v_w_ffn_down', 'new_v_norm_f_g']
TWIN_LEAF_KINDS = {'loss': 'loss', 'grad_x': 'grad_x', 'grad_meta_tokens': 'grad_w', 'grad_norm1_g': 'grad_w', 'grad_w_in': 'grad_w', 'grad_conv_w': 'grad_w', 'grad_conv_b': 'grad_w', 'grad_rg_wa': 'grad_w', 'grad_rg_ba': 'grad_w', 'grad_rg_wx': 'grad_w', 'grad_rg_bx': 'grad_w', 'grad_rg_lambda': 'grad_w', 'grad_hg_lb_logits': 'grad_w', 'grad_hg_norm_g': 'grad_w', 'grad_w_proj_a': 'grad_w', 'grad_w_proj_b': 'grad_w', 'grad_w_out': 'grad_w', 'grad_norm2_g': 'grad_w', 'grad_w_ffn_in': 'grad_w', 'grad_w_ffn_down': 'grad_w', 'grad_norm_f_g': 'grad_w', 'delta_meta_tokens': 'delta_w', 'delta_norm1_g': 'delta_w', 'delta_w_in': 'delta_w', 'delta_conv_w': 'delta_w', 'delta_conv_b': 'delta_w', 'delta_rg_wa': 'delta_w', 'delta_rg_ba': 'delta_w', 'delta_rg_wx': 'delta_w', 'delta_rg_bx': 'delta_w', 'delta_rg_lambda': 'delta_w', 'delta_hg_lb_logits': 'delta_w', 'delta_hg_norm_g': 'delta_w', 'delta_w_proj_a': 'delta_w', 'delta_w_proj_b': 'delta_w', 'delta_w_out': 'delta_w', 'delta_norm2_g': 'delta_w', 'delta_w_ffn_in': 'delta_w', 'delta_w_ffn_down': 'delta_w', 'delta_norm_f_g': 'delta_w', 'new_m_meta_tokens': 'new_m', 'new_m_norm1_g': 'new_m', 'new_m_w_in': 'new_m', 'new_m_conv_w': 'new_m', 'new_m_conv_b': 'new_m', 'new_m_rg_wa': 'new_m', 'new_m_rg_ba': 'new_m', 'new_m_rg_wx': 'new_m', 'new_m_rg_bx': 'new_m', 'new_m_rg_lambda': 'new_m', 'new_m_hg_lb_logits': 'new_m', 'new_m_hg_norm_g': 'new_m', 'new_m_w_proj_a': 'new_m', 'new_m_w_proj_b': 'new_m', 'new_m_w_out': 'new_m', 'new_m_norm2_g': 'new_m', 'new_m_w_ffn_in': 'new_m', 'new_m_w_ffn_down': 'new_m', 'new_m_norm_f_g': 'new_m', 'new_v_meta_tokens': 'new_v', 'new_v_norm1_g': 'new_v', 'new_v_w_in': 'new_v', 'new_v_conv_w': 'new_v', 'new_v_conv_b': 'new_v', 'new_v_rg_wa': 'new_v', 'new_v_rg_ba': 'new_v', 'new_v_rg_wx': 'new_v', 'new_v_rg_bx': 'new_v', 'new_v_rg_lambda': 'new_v', 'new_v_hg_lb_logits': 'new_v', 'new_v_hg_norm_g': 'new_v', 'new_v_w_proj_a': 'new_v', 'new_v_w_proj_b': 'new_v', 'new_v_w_out': 'new_v', 'new_v_norm2_g': 'new_v', 'new_v_w_ffn_in': 'new_v', 'new_v_w_ffn_down': 'new_v', 'new_v_norm_f_g': 'new_v'}


def _forward(args):
    return _fwd_reference(*[args[k] for k in FWD_PARAMS])


def _output_shape():
    def fwd():
        inp = _fwd_setup_inputs(0)
        return _fwd_reference(*[inp[k] for k in FWD_PARAMS])
    out = _jax.eval_shape(fwd)
    return out.shape, out.dtype

N_MICROBATCH = 1
ADAM_LR = 0.001
ADAM_B1 = 0.9
ADAM_B2 = 0.999
ADAM_EPS = 1e-08
ADAM_WD = 0.01
ADAM_STEP = 10
PER_EXAMPLE_BATCH_AXIS = {'x': 0, 'loss_target': 0}
SHARED_INPUTS = []
_WEIGHT_DTYPES = {'meta_tokens': _jnp.float32, 'norm1_g': _jnp.float32, 'w_in': _jnp.float32, 'conv_w': _jnp.float32, 'conv_b': _jnp.float32, 'rg_wa': _jnp.float32, 'rg_ba': _jnp.float32, 'rg_wx': _jnp.float32, 'rg_bx': _jnp.float32, 'rg_lambda': _jnp.float32, 'hg_lb_logits': _jnp.float32, 'hg_norm_g': _jnp.float32, 'w_proj_a': _jnp.float32, 'w_proj_b': _jnp.float32, 'w_out': _jnp.float32, 'norm2_g': _jnp.float32, 'w_ffn_in': _jnp.float32, 'w_ffn_down': _jnp.float32, 'norm_f_g': _jnp.float32}
MOMENT_SCALE = {'meta_tokens': 3.362497e-03, 'norm1_g': 1.705334e-01, 'w_in': 5.372723e-02, 'conv_w': 5.891093e-02, 'conv_b': 6.804745e-01, 'rg_wa': 2.120541e-02, 'rg_ba': 1.568336e-02, 'rg_wx': 3.848514e-02, 'rg_bx': 1.944867e-02, 'rg_lambda': 3.185883e-02, 'hg_lb_logits': 7.080745e-03, 'hg_norm_g': 2.547430e-01, 'w_proj_a': 6.898744e-02, 'w_proj_b': 8.455813e-02, 'w_out': 1.075699e-01, 'norm2_g': 1.857361e-01, 'w_ffn_in': 7.475944e-02, 'w_ffn_down': 1.217957e-01, 'norm_f_g': 6.398940e+01}


def _to_microbatches(a, axis):
    t = _jnp.moveaxis(a, axis, 0)
    t = t.reshape((N_MICROBATCH, t.shape[0] // N_MICROBATCH) + t.shape[1:])
    return _jnp.moveaxis(t, 1, axis + 1)


def setup_inputs(seed: int = 0) -> dict:
    inp = _fwd_setup_inputs(seed)
    key = _jax.random.fold_in(_jax.random.key(seed), 7919)
    shape, _ = _output_shape()
    out = dict(inp)
    out["loss_target"] = _jax.random.normal(_jax.random.fold_in(key, 0), shape, _jnp.float32)
    for i, name in enumerate(TWIN_WEIGHTS):
        w = inp[name].astype(_jnp.float32)
        if MOMENT_SCALE is None:
            s = _jnp.sqrt(_jnp.mean(_jnp.square(w)) + 1e-30)
        else:
            s = MOMENT_SCALE[name]
        km, kv = _jax.random.split(_jax.random.fold_in(key, i + 1))
        out[name] = w
        out["m_" + name] = s * _jax.random.normal(km, w.shape, _jnp.float32)
        out["v_" + name] = (s * s) * _jax.random.uniform(kv, w.shape, _jnp.float32, 0.5, 1.5)
    if N_MICROBATCH > 1:
        for name, axis in PER_EXAMPLE_BATCH_AXIS.items():
            out[name] = _to_microbatches(out[name], axis)
    return {'x': out['x'], 'meta_tokens': out['meta_tokens'], 'norm1_g': out['norm1_g'], 'w_in': out['w_in'], 'conv_w': out['conv_w'], 'conv_b': out['conv_b'], 'rg_wa': out['rg_wa'], 'rg_ba': out['rg_ba'], 'rg_wx': out['rg_wx'], 'rg_bx': out['rg_bx'], 'rg_lambda': out['rg_lambda'], 'hg_lb_logits': out['hg_lb_logits'], 'hg_norm_g': out['hg_norm_g'], 'w_proj_a': out['w_proj_a'], 'w_proj_b': out['w_proj_b'], 'w_out': out['w_out'], 'norm2_g': out['norm2_g'], 'w_ffn_in': out['w_ffn_in'], 'w_ffn_down': out['w_ffn_down'], 'norm_f_g': out['norm_f_g'], 'loss_target': out['loss_target'], 'm_meta_tokens': out['m_meta_tokens'], 'm_norm1_g': out['m_norm1_g'], 'm_w_in': out['m_w_in'], 'm_conv_w': out['m_conv_w'], 'm_conv_b': out['m_conv_b'], 'm_rg_wa': out['m_rg_wa'], 'm_rg_ba': out['m_rg_ba'], 'm_rg_wx': out['m_rg_wx'], 'm_rg_bx': out['m_rg_bx'], 'm_rg_lambda': out['m_rg_lambda'], 'm_hg_lb_logits': out['m_hg_lb_logits'], 'm_hg_norm_g': out['m_hg_norm_g'], 'm_w_proj_a': out['m_w_proj_a'], 'm_w_proj_b': out['m_w_proj_b'], 'm_w_out': out['m_w_out'], 'm_norm2_g': out['m_norm2_g'], 'm_w_ffn_in': out['m_w_ffn_in'], 'm_w_ffn_down': out['m_w_ffn_down'], 'm_norm_f_g': out['m_norm_f_g'], 'v_meta_tokens': out['v_meta_tokens'], 'v_norm1_g': out['v_norm1_g'], 'v_w_in': out['v_w_in'], 'v_conv_w': out['v_conv_w'], 'v_conv_b': out['v_conv_b'], 'v_rg_wa': out['v_rg_wa'], 'v_rg_ba': out['v_rg_ba'], 'v_rg_wx': out['v_rg_wx'], 'v_rg_bx': out['v_rg_bx'], 'v_rg_lambda': out['v_rg_lambda'], 'v_hg_lb_logits': out['v_hg_lb_logits'], 'v_hg_norm_g': out['v_hg_norm_g'], 'v_w_proj_a': out['v_w_proj_a'], 'v_w_proj_b': out['v_w_proj_b'], 'v_w_out': out['v_w_out'], 'v_norm2_g': out['v_norm2_g'], 'v_w_ffn_in': out['v_w_ffn_in'], 'v_w_ffn_down': out['v_w_ffn_down'], 'v_norm_f_g': out['v_norm_f_g']}


def _loss(weights, diff, rest, loss_target):
    with _jax.named_scope("forward"):
        args = {**rest, TWIN_DIFF_INPUT: diff, **{k: w.astype(_WEIGHT_DTYPES[k]) for k, w in weights.items()}}
        y = _forward(args)
    with _jax.named_scope("loss_head"):
        err = _jnp.square(y.astype(_jnp.float32) - loss_target)
        return 0.5 * _jnp.sum(_jnp.mean(err, axis=-1)) if err.ndim else 0.5 * err


def _adamw(w, g, m, v):
    m = ADAM_B1 * m + (1.0 - ADAM_B1) * g
    v = ADAM_B2 * v + (1.0 - ADAM_B2) * _jnp.square(g)
    m_hat = m / (1.0 - ADAM_B1 ** ADAM_STEP)
    v_hat = v / (1.0 - ADAM_B2 ** ADAM_STEP)
    delta = -ADAM_LR * (m_hat / (_jnp.sqrt(v_hat) + ADAM_EPS) + ADAM_WD * w)
    return delta, m, v


def reference(x, meta_tokens, norm1_g, w_in, conv_w, conv_b, rg_wa, rg_ba, rg_wx, rg_bx, rg_lambda, hg_lb_logits, hg_norm_g, w_proj_a, w_proj_b, w_out, norm2_g, w_ffn_in, w_ffn_down, norm_f_g, loss_target, m_meta_tokens, m_norm1_g, m_w_in, m_conv_w, m_conv_b, m_rg_wa, m_rg_ba, m_rg_wx, m_rg_bx, m_rg_lambda, m_hg_lb_logits, m_hg_norm_g, m_w_proj_a, m_w_proj_b, m_w_out, m_norm2_g, m_w_ffn_in, m_w_ffn_down, m_norm_f_g, v_meta_tokens, v_norm1_g, v_w_in, v_conv_w, v_conv_b, v_rg_wa, v_rg_ba, v_rg_wx, v_rg_bx, v_rg_lambda, v_hg_lb_logits, v_hg_norm_g, v_w_proj_a, v_w_proj_b, v_w_out, v_norm2_g, v_w_ffn_in, v_w_ffn_down, v_norm_f_g):
    given = dict(x=x, meta_tokens=meta_tokens, norm1_g=norm1_g, w_in=w_in, conv_w=conv_w, conv_b=conv_b, rg_wa=rg_wa, rg_ba=rg_ba, rg_wx=rg_wx, rg_bx=rg_bx, rg_lambda=rg_lambda, hg_lb_logits=hg_lb_logits, hg_norm_g=hg_norm_g, w_proj_a=w_proj_a, w_proj_b=w_proj_b, w_out=w_out, norm2_g=norm2_g, w_ffn_in=w_ffn_in, w_ffn_down=w_ffn_down, norm_f_g=norm_f_g, loss_target=loss_target, m_meta_tokens=m_meta_tokens, m_norm1_g=m_norm1_g, m_w_in=m_w_in, m_conv_w=m_conv_w, m_conv_b=m_conv_b, m_rg_wa=m_rg_wa, m_rg_ba=m_rg_ba, m_rg_wx=m_rg_wx, m_rg_bx=m_rg_bx, m_rg_lambda=m_rg_lambda, m_hg_lb_logits=m_hg_lb_logits, m_hg_norm_g=m_hg_norm_g, m_w_proj_a=m_w_proj_a, m_w_proj_b=m_w_proj_b, m_w_out=m_w_out, m_norm2_g=m_norm2_g, m_w_ffn_in=m_w_ffn_in, m_w_ffn_down=m_w_ffn_down, m_norm_f_g=m_norm_f_g, v_meta_tokens=v_meta_tokens, v_norm1_g=v_norm1_g, v_w_in=v_w_in, v_conv_w=v_conv_w, v_conv_b=v_conv_b, v_rg_wa=v_rg_wa, v_rg_ba=v_rg_ba, v_rg_wx=v_rg_wx, v_rg_bx=v_rg_bx, v_rg_lambda=v_rg_lambda, v_hg_lb_logits=v_hg_lb_logits, v_hg_norm_g=v_hg_norm_g, v_w_proj_a=v_w_proj_a, v_w_proj_b=v_w_proj_b, v_w_out=v_w_out, v_norm2_g=v_norm2_g, v_w_ffn_in=v_w_ffn_in, v_w_ffn_down=v_w_ffn_down, v_norm_f_g=v_norm_f_g)
    weights = {n: given[n] for n in TWIN_WEIGHTS}
    shared = {n: given[n] for n in SHARED_INPUTS}
    per_example = {n: given[n] for n in ['x']}
    grad_fn = _jax.value_and_grad(_loss, argnums=(0, 1))

    def one_microbatch(ex, loss_target):
        ex = dict(ex)
        diff = ex.pop(TWIN_DIFF_INPUT)
        return grad_fn(weights, diff, {**shared, **ex}, loss_target)

    if N_MICROBATCH == 1:
        loss, (grad_w, grad_x) = one_microbatch(per_example, given["loss_target"])
    else:
        def body(carry, xs):
            loss_sum, grad_sum = carry
            l_k, (gw_k, gx_k) = one_microbatch(xs[0], xs[1])
            with _jax.named_scope("update"):
                return (loss_sum + l_k, _jax.tree.map(_jnp.add, grad_sum, gw_k)), gx_k

        init = (_jnp.zeros((), _jnp.float32), _jax.tree.map(_jnp.zeros_like, weights))
        (loss, grad_w), grad_x = _jax.lax.scan(body, init, (per_example, given["loss_target"]))
    with _jax.named_scope("update"):
        delta_w, new_m, new_v = {}, {}, {}
        for n in TWIN_WEIGHTS:
            delta_w[n], new_m[n], new_v[n] = _adamw(weights[n], grad_w[n], given["m_" + n], given["v_" + n])
    return (loss, grad_x, *[grad_w[n] for n in TWIN_WEIGHTS], *[delta_w[n] for n in TWIN_WEIGHTS],
            *[new_m[n] for n in TWIN_WEIGHTS], *[new_v[n] for n in TWIN_WEIGHTS])
```

```python
import functools

import jax
import jax.numpy as jnp
from jax import lax
from jax.experimental import pallas as pl
from jax.experimental.pallas import tpu as pltpu

F32, BF16 = jnp.float32, jnp.bfloat16
D = 1024
N_META = 16
RGW = 1280
RG_BLOCKS, RG_BLOCK = 16, 80
RG_C = 8.0
HEADS, HD = 8, 128
HGW = HEADS * HD
DFF = 2816
D_IN = 2 * RGW + 6 * D
ROT = 2 * RGW
EPS = 1e-6
CH = 64
TM = 256
ROW0 = TM
PAD = ROW0 - N_META
CT = 256
NCT = RGW // CT
EXP_CLAMP = 80.0
VMEM_LIMIT = 56 * 1024 * 1024

LR, B1, B2, ADAM_EPS, WD, STEP = 0.001, 0.9, 0.999, 1e-08, 0.01, 10
MESH = pl.DeviceIdType.MESH
ANY = pl.BlockSpec(memory_space=pl.ANY)


def _cparams(sem):
    return pltpu.CompilerParams(dimension_semantics=sem, vmem_limit_bytes=VMEM_LIMIT)


def _pick(n, prefs):
    for p in prefs:
        if n % p == 0:
            return p
    return n


def _sig(x):
    return jax.nn.sigmoid(x)


def _dot(a, b, dims, precision=None):
    return lax.dot_general(a, b, (dims, ((), ())), preferred_element_type=F32, precision=precision)


NN, NT, TN_ = ((1,), (0,)), ((1,), (1,)), ((0,), (0,))


def _mm(a, b, mode, out_dtype, name, resid=None):
    if mode == "nn":
        (m, kd), n = a.shape, b.shape[1]
    elif mode == "nt":
        (m, kd), n = a.shape, b.shape[0]
    else:
        (kd, m), n = a.shape, b.shape[1]
    tm = _pick(m, (768, 512, 640, 256, 128))
    tn = _pick(n, (1280, 1024, 1408, 512, 256, 128))
    tk = _pick(kd, (512, 768, 640, 1408, 256, 128))
    nk = kd // tk
    dims = {"nn": NN, "nt": NT, "tn": TN_}[mode]

    def body(*refs):
        if resid is None:
            a_ref, b_ref, o_ref, acc = refs
        else:
            a_ref, b_ref, r_ref, o_ref, acc = refs
        k = pl.program_id(2)

        @pl.when(k == 0)
        def _():
            acc[...] = jnp.zeros_like(acc)

        acc[...] += _dot(a_ref[...], b_ref[...], dims)

        @pl.when(k == nk - 1)
        def _():
            r = acc[...]
            if resid is not None:
                r = r + r_ref[...]
            o_ref[...] = r.astype(out_dtype)

    if mode == "tn":
        a_spec = pl.BlockSpec((tk, tm), lambda i, j, k: (k, i))
    else:
        a_spec = pl.BlockSpec((tm, tk), lambda i, j, k: (i, k))
    if mode == "nt":
        b_spec = pl.BlockSpec((tn, tk), lambda i, j, k: (j, k))
    else:
        b_spec = pl.BlockSpec((tk, tn), lambda i, j, k: (k, j))
    o_spec = pl.BlockSpec((tm, tn), lambda i, j, k: (i, j))
    in_specs, args = [a_spec, b_spec], [a, b]
    if resid is not None:
        in_specs.append(o_spec)
        args.append(resid)
    return pl.pallas_call(
        body, name=name, grid=(m // tm, n // tn, nk),
        in_specs=in_specs, out_specs=o_spec,
        out_shape=jax.ShapeDtypeStruct((m, n), out_dtype),
        scratch_shapes=[pltpu.VMEM((tm, tn), F32)],
        compiler_params=_cparams(("parallel", "parallel", "arbitrary")),
    )(*args)


def _rms(x):
    r = lax.rsqrt(jnp.mean(x * x, axis=-1, keepdims=True) + EPS)
    return x * r, r


def _rmsnorm_fwd(h, g, name):
    t = h.shape[0]

    def body(h_ref, g_ref, o_ref):
        xh, _ = _rms(h_ref[...])
        o_ref[...] = (xh * g_ref[...]).astype(BF16)

    return pl.pallas_call(
        body, name=name, grid=(t // TM,),
        in_specs=[pl.BlockSpec((TM, D), lambda i: (i, 0)), pl.BlockSpec((1, D), lambda i: (0, 0))],
        out_specs=pl.BlockSpec((TM, D), lambda i: (i, 0)),
        out_shape=jax.ShapeDtypeStruct((t, D), BF16),
        compiler_params=_cparams(("parallel",)),
    )(h, g)


def _rms_bwd_math(dhn, x, g):
    xh, r = _rms(x)
    dxh = dhn * g
    dx = r * (dxh - xh * jnp.mean(dxh * xh, axis=-1, keepdims=True))
    return dx, jnp.sum(dhn * xh, axis=0, keepdims=True)


def _final_loss(h2, gf, target):
    t = h2.shape[0]

    def body(h_ref, g_ref, t_ref, dh_ref, loss_ref, dg_ref):
        i = pl.program_id(0)

        @pl.when(i == 0)
        def _():
            loss_ref[...] = jnp.zeros_like(loss_ref)
            dg_ref[...] = jnp.zeros_like(dg_ref)
            dh_ref[...] = jnp.zeros_like(dh_ref)

        @pl.when(i > 0)
        def _():
            x = h_ref[...]
            g = g_ref[...]
            xh, _ = _rms(x)
            err = xh * g - t_ref[...]
            loss_ref[...] += 0.5 * jnp.sum(jnp.sum(err * err, axis=-1, keepdims=True) * (1.0 / D))
            dx, dg = _rms_bwd_math(err * (1.0 / D), x, g)
            dh_ref[...] = dx
            dg_ref[...] += dg

    return pl.pallas_call(
        body, name="final_loss", grid=(t // TM,),
        in_specs=[pl.BlockSpec((TM, D), lambda i: (i, 0)), pl.BlockSpec((1, D), lambda i: (0, 0)),
                  pl.BlockSpec((TM, D), lambda i: (jnp.maximum(i - 1, 0), 0))],
        out_specs=[pl.BlockSpec((TM, D), lambda i: (i, 0)), pl.BlockSpec((8, 128), lambda i: (0, 0)),
                   pl.BlockSpec((1, D), lambda i: (0, 0))],
        out_shape=[jax.ShapeDtypeStruct((t, D), F32), jax.ShapeDtypeStruct((8, 128), F32),
                   jax.ShapeDtypeStruct((1, D), F32)],
        compiler_params=_cparams(("arbitrary",)),
    )(h2, gf, target)


def _rmsnorm_bwd(dhn, h, g, dres, name):
    t = h.shape[0]

    def body(dhn_ref, h_ref, g_ref, dres_ref, dh_ref, dg_ref):
        @pl.when(pl.program_id(0) == 0)
        def _():
            dg_ref[...] = jnp.zeros_like(dg_ref)

        dx, dg = _rms_bwd_math(dhn_ref[...], h_ref[...], g_ref[...])
        dh_ref[...] = dres_ref[...] + dx
        dg_ref[...] += dg

    row = pl.BlockSpec((TM, D), lambda i: (i, 0))
    vec = pl.BlockSpec((1, D), lambda i: (0, 0))
    return pl.pallas_call(
        body, name=name, grid=(t // TM,),
        in_specs=[row, row, vec, row], out_specs=[row, vec],
        out_shape=[jax.ShapeDtypeStruct((t, D), F32), jax.ShapeDtypeStruct((1, D), F32)],
        compiler_params=_cparams(("arbitrary",)),
    )(dhn, h, g, dres)


def _rmsnorm1_bwd(dhn, h, g, dres):
    t = h.shape[0]

    def body(dhn_ref, h_ref, g_ref, dres_ref, dx_ref, dmeta_ref, dg_ref):
        i = pl.program_id(0)

        @pl.when(i == 0)
        def _():
            dg_ref[...] = jnp.zeros_like(dg_ref)

        dx, dg = _rms_bwd_math(dhn_ref[...], h_ref[...], g_ref[...])
        dh = dres_ref[...] + dx
        dg_ref[...] += dg
        dx_ref[...] = dh

        @pl.when(i == 0)
        def _():
            dmeta_ref[...] = dh[PAD:, :]

    row = pl.BlockSpec((TM, D), lambda i: (i, 0))
    vec = pl.BlockSpec((1, D), lambda i: (0, 0))
    return pl.pallas_call(
        body, name="rmsnorm1_bwd", grid=(t // TM,),
        in_specs=[row, row, vec, row],
        out_specs=[pl.BlockSpec((TM, D), lambda i: (jnp.maximum(i - 1, 0), 0)),
                   pl.BlockSpec((N_META, D), lambda i: (0, 0)), vec],
        out_shape=[jax.ShapeDtypeStruct((t - ROW0, D), F32), jax.ShapeDtypeStruct((N_META, D), F32),
                   jax.ShapeDtypeStruct((1, D), F32)],
        compiler_params=_cparams(("arbitrary",)),
    )(dhn, h, g, dres)


FT = DFF // 2


def _swiglu_fwd(gu):
    t = gu.shape[0]

    def body(g_ref, u_ref, o_ref):
        g = g_ref[...]
        o_ref[...] = (g * _sig(g) * u_ref[...]).astype(BF16)

    return pl.pallas_call(
        body, name="swiglu_fwd", grid=(t // TM, 2),
        in_specs=[pl.BlockSpec((TM, FT), lambda i, j: (i, j)), pl.BlockSpec((TM, FT), lambda i, j: (i, j + 2))],
        out_specs=pl.BlockSpec((TM, FT), lambda i, j: (i, j)),
        out_shape=jax.ShapeDtypeStruct((t, DFF), BF16),
        compiler_params=_cparams(("parallel", "parallel")),
    )(gu, gu)


def _swiglu_bwd(dact, gu):
    t = gu.shape[0]

    def body(d_ref, g_ref, u_ref, o_ref):
        j = pl.program_id(1)
        g, d = g_ref[...], d_ref[...]
        s = _sig(g)
        dg = d * u_ref[...] * (s * (1.0 + g * (1.0 - s)))
        du = d * (g * s)
        o_ref[...] = jnp.where(j < 2, dg, du).astype(BF16)

    return pl.pallas_call(
        body, name="swiglu_bwd", grid=(t // TM, 4),
        in_specs=[pl.BlockSpec((TM, FT), lambda i, j: (i, j % 2)), pl.BlockSpec((TM, FT), lambda i, j: (i, j % 2)),
                  pl.BlockSpec((TM, FT), lambda i, j: (i, 2 + j % 2))],
        out_specs=pl.BlockSpec((TM, FT), lambda i, j: (i, j)),
        out_shape=jax.ShapeDtypeStruct((t, 2 * DFF), BF16),
        compiler_params=_cparams(("parallel", "parallel")),
    )(dact, gu, gu)


COL_GA, COL_GB = 4, 5
COL_AX, COL_AG = (6 * D) // CT, (6 * D + RGW) // CT


def _merge_fwd(pa, pb, proj):
    t = pa.shape[0]

    def body(pa_ref, pb_ref, ga_ref, gb_ref, o_ref):
        o_ref[...] = (_sig(ga_ref[...]) * pa_ref[...] + _sig(gb_ref[...]) * pb_ref[...]).astype(BF16)

    row = pl.BlockSpec((TM, D), lambda i: (i, 0))
    return pl.pallas_call(
        body, name="merge_fwd", grid=(t // TM,),
        in_specs=[row, row, pl.BlockSpec((TM, D), lambda i: (i, COL_GA)), pl.BlockSpec((TM, D), lambda i: (i, COL_GB))],
        out_specs=row, out_shape=jax.ShapeDtypeStruct((t, D), BF16),
        compiler_params=_cparams(("parallel",)),
    )(pa, pb, proj, proj)


def _merge_bwd(dm, pa, pb, proj):
    t = pa.shape[0]

    def body(dm_ref, pa_ref, pb_ref, ga_ref, gb_ref, dpa_ref, dpb_ref, dproj_ref):
        d = dm_ref[...]
        sa, sb = _sig(ga_ref[...]), _sig(gb_ref[...])
        dpa_ref[...] = (sa * d).astype(BF16)
        dpb_ref[...] = (sb * d).astype(BF16)
        dproj_ref[:, :D] = (d * pa_ref[...] * sa * (1.0 - sa)).astype(BF16)
        dproj_ref[:, D:] = (d * pb_ref[...] * sb * (1.0 - sb)).astype(BF16)

    row = pl.BlockSpec((TM, D), lambda i: (i, 0))
    return pl.pallas_call(
        body, name="merge_bwd", grid=(t // TM,),
        in_specs=[row, row, row, pl.BlockSpec((TM, D), lambda i: (i, COL_GA)), pl.BlockSpec((TM, D), lambda i: (i, COL_GB))],
        out_specs=[row, row, pl.BlockSpec((TM, 2 * D), lambda i: (i, 2))],
        out_shape=[jax.ShapeDtypeStruct((t, D), BF16), jax.ShapeDtypeStruct((t, D), BF16),
                   jax.ShapeDtypeStruct((t, D_IN), BF16)],
        compiler_params=_cparams(("parallel",)),
    )(dm, pa, pb, proj, proj)


def _prev_halo(col0):
    return pl.BlockSpec((8, CT), lambda c, i: (jnp.maximum(i * (TM // 8) - 1, 0), col0 + c))


def _conv_fwd(proj, w, b):
    t = proj.shape[0]

    def body(x_ref, halo_ref, w_ref, b_ref, o_ref, ob_ref, buf):
        i = pl.program_id(1)
        buf[0:8, :] = jnp.where(i > 0, halo_ref[...], 0.0)
        buf[8:, :] = x_ref[...]
        wv = w_ref[...]
        y = b_ref[...] + wv[3:4, :] * x_ref[...]
        for j in range(3):
            y = y + wv[j:j + 1, :] * buf[pl.ds(5 + j, TM), :]
        o_ref[...] = y
        ob_ref[...] = y.astype(BF16)

    blk = pl.BlockSpec((TM, CT), lambda c, i: (i, c))
    return pl.pallas_call(
        body, name="conv_fwd", grid=(NCT, t // TM),
        in_specs=[pl.BlockSpec((TM, CT), lambda c, i: (i, COL_AX + c)), _prev_halo(COL_AX),
                  pl.BlockSpec((4, CT), lambda c, i: (0, c)), pl.BlockSpec((1, CT), lambda c, i: (0, c))],
        out_specs=[blk, blk],
        out_shape=[jax.ShapeDtypeStruct((t, RGW), F32), jax.ShapeDtypeStruct((t, RGW), BF16)],
        scratch_shapes=[pltpu.VMEM((TM + 8, CT), F32)],
        compiler_params=_cparams(("parallel", "parallel")),
    )(proj, proj, w, b)


def _conv_bwd(dxc, proj, w, dproj):
    t = proj.shape[0]
    nt = t // TM

    def body(d_ref, dn_ref, x_ref, halo_ref, w_ref, _, dx_ref, dw_ref, db_ref, bufd, bufx):
        i = pl.program_id(1)

        @pl.when(i == 0)
        def _():
            dw_ref[...] = jnp.zeros_like(dw_ref)
            db_ref[...] = jnp.zeros_like(db_ref)

        d = d_ref[...]
        bufd[0:TM, :] = d
        bufd[TM:, :] = jnp.where(i < nt - 1, dn_ref[...], 0.0)
        bufx[0:8, :] = jnp.where(i > 0, halo_ref[...], 0.0)
        bufx[8:, :] = x_ref[...]
        wv = w_ref[...]
        dx = wv[3:4, :] * d
        for j in range(3):
            dx = dx + wv[j:j + 1, :] * bufd[pl.ds(3 - j, TM), :]
            dw_ref[j:j + 1, :] += jnp.sum(d * bufx[pl.ds(5 + j, TM), :], axis=0, keepdims=True)
        dw_ref[3:4, :] += jnp.sum(d * x_ref[...], axis=0, keepdims=True)
        db_ref[...] += jnp.sum(d, axis=0, keepdims=True)
        dx_ref[...] = dx.astype(BF16)

    return pl.pallas_call(
        body, name="conv_bwd", grid=(NCT, nt),
        in_specs=[pl.BlockSpec((TM, CT), lambda c, i: (i, c)),
                  pl.BlockSpec((8, CT), lambda c, i: (jnp.minimum((i + 1) * (TM // 8), t // 8 - 1), c)),
                  pl.BlockSpec((TM, CT), lambda c, i: (i, COL_AX + c)), _prev_halo(COL_AX),
                  pl.BlockSpec((4, CT), lambda c, i: (0, c)), ANY],
        out_specs=[pl.BlockSpec((TM, CT), lambda c, i: (i, COL_AX + c)),
                   pl.BlockSpec((4, CT), lambda c, i: (0, c)), pl.BlockSpec((1, CT), lambda c, i: (0, c))],
        out_shape=[jax.ShapeDtypeStruct((t, D_IN), BF16), jax.ShapeDtypeStruct((4, RGW), F32),
                   jax.ShapeDtypeStruct((1, RGW), F32)],
        scratch_shapes=[pltpu.VMEM((TM + 8, CT), F32), pltpu.VMEM((TM + 8, CT), F32)],
        input_output_aliases={5: 0},
        compiler_params=_cparams(("parallel", "arbitrary")),
    )(dxc, dxc, proj, proj, w, dproj)


def _gelu(x):
    c = 0.7978845608028654
    th = jnp.tanh(c * (x + 0.044715 * x * x * x))
    return 0.5 * x * (1.0 + th), th


def _rg_gates(gr, gi, xc, ba, bx, lam, row0):
    r = _sig(gr + ba)
    ig = _sig(gi + bx)
    sp = jax.nn.softplus(-lam)
    a = jnp.exp(-RG_C * r * sp)
    s = jnp.sqrt(jnp.maximum(1.0 - a * a, 1e-30))
    rows = row0 + lax.broadcasted_iota(jnp.int32, gr.shape, 0)
    live = rows >= PAD
    u = jnp.where(live, s * ig * xc, 0.0)
    return r, ig, sp, a, s, u, live


def _rg_fwd(gr, gi, xc, proj, ba, bx, lam):
    t = gr.shape[0]

    def body(gr_ref, gi_ref, xc_ref, ag_ref, ba_ref, bx_ref, lam_ref, h_ref, ya_ref, a_s, u_s, hc):
        i = pl.program_id(1)

        @pl.when(i == 0)
        def _():
            hc[...] = jnp.zeros_like(hc)

        _, _, _, a, _, u, _ = _rg_gates(gr_ref[...], gi_ref[...], xc_ref[...], ba_ref[...], bx_ref[...],
                                        lam_ref[...], i * TM)
        a_s[...] = a
        u_s[...] = u
        sub = lax.broadcasted_iota(jnp.int32, (8, CT), 0)

        def blk(k, h):
            off = pl.multiple_of(k * 8, 8)
            av, uv = a_s[pl.ds(off, 8), :], u_s[pl.ds(off, 8), :]
            out = jnp.zeros((8, CT), F32)
            for r in range(8):
                h = av[r:r + 1, :] * h + uv[r:r + 1, :]
                out = jnp.where(sub == r, h, out)
            h_ref[pl.ds(off, 8), :] = out
            return h

        hc[...] = lax.fori_loop(0, TM // 8, blk, hc[...])
        ge, _ = _gelu(ag_ref[...])
        ya_ref[...] = (h_ref[...] * ge).astype(BF16)

    blk_ = pl.BlockSpec((TM, CT), lambda c, i: (i, c))
    vec = pl.BlockSpec((1, CT), lambda c, i: (0, c))
    return pl.pallas_call(
        body, name="rg_fwd", grid=(NCT, t // TM),
        in_specs=[blk_, blk_, blk_, pl.BlockSpec((TM, CT), lambda c, i: (i, COL_AG + c)), vec, vec, vec],
        out_specs=[blk_, blk_],
        out_shape=[jax.ShapeDtypeStruct((t, RGW), F32), jax.ShapeDtypeStruct((t, RGW), BF16)],
        scratch_shapes=[pltpu.VMEM((TM, CT), F32), pltpu.VMEM((TM, CT), F32), pltpu.VMEM((1, CT), F32)],
        compiler_params=_cparams(("parallel", "arbitrary")),
    )(gr, gi, xc, proj, ba, bx, lam)


def _rg_bwd(dya, h, gr, gi, xc, proj, ba, bx, lam, dproj):
    t = gr.shape[0]
    nt = t // TM

    def body(dya_ref, h_ref, hh_ref, gr_ref, gi_ref, xc_ref, ag_ref, ba_ref, bx_ref, lam_ref, _,
             dgr_ref, dgi_ref, dxc_ref, dag_ref, dba_ref, dbx_ref, dlam_ref, a_s, d_s, g_s, hbuf, cc):
        i = pl.program_id(1)
        ri = nt - 1 - i

        @pl.when(i == 0)
        def _():
            cc[...] = jnp.zeros_like(cc)
            dba_ref[...] = jnp.zeros_like(dba_ref)
            dbx_ref[...] = jnp.zeros_like(dbx_ref)
            dlam_ref[...] = jnp.zeros_like(dlam_ref)

        xc = xc_ref[...]
        lam = lam_ref[...]
        r, ig, sp, a, s, _, live = _rg_gates(gr_ref[...], gi_ref[...], xc, ba_ref[...], bx_ref[...], lam, ri * TM)
        ag = ag_ref[...]
        ge, th = _gelu(ag)
        dya = dya_ref[...]
        hv = h_ref[...]
        c0 = 0.7978845608028654
        dge = 0.5 * (1.0 + th) + 0.5 * ag * (1.0 - th * th) * c0 * (1.0 + 3.0 * 0.044715 * ag * ag)
        dag_ref[...] = (dya * hv * dge).astype(BF16)
        a_s[...] = a
        d_s[...] = dya * ge
        sub = lax.broadcasted_iota(jnp.int32, (8, CT), 0)

        def blk(k, c):
            off = pl.multiple_of((TM // 8 - 1 - k) * 8, 8)
            av, dv = a_s[pl.ds(off, 8), :], d_s[pl.ds(off, 8), :]
            out = jnp.zeros((8, CT), F32)
            for rr in range(7, -1, -1):
                g = dv[rr:rr + 1, :] + c
                c = av[rr:rr + 1, :] * g
                out = jnp.where(sub == rr, g, out)
            g_s[pl.ds(off, 8), :] = out
            return c

        cc[...] = lax.fori_loop(0, TM // 8, blk, cc[...])
        g = g_s[...]
        hbuf[0:8, :] = jnp.where(ri > 0, hh_ref[...], 0.0)
        hbuf[8:, :] = hv
        hprev = hbuf[pl.ds(7, TM), :]
        du = jnp.where(live, g, 0.0)
        m = ig * xc
        ds = du * m
        dm = du * s
        da = g * hprev - a * ds / s
        dla = da * a
        dr = dla * (-RG_C * sp) * r * (1.0 - r)
        di = dm * xc * ig * (1.0 - ig)
        dgr_ref[...] = dr.astype(BF16)
        dgi_ref[...] = di.astype(BF16)
        dxc_ref[...] = dm * ig
        dba_ref[...] += jnp.sum(dr, axis=0, keepdims=True)
        dbx_ref[...] += jnp.sum(di, axis=0, keepdims=True)
        dlam_ref[...] += jnp.sum(dla * (-RG_C * r), axis=0, keepdims=True) * (-_sig(-lam))

    rblk = pl.BlockSpec((TM, CT), lambda c, i: (nt - 1 - i, c))
    vec = pl.BlockSpec((1, CT), lambda c, i: (0, c))
    hh = pl.BlockSpec((8, CT), lambda c, i: (jnp.maximum((nt - 1 - i) * (TM // 8) - 1, 0), c))
    agb = pl.BlockSpec((TM, CT), lambda c, i: (nt - 1 - i, COL_AG + c))
    return pl.pallas_call(
        body, name="rg_bwd", grid=(NCT, nt),
        in_specs=[rblk, rblk, hh, rblk, rblk, rblk, agb, vec, vec, vec, ANY],
        out_specs=[rblk, rblk, rblk, agb, vec, vec, vec],
        out_shape=[jax.ShapeDtypeStruct((t, RGW), BF16), jax.ShapeDtypeStruct((t, RGW), BF16),
                   jax.ShapeDtypeStruct((t, RGW), F32), jax.ShapeDtypeStruct((t, D_IN), BF16),
                   jax.ShapeDtypeStruct((1, RGW), F32), jax.ShapeDtypeStruct((1, RGW), F32),
                   jax.ShapeDtypeStruct((1, RGW), F32)],
        scratch_shapes=[pltpu.VMEM((TM, CT), F32), pltpu.VMEM((TM, CT), F32), pltpu.VMEM((TM, CT), F32),
                        pltpu.VMEM((TM + 8, CT), F32), pltpu.VMEM((1, CT), F32)],
        input_output_aliases={10: 3},
        compiler_params=_cparams(("parallel", "arbitrary")),
    )(dya, h, h, gr, gi, xc, proj, ba, bx, lam, dproj)


NCH = TM // CH
HI = lax.Precision.HIGHEST


def _hg_chunk(qr, fr, lb):
    sf = _sig(fr)
    fg = lb + (1.0 - lb) * sf
    k = (1.0 - lb) * _sig(-fr)
    sq = _sig(qr)
    q = qr * sq
    ri = lax.broadcasted_iota(jnp.int32, (CH, CH), 0)
    ci = lax.broadcasted_iota(jnp.int32, (CH, CH), 1)
    b = _dot((ri >= ci).astype(F32), jnp.log(fg), NN, HI)
    bm, bl = b[CH // 2 - 1:CH // 2, :], b[CH - 1:CH, :]
    eb = jnp.exp(b)
    ebm = jnp.exp(b - bm)
    ekm = jnp.exp(jnp.minimum(bm - b, EXP_CLAMP))
    ekl = jnp.exp(bl - b)
    return dict(sf=sf, fg=fg, k=k, sq=sq, q=q, eb=eb, ebm=ebm, ekm=ekm, ekl=ekl, ebl=jnp.exp(bl),
                qe=q * eb, qh=q * ebm, kh=k * ekm, kd=k * ekl, causal=ri >= ci, upper=(ci >= ri).astype(F32))


def _hgrn_fwd(proj, lbl, gn):
    t = proj.shape[0]
    nt = t // TM

    def body(q_ref, f_ref, v_ref, g_ref, lbl_ref, gn_ref, yb_ref, o_ref, st_ref, st):
        @pl.when(pl.program_id(0) == 0)
        def _():
            st[...] = jnp.zeros_like(st)

        l = lbl_ref[...]
        lb = _sig(l[0:1, :] - l[1:2, :])
        gnv = gn_ref[...]

        def chunk(c, carry):
            off = pl.multiple_of(c * CH, CH)
            rows = pl.ds(off, CH)
            z = _hg_chunk(q_ref[rows, :], f_ref[rows, :], lb)
            v, gg = v_ref[rows, :], g_ref[rows, :]
            for hh in range(HEADS):
                sl = slice(hh * HD, (hh + 1) * HD)
                s_prev = st[hh]
                st_ref[c, hh] = s_prev
                vb = v[:, sl].astype(BF16)
                att = jnp.where(z["causal"], _dot(z["qh"][:, sl].astype(BF16), z["kh"][:, sl].astype(BF16), NT), 0.0)
                o = _dot(z["qe"][:, sl].astype(BF16), s_prev.astype(BF16), NT) + _dot(att.astype(BF16), vb, NN)
                st[hh] = s_prev * z["ebl"][:, sl] + _dot(vb, z["kd"][:, sl].astype(BF16), TN_)
                xh, _ = _rms(o)
                gh = gg[:, sl]
                o_ref[rows, sl] = o
                yb_ref[rows, sl] = (xh * gnv * gh * _sig(gh)).astype(BF16)
            return carry

        lax.fori_loop(0, NCH, chunk, 0)

    def col(j):
        return pl.BlockSpec((TM, HGW), lambda i, j=j: (i, j))

    return pl.pallas_call(
        body, name="hgrn_fwd", grid=(nt,),
        in_specs=[col(0), col(1), col(2), col(3), pl.BlockSpec((2, HGW), lambda i: (0, 0)),
                  pl.BlockSpec((1, HD), lambda i: (0, 0))],
        out_specs=[col(0), col(0), pl.BlockSpec((NCH, HEADS, HD, HD), lambda i: (i, 0, 0, 0))],
        out_shape=[jax.ShapeDtypeStruct((t, HGW), BF16), jax.ShapeDtypeStruct((t, HGW), F32),
                   jax.ShapeDtypeStruct((t // CH, HEADS, HD, HD), F32)],
        scratch_shapes=[pltpu.VMEM((HEADS, HD, HD), F32)],
        compiler_params=_cparams(("arbitrary",)),
    )(proj, proj, proj, proj, lbl, gn)


def _hgrn_bwd(dyb, proj, o, states, lbl, gn, dproj):
    t = proj.shape[0]
    nt = t // TM

    def body(dy_ref, q_ref, f_ref, v_ref, g_ref, o_ref, st_ref, lbl_ref, gn_ref, _,
             dp_ref, dgn_ref, dl_ref, dst, dlb):
        i = pl.program_id(0)

        @pl.when(i == 0)
        def _():
            dst[...] = jnp.zeros_like(dst)
            dlb[...] = jnp.zeros_like(dlb)
            dgn_ref[...] = jnp.zeros_like(dgn_ref)

        l = lbl_ref[...]
        lb = _sig(l[0:1, :] - l[1:2, :])
        gnv = gn_ref[...]
        last = lax.broadcasted_iota(jnp.int32, (CH, HD), 0) == CH - 1

        def chunk(cc, carry):
            c = NCH - 1 - cc
            off = pl.multiple_of(c * CH, CH)
            rows = pl.ds(off, CH)
            qr, fr = q_ref[rows, :], f_ref[rows, :]
            z = _hg_chunk(qr, fr, lb)
            v, gg, ov, dy = v_ref[rows, :], g_ref[rows, :], o_ref[rows, :], dy_ref[rows, :]
            dqs, dks, dbs, dvs, dgs = [], [], [], [], []
            dgn = jnp.zeros((1, HD), F32)
            for hh in range(HEADS):
                sl = slice(hh * HD, (hh + 1) * HD)
                s_prev, ds_new = st_ref[c, hh], dst[hh]
                qe, qh, kh, kd = z["qe"][:, sl], z["qh"][:, sl], z["kh"][:, sl], z["kd"][:, sl]
                ebl = z["ebl"][:, sl]
                gh, dyh = gg[:, sl], dy[:, sl]
                xh, rr = _rms(ov[:, sl])
                sg = _sig(gh)
                dyn = dyh * (gh * sg)
                dgs.append(dyh * (xh * gnv) * (sg * (1.0 + gh * (1.0 - sg))))
                dgn = dgn + jnp.sum(dyn * xh, axis=0, keepdims=True)
                dxh = dyn * gnv
                do = (rr * (dxh - xh * jnp.mean(dxh * xh, axis=-1, keepdims=True))).astype(BF16)
                vb, dsb = v[:, sl].astype(BF16), ds_new.astype(BF16)
                qeb, qhb, khb, kdb = (a.astype(BF16) for a in (qe, qh, kh, kd))
                att = jnp.where(z["causal"], _dot(qhb, khb, NT), 0.0).astype(BF16)
                datt = jnp.where(z["causal"], _dot(do, vb, NT), 0.0).astype(BF16)
                dvs.append(_dot(att, do, TN_) + _dot(kdb, dsb, NT))
                dqe = _dot(do, s_prev.astype(BF16), NN)
                dqh = _dot(datt, khb, NN)
                dkh = _dot(datt, qhb, TN_)
                dkd = _dot(vb, dsb, NN)
                qe, qh, kh, kd = (a.astype(F32) for a in (qeb, qhb, khb, kdb))
                dbl = (jnp.sum(dkd * kd, axis=0, keepdims=True)
                       + jnp.sum(ds_new * s_prev, axis=0, keepdims=True) * ebl)
                dqs.append(dqe * z["eb"][:, sl] + dqh * z["ebm"][:, sl])
                dks.append(dkh * z["ekm"][:, sl] + dkd * z["ekl"][:, sl])
                dbs.append(dqe * qe + dqh * qh - dkh * kh - dkd * kd + jnp.where(last, dbl, 0.0))
                dst[hh] = _dot(do, qeb, TN_) + ds_new * ebl
            dgn_ref[...] += dgn
            dq, dk, db = (jnp.concatenate(x, axis=1) for x in (dqs, dks, dbs))
            dlf = _dot(z["upper"], db, NN, HI)
            sf, fg, sq = z["sf"], z["fg"], z["sq"]
            dsf = (dlf / fg - dk) * (1.0 - lb)
            dlb[...] += jnp.sum(dlf * (1.0 - sf) / fg - dk * _sig(-fr), axis=0, keepdims=True)
            dp_ref[rows, 0:HGW] = (dq * (sq * (1.0 + qr * (1.0 - sq)))).astype(BF16)
            dp_ref[rows, HGW:2 * HGW] = (dsf * sf * (1.0 - sf)).astype(BF16)
            dp_ref[rows, 2 * HGW:3 * HGW] = jnp.concatenate(dvs, axis=1).astype(BF16)
            dp_ref[rows, 3 * HGW:4 * HGW] = jnp.concatenate(dgs, axis=1).astype(BF16)
            return carry

        lax.fori_loop(0, NCH, chunk, 0)
        dl0 = dlb[...] * lb * (1.0 - lb)
        dl_ref[0:1, :] = dl0
        dl_ref[1:2, :] = -dl0

    def col(j):
        return pl.BlockSpec((TM, HGW), lambda i, j=j: (nt - 1 - i, j))

    return pl.pallas_call(
        body, name="hgrn_bwd", grid=(nt,),
        in_specs=[col(0), col(0), col(1), col(2), col(3), col(0),
                  pl.BlockSpec((NCH, HEADS, HD, HD), lambda i: (nt - 1 - i, 0, 0, 0)),
                  pl.BlockSpec((2, HGW), lambda i: (0, 0)), pl.BlockSpec((1, HD), lambda i: (0, 0)), ANY],
        out_specs=[pl.BlockSpec((TM, 4 * HGW), lambda i: (nt - 1 - i, 0)),
                   pl.BlockSpec((1, HD), lambda i: (0, 0)), pl.BlockSpec((2, HGW), lambda i: (0, 0))],
        out_shape=[jax.ShapeDtypeStruct((t, D_IN), BF16), jax.ShapeDtypeStruct((1, HD), F32),
                   jax.ShapeDtypeStruct((2, HGW), F32)],
        scratch_shapes=[pltpu.VMEM((HEADS, HD, HD), F32), pltpu.VMEM((1, HGW), F32)],
        input_output_aliases={9: 0},
        compiler_params=_cparams(("arbitrary",)),
    )(dyb, proj, proj, proj, proj, o, states, lbl, gn, dproj)


def _block_diag(w):
    eye = jnp.eye(RG_BLOCKS, dtype=w.dtype)
    return (w[:, :, None, :] * eye[:, None, :, None]).reshape(RGW, RGW)


def _diag_blocks(wd):
    w4 = wd.reshape(RG_BLOCKS, RG_BLOCK, RG_BLOCKS, RG_BLOCK)
    return jnp.stack([w4[n, :, n, :] for n in range(RG_BLOCKS)])


def _local_step(h0, target, wts, small):
    w_in, w_pa, w_pb, w_out, w_fi, w_fd = (wts[k] for k in ("w_in", "w_proj_a", "w_proj_b", "w_out", "w_ffn_in", "w_ffn_down"))
    wa_d = _block_diag(small["rg_wa"]).astype(BF16)
    wx_d = _block_diag(small["rg_wx"]).astype(BF16)
    ba, bx, lam = small["rg_ba"], small["rg_bx"], small["rg_lambda"]
    lbl, gn = small["hg_lb_logits"], small["hg_norm_g"]
    conv_w, conv_b = small["conv_w"], small["conv_b"]

    hn1 = _rmsnorm_fwd(h0, small["norm1_g"], "rmsnorm1")
    proj = _mm(hn1, w_in, "nn", F32, "mm_proj")
    xc, xcb = _conv_fwd(proj, conv_w, conv_b)
    gr = _mm(xcb, wa_d, "nn", F32, "mm_rg_a")
    gi = _mm(xcb, wx_d, "nn", F32, "mm_rg_x")
    hrg, ya = _rg_fwd(gr, gi, xc, proj, ba, bx, lam)
    yb, o, states = _hgrn_fwd(proj, lbl, gn)
    pa = _mm(ya, w_pa, "nn", F32, "mm_pa")
    pb = _mm(yb, w_pb, "nn", F32, "mm_pb")
    merged = _merge_fwd(pa, pb, proj)
    h1 = _mm(merged, w_out, "nn", F32, "mm_out", resid=h0)
    hn2 = _rmsnorm_fwd(h1, small["norm2_g"], "rmsnorm2")
    gu = _mm(hn2, w_fi, "nn", F32, "mm_ffn_in")
    act = _swiglu_fwd(gu)
    h2 = _mm(act, w_fd, "nn", F32, "mm_ffn_down", resid=h1)
    dh2, loss, d_normf = _final_loss(h2, small["norm_f_g"], target)

    g = {}
    dh2b = dh2.astype(BF16)
    g["w_ffn_down"] = _mm(act, dh2b, "tn", BF16, "mm_d_wfd")
    dact = _mm(dh2b, w_fd, "nt", F32, "mm_d_act")
    dgu = _swiglu_bwd(dact, gu)
    g["w_ffn_in"] = _mm(hn2, dgu, "tn", BF16, "mm_d_wfi")
    dhn2 = _mm(dgu, w_fi, "nt", F32, "mm_d_hn2")
    dh1, g["norm2_g"] = _rmsnorm_bwd(dhn2, h1, small["norm2_g"], dh2, "rmsnorm2_bwd")
    dh1b = dh1.astype(BF16)
    g["w_out"] = _mm(merged, dh1b, "tn", BF16, "mm_d_wout")
    dmerged = _mm(dh1b, w_out, "nt", F32, "mm_d_merged")
    dpa, dpb, dproj = _merge_bwd(dmerged, pa, pb, proj)
    g["w_proj_a"] = _mm(ya, dpa, "tn", BF16, "mm_d_wpa")
    g["w_proj_b"] = _mm(yb, dpb, "tn", BF16, "mm_d_wpb")
    dya = _mm(dpa, w_pa, "nt", F32, "mm_d_ya")
    dyb = _mm(dpb, w_pb, "nt", F32, "mm_d_yb")
    dproj, g["hg_norm_g"], g["hg_lb_logits"] = _hgrn_bwd(dyb, proj, o, states, lbl, gn, dproj)
    dgr, dgi, dxc, dproj, g["rg_ba"], g["rg_bx"], g["rg_lambda"] = _rg_bwd(dya, hrg, gr, gi, xc, proj, ba, bx, lam, dproj)
    g["rg_wa"] = _diag_blocks(_mm(xcb, dgr, "tn", F32, "mm_d_wa"))
    g["rg_wx"] = _diag_blocks(_mm(xcb, dgi, "tn", F32, "mm_d_wx"))
    dxc = _mm(dgr, wa_d, "nt", F32, "mm_d_xc_a", resid=dxc)
    dxc = _mm(dgi, wx_d, "nt", F32, "mm_d_xc_x", resid=dxc)
    dproj, g["conv_w"], g["conv_b"] = _conv_bwd(dxc, proj, conv_w, dproj)
    g["w_in"] = _mm(hn1, dproj, "tn", BF16, "mm_d_win")
    dhn1 = _mm(dproj, w_in, "nt", F32, "mm_d_hn1")
    grad_x, g["meta_tokens"], g["norm1_g"] = _rmsnorm1_bwd(dhn1, h0, small["norm1_g"], dh1)
    g["norm_f_g"] = d_normf
    return loss, grad_x, g


def _adamw_math(w, g, m, v):
    m = B1 * m + (1.0 - B1) * g
    v = B2 * v + (1.0 - B2) * (g * g)
    m_hat = m / (1.0 - B1 ** STEP)
    v_hat = v / (1.0 - B2 ** STEP)
    return -LR * (m_hat / (jnp.sqrt(v_hat) + ADAM_EPS) + WD * w), m, v


def _adamw(w, g, m, v, name):
    r, c = w.shape
    tr = _pick(r, (256, 352, 320, 128, 64, 32, 16, 8))

    def body(w_ref, g_ref, m_ref, v_ref, d_ref, mo_ref, vo_ref):
        d_ref[...], mo_ref[...], vo_ref[...] = _adamw_math(w_ref[...], g_ref[...], m_ref[...], v_ref[...])

    blk = pl.BlockSpec((tr, c), lambda i: (i, 0))
    return pl.pallas_call(
        body, name=name, grid=(r // tr,), in_specs=[blk] * 4, out_specs=[blk] * 3,
        out_shape=[jax.ShapeDtypeStruct((r, c), F32)] * 3,
        compiler_params=_cparams(("parallel",)),
    )(w, g, m, v)


def _adamw_small(recv, w, m, v):
    rows = w.shape[0]
    tr = _pick(rows, (376, 256, 128, 64, 32, 16, 8))

    def body(r_ref, w_ref, m_ref, v_ref, g_ref, d_ref, mo_ref, vo_ref):
        g = r_ref[0]
        for p in range(1, 8):
            g = g + r_ref[p]
        g_ref[...] = g
        d_ref[...], mo_ref[...], vo_ref[...] = _adamw_math(w_ref[...], g, m_ref[...], v_ref[...])

    blk = pl.BlockSpec((tr, 128), lambda i: (i, 0))
    return pl.pallas_call(
        body, name="adamw_small", grid=(rows // tr,),
        in_specs=[pl.BlockSpec((8, tr, 128), lambda i: (0, i, 0)), blk, blk, blk], out_specs=[blk] * 4,
        out_shape=[jax.ShapeDtypeStruct((rows, 128), F32)] * 4,
        compiler_params=_cparams(("parallel",)),
    )(recv, w, m, v)


def _sum8(recv, name):
    _, r, c = recv.shape
    tr = _pick(r, (128, 176, 160, 64, 32, 16))

    def body(r_ref, o_ref):
        g = r_ref[0].astype(F32)
        for p in range(1, 8):
            g = g + r_ref[p].astype(F32)
        o_ref[...] = g

    return pl.pallas_call(
        body, name=name, grid=(r // tr,),
        in_specs=[pl.BlockSpec((8, tr, c), lambda i: (0, i, 0))], out_specs=pl.BlockSpec((tr, c), lambda i: (i, 0)),
        out_shape=jax.ShapeDtypeStruct((r, c), F32),
        compiler_params=_cparams(("parallel",)),
    )(recv)


BIG = ("w_in", "w_proj_a", "w_proj_b", "w_out", "w_ffn_in", "w_ffn_down")
COL_SHARDED = ("w_in", "w_ffn_in")


def _coords():
    return lax.axis_index("x"), lax.axis_index("y"), lax.axis_index("c")


def _allgather_shards(shards):
    n = len(shards)

    def body(*refs):
        ins, outs = refs[:n], refs[n:2 * n]
        send, recv, local = refs[2 * n:]
        x, y, c = _coords()
        chips = [(1 - x, y), (x, 1 - y), (1 - x, 1 - y)]
        me = 2 * x + y
        locals_, sends, fwds = [], [], []
        for w in range(n):
            half = ins[w].shape[0] // 2
            rows = pl.ds(pl.multiple_of(c * half, 8), half)
            cp = pltpu.make_async_copy(ins[w], outs[w].at[me], local.at[w])
            cp.start()
            locals_.append(cp)
            for k, (cx, cy) in enumerate(chips):
                cp = pltpu.make_async_remote_copy(
                    src_ref=ins[w].at[rows], dst_ref=outs[w].at[me, rows],
                    send_sem=send.at[w, k], recv_sem=recv.at[w, k], device_id=(cx, cy, c), device_id_type=MESH)
                cp.start()
                sends.append(cp)
        for w in range(n):
            half = ins[w].shape[0] // 2
            rows = pl.ds(pl.multiple_of(c * half, 8), half)
            for k, (cx, cy) in enumerate(chips):
                slab = outs[w].at[2 * cx + cy, rows]
                pltpu.make_async_remote_copy(
                    src_ref=slab, dst_ref=slab, send_sem=send.at[w, k], recv_sem=recv.at[w, k],
                    device_id=(cx, cy, c), device_id_type=MESH).wait_recv()
                cp = pltpu.make_async_remote_copy(
                    src_ref=slab, dst_ref=slab, send_sem=send.at[w, 3 + k], recv_sem=recv.at[w, 3 + k],
                    device_id=(x, y, 1 - c), device_id_type=MESH)
                cp.start()
                fwds.append(cp)
        for w in range(n):
            half = ins[w].shape[0] // 2
            orows = pl.ds(pl.multiple_of((1 - c) * half, 8), half)
            for k, (cx, cy) in enumerate(chips):
                slab = outs[w].at[2 * cx + cy, orows]
                pltpu.make_async_remote_copy(
                    src_ref=slab, dst_ref=slab, send_sem=send.at[w, 3 + k], recv_sem=recv.at[w, 3 + k],
                    device_id=(x, y, 1 - c), device_id_type=MESH).wait_recv()
        for cp in sends + fwds:
            cp.wait_send()
        for cp in locals_:
            cp.wait()

    return pl.pallas_call(
        body, name="allgather_weights",
        in_specs=[ANY] * n, out_specs=[ANY] * n,
        out_shape=[jax.ShapeDtypeStruct((4,) + s.shape, s.dtype) for s in shards],
        scratch_shapes=[pltpu.SemaphoreType.DMA((n, 6)), pltpu.SemaphoreType.DMA((n, 6)), pltpu.SemaphoreType.DMA((n,))],
    )(*shards)


def _peer(x, y, c, k):
    fx, fy, fc = (k >> 2) & 1, (k >> 1) & 1, k & 1
    return (x ^ fx if fx else x, y ^ fy if fy else y, c ^ fc if fc else c)


def _exchange_grads(grads, pack):
    names = list(BIG)
    n = len(names) + 1
    srcs = [grads[k] for k in names] + [pack]

    def sub_shape(name, full):
        r, c = full.shape
        return (r // 2, c // 4) if name in COL_SHARDED else (r // 8, c)

    shapes = [sub_shape(k, grads[k]) for k in names] + [pack.shape]

    def body(*refs):
        ins, outs = refs[:n], refs[n:2 * n]
        send, recv, local = refs[2 * n:]
        x, y, c = _coords()
        me = 4 * x + 2 * y + c

        def region(w, px, py, pc):
            if w == n - 1:
                return ins[w]
            sr, sc = shapes[w]
            if names[w] in COL_SHARDED:
                return ins[w].at[pl.ds(pl.multiple_of(pc * sr, 16), sr), pl.ds(pl.multiple_of((2 * px + py) * sc, 128), sc)]
            return ins[w].at[pl.ds(pl.multiple_of((4 * px + 2 * py + pc) * sr, 16), sr)]

        copies, locals_ = [], []
        for w in range(n):
            cp = pltpu.make_async_copy(region(w, x, y, c), outs[w].at[me], local.at[w])
            cp.start()
            locals_.append(cp)
            for k in range(1, 8):
                px, py, pc = _peer(x, y, c, k)
                cp = pltpu.make_async_remote_copy(
                    src_ref=region(w, px, py, pc), dst_ref=outs[w].at[me],
                    send_sem=send.at[w, k - 1], recv_sem=recv.at[w, k - 1], device_id=(px, py, pc), device_id_type=MESH)
                cp.start()
                copies.append(cp)
        for w in range(n):
            for k in range(1, 8):
                px, py, pc = _peer(x, y, c, k)
                slot = outs[w].at[4 * px + 2 * py + pc]
                pltpu.make_async_remote_copy(
                    src_ref=slot, dst_ref=slot, send_sem=send.at[w, k - 1], recv_sem=recv.at[w, k - 1],
                    device_id=(px, py, pc), device_id_type=MESH).wait_recv()
        for cp in copies:
            cp.wait_send()
        for cp in locals_:
            cp.wait()

    return pl.pallas_call(
        body, name="exchange_grads",
        in_specs=[ANY] * n, out_specs=[ANY] * n,
        out_shape=[jax.ShapeDtypeStruct((8,) + s, a.dtype) for s, a in zip(shapes, srcs)],
        scratch_shapes=[pltpu.SemaphoreType.DMA((n, 7)), pltpu.SemaphoreType.DMA((n, 7)), pltpu.SemaphoreType.DMA((n,))],
    )(*srcs)


def _sibling_merge(halves):
    n = len(halves)

    def body(*refs):
        ins, outs = refs[:n], refs[n:2 * n]
        send, recv, local = refs[2 * n:]
        x, y, c = _coords()
        copies, locals_ = [], []
        for w in range(n):
            half = ins[w].shape[0]
            rows = pl.ds(pl.multiple_of(c * half, 8), half)
            cp = pltpu.make_async_copy(ins[w], outs[w].at[rows], local.at[w])
            cp.start()
            locals_.append(cp)
            cp = pltpu.make_async_remote_copy(
                src_ref=ins[w], dst_ref=outs[w].at[rows], send_sem=send.at[w], recv_sem=recv.at[w],
                device_id=(x, y, 1 - c), device_id_type=MESH)
            cp.start()
            copies.append(cp)
        for w in range(n):
            half = ins[w].shape[0]
            orows = outs[w].at[pl.ds(pl.multiple_of((1 - c) * half, 8), half)]
            pltpu.make_async_remote_copy(
                src_ref=orows, dst_ref=orows, send_sem=send.at[w], recv_sem=recv.at[w],
                device_id=(x, y, 1 - c), device_id_type=MESH).wait_recv()
        for cp in copies:
            cp.wait_send()
        for cp in locals_:
            cp.wait()

    return pl.pallas_call(
        body, name="sibling_merge",
        in_specs=[ANY] * n, out_specs=[ANY] * n,
        out_shape=[jax.ShapeDtypeStruct((2 * h.shape[0], h.shape[1]), h.dtype) for h in halves],
        scratch_shapes=[pltpu.SemaphoreType.DMA((n,)), pltpu.SemaphoreType.DMA((n,)), pltpu.SemaphoreType.DMA((n,))],
    )(*halves)


SMALL = ("norm1_g", "conv_b", "rg_wa", "rg_ba", "rg_wx", "rg_bx", "rg_lambda", "hg_lb_logits", "hg_norm_g",
         "norm2_g", "norm_f_g")
SHARDED_SMALL = ("meta_tokens", "conv_w")
SEG = 1024


def _pack(vals):
    parts = []
    for k in SMALL + SHARDED_SMALL:
        v = vals[k].reshape(-1).astype(F32)
        parts.append(jnp.pad(v, (0, -v.shape[0] % SEG)))
    return jnp.concatenate(parts).reshape(-1, 128)


def _unpack(flat, shapes):
    out, off = {}, 0
    flat = flat.reshape(-1)
    for k in SMALL + SHARDED_SMALL:
        size = 1
        for s in shapes[k]:
            size *= s
        out[k] = flat[off:off + size].reshape(shapes[k])
        off += size + (-size % SEG)
    return out


ORDER = ("meta_tokens", "norm1_g", "w_in", "conv_w", "conv_b", "rg_wa", "rg_ba", "rg_wx", "rg_bx", "rg_lambda",
         "hg_lb_logits", "hg_norm_g", "w_proj_a", "w_proj_b", "w_out", "norm2_g", "w_ffn_in", "w_ffn_down", "norm_f_g")


def kernel(x, meta_tokens, norm1_g, w_in, conv_w, conv_b, rg_wa, rg_ba, rg_wx, rg_bx, rg_lambda, hg_lb_logits, hg_norm_g, w_proj_a, w_proj_b, w_out, norm2_g, w_ffn_in, w_ffn_down, norm_f_g, loss_target, m_meta_tokens, m_norm1_g, m_w_in, m_conv_w, m_conv_b, m_rg_wa, m_rg_ba, m_rg_wx, m_rg_bx, m_rg_lambda, m_hg_lb_logits, m_hg_norm_g, m_w_proj_a, m_w_proj_b, m_w_out, m_norm2_g, m_w_ffn_in, m_w_ffn_down, m_norm_f_g, v_meta_tokens, v_norm1_g, v_w_in, v_conv_w, v_conv_b, v_rg_wa, v_rg_ba, v_rg_wx, v_rg_bx, v_rg_lambda, v_hg_lb_logits, v_hg_norm_g, v_w_proj_a, v_w_proj_b, v_w_out, v_norm2_g, v_w_ffn_in, v_w_ffn_down, v_norm_f_g):
    args = dict(locals())
    w = {k: args[k] for k in ORDER}
    m = {k: args["m_" + k] for k in ORDER}
    v = {k: args["v_" + k] for k in ORDER}
    xi, yi, _ = _coords()
    chip = 2 * xi + yi

    shard2d = {k: w[k][0] for k in BIG}
    tiny = jnp.zeros((32, 384), F32)
    tiny = tiny.at[0:N_META, 0:256].set(meta_tokens).at[N_META:N_META + 4, 0:320].set(conv_w[0])
    gathered = _allgather_shards([shard2d[k].astype(BF16) for k in BIG] + [tiny])
    full = {}
    for k, gth in zip(BIG, gathered[:-1]):
        if k in COL_SHARDED:
            full[k] = jnp.transpose(gth, (1, 0, 2)).reshape(gth.shape[1], -1)
        else:
            full[k] = gth.reshape(-1, gth.shape[2])
    full["w_in"] = jnp.concatenate([full["w_in"][:, ROT:], full["w_in"][:, :ROT]], axis=1)
    tiny_all = gathered[-1]
    meta_full = jnp.transpose(tiny_all[:, 0:N_META, 0:256], (1, 0, 2)).reshape(N_META, D)
    conv_w_full = jnp.transpose(tiny_all[:, N_META:N_META + 4, 0:320], (1, 0, 2)).reshape(4, RGW)

    small = dict(norm1_g=norm1_g, conv_w=conv_w_full, conv_b=conv_b, rg_wa=rg_wa[0], rg_ba=rg_ba, rg_wx=rg_wx[0],
                 rg_bx=rg_bx, rg_lambda=rg_lambda, hg_lb_logits=hg_lb_logits, hg_norm_g=hg_norm_g,
                 norm2_g=norm2_g, norm_f_g=norm_f_g.reshape(1, D))

    h0 = jnp.concatenate([jnp.zeros((PAD, D), F32), meta_full, x[0]], axis=0)
    loss_blk, grad_x, g = _local_step(h0, loss_target[0], full, small)
    loss = lax.psum(loss_blk[0, 0], ("x", "y", "c"))

    g["w_in"] = jnp.concatenate([g["w_in"][:, D_IN - ROT:], g["w_in"][:, :D_IN - ROT]], axis=1)
    small_shapes = {k: w[k].shape for k in SMALL}
    small_shapes["meta_tokens"], small_shapes["conv_w"] = (N_META, D), (1, 4, RGW)
    pack = _pack(g)
    recv = _exchange_grads(g, pack)
    halves = [_sum8(r, "sum8_" + k) for k, r in zip(BIG, recv[:-1])]
    g_big = dict(zip(BIG, _sibling_merge(halves)))

    w_sm = {k: w[k] for k in SMALL}
    m_sm = {k: m[k] for k in SMALL}
    v_sm = {k: v[k] for k in SMALL}
    for d in (w_sm, m_sm, v_sm):
        d["meta_tokens"], d["conv_w"] = jnp.zeros((N_META, D), F32), jnp.zeros((1, 4, RGW), F32)
    outs_sm = [_unpack(o, small_shapes) for o in _adamw_small(recv[-1], _pack(w_sm), _pack(m_sm), _pack(v_sm))]
    grad, delta, new_m, new_v = ({k: o[k] for k in SMALL} for o in outs_sm)
    g_small_full = outs_sm[0]
    g_shard = {"meta_tokens": lax.dynamic_slice(g_small_full["meta_tokens"], (0, chip * 256), (N_META, 256)),
               "conv_w": lax.dynamic_slice(g_small_full["conv_w"], (0, 0, chip * 320), (1, 4, 320))[0]}
    for k in BIG + SHARDED_SMALL:
        gk = g_big[k] if k in BIG else g_shard[k]
        wk, mk, vk = (d[k].reshape(gk.shape) for d in (w, m, v))
        dk, mk, vk = _adamw(wk, gk, mk, vk, "adamw_" + k)
        grad[k], delta[k], new_m[k], new_v[k] = (a.reshape(w[k].shape) for a in (gk, dk, mk, vk))

    return (loss, grad_x[None], *[grad[k] for k in ORDER], *[delta[k] for k in ORDER],
            *[new_m[k] for k in ORDER], *[new_v[k] for k in ORDER])
```

```python
import functools

import jax
import jax.numpy as jnp
from jax import lax
from jax.experimental import pallas as pl
from jax.experimental.pallas import tpu as pltpu

F32, BF16 = jnp.float32, jnp.bfloat16
D = 1024
N_META = 16
RGW = 1280
RG_BLOCKS, RG_BLOCK = 16, 80
RG_C = 8.0
HEADS, HD = 8, 128
HGW = HEADS * HD
DFF = 2816
D_IN = 2 * RGW + 6 * D
ROT = 2 * RGW
COL_PAD = 256
D_INP = D_IN + COL_PAD
EPS = 1e-6
CH = 64
TM = 256
ROW0 = TM
PAD = ROW0 - N_META
CT = RGW
NCT = RGW // CT
EXP_CLAMP = 80.0
VMEM_LIMIT = 56 * 1024 * 1024

LR, B1, B2, ADAM_EPS, WD, STEP = 0.001, 0.9, 0.999, 1e-08, 0.01, 10
MESH = pl.DeviceIdType.MESH
ANY = pl.BlockSpec(memory_space=pl.ANY)


def _cparams(sem):
    return pltpu.CompilerParams(dimension_semantics=sem, vmem_limit_bytes=VMEM_LIMIT)


def _pick(n, prefs):
    for p in prefs:
        if n % p == 0:
            return p
    return n


def _sig(x):
    return jax.nn.sigmoid(x)


def _dot(a, b, dims, precision=None):
    return lax.dot_general(a, b, (dims, ((), ())), preferred_element_type=F32, precision=precision)


NN, NT, TN_ = ((1,), (0,)), ((1,), (1,)), ((0,), (0,))


def _mm(a, b, mode, out_dtype, name, resid=None):
    if mode == "nn":
        (m, kd), n = a.shape, b.shape[1]
    elif mode == "nt":
        (m, kd), n = a.shape, b.shape[0]
    else:
        (kd, m), n = a.shape, b.shape[1]
    if mode == "tn":
        tm = _pick(m, (1024, 1280, 1408, 640, 512, 256, 128))
        tk = _pick(kd, (1408, 768, 512, 256, 128))
    else:
        tm = _pick(m, (768, 512, 640, 256, 128))
        tk = kd if kd <= 2816 else _pick(kd, (1792, 1408, 1024, 512, 256, 128))
    tn = _pick(n, (1792, 1408, 1280, 1024, 512, 256, 128))
    nk = kd // tk
    dims = {"nn": NN, "nt": NT, "tn": TN_}[mode]

    def finish(r, r_ref, o_ref):
        if resid is not None:
            r = r + r_ref[...]
        o_ref[...] = r.astype(out_dtype)

    def body(*refs):
        a_ref, b_ref = refs[:2]
        r_ref = refs[2] if resid is not None else None
        o_ref = refs[3] if resid is not None else refs[2]
        if nk == 1:
            finish(_dot(a_ref[...], b_ref[...], dims), r_ref, o_ref)
            return
        acc = refs[-1]
        k = pl.program_id(2)

        @pl.when(k == 0)
        def _():
            acc[...] = jnp.zeros_like(acc)

        acc[...] += _dot(a_ref[...], b_ref[...], dims)

        @pl.when(k == nk - 1)
        def _():
            finish(acc[...], r_ref, o_ref)

    if mode == "tn":
        a_spec = pl.BlockSpec((tk, tm), lambda i, j, k: (k, i))
    else:
        a_spec = pl.BlockSpec((tm, tk), lambda i, j, k: (i, k))
    if mode == "nt":
        b_spec = pl.BlockSpec((tn, tk), lambda i, j, k: (j, k))
    else:
        b_spec = pl.BlockSpec((tk, tn), lambda i, j, k: (k, j))
    o_spec = pl.BlockSpec((tm, tn), lambda i, j, k: (i, j))
    in_specs, args = [a_spec, b_spec], [a, b]
    if resid is not None:
        in_specs.append(o_spec)
        args.append(resid)
    return pl.pallas_call(
        body, name=name, grid=(m // tm, n // tn, nk),
        in_specs=in_specs, out_specs=o_spec,
        out_shape=jax.ShapeDtypeStruct((m, n), out_dtype),
        scratch_shapes=[pltpu.VMEM((tm, tn), F32)] if nk > 1 else [],
        compiler_params=_cparams(("parallel", "parallel", "arbitrary")),
    )(*args)


def _rms(x):
    r = lax.rsqrt(jnp.mean(x * x, axis=-1, keepdims=True) + EPS)
    return x * r, r


def _rmsnorm_fwd(h, g, name):
    t = h.shape[0]

    def body(h_ref, g_ref, o_ref):
        xh, _ = _rms(h_ref[...])
        o_ref[...] = (xh * g_ref[...]).astype(BF16)

    return pl.pallas_call(
        body, name=name, grid=(t // TM,),
        in_specs=[pl.BlockSpec((TM, D), lambda i: (i, 0)), pl.BlockSpec((1, D), lambda i: (0, 0))],
        out_specs=pl.BlockSpec((TM, D), lambda i: (i, 0)),
        out_shape=jax.ShapeDtypeStruct((t, D), BF16),
        compiler_params=_cparams(("parallel",)),
    )(h, g)


def _rms_bwd_math(dhn, x, g):
    xh, r = _rms(x)
    dxh = dhn * g
    dx = r * (dxh - xh * jnp.mean(dxh * xh, axis=-1, keepdims=True))
    return dx, jnp.sum(dhn * xh, axis=0, keepdims=True)


def _final_loss(h2, gf, target):
    t = h2.shape[0]

    def body(h_ref, g_ref, t_ref, dh_ref, dhb_ref, loss_ref, dg_ref):
        i = pl.program_id(0)

        @pl.when(i == 0)
        def _():
            loss_ref[...] = jnp.zeros_like(loss_ref)
            dg_ref[...] = jnp.zeros_like(dg_ref)
            dh_ref[...] = jnp.zeros_like(dh_ref)
            dhb_ref[...] = jnp.zeros_like(dhb_ref)

        @pl.when(i > 0)
        def _():
            x = h_ref[...]
            g = g_ref[...]
            xh, _ = _rms(x)
            err = xh * g - t_ref[...]
            loss_ref[...] += 0.5 * jnp.sum(jnp.sum(err * err, axis=-1, keepdims=True) * (1.0 / D))
            dx, dg = _rms_bwd_math(err * (1.0 / D), x, g)
            dh_ref[...] = dx
            dhb_ref[...] = dx.astype(BF16)
            dg_ref[...] += dg

    row = pl.BlockSpec((TM, D), lambda i: (i, 0))
    return pl.pallas_call(
        body, name="final_loss", grid=(t // TM,),
        in_specs=[row, pl.BlockSpec((1, D), lambda i: (0, 0)),
                  pl.BlockSpec((TM, D), lambda i: (jnp.maximum(i - 1, 0), 0))],
        out_specs=[row, row, pl.BlockSpec((8, 128), lambda i: (0, 0)), pl.BlockSpec((1, D), lambda i: (0, 0))],
        out_shape=[jax.ShapeDtypeStruct((t, D), F32), jax.ShapeDtypeStruct((t, D), BF16),
                   jax.ShapeDtypeStruct((8, 128), F32), jax.ShapeDtypeStruct((1, D), F32)],
        compiler_params=_cparams(("arbitrary",)),
    )(h2, gf, target)


def _rmsnorm_bwd(dhn, h, g, dres, name):
    t = h.shape[0]

    def body(dhn_ref, h_ref, g_ref, dres_ref, dh_ref, dhb_ref, dg_ref):
        @pl.when(pl.program_id(0) == 0)
        def _():
            dg_ref[...] = jnp.zeros_like(dg_ref)

        dx, dg = _rms_bwd_math(dhn_ref[...], h_ref[...], g_ref[...])
        dh = dres_ref[...] + dx
        dh_ref[...] = dh
        dhb_ref[...] = dh.astype(BF16)
        dg_ref[...] += dg

    row = pl.BlockSpec((TM, D), lambda i: (i, 0))
    vec = pl.BlockSpec((1, D), lambda i: (0, 0))
    return pl.pallas_call(
        body, name=name, grid=(t // TM,),
        in_specs=[row, row, vec, row], out_specs=[row, row, vec],
        out_shape=[jax.ShapeDtypeStruct((t, D), F32), jax.ShapeDtypeStruct((t, D), BF16),
                   jax.ShapeDtypeStruct((1, D), F32)],
        compiler_params=_cparams(("arbitrary",)),
    )(dhn, h, g, dres)


def _rmsnorm1_bwd(dhn, h, g, dres):
    t = h.shape[0]

    def body(dhn_ref, h_ref, g_ref, dres_ref, dx_ref, dmeta_ref, dg_ref):
        i = pl.program_id(0)

        @pl.when(i == 0)
        def _():
            dg_ref[...] = jnp.zeros_like(dg_ref)

        dx, dg = _rms_bwd_math(dhn_ref[...], h_ref[...], g_ref[...])
        dh = dres_ref[...] + dx
        dg_ref[...] += dg
        dx_ref[...] = dh

        @pl.when(i == 0)
        def _():
            dmeta_ref[...] = dh[PAD:, :]

    row = pl.BlockSpec((TM, D), lambda i: (i, 0))
    vec = pl.BlockSpec((1, D), lambda i: (0, 0))
    return pl.pallas_call(
        body, name="rmsnorm1_bwd", grid=(t // TM,),
        in_specs=[row, row, vec, row],
        out_specs=[pl.BlockSpec((TM, D), lambda i: (jnp.maximum(i - 1, 0), 0)),
                   pl.BlockSpec((N_META, D), lambda i: (0, 0)), vec],
        out_shape=[jax.ShapeDtypeStruct((t - ROW0, D), F32), jax.ShapeDtypeStruct((N_META, D), F32),
                   jax.ShapeDtypeStruct((1, D), F32)],
        compiler_params=_cparams(("arbitrary",)),
    )(dhn, h, g, dres)


def _swiglu_fwd(gu):
    t = gu.shape[0]

    def body(g_ref, u_ref, o_ref):
        g = g_ref[...]
        o_ref[...] = (g * _sig(g) * u_ref[...]).astype(BF16)

    return pl.pallas_call(
        body, name="swiglu_fwd", grid=(t // TM,),
        in_specs=[pl.BlockSpec((TM, DFF), lambda i: (i, 0)), pl.BlockSpec((TM, DFF), lambda i: (i, 1))],
        out_specs=pl.BlockSpec((TM, DFF), lambda i: (i, 0)),
        out_shape=jax.ShapeDtypeStruct((t, DFF), BF16),
        compiler_params=_cparams(("parallel",)),
    )(gu, gu)


def _swiglu_bwd(dact, gu):
    t = gu.shape[0]

    def body(d_ref, g_ref, u_ref, o_ref):
        g, d = g_ref[...], d_ref[...]
        s = _sig(g)
        o_ref[:, :DFF] = (d * u_ref[...] * (s * (1.0 + g * (1.0 - s)))).astype(BF16)
        o_ref[:, DFF:] = (d * (g * s)).astype(BF16)

    return pl.pallas_call(
        body, name="swiglu_bwd", grid=(t // TM,),
        in_specs=[pl.BlockSpec((TM, DFF), lambda i: (i, 0)), pl.BlockSpec((TM, DFF), lambda i: (i, 0)),
                  pl.BlockSpec((TM, DFF), lambda i: (i, 1))],
        out_specs=pl.BlockSpec((TM, 2 * DFF), lambda i: (i, 0)),
        out_shape=jax.ShapeDtypeStruct((t, 2 * DFF), BF16),
        compiler_params=_cparams(("parallel",)),
    )(dact, gu, gu)


COL_GA, COL_GB = 4, 5
COL_AX, COL_AG = (6 * D + COL_PAD) // CT, (6 * D + COL_PAD + RGW) // CT


def _merge_fwd(pa, pb, proj):
    t = pa.shape[0]

    def body(pa_ref, pb_ref, ga_ref, gb_ref, o_ref):
        o_ref[...] = (_sig(ga_ref[...]) * pa_ref[...] + _sig(gb_ref[...]) * pb_ref[...]).astype(BF16)

    row = pl.BlockSpec((TM, D), lambda i: (i, 0))
    return pl.pallas_call(
        body, name="merge_fwd", grid=(t // TM,),
        in_specs=[row, row, pl.BlockSpec((TM, D), lambda i: (i, COL_GA)), pl.BlockSpec((TM, D), lambda i: (i, COL_GB))],
        out_specs=row, out_shape=jax.ShapeDtypeStruct((t, D), BF16),
        compiler_params=_cparams(("parallel",)),
    )(pa, pb, proj, proj)


def _merge_bwd(dm, pa, pb, proj):
    t = pa.shape[0]

    def body(dm_ref, pa_ref, pb_ref, ga_ref, gb_ref, dpa_ref, dpb_ref, dproj_ref):
        d = dm_ref[...]
        sa, sb = _sig(ga_ref[...]), _sig(gb_ref[...])
        dpa_ref[...] = (sa * d).astype(BF16)
        dpb_ref[...] = (sb * d).astype(BF16)
        dproj_ref[:, :D] = (d * pa_ref[...] * sa * (1.0 - sa)).astype(BF16)
        dproj_ref[:, D:] = (d * pb_ref[...] * sb * (1.0 - sb)).astype(BF16)

    row = pl.BlockSpec((TM, D), lambda i: (i, 0))
    return pl.pallas_call(
        body, name="merge_bwd", grid=(t // TM,),
        in_specs=[row, row, row, pl.BlockSpec((TM, D), lambda i: (i, COL_GA)), pl.BlockSpec((TM, D), lambda i: (i, COL_GB))],
        out_specs=[row, row, pl.BlockSpec((TM, 2 * D), lambda i: (i, 2))],
        out_shape=[jax.ShapeDtypeStruct((t, D), BF16), jax.ShapeDtypeStruct((t, D), BF16),
                   jax.ShapeDtypeStruct((t, D_INP), BF16)],
        compiler_params=_cparams(("parallel",)),
    )(dm, pa, pb, proj, proj)


def _zero_pad_cols(dproj):
    t = dproj.shape[0]
    tz = _pick(t, (768, 256))

    def body(_, o_ref):
        o_ref[...] = jnp.zeros_like(o_ref)

    return pl.pallas_call(
        body, name="dproj_pad", grid=(t // tz,), in_specs=[ANY],
        out_specs=pl.BlockSpec((tz, COL_PAD), lambda i: (i, (6 * D) // COL_PAD)),
        out_shape=jax.ShapeDtypeStruct(dproj.shape, BF16), input_output_aliases={0: 0},
        compiler_params=_cparams(("parallel",)),
    )(dproj)


def _prev_halo(col0):
    return pl.BlockSpec((8, CT), lambda c, i: (jnp.maximum(i * (TM // 8) - 1, 0), col0 + c))


def _conv_fwd(proj, w, b):
    t = proj.shape[0]

    def body(x_ref, halo_ref, w_ref, b_ref, o_ref, ob_ref, buf):
        i = pl.program_id(1)
        buf[0:8, :] = jnp.where(i > 0, halo_ref[...], 0.0)
        buf[8:, :] = x_ref[...]
        wv = w_ref[...]
        y = b_ref[...] + wv[3:4, :] * x_ref[...]
        for j in range(3):
            y = y + wv[j:j + 1, :] * buf[pl.ds(5 + j, TM), :]
        o_ref[...] = y
        ob_ref[...] = y.astype(BF16)

    blk = pl.BlockSpec((TM, CT), lambda c, i: (i, c))
    return pl.pallas_call(
        body, name="conv_fwd", grid=(NCT, t // TM),
        in_specs=[pl.BlockSpec((TM, CT), lambda c, i: (i, COL_AX + c)), _prev_halo(COL_AX),
                  pl.BlockSpec((4, CT), lambda c, i: (0, c)), pl.BlockSpec((1, CT), lambda c, i: (0, c))],
        out_specs=[blk, blk],
        out_shape=[jax.ShapeDtypeStruct((t, RGW), F32), jax.ShapeDtypeStruct((t, RGW), BF16)],
        scratch_shapes=[pltpu.VMEM((TM + 8, CT), F32)],
        compiler_params=_cparams(("parallel", "parallel")),
    )(proj, proj, w, b)


def _conv_bwd(dxc, proj, w, dproj):
    t = proj.shape[0]
    nt = t // TM

    def body(d_ref, dn_ref, x_ref, halo_ref, w_ref, _, dx_ref, dw_ref, db_ref, bufd, bufx):
        i = pl.program_id(1)

        @pl.when(i == 0)
        def _():
            dw_ref[...] = jnp.zeros_like(dw_ref)
            db_ref[...] = jnp.zeros_like(db_ref)

        d = d_ref[...]
        bufd[0:TM, :] = d
        bufd[TM:, :] = jnp.where(i < nt - 1, dn_ref[...], 0.0)
        bufx[0:8, :] = jnp.where(i > 0, halo_ref[...], 0.0)
        bufx[8:, :] = x_ref[...]
        wv = w_ref[...]
        dx = wv[3:4, :] * d
        for j in range(3):
            dx = dx + wv[j:j + 1, :] * bufd[pl.ds(3 - j, TM), :]
            dw_ref[j:j + 1, :] += jnp.sum(d * bufx[pl.ds(5 + j, TM), :], axis=0, keepdims=True)
        dw_ref[3:4, :] += jnp.sum(d * x_ref[...], axis=0, keepdims=True)
        db_ref[...] += jnp.sum(d, axis=0, keepdims=True)
        dx_ref[...] = dx.astype(BF16)

    return pl.pallas_call(
        body, name="conv_bwd", grid=(NCT, nt),
        in_specs=[pl.BlockSpec((TM, CT), lambda c, i: (i, c)),
                  pl.BlockSpec((8, CT), lambda c, i: (jnp.minimum((i + 1) * (TM // 8), t // 8 - 1), c)),
                  pl.BlockSpec((TM, CT), lambda c, i: (i, COL_AX + c)), _prev_halo(COL_AX),
                  pl.BlockSpec((4, CT), lambda c, i: (0, c)), ANY],
        out_specs=[pl.BlockSpec((TM, CT), lambda c, i: (i, COL_AX + c)),
                   pl.BlockSpec((4, CT), lambda c, i: (0, c)), pl.BlockSpec((1, CT), lambda c, i: (0, c))],
        out_shape=[jax.ShapeDtypeStruct((t, D_INP), BF16), jax.ShapeDtypeStruct((4, RGW), F32),
                   jax.ShapeDtypeStruct((1, RGW), F32)],
        scratch_shapes=[pltpu.VMEM((TM + 8, CT), F32), pltpu.VMEM((TM + 8, CT), F32)],
        input_output_aliases={5: 0},
        compiler_params=_cparams(("parallel", "arbitrary")),
    )(dxc, dxc, proj, proj, w, dproj)


def _gelu(x):
    c = 0.7978845608028654
    th = jnp.tanh(c * (x + 0.044715 * x * x * x))
    return 0.5 * x * (1.0 + th), th


def _rg_gates(gr, gi, xc, ba, bx, lam, row0):
    r = _sig(gr + ba)
    ig = _sig(gi + bx)
    sp = jax.nn.softplus(-lam)
    a = jnp.exp(-RG_C * r * sp)
    s = jnp.sqrt(jnp.maximum(1.0 - a * a, 1e-30))
    rows = row0 + lax.broadcasted_iota(jnp.int32, gr.shape, 0)
    live = rows >= PAD
    u = jnp.where(live, s * ig * xc, 0.0)
    return r, ig, sp, a, s, u, live


def _rg_fwd(gr, gi, xc, proj, ba, bx, lam):
    t = gr.shape[0]

    def body(gr_ref, gi_ref, xc_ref, ag_ref, ba_ref, bx_ref, lam_ref, h_ref, ya_ref, a_s, u_s, hc):
        i = pl.program_id(1)

        @pl.when(i == 0)
        def _():
            hc[...] = jnp.zeros_like(hc)

        _, _, _, a, _, u, _ = _rg_gates(gr_ref[...], gi_ref[...], xc_ref[...], ba_ref[...], bx_ref[...],
                                        lam_ref[...], i * TM)
        a_s[...] = a
        u_s[...] = u
        sub = lax.broadcasted_iota(jnp.int32, (8, CT), 0)

        def blk(k, h):
            off = pl.multiple_of(k * 8, 8)
            av, uv = a_s[pl.ds(off, 8), :], u_s[pl.ds(off, 8), :]
            out = jnp.zeros((8, CT), F32)
            for r in range(8):
                h = av[r:r + 1, :] * h + uv[r:r + 1, :]
                out = jnp.where(sub == r, h, out)
            h_ref[pl.ds(off, 8), :] = out
            return h

        hc[...] = lax.fori_loop(0, TM // 8, blk, hc[...])
        ge, _ = _gelu(ag_ref[...])
        ya_ref[...] = (h_ref[...] * ge).astype(BF16)

    blk_ = pl.BlockSpec((TM, CT), lambda c, i: (i, c))
    vec = pl.BlockSpec((1, CT), lambda c, i: (0, c))
    return pl.pallas_call(
        body, name="rg_fwd", grid=(NCT, t // TM),
        in_specs=[blk_, blk_, blk_, pl.BlockSpec((TM, CT), lambda c, i: (i, COL_AG + c)), vec, vec, vec],
        out_specs=[blk_, blk_],
        out_shape=[jax.ShapeDtypeStruct((t, RGW), F32), jax.ShapeDtypeStruct((t, RGW), BF16)],
        scratch_shapes=[pltpu.VMEM((TM, CT), F32), pltpu.VMEM((TM, CT), F32), pltpu.VMEM((1, CT), F32)],
        compiler_params=_cparams(("parallel", "arbitrary")),
    )(gr, gi, xc, proj, ba, bx, lam)


def _rg_bwd(dya, h, gr, gi, xc, proj, ba, bx, lam, dproj):
    t = gr.shape[0]
    nt = t // TM

    def body(dya_ref, h_ref, hh_ref, gr_ref, gi_ref, xc_ref, ag_ref, ba_ref, bx_ref, lam_ref, _,
             dgr_ref, dgi_ref, dxc_ref, dag_ref, dba_ref, dbx_ref, dlam_ref, a_s, d_s, g_s, hbuf, cc):
        i = pl.program_id(1)
        ri = nt - 1 - i

        @pl.when(i == 0)
        def _():
            cc[...] = jnp.zeros_like(cc)
            dba_ref[...] = jnp.zeros_like(dba_ref)
            dbx_ref[...] = jnp.zeros_like(dbx_ref)
            dlam_ref[...] = jnp.zeros_like(dlam_ref)

        xc = xc_ref[...]
        lam = lam_ref[...]
        r, ig, sp, a, s, _, live = _rg_gates(gr_ref[...], gi_ref[...], xc, ba_ref[...], bx_ref[...], lam, ri * TM)
        ag = ag_ref[...]
        ge, th = _gelu(ag)
        dya = dya_ref[...]
        hv = h_ref[...]
        c0 = 0.7978845608028654
        dge = 0.5 * (1.0 + th) + 0.5 * ag * (1.0 - th * th) * c0 * (1.0 + 3.0 * 0.044715 * ag * ag)
        dag_ref[...] = (dya * hv * dge).astype(BF16)
        a_s[...] = a
        d_s[...] = dya * ge
        sub = lax.broadcasted_iota(jnp.int32, (8, CT), 0)

        def blk(k, c):
            off = pl.multiple_of((TM // 8 - 1 - k) * 8, 8)
            av, dv = a_s[pl.ds(off, 8), :], d_s[pl.ds(off, 8), :]
            out = jnp.zeros((8, CT), F32)
            for rr in range(7, -1, -1):
                g = dv[rr:rr + 1, :] + c
                c = av[rr:rr + 1, :] * g
                out = jnp.where(sub == rr, g, out)
            g_s[pl.ds(off, 8), :] = out
            return c

        cc[...] = lax.fori_loop(0, TM // 8, blk, cc[...])
        g = g_s[...]
        hbuf[0:8, :] = jnp.where(ri > 0, hh_ref[...], 0.0)
        hbuf[8:, :] = hv
        hprev = hbuf[pl.ds(7, TM), :]
        du = jnp.where(live, g, 0.0)
        m = ig * xc
        ds = du * m
        dm = du * s
        da = g * hprev - a * ds / s
        dla = da * a
        dr = dla * (-RG_C * sp) * r * (1.0 - r)
        di = dm * xc * ig * (1.0 - ig)
        dgr_ref[...] = dr.astype(BF16)
        dgi_ref[...] = di.astype(BF16)
        dxc_ref[...] = dm * ig
        dba_ref[...] += jnp.sum(dr, axis=0, keepdims=True)
        dbx_ref[...] += jnp.sum(di, axis=0, keepdims=True)
        dlam_ref[...] += jnp.sum(dla * (-RG_C * r), axis=0, keepdims=True) * (-_sig(-lam))

    rblk = pl.BlockSpec((TM, CT), lambda c, i: (nt - 1 - i, c))
    vec = pl.BlockSpec((1, CT), lambda c, i: (0, c))
    hh = pl.BlockSpec((8, CT), lambda c, i: (jnp.maximum((nt - 1 - i) * (TM // 8) - 1, 0), c))
    agb = pl.BlockSpec((TM, CT), lambda c, i: (nt - 1 - i, COL_AG + c))
    return pl.pallas_call(
        body, name="rg_bwd", grid=(NCT, nt),
        in_specs=[rblk, rblk, hh, rblk, rblk, rblk, agb, vec, vec, vec, ANY],
        out_specs=[rblk, rblk, rblk, agb, vec, vec, vec],
        out_shape=[jax.ShapeDtypeStruct((t, RGW), BF16), jax.ShapeDtypeStruct((t, RGW), BF16),
                   jax.ShapeDtypeStruct((t, RGW), F32), jax.ShapeDtypeStruct((t, D_INP), BF16),
                   jax.ShapeDtypeStruct((1, RGW), F32), jax.ShapeDtypeStruct((1, RGW), F32),
                   jax.ShapeDtypeStruct((1, RGW), F32)],
        scratch_shapes=[pltpu.VMEM((TM, CT), F32), pltpu.VMEM((TM, CT), F32), pltpu.VMEM((TM, CT), F32),
                        pltpu.VMEM((TM + 8, CT), F32), pltpu.VMEM((1, CT), F32)],
        input_output_aliases={10: 3},
        compiler_params=_cparams(("parallel", "arbitrary")),
    )(dya, h, h, gr, gi, xc, proj, ba, bx, lam, dproj)


NCH = TM // CH
HI = lax.Precision.HIGHEST


def _hg_chunk(qr, fr, lb):
    sf = _sig(fr)
    fg = lb + (1.0 - lb) * sf
    k = (1.0 - lb) * _sig(-fr)
    sq = _sig(qr)
    q = qr * sq
    ri = lax.broadcasted_iota(jnp.int32, (CH, CH), 0)
    ci = lax.broadcasted_iota(jnp.int32, (CH, CH), 1)
    b = _dot((ri >= ci).astype(F32), jnp.log(fg), NN, HI)
    bm, bl = b[CH // 2 - 1:CH // 2, :], b[CH - 1:CH, :]
    eb = jnp.exp(b)
    ebm = jnp.exp(b - bm)
    ekm = jnp.exp(jnp.minimum(bm - b, EXP_CLAMP))
    ekl = jnp.exp(bl - b)
    return dict(sf=sf, fg=fg, k=k, sq=sq, q=q, eb=eb, ebm=ebm, ekm=ekm, ekl=ekl, ebl=jnp.exp(bl),
                qe=q * eb, qh=q * ebm, kh=k * ekm, kd=k * ekl, causal=ri >= ci, upper=(ci >= ri).astype(F32))


def _hgrn_fwd(proj, lbl, gn):
    t = proj.shape[0]
    nt = t // TM

    def body(q_ref, f_ref, v_ref, g_ref, lbl_ref, gn_ref, yb_ref, o_ref, st_ref, st):
        @pl.when(pl.program_id(0) == 0)
        def _():
            st[...] = jnp.zeros_like(st)

        l = lbl_ref[...]
        lb = _sig(l[0:1, :] - l[1:2, :])
        gnv = gn_ref[...]

        def chunk(c, carry):
            off = pl.multiple_of(c * CH, CH)
            rows = pl.ds(off, CH)
            z = _hg_chunk(q_ref[rows, :], f_ref[rows, :], lb)
            v, gg = v_ref[rows, :], g_ref[rows, :]
            for hh in range(HEADS):
                sl = slice(hh * HD, (hh + 1) * HD)
                s_prev = st[hh]
                st_ref[c, hh] = s_prev
                vb = v[:, sl].astype(BF16)
                att = jnp.where(z["causal"], _dot(z["qh"][:, sl].astype(BF16), z["kh"][:, sl].astype(BF16), NT), 0.0)
                o = _dot(z["qe"][:, sl].astype(BF16), s_prev.astype(BF16), NT) + _dot(att.astype(BF16), vb, NN)
                st[hh] = s_prev * z["ebl"][:, sl] + _dot(vb, z["kd"][:, sl].astype(BF16), TN_)
                xh, _ = _rms(o)
                gh = gg[:, sl]
                o_ref[rows, sl] = o
                yb_ref[rows, sl] = (xh * gnv * gh * _sig(gh)).astype(BF16)
            return carry

        lax.fori_loop(0, NCH, chunk, 0)

    def col(j):
        return pl.BlockSpec((TM, HGW), lambda i, j=j: (i, j))

    return pl.pallas_call(
        body, name="hgrn_fwd", grid=(nt,),
        in_specs=[col(0), col(1), col(2), col(3), pl.BlockSpec((2, HGW), lambda i: (0, 0)),
                  pl.BlockSpec((1, HD), lambda i: (0, 0))],
        out_specs=[col(0), col(0), pl.BlockSpec((NCH, HEADS, HD, HD), lambda i: (i, 0, 0, 0))],
        out_shape=[jax.ShapeDtypeStruct((t, HGW), BF16), jax.ShapeDtypeStruct((t, HGW), F32),
                   jax.ShapeDtypeStruct((t // CH, HEADS, HD, HD), F32)],
        scratch_shapes=[pltpu.VMEM((HEADS, HD, HD), F32)],
        compiler_params=_cparams(("arbitrary",)),
    )(proj, proj, proj, proj, lbl, gn)


def _hgrn_bwd(dyb, proj, o, states, lbl, gn, dproj):
    t = proj.shape[0]
    nt = t // TM

    def body(dy_ref, q_ref, f_ref, v_ref, g_ref, o_ref, st_ref, lbl_ref, gn_ref, _,
             dp_ref, dgn_ref, dl_ref, dst, dlb):
        i = pl.program_id(0)

        @pl.when(i == 0)
        def _():
            dst[...] = jnp.zeros_like(dst)
            dlb[...] = jnp.zeros_like(dlb)
            dgn_ref[...] = jnp.zeros_like(dgn_ref)

        l = lbl_ref[...]
        lb = _sig(l[0:1, :] - l[1:2, :])
        gnv = gn_ref[...]
        last = lax.broadcasted_iota(jnp.int32, (CH, HD), 0) == CH - 1

        def chunk(cc, carry):
            c = NCH - 1 - cc
            off = pl.multiple_of(c * CH, CH)
            rows = pl.ds(off, CH)
            qr, fr = q_ref[rows, :], f_ref[rows, :]
            z = _hg_chunk(qr, fr, lb)
            v, gg, ov, dy = v_ref[rows, :], g_ref[rows, :], o_ref[rows, :], dy_ref[rows, :]
            dqs, dks, dbs, dvs, dgs = [], [], [], [], []
            dgn = jnp.zeros((1, HD), F32)
            for hh in range(HEADS):
                sl = slice(hh * HD, (hh + 1) * HD)
                s_prev, ds_new = st_ref[c, hh], dst[hh]
                qe, qh, kh, kd = z["qe"][:, sl], z["qh"][:, sl], z["kh"][:, sl], z["kd"][:, sl]
                ebl = z["ebl"][:, sl]
                gh, dyh = gg[:, sl], dy[:, sl]
                xh, rr = _rms(ov[:, sl])
                sg = _sig(gh)
                dyn = dyh * (gh * sg)
                dgs.append(dyh * (xh * gnv) * (sg * (1.0 + gh * (1.0 - sg))))
                dgn = dgn + jnp.sum(dyn * xh, axis=0, keepdims=True)
                dxh = dyn * gnv
                do = (rr * (dxh - xh * jnp.mean(dxh * xh, axis=-1, keepdims=True))).astype(BF16)
                vb, dsb = v[:, sl].astype(BF16), ds_new.astype(BF16)
                qeb, qhb, khb, kdb = (a.astype(BF16) for a in (qe, qh, kh, kd))
                att = jnp.where(z["causal"], _dot(qhb, khb, NT), 0.0).astype(BF16)
                datt = jnp.where(z["causal"], _dot(do, vb, NT), 0.0).astype(BF16)
                dvs.append(_dot(att, do, TN_) + _dot(kdb, dsb, NT))
                dqe = _dot(do, s_prev.astype(BF16), NN)
                dqh = _dot(datt, khb, NN)
                dkh = _dot(datt, qhb, TN_)
                dkd = _dot(vb, dsb, NN)
                qe, qh, kh, kd = (a.astype(F32) for a in (qeb, qhb, khb, kdb))
                dbl = (jnp.sum(dkd * kd, axis=0, keepdims=True)
                       + jnp.sum(ds_new * s_prev, axis=0, keepdims=True) * ebl)
                dqs.append(dqe * z["eb"][:, sl] + dqh * z["ebm"][:, sl])
                dks.append(dkh * z["ekm"][:, sl] + dkd * z["ekl"][:, sl])
                dbs.append(dqe * qe + dqh * qh - dkh * kh - dkd * kd + jnp.where(last, dbl, 0.0))
                dst[hh] = _dot(do, qeb, TN_) + ds_new * ebl
            dgn_ref[...] += dgn
            dq, dk, db = (jnp.concatenate(x, axis=1) for x in (dqs, dks, dbs))
            dlf = _dot(z["upper"], db, NN, HI)
            sf, fg, sq = z["sf"], z["fg"], z["sq"]
            dsf = (dlf / fg - dk) * (1.0 - lb)
            dlb[...] += jnp.sum(dlf * (1.0 - sf) / fg - dk * _sig(-fr), axis=0, keepdims=True)
            dp_ref[rows, 0:HGW] = (dq * (sq * (1.0 + qr * (1.0 - sq)))).astype(BF16)
            dp_ref[rows, HGW:2 * HGW] = (dsf * sf * (1.0 - sf)).astype(BF16)
            dp_ref[rows, 2 * HGW:3 * HGW] = jnp.concatenate(dvs, axis=1).astype(BF16)
            dp_ref[rows, 3 * HGW:4 * HGW] = jnp.concatenate(dgs, axis=1).astype(BF16)
            return carry

        lax.fori_loop(0, NCH, chunk, 0)
        dl0 = dlb[...] * lb * (1.0 - lb)
        dl_ref[0:1, :] = dl0
        dl_ref[1:2, :] = -dl0

    def col(j):
        return pl.BlockSpec((TM, HGW), lambda i, j=j: (nt - 1 - i, j))

    return pl.pallas_call(
        body, name="hgrn_bwd", grid=(nt,),
        in_specs=[col(0), col(0), col(1), col(2), col(3), col(0),
                  pl.BlockSpec((NCH, HEADS, HD, HD), lambda i: (nt - 1 - i, 0, 0, 0)),
                  pl.BlockSpec((2, HGW), lambda i: (0, 0)), pl.BlockSpec((1, HD), lambda i: (0, 0)), ANY],
        out_specs=[pl.BlockSpec((TM, 4 * HGW), lambda i: (nt - 1 - i, 0)),
                   pl.BlockSpec((1, HD), lambda i: (0, 0)), pl.BlockSpec((2, HGW), lambda i: (0, 0))],
        out_shape=[jax.ShapeDtypeStruct((t, D_INP), BF16), jax.ShapeDtypeStruct((1, HD), F32),
                   jax.ShapeDtypeStruct((2, HGW), F32)],
        scratch_shapes=[pltpu.VMEM((HEADS, HD, HD), F32), pltpu.VMEM((1, HGW), F32)],
        input_output_aliases={9: 0},
        compiler_params=_cparams(("arbitrary",)),
    )(dyb, proj, proj, proj, proj, o, states, lbl, gn, dproj)


def _rotate_w_in(w):
    return jnp.concatenate([w[:, ROT:], jnp.zeros((w.shape[0], COL_PAD), w.dtype), w[:, :ROT]], axis=1)


def _unrotate_g_in(g):
    return jnp.concatenate([g[:, D_INP - ROT:], g[:, :D_IN - ROT]], axis=1)


def _block_diag(w):
    eye = jnp.eye(RG_BLOCKS, dtype=w.dtype)
    return (w[:, :, None, :] * eye[:, None, :, None]).reshape(RGW, RGW)


def _diag_blocks(wd):
    w4 = wd.reshape(RG_BLOCKS, RG_BLOCK, RG_BLOCKS, RG_BLOCK)
    return jnp.stack([w4[n, :, n, :] for n in range(RG_BLOCKS)])


def _local_step(h0, target, wts, small):
    w_in, w_pa, w_pb, w_out, w_fi, w_fd = (wts[k] for k in ("w_in", "w_proj_a", "w_proj_b", "w_out", "w_ffn_in", "w_ffn_down"))
    wa_d = _block_diag(small["rg_wa"]).astype(BF16)
    wx_d = _block_diag(small["rg_wx"]).astype(BF16)
    ba, bx, lam = small["rg_ba"], small["rg_bx"], small["rg_lambda"]
    lbl, gn = small["hg_lb_logits"], small["hg_norm_g"]
    conv_w, conv_b = small["conv_w"], small["conv_b"]

    hn1 = _rmsnorm_fwd(h0, small["norm1_g"], "rmsnorm1")
    proj = _mm(hn1, w_in, "nn", F32, "mm_proj")
    xc, xcb = _conv_fwd(proj, conv_w, conv_b)
    gr = _mm(xcb, wa_d, "nn", F32, "mm_rg_a")
    gi = _mm(xcb, wx_d, "nn", F32, "mm_rg_x")
    hrg, ya = _rg_fwd(gr, gi, xc, proj, ba, bx, lam)
    yb, o, states = _hgrn_fwd(proj, lbl, gn)
    pa = _mm(ya, w_pa, "nn", F32, "mm_pa")
    pb = _mm(yb, w_pb, "nn", F32, "mm_pb")
    merged = _merge_fwd(pa, pb, proj)
    h1 = _mm(merged, w_out, "nn", F32, "mm_out", resid=h0)
    hn2 = _rmsnorm_fwd(h1, small["norm2_g"], "rmsnorm2")
    gu = _mm(hn2, w_fi, "nn", F32, "mm_ffn_in")
    act = _swiglu_fwd(gu)
    h2 = _mm(act, w_fd, "nn", F32, "mm_ffn_down", resid=h1)
    dh2, dh2b, loss, d_normf = _final_loss(h2, small["norm_f_g"], target)

    g = {}
    g["w_ffn_down"] = _mm(act, dh2b, "tn", BF16, "mm_d_wfd")
    dact = _mm(dh2b, w_fd, "nt", F32, "mm_d_act")
    dgu = _swiglu_bwd(dact, gu)
    g["w_ffn_in"] = _mm(hn2, dgu, "tn", BF16, "mm_d_wfi")
    dhn2 = _mm(dgu, w_fi, "nt", F32, "mm_d_hn2")
    dh1, dh1b, g["norm2_g"] = _rmsnorm_bwd(dhn2, h1, small["norm2_g"], dh2, "rmsnorm2_bwd")
    g["w_out"] = _mm(merged, dh1b, "tn", BF16, "mm_d_wout")
    dmerged = _mm(dh1b, w_out, "nt", F32, "mm_d_merged")
    dpa, dpb, dproj = _merge_bwd(dmerged, pa, pb, proj)
    dproj = _zero_pad_cols(dproj)
    g["w_proj_a"] = _mm(ya, dpa, "tn", BF16, "mm_d_wpa")
    g["w_proj_b"] = _mm(yb, dpb, "tn", BF16, "mm_d_wpb")
    dya = _mm(dpa, w_pa, "nt", F32, "mm_d_ya")
    dyb = _mm(dpb, w_pb, "nt", F32, "mm_d_yb")
    dproj, g["hg_norm_g"], g["hg_lb_logits"] = _hgrn_bwd(dyb, proj, o, states, lbl, gn, dproj)
    dgr, dgi, dxc, dproj, g["rg_ba"], g["rg_bx"], g["rg_lambda"] = _rg_bwd(dya, hrg, gr, gi, xc, proj, ba, bx, lam, dproj)
    g["rg_wa"] = _diag_blocks(_mm(xcb, dgr, "tn", F32, "mm_d_wa"))
    g["rg_wx"] = _diag_blocks(_mm(xcb, dgi, "tn", F32, "mm_d_wx"))
    dxc = _mm(dgr, wa_d, "nt", F32, "mm_d_xc_a", resid=dxc)
    dxc = _mm(dgi, wx_d, "nt", F32, "mm_d_xc_x", resid=dxc)
    dproj, g["conv_w"], g["conv_b"] = _conv_bwd(dxc, proj, conv_w, dproj)
    g["w_in"] = _mm(hn1, dproj, "tn", BF16, "mm_d_win")
    dhn1 = _mm(dproj, w_in, "nt", F32, "mm_d_hn1")
    grad_x, g["meta_tokens"], g["norm1_g"] = _rmsnorm1_bwd(dhn1, h0, small["norm1_g"], dh1)
    g["norm_f_g"] = d_normf
    return loss, grad_x, g


def _adamw_math(w, g, m, v):
    m = B1 * m + (1.0 - B1) * g
    v = B2 * v + (1.0 - B2) * (g * g)
    m_hat = m / (1.0 - B1 ** STEP)
    v_hat = v / (1.0 - B2 ** STEP)
    return -LR * (m_hat / (jnp.sqrt(v_hat) + ADAM_EPS) + WD * w), m, v


def _adamw(w, g, m, v, name):
    r, c = w.shape
    tr = _pick(r, (256, 352, 320, 128, 64, 32, 16, 8))

    def body(w_ref, g_ref, m_ref, v_ref, d_ref, mo_ref, vo_ref):
        d_ref[...], mo_ref[...], vo_ref[...] = _adamw_math(w_ref[...], g_ref[...], m_ref[...], v_ref[...])

    blk = pl.BlockSpec((tr, c), lambda i: (i, 0))
    return pl.pallas_call(
        body, name=name, grid=(r // tr,), in_specs=[blk] * 4, out_specs=[blk] * 3,
        out_shape=[jax.ShapeDtypeStruct((r, c), F32)] * 3,
        compiler_params=_cparams(("parallel",)),
    )(w, g, m, v)


def _adamw_small(recv, w, m, v):
    rows = w.shape[0]
    tr = _pick(rows, (376, 256, 128, 64, 32, 16, 8))

    def body(r_ref, w_ref, m_ref, v_ref, g_ref, d_ref, mo_ref, vo_ref):
        g = r_ref[0]
        for p in range(1, 8):
            g = g + r_ref[p]
        g_ref[...] = g
        d_ref[...], mo_ref[...], vo_ref[...] = _adamw_math(w_ref[...], g, m_ref[...], v_ref[...])

    blk = pl.BlockSpec((tr, 128), lambda i: (i, 0))
    return pl.pallas_call(
        body, name="adamw_small", grid=(rows // tr,),
        in_specs=[pl.BlockSpec((8, tr, 128), lambda i: (0, i, 0)), blk, blk, blk], out_specs=[blk] * 4,
        out_shape=[jax.ShapeDtypeStruct((rows, 128), F32)] * 4,
        compiler_params=_cparams(("parallel",)),
    )(recv, w, m, v)


def _sum8(recv, name):
    _, r, c = recv.shape
    tr = _pick(r, (128, 176, 160, 64, 32, 16))

    def body(r_ref, o_ref):
        g = r_ref[0].astype(F32)
        for p in range(1, 8):
            g = g + r_ref[p].astype(F32)
        o_ref[...] = g

    return pl.pallas_call(
        body, name=name, grid=(r // tr,),
        in_specs=[pl.BlockSpec((8, tr, c), lambda i: (0, i, 0))], out_specs=pl.BlockSpec((tr, c), lambda i: (i, 0)),
        out_shape=jax.ShapeDtypeStruct((r, c), F32),
        compiler_params=_cparams(("parallel",)),
    )(recv)


BIG = ("w_in", "w_proj_a", "w_proj_b", "w_out", "w_ffn_in", "w_ffn_down")
COL_SHARDED = ("w_in", "w_ffn_in")


def _coords():
    return lax.axis_index("x"), lax.axis_index("y"), lax.axis_index("c")


def _allgather_shards(shards):
    n = len(shards)

    def body(*refs):
        ins, outs = refs[:n], refs[n:2 * n]
        send, recv, local = refs[2 * n:]
        x, y, c = _coords()
        chips = [(1 - x, y), (x, 1 - y), (1 - x, 1 - y)]
        me = 2 * x + y
        locals_, sends, fwds = [], [], []
        for w in range(n):
            half = ins[w].shape[0] // 2
            rows = pl.ds(pl.multiple_of(c * half, 8), half)
            cp = pltpu.make_async_copy(ins[w], outs[w].at[me], local.at[w])
            cp.start()
            locals_.append(cp)
            for k, (cx, cy) in enumerate(chips):
                cp = pltpu.make_async_remote_copy(
                    src_ref=ins[w].at[rows], dst_ref=outs[w].at[me, rows],
                    send_sem=send.at[w, k], recv_sem=recv.at[w, k], device_id=(cx, cy, c), device_id_type=MESH)
                cp.start()
                sends.append(cp)
        for w in range(n):
            half = ins[w].shape[0] // 2
            rows = pl.ds(pl.multiple_of(c * half, 8), half)
            for k, (cx, cy) in enumerate(chips):
                slab = outs[w].at[2 * cx + cy, rows]
                pltpu.make_async_remote_copy(
                    src_ref=slab, dst_ref=slab, send_sem=send.at[w, k], recv_sem=recv.at[w, k],
                    device_id=(cx, cy, c), device_id_type=MESH).wait_recv()
                cp = pltpu.make_async_remote_copy(
                    src_ref=slab, dst_ref=slab, send_sem=send.at[w, 3 + k], recv_sem=recv.at[w, 3 + k],
                    device_id=(x, y, 1 - c), device_id_type=MESH)
                cp.start()
                fwds.append(cp)
        for w in range(n):
            half = ins[w].shape[0] // 2
            orows = pl.ds(pl.multiple_of((1 - c) * half, 8), half)
            for k, (cx, cy) in enumerate(chips):
                slab = outs[w].at[2 * cx + cy, orows]
                pltpu.make_async_remote_copy(
                    src_ref=slab, dst_ref=slab, send_sem=send.at[w, 3 + k], recv_sem=recv.at[w, 3 + k],
                    device_id=(x, y, 1 - c), device_id_type=MESH).wait_recv()
        for cp in sends + fwds:
            cp.wait_send()
        for cp in locals_:
            cp.wait()

    return pl.pallas_call(
        body, name="allgather_weights",
        in_specs=[ANY] * n, out_specs=[ANY] * n,
        out_shape=[jax.ShapeDtypeStruct((4,) + s.shape, s.dtype) for s in shards],
        scratch_shapes=[pltpu.SemaphoreType.DMA((n, 6)), pltpu.SemaphoreType.DMA((n, 6)), pltpu.SemaphoreType.DMA((n,))],
    )(*shards)


def _peer(x, y, c, k):
    fx, fy, fc = (k >> 2) & 1, (k >> 1) & 1, k & 1
    return (x ^ fx if fx else x, y ^ fy if fy else y, c ^ fc if fc else c)


def _exchange_grads(grads, pack):
    names = list(BIG)
    n = len(names) + 1
    srcs = [grads[k] for k in names] + [pack]

    def sub_shape(name, full):
        r, c = full.shape
        return (r // 2, c // 4) if name in COL_SHARDED else (r // 8, c)

    shapes = [sub_shape(k, grads[k]) for k in names] + [pack.shape]

    def body(*refs):
        ins, outs = refs[:n], refs[n:2 * n]
        send, recv, local = refs[2 * n:]
        x, y, c = _coords()
        me = 4 * x + 2 * y + c

        def region(w, px, py, pc):
            if w == n - 1:
                return ins[w]
            sr, sc = shapes[w]
            if names[w] in COL_SHARDED:
                return ins[w].at[pl.ds(pl.multiple_of(pc * sr, 16), sr), pl.ds(pl.multiple_of((2 * px + py) * sc, 128), sc)]
            return ins[w].at[pl.ds(pl.multiple_of((4 * px + 2 * py + pc) * sr, 16), sr)]

        copies, locals_ = [], []
        for w in range(n):
            cp = pltpu.make_async_copy(region(w, x, y, c), outs[w].at[me], local.at[w])
            cp.start()
            locals_.append(cp)
            for k in range(1, 8):
                px, py, pc = _peer(x, y, c, k)
                cp = pltpu.make_async_remote_copy(
                    src_ref=region(w, px, py, pc), dst_ref=outs[w].at[me],
                    send_sem=send.at[w, k - 1], recv_sem=recv.at[w, k - 1], device_id=(px, py, pc), device_id_type=MESH)
                cp.start()
                copies.append(cp)
        for w in range(n):
            for k in range(1, 8):
                px, py, pc = _peer(x, y, c, k)
                slot = outs[w].at[4 * px + 2 * py + pc]
                pltpu.make_async_remote_copy(
                    src_ref=slot, dst_ref=slot, send_sem=send.at[w, k - 1], recv_sem=recv.at[w, k - 1],
                    device_id=(px, py, pc), device_id_type=MESH).wait_recv()
        for cp in copies:
            cp.wait_send()
        for cp in locals_:
            cp.wait()

    return pl.pallas_call(
        body, name="exchange_grads",
        in_specs=[ANY] * n, out_specs=[ANY] * n,
        out_shape=[jax.ShapeDtypeStruct((8,) + s, a.dtype) for s, a in zip(shapes, srcs)],
        scratch_shapes=[pltpu.SemaphoreType.DMA((n, 7)), pltpu.SemaphoreType.DMA((n, 7)), pltpu.SemaphoreType.DMA((n,))],
    )(*srcs)


def _sibling_merge(halves):
    n = len(halves)

    def body(*refs):
        ins, outs = refs[:n], refs[n:2 * n]
        send, recv, local = refs[2 * n:]
        x, y, c = _coords()
        copies, locals_ = [], []
        for w in range(n):
            half = ins[w].shape[0]
            rows = pl.ds(pl.multiple_of(c * half, 8), half)
            cp = pltpu.make_async_copy(ins[w], outs[w].at[rows], local.at[w])
            cp.start()
            locals_.append(cp)
            cp = pltpu.make_async_remote_copy(
                src_ref=ins[w], dst_ref=outs[w].at[rows], send_sem=send.at[w], recv_sem=recv.at[w],
                device_id=(x, y, 1 - c), device_id_type=MESH)
            cp.start()
            copies.append(cp)
        for w in range(n):
            half = ins[w].shape[0]
            orows = outs[w].at[pl.ds(pl.multiple_of((1 - c) * half, 8), half)]
            pltpu.make_async_remote_copy(
                src_ref=orows, dst_ref=orows, send_sem=send.at[w], recv_sem=recv.at[w],
                device_id=(x, y, 1 - c), device_id_type=MESH).wait_recv()
        for cp in copies:
            cp.wait_send()
        for cp in locals_:
            cp.wait()

    return pl.pallas_call(
        body, name="sibling_merge",
        in_specs=[ANY] * n, out_specs=[ANY] * n,
        out_shape=[jax.ShapeDtypeStruct((2 * h.shape[0], h.shape[1]), h.dtype) for h in halves],
        scratch_shapes=[pltpu.SemaphoreType.DMA((n,)), pltpu.SemaphoreType.DMA((n,)), pltpu.SemaphoreType.DMA((n,))],
    )(*halves)


SMALL = ("norm1_g", "conv_b", "rg_wa", "rg_ba", "rg_wx", "rg_bx", "rg_lambda", "hg_lb_logits", "hg_norm_g",
         "norm2_g", "norm_f_g")
SHARDED_SMALL = ("meta_tokens", "conv_w")
SEG = 1024


def _pack(vals):
    parts = []
    for k in SMALL + SHARDED_SMALL:
        v = vals[k].reshape(-1).astype(F32)
        parts.append(jnp.pad(v, (0, -v.shape[0] % SEG)))
    return jnp.concatenate(parts).reshape(-1, 128)


def _unpack(flat, shapes):
    out, off = {}, 0
    flat = flat.reshape(-1)
    for k in SMALL + SHARDED_SMALL:
        size = 1
        for s in shapes[k]:
            size *= s
        out[k] = flat[off:off + size].reshape(shapes[k])
        off += size + (-size % SEG)
    return out


ORDER = ("meta_tokens", "norm1_g", "w_in", "conv_w", "conv_b", "rg_wa", "rg_ba", "rg_wx", "rg_bx", "rg_lambda",
         "hg_lb_logits", "hg_norm_g", "w_proj_a", "w_proj_b", "w_out", "norm2_g", "w_ffn_in", "w_ffn_down", "norm_f_g")


def kernel(x, meta_tokens, norm1_g, w_in, conv_w, conv_b, rg_wa, rg_ba, rg_wx, rg_bx, rg_lambda, hg_lb_logits, hg_norm_g, w_proj_a, w_proj_b, w_out, norm2_g, w_ffn_in, w_ffn_down, norm_f_g, loss_target, m_meta_tokens, m_norm1_g, m_w_in, m_conv_w, m_conv_b, m_rg_wa, m_rg_ba, m_rg_wx, m_rg_bx, m_rg_lambda, m_hg_lb_logits, m_hg_norm_g, m_w_proj_a, m_w_proj_b, m_w_out, m_norm2_g, m_w_ffn_in, m_w_ffn_down, m_norm_f_g, v_meta_tokens, v_norm1_g, v_w_in, v_conv_w, v_conv_b, v_rg_wa, v_rg_ba, v_rg_wx, v_rg_bx, v_rg_lambda, v_hg_lb_logits, v_hg_norm_g, v_w_proj_a, v_w_proj_b, v_w_out, v_norm2_g, v_w_ffn_in, v_w_ffn_down, v_norm_f_g):
    args = dict(locals())
    w = {k: args[k] for k in ORDER}
    m = {k: args["m_" + k] for k in ORDER}
    v = {k: args["v_" + k] for k in ORDER}
    xi, yi, _ = _coords()
    chip = 2 * xi + yi

    shard2d = {k: w[k][0] for k in BIG}
    tiny = jnp.zeros((32, 384), F32)
    tiny = tiny.at[0:N_META, 0:256].set(meta_tokens).at[N_META:N_META + 4, 0:320].set(conv_w[0])
    gathered = _allgather_shards([shard2d[k].astype(BF16) for k in BIG] + [tiny])
    full = {}
    for k, gth in zip(BIG, gathered[:-1]):
        if k in COL_SHARDED:
            full[k] = jnp.transpose(gth, (1, 0, 2)).reshape(gth.shape[1], -1)
        else:
            full[k] = gth.reshape(-1, gth.shape[2])
    full["w_in"] = _rotate_w_in(full["w_in"])
    tiny_all = gathered[-1]
    meta_full = jnp.transpose(tiny_all[:, 0:N_META, 0:256], (1, 0, 2)).reshape(N_META, D)
    conv_w_full = jnp.transpose(tiny_all[:, N_META:N_META + 4, 0:320], (1, 0, 2)).reshape(4, RGW)

    small = dict(norm1_g=norm1_g, conv_w=conv_w_full, conv_b=conv_b, rg_wa=rg_wa[0], rg_ba=rg_ba, rg_wx=rg_wx[0],
                 rg_bx=rg_bx, rg_lambda=rg_lambda, hg_lb_logits=hg_lb_logits, hg_norm_g=hg_norm_g,
                 norm2_g=norm2_g, norm_f_g=norm_f_g.reshape(1, D))

    h0 = jnp.concatenate([jnp.zeros((PAD, D), F32), meta_full, x[0]], axis=0)
    loss_blk, grad_x, g = _local_step(h0, loss_target[0], full, small)
    loss = lax.psum(loss_blk[0, 0], ("x", "y", "c"))

    g["w_in"] = _unrotate_g_in(g["w_in"])
    small_shapes = {k: w[k].shape for k in SMALL}
    small_shapes["meta_tokens"], small_shapes["conv_w"] = (N_META, D), (1, 4, RGW)
    pack = _pack(g)
    recv = _exchange_grads(g, pack)
    halves = [_sum8(r, "sum8_" + k) for k, r in zip(BIG, recv[:-1])]
    g_big = dict(zip(BIG, _sibling_merge(halves)))

    w_sm = {k: w[k] for k in SMALL}
    m_sm = {k: m[k] for k in SMALL}
    v_sm = {k: v[k] for k in SMALL}
    for d in (w_sm, m_sm, v_sm):
        d["meta_tokens"], d["conv_w"] = jnp.zeros((N_META, D), F32), jnp.zeros((1, 4, RGW), F32)
    outs_sm = [_unpack(o, small_shapes) for o in _adamw_small(recv[-1], _pack(w_sm), _pack(m_sm), _pack(v_sm))]
    grad, delta, new_m, new_v = ({k: o[k] for k in SMALL} for o in outs_sm)
    g_small_full = outs_sm[0]
    g_shard = {"meta_tokens": lax.dynamic_slice(g_small_full["meta_tokens"], (0, chip * 256), (N_META, 256)),
               "conv_w": lax.dynamic_slice(g_small_full["conv_w"], (0, 0, chip * 320), (1, 4, 320))[0]}
    for k in BIG + SHARDED_SMALL:
        gk = g_big[k] if k in BIG else g_shard[k]
        wk, mk, vk = (d[k].reshape(gk.shape) for d in (w, m, v))
        dk, mk, vk = _adamw(wk, gk, mk, vk, "adamw_" + k)
        grad[k], delta[k], new_m[k], new_v[k] = (a.reshape(w[k].shape) for a in (gk, dk, mk, vk))

    return (loss, grad_x[None], *[grad[k] for k in ORDER], *[delta[k] for k in ORDER],
            *[new_m[k] for k in ORDER], *[new_v[k] for k in ORDER])
```

```python
import functools

import jax
import jax.numpy as jnp
from jax import lax
from jax.experimental import pallas as pl
from jax.experimental.pallas import tpu as pltpu

F32, BF16 = jnp.float32, jnp.bfloat16
D = 1024
N_META = 16
RGW = 1280
RG_BLOCKS, RG_BLOCK = 16, 80
RG_C = 8.0
HEADS, HD = 8, 128
HGW = HEADS * HD
DFF = 2816
D_IN = 2 * RGW + 6 * D
ROT = 2 * RGW
COL_PAD = 256
D_INP = D_IN + COL_PAD
EPS = 1e-6
CH = 64
TM = 256
ROW0 = TM
PAD = ROW0 - N_META
CT = RGW
NCT = RGW // CT
EXP_CLAMP = 80.0
VMEM_LIMIT = 56 * 1024 * 1024

LR, B1, B2, ADAM_EPS, WD, STEP = 0.001, 0.9, 0.999, 1e-08, 0.01, 10
MESH = pl.DeviceIdType.MESH
ANY = pl.BlockSpec(memory_space=pl.ANY)


def _cparams(sem):
    return pltpu.CompilerParams(dimension_semantics=sem, vmem_limit_bytes=VMEM_LIMIT)


def _pick(n, prefs):
    for p in prefs:
        if n % p == 0:
            return p
    return n


def _sig(x):
    return jax.nn.sigmoid(x)


def _dot(a, b, dims, precision=None):
    return lax.dot_general(a, b, (dims, ((), ())), preferred_element_type=F32, precision=precision)


NN, NT, TN_ = ((1,), (0,)), ((1,), (1,)), ((0,), (0,))


class _Host:
    def __init__(self, ins, out_shapes, aliases, sems, start, mid, finish):
        self.ins, self.out_shapes, self.aliases, self.sems = ins, out_shapes, aliases, sems
        self.start, self.mid, self.finish = start, mid, finish


def _mm(a, b, mode, out_dtype, name, resid=None, host=None):
    if mode == "nn":
        (m, kd), n = a.shape, b.shape[1]
    elif mode == "nt":
        (m, kd), n = a.shape, b.shape[0]
    else:
        (kd, m), n = a.shape, b.shape[1]
    if mode == "tn":
        tm = _pick(m, (1024, 1280, 1408, 640, 512, 256, 128))
        tk = _pick(kd, (1408, 768, 512, 256, 128))
    else:
        tm = _pick(m, (768, 512, 640, 256, 128))
        tk = kd if kd <= 2816 else _pick(kd, (1792, 1408, 1024, 512, 256, 128))
    tn = _pick(n, (1792, 1408, 1280, 1024, 512, 256, 128))
    nk = kd // tk
    dims = {"nn": NN, "nt": NT, "tn": TN_}[mode]

    ni, nj = m // tm, n // tn
    n_hin = len(host.ins) if host else 0
    n_hout = len(host.out_shapes) if host else 0
    n_res = 0 if resid is None else 1

    def finish(r, r_ref, o_ref):
        if resid is not None:
            r = r + r_ref[...]
        o_ref[...] = r.astype(out_dtype)

    def body(*refs):
        a_ref, b_ref = refs[:2]
        r_ref = refs[2] if resid is not None else None
        pos = 2 + n_res
        h_in = refs[pos:pos + n_hin]
        o_ref = refs[pos + n_hin]
        h_out = refs[pos + n_hin + 1:pos + n_hin + 1 + n_hout]
        scratch = refs[pos + n_hin + 1 + n_hout:]
        h_sems = scratch[(1 if nk > 1 else 0):]
        i, j, k = pl.program_id(0), pl.program_id(1), pl.program_id(2)
        if host:
            @pl.when((i == 0) & (j == 0) & (k == 0))
            def _():
                host.start(h_in, h_out, h_sems)

            if host.mid is not None:
                @pl.when((i == ni // 2) & (j == 0) & (k == 0))
                def _():
                    host.mid(h_in, h_out, h_sems)

        if nk == 1:
            finish(_dot(a_ref[...], b_ref[...], dims), r_ref, o_ref)
        else:
            acc = scratch[0]

            @pl.when(k == 0)
            def _():
                acc[...] = jnp.zeros_like(acc)

            acc[...] += _dot(a_ref[...], b_ref[...], dims)

            @pl.when(k == nk - 1)
            def _():
                finish(acc[...], r_ref, o_ref)

        if host:
            @pl.when((i == ni - 1) & (j == nj - 1) & (k == nk - 1))
            def _():
                host.finish(h_in, h_out, h_sems)

    if mode == "tn":
        a_spec = pl.BlockSpec((tk, tm), lambda i, j, k: (k, i))
    else:
        a_spec = pl.BlockSpec((tm, tk), lambda i, j, k: (i, k))
    if mode == "nt":
        b_spec = pl.BlockSpec((tn, tk), lambda i, j, k: (j, k))
    else:
        b_spec = pl.BlockSpec((tk, tn), lambda i, j, k: (k, j))
    o_spec = pl.BlockSpec((tm, tn), lambda i, j, k: (i, j))
    in_specs, args = [a_spec, b_spec], [a, b]
    if resid is not None:
        in_specs.append(o_spec)
        args.append(resid)
    out_shape = jax.ShapeDtypeStruct((m, n), out_dtype)
    scratch = [pltpu.VMEM((tm, tn), F32)] if nk > 1 else []
    if not host:
        return pl.pallas_call(
            body, name=name, grid=(ni, nj, nk), in_specs=in_specs, out_specs=o_spec, out_shape=out_shape,
            scratch_shapes=scratch, compiler_params=_cparams(("parallel", "parallel", "arbitrary")),
        )(*args)
    outs = pl.pallas_call(
        body, name=name, grid=(ni, nj, nk),
        in_specs=in_specs + [ANY] * n_hin, out_specs=[o_spec] + [ANY] * n_hout,
        out_shape=[out_shape] + list(host.out_shapes),
        scratch_shapes=scratch + list(host.sems),
        input_output_aliases={2 + n_res + i_in: 1 + i_out for i_in, i_out in host.aliases.items()},
        compiler_params=_cparams(("arbitrary", "arbitrary", "arbitrary")),
    )(*args, *host.ins)
    return outs[0], list(outs[1:])


def _rms(x):
    r = lax.rsqrt(jnp.mean(x * x, axis=-1, keepdims=True) + EPS)
    return x * r, r


def _rmsnorm_fwd(h, g, name):
    t = h.shape[0]

    def body(h_ref, g_ref, o_ref):
        xh, _ = _rms(h_ref[...])
        o_ref[...] = (xh * g_ref[...]).astype(BF16)

    return pl.pallas_call(
        body, name=name, grid=(t // TM,),
        in_specs=[pl.BlockSpec((TM, D), lambda i: (i, 0)), pl.BlockSpec((1, D), lambda i: (0, 0))],
        out_specs=pl.BlockSpec((TM, D), lambda i: (i, 0)),
        out_shape=jax.ShapeDtypeStruct((t, D), BF16),
        compiler_params=_cparams(("parallel",)),
    )(h, g)


def _rms_bwd_math(dhn, x, g):
    xh, r = _rms(x)
    dxh = dhn * g
    dx = r * (dxh - xh * jnp.mean(dxh * xh, axis=-1, keepdims=True))
    return dx, jnp.sum(dhn * xh, axis=0, keepdims=True)


def _final_loss(h2, gf, target):
    t = h2.shape[0]

    def body(h_ref, g_ref, t_ref, dh_ref, dhb_ref, loss_ref, dg_ref):
        i = pl.program_id(0)

        @pl.when(i == 0)
        def _():
            loss_ref[...] = jnp.zeros_like(loss_ref)
            dg_ref[...] = jnp.zeros_like(dg_ref)
            dh_ref[...] = jnp.zeros_like(dh_ref)
            dhb_ref[...] = jnp.zeros_like(dhb_ref)

        @pl.when(i > 0)
        def _():
            x = h_ref[...]
            g = g_ref[...]
            xh, _ = _rms(x)
            err = xh * g - t_ref[...]
            loss_ref[...] += 0.5 * jnp.sum(jnp.sum(err * err, axis=-1, keepdims=True) * (1.0 / D))
            dx, dg = _rms_bwd_math(err * (1.0 / D), x, g)
            dh_ref[...] = dx
            dhb_ref[...] = dx.astype(BF16)
            dg_ref[...] += dg

    row = pl.BlockSpec((TM, D), lambda i: (i, 0))
    return pl.pallas_call(
        body, name="final_loss", grid=(t // TM,),
        in_specs=[row, pl.BlockSpec((1, D), lambda i: (0, 0)),
                  pl.BlockSpec((TM, D), lambda i: (jnp.maximum(i - 1, 0), 0))],
        out_specs=[row, row, pl.BlockSpec((8, 128), lambda i: (0, 0)), pl.BlockSpec((1, D), lambda i: (0, 0))],
        out_shape=[jax.ShapeDtypeStruct((t, D), F32), jax.ShapeDtypeStruct((t, D), BF16),
                   jax.ShapeDtypeStruct((8, 128), F32), jax.ShapeDtypeStruct((1, D), F32)],
        compiler_params=_cparams(("arbitrary",)),
    )(h2, gf, target)


def _rmsnorm_bwd(dhn, h, g, dres, name):
    t = h.shape[0]

    def body(dhn_ref, h_ref, g_ref, dres_ref, dh_ref, dhb_ref, dg_ref):
        @pl.when(pl.program_id(0) == 0)
        def _():
            dg_ref[...] = jnp.zeros_like(dg_ref)

        dx, dg = _rms_bwd_math(dhn_ref[...], h_ref[...], g_ref[...])
        dh = dres_ref[...] + dx
        dh_ref[...] = dh
        dhb_ref[...] = dh.astype(BF16)
        dg_ref[...] += dg

    row = pl.BlockSpec((TM, D), lambda i: (i, 0))
    vec = pl.BlockSpec((1, D), lambda i: (0, 0))
    return pl.pallas_call(
        body, name=name, grid=(t // TM,),
        in_specs=[row, row, vec, row], out_specs=[row, row, vec],
        out_shape=[jax.ShapeDtypeStruct((t, D), F32), jax.ShapeDtypeStruct((t, D), BF16),
                   jax.ShapeDtypeStruct((1, D), F32)],
        compiler_params=_cparams(("arbitrary",)),
    )(dhn, h, g, dres)


def _rmsnorm1_bwd(dhn, h, g, dres):
    t = h.shape[0]

    def body(dhn_ref, h_ref, g_ref, dres_ref, dx_ref, dmeta_ref, dg_ref):
        i = pl.program_id(0)

        @pl.when(i == 0)
        def _():
            dg_ref[...] = jnp.zeros_like(dg_ref)

        dx, dg = _rms_bwd_math(dhn_ref[...], h_ref[...], g_ref[...])
        dh = dres_ref[...] + dx
        dg_ref[...] += dg
        dx_ref[...] = dh

        @pl.when(i == 0)
        def _():
            dmeta_ref[...] = dh[PAD:, :]

    row = pl.BlockSpec((TM, D), lambda i: (i, 0))
    vec = pl.BlockSpec((1, D), lambda i: (0, 0))
    return pl.pallas_call(
        body, name="rmsnorm1_bwd", grid=(t // TM,),
        in_specs=[row, row, vec, row],
        out_specs=[pl.BlockSpec((TM, D), lambda i: (jnp.maximum(i - 1, 0), 0)),
                   pl.BlockSpec((N_META, D), lambda i: (0, 0)), vec],
        out_shape=[jax.ShapeDtypeStruct((t - ROW0, D), F32), jax.ShapeDtypeStruct((N_META, D), F32),
                   jax.ShapeDtypeStruct((1, D), F32)],
        compiler_params=_cparams(("arbitrary",)),
    )(dhn, h, g, dres)


def _swiglu_fwd(gu):
    t = gu.shape[0]

    def body(g_ref, u_ref, o_ref):
        g = g_ref[...]
        o_ref[...] = (g * _sig(g) * u_ref[...]).astype(BF16)

    return pl.pallas_call(
        body, name="swiglu_fwd", grid=(t // TM,),
        in_specs=[pl.BlockSpec((TM, DFF), lambda i: (i, 0)), pl.BlockSpec((TM, DFF), lambda i: (i, 1))],
        out_specs=pl.BlockSpec((TM, DFF), lambda i: (i, 0)),
        out_shape=jax.ShapeDtypeStruct((t, DFF), BF16),
        compiler_params=_cparams(("parallel",)),
    )(gu, gu)


def _swiglu_bwd(dact, gu):
    t = gu.shape[0]

    def body(d_ref, g_ref, u_ref, o_ref):
        g, d = g_ref[...], d_ref[...]
        s = _sig(g)
        o_ref[:, :DFF] = (d * u_ref[...] * (s * (1.0 + g * (1.0 - s)))).astype(BF16)
        o_ref[:, DFF:] = (d * (g * s)).astype(BF16)

    return pl.pallas_call(
        body, name="swiglu_bwd", grid=(t // TM,),
        in_specs=[pl.BlockSpec((TM, DFF), lambda i: (i, 0)), pl.BlockSpec((TM, DFF), lambda i: (i, 0)),
                  pl.BlockSpec((TM, DFF), lambda i: (i, 1))],
        out_specs=pl.BlockSpec((TM, 2 * DFF), lambda i: (i, 0)),
        out_shape=jax.ShapeDtypeStruct((t, 2 * DFF), BF16),
        compiler_params=_cparams(("parallel",)),
    )(dact, gu, gu)


COL_GA, COL_GB = 4, 5
COL_AX, COL_AG = (6 * D + COL_PAD) // CT, (6 * D + COL_PAD + RGW) // CT


def _merge_fwd(pa, pb, proj):
    t = pa.shape[0]

    def body(pa_ref, pb_ref, ga_ref, gb_ref, o_ref):
        o_ref[...] = (_sig(ga_ref[...]) * pa_ref[...] + _sig(gb_ref[...]) * pb_ref[...]).astype(BF16)

    row = pl.BlockSpec((TM, D), lambda i: (i, 0))
    return pl.pallas_call(
        body, name="merge_fwd", grid=(t // TM,),
        in_specs=[row, row, pl.BlockSpec((TM, D), lambda i: (i, COL_GA)), pl.BlockSpec((TM, D), lambda i: (i, COL_GB))],
        out_specs=row, out_shape=jax.ShapeDtypeStruct((t, D), BF16),
        compiler_params=_cparams(("parallel",)),
    )(pa, pb, proj, proj)


def _merge_bwd(dm, pa, pb, proj):
    t = pa.shape[0]

    def body(dm_ref, pa_ref, pb_ref, ga_ref, gb_ref, dpa_ref, dpb_ref, dproj_ref):
        d = dm_ref[...]
        sa, sb = _sig(ga_ref[...]), _sig(gb_ref[...])
        dpa_ref[...] = (sa * d).astype(BF16)
        dpb_ref[...] = (sb * d).astype(BF16)
        dproj_ref[:, :D] = (d * pa_ref[...] * sa * (1.0 - sa)).astype(BF16)
        dproj_ref[:, D:] = (d * pb_ref[...] * sb * (1.0 - sb)).astype(BF16)

    row = pl.BlockSpec((TM, D), lambda i: (i, 0))
    return pl.pallas_call(
        body, name="merge_bwd", grid=(t // TM,),
        in_specs=[row, row, row, pl.BlockSpec((TM, D), lambda i: (i, COL_GA)), pl.BlockSpec((TM, D), lambda i: (i, COL_GB))],
        out_specs=[row, row, pl.BlockSpec((TM, 2 * D), lambda i: (i, 2))],
        out_shape=[jax.ShapeDtypeStruct((t, D), BF16), jax.ShapeDtypeStruct((t, D), BF16),
                   jax.ShapeDtypeStruct((t, D_INP), BF16)],
        compiler_params=_cparams(("parallel",)),
    )(dm, pa, pb, proj, proj)


def _zero_pad_cols(dproj):
    t = dproj.shape[0]
    tz = _pick(t, (768, 256))

    def body(_, o_ref):
        o_ref[...] = jnp.zeros_like(o_ref)

    return pl.pallas_call(
        body, name="dproj_pad", grid=(t // tz,), in_specs=[ANY],
        out_specs=pl.BlockSpec((tz, COL_PAD), lambda i: (i, (6 * D) // COL_PAD)),
        out_shape=jax.ShapeDtypeStruct(dproj.shape, BF16), input_output_aliases={0: 0},
        compiler_params=_cparams(("parallel",)),
    )(dproj)


def _prev_halo(col0):
    return pl.BlockSpec((8, CT), lambda c, i: (jnp.maximum(i * (TM // 8) - 1, 0), col0 + c))


def _conv_fwd(proj, w, b):
    t = proj.shape[0]

    def body(x_ref, halo_ref, w_ref, b_ref, o_ref, ob_ref, buf):
        i = pl.program_id(1)
        buf[0:8, :] = jnp.where(i > 0, halo_ref[...], 0.0)
        buf[8:, :] = x_ref[...]
        wv = w_ref[...]
        y = b_ref[...] + wv[3:4, :] * x_ref[...]
        for j in range(3):
            y = y + wv[j:j + 1, :] * buf[pl.ds(5 + j, TM), :]
        o_ref[...] = y
        ob_ref[...] = y.astype(BF16)

    blk = pl.BlockSpec((TM, CT), lambda c, i: (i, c))
    return pl.pallas_call(
        body, name="conv_fwd", grid=(NCT, t // TM),
        in_specs=[pl.BlockSpec((TM, CT), lambda c, i: (i, COL_AX + c)), _prev_halo(COL_AX),
                  pl.BlockSpec((4, CT), lambda c, i: (0, c)), pl.BlockSpec((1, CT), lambda c, i: (0, c))],
        out_specs=[blk, blk],
        out_shape=[jax.ShapeDtypeStruct((t, RGW), F32), jax.ShapeDtypeStruct((t, RGW), BF16)],
        scratch_shapes=[pltpu.VMEM((TM + 8, CT), F32)],
        compiler_params=_cparams(("parallel", "parallel")),
    )(proj, proj, w, b)


def _conv_bwd(dxc, proj, w, dproj):
    t = proj.shape[0]
    nt = t // TM

    def body(d_ref, dn_ref, x_ref, halo_ref, w_ref, _, dx_ref, dw_ref, db_ref, bufd, bufx):
        i = pl.program_id(1)

        @pl.when(i == 0)
        def _():
            dw_ref[...] = jnp.zeros_like(dw_ref)
            db_ref[...] = jnp.zeros_like(db_ref)

        d = d_ref[...]
        bufd[0:TM, :] = d
        bufd[TM:, :] = jnp.where(i < nt - 1, dn_ref[...], 0.0)
        bufx[0:8, :] = jnp.where(i > 0, halo_ref[...], 0.0)
        bufx[8:, :] = x_ref[...]
        wv = w_ref[...]
        dx = wv[3:4, :] * d
        for j in range(3):
            dx = dx + wv[j:j + 1, :] * bufd[pl.ds(3 - j, TM), :]
            dw_ref[j:j + 1, :] += jnp.sum(d * bufx[pl.ds(5 + j, TM), :], axis=0, keepdims=True)
        dw_ref[3:4, :] += jnp.sum(d * x_ref[...], axis=0, keepdims=True)
        db_ref[...] += jnp.sum(d, axis=0, keepdims=True)
        dx_ref[...] = dx.astype(BF16)

    return pl.pallas_call(
        body, name="conv_bwd", grid=(NCT, nt),
        in_specs=[pl.BlockSpec((TM, CT), lambda c, i: (i, c)),
                  pl.BlockSpec((8, CT), lambda c, i: (jnp.minimum((i + 1) * (TM // 8), t // 8 - 1), c)),
                  pl.BlockSpec((TM, CT), lambda c, i: (i, COL_AX + c)), _prev_halo(COL_AX),
                  pl.BlockSpec((4, CT), lambda c, i: (0, c)), ANY],
        out_specs=[pl.BlockSpec((TM, CT), lambda c, i: (i, COL_AX + c)),
                   pl.BlockSpec((4, CT), lambda c, i: (0, c)), pl.BlockSpec((1, CT), lambda c, i: (0, c))],
        out_shape=[jax.ShapeDtypeStruct((t, D_INP), BF16), jax.ShapeDtypeStruct((4, RGW), F32),
                   jax.ShapeDtypeStruct((1, RGW), F32)],
        scratch_shapes=[pltpu.VMEM((TM + 8, CT), F32), pltpu.VMEM((TM + 8, CT), F32)],
        input_output_aliases={5: 0},
        compiler_params=_cparams(("parallel", "arbitrary")),
    )(dxc, dxc, proj, proj, w, dproj)


def _gelu(x):
    c = 0.7978845608028654
    th = jnp.tanh(c * (x + 0.044715 * x * x * x))
    return 0.5 * x * (1.0 + th), th


def _rg_gates(gr, gi, xc, ba, bx, lam, row0):
    r = _sig(gr + ba)
    ig = _sig(gi + bx)
    sp = jax.nn.softplus(-lam)
    a = jnp.exp(-RG_C * r * sp)
    s = jnp.sqrt(jnp.maximum(1.0 - a * a, 1e-30))
    rows = row0 + lax.broadcasted_iota(jnp.int32, gr.shape, 0)
    live = rows >= PAD
    u = jnp.where(live, s * ig * xc, 0.0)
    return r, ig, sp, a, s, u, live


def _rg_fwd(gr, gi, xc, proj, ba, bx, lam):
    t = gr.shape[0]

    def body(gr_ref, gi_ref, xc_ref, ag_ref, ba_ref, bx_ref, lam_ref, h_ref, ya_ref, a_s, u_s, hc):
        i = pl.program_id(1)

        @pl.when(i == 0)
        def _():
            hc[...] = jnp.zeros_like(hc)

        _, _, _, a, _, u, _ = _rg_gates(gr_ref[...], gi_ref[...], xc_ref[...], ba_ref[...], bx_ref[...],
                                        lam_ref[...], i * TM)
        a_s[...] = a
        u_s[...] = u
        sub = lax.broadcasted_iota(jnp.int32, (8, CT), 0)

        def blk(k, h):
            off = pl.multiple_of(k * 8, 8)
            av, uv = a_s[pl.ds(off, 8), :], u_s[pl.ds(off, 8), :]
            out = jnp.zeros((8, CT), F32)
            for r in range(8):
                h = av[r:r + 1, :] * h + uv[r:r + 1, :]
                out = jnp.where(sub == r, h, out)
            h_ref[pl.ds(off, 8), :] = out
            return h

        hc[...] = lax.fori_loop(0, TM // 8, blk, hc[...])
        ge, _ = _gelu(ag_ref[...])
        ya_ref[...] = (h_ref[...] * ge).astype(BF16)

    blk_ = pl.BlockSpec((TM, CT), lambda c, i: (i, c))
    vec = pl.BlockSpec((1, CT), lambda c, i: (0, c))
    return pl.pallas_call(
        body, name="rg_fwd", grid=(NCT, t // TM),
        in_specs=[blk_, blk_, blk_, pl.BlockSpec((TM, CT), lambda c, i: (i, COL_AG + c)), vec, vec, vec],
        out_specs=[blk_, blk_],
        out_shape=[jax.ShapeDtypeStruct((t, RGW), F32), jax.ShapeDtypeStruct((t, RGW), BF16)],
        scratch_shapes=[pltpu.VMEM((TM, CT), F32), pltpu.VMEM((TM, CT), F32), pltpu.VMEM((1, CT), F32)],
        compiler_params=_cparams(("parallel", "arbitrary")),
    )(gr, gi, xc, proj, ba, bx, lam)


def _rg_bwd(dya, h, gr, gi, xc, proj, ba, bx, lam, dproj):
    t = gr.shape[0]
    nt = t // TM

    def body(dya_ref, h_ref, hh_ref, gr_ref, gi_ref, xc_ref, ag_ref, ba_ref, bx_ref, lam_ref, _,
             dgr_ref, dgi_ref, dxc_ref, dag_ref, dba_ref, dbx_ref, dlam_ref, a_s, d_s, g_s, hbuf, cc):
        i = pl.program_id(1)
        ri = nt - 1 - i

        @pl.when(i == 0)
        def _():
            cc[...] = jnp.zeros_like(cc)
            dba_ref[...] = jnp.zeros_like(dba_ref)
            dbx_ref[...] = jnp.zeros_like(dbx_ref)
            dlam_ref[...] = jnp.zeros_like(dlam_ref)

        xc = xc_ref[...]
        lam = lam_ref[...]
        r, ig, sp, a, s, _, live = _rg_gates(gr_ref[...], gi_ref[...], xc, ba_ref[...], bx_ref[...], lam, ri * TM)
        ag = ag_ref[...]
        ge, th = _gelu(ag)
        dya = dya_ref[...]
        hv = h_ref[...]
        c0 = 0.7978845608028654
        dge = 0.5 * (1.0 + th) + 0.5 * ag * (1.0 - th * th) * c0 * (1.0 + 3.0 * 0.044715 * ag * ag)
        dag_ref[...] = (dya * hv * dge).astype(BF16)
        a_s[...] = a
        d_s[...] = dya * ge
        sub = lax.broadcasted_iota(jnp.int32, (8, CT), 0)

        def blk(k, c):
            off = pl.multiple_of((TM // 8 - 1 - k) * 8, 8)
            av, dv = a_s[pl.ds(off, 8), :], d_s[pl.ds(off, 8), :]
            out = jnp.zeros((8, CT), F32)
            for rr in range(7, -1, -1):
                g = dv[rr:rr + 1, :] + c
                c = av[rr:rr + 1, :] * g
                out = jnp.where(sub == rr, g, out)
            g_s[pl.ds(off, 8), :] = out
            return c

        cc[...] = lax.fori_loop(0, TM // 8, blk, cc[...])
        g = g_s[...]
        hbuf[0:8, :] = jnp.where(ri > 0, hh_ref[...], 0.0)
        hbuf[8:, :] = hv
        hprev = hbuf[pl.ds(7, TM), :]
        du = jnp.where(live, g, 0.0)
        m = ig * xc
        ds = du * m
        dm = du * s
        da = g * hprev - a * ds / s
        dla = da * a
        dr = dla * (-RG_C * sp) * r * (1.0 - r)
        di = dm * xc * ig * (1.0 - ig)
        dgr_ref[...] = dr.astype(BF16)
        dgi_ref[...] = di.astype(BF16)
        dxc_ref[...] = dm * ig
        dba_ref[...] += jnp.sum(dr, axis=0, keepdims=True)
        dbx_ref[...] += jnp.sum(di, axis=0, keepdims=True)
        dlam_ref[...] += jnp.sum(dla * (-RG_C * r), axis=0, keepdims=True) * (-_sig(-lam))

    rblk = pl.BlockSpec((TM, CT), lambda c, i: (nt - 1 - i, c))
    vec = pl.BlockSpec((1, CT), lambda c, i: (0, c))
    hh = pl.BlockSpec((8, CT), lambda c, i: (jnp.maximum((nt - 1 - i) * (TM // 8) - 1, 0), c))
    agb = pl.BlockSpec((TM, CT), lambda c, i: (nt - 1 - i, COL_AG + c))
    return pl.pallas_call(
        body, name="rg_bwd", grid=(NCT, nt),
        in_specs=[rblk, rblk, hh, rblk, rblk, rblk, agb, vec, vec, vec, ANY],
        out_specs=[rblk, rblk, rblk, agb, vec, vec, vec],
        out_shape=[jax.ShapeDtypeStruct((t, RGW), BF16), jax.ShapeDtypeStruct((t, RGW), BF16),
                   jax.ShapeDtypeStruct((t, RGW), F32), jax.ShapeDtypeStruct((t, D_INP), BF16),
                   jax.ShapeDtypeStruct((1, RGW), F32), jax.ShapeDtypeStruct((1, RGW), F32),
                   jax.ShapeDtypeStruct((1, RGW), F32)],
        scratch_shapes=[pltpu.VMEM((TM, CT), F32), pltpu.VMEM((TM, CT), F32), pltpu.VMEM((TM, CT), F32),
                        pltpu.VMEM((TM + 8, CT), F32), pltpu.VMEM((1, CT), F32)],
        input_output_aliases={10: 3},
        compiler_params=_cparams(("parallel", "arbitrary")),
    )(dya, h, h, gr, gi, xc, proj, ba, bx, lam, dproj)


NCH = TM // CH
HI = lax.Precision.HIGHEST


def _hg_chunk(qr, fr, lb):
    sf = _sig(fr)
    fg = lb + (1.0 - lb) * sf
    k = (1.0 - lb) * _sig(-fr)
    sq = _sig(qr)
    q = qr * sq
    ri = lax.broadcasted_iota(jnp.int32, (CH, CH), 0)
    ci = lax.broadcasted_iota(jnp.int32, (CH, CH), 1)
    b = _dot((ri >= ci).astype(F32), jnp.log(fg), NN, HI)
    bm, bl = b[CH // 2 - 1:CH // 2, :], b[CH - 1:CH, :]
    eb = jnp.exp(b)
    ebm = jnp.exp(b - bm)
    ekm = jnp.exp(jnp.minimum(bm - b, EXP_CLAMP))
    ekl = jnp.exp(bl - b)
    return dict(sf=sf, fg=fg, k=k, sq=sq, q=q, eb=eb, ebm=ebm, ekm=ekm, ekl=ekl, ebl=jnp.exp(bl),
                qe=q * eb, qh=q * ebm, kh=k * ekm, kd=k * ekl, causal=ri >= ci, upper=(ci >= ri).astype(F32))


def _hgrn_fwd(proj, lbl, gn):
    t = proj.shape[0]
    nt = t // TM

    def body(q_ref, f_ref, v_ref, g_ref, lbl_ref, gn_ref, yb_ref, o_ref, st_ref, st):
        @pl.when(pl.program_id(0) == 0)
        def _():
            st[...] = jnp.zeros_like(st)

        l = lbl_ref[...]
        lb = _sig(l[0:1, :] - l[1:2, :])
        gnv = gn_ref[...]

        def chunk(c, carry):
            off = pl.multiple_of(c * CH, CH)
            rows = pl.ds(off, CH)
            z = _hg_chunk(q_ref[rows, :], f_ref[rows, :], lb)
            v, gg = v_ref[rows, :], g_ref[rows, :]
            for hh in range(HEADS):
                sl = slice(hh * HD, (hh + 1) * HD)
                s_prev = st[hh]
                st_ref[c, hh] = s_prev
                vb = v[:, sl].astype(BF16)
                att = jnp.where(z["causal"], _dot(z["qh"][:, sl].astype(BF16), z["kh"][:, sl].astype(BF16), NT), 0.0)
                o = _dot(z["qe"][:, sl].astype(BF16), s_prev.astype(BF16), NT) + _dot(att.astype(BF16), vb, NN)
                st[hh] = s_prev * z["ebl"][:, sl] + _dot(vb, z["kd"][:, sl].astype(BF16), TN_)
                xh, _ = _rms(o)
                gh = gg[:, sl]
                o_ref[rows, sl] = o
                yb_ref[rows, sl] = (xh * gnv * gh * _sig(gh)).astype(BF16)
            return carry

        lax.fori_loop(0, NCH, chunk, 0)

    def col(j):
        return pl.BlockSpec((TM, HGW), lambda i, j=j: (i, j))

    return pl.pallas_call(
        body, name="hgrn_fwd", grid=(nt,),
        in_specs=[col(0), col(1), col(2), col(3), pl.BlockSpec((2, HGW), lambda i: (0, 0)),
                  pl.BlockSpec((1, HD), lambda i: (0, 0))],
        out_specs=[col(0), col(0), pl.BlockSpec((NCH, HEADS, HD, HD), lambda i: (i, 0, 0, 0))],
        out_shape=[jax.ShapeDtypeStruct((t, HGW), BF16), jax.ShapeDtypeStruct((t, HGW), F32),
                   jax.ShapeDtypeStruct((t // CH, HEADS, HD, HD), F32)],
        scratch_shapes=[pltpu.VMEM((HEADS, HD, HD), F32)],
        compiler_params=_cparams(("arbitrary",)),
    )(proj, proj, proj, proj, lbl, gn)


def _hgrn_bwd(dyb, proj, o, states, lbl, gn, dproj):
    t = proj.shape[0]
    nt = t // TM

    def body(dy_ref, q_ref, f_ref, v_ref, g_ref, o_ref, st_ref, lbl_ref, gn_ref, _,
             dp_ref, dgn_ref, dl_ref, dst, dlb):
        i = pl.program_id(0)

        @pl.when(i == 0)
        def _():
            dst[...] = jnp.zeros_like(dst)
            dlb[...] = jnp.zeros_like(dlb)
            dgn_ref[...] = jnp.zeros_like(dgn_ref)

        l = lbl_ref[...]
        lb = _sig(l[0:1, :] - l[1:2, :])
        gnv = gn_ref[...]
        last = lax.broadcasted_iota(jnp.int32, (CH, HD), 0) == CH - 1

        def chunk(cc, carry):
            c = NCH - 1 - cc
            off = pl.multiple_of(c * CH, CH)
            rows = pl.ds(off, CH)
            qr, fr = q_ref[rows, :], f_ref[rows, :]
            z = _hg_chunk(qr, fr, lb)
            v, gg, ov, dy = v_ref[rows, :], g_ref[rows, :], o_ref[rows, :], dy_ref[rows, :]
            dqs, dks, dbs, dvs, dgs = [], [], [], [], []
            dgn = jnp.zeros((1, HD), F32)
            for hh in range(HEADS):
                sl = slice(hh * HD, (hh + 1) * HD)
                s_prev, ds_new = st_ref[c, hh], dst[hh]
                qe, qh, kh, kd = z["qe"][:, sl], z["qh"][:, sl], z["kh"][:, sl], z["kd"][:, sl]
                ebl = z["ebl"][:, sl]
                gh, dyh = gg[:, sl], dy[:, sl]
                xh, rr = _rms(ov[:, sl])
                sg = _sig(gh)
                dyn = dyh * (gh * sg)
                dgs.append(dyh * (xh * gnv) * (sg * (1.0 + gh * (1.0 - sg))))
                dgn = dgn + jnp.sum(dyn * xh, axis=0, keepdims=True)
                dxh = dyn * gnv
                do = (rr * (dxh - xh * jnp.mean(dxh * xh, axis=-1, keepdims=True))).astype(BF16)
                vb, dsb = v[:, sl].astype(BF16), ds_new.astype(BF16)
                qeb, qhb, khb, kdb = (a.astype(BF16) for a in (qe, qh, kh, kd))
                att = jnp.where(z["causal"], _dot(qhb, khb, NT), 0.0).astype(BF16)
                datt = jnp.where(z["causal"], _dot(do, vb, NT), 0.0).astype(BF16)
                dvs.append(_dot(att, do, TN_) + _dot(kdb, dsb, NT))
                dqe = _dot(do, s_prev.astype(BF16), NN)
                dqh = _dot(datt, khb, NN)
                dkh = _dot(datt, qhb, TN_)
                dkd = _dot(vb, dsb, NN)
                qe, qh, kh, kd = (a.astype(F32) for a in (qeb, qhb, khb, kdb))
                dbl = (jnp.sum(dkd * kd, axis=0, keepdims=True)
                       + jnp.sum(ds_new * s_prev, axis=0, keepdims=True) * ebl)
                dqs.append(dqe * z["eb"][:, sl] + dqh * z["ebm"][:, sl])
                dks.append(dkh * z["ekm"][:, sl] + dkd * z["ekl"][:, sl])
                dbs.append(dqe * qe + dqh * qh - dkh * kh - dkd * kd + jnp.where(last, dbl, 0.0))
                dst[hh] = _dot(do, qeb, TN_) + ds_new * ebl
            dgn_ref[...] += dgn
            dq, dk, db = (jnp.concatenate(x, axis=1) for x in (dqs, dks, dbs))
            dlf = _dot(z["upper"], db, NN, HI)
            sf, fg, sq = z["sf"], z["fg"], z["sq"]
            dsf = (dlf / fg - dk) * (1.0 - lb)
            dlb[...] += jnp.sum(dlf * (1.0 - sf) / fg - dk * _sig(-fr), axis=0, keepdims=True)
            dp_ref[rows, 0:HGW] = (dq * (sq * (1.0 + qr * (1.0 - sq)))).astype(BF16)
            dp_ref[rows, HGW:2 * HGW] = (dsf * sf * (1.0 - sf)).astype(BF16)
            dp_ref[rows, 2 * HGW:3 * HGW] = jnp.concatenate(dvs, axis=1).astype(BF16)
            dp_ref[rows, 3 * HGW:4 * HGW] = jnp.concatenate(dgs, axis=1).astype(BF16)
            return carry

        lax.fori_loop(0, NCH, chunk, 0)
        dl0 = dlb[...] * lb * (1.0 - lb)
        dl_ref[0:1, :] = dl0
        dl_ref[1:2, :] = -dl0

    def col(j):
        return pl.BlockSpec((TM, HGW), lambda i, j=j: (nt - 1 - i, j))

    return pl.pallas_call(
        body, name="hgrn_bwd", grid=(nt,),
        in_specs=[col(0), col(0), col(1), col(2), col(3), col(0),
                  pl.BlockSpec((NCH, HEADS, HD, HD), lambda i: (nt - 1 - i, 0, 0, 0)),
                  pl.BlockSpec((2, HGW), lambda i: (0, 0)), pl.BlockSpec((1, HD), lambda i: (0, 0)), ANY],
        out_specs=[pl.BlockSpec((TM, 4 * HGW), lambda i: (nt - 1 - i, 0)),
                   pl.BlockSpec((1, HD), lambda i: (0, 0)), pl.BlockSpec((2, HGW), lambda i: (0, 0))],
        out_shape=[jax.ShapeDtypeStruct((t, D_INP), BF16), jax.ShapeDtypeStruct((1, HD), F32),
                   jax.ShapeDtypeStruct((2, HGW), F32)],
        scratch_shapes=[pltpu.VMEM((HEADS, HD, HD), F32), pltpu.VMEM((1, HGW), F32)],
        input_output_aliases={9: 0},
        compiler_params=_cparams(("arbitrary",)),
    )(dyb, proj, proj, proj, proj, o, states, lbl, gn, dproj)


def _rotate_w_in(w):
    return jnp.concatenate([w[:, ROT:], jnp.zeros((w.shape[0], COL_PAD), w.dtype), w[:, :ROT]], axis=1)


def _unrotate_g_in(g):
    return jnp.concatenate([g[:, D_INP - ROT:], g[:, :D_IN - ROT]], axis=1)


def _block_diag(w):
    eye = jnp.eye(RG_BLOCKS, dtype=w.dtype)
    return (w[:, :, None, :] * eye[:, None, :, None]).reshape(RGW, RGW)


def _diag_blocks(wd):
    w4 = wd.reshape(RG_BLOCKS, RG_BLOCK, RG_BLOCKS, RG_BLOCK)
    return jnp.stack([w4[n, :, n, :] for n in range(RG_BLOCKS)])


def _local_step(h0, target, wts, small, hosts=None, finalize=None):
    hosts = hosts or {}
    carried = {}
    g = {}

    def mm(a, b, mode, out_dtype, name, resid=None):
        if name not in hosts:
            return _mm(a, b, mode, out_dtype, name, resid=resid)
        host = hosts[name](g)
        out, res = _mm(a, b, mode, out_dtype, name, resid=resid, host=host)
        carried[name] = (host.ins, res)
        return out

    w_in = wts["w_in"]
    wa_d = _block_diag(small["rg_wa"]).astype(BF16)
    wx_d = _block_diag(small["rg_wx"]).astype(BF16)
    ba, bx, lam = small["rg_ba"], small["rg_bx"], small["rg_lambda"]
    lbl, gn = small["hg_lb_logits"], small["hg_norm_g"]
    conv_w, conv_b = small["conv_w"], small["conv_b"]

    hn1 = _rmsnorm_fwd(h0, small["norm1_g"], "rmsnorm1")
    proj = mm(hn1, w_in, "nn", F32, "mm_proj")
    if finalize is not None:
        wts = finalize(carried["mm_proj"][1])
    w_pa, w_pb, w_out, w_fi, w_fd = (wts[k] for k in ("w_proj_a", "w_proj_b", "w_out", "w_ffn_in", "w_ffn_down"))
    xc, xcb = _conv_fwd(proj, conv_w, conv_b)
    gr = mm(xcb, wa_d, "nn", F32, "mm_rg_a")
    gi = mm(xcb, wx_d, "nn", F32, "mm_rg_x")
    hrg, ya = _rg_fwd(gr, gi, xc, proj, ba, bx, lam)
    yb, o, states = _hgrn_fwd(proj, lbl, gn)
    pa = mm(ya, w_pa, "nn", F32, "mm_pa")
    pb = mm(yb, w_pb, "nn", F32, "mm_pb")
    merged = _merge_fwd(pa, pb, proj)
    h1 = mm(merged, w_out, "nn", F32, "mm_out", resid=h0)
    hn2 = _rmsnorm_fwd(h1, small["norm2_g"], "rmsnorm2")
    gu = mm(hn2, w_fi, "nn", F32, "mm_ffn_in")
    act = _swiglu_fwd(gu)
    h2 = mm(act, w_fd, "nn", F32, "mm_ffn_down", resid=h1)
    dh2, dh2b, loss, d_normf = _final_loss(h2, small["norm_f_g"], target)

    g["w_ffn_down"] = mm(act, dh2b, "tn", BF16, "mm_d_wfd")
    dact = mm(dh2b, w_fd, "nt", F32, "mm_d_act")
    dgu = _swiglu_bwd(dact, gu)
    g["w_ffn_in"] = mm(hn2, dgu, "tn", BF16, "mm_d_wfi")
    dhn2 = mm(dgu, w_fi, "nt", F32, "mm_d_hn2")
    dh1, dh1b, g["norm2_g"] = _rmsnorm_bwd(dhn2, h1, small["norm2_g"], dh2, "rmsnorm2_bwd")
    g["w_out"] = mm(merged, dh1b, "tn", BF16, "mm_d_wout")
    dmerged = mm(dh1b, w_out, "nt", F32, "mm_d_merged")
    dpa, dpb, dproj = _merge_bwd(dmerged, pa, pb, proj)
    dproj = _zero_pad_cols(dproj)
    g["w_proj_a"] = mm(ya, dpa, "tn", BF16, "mm_d_wpa")
    g["w_proj_b"] = mm(yb, dpb, "tn", BF16, "mm_d_wpb")
    dya = mm(dpa, w_pa, "nt", F32, "mm_d_ya")
    dyb = mm(dpb, w_pb, "nt", F32, "mm_d_yb")
    dproj, g["hg_norm_g"], g["hg_lb_logits"] = _hgrn_bwd(dyb, proj, o, states, lbl, gn, dproj)
    dgr, dgi, dxc, dproj, g["rg_ba"], g["rg_bx"], g["rg_lambda"] = _rg_bwd(dya, hrg, gr, gi, xc, proj, ba, bx, lam, dproj)
    g["rg_wa"] = _diag_blocks(mm(xcb, dgr, "tn", F32, "mm_d_wa"))
    g["rg_wx"] = _diag_blocks(mm(xcb, dgi, "tn", F32, "mm_d_wx"))
    dxc = mm(dgr, wa_d, "nt", F32, "mm_d_xc_a", resid=dxc)
    dxc = mm(dgi, wx_d, "nt", F32, "mm_d_xc_x", resid=dxc)
    dproj, g["conv_w"], g["conv_b"] = _conv_bwd(dxc, proj, conv_w, dproj)
    g["w_in"] = mm(hn1, dproj, "tn", BF16, "mm_d_win")
    dhn1 = mm(dproj, w_in, "nt", F32, "mm_d_hn1")
    grad_x, g["meta_tokens"], g["norm1_g"] = _rmsnorm1_bwd(dhn1, h0, small["norm1_g"], dh1)
    g["norm_f_g"] = d_normf
    return loss, grad_x, g, carried


def _adamw_math(w, g, m, v):
    m = B1 * m + (1.0 - B1) * g
    v = B2 * v + (1.0 - B2) * (g * g)
    m_hat = m / (1.0 - B1 ** STEP)
    v_hat = v / (1.0 - B2 ** STEP)
    return -LR * (m_hat / (jnp.sqrt(v_hat) + ADAM_EPS) + WD * w), m, v


def _adamw(w, g, m, v, name):
    r, c = w.shape
    tr = _pick(r, (256, 352, 320, 128, 64, 32, 16, 8))

    def body(w_ref, g_ref, m_ref, v_ref, d_ref, mo_ref, vo_ref):
        d_ref[...], mo_ref[...], vo_ref[...] = _adamw_math(w_ref[...], g_ref[...], m_ref[...], v_ref[...])

    blk = pl.BlockSpec((tr, c), lambda i: (i, 0))
    return pl.pallas_call(
        body, name=name, grid=(r // tr,), in_specs=[blk] * 4, out_specs=[blk] * 3,
        out_shape=[jax.ShapeDtypeStruct((r, c), F32)] * 3,
        compiler_params=_cparams(("parallel",)),
    )(w, g, m, v)


def _adamw_small(recv, w, m, v):
    rows = w.shape[0]
    tr = _pick(rows, (376, 256, 128, 64, 32, 16, 8))

    def body(r_ref, w_ref, m_ref, v_ref, g_ref, d_ref, mo_ref, vo_ref):
        g = r_ref[0]
        for p in range(1, 8):
            g = g + r_ref[p]
        g_ref[...] = g
        d_ref[...], mo_ref[...], vo_ref[...] = _adamw_math(w_ref[...], g, m_ref[...], v_ref[...])

    blk = pl.BlockSpec((tr, 128), lambda i: (i, 0))
    return pl.pallas_call(
        body, name="adamw_small", grid=(rows // tr,),
        in_specs=[pl.BlockSpec((8, tr, 128), lambda i: (0, i, 0)), blk, blk, blk], out_specs=[blk] * 4,
        out_shape=[jax.ShapeDtypeStruct((rows, 128), F32)] * 4,
        compiler_params=_cparams(("parallel",)),
    )(recv, w, m, v)


def _sum_place(recv, gfull, col_sharded, xyc, name):
    _, r, c = recv.shape
    tr = _pick(r, (128, 176, 160, 64, 32, 16))
    nb = r // tr

    def body(_, own_ref, r_ref, o_ref):
        g = own_ref[...].astype(F32)
        for p in range(7):
            g = g + r_ref[p].astype(F32)
        o_ref[...] = g

    if col_sharded:
        own = pl.BlockSpec((tr, c), lambda i, s: (s[2] * nb + i, 2 * s[0] + s[1]))
    else:
        own = pl.BlockSpec((tr, c), lambda i, s: ((4 * s[0] + 2 * s[1] + s[2]) * nb + i, 0))
    return pl.pallas_call(
        body, name=name,
        grid_spec=pltpu.PrefetchScalarGridSpec(
            num_scalar_prefetch=1, grid=(nb,),
            in_specs=[own, pl.BlockSpec((7, tr, c), lambda i, s: (0, i, 0))],
            out_specs=pl.BlockSpec((tr, c), lambda i, s: (s[2] * nb + i, 0))),
        out_shape=jax.ShapeDtypeStruct((2 * r, c), F32),
        compiler_params=_cparams(("arbitrary",)),
    )(xyc, gfull, recv)


def _place_own(shard, slot, dtype, name):
    r, c = shard.shape
    tr = _pick(r, (256, 352, 320, 128, 32))

    def body(_, x_ref, o_ref):
        o_ref[...] = x_ref[...].astype(dtype)

    return pl.pallas_call(
        body, name=name,
        grid_spec=pltpu.PrefetchScalarGridSpec(
            num_scalar_prefetch=1, grid=(r // tr,),
            in_specs=[pl.BlockSpec((tr, c), lambda i, s: (i, 0))],
            out_specs=pl.BlockSpec((None, tr, c), lambda i, s: (s[0], i, 0))),
        out_shape=jax.ShapeDtypeStruct((4, r, c), dtype),
        compiler_params=_cparams(("arbitrary",)),
    )(slot, shard)


BIG = ("w_in", "w_proj_a", "w_proj_b", "w_out", "w_ffn_in", "w_ffn_down")
COL_SHARDED = ("w_in", "w_ffn_in")


def _coords():
    return lax.axis_index("x"), lax.axis_index("y"), lax.axis_index("c")


def _gather_host(bufs):
    n = len(bufs)

    def place():
        x, y, c = _coords()
        return x, y, c, [(1 - x, y), (x, 1 - y), (1 - x, 1 - y)]

    def slab(b, chip, half_of):
        half = b.shape[1] // 2
        return b.at[chip, pl.ds(pl.multiple_of(half_of * half, 8), half)]

    def copy(b, sems, w, k, chip, half_of, to):
        s = slab(b, chip, half_of)
        return pltpu.make_async_remote_copy(src_ref=s, dst_ref=s, send_sem=sems[0].at[w, k], recv_sem=sems[1].at[w, k],
                                            device_id=to, device_id_type=MESH)

    def start(_, outs, sems):
        x, y, c, chips = place()
        for w, b in enumerate(outs):
            for k, (cx, cy) in enumerate(chips):
                copy(b, sems, w, k, 2 * x + y, c, (cx, cy, c)).start()

    def mid(_, outs, sems):
        x, y, c, chips = place()
        for w, b in enumerate(outs):
            for k, (cx, cy) in enumerate(chips):
                copy(b, sems, w, k, 2 * cx + cy, c, (cx, cy, c)).wait_recv()
                copy(b, sems, w, 3 + k, 2 * cx + cy, c, (x, y, 1 - c)).start()

    def finish(_, outs, sems):
        x, y, c, chips = place()
        for w, b in enumerate(outs):
            for k, (cx, cy) in enumerate(chips):
                copy(b, sems, w, 3 + k, 2 * cx + cy, 1 - c, (x, y, 1 - c)).wait_recv()
        for w, b in enumerate(outs):
            for k, (cx, cy) in enumerate(chips):
                copy(b, sems, w, k, 2 * x + y, c, (cx, cy, c)).wait_send()
                copy(b, sems, w, 3 + k, 2 * cx + cy, c, (x, y, 1 - c)).wait_send()

    return _Host(ins=list(bufs), out_shapes=[jax.ShapeDtypeStruct(b.shape, b.dtype) for b in bufs],
                 aliases={i: i for i in range(n)},
                 sems=[pltpu.SemaphoreType.DMA((n, 6)), pltpu.SemaphoreType.DMA((n, 6))],
                 start=start, mid=mid, finish=finish)


def _run_host(host, name):
    n_in, n_out = len(host.ins), len(host.out_shapes)

    def body(*refs):
        ins, outs, sems = refs[:n_in], refs[n_in:n_in + n_out], refs[n_in + n_out:]
        host.start(ins, outs, sems)
        if host.mid is not None:
            host.mid(ins, outs, sems)
        host.finish(ins, outs, sems)

    return pl.pallas_call(
        body, name=name, in_specs=[ANY] * n_in, out_specs=[ANY] * n_out, out_shape=list(host.out_shapes),
        scratch_shapes=list(host.sems), input_output_aliases=dict(host.aliases),
    )(*host.ins)


def _peer(x, y, c, k):
    fx, fy, fc = (k >> 2) & 1, (k >> 1) & 1, k & 1
    return (x ^ fx if fx else x, y ^ fy if fy else y, c ^ fc if fc else c)


def _sub_shape(name, full_shape):
    r, c = full_shape
    return (r // 2, c // 4) if name in COL_SHARDED else (r // 8, c)


def _exchange_host(names, grads):
    n = len(names)
    shapes = [_sub_shape(k, g.shape) for k, g in zip(names, grads)]

    def copy(ins, outs, sems, w, k):
        x, y, c = _coords()
        px, py, pc = _peer(x, y, c, k)
        sr, sc = shapes[w]
        if names[w] in COL_SHARDED:
            src = ins[w].at[pl.ds(pl.multiple_of(pc * sr, 16), sr), pl.ds(pl.multiple_of((2 * px + py) * sc, 128), sc)]
        else:
            src = ins[w].at[pl.ds(pl.multiple_of((4 * px + 2 * py + pc) * sr, 16), sr)]
        return pltpu.make_async_remote_copy(
            src_ref=src, dst_ref=outs[w].at[k - 1], send_sem=sems[0].at[w, k - 1], recv_sem=sems[1].at[w, k - 1],
            device_id=(px, py, pc), device_id_type=MESH)

    def start(ins, outs, sems):
        for w in range(n):
            for k in range(1, 8):
                copy(ins, outs, sems, w, k).start()

    def finish(ins, outs, sems):
        for w in range(n):
            for k in range(1, 8):
                copy(ins, outs, sems, w, k).wait_recv()
        for w in range(n):
            for k in range(1, 8):
                copy(ins, outs, sems, w, k).wait_send()

    return _Host(ins=list(grads), out_shapes=[jax.ShapeDtypeStruct((7,) + s, g.dtype) for s, g in zip(shapes, grads)],
                 aliases={}, sems=[pltpu.SemaphoreType.DMA((n, 7)), pltpu.SemaphoreType.DMA((n, 7))],
                 start=start, mid=None, finish=finish)


def _sibling_host(bufs):
    n = len(bufs)

    def copy(outs, sems, w, half_of):
        x, y, c = _coords()
        half = outs[w].shape[0] // 2
        rows = outs[w].at[pl.ds(pl.multiple_of(half_of * half, 8), half)]
        return pltpu.make_async_remote_copy(src_ref=rows, dst_ref=rows, send_sem=sems[0].at[w], recv_sem=sems[1].at[w],
                                            device_id=(x, y, 1 - c), device_id_type=MESH)

    def start(_, outs, sems):
        c = lax.axis_index("c")
        for w in range(n):
            copy(outs, sems, w, c).start()

    def finish(_, outs, sems):
        c = lax.axis_index("c")
        for w in range(n):
            copy(outs, sems, w, 1 - c).wait_recv()
        for w in range(n):
            copy(outs, sems, w, c).wait_send()

    return _Host(ins=list(bufs), out_shapes=[jax.ShapeDtypeStruct(b.shape, b.dtype) for b in bufs],
                 aliases={i: i for i in range(n)},
                 sems=[pltpu.SemaphoreType.DMA((n,)), pltpu.SemaphoreType.DMA((n,))], start=start, mid=None, finish=finish)


def _allgather_pack(pack):
    rows = pack.shape[0]

    def body(p_ref, o_ref, send, recv, local):
        x, y, c = _coords()
        me = 4 * x + 2 * y + c
        mine = pltpu.make_async_copy(p_ref, o_ref.at[me], local)
        mine.start()

        def copy(k, slot):
            return pltpu.make_async_remote_copy(src_ref=p_ref, dst_ref=o_ref.at[slot], send_sem=send.at[k - 1],
                                                recv_sem=recv.at[k - 1], device_id=_peer(x, y, c, k), device_id_type=MESH)

        for k in range(1, 8):
            copy(k, me).start()
        for k in range(1, 8):
            px, py, pc = _peer(x, y, c, k)
            copy(k, 4 * px + 2 * py + pc).wait_recv()
        for k in range(1, 8):
            copy(k, me).wait_send()
        mine.wait()

    return pl.pallas_call(
        body, name="allgather_pack", in_specs=[pl.BlockSpec(memory_space=pltpu.VMEM)], out_specs=ANY,
        out_shape=jax.ShapeDtypeStruct((8, rows, 128), F32),
        scratch_shapes=[pltpu.SemaphoreType.DMA((7,)), pltpu.SemaphoreType.DMA((7,)), pltpu.SemaphoreType.DMA],
        compiler_params=pltpu.CompilerParams(vmem_limit_bytes=VMEM_LIMIT),
    )(pack)


SMALL = ("norm1_g", "conv_b", "rg_wa", "rg_ba", "rg_wx", "rg_bx", "rg_lambda", "hg_lb_logits", "hg_norm_g",
         "norm2_g", "norm_f_g")
SHARDED_SMALL = ("meta_tokens", "conv_w")
SEG = 1024


def _pack(vals):
    parts = []
    for k in SMALL + SHARDED_SMALL:
        v = vals[k].reshape(-1).astype(F32)
        parts.append(jnp.pad(v, (0, -v.shape[0] % SEG)))
    return jnp.concatenate(parts).reshape(-1, 128)


def _unpack(flat, shapes):
    out, off = {}, 0
    flat = flat.reshape(-1)
    for k in SMALL + SHARDED_SMALL:
        size = 1
        for s in shapes[k]:
            size *= s
        out[k] = flat[off:off + size].reshape(shapes[k])
        off += size + (-size % SEG)
    return out


ORDER = ("meta_tokens", "norm1_g", "w_in", "conv_w", "conv_b", "rg_wa", "rg_ba", "rg_wx", "rg_bx", "rg_lambda",
         "hg_lb_logits", "hg_norm_g", "w_proj_a", "w_proj_b", "w_out", "norm2_g", "w_ffn_in", "w_ffn_down", "norm_f_g")


def kernel(x, meta_tokens, norm1_g, w_in, conv_w, conv_b, rg_wa, rg_ba, rg_wx, rg_bx, rg_lambda, hg_lb_logits, hg_norm_g, w_proj_a, w_proj_b, w_out, norm2_g, w_ffn_in, w_ffn_down, norm_f_g, loss_target, m_meta_tokens, m_norm1_g, m_w_in, m_conv_w, m_conv_b, m_rg_wa, m_rg_ba, m_rg_wx, m_rg_bx, m_rg_lambda, m_hg_lb_logits, m_hg_norm_g, m_w_proj_a, m_w_proj_b, m_w_out, m_norm2_g, m_w_ffn_in, m_w_ffn_down, m_norm_f_g, v_meta_tokens, v_norm1_g, v_w_in, v_conv_w, v_conv_b, v_rg_wa, v_rg_ba, v_rg_wx, v_rg_bx, v_rg_lambda, v_hg_lb_logits, v_hg_norm_g, v_w_proj_a, v_w_proj_b, v_w_out, v_norm2_g, v_w_ffn_in, v_w_ffn_down, v_norm_f_g):
    args = dict(locals())
    w = {k: args[k] for k in ORDER}
    m = {k: args["m_" + k] for k in ORDER}
    v = {k: args["v_" + k] for k in ORDER}
    xi, yi, ci = _coords()
    chip = 2 * xi + yi
    slot = jnp.reshape(chip, (1,)).astype(jnp.int32)
    xyc = jnp.stack([xi, yi, ci]).astype(jnp.int32)

    def full_matrix(k, gth):
        if k in COL_SHARDED:
            return jnp.transpose(gth, (1, 0, 2)).reshape(gth.shape[1], -1)
        return gth.reshape(-1, gth.shape[2])

    tiny = jnp.zeros((32, 384), F32)
    tiny = tiny.at[0:N_META, 0:256].set(meta_tokens).at[N_META:N_META + 4, 0:320].set(conv_w[0])
    slots = {k: _place_own(w[k][0], slot, BF16, "place_" + k) for k in BIG}
    w_in_all, tiny_all = _run_host(_gather_host([slots["w_in"], _place_own(tiny, slot, F32, "place_tiny")]),
                                   "allgather_w_in")
    later = [k for k in BIG if k != "w_in"]
    meta_full = jnp.transpose(tiny_all[:, 0:N_META, 0:256], (1, 0, 2)).reshape(N_META, D)
    conv_w_full = jnp.transpose(tiny_all[:, N_META:N_META + 4, 0:320], (1, 0, 2)).reshape(4, RGW)

    small = dict(norm1_g=norm1_g, conv_w=conv_w_full, conv_b=conv_b, rg_wa=rg_wa[0], rg_ba=rg_ba, rg_wx=rg_wx[0],
                 rg_bx=rg_bx, rg_lambda=rg_lambda, hg_lb_logits=hg_lb_logits, hg_norm_g=hg_norm_g,
                 norm2_g=norm2_g, norm_f_g=norm_f_g.reshape(1, D))

    hosts = {
        "mm_proj": lambda g: _gather_host([slots[k] for k in later]),
        "mm_d_act": lambda g: _exchange_host(["w_ffn_down"], [g["w_ffn_down"]]),
        "mm_d_hn2": lambda g: _exchange_host(["w_ffn_in"], [g["w_ffn_in"]]),
        "mm_d_win": lambda g: _exchange_host(["w_out", "w_proj_a", "w_proj_b"], [g["w_out"], g["w_proj_a"], g["w_proj_b"]]),
        "mm_d_hn1": lambda g: _exchange_host(["w_in"], [_unrotate_g_in(g["w_in"])]),
    }
    h0 = jnp.concatenate([jnp.zeros((PAD, D), F32), meta_full, x[0]], axis=0)
    loss_blk, grad_x, g, carried = _local_step(
        h0, loss_target[0], {"w_in": _rotate_w_in(full_matrix("w_in", w_in_all))}, small, hosts=hosts,
        finalize=lambda gathered: {k: full_matrix(k, gth) for k, gth in zip(later, gathered)})
    loss = lax.psum(loss_blk[0, 0], ("x", "y", "c"))

    halves = {}
    sources = {"mm_d_act": ["w_ffn_down"], "mm_d_hn2": ["w_ffn_in"], "mm_d_win": ["w_out", "w_proj_a", "w_proj_b"],
               "mm_d_hn1": ["w_in"]}
    for name, keys in sources.items():
        partials, received = carried[name]
        for k, part, rec in zip(keys, partials, received):
            halves[k] = _sum_place(rec, part, k in COL_SHARDED, xyc, "sum_" + k)
    g_big = dict(zip(BIG, _run_host(_sibling_host([halves[k] for k in BIG]), "sibling_swap")))
    small_shapes = {k: w[k].shape for k in SMALL}
    small_shapes["meta_tokens"], small_shapes["conv_w"] = (N_META, D), (1, 4, RGW)
    recv_pack = _allgather_pack(_pack(g))

    w_sm = {k: w[k] for k in SMALL}
    m_sm = {k: m[k] for k in SMALL}
    v_sm = {k: v[k] for k in SMALL}
    for d in (w_sm, m_sm, v_sm):
        d["meta_tokens"], d["conv_w"] = jnp.zeros((N_META, D), F32), jnp.zeros((1, 4, RGW), F32)
    outs_sm = [_unpack(o, small_shapes) for o in _adamw_small(recv_pack, _pack(w_sm), _pack(m_sm), _pack(v_sm))]
    grad, delta, new_m, new_v = ({k: o[k] for k in SMALL} for o in outs_sm)
    g_small_full = outs_sm[0]
    g_shard = {"meta_tokens": lax.dynamic_slice(g_small_full["meta_tokens"], (0, chip * 256), (N_META, 256)),
               "conv_w": lax.dynamic_slice(g_small_full["conv_w"], (0, 0, chip * 320), (1, 4, 320))[0]}
    for k in BIG + SHARDED_SMALL:
        gk = g_big[k] if k in BIG else g_shard[k]
        wk, mk, vk = (d[k].reshape(gk.shape) for d in (w, m, v))
        dk, mk, vk = _adamw(wk, gk, mk, vk, "adamw_" + k)
        grad[k], delta[k], new_m[k], new_v[k] = (a.reshape(w[k].shape) for a in (gk, dk, mk, vk))

    return (loss, grad_x[None], *[grad[k] for k in ORDER], *[delta[k] for k in ORDER],
            *[new_m[k] for k in ORDER], *[new_v[k] for k in ORDER])
```

```python
import functools

import jax
import jax.numpy as jnp
from jax import lax
from jax.experimental import pallas as pl
from jax.experimental.pallas import tpu as pltpu

F32, BF16 = jnp.float32, jnp.bfloat16
D = 1024
N_META = 16
RGW = 1280
RG_BLOCKS, RG_BLOCK = 16, 80
RG_C = 8.0
HEADS, HD = 8, 128
HGW = HEADS * HD
DFF = 2816
D_IN = 2 * RGW + 6 * D
ROT = 2 * RGW
COL_PAD = 256
D_INP = D_IN + COL_PAD
EPS = 1e-6
CH = 64
TM = 256
ROW0 = TM
PAD = ROW0 - N_META
CT = RGW
NCT = RGW // CT
EXP_CLAMP = 80.0
VMEM_LIMIT = 56 * 1024 * 1024

LR, B1, B2, ADAM_EPS, WD, STEP = 0.001, 0.9, 0.999, 1e-08, 0.01, 10
MESH = pl.DeviceIdType.MESH
ANY = pl.BlockSpec(memory_space=pl.ANY)


def _cparams(sem):
    return pltpu.CompilerParams(dimension_semantics=sem, vmem_limit_bytes=VMEM_LIMIT)


def _pick(n, prefs):
    for p in prefs:
        if n % p == 0:
            return p
    return n


def _sig(x):
    return 0.5 * jnp.tanh(0.5 * x) + 0.5


def _dot(a, b, dims, precision=None):
    return lax.dot_general(a, b, (dims, ((), ())), preferred_element_type=F32, precision=precision)


NN, NT, TN_ = ((1,), (0,)), ((1,), (1,)), ((0,), (0,))


class _Host:
    def __init__(self, ins, out_shapes, aliases, sems, start, mid, finish):
        self.ins, self.out_shapes, self.aliases, self.sems = ins, out_shapes, aliases, sems
        self.start, self.mid, self.finish = start, mid, finish


class _Epi:
    def __init__(self, ins, in_specs, out_shapes, out_specs, fn):
        self.ins, self.in_specs, self.out_shapes, self.out_specs, self.fn = ins, in_specs, out_shapes, out_specs, fn


def _mm(a, b, mode, out_dtype, name, resid=None, host=None, epi=None, tiles=None):
    if mode == "nn":
        (m, kd), n = a.shape, b.shape[1]
    elif mode == "nt":
        (m, kd), n = a.shape, b.shape[0]
    else:
        (kd, m), n = a.shape, b.shape[1]
    if mode == "tn":
        tm = _pick(m, (1024, 1280, 1408, 640, 512, 256, 128))
        tk = _pick(kd, (1408, 768, 512, 256, 128))
    else:
        tm = _pick(m, (768, 512, 640, 256, 128))
        tk = kd if kd <= 2816 else _pick(kd, (1792, 1408, 1024, 512, 256, 128))
    tn = _pick(n, (1792, 1408, 1280, 1024, 512, 256, 128))
    if tiles is not None:
        tm, tn = tiles
    assert m % tm == 0 and n % tn == 0 and kd % tk == 0, (name, m, n, kd, tm, tn, tk)
    nk = kd // tk
    dims = {"nn": NN, "nt": NT, "tn": TN_}[mode]

    ni, nj = m // tm, n // tn
    n_hin = len(host.ins) if host else 0
    n_hout = len(host.out_shapes) if host else 0
    n_res = 0 if resid is None else 1
    n_ein = len(epi.ins) if epi else 0
    n_out = len(epi.out_shapes) if epi else 1

    def finish(r, r_ref, e_in, o_refs):
        if resid is not None:
            r = r + r_ref[...]
        if epi:
            epi.fn(r, e_in, o_refs)
        else:
            o_refs[0][...] = r.astype(out_dtype)

    def body(*refs):
        a_ref, b_ref = refs[:2]
        r_ref = refs[2] if resid is not None else None
        pos = 2 + n_res
        e_in = refs[pos:pos + n_ein]
        pos += n_ein
        h_in = refs[pos:pos + n_hin]
        pos += n_hin
        o_ref = refs[pos:pos + n_out]
        pos += n_out
        h_out = refs[pos:pos + n_hout]
        scratch = refs[pos + n_hout:]
        h_sems = scratch[(1 if nk > 1 else 0):]
        i, j, k = pl.program_id(0), pl.program_id(1), pl.program_id(2)
        if host:
            @pl.when((i == 0) & (j == 0) & (k == 0))
            def _():
                host.start(h_in, h_out, h_sems)

            if host.mid is not None:
                @pl.when((i == ni // 2) & (j == 0) & (k == 0))
                def _():
                    host.mid(h_in, h_out, h_sems)

        if nk == 1:
            finish(_dot(a_ref[...], b_ref[...], dims), r_ref, e_in, o_ref)
        else:
            acc = scratch[0]

            @pl.when(k == 0)
            def _():
                acc[...] = jnp.zeros_like(acc)

            acc[...] += _dot(a_ref[...], b_ref[...], dims)

            @pl.when(k == nk - 1)
            def _():
                finish(acc[...], r_ref, e_in, o_ref)

        if host:
            @pl.when((i == ni - 1) & (j == nj - 1) & (k == nk - 1))
            def _():
                host.finish(h_in, h_out, h_sems)

    if mode == "tn":
        a_spec = pl.BlockSpec((tk, tm), lambda i, j, k: (k, i))
    else:
        a_spec = pl.BlockSpec((tm, tk), lambda i, j, k: (i, k))
    if mode == "nt":
        b_spec = pl.BlockSpec((tn, tk), lambda i, j, k: (j, k))
    else:
        b_spec = pl.BlockSpec((tk, tn), lambda i, j, k: (k, j))
    o_spec = pl.BlockSpec((tm, tn), lambda i, j, k: (i, j))
    in_specs, args = [a_spec, b_spec], [a, b]
    if resid is not None:
        in_specs.append(o_spec)
        args.append(resid)
    out_shapes, out_specs = [jax.ShapeDtypeStruct((m, n), out_dtype)], [o_spec]
    if epi:
        in_specs += list(epi.in_specs)
        args += list(epi.ins)
        out_shapes, out_specs = list(epi.out_shapes), list(epi.out_specs)
    scratch = [pltpu.VMEM((tm, tn), F32)] if nk > 1 else []
    if not host:
        outs = pl.pallas_call(
            body, name=name, grid=(ni, nj, nk), in_specs=in_specs, out_specs=out_specs, out_shape=out_shapes,
            scratch_shapes=scratch, compiler_params=_cparams(("parallel", "parallel", "arbitrary")),
        )(*args)
        return list(outs) if epi else outs[0]
    outs = pl.pallas_call(
        body, name=name, grid=(ni, nj, nk),
        in_specs=in_specs + [ANY] * n_hin, out_specs=out_specs + [ANY] * n_hout,
        out_shape=out_shapes + list(host.out_shapes),
        scratch_shapes=scratch + list(host.sems),
        input_output_aliases={2 + n_res + n_ein + i_in: n_out + i_out for i_in, i_out in host.aliases.items()},
        compiler_params=_cparams(("arbitrary", "arbitrary", "arbitrary")),
    )(*args, *host.ins)
    return (list(outs[:n_out]) if epi else outs[0]), list(outs[n_out:])


def _rms(x):
    r = lax.rsqrt(jnp.mean(x * x, axis=-1, keepdims=True) + EPS)
    return x * r, r


def _rmsnorm_fwd(h, g, name):
    t = h.shape[0]

    def body(h_ref, g_ref, o_ref):
        xh, _ = _rms(h_ref[...])
        o_ref[...] = (xh * g_ref[...]).astype(BF16)

    return pl.pallas_call(
        body, name=name, grid=(t // TM,),
        in_specs=[pl.BlockSpec((TM, D), lambda i: (i, 0)), pl.BlockSpec((1, D), lambda i: (0, 0))],
        out_specs=pl.BlockSpec((TM, D), lambda i: (i, 0)),
        out_shape=jax.ShapeDtypeStruct((t, D), BF16),
        compiler_params=_cparams(("parallel",)),
    )(h, g)


def _rms_bwd_math(dhn, x, g):
    xh, r = _rms(x)
    dxh = dhn * g
    dx = r * (dxh - xh * jnp.mean(dxh * xh, axis=-1, keepdims=True))
    return dx, jnp.sum(dhn * xh, axis=0, keepdims=True)


def _final_loss(h2, gf, target):
    t = h2.shape[0]

    def body(h_ref, g_ref, t_ref, dh_ref, dhb_ref, loss_ref, dg_ref):
        i = pl.program_id(0)

        @pl.when(i == 0)
        def _():
            loss_ref[...] = jnp.zeros_like(loss_ref)
            dg_ref[...] = jnp.zeros_like(dg_ref)
            dh_ref[...] = jnp.zeros_like(dh_ref)
            dhb_ref[...] = jnp.zeros_like(dhb_ref)

        @pl.when(i > 0)
        def _():
            x = h_ref[...]
            g = g_ref[...]
            xh, _ = _rms(x)
            err = xh * g - t_ref[...]
            loss_ref[...] += 0.5 * jnp.sum(jnp.sum(err * err, axis=-1, keepdims=True) * (1.0 / D))
            dx, dg = _rms_bwd_math(err * (1.0 / D), x, g)
            dh_ref[...] = dx
            dhb_ref[...] = dx.astype(BF16)
            dg_ref[...] += dg

    row = pl.BlockSpec((TM, D), lambda i: (i, 0))
    return pl.pallas_call(
        body, name="final_loss", grid=(t // TM,),
        in_specs=[row, pl.BlockSpec((1, D), lambda i: (0, 0)),
                  pl.BlockSpec((TM, D), lambda i: (jnp.maximum(i - 1, 0), 0))],
        out_specs=[row, row, pl.BlockSpec((8, 128), lambda i: (0, 0)), pl.BlockSpec((1, D), lambda i: (0, 0))],
        out_shape=[jax.ShapeDtypeStruct((t, D), F32), jax.ShapeDtypeStruct((t, D), BF16),
                   jax.ShapeDtypeStruct((8, 128), F32), jax.ShapeDtypeStruct((1, D), F32)],
        compiler_params=_cparams(("arbitrary",)),
    )(h2, gf, target)


def _rmsnorm_bwd(dhn, h, g, dres, name):
    t = h.shape[0]

    def body(dhn_ref, h_ref, g_ref, dres_ref, dh_ref, dhb_ref, dg_ref):
        @pl.when(pl.program_id(0) == 0)
        def _():
            dg_ref[...] = jnp.zeros_like(dg_ref)

        dx, dg = _rms_bwd_math(dhn_ref[...], h_ref[...], g_ref[...])
        dh = dres_ref[...] + dx
        dh_ref[...] = dh
        dhb_ref[...] = dh.astype(BF16)
        dg_ref[...] += dg

    row = pl.BlockSpec((TM, D), lambda i: (i, 0))
    vec = pl.BlockSpec((1, D), lambda i: (0, 0))
    return pl.pallas_call(
        body, name=name, grid=(t // TM,),
        in_specs=[row, row, vec, row], out_specs=[row, row, vec],
        out_shape=[jax.ShapeDtypeStruct((t, D), F32), jax.ShapeDtypeStruct((t, D), BF16),
                   jax.ShapeDtypeStruct((1, D), F32)],
        compiler_params=_cparams(("arbitrary",)),
    )(dhn, h, g, dres)


def _rmsnorm1_bwd(dhn, h, g, dres):
    t = h.shape[0]

    def body(dhn_ref, h_ref, g_ref, dres_ref, dx_ref, dmeta_ref, dg_ref):
        i = pl.program_id(0)

        @pl.when(i == 0)
        def _():
            dg_ref[...] = jnp.zeros_like(dg_ref)

        dx, dg = _rms_bwd_math(dhn_ref[...], h_ref[...], g_ref[...])
        dh = dres_ref[...] + dx
        dg_ref[...] += dg
        dx_ref[...] = dh

        @pl.when(i == 0)
        def _():
            dmeta_ref[...] = dh[PAD:, :]

    row = pl.BlockSpec((TM, D), lambda i: (i, 0))
    vec = pl.BlockSpec((1, D), lambda i: (0, 0))
    return pl.pallas_call(
        body, name="rmsnorm1_bwd", grid=(t // TM,),
        in_specs=[row, row, vec, row],
        out_specs=[pl.BlockSpec((TM, D), lambda i: (jnp.maximum(i - 1, 0), 0)),
                   pl.BlockSpec((N_META, D), lambda i: (0, 0)), vec],
        out_shape=[jax.ShapeDtypeStruct((t - ROW0, D), F32), jax.ShapeDtypeStruct((N_META, D), F32),
                   jax.ShapeDtypeStruct((1, D), F32)],
        compiler_params=_cparams(("arbitrary",)),
    )(dhn, h, g, dres)


FH = DFF // 2
FM = 384


def _swiglu_epi(t):
    def fn(r, _, outs):
        g, u = r[:, :FH], r[:, FH:]
        outs[0][...] = r.astype(BF16)
        outs[1][...] = (g * _sig(g) * u).astype(BF16)

    return _Epi(ins=[], in_specs=[],
                out_shapes=[jax.ShapeDtypeStruct((t, 2 * DFF), BF16), jax.ShapeDtypeStruct((t, DFF), BF16)],
                out_specs=[pl.BlockSpec((FM, 2 * FH), lambda i, j, k: (i, j)), pl.BlockSpec((FM, FH), lambda i, j, k: (i, j))],
                fn=fn)


def _swiglu_bwd_epi(gu):
    def fn(d, ins, outs):
        gu_t = ins[0][...].astype(F32)
        g, u = gu_t[:, :FH], gu_t[:, FH:]
        s = _sig(g)
        outs[0][:, :FH] = (d * u * (s * (1.0 + g * (1.0 - s)))).astype(BF16)
        outs[0][:, FH:] = (d * (g * s)).astype(BF16)

    spec = pl.BlockSpec((FM, 2 * FH), lambda i, j, k: (i, j))
    return _Epi(ins=[gu], in_specs=[spec], out_shapes=[jax.ShapeDtypeStruct(gu.shape, BF16)], out_specs=[spec], fn=fn)


COL_GA, COL_GB = 4, 5
COL_AX, COL_AG = (6 * D + COL_PAD) // CT, (6 * D + COL_PAD + RGW) // CT


def _merge_fwd(pa, pb, proj):
    t = pa.shape[0]

    def body(pa_ref, pb_ref, ga_ref, gb_ref, o_ref):
        o_ref[...] = (_sig(ga_ref[...]) * pa_ref[...] + _sig(gb_ref[...]) * pb_ref[...]).astype(BF16)

    row = pl.BlockSpec((TM, D), lambda i: (i, 0))
    return pl.pallas_call(
        body, name="merge_fwd", grid=(t // TM,),
        in_specs=[row, row, pl.BlockSpec((TM, D), lambda i: (i, COL_GA)), pl.BlockSpec((TM, D), lambda i: (i, COL_GB))],
        out_specs=row, out_shape=jax.ShapeDtypeStruct((t, D), BF16),
        compiler_params=_cparams(("parallel",)),
    )(pa, pb, proj, proj)


def _merge_bwd(dm, pa, pb, proj):
    t = pa.shape[0]

    def body(dm_ref, pa_ref, pb_ref, ga_ref, gb_ref, dpa_ref, dpb_ref, dproj_ref):
        d = dm_ref[...]
        sa, sb = _sig(ga_ref[...]), _sig(gb_ref[...])
        dpa_ref[...] = (sa * d).astype(BF16)
        dpb_ref[...] = (sb * d).astype(BF16)
        dproj_ref[:, :D] = (d * pa_ref[...] * sa * (1.0 - sa)).astype(BF16)
        dproj_ref[:, D:] = (d * pb_ref[...] * sb * (1.0 - sb)).astype(BF16)

    row = pl.BlockSpec((TM, D), lambda i: (i, 0))
    return pl.pallas_call(
        body, name="merge_bwd", grid=(t // TM,),
        in_specs=[row, row, row, pl.BlockSpec((TM, D), lambda i: (i, COL_GA)), pl.BlockSpec((TM, D), lambda i: (i, COL_GB))],
        out_specs=[row, row, pl.BlockSpec((TM, 2 * D), lambda i: (i, 2))],
        out_shape=[jax.ShapeDtypeStruct((t, D), BF16), jax.ShapeDtypeStruct((t, D), BF16),
                   jax.ShapeDtypeStruct((t, D_INP), BF16)],
        compiler_params=_cparams(("parallel",)),
    )(dm, pa, pb, proj, proj)


def _zero_pad_cols(dproj):
    t = dproj.shape[0]
    tz = _pick(t, (768, 256))

    def body(_, o_ref):
        o_ref[...] = jnp.zeros_like(o_ref)

    return pl.pallas_call(
        body, name="dproj_pad", grid=(t // tz,), in_specs=[ANY],
        out_specs=pl.BlockSpec((tz, COL_PAD), lambda i: (i, (6 * D) // COL_PAD)),
        out_shape=jax.ShapeDtypeStruct(dproj.shape, BF16), input_output_aliases={0: 0},
        compiler_params=_cparams(("parallel",)),
    )(dproj)


def _prev_halo(col0):
    return pl.BlockSpec((8, CT), lambda c, i: (jnp.maximum(i * (TM // 8) - 1, 0), col0 + c))


def _conv_fwd(proj, w, b):
    t = proj.shape[0]

    def body(x_ref, halo_ref, w_ref, b_ref, o_ref, ob_ref, buf):
        i = pl.program_id(1)
        buf[0:8, :] = jnp.where(i > 0, halo_ref[...], 0.0)
        buf[8:, :] = x_ref[...]
        wv = w_ref[...]
        y = b_ref[...] + wv[3:4, :] * x_ref[...]
        for j in range(3):
            y = y + wv[j:j + 1, :] * buf[pl.ds(5 + j, TM), :]
        o_ref[...] = y
        ob_ref[...] = y.astype(BF16)

    blk = pl.BlockSpec((TM, CT), lambda c, i: (i, c))
    return pl.pallas_call(
        body, name="conv_fwd", grid=(NCT, t // TM),
        in_specs=[pl.BlockSpec((TM, CT), lambda c, i: (i, COL_AX + c)), _prev_halo(COL_AX),
                  pl.BlockSpec((4, CT), lambda c, i: (0, c)), pl.BlockSpec((1, CT), lambda c, i: (0, c))],
        out_specs=[blk, blk],
        out_shape=[jax.ShapeDtypeStruct((t, RGW), F32), jax.ShapeDtypeStruct((t, RGW), BF16)],
        scratch_shapes=[pltpu.VMEM((TM + 8, CT), F32)],
        compiler_params=_cparams(("parallel", "parallel")),
    )(proj, proj, w, b)


def _conv_bwd(dxc, proj, w, dproj):
    t = proj.shape[0]
    nt = t // TM

    def body(d_ref, dn_ref, x_ref, halo_ref, w_ref, _, dx_ref, dw_ref, db_ref, bufd, bufx):
        i = pl.program_id(1)

        @pl.when(i == 0)
        def _():
            dw_ref[...] = jnp.zeros_like(dw_ref)
            db_ref[...] = jnp.zeros_like(db_ref)

        d = d_ref[...]
        bufd[0:TM, :] = d
        bufd[TM:, :] = jnp.where(i < nt - 1, dn_ref[...], 0.0)
        bufx[0:8, :] = jnp.where(i > 0, halo_ref[...], 0.0)
        bufx[8:, :] = x_ref[...]
        wv = w_ref[...]
        dx = wv[3:4, :] * d
        for j in range(3):
            dx = dx + wv[j:j + 1, :] * bufd[pl.ds(3 - j, TM), :]
            dw_ref[j:j + 1, :] += jnp.sum(d * bufx[pl.ds(5 + j, TM), :], axis=0, keepdims=True)
        dw_ref[3:4, :] += jnp.sum(d * x_ref[...], axis=0, keepdims=True)
        db_ref[...] += jnp.sum(d, axis=0, keepdims=True)
        dx_ref[...] = dx.astype(BF16)

    return pl.pallas_call(
        body, name="conv_bwd", grid=(NCT, nt),
        in_specs=[pl.BlockSpec((TM, CT), lambda c, i: (i, c)),
                  pl.BlockSpec((8, CT), lambda c, i: (jnp.minimum((i + 1) * (TM // 8), t // 8 - 1), c)),
                  pl.BlockSpec((TM, CT), lambda c, i: (i, COL_AX + c)), _prev_halo(COL_AX),
                  pl.BlockSpec((4, CT), lambda c, i: (0, c)), ANY],
        out_specs=[pl.BlockSpec((TM, CT), lambda c, i: (i, COL_AX + c)),
                   pl.BlockSpec((4, CT), lambda c, i: (0, c)), pl.BlockSpec((1, CT), lambda c, i: (0, c))],
        out_shape=[jax.ShapeDtypeStruct((t, D_INP), BF16), jax.ShapeDtypeStruct((4, RGW), F32),
                   jax.ShapeDtypeStruct((1, RGW), F32)],
        scratch_shapes=[pltpu.VMEM((TM + 8, CT), F32), pltpu.VMEM((TM + 8, CT), F32)],
        input_output_aliases={5: 0},
        compiler_params=_cparams(("parallel", "arbitrary")),
    )(dxc, dxc, proj, proj, w, dproj)


def _gelu(x):
    c = 0.7978845608028654
    th = jnp.tanh(c * (x + 0.044715 * x * x * x))
    return 0.5 * x * (1.0 + th), th


def _rg_gates(gr, gi, xc, ba, bx, lam, row0):
    r = _sig(gr + ba)
    ig = _sig(gi + bx)
    sp = jax.nn.softplus(-lam)
    a = jnp.exp(-RG_C * r * sp)
    s2 = jnp.maximum(1.0 - a * a, 1e-30)
    rs = lax.rsqrt(s2)
    s = s2 * rs
    rows = row0 + lax.broadcasted_iota(jnp.int32, gr.shape, 0)
    live = rows >= PAD
    u = jnp.where(live, s * ig * xc, 0.0)
    return r, ig, sp, a, s, rs, u, live


def _rg_fwd(gri, xc, proj, ba, bx, lam):
    t = gri.shape[0]

    def body(gr_ref, gi_ref, xc_ref, ag_ref, ba_ref, bx_ref, lam_ref, h_ref, ya_ref, a_s, u_s, hc):
        i = pl.program_id(1)

        @pl.when(i == 0)
        def _():
            hc[...] = jnp.zeros_like(hc)

        _, _, _, a, _, _, u, _ = _rg_gates(gr_ref[...], gi_ref[...], xc_ref[...], ba_ref[...], bx_ref[...],
                                           lam_ref[...], i * TM)
        a_s[...] = a
        u_s[...] = u
        sub = lax.broadcasted_iota(jnp.int32, (8, CT), 0)

        def blk(k, h):
            off = pl.multiple_of(k * 8, 8)
            av, uv = a_s[pl.ds(off, 8), :], u_s[pl.ds(off, 8), :]
            out = jnp.zeros((8, CT), F32)
            for r in range(8):
                h = av[r:r + 1, :] * h + uv[r:r + 1, :]
                out = jnp.where(sub == r, h, out)
            h_ref[pl.ds(off, 8), :] = out
            return h

        hc[...] = lax.fori_loop(0, TM // 8, blk, hc[...])
        ge, _ = _gelu(ag_ref[...])
        ya_ref[...] = (h_ref[...] * ge).astype(BF16)

    blk_ = pl.BlockSpec((TM, CT), lambda c, i: (i, c))
    vec = pl.BlockSpec((1, CT), lambda c, i: (0, c))
    return pl.pallas_call(
        body, name="rg_fwd", grid=(NCT, t // TM),
        in_specs=[blk_, pl.BlockSpec((TM, CT), lambda c, i: (i, NCT + c)), blk_,
                  pl.BlockSpec((TM, CT), lambda c, i: (i, COL_AG + c)), vec, vec, vec],
        out_specs=[blk_, blk_],
        out_shape=[jax.ShapeDtypeStruct((t, RGW), F32), jax.ShapeDtypeStruct((t, RGW), BF16)],
        scratch_shapes=[pltpu.VMEM((TM, CT), F32), pltpu.VMEM((TM, CT), F32), pltpu.VMEM((1, CT), F32)],
        compiler_params=_cparams(("parallel", "arbitrary")),
    )(gri, gri, xc, proj, ba, bx, lam)


def _rg_bwd(dya, h, gri, xc, proj, ba, bx, lam, dproj):
    t = gri.shape[0]
    nt = t // TM
    assert NCT == 1

    def body(dya_ref, h_ref, hh_ref, gr_ref, gi_ref, xc_ref, ag_ref, ba_ref, bx_ref, lam_ref, _,
             dgri_ref, dxc_ref, dag_ref, dba_ref, dbx_ref, dlam_ref, a_s, d_s, g_s, hbuf, cc):
        i = pl.program_id(1)
        ri = nt - 1 - i

        @pl.when(i == 0)
        def _():
            cc[...] = jnp.zeros_like(cc)
            dba_ref[...] = jnp.zeros_like(dba_ref)
            dbx_ref[...] = jnp.zeros_like(dbx_ref)
            dlam_ref[...] = jnp.zeros_like(dlam_ref)

        xc = xc_ref[...]
        lam = lam_ref[...]
        r, ig, sp, a, s, rs, _, live = _rg_gates(gr_ref[...], gi_ref[...], xc, ba_ref[...], bx_ref[...], lam, ri * TM)
        ag = ag_ref[...]
        ge, th = _gelu(ag)
        dya = dya_ref[...]
        hv = h_ref[...]
        c0 = 0.7978845608028654
        dge = 0.5 * (1.0 + th) + 0.5 * ag * (1.0 - th * th) * c0 * (1.0 + 3.0 * 0.044715 * ag * ag)
        dag_ref[...] = (dya * hv * dge).astype(BF16)
        a_s[...] = a
        d_s[...] = dya * ge
        sub = lax.broadcasted_iota(jnp.int32, (8, CT), 0)

        def blk(k, c):
            off = pl.multiple_of((TM // 8 - 1 - k) * 8, 8)
            av, dv = a_s[pl.ds(off, 8), :], d_s[pl.ds(off, 8), :]
            out = jnp.zeros((8, CT), F32)
            for rr in range(7, -1, -1):
                g = dv[rr:rr + 1, :] + c
                c = av[rr:rr + 1, :] * g
                out = jnp.where(sub == rr, g, out)
            g_s[pl.ds(off, 8), :] = out
            return c

        cc[...] = lax.fori_loop(0, TM // 8, blk, cc[...])
        g = g_s[...]
        hbuf[0:8, :] = jnp.where(ri > 0, hh_ref[...], 0.0)
        hbuf[8:, :] = hv
        hprev = hbuf[pl.ds(7, TM), :]
        du = jnp.where(live, g, 0.0)
        m = ig * xc
        ds = du * m
        dm = du * s
        da = g * hprev - a * ds * rs
        dla = da * a
        dr = dla * (-RG_C * sp) * r * (1.0 - r)
        di = dm * xc * ig * (1.0 - ig)
        dgri_ref[:, :CT] = dr.astype(BF16)
        dgri_ref[:, CT:] = di.astype(BF16)
        dxc_ref[...] = dm * ig
        dba_ref[...] += jnp.sum(dr, axis=0, keepdims=True)
        dbx_ref[...] += jnp.sum(di, axis=0, keepdims=True)
        dlam_ref[...] += jnp.sum(dla * (-RG_C * r), axis=0, keepdims=True) * (-_sig(-lam))

    rblk = pl.BlockSpec((TM, CT), lambda c, i: (nt - 1 - i, c))
    vec = pl.BlockSpec((1, CT), lambda c, i: (0, c))
    hh = pl.BlockSpec((8, CT), lambda c, i: (jnp.maximum((nt - 1 - i) * (TM // 8) - 1, 0), c))
    agb = pl.BlockSpec((TM, CT), lambda c, i: (nt - 1 - i, COL_AG + c))
    return pl.pallas_call(
        body, name="rg_bwd", grid=(NCT, nt),
        in_specs=[rblk, rblk, hh, rblk, pl.BlockSpec((TM, CT), lambda c, i: (nt - 1 - i, NCT + c)), rblk, agb,
                  vec, vec, vec, ANY],
        out_specs=[pl.BlockSpec((TM, 2 * CT), lambda c, i: (nt - 1 - i, c)), rblk, agb, vec, vec, vec],
        out_shape=[jax.ShapeDtypeStruct((t, 2 * RGW), BF16),
                   jax.ShapeDtypeStruct((t, RGW), F32), jax.ShapeDtypeStruct((t, D_INP), BF16),
                   jax.ShapeDtypeStruct((1, RGW), F32), jax.ShapeDtypeStruct((1, RGW), F32),
                   jax.ShapeDtypeStruct((1, RGW), F32)],
        scratch_shapes=[pltpu.VMEM((TM, CT), F32), pltpu.VMEM((TM, CT), F32), pltpu.VMEM((TM, CT), F32),
                        pltpu.VMEM((TM + 8, CT), F32), pltpu.VMEM((1, CT), F32)],
        input_output_aliases={10: 2},
        compiler_params=_cparams(("parallel", "arbitrary")),
    )(dya, h, h, gri, gri, xc, proj, ba, bx, lam, dproj)


NCH = TM // CH
HI = lax.Precision.HIGHEST


def _tri_dot(tri, x):
    hi = x.astype(BF16)
    r1 = x - hi.astype(F32)
    mid = r1.astype(BF16)
    lo = (r1 - mid.astype(F32)).astype(BF16)
    return _dot(tri, hi, NN) + _dot(tri, mid, NN) + _dot(tri, lo, NN)


def _hg_chunk(qr, fr, lb):
    sf = _sig(fr)
    fg = lb + (1.0 - lb) * sf
    k = (1.0 - lb) * (1.0 - sf)
    sq = _sig(qr)
    q = qr * sq
    ri = lax.broadcasted_iota(jnp.int32, (CH, CH), 0)
    ci = lax.broadcasted_iota(jnp.int32, (CH, CH), 1)
    b = _tri_dot((ri >= ci).astype(BF16), jnp.log(fg))
    bm, bl = b[CH // 2 - 1:CH // 2, :], b[CH - 1:CH, :]
    eb = jnp.exp(b)
    ebm = jnp.exp(b - bm)
    ekm = jnp.exp(jnp.minimum(bm - b, EXP_CLAMP))
    ekl = jnp.exp(bl - b)
    return dict(sf=sf, fg=fg, k=k, sq=sq, q=q, eb=eb, ebm=ebm, ekm=ekm, ekl=ekl, ebl=jnp.exp(bl),
                qe=q * eb, qh=q * ebm, kh=k * ekm, kd=k * ekl, causal=ri >= ci, upper=(ci >= ri).astype(BF16))


def _hgrn_fwd(proj, lbl, gn):
    t = proj.shape[0]
    nt = t // TM

    def body(q_ref, f_ref, v_ref, g_ref, lbl_ref, gn_ref, yb_ref, o_ref, st_ref, st):
        @pl.when(pl.program_id(0) == 0)
        def _():
            st[...] = jnp.zeros_like(st)

        l = lbl_ref[...]
        lb = _sig(l[0:1, :] - l[1:2, :])
        gnv = gn_ref[...]

        def chunk(c, carry):
            off = pl.multiple_of(c * CH, CH)
            rows = pl.ds(off, CH)
            z = _hg_chunk(q_ref[rows, :], f_ref[rows, :], lb)
            v, gg = v_ref[rows, :], g_ref[rows, :]
            for hh in range(HEADS):
                sl = slice(hh * HD, (hh + 1) * HD)
                s_prev = st[hh]
                st_ref[c, hh] = s_prev
                vb = v[:, sl].astype(BF16)
                att = jnp.where(z["causal"], _dot(z["qh"][:, sl].astype(BF16), z["kh"][:, sl].astype(BF16), NT), 0.0)
                o = _dot(z["qe"][:, sl].astype(BF16), s_prev.astype(BF16), NT) + _dot(att.astype(BF16), vb, NN)
                st[hh] = s_prev * z["ebl"][:, sl] + _dot(vb, z["kd"][:, sl].astype(BF16), TN_)
                xh, _ = _rms(o)
                gh = gg[:, sl]
                o_ref[rows, sl] = o
                yb_ref[rows, sl] = (xh * gnv * gh * _sig(gh)).astype(BF16)
            return carry

        lax.fori_loop(0, NCH, chunk, 0)

    def col(j):
        return pl.BlockSpec((TM, HGW), lambda i, j=j: (i, j))

    return pl.pallas_call(
        body, name="hgrn_fwd", grid=(nt,),
        in_specs=[col(0), col(1), col(2), col(3), pl.BlockSpec((2, HGW), lambda i: (0, 0)),
                  pl.BlockSpec((1, HD), lambda i: (0, 0))],
        out_specs=[col(0), col(0), pl.BlockSpec((NCH, HEADS, HD, HD), lambda i: (i, 0, 0, 0))],
        out_shape=[jax.ShapeDtypeStruct((t, HGW), BF16), jax.ShapeDtypeStruct((t, HGW), F32),
                   jax.ShapeDtypeStruct((t // CH, HEADS, HD, HD), F32)],
        scratch_shapes=[pltpu.VMEM((HEADS, HD, HD), F32)],
        compiler_params=_cparams(("arbitrary",)),
    )(proj, proj, proj, proj, lbl, gn)


def _hgrn_bwd(dyb, proj, o, states, lbl, gn, dproj):
    t = proj.shape[0]
    nt = t // TM

    def body(dy_ref, q_ref, f_ref, v_ref, g_ref, o_ref, st_ref, lbl_ref, gn_ref, _,
             dp_ref, dgn_ref, dl_ref, dst, dlb):
        i = pl.program_id(0)

        @pl.when(i == 0)
        def _():
            dst[...] = jnp.zeros_like(dst)
            dlb[...] = jnp.zeros_like(dlb)
            dgn_ref[...] = jnp.zeros_like(dgn_ref)

        l = lbl_ref[...]
        lb = _sig(l[0:1, :] - l[1:2, :])
        gnv = gn_ref[...]
        last = lax.broadcasted_iota(jnp.int32, (CH, HD), 0) == CH - 1

        def chunk(cc, carry):
            c = NCH - 1 - cc
            off = pl.multiple_of(c * CH, CH)
            rows = pl.ds(off, CH)
            qr, fr = q_ref[rows, :], f_ref[rows, :]
            z = _hg_chunk(qr, fr, lb)
            v, gg, ov, dy = v_ref[rows, :], g_ref[rows, :], o_ref[rows, :], dy_ref[rows, :]
            dqs, dks, dbs, dvs, dgs = [], [], [], [], []
            dgn = jnp.zeros((1, HD), F32)
            for hh in range(HEADS):
                sl = slice(hh * HD, (hh + 1) * HD)
                s_prev, ds_new = st_ref[c, hh], dst[hh]
                qe, qh, kh, kd = z["qe"][:, sl], z["qh"][:, sl], z["kh"][:, sl], z["kd"][:, sl]
                ebl = z["ebl"][:, sl]
                gh, dyh = gg[:, sl], dy[:, sl]
                xh, rr = _rms(ov[:, sl])
                sg = _sig(gh)
                dyn = dyh * (gh * sg)
                dgs.append(dyh * (xh * gnv) * (sg * (1.0 + gh * (1.0 - sg))))
                dgn = dgn + jnp.sum(dyn * xh, axis=0, keepdims=True)
                dxh = dyn * gnv
                do = (rr * (dxh - xh * jnp.mean(dxh * xh, axis=-1, keepdims=True))).astype(BF16)
                vb, dsb = v[:, sl].astype(BF16), ds_new.astype(BF16)
                qeb, qhb, khb, kdb = (a.astype(BF16) for a in (qe, qh, kh, kd))
                att = jnp.where(z["causal"], _dot(qhb, khb, NT), 0.0).astype(BF16)
                datt = jnp.where(z["causal"], _dot(do, vb, NT), 0.0).astype(BF16)
                dvs.append(_dot(att, do, TN_) + _dot(kdb, dsb, NT))
                dqe = _dot(do, s_prev.astype(BF16), NN)
                dqh = _dot(datt, khb, NN)
                dkh = _dot(datt, qhb, TN_)
                dkd = _dot(vb, dsb, NN)
                qe, qh, kh, kd = (a.astype(F32) for a in (qeb, qhb, khb, kdb))
                dbl = (jnp.sum(dkd * kd, axis=0, keepdims=True)
                       + jnp.sum(ds_new * s_prev, axis=0, keepdims=True) * ebl)
                dqs.append(dqe * z["eb"][:, sl] + dqh * z["ebm"][:, sl])
                dks.append(dkh * z["ekm"][:, sl] + dkd * z["ekl"][:, sl])
                dbs.append(dqe * qe + dqh * qh - dkh * kh - dkd * kd + jnp.where(last, dbl, 0.0))
                dst[hh] = _dot(do, qeb, TN_) + ds_new * ebl
            dgn_ref[...] += dgn
            dq, dk, db = (jnp.concatenate(x, axis=1) for x in (dqs, dks, dbs))
            dlf = _tri_dot(z["upper"], db)
            sf, fg, sq = z["sf"], z["fg"], z["sq"]
            dmix = dlf / fg - dk
            dsf = dmix * (1.0 - lb)
            dlb[...] += jnp.sum(dmix * (1.0 - sf), axis=0, keepdims=True)
            dp_ref[rows, 0:HGW] = (dq * (sq * (1.0 + qr * (1.0 - sq)))).astype(BF16)
            dp_ref[rows, HGW:2 * HGW] = (dsf * sf * (1.0 - sf)).astype(BF16)
            dp_ref[rows, 2 * HGW:3 * HGW] = jnp.concatenate(dvs, axis=1).astype(BF16)
            dp_ref[rows, 3 * HGW:4 * HGW] = jnp.concatenate(dgs, axis=1).astype(BF16)
            return carry

        lax.fori_loop(0, NCH, chunk, 0)
        dl0 = dlb[...] * lb * (1.0 - lb)
        dl_ref[0:1, :] = dl0
        dl_ref[1:2, :] = -dl0

    def col(j):
        return pl.BlockSpec((TM, HGW), lambda i, j=j: (nt - 1 - i, j))

    return pl.pallas_call(
        body, name="hgrn_bwd", grid=(nt,),
        in_specs=[col(0), col(0), col(1), col(2), col(3), col(0),
                  pl.BlockSpec((NCH, HEADS, HD, HD), lambda i: (nt - 1 - i, 0, 0, 0)),
                  pl.BlockSpec((2, HGW), lambda i: (0, 0)), pl.BlockSpec((1, HD), lambda i: (0, 0)), ANY],
        out_specs=[pl.BlockSpec((TM, 4 * HGW), lambda i: (nt - 1 - i, 0)),
                   pl.BlockSpec((1, HD), lambda i: (0, 0)), pl.BlockSpec((2, HGW), lambda i: (0, 0))],
        out_shape=[jax.ShapeDtypeStruct((t, D_INP), BF16), jax.ShapeDtypeStruct((1, HD), F32),
                   jax.ShapeDtypeStruct((2, HGW), F32)],
        scratch_shapes=[pltpu.VMEM((HEADS, HD, HD), F32), pltpu.VMEM((1, HGW), F32)],
        input_output_aliases={9: 0},
        compiler_params=_cparams(("arbitrary",)),
    )(dyb, proj, proj, proj, proj, o, states, lbl, gn, dproj)


def _rotate_w_in(w):
    return jnp.concatenate([w[:, ROT:], jnp.zeros((w.shape[0], COL_PAD), w.dtype), w[:, :ROT]], axis=1)


def _unrotate_g_in(g):
    return jnp.concatenate([g[:, D_INP - ROT:], g[:, :D_IN - ROT]], axis=1)


def _block_diag(w):
    eye = jnp.eye(RG_BLOCKS, dtype=w.dtype)
    return (w[:, :, None, :] * eye[:, None, :, None]).reshape(RGW, RGW)


def _diag_blocks(wd):
    w4 = wd.reshape(RG_BLOCKS, RG_BLOCK, RG_BLOCKS, RG_BLOCK)
    return jnp.stack([w4[n, :, n, :] for n in range(RG_BLOCKS)])


def _local_step(h0, target, wts, small, hosts=None, finalize=None):
    hosts = hosts or {}
    carried = {}
    g = {}

    def mm(a, b, mode, out_dtype, name, **kw):
        if name not in hosts:
            return _mm(a, b, mode, out_dtype, name, **kw)
        host = hosts[name](g)
        out, res = _mm(a, b, mode, out_dtype, name, host=host, **kw)
        carried[name] = (host.ins, res)
        return out

    w_in = wts["w_in"]
    wax_d = jnp.concatenate([_block_diag(small["rg_wa"]), _block_diag(small["rg_wx"])], axis=1).astype(BF16)
    ba, bx, lam = small["rg_ba"], small["rg_bx"], small["rg_lambda"]
    lbl, gn = small["hg_lb_logits"], small["hg_norm_g"]
    conv_w, conv_b = small["conv_w"], small["conv_b"]

    hn1 = _rmsnorm_fwd(h0, small["norm1_g"], "rmsnorm1")
    proj = mm(hn1, w_in, "nn", F32, "mm_proj")
    if finalize is not None:
        wts = finalize(carried["mm_proj"][1])
    w_pa, w_pb, w_out, w_fi, w_fd = (wts[k] for k in ("w_proj_a", "w_proj_b", "w_out", "w_ffn_in", "w_ffn_down"))
    xc, xcb = _conv_fwd(proj, conv_w, conv_b)
    gri = mm(xcb, wax_d, "nn", F32, "mm_rg_gates")
    hrg, ya = _rg_fwd(gri, xc, proj, ba, bx, lam)
    yb, o, states = _hgrn_fwd(proj, lbl, gn)
    pa = mm(ya, w_pa, "nn", F32, "mm_pa")
    pb = mm(yb, w_pb, "nn", F32, "mm_pb")
    merged = _merge_fwd(pa, pb, proj)
    h1 = mm(merged, w_out, "nn", F32, "mm_out", resid=h0)
    hn2 = _rmsnorm_fwd(h1, small["norm2_g"], "rmsnorm2")
    gu, act = mm(hn2, w_fi, "nn", BF16, "mm_ffn_in", epi=_swiglu_epi(h0.shape[0]), tiles=(FM, 2 * FH))
    h2 = mm(act, w_fd, "nn", F32, "mm_ffn_down", resid=h1)
    dh2, dh2b, loss, g["norm_f_g"] = _final_loss(h2, small["norm_f_g"], target)

    g["w_ffn_down"] = mm(act, dh2b, "tn", BF16, "mm_d_wfd")
    (dgu,) = mm(dh2b, w_fd, "nt", BF16, "mm_d_act", epi=_swiglu_bwd_epi(gu), tiles=(FM, FH))
    g["w_ffn_in"] = mm(hn2, dgu, "tn", BF16, "mm_d_wfi")
    dhn2 = mm(dgu, w_fi, "nt", F32, "mm_d_hn2")
    dh1, dh1b, g["norm2_g"] = _rmsnorm_bwd(dhn2, h1, small["norm2_g"], dh2, "rmsnorm2_bwd")
    g["w_out"] = mm(merged, dh1b, "tn", BF16, "mm_d_wout")
    dmerged = mm(dh1b, w_out, "nt", F32, "mm_d_merged")
    dpa, dpb, dproj = _merge_bwd(dmerged, pa, pb, proj)
    dproj = _zero_pad_cols(dproj)
    g["w_proj_a"] = mm(ya, dpa, "tn", BF16, "mm_d_wpa")
    g["w_proj_b"] = mm(yb, dpb, "tn", BF16, "mm_d_wpb")
    dya = mm(dpa, w_pa, "nt", F32, "mm_d_ya")
    dyb = mm(dpb, w_pb, "nt", F32, "mm_d_yb")
    dproj, g["hg_norm_g"], g["hg_lb_logits"] = _hgrn_bwd(dyb, proj, o, states, lbl, gn, dproj)
    dgri, dxc, dproj, g["rg_ba"], g["rg_bx"], g["rg_lambda"] = _rg_bwd(dya, hrg, gri, xc, proj, ba, bx, lam, dproj)
    dwax = mm(xcb, dgri, "tn", F32, "mm_d_wax")
    g["rg_wa"], g["rg_wx"] = _diag_blocks(dwax[:, :RGW]), _diag_blocks(dwax[:, RGW:])
    dxc = mm(dgri, wax_d, "nt", F32, "mm_d_xc", resid=dxc)
    dproj, g["conv_w"], g["conv_b"] = _conv_bwd(dxc, proj, conv_w, dproj)
    g["w_in"] = mm(hn1, dproj, "tn", BF16, "mm_d_win")
    dhn1 = mm(dproj, w_in, "nt", F32, "mm_d_hn1")
    grad_x, g["meta_tokens"], g["norm1_g"] = _rmsnorm1_bwd(dhn1, h0, small["norm1_g"], dh1)
    return loss, grad_x, g, carried


def _adamw_math(w, g, m, v):
    m = B1 * m + (1.0 - B1) * g
    v = B2 * v + (1.0 - B2) * (g * g)
    m_hat = m / (1.0 - B1 ** STEP)
    v_hat = v / (1.0 - B2 ** STEP)
    return -LR * (m_hat / (jnp.sqrt(v_hat) + ADAM_EPS) + WD * w), m, v


def _adamw(w, g, m, v, name):
    r, c = w.shape
    tr = _pick(r, (256, 352, 320, 128, 64, 32, 16, 8))

    def body(w_ref, g_ref, m_ref, v_ref, d_ref, mo_ref, vo_ref):
        d_ref[...], mo_ref[...], vo_ref[...] = _adamw_math(w_ref[...], g_ref[...], m_ref[...], v_ref[...])

    blk = pl.BlockSpec((tr, c), lambda i: (i, 0))
    return pl.pallas_call(
        body, name=name, grid=(r // tr,), in_specs=[blk] * 4, out_specs=[blk] * 3,
        out_shape=[jax.ShapeDtypeStruct((r, c), F32)] * 3,
        compiler_params=_cparams(("parallel",)),
    )(w, g, m, v)


def _adamw_small(recv, w, m, v, name):
    rows = w.shape[0]
    tr = rows

    def body(r_ref, w_ref, m_ref, v_ref, g_ref, d_ref, mo_ref, vo_ref):
        g = r_ref[0]
        for p in range(1, 8):
            g = g + r_ref[p]
        g_ref[...] = g
        d_ref[...], mo_ref[...], vo_ref[...] = _adamw_math(w_ref[...], g, m_ref[...], v_ref[...])

    blk = pl.BlockSpec((tr, 128), lambda i: (i, 0))
    return pl.pallas_call(
        body, name=name, grid=(rows // tr,),
        in_specs=[pl.BlockSpec((8, tr, 128), lambda i: (0, i, 0)), blk, blk, blk], out_specs=[blk] * 4,
        out_shape=[jax.ShapeDtypeStruct((rows, 128), F32)] * 4,
        compiler_params=_cparams(("parallel",)),
    )(recv, w, m, v)


def _col_block(weight, px, py):
    return 2 * py + px if weight == "w_ffn_in" else 2 * px + py


def _sum_place(recv, gfull, weight, xyc, name):
    _, r, c = recv.shape
    tr = _pick(r, (128, 176, 160, 64, 32, 16))
    nb = r // tr

    def body(_, own_ref, r_ref, o_ref):
        g = own_ref[...].astype(F32)
        for p in range(7):
            g = g + r_ref[p].astype(F32)
        o_ref[...] = g

    if weight in COL_SHARDED:
        own = pl.BlockSpec((tr, c), lambda i, s: (s[2] * nb + i, _col_block(weight, s[0], s[1])))
    else:
        own = pl.BlockSpec((tr, c), lambda i, s: ((4 * s[0] + 2 * s[1] + s[2]) * nb + i, 0))
    return pl.pallas_call(
        body, name=name,
        grid_spec=pltpu.PrefetchScalarGridSpec(
            num_scalar_prefetch=1, grid=(nb,),
            in_specs=[own, pl.BlockSpec((7, tr, c), lambda i, s: (0, i, 0))],
            out_specs=pl.BlockSpec((tr, c), lambda i, s: (s[2] * nb + i, 0))),
        out_shape=jax.ShapeDtypeStruct((2 * r, c), F32),
        compiler_params=_cparams(("arbitrary",)),
    )(xyc, gfull, recv)


def _place_own(shard, slot, dtype, name):
    r, c = shard.shape
    tr = _pick(r, (256, 352, 320, 128, 32))

    def body(_, x_ref, o_ref):
        o_ref[...] = x_ref[...].astype(dtype)

    return pl.pallas_call(
        body, name=name,
        grid_spec=pltpu.PrefetchScalarGridSpec(
            num_scalar_prefetch=1, grid=(r // tr,),
            in_specs=[pl.BlockSpec((tr, c), lambda i, s: (i, 0))],
            out_specs=pl.BlockSpec((None, tr, c), lambda i, s: (s[0], i, 0))),
        out_shape=jax.ShapeDtypeStruct((4, r, c), dtype),
        compiler_params=_cparams(("arbitrary",)),
    )(slot, shard)


BIG = ("w_in", "w_proj_a", "w_proj_b", "w_out", "w_ffn_in", "w_ffn_down")
COL_SHARDED = ("w_in", "w_ffn_in")


def _coords():
    return lax.axis_index("x"), lax.axis_index("y"), lax.axis_index("c")


def _gather_host(bufs):
    n = len(bufs)

    def place():
        x, y, c = _coords()
        return x, y, c, [(1 - x, y), (x, 1 - y), (1 - x, 1 - y)]

    def slab(b, chip, half_of):
        half = b.shape[1] // 2
        return b.at[chip, pl.ds(pl.multiple_of(half_of * half, 8), half)]

    def copy(b, sems, w, k, chip, half_of, to):
        s = slab(b, chip, half_of)
        return pltpu.make_async_remote_copy(src_ref=s, dst_ref=s, send_sem=sems[0].at[w, k], recv_sem=sems[1].at[w, k],
                                            device_id=to, device_id_type=MESH)

    def start(_, outs, sems):
        x, y, c, chips = place()
        for w, b in enumerate(outs):
            for k, (cx, cy) in enumerate(chips):
                copy(b, sems, w, k, 2 * x + y, c, (cx, cy, c)).start()

    def mid(_, outs, sems):
        x, y, c, chips = place()
        for w, b in enumerate(outs):
            for k, (cx, cy) in enumerate(chips):
                copy(b, sems, w, k, 2 * cx + cy, c, (cx, cy, c)).wait_recv()
                copy(b, sems, w, 3 + k, 2 * cx + cy, c, (x, y, 1 - c)).start()

    def finish(_, outs, sems):
        x, y, c, chips = place()
        for w, b in enumerate(outs):
            for k, (cx, cy) in enumerate(chips):
                copy(b, sems, w, 3 + k, 2 * cx + cy, 1 - c, (x, y, 1 - c)).wait_recv()
        for w, b in enumerate(outs):
            for k, (cx, cy) in enumerate(chips):
                copy(b, sems, w, k, 2 * x + y, c, (cx, cy, c)).wait_send()
                copy(b, sems, w, 3 + k, 2 * cx + cy, c, (x, y, 1 - c)).wait_send()

    return _Host(ins=list(bufs), out_shapes=[jax.ShapeDtypeStruct(b.shape, b.dtype) for b in bufs],
                 aliases={i: i for i in range(n)},
                 sems=[pltpu.SemaphoreType.DMA((n, 6)), pltpu.SemaphoreType.DMA((n, 6))],
                 start=start, mid=mid, finish=finish)


def _run_host(host, name):
    n_in, n_out = len(host.ins), len(host.out_shapes)

    def body(*refs):
        ins, outs, sems = refs[:n_in], refs[n_in:n_in + n_out], refs[n_in + n_out:]
        host.start(ins, outs, sems)
        if host.mid is not None:
            host.mid(ins, outs, sems)
        host.finish(ins, outs, sems)

    return pl.pallas_call(
        body, name=name, in_specs=[ANY] * n_in, out_specs=[ANY] * n_out, out_shape=list(host.out_shapes),
        scratch_shapes=list(host.sems), input_output_aliases=dict(host.aliases),
    )(*host.ins)


def _peer(x, y, c, k):
    fx, fy, fc = (k >> 2) & 1, (k >> 1) & 1, k & 1
    return (x ^ fx if fx else x, y ^ fy if fy else y, c ^ fc if fc else c)


def _sub_shape(name, full_shape):
    r, c = full_shape
    return (r // 2, c // 4) if name in COL_SHARDED else (r // 8, c)


def _exchange_host(names, grads):
    n = len(names)
    shapes = [_sub_shape(k, g.shape) for k, g in zip(names, grads)]

    def copy(ins, outs, sems, w, k):
        x, y, c = _coords()
        px, py, pc = _peer(x, y, c, k)
        sr, sc = shapes[w]
        if names[w] in COL_SHARDED:
            col = _col_block(names[w], px, py) * sc
            src = ins[w].at[pl.ds(pl.multiple_of(pc * sr, 16), sr), pl.ds(pl.multiple_of(col, 128), sc)]
        else:
            src = ins[w].at[pl.ds(pl.multiple_of((4 * px + 2 * py + pc) * sr, 16), sr)]
        return pltpu.make_async_remote_copy(
            src_ref=src, dst_ref=outs[w].at[k - 1], send_sem=sems[0].at[w, k - 1], recv_sem=sems[1].at[w, k - 1],
            device_id=(px, py, pc), device_id_type=MESH)

    def start(ins, outs, sems):
        for w in range(n):
            for k in range(1, 8):
                copy(ins, outs, sems, w, k).start()

    def finish(ins, outs, sems):
        for w in range(n):
            for k in range(1, 8):
                copy(ins, outs, sems, w, k).wait_recv()
        for w in range(n):
            for k in range(1, 8):
                copy(ins, outs, sems, w, k).wait_send()

    return _Host(ins=list(grads), out_shapes=[jax.ShapeDtypeStruct((7,) + s, g.dtype) for s, g in zip(shapes, grads)],
                 aliases={}, sems=[pltpu.SemaphoreType.DMA((n, 7)), pltpu.SemaphoreType.DMA((n, 7))],
                 start=start, mid=None, finish=finish)


def _sibling_host(bufs):
    n = len(bufs)

    def copy(outs, sems, w, half_of):
        x, y, c = _coords()
        half = outs[w].shape[0] // 2
        rows = outs[w].at[pl.ds(pl.multiple_of(half_of * half, 8), half)]
        return pltpu.make_async_remote_copy(src_ref=rows, dst_ref=rows, send_sem=sems[0].at[w], recv_sem=sems[1].at[w],
                                            device_id=(x, y, 1 - c), device_id_type=MESH)

    def start(_, outs, sems):
        c = lax.axis_index("c")
        for w in range(n):
            copy(outs, sems, w, c).start()

    def finish(_, outs, sems):
        c = lax.axis_index("c")
        for w in range(n):
            copy(outs, sems, w, 1 - c).wait_recv()
        for w in range(n):
            copy(outs, sems, w, c).wait_send()

    return _Host(ins=list(bufs), out_shapes=[jax.ShapeDtypeStruct(b.shape, b.dtype) for b in bufs],
                 aliases={i: i for i in range(n)},
                 sems=[pltpu.SemaphoreType.DMA((n,)), pltpu.SemaphoreType.DMA((n,))], start=start, mid=None, finish=finish)


def _pack_host(pack):
    rows = pack.shape[0]

    def me_of():
        x, y, c = _coords()
        return x, y, c, 4 * x + 2 * y + c

    def copy(ins, outs, sems, k, slot):
        x, y, c, _ = me_of()
        return pltpu.make_async_remote_copy(src_ref=ins[0], dst_ref=outs[0].at[slot], send_sem=sems[0].at[k - 1],
                                            recv_sem=sems[1].at[k - 1], device_id=_peer(x, y, c, k), device_id_type=MESH)

    def start(ins, outs, sems):
        me = me_of()[3]
        pltpu.make_async_copy(ins[0], outs[0].at[me], sems[2]).start()
        for k in range(1, 8):
            copy(ins, outs, sems, k, me).start()

    def finish(ins, outs, sems):
        x, y, c, me = me_of()
        for k in range(1, 8):
            px, py, pc = _peer(x, y, c, k)
            copy(ins, outs, sems, k, 4 * px + 2 * py + pc).wait_recv()
        for k in range(1, 8):
            copy(ins, outs, sems, k, me).wait_send()
        pltpu.make_async_copy(ins[0], outs[0].at[me], sems[2]).wait()

    return _Host(ins=[pack], out_shapes=[jax.ShapeDtypeStruct((8, rows, 128), F32)], aliases={},
                 sems=[pltpu.SemaphoreType.DMA((7,)), pltpu.SemaphoreType.DMA((7,)), pltpu.SemaphoreType.DMA],
                 start=start, mid=None, finish=finish)


def _join_hosts(hosts):
    ins, outs, sems, aliases, spans = [], [], [], {}, []
    for h in hosts:
        spans.append((len(ins), len(h.ins), len(outs), len(h.out_shapes), len(sems), len(h.sems)))
        for i_in, i_out in h.aliases.items():
            aliases[len(ins) + i_in] = len(outs) + i_out
        ins, outs, sems = ins + list(h.ins), outs + list(h.out_shapes), sems + list(h.sems)

    def phase(which):
        fns = [getattr(h, which) for h in hosts]
        if all(f is None for f in fns):
            return None

        def run(i_refs, o_refs, s_refs):
            for f, (i0, n_i, o0, n_o, s0, n_s) in zip(fns, spans):
                if f is not None:
                    f(i_refs[i0:i0 + n_i], o_refs[o0:o0 + n_o], s_refs[s0:s0 + n_s])

        return run

    return _Host(ins, outs, aliases, sems, phase("start"), phase("mid"), phase("finish"))


def _allgather_pack(pack):
    rows = pack.shape[0]

    def body(p_ref, o_ref, send, recv, local):
        x, y, c = _coords()
        me = 4 * x + 2 * y + c
        mine = pltpu.make_async_copy(p_ref, o_ref.at[me], local)
        mine.start()

        def copy(k, slot):
            return pltpu.make_async_remote_copy(src_ref=p_ref, dst_ref=o_ref.at[slot], send_sem=send.at[k - 1],
                                                recv_sem=recv.at[k - 1], device_id=_peer(x, y, c, k), device_id_type=MESH)

        for k in range(1, 8):
            copy(k, me).start()
        for k in range(1, 8):
            px, py, pc = _peer(x, y, c, k)
            copy(k, 4 * px + 2 * py + pc).wait_recv()
        for k in range(1, 8):
            copy(k, me).wait_send()
        mine.wait()

    return pl.pallas_call(
        body, name="allgather_pack", in_specs=[pl.BlockSpec(memory_space=pltpu.VMEM)], out_specs=ANY,
        out_shape=jax.ShapeDtypeStruct((8, rows, 128), F32),
        scratch_shapes=[pltpu.SemaphoreType.DMA((7,)), pltpu.SemaphoreType.DMA((7,)), pltpu.SemaphoreType.DMA],
        compiler_params=pltpu.CompilerParams(vmem_limit_bytes=VMEM_LIMIT),
    )(pack)


SMALL = ("norm1_g", "conv_b", "rg_wa", "rg_ba", "rg_wx", "rg_bx", "rg_lambda", "hg_lb_logits", "hg_norm_g",
         "norm2_g", "norm_f_g")
SHARDED_SMALL = ("meta_tokens", "conv_w")
SEG = 1024


PACK_LATE = ("norm1_g", "meta_tokens")
PACK_EARLY = tuple(k for k in SMALL + SHARDED_SMALL if k not in PACK_LATE)


def _pack(vals, names):
    parts = []
    for k in names:
        v = vals[k].reshape(-1).astype(F32)
        parts.append(jnp.pad(v, (0, -v.shape[0] % SEG)))
    return jnp.concatenate(parts).reshape(-1, 128)


def _unpack(flat, shapes, names):
    out, off = {}, 0
    flat = flat.reshape(-1)
    for k in names:
        size = 1
        for s in shapes[k]:
            size *= s
        out[k] = flat[off:off + size].reshape(shapes[k])
        off += size + (-size % SEG)
    return out


ORDER = ("meta_tokens", "norm1_g", "w_in", "conv_w", "conv_b", "rg_wa", "rg_ba", "rg_wx", "rg_bx", "rg_lambda",
         "hg_lb_logits", "hg_norm_g", "w_proj_a", "w_proj_b", "w_out", "norm2_g", "w_ffn_in", "w_ffn_down", "norm_f_g")


def kernel(x, meta_tokens, norm1_g, w_in, conv_w, conv_b, rg_wa, rg_ba, rg_wx, rg_bx, rg_lambda, hg_lb_logits, hg_norm_g, w_proj_a, w_proj_b, w_out, norm2_g, w_ffn_in, w_ffn_down, norm_f_g, loss_target, m_meta_tokens, m_norm1_g, m_w_in, m_conv_w, m_conv_b, m_rg_wa, m_rg_ba, m_rg_wx, m_rg_bx, m_rg_lambda, m_hg_lb_logits, m_hg_norm_g, m_w_proj_a, m_w_proj_b, m_w_out, m_norm2_g, m_w_ffn_in, m_w_ffn_down, m_norm_f_g, v_meta_tokens, v_norm1_g, v_w_in, v_conv_w, v_conv_b, v_rg_wa, v_rg_ba, v_rg_wx, v_rg_bx, v_rg_lambda, v_hg_lb_logits, v_hg_norm_g, v_w_proj_a, v_w_proj_b, v_w_out, v_norm2_g, v_w_ffn_in, v_w_ffn_down, v_norm_f_g):
    args = dict(locals())
    w = {k: args[k] for k in ORDER}
    m = {k: args["m_" + k] for k in ORDER}
    v = {k: args["v_" + k] for k in ORDER}
    xi, yi, ci = _coords()
    chip = 2 * xi + yi
    slot = jnp.reshape(chip, (1,)).astype(jnp.int32)
    xyc = jnp.stack([xi, yi, ci]).astype(jnp.int32)

    def full_matrix(k, gth):
        if k == "w_ffn_in":
            return jnp.concatenate([gth[0], gth[2], gth[1], gth[3]], axis=1)
        if k in COL_SHARDED:
            return jnp.concatenate([gth[j] for j in range(4)], axis=1)
        return gth.reshape(-1, gth.shape[2])

    tiny = jnp.zeros((32, 384), F32)
    tiny = tiny.at[0:N_META, 0:256].set(meta_tokens).at[N_META:N_META + 4, 0:320].set(conv_w[0])
    slots = {k: _place_own(w[k][0], slot, BF16, "place_" + k) for k in BIG}
    w_in_all, tiny_all = _run_host(_gather_host([slots["w_in"], _place_own(tiny, slot, F32, "place_tiny")]),
                                   "allgather_w_in")
    later = [k for k in BIG if k != "w_in"]
    meta_full = jnp.transpose(tiny_all[:, 0:N_META, 0:256], (1, 0, 2)).reshape(N_META, D)
    conv_w_full = jnp.transpose(tiny_all[:, N_META:N_META + 4, 0:320], (1, 0, 2)).reshape(4, RGW)

    small = dict(norm1_g=norm1_g, conv_w=conv_w_full, conv_b=conv_b, rg_wa=rg_wa[0], rg_ba=rg_ba, rg_wx=rg_wx[0],
                 rg_bx=rg_bx, rg_lambda=rg_lambda, hg_lb_logits=hg_lb_logits, hg_norm_g=hg_norm_g,
                 norm2_g=norm2_g, norm_f_g=norm_f_g.reshape(1, D))

    hosts = {
        "mm_proj": lambda g: _gather_host([slots[k] for k in later]),
        "mm_d_act": lambda g: _exchange_host(["w_ffn_down"], [g["w_ffn_down"]]),
        "mm_d_hn2": lambda g: _exchange_host(["w_ffn_in"], [g["w_ffn_in"]]),
        "mm_d_win": lambda g: _exchange_host(["w_out", "w_proj_a", "w_proj_b"], [g["w_out"], g["w_proj_a"], g["w_proj_b"]]),
        "mm_d_hn1": lambda g: _join_hosts([_exchange_host(["w_in"], [_unrotate_g_in(g["w_in"])]),
                                           _pack_host(_pack(g, PACK_EARLY))]),
    }
    h0 = jnp.concatenate([jnp.zeros((PAD, D), F32), meta_full, x[0]], axis=0)
    loss_blk, grad_x, g, carried = _local_step(
        h0, loss_target[0], {"w_in": _rotate_w_in(full_matrix("w_in", w_in_all))}, small, hosts=hosts,
        finalize=lambda gathered: {k: full_matrix(k, gth) for k, gth in zip(later, gathered)})
    loss = lax.psum(loss_blk[0, 0], ("x", "y", "c"))

    halves = {}
    sources = {"mm_d_act": ["w_ffn_down"], "mm_d_hn2": ["w_ffn_in"], "mm_d_win": ["w_out", "w_proj_a", "w_proj_b"],
               "mm_d_hn1": ["w_in"]}
    for name, keys in sources.items():
        partials, received = carried[name]
        for k, part, rec in zip(keys, partials, received):
            halves[k] = _sum_place(rec, part, k, xyc, "sum_" + k)
    g_big = dict(zip(BIG, _run_host(_sibling_host([halves[k] for k in BIG]), "sibling_swap")))
    small_shapes = {k: w[k].shape for k in SMALL}
    small_shapes["meta_tokens"], small_shapes["conv_w"] = (N_META, D), (1, 4, RGW)
    recv_packs = {PACK_EARLY: carried["mm_d_hn1"][1][1], PACK_LATE: _allgather_pack(_pack(g, PACK_LATE))}

    w_sm = {k: w[k] for k in SMALL}
    m_sm = {k: m[k] for k in SMALL}
    v_sm = {k: v[k] for k in SMALL}
    for d in (w_sm, m_sm, v_sm):
        d["meta_tokens"], d["conv_w"] = jnp.zeros((N_META, D), F32), jnp.zeros((1, 4, RGW), F32)
    outs_sm = [{}, {}, {}, {}]
    for names, rec in recv_packs.items():
        res = _adamw_small(rec, _pack(w_sm, names), _pack(m_sm, names), _pack(v_sm, names), "adamw_" + names[0])
        for acc, o in zip(outs_sm, res):
            acc.update(_unpack(o, small_shapes, names))
    grad, delta, new_m, new_v = ({k: o[k] for k in SMALL} for o in outs_sm)
    g_small_full = outs_sm[0]
    g_shard = {"meta_tokens": lax.dynamic_slice(g_small_full["meta_tokens"], (0, chip * 256), (N_META, 256)),
               "conv_w": lax.dynamic_slice(g_small_full["conv_w"], (0, 0, chip * 320), (1, 4, 320))[0]}
    for k in BIG + SHARDED_SMALL:
        gk = g_big[k] if k in BIG else g_shard[k]
        wk, mk, vk = (d[k].reshape(gk.shape) for d in (w, m, v))
        dk, mk, vk = _adamw(wk, gk, mk, vk, "adamw_" + k)
        grad[k], delta[k], new_m[k], new_v[k] = (a.reshape(w[k].shape) for a in (gk, dk, mk, vk))

    return (loss, grad_x[None], *[grad[k] for k in ORDER], *[delta[k] for k in ORDER],
            *[new_m[k] for k in ORDER], *[new_v[k] for k in ORDER])
```

```python
import functools

import jax
import jax.numpy as jnp
from jax import lax
from jax.experimental import pallas as pl
from jax.experimental.pallas import tpu as pltpu

F32, BF16 = jnp.float32, jnp.bfloat16
D = 1024
N_META = 16
RGW = 1280
RG_BLOCKS, RG_BLOCK = 16, 80
RG_C = 8.0
HEADS, HD = 8, 128
HGW = HEADS * HD
DFF = 2816
D_IN = 2 * RGW + 6 * D
ROT = 2 * RGW
COL_PAD = 256
D_INP = D_IN + COL_PAD
EPS = 1e-6
CH = 64
TM = 256
ROW0 = TM
PAD = ROW0 - N_META
CT = RGW
RS = 16
NCT = RGW // CT
EXP_CLAMP = 80.0
VMEM_LIMIT = 56 * 1024 * 1024

LR, B1, B2, ADAM_EPS, WD, STEP = 0.001, 0.9, 0.999, 1e-08, 0.01, 10
MESH = pl.DeviceIdType.MESH
ANY = pl.BlockSpec(memory_space=pl.ANY)


def _cparams(sem):
    return pltpu.CompilerParams(dimension_semantics=sem, vmem_limit_bytes=VMEM_LIMIT)


def _pick(n, prefs):
    for p in prefs:
        if n % p == 0:
            return p
    return n


def _sig(x):
    return 0.5 * jnp.tanh(0.5 * x) + 0.5


def _dot(a, b, dims, precision=None):
    return lax.dot_general(a, b, (dims, ((), ())), preferred_element_type=F32, precision=precision)


NN, NT, TN_ = ((1,), (0,)), ((1,), (1,)), ((0,), (0,))


class _Host:
    def __init__(self, ins, out_shapes, aliases, sems, start, mid, finish):
        self.ins, self.out_shapes, self.aliases, self.sems = ins, out_shapes, aliases, sems
        self.start, self.mid, self.finish = start, mid, finish


class _Epi:
    def __init__(self, ins, in_specs, out_shapes, out_specs, fn):
        self.ins, self.in_specs, self.out_shapes, self.out_specs, self.fn = ins, in_specs, out_shapes, out_specs, fn


def _mm(a, b, mode, out_dtype, name, resid=None, host=None, epi=None, tiles=None):
    if mode == "nn":
        (m, kd), n = a.shape, b.shape[1]
    elif mode == "nt":
        (m, kd), n = a.shape, b.shape[0]
    else:
        (kd, m), n = a.shape, b.shape[1]
    if mode == "tn":
        tm = _pick(m, (1024, 1280, 1408, 640, 512, 256, 128))
        tk = _pick(kd, (1408, 768, 512, 256, 128))
    else:
        tm = _pick(m, (768, 512, 640, 256, 128))
        tk = kd if kd <= 2816 else _pick(kd, (1792, 1408, 1024, 512, 256, 128))
    tn = _pick(n, (1792, 1408, 1280, 1024, 512, 256, 128))
    if tiles is not None:
        tm, tn = tiles
    assert m % tm == 0 and n % tn == 0 and kd % tk == 0, (name, m, n, kd, tm, tn, tk)
    nk = kd // tk
    dims = {"nn": NN, "nt": NT, "tn": TN_}[mode]

    ni, nj = m // tm, n // tn
    n_hin = len(host.ins) if host else 0
    n_hout = len(host.out_shapes) if host else 0
    n_res = 0 if resid is None else 1
    n_ein = len(epi.ins) if epi else 0
    n_out = len(epi.out_shapes) if epi else 1

    def finish(r, r_ref, e_in, o_refs):
        if resid is not None:
            r = r + r_ref[...]
        if epi:
            epi.fn(r, e_in, o_refs)
        else:
            o_refs[0][...] = r.astype(out_dtype)

    def body(*refs):
        a_ref, b_ref = refs[:2]
        r_ref = refs[2] if resid is not None else None
        pos = 2 + n_res
        e_in = refs[pos:pos + n_ein]
        pos += n_ein
        h_in = refs[pos:pos + n_hin]
        pos += n_hin
        o_ref = refs[pos:pos + n_out]
        pos += n_out
        h_out = refs[pos:pos + n_hout]
        scratch = refs[pos + n_hout:]
        h_sems = scratch[(1 if nk > 1 else 0):]
        i, j, k = pl.program_id(0), pl.program_id(1), pl.program_id(2)
        if host:
            @pl.when((i == 0) & (j == 0) & (k == 0))
            def _():
                host.start(h_in, h_out, h_sems)

            if host.mid is not None:
                @pl.when((i == ni // 2) & (j == 0) & (k == 0))
                def _():
                    host.mid(h_in, h_out, h_sems)

        if nk == 1:
            finish(_dot(a_ref[...], b_ref[...], dims), r_ref, e_in, o_ref)
        else:
            acc = scratch[0]

            @pl.when(k == 0)
            def _():
                acc[...] = jnp.zeros_like(acc)

            acc[...] += _dot(a_ref[...], b_ref[...], dims)

            @pl.when(k == nk - 1)
            def _():
                finish(acc[...], r_ref, e_in, o_ref)

        if host:
            @pl.when((i == ni - 1) & (j == nj - 1) & (k == nk - 1))
            def _():
                host.finish(h_in, h_out, h_sems)

    if mode == "tn":
        a_spec = pl.BlockSpec((tk, tm), lambda i, j, k: (k, i))
    else:
        a_spec = pl.BlockSpec((tm, tk), lambda i, j, k: (i, k))
    if mode == "nt":
        b_spec = pl.BlockSpec((tn, tk), lambda i, j, k: (j, k))
    else:
        b_spec = pl.BlockSpec((tk, tn), lambda i, j, k: (k, j))
    o_spec = pl.BlockSpec((tm, tn), lambda i, j, k: (i, j))
    in_specs, args = [a_spec, b_spec], [a, b]
    if resid is not None:
        in_specs.append(o_spec)
        args.append(resid)
    out_shapes, out_specs = [jax.ShapeDtypeStruct((m, n), out_dtype)], [o_spec]
    if epi:
        in_specs += list(epi.in_specs)
        args += list(epi.ins)
        out_shapes, out_specs = list(epi.out_shapes), list(epi.out_specs)
    scratch = [pltpu.VMEM((tm, tn), F32)] if nk > 1 else []
    if not host:
        outs = pl.pallas_call(
            body, name=name, grid=(ni, nj, nk), in_specs=in_specs, out_specs=out_specs, out_shape=out_shapes,
            scratch_shapes=scratch, compiler_params=_cparams(("parallel", "parallel", "arbitrary")),
        )(*args)
        return list(outs) if epi else outs[0]
    outs = pl.pallas_call(
        body, name=name, grid=(ni, nj, nk),
        in_specs=in_specs + [ANY] * n_hin, out_specs=out_specs + [ANY] * n_hout,
        out_shape=out_shapes + list(host.out_shapes),
        scratch_shapes=scratch + list(host.sems),
        input_output_aliases={2 + n_res + n_ein + i_in: n_out + i_out for i_in, i_out in host.aliases.items()},
        compiler_params=_cparams(("arbitrary", "arbitrary", "arbitrary")),
    )(*args, *host.ins)
    return (list(outs[:n_out]) if epi else outs[0]), list(outs[n_out:])


def _rms(x):
    r = lax.rsqrt(jnp.mean(x * x, axis=-1, keepdims=True) + EPS)
    return x * r, r


def _rmsnorm_fwd(h, g, name):
    t = h.shape[0]

    def body(h_ref, g_ref, o_ref):
        xh, _ = _rms(h_ref[...])
        o_ref[...] = (xh * g_ref[...]).astype(BF16)

    return pl.pallas_call(
        body, name=name, grid=(t // TM,),
        in_specs=[pl.BlockSpec((TM, D), lambda i: (i, 0)), pl.BlockSpec((1, D), lambda i: (0, 0))],
        out_specs=pl.BlockSpec((TM, D), lambda i: (i, 0)),
        out_shape=jax.ShapeDtypeStruct((t, D), BF16),
        compiler_params=_cparams(("parallel",)),
    )(h, g)


def _rms_bwd_math(dhn, x, g):
    xh, r = _rms(x)
    dxh = dhn * g
    dx = r * (dxh - xh * jnp.mean(dxh * xh, axis=-1, keepdims=True))
    return dx, jnp.sum(dhn * xh, axis=0, keepdims=True)


def _final_loss(h2, gf, target):
    t = h2.shape[0]

    def body(h_ref, g_ref, t_ref, dh_ref, dhb_ref, loss_ref, dg_ref):
        i = pl.program_id(0)

        @pl.when(i == 0)
        def _():
            loss_ref[...] = jnp.zeros_like(loss_ref)
            dg_ref[...] = jnp.zeros_like(dg_ref)
            dh_ref[...] = jnp.zeros_like(dh_ref)
            dhb_ref[...] = jnp.zeros_like(dhb_ref)

        @pl.when(i > 0)
        def _():
            x = h_ref[...]
            g = g_ref[...]
            xh, _ = _rms(x)
            err = xh * g - t_ref[...]
            loss_ref[...] += 0.5 * jnp.sum(jnp.sum(err * err, axis=-1, keepdims=True) * (1.0 / D))
            dx, dg = _rms_bwd_math(err * (1.0 / D), x, g)
            dh_ref[...] = dx
            dhb_ref[...] = dx.astype(BF16)
            dg_ref[...] += dg

    row = pl.BlockSpec((TM, D), lambda i: (i, 0))
    return pl.pallas_call(
        body, name="final_loss", grid=(t // TM,),
        in_specs=[row, pl.BlockSpec((1, D), lambda i: (0, 0)),
                  pl.BlockSpec((TM, D), lambda i: (jnp.maximum(i - 1, 0), 0))],
        out_specs=[row, row, pl.BlockSpec((8, 128), lambda i: (0, 0)), pl.BlockSpec((1, D), lambda i: (0, 0))],
        out_shape=[jax.ShapeDtypeStruct((t, D), F32), jax.ShapeDtypeStruct((t, D), BF16),
                   jax.ShapeDtypeStruct((8, 128), F32), jax.ShapeDtypeStruct((1, D), F32)],
        compiler_params=_cparams(("arbitrary",)),
    )(h2, gf, target)


def _rmsnorm_bwd(dhn, h, g, dres, name):
    t = h.shape[0]

    def body(dhn_ref, h_ref, g_ref, dres_ref, dh_ref, dhb_ref, dg_ref):
        @pl.when(pl.program_id(0) == 0)
        def _():
            dg_ref[...] = jnp.zeros_like(dg_ref)

        dx, dg = _rms_bwd_math(dhn_ref[...], h_ref[...], g_ref[...])
        dh = dres_ref[...] + dx
        dh_ref[...] = dh
        dhb_ref[...] = dh.astype(BF16)
        dg_ref[...] += dg

    row = pl.BlockSpec((TM, D), lambda i: (i, 0))
    vec = pl.BlockSpec((1, D), lambda i: (0, 0))
    return pl.pallas_call(
        body, name=name, grid=(t // TM,),
        in_specs=[row, row, vec, row], out_specs=[row, row, vec],
        out_shape=[jax.ShapeDtypeStruct((t, D), F32), jax.ShapeDtypeStruct((t, D), BF16),
                   jax.ShapeDtypeStruct((1, D), F32)],
        compiler_params=_cparams(("arbitrary",)),
    )(dhn, h, g, dres)


def _rmsnorm1_bwd(dhn, h, g, dres):
    t = h.shape[0]

    def body(dhn_ref, h_ref, g_ref, dres_ref, dx_ref, dmeta_ref, dg_ref):
        i = pl.program_id(0)

        @pl.when(i == 0)
        def _():
            dg_ref[...] = jnp.zeros_like(dg_ref)

        dx, dg = _rms_bwd_math(dhn_ref[...], h_ref[...], g_ref[...])
        dh = dres_ref[...] + dx
        dg_ref[...] += dg
        dx_ref[...] = dh

        @pl.when(i == 0)
        def _():
            dmeta_ref[...] = dh[PAD:, :]

    row = pl.BlockSpec((TM, D), lambda i: (i, 0))
    vec = pl.BlockSpec((1, D), lambda i: (0, 0))
    return pl.pallas_call(
        body, name="rmsnorm1_bwd", grid=(t // TM,),
        in_specs=[row, row, vec, row],
        out_specs=[pl.BlockSpec((TM, D), lambda i: (jnp.maximum(i - 1, 0), 0)),
                   pl.BlockSpec((N_META, D), lambda i: (0, 0)), vec],
        out_shape=[jax.ShapeDtypeStruct((t - ROW0, D), F32), jax.ShapeDtypeStruct((N_META, D), F32),
                   jax.ShapeDtypeStruct((1, D), F32)],
        compiler_params=_cparams(("arbitrary",)),
    )(dhn, h, g, dres)


FH = DFF // 2
FM = 384


def _swiglu_epi(t):
    def fn(r, _, outs):
        g, u = r[:, :FH], r[:, FH:]
        outs[0][...] = r.astype(BF16)
        outs[1][...] = (g * _sig(g) * u).astype(BF16)

    return _Epi(ins=[], in_specs=[],
                out_shapes=[jax.ShapeDtypeStruct((t, 2 * DFF), BF16), jax.ShapeDtypeStruct((t, DFF), BF16)],
                out_specs=[pl.BlockSpec((FM, 2 * FH), lambda i, j, k: (i, j)), pl.BlockSpec((FM, FH), lambda i, j, k: (i, j))],
                fn=fn)


def _swiglu_bwd_epi(gu):
    def fn(d, ins, outs):
        gu_t = ins[0][...].astype(F32)
        g, u = gu_t[:, :FH], gu_t[:, FH:]
        s = _sig(g)
        outs[0][:, :FH] = (d * u * (s * (1.0 + g * (1.0 - s)))).astype(BF16)
        outs[0][:, FH:] = (d * (g * s)).astype(BF16)

    spec = pl.BlockSpec((FM, 2 * FH), lambda i, j, k: (i, j))
    return _Epi(ins=[gu], in_specs=[spec], out_shapes=[jax.ShapeDtypeStruct(gu.shape, BF16)], out_specs=[spec], fn=fn)


COL_GA, COL_GB = 4, 5
COL_AX, COL_AG = (6 * D + COL_PAD) // CT, (6 * D + COL_PAD + RGW) // CT


def _merge_fwd(pa, pb, proj):
    t = pa.shape[0]

    def body(pa_ref, pb_ref, ga_ref, gb_ref, o_ref):
        o_ref[...] = (_sig(ga_ref[...]) * pa_ref[...] + _sig(gb_ref[...]) * pb_ref[...]).astype(BF16)

    row = pl.BlockSpec((TM, D), lambda i: (i, 0))
    return pl.pallas_call(
        body, name="merge_fwd", grid=(t // TM,),
        in_specs=[row, row, pl.BlockSpec((TM, D), lambda i: (i, COL_GA)), pl.BlockSpec((TM, D), lambda i: (i, COL_GB))],
        out_specs=row, out_shape=jax.ShapeDtypeStruct((t, D), BF16),
        compiler_params=_cparams(("parallel",)),
    )(pa, pb, proj, proj)


def _merge_bwd(dm, pa, pb, proj):
    t = pa.shape[0]

    def body(dm_ref, pa_ref, pb_ref, ga_ref, gb_ref, dpa_ref, dpb_ref, dproj_ref):
        d = dm_ref[...]
        sa, sb = _sig(ga_ref[...]), _sig(gb_ref[...])
        dpa_ref[...] = (sa * d).astype(BF16)
        dpb_ref[...] = (sb * d).astype(BF16)
        dproj_ref[:, :D] = (d * pa_ref[...] * sa * (1.0 - sa)).astype(BF16)
        dproj_ref[:, D:] = (d * pb_ref[...] * sb * (1.0 - sb)).astype(BF16)

    row = pl.BlockSpec((TM, D), lambda i: (i, 0))
    return pl.pallas_call(
        body, name="merge_bwd", grid=(t // TM,),
        in_specs=[row, row, row, pl.BlockSpec((TM, D), lambda i: (i, COL_GA)), pl.BlockSpec((TM, D), lambda i: (i, COL_GB))],
        out_specs=[row, row, pl.BlockSpec((TM, 2 * D), lambda i: (i, 2))],
        out_shape=[jax.ShapeDtypeStruct((t, D), BF16), jax.ShapeDtypeStruct((t, D), BF16),
                   jax.ShapeDtypeStruct((t, D_INP), BF16)],
        compiler_params=_cparams(("parallel",)),
    )(dm, pa, pb, proj, proj)


def _zero_pad_cols(dproj):
    t = dproj.shape[0]
    tz = _pick(t, (768, 256))

    def body(_, o_ref):
        o_ref[...] = jnp.zeros_like(o_ref)

    return pl.pallas_call(
        body, name="dproj_pad", grid=(t // tz,), in_specs=[ANY],
        out_specs=pl.BlockSpec((tz, COL_PAD), lambda i: (i, (6 * D) // COL_PAD)),
        out_shape=jax.ShapeDtypeStruct(dproj.shape, BF16), input_output_aliases={0: 0},
        compiler_params=_cparams(("parallel",)),
    )(dproj)


def _prev_halo(col0):
    return pl.BlockSpec((8, CT), lambda c, i: (jnp.maximum(i * (TM // 8) - 1, 0), col0 + c))


def _conv_fwd(proj, w, b):
    t = proj.shape[0]

    def body(x_ref, halo_ref, w_ref, b_ref, o_ref, ob_ref, buf):
        i = pl.program_id(1)
        buf[0:8, :] = jnp.where(i > 0, halo_ref[...], 0.0)
        buf[8:, :] = x_ref[...]
        wv = w_ref[...]
        y = b_ref[...] + wv[3:4, :] * x_ref[...]
        for j in range(3):
            y = y + wv[j:j + 1, :] * buf[pl.ds(5 + j, TM), :]
        o_ref[...] = y
        ob_ref[...] = y.astype(BF16)

    blk = pl.BlockSpec((TM, CT), lambda c, i: (i, c))
    return pl.pallas_call(
        body, name="conv_fwd", grid=(NCT, t // TM),
        in_specs=[pl.BlockSpec((TM, CT), lambda c, i: (i, COL_AX + c)), _prev_halo(COL_AX),
                  pl.BlockSpec((4, CT), lambda c, i: (0, c)), pl.BlockSpec((1, CT), lambda c, i: (0, c))],
        out_specs=[blk, blk],
        out_shape=[jax.ShapeDtypeStruct((t, RGW), F32), jax.ShapeDtypeStruct((t, RGW), BF16)],
        scratch_shapes=[pltpu.VMEM((TM + 8, CT), F32)],
        compiler_params=_cparams(("parallel", "parallel")),
    )(proj, proj, w, b)


def _conv_bwd(dxc, proj, w, dproj):
    t = proj.shape[0]
    nt = t // TM

    def body(d_ref, dn_ref, x_ref, halo_ref, w_ref, _, dx_ref, dw_ref, db_ref, bufd, bufx):
        i = pl.program_id(1)

        @pl.when(i == 0)
        def _():
            dw_ref[...] = jnp.zeros_like(dw_ref)
            db_ref[...] = jnp.zeros_like(db_ref)

        d = d_ref[...]
        bufd[0:TM, :] = d
        bufd[TM:, :] = jnp.where(i < nt - 1, dn_ref[...], 0.0)
        bufx[0:8, :] = jnp.where(i > 0, halo_ref[...], 0.0)
        bufx[8:, :] = x_ref[...]
        wv = w_ref[...]
        dx = wv[3:4, :] * d
        for j in range(3):
            dx = dx + wv[j:j + 1, :] * bufd[pl.ds(3 - j, TM), :]
            dw_ref[j:j + 1, :] += jnp.sum(d * bufx[pl.ds(5 + j, TM), :], axis=0, keepdims=True)
        dw_ref[3:4, :] += jnp.sum(d * x_ref[...], axis=0, keepdims=True)
        db_ref[...] += jnp.sum(d, axis=0, keepdims=True)
        dx_ref[...] = dx.astype(BF16)

    return pl.pallas_call(
        body, name="conv_bwd", grid=(NCT, nt),
        in_specs=[pl.BlockSpec((TM, CT), lambda c, i: (i, c)),
                  pl.BlockSpec((8, CT), lambda c, i: (jnp.minimum((i + 1) * (TM // 8), t // 8 - 1), c)),
                  pl.BlockSpec((TM, CT), lambda c, i: (i, COL_AX + c)), _prev_halo(COL_AX),
                  pl.BlockSpec((4, CT), lambda c, i: (0, c)), ANY],
        out_specs=[pl.BlockSpec((TM, CT), lambda c, i: (i, COL_AX + c)),
                   pl.BlockSpec((4, CT), lambda c, i: (0, c)), pl.BlockSpec((1, CT), lambda c, i: (0, c))],
        out_shape=[jax.ShapeDtypeStruct((t, D_INP), BF16), jax.ShapeDtypeStruct((4, RGW), F32),
                   jax.ShapeDtypeStruct((1, RGW), F32)],
        scratch_shapes=[pltpu.VMEM((TM + 8, CT), F32), pltpu.VMEM((TM + 8, CT), F32)],
        input_output_aliases={5: 0},
        compiler_params=_cparams(("parallel", "arbitrary")),
    )(dxc, dxc, proj, proj, w, dproj)


def _gelu(x):
    c = 0.7978845608028654
    th = jnp.tanh(c * (x + 0.044715 * x * x * x))
    return 0.5 * x * (1.0 + th), th


def _rg_gates(gr, gi, xc, ba, bx, lam, row0):
    r = _sig(gr + ba)
    ig = _sig(gi + bx)
    sp = jax.nn.softplus(-lam)
    a = jnp.exp(-RG_C * r * sp)
    s2 = jnp.maximum(1.0 - a * a, 1e-30)
    rs = lax.rsqrt(s2)
    s = s2 * rs
    rows = row0 + lax.broadcasted_iota(jnp.int32, gr.shape, 0)
    live = rows >= PAD
    u = jnp.where(live, s * ig * xc, 0.0)
    return r, ig, sp, a, s, rs, u, live


def _rg_fwd(gri, xc, proj, ba, bx, lam):
    t = gri.shape[0]

    def body(gr_ref, gi_ref, xc_ref, ag_ref, ba_ref, bx_ref, lam_ref, h_ref, ya_ref, hc):
        i = pl.program_id(1)

        @pl.when(i == 0)
        def _():
            hc[...] = jnp.zeros_like(hc)

        ba, bx, lam = ba_ref[...], bx_ref[...], lam_ref[...]
        sub = lax.broadcasted_iota(jnp.int32, (8, CT), 0)

        def strip(k, h):
            off = pl.multiple_of(k * RS, RS)
            rows = pl.ds(off, RS)
            _, _, _, a, _, _, u, _ = _rg_gates(gr_ref[rows, :], gi_ref[rows, :], xc_ref[rows, :], ba, bx, lam,
                                               i * TM + off)
            outs = []
            for half in range(RS // 8):
                out = jnp.zeros((8, CT), F32)
                for r in range(8):
                    h = a[8 * half + r:8 * half + r + 1, :] * h + u[8 * half + r:8 * half + r + 1, :]
                    out = jnp.where(sub == r, h, out)
                outs.append(out)
            hs = jnp.concatenate(outs, axis=0)
            h_ref[rows, :] = hs
            ge, _ = _gelu(ag_ref[rows, :])
            ya_ref[rows, :] = (hs * ge).astype(BF16)
            return h

        hc[...] = lax.fori_loop(0, TM // RS, strip, hc[...])

    blk_ = pl.BlockSpec((TM, CT), lambda c, i: (i, c))
    vec = pl.BlockSpec((1, CT), lambda c, i: (0, c))
    return pl.pallas_call(
        body, name="rg_fwd", grid=(NCT, t // TM),
        in_specs=[blk_, pl.BlockSpec((TM, CT), lambda c, i: (i, NCT + c)), blk_,
                  pl.BlockSpec((TM, CT), lambda c, i: (i, COL_AG + c)), vec, vec, vec],
        out_specs=[blk_, blk_],
        out_shape=[jax.ShapeDtypeStruct((t, RGW), F32), jax.ShapeDtypeStruct((t, RGW), BF16)],
        scratch_shapes=[pltpu.VMEM((1, CT), F32)],
        compiler_params=_cparams(("parallel", "arbitrary")),
    )(gri, gri, xc, proj, ba, bx, lam)


def _rg_bwd(dya, h, gri, xc, proj, ba, bx, lam, dproj):
    t = gri.shape[0]
    nt = t // TM
    assert NCT == 1

    def body(dya_ref, h_ref, hh_ref, gr_ref, gi_ref, xc_ref, ag_ref, ba_ref, bx_ref, lam_ref, _,
             dgri_ref, dxc_ref, dag_ref, dba_ref, dbx_ref, dlam_ref, hbuf, acc, cc):
        i = pl.program_id(1)
        ri = nt - 1 - i

        @pl.when(i == 0)
        def _():
            cc[...] = jnp.zeros_like(cc)
            dba_ref[...] = jnp.zeros_like(dba_ref)
            dbx_ref[...] = jnp.zeros_like(dbx_ref)
            dlam_ref[...] = jnp.zeros_like(dlam_ref)

        acc[...] = jnp.zeros_like(acc)
        ba, bx, lam = ba_ref[...], bx_ref[...], lam_ref[...]
        halo_last = jnp.where(ri > 0, hh_ref[7:8, :], 0.0)
        sub = lax.broadcasted_iota(jnp.int32, (8, CT), 0)
        c0 = 0.7978845608028654

        def strip(kk, c):
            k = TM // RS - 1 - kk
            off = pl.multiple_of(k * RS, RS)
            rows = pl.ds(off, RS)
            xc, ag, dya, hv = xc_ref[rows, :], ag_ref[rows, :], dya_ref[rows, :], h_ref[rows, :]
            r, ig, sp, a, s, rs, _, live = _rg_gates(gr_ref[rows, :], gi_ref[rows, :], xc, ba, bx, lam, ri * TM + off)
            ge, th = _gelu(ag)
            dge = 0.5 * (1.0 + th) + 0.5 * ag * (1.0 - th * th) * c0 * (1.0 + 3.0 * 0.044715 * ag * ag)
            dag_ref[rows, :] = (dya * hv * dge).astype(BF16)
            d = dya * ge
            outs = []
            for half in range(RS // 8 - 1, -1, -1):
                out = jnp.zeros((8, CT), F32)
                for rr in range(7, -1, -1):
                    g = d[8 * half + rr:8 * half + rr + 1, :] + c
                    c = a[8 * half + rr:8 * half + rr + 1, :] * g
                    out = jnp.where(sub == rr, g, out)
                outs.insert(0, out)
            g = jnp.concatenate(outs, axis=0)
            before = h_ref[pl.ds(pl.multiple_of(jnp.maximum(off - RS, 0), RS), RS), :]
            hbuf[7:8, :] = jnp.where(k == 0, halo_last, before[RS - 1:RS, :])
            hbuf[8:, :] = hv
            hprev = hbuf[pl.ds(7, RS), :]
            du = jnp.where(live, g, 0.0)
            ds = du * (ig * xc)
            dm = du * s
            dla = (g * hprev - a * ds * rs) * a
            dr = dla * (-RG_C * sp) * r * (1.0 - r)
            di = dm * xc * ig * (1.0 - ig)
            dgri_ref[rows, :CT] = dr.astype(BF16)
            dgri_ref[rows, CT:] = di.astype(BF16)
            dxc_ref[rows, :] = dm * ig
            dl = dla * (-RG_C * r)
            for half in range(RS // 8):
                part = slice(8 * half, 8 * half + 8)
                acc[0] += dr[part]
                acc[1] += di[part]
                acc[2] += dl[part]
            return c

        cc[...] = lax.fori_loop(0, TM // RS, strip, cc[...])
        dba_ref[...] += jnp.sum(acc[0], axis=0, keepdims=True)
        dbx_ref[...] += jnp.sum(acc[1], axis=0, keepdims=True)
        dlam_ref[...] += jnp.sum(acc[2], axis=0, keepdims=True) * (-_sig(-lam))

    rblk = pl.BlockSpec((TM, CT), lambda c, i: (nt - 1 - i, c))
    vec = pl.BlockSpec((1, CT), lambda c, i: (0, c))
    hh = pl.BlockSpec((8, CT), lambda c, i: (jnp.maximum((nt - 1 - i) * (TM // 8) - 1, 0), c))
    agb = pl.BlockSpec((TM, CT), lambda c, i: (nt - 1 - i, COL_AG + c))
    return pl.pallas_call(
        body, name="rg_bwd", grid=(NCT, nt),
        in_specs=[rblk, rblk, hh, rblk, pl.BlockSpec((TM, CT), lambda c, i: (nt - 1 - i, NCT + c)), rblk, agb,
                  vec, vec, vec, ANY],
        out_specs=[pl.BlockSpec((TM, 2 * CT), lambda c, i: (nt - 1 - i, c)), rblk, agb, vec, vec, vec],
        out_shape=[jax.ShapeDtypeStruct((t, 2 * RGW), BF16),
                   jax.ShapeDtypeStruct((t, RGW), F32), jax.ShapeDtypeStruct((t, D_INP), BF16),
                   jax.ShapeDtypeStruct((1, RGW), F32), jax.ShapeDtypeStruct((1, RGW), F32),
                   jax.ShapeDtypeStruct((1, RGW), F32)],
        scratch_shapes=[pltpu.VMEM((RS + 8, CT), F32), pltpu.VMEM((3, 8, CT), F32), pltpu.VMEM((1, CT), F32)],
        input_output_aliases={10: 2},
        compiler_params=_cparams(("parallel", "arbitrary")),
    )(dya, h, h, gri, gri, xc, proj, ba, bx, lam, dproj)


NCH = TM // CH
HI = lax.Precision.HIGHEST


def _tri_dot(tri, x):
    hi = x.astype(BF16)
    r1 = x - hi.astype(F32)
    mid = r1.astype(BF16)
    lo = (r1 - mid.astype(F32)).astype(BF16)
    return _dot(tri, hi, NN) + _dot(tri, mid, NN) + _dot(tri, lo, NN)


def _hg_chunk(qr, fr, lb):
    sf = _sig(fr)
    fg = lb + (1.0 - lb) * sf
    k = (1.0 - lb) * (1.0 - sf)
    sq = _sig(qr)
    q = qr * sq
    ri = lax.broadcasted_iota(jnp.int32, (CH, CH), 0)
    ci = lax.broadcasted_iota(jnp.int32, (CH, CH), 1)
    b = _tri_dot((ri >= ci).astype(BF16), jnp.log(fg))
    bm, bl = b[CH // 2 - 1:CH // 2, :], b[CH - 1:CH, :]
    eb = jnp.exp(b)
    ebm = jnp.exp(b - bm)
    ekm = jnp.exp(jnp.minimum(bm - b, EXP_CLAMP))
    ekl = jnp.exp(bl - b)
    return dict(sf=sf, fg=fg, k=k, sq=sq, q=q, eb=eb, ebm=ebm, ekm=ekm, ekl=ekl, ebl=jnp.exp(bl),
                qe=q * eb, qh=q * ebm, kh=k * ekm, kd=k * ekl, causal=ri >= ci, upper=(ci >= ri).astype(BF16))


def _hgrn_fwd(proj, lbl, gn):
    t = proj.shape[0]
    nt = t // TM

    def body(q_ref, f_ref, v_ref, g_ref, lbl_ref, gn_ref, yb_ref, o_ref, st_ref, st):
        @pl.when(pl.program_id(0) == 0)
        def _():
            st[...] = jnp.zeros_like(st)

        l = lbl_ref[...]
        lb = _sig(l[0:1, :] - l[1:2, :])
        gnv = gn_ref[...]

        def chunk(c, carry):
            off = pl.multiple_of(c * CH, CH)
            rows = pl.ds(off, CH)
            z = _hg_chunk(q_ref[rows, :], f_ref[rows, :], lb)
            v, gg = v_ref[rows, :], g_ref[rows, :]
            for hh in range(HEADS):
                sl = slice(hh * HD, (hh + 1) * HD)
                s_prev = st[hh]
                st_ref[c, hh] = s_prev
                vb = v[:, sl].astype(BF16)
                att = jnp.where(z["causal"], _dot(z["qh"][:, sl].astype(BF16), z["kh"][:, sl].astype(BF16), NT), 0.0)
                o = _dot(z["qe"][:, sl].astype(BF16), s_prev.astype(BF16), NT) + _dot(att.astype(BF16), vb, NN)
                st[hh] = s_prev * z["ebl"][:, sl] + _dot(vb, z["kd"][:, sl].astype(BF16), TN_)
                xh, _ = _rms(o)
                gh = gg[:, sl]
                o_ref[rows, sl] = o
                yb_ref[rows, sl] = (xh * gnv * gh * _sig(gh)).astype(BF16)
            return carry

        lax.fori_loop(0, NCH, chunk, 0)

    def col(j):
        return pl.BlockSpec((TM, HGW), lambda i, j=j: (i, j))

    return pl.pallas_call(
        body, name="hgrn_fwd", grid=(nt,),
        in_specs=[col(0), col(1), col(2), col(3), pl.BlockSpec((2, HGW), lambda i: (0, 0)),
                  pl.BlockSpec((1, HD), lambda i: (0, 0))],
        out_specs=[col(0), col(0), pl.BlockSpec((NCH, HEADS, HD, HD), lambda i: (i, 0, 0, 0))],
        out_shape=[jax.ShapeDtypeStruct((t, HGW), BF16), jax.ShapeDtypeStruct((t, HGW), F32),
                   jax.ShapeDtypeStruct((t // CH, HEADS, HD, HD), F32)],
        scratch_shapes=[pltpu.VMEM((HEADS, HD, HD), F32)],
        compiler_params=_cparams(("arbitrary",)),
    )(proj, proj, proj, proj, lbl, gn)


def _hgrn_bwd(dyb, proj, o, states, lbl, gn, dproj):
    t = proj.shape[0]
    nt = t // TM

    def body(dy_ref, q_ref, f_ref, v_ref, g_ref, o_ref, st_ref, lbl_ref, gn_ref, _,
             dp_ref, dgn_ref, dl_ref, dst, dlb):
        i = pl.program_id(0)

        @pl.when(i == 0)
        def _():
            dst[...] = jnp.zeros_like(dst)
            dlb[...] = jnp.zeros_like(dlb)
            dgn_ref[...] = jnp.zeros_like(dgn_ref)

        l = lbl_ref[...]
        lb = _sig(l[0:1, :] - l[1:2, :])
        gnv = gn_ref[...]
        last = lax.broadcasted_iota(jnp.int32, (CH, HD), 0) == CH - 1

        def chunk(cc, carry):
            c = NCH - 1 - cc
            off = pl.multiple_of(c * CH, CH)
            rows = pl.ds(off, CH)
            qr, fr = q_ref[rows, :], f_ref[rows, :]
            z = _hg_chunk(qr, fr, lb)
            v, gg, ov, dy = v_ref[rows, :], g_ref[rows, :], o_ref[rows, :], dy_ref[rows, :]
            dqs, dks, dbs, dvs, dgs = [], [], [], [], []
            dgn = jnp.zeros((1, HD), F32)
            for hh in range(HEADS):
                sl = slice(hh * HD, (hh + 1) * HD)
                s_prev, ds_new = st_ref[c, hh], dst[hh]
                qe, qh, kh, kd = z["qe"][:, sl], z["qh"][:, sl], z["kh"][:, sl], z["kd"][:, sl]
                ebl = z["ebl"][:, sl]
                gh, dyh = gg[:, sl], dy[:, sl]
                xh, rr = _rms(ov[:, sl])
                sg = _sig(gh)
                dyn = dyh * (gh * sg)
                dgs.append(dyh * (xh * gnv) * (sg * (1.0 + gh * (1.0 - sg))))
                dgn = dgn + jnp.sum(dyn * xh, axis=0, keepdims=True)
                dxh = dyn * gnv
                do = (rr * (dxh - xh * jnp.mean(dxh * xh, axis=-1, keepdims=True))).astype(BF16)
                vb, dsb = v[:, sl].astype(BF16), ds_new.astype(BF16)
                qeb, qhb, khb, kdb = (a.astype(BF16) for a in (qe, qh, kh, kd))
                att = jnp.where(z["causal"], _dot(qhb, khb, NT), 0.0).astype(BF16)
                datt = jnp.where(z["causal"], _dot(do, vb, NT), 0.0).astype(BF16)
                dvs.append(_dot(att, do, TN_) + _dot(kdb, dsb, NT))
                dqe = _dot(do, s_prev.astype(BF16), NN)
                dqh = _dot(datt, khb, NN)
                dkh = _dot(datt, qhb, TN_)
                dkd = _dot(vb, dsb, NN)
                qe, qh, kh, kd = (a.astype(F32) for a in (qeb, qhb, khb, kdb))
                dbl = (jnp.sum(dkd * kd, axis=0, keepdims=True)
                       + jnp.sum(ds_new * s_prev, axis=0, keepdims=True) * ebl)
                dqs.append(dqe * z["eb"][:, sl] + dqh * z["ebm"][:, sl])
                dks.append(dkh * z["ekm"][:, sl] + dkd * z["ekl"][:, sl])
                dbs.append(dqe * qe + dqh * qh - dkh * kh - dkd * kd + jnp.where(last, dbl, 0.0))
                dst[hh] = _dot(do, qeb, TN_) + ds_new * ebl
            dgn_ref[...] += dgn
            dq, dk, db = (jnp.concatenate(x, axis=1) for x in (dqs, dks, dbs))
            dlf = _tri_dot(z["upper"], db)
            sf, fg, sq = z["sf"], z["fg"], z["sq"]
            dmix = dlf / fg - dk
            dsf = dmix * (1.0 - lb)
            dlb[...] += jnp.sum(dmix * (1.0 - sf), axis=0, keepdims=True)
            dp_ref[rows, 0:HGW] = (dq * (sq * (1.0 + qr * (1.0 - sq)))).astype(BF16)
            dp_ref[rows, HGW:2 * HGW] = (dsf * sf * (1.0 - sf)).astype(BF16)
            dp_ref[rows, 2 * HGW:3 * HGW] = jnp.concatenate(dvs, axis=1).astype(BF16)
            dp_ref[rows, 3 * HGW:4 * HGW] = jnp.concatenate(dgs, axis=1).astype(BF16)
            return carry

        lax.fori_loop(0, NCH, chunk, 0)
        dl0 = dlb[...] * lb * (1.0 - lb)
        dl_ref[0:1, :] = dl0
        dl_ref[1:2, :] = -dl0

    def col(j):
        return pl.BlockSpec((TM, HGW), lambda i, j=j: (nt - 1 - i, j))

    return pl.pallas_call(
        body, name="hgrn_bwd", grid=(nt,),
        in_specs=[col(0), col(0), col(1), col(2), col(3), col(0),
                  pl.BlockSpec((NCH, HEADS, HD, HD), lambda i: (nt - 1 - i, 0, 0, 0)),
                  pl.BlockSpec((2, HGW), lambda i: (0, 0)), pl.BlockSpec((1, HD), lambda i: (0, 0)), ANY],
        out_specs=[pl.BlockSpec((TM, 4 * HGW), lambda i: (nt - 1 - i, 0)),
                   pl.BlockSpec((1, HD), lambda i: (0, 0)), pl.BlockSpec((2, HGW), lambda i: (0, 0))],
        out_shape=[jax.ShapeDtypeStruct((t, D_INP), BF16), jax.ShapeDtypeStruct((1, HD), F32),
                   jax.ShapeDtypeStruct((2, HGW), F32)],
        scratch_shapes=[pltpu.VMEM((HEADS, HD, HD), F32), pltpu.VMEM((1, HGW), F32)],
        input_output_aliases={9: 0},
        compiler_params=_cparams(("arbitrary",)),
    )(dyb, proj, proj, proj, proj, o, states, lbl, gn, dproj)


def _rotate_w_in(w):
    return jnp.concatenate([w[:, ROT:], jnp.zeros((w.shape[0], COL_PAD), w.dtype), w[:, :ROT]], axis=1)


def _unrotate_g_in(g):
    return jnp.concatenate([g[:, D_INP - ROT:], g[:, :D_IN - ROT]], axis=1)


def _block_diag(w):
    eye = jnp.eye(RG_BLOCKS, dtype=w.dtype)
    return (w[:, :, None, :] * eye[:, None, :, None]).reshape(RGW, RGW)


def _diag_blocks(wd):
    w4 = wd.reshape(RG_BLOCKS, RG_BLOCK, RG_BLOCKS, RG_BLOCK)
    return jnp.stack([w4[n, :, n, :] for n in range(RG_BLOCKS)])


def _local_step(h0, target, wts, small, hosts=None, finalize=None):
    hosts = hosts or {}
    carried = {}
    g = {}

    def mm(a, b, mode, out_dtype, name, **kw):
        if name not in hosts:
            return _mm(a, b, mode, out_dtype, name, **kw)
        host = hosts[name](g)
        out, res = _mm(a, b, mode, out_dtype, name, host=host, **kw)
        carried[name] = (host.ins, res)
        return out

    w_in = wts["w_in"]
    wax_d = jnp.concatenate([_block_diag(small["rg_wa"]), _block_diag(small["rg_wx"])], axis=1).astype(BF16)
    ba, bx, lam = small["rg_ba"], small["rg_bx"], small["rg_lambda"]
    lbl, gn = small["hg_lb_logits"], small["hg_norm_g"]
    conv_w, conv_b = small["conv_w"], small["conv_b"]

    hn1 = _rmsnorm_fwd(h0, small["norm1_g"], "rmsnorm1")
    proj = mm(hn1, w_in, "nn", F32, "mm_proj")
    if finalize is not None:
        wts = finalize(carried["mm_proj"][1])
    w_pa, w_pb, w_out, w_fi, w_fd = (wts[k] for k in ("w_proj_a", "w_proj_b", "w_out", "w_ffn_in", "w_ffn_down"))
    xc, xcb = _conv_fwd(proj, conv_w, conv_b)
    gri = mm(xcb, wax_d, "nn", F32, "mm_rg_gates")
    hrg, ya = _rg_fwd(gri, xc, proj, ba, bx, lam)
    yb, o, states = _hgrn_fwd(proj, lbl, gn)
    pa = mm(ya, w_pa, "nn", F32, "mm_pa")
    pb = mm(yb, w_pb, "nn", F32, "mm_pb")
    merged = _merge_fwd(pa, pb, proj)
    h1 = mm(merged, w_out, "nn", F32, "mm_out", resid=h0)
    hn2 = _rmsnorm_fwd(h1, small["norm2_g"], "rmsnorm2")
    gu, act = mm(hn2, w_fi, "nn", BF16, "mm_ffn_in", epi=_swiglu_epi(h0.shape[0]), tiles=(FM, 2 * FH))
    h2 = mm(act, w_fd, "nn", F32, "mm_ffn_down", resid=h1)
    dh2, dh2b, loss, g["norm_f_g"] = _final_loss(h2, small["norm_f_g"], target)

    g["w_ffn_down"] = mm(act, dh2b, "tn", BF16, "mm_d_wfd")
    (dgu,) = mm(dh2b, w_fd, "nt", BF16, "mm_d_act", epi=_swiglu_bwd_epi(gu), tiles=(FM, FH))
    g["w_ffn_in"] = mm(hn2, dgu, "tn", BF16, "mm_d_wfi")
    dhn2 = mm(dgu, w_fi, "nt", F32, "mm_d_hn2")
    dh1, dh1b, g["norm2_g"] = _rmsnorm_bwd(dhn2, h1, small["norm2_g"], dh2, "rmsnorm2_bwd")
    g["w_out"] = mm(merged, dh1b, "tn", BF16, "mm_d_wout")
    dmerged = mm(dh1b, w_out, "nt", F32, "mm_d_merged")
    dpa, dpb, dproj = _merge_bwd(dmerged, pa, pb, proj)
    dproj = _zero_pad_cols(dproj)
    g["w_proj_a"] = mm(ya, dpa, "tn", BF16, "mm_d_wpa")
    g["w_proj_b"] = mm(yb, dpb, "tn", BF16, "mm_d_wpb")
    dya = mm(dpa, w_pa, "nt", F32, "mm_d_ya")
    dyb = mm(dpb, w_pb, "nt", F32, "mm_d_yb")
    dproj, g["hg_norm_g"], g["hg_lb_logits"] = _hgrn_bwd(dyb, proj, o, states, lbl, gn, dproj)
    dgri, dxc, dproj, g["rg_ba"], g["rg_bx"], g["rg_lambda"] = _rg_bwd(dya, hrg, gri, xc, proj, ba, bx, lam, dproj)
    dwax = mm(xcb, dgri, "tn", F32, "mm_d_wax")
    g["rg_wa"], g["rg_wx"] = _diag_blocks(dwax[:, :RGW]), _diag_blocks(dwax[:, RGW:])
    dxc = mm(dgri, wax_d, "nt", F32, "mm_d_xc", resid=dxc)
    dproj, g["conv_w"], g["conv_b"] = _conv_bwd(dxc, proj, conv_w, dproj)
    g["w_in"] = mm(hn1, dproj, "tn", BF16, "mm_d_win")
    dhn1 = mm(dproj, w_in, "nt", F32, "mm_d_hn1")
    grad_x, g["meta_tokens"], g["norm1_g"] = _rmsnorm1_bwd(dhn1, h0, small["norm1_g"], dh1)
    return loss, grad_x, g, carried


def _adamw_math(w, g, m, v):
    m = B1 * m + (1.0 - B1) * g
    v = B2 * v + (1.0 - B2) * (g * g)
    m_hat = m / (1.0 - B1 ** STEP)
    v_hat = v / (1.0 - B2 ** STEP)
    return -LR * (m_hat / (jnp.sqrt(v_hat) + ADAM_EPS) + WD * w), m, v


def _adamw(w, g, m, v, name):
    r, c = w.shape
    tr = _pick(r, (256, 352, 320, 128, 64, 32, 16, 8))

    def body(w_ref, g_ref, m_ref, v_ref, d_ref, mo_ref, vo_ref):
        d_ref[...], mo_ref[...], vo_ref[...] = _adamw_math(w_ref[...], g_ref[...], m_ref[...], v_ref[...])

    blk = pl.BlockSpec((tr, c), lambda i: (i, 0))
    return pl.pallas_call(
        body, name=name, grid=(r // tr,), in_specs=[blk] * 4, out_specs=[blk] * 3,
        out_shape=[jax.ShapeDtypeStruct((r, c), F32)] * 3,
        compiler_params=_cparams(("parallel",)),
    )(w, g, m, v)


def _adamw_small(recv, w, m, v, name):
    rows = w.shape[0]
    tr = rows

    def body(r_ref, w_ref, m_ref, v_ref, g_ref, d_ref, mo_ref, vo_ref):
        g = r_ref[0]
        for p in range(1, 8):
            g = g + r_ref[p]
        g_ref[...] = g
        d_ref[...], mo_ref[...], vo_ref[...] = _adamw_math(w_ref[...], g, m_ref[...], v_ref[...])

    blk = pl.BlockSpec((tr, 128), lambda i: (i, 0))
    return pl.pallas_call(
        body, name=name, grid=(rows // tr,),
        in_specs=[pl.BlockSpec((8, tr, 128), lambda i: (0, i, 0)), blk, blk, blk], out_specs=[blk] * 4,
        out_shape=[jax.ShapeDtypeStruct((rows, 128), F32)] * 4,
        compiler_params=_cparams(("parallel",)),
    )(recv, w, m, v)


def _col_block(weight, px, py):
    return 2 * py + px if weight == "w_ffn_in" else 2 * px + py


def _sum_place(recv, gfull, weight, xyc, name):
    _, r, c = recv.shape
    tr = _pick(r, (128, 176, 160, 64, 32, 16))
    nb = r // tr

    def body(_, own_ref, r_ref, o_ref):
        g = own_ref[...].astype(F32)
        for p in range(7):
            g = g + r_ref[p].astype(F32)
        o_ref[...] = g

    if weight in COL_SHARDED:
        own = pl.BlockSpec((tr, c), lambda i, s: (s[2] * nb + i, _col_block(weight, s[0], s[1])))
    else:
        own = pl.BlockSpec((tr, c), lambda i, s: ((4 * s[0] + 2 * s[1] + s[2]) * nb + i, 0))
    return pl.pallas_call(
        body, name=name,
        grid_spec=pltpu.PrefetchScalarGridSpec(
            num_scalar_prefetch=1, grid=(nb,),
            in_specs=[own, pl.BlockSpec((7, tr, c), lambda i, s: (0, i, 0))],
            out_specs=pl.BlockSpec((tr, c), lambda i, s: (s[2] * nb + i, 0))),
        out_shape=jax.ShapeDtypeStruct((2 * r, c), F32),
        compiler_params=_cparams(("arbitrary",)),
    )(xyc, gfull, recv)


def _place_own(shard, slot, dtype, name):
    r, c = shard.shape
    tr = _pick(r, (256, 352, 320, 128, 32))

    def body(_, x_ref, o_ref):
        o_ref[...] = x_ref[...].astype(dtype)

    return pl.pallas_call(
        body, name=name,
        grid_spec=pltpu.PrefetchScalarGridSpec(
            num_scalar_prefetch=1, grid=(r // tr,),
            in_specs=[pl.BlockSpec((tr, c), lambda i, s: (i, 0))],
            out_specs=pl.BlockSpec((None, tr, c), lambda i, s: (s[0], i, 0))),
        out_shape=jax.ShapeDtypeStruct((4, r, c), dtype),
        compiler_params=_cparams(("arbitrary",)),
    )(slot, shard)


BIG = ("w_in", "w_proj_a", "w_proj_b", "w_out", "w_ffn_in", "w_ffn_down")
COL_SHARDED = ("w_in", "w_ffn_in")


def _coords():
    return lax.axis_index("x"), lax.axis_index("y"), lax.axis_index("c")


def _gather_host(bufs):
    n = len(bufs)

    def place():
        x, y, c = _coords()
        return x, y, c, [(1 - x, y), (x, 1 - y), (1 - x, 1 - y)]

    def slab(b, chip, half_of):
        half = b.shape[1] // 2
        return b.at[chip, pl.ds(pl.multiple_of(half_of * half, 8), half)]

    def copy(b, sems, w, k, chip, half_of, to):
        s = slab(b, chip, half_of)
        return pltpu.make_async_remote_copy(src_ref=s, dst_ref=s, send_sem=sems[0].at[w, k], recv_sem=sems[1].at[w, k],
                                            device_id=to, device_id_type=MESH)

    def start(_, outs, sems):
        x, y, c, chips = place()
        for w, b in enumerate(outs):
            for k, (cx, cy) in enumerate(chips):
                copy(b, sems, w, k, 2 * x + y, c, (cx, cy, c)).start()

    def mid(_, outs, sems):
        x, y, c, chips = place()
        for w, b in enumerate(outs):
            for k, (cx, cy) in enumerate(chips):
                copy(b, sems, w, k, 2 * cx + cy, c, (cx, cy, c)).wait_recv()
                copy(b, sems, w, 3 + k, 2 * cx + cy, c, (x, y, 1 - c)).start()

    def finish(_, outs, sems):
        x, y, c, chips = place()
        for w, b in enumerate(outs):
            for k, (cx, cy) in enumerate(chips):
                copy(b, sems, w, 3 + k, 2 * cx + cy, 1 - c, (x, y, 1 - c)).wait_recv()
        for w, b in enumerate(outs):
            for k, (cx, cy) in enumerate(chips):
                copy(b, sems, w, k, 2 * x + y, c, (cx, cy, c)).wait_send()
                copy(b, sems, w, 3 + k, 2 * cx + cy, c, (x, y, 1 - c)).wait_send()

    return _Host(ins=list(bufs), out_shapes=[jax.ShapeDtypeStruct(b.shape, b.dtype) for b in bufs],
                 aliases={i: i for i in range(n)},
                 sems=[pltpu.SemaphoreType.DMA((n, 6)), pltpu.SemaphoreType.DMA((n, 6))],
                 start=start, mid=mid, finish=finish)


def _run_host(host, name):
    n_in, n_out = len(host.ins), len(host.out_shapes)

    def body(*refs):
        ins, outs, sems = refs[:n_in], refs[n_in:n_in + n_out], refs[n_in + n_out:]
        host.start(ins, outs, sems)
        if host.mid is not None:
            host.mid(ins, outs, sems)
        host.finish(ins, outs, sems)

    return pl.pallas_call(
        body, name=name, in_specs=[ANY] * n_in, out_specs=[ANY] * n_out, out_shape=list(host.out_shapes),
        scratch_shapes=list(host.sems), input_output_aliases=dict(host.aliases),
    )(*host.ins)


def _peer(x, y, c, k):
    fx, fy, fc = (k >> 2) & 1, (k >> 1) & 1, k & 1
    return (x ^ fx if fx else x, y ^ fy if fy else y, c ^ fc if fc else c)


def _sub_shape(name, full_shape):
    r, c = full_shape
    return (r // 2, c // 4) if name in COL_SHARDED else (r // 8, c)


def _exchange_host(names, grads):
    n = len(names)
    shapes = [_sub_shape(k, g.shape) for k, g in zip(names, grads)]

    def copy(ins, outs, sems, w, k):
        x, y, c = _coords()
        px, py, pc = _peer(x, y, c, k)
        sr, sc = shapes[w]
        if names[w] in COL_SHARDED:
            col = _col_block(names[w], px, py) * sc
            src = ins[w].at[pl.ds(pl.multiple_of(pc * sr, 16), sr), pl.ds(pl.multiple_of(col, 128), sc)]
        else:
            src = ins[w].at[pl.ds(pl.multiple_of((4 * px + 2 * py + pc) * sr, 16), sr)]
        return pltpu.make_async_remote_copy(
            src_ref=src, dst_ref=outs[w].at[k - 1], send_sem=sems[0].at[w, k - 1], recv_sem=sems[1].at[w, k - 1],
            device_id=(px, py, pc), device_id_type=MESH)

    def start(ins, outs, sems):
        for w in range(n):
            for k in range(1, 8):
                copy(ins, outs, sems, w, k).start()

    def finish(ins, outs, sems):
        for w in range(n):
            for k in range(1, 8):
                copy(ins, outs, sems, w, k).wait_recv()
        for w in range(n):
            for k in range(1, 8):
                copy(ins, outs, sems, w, k).wait_send()

    return _Host(ins=list(grads), out_shapes=[jax.ShapeDtypeStruct((7,) + s, g.dtype) for s, g in zip(shapes, grads)],
                 aliases={}, sems=[pltpu.SemaphoreType.DMA((n, 7)), pltpu.SemaphoreType.DMA((n, 7))],
                 start=start, mid=None, finish=finish)


def _sibling_host(bufs):
    n = len(bufs)

    def copy(outs, sems, w, half_of):
        x, y, c = _coords()
        half = outs[w].shape[0] // 2
        rows = outs[w].at[pl.ds(pl.multiple_of(half_of * half, 8), half)]
        return pltpu.make_async_remote_copy(src_ref=rows, dst_ref=rows, send_sem=sems[0].at[w], recv_sem=sems[1].at[w],
                                            device_id=(x, y, 1 - c), device_id_type=MESH)

    def start(_, outs, sems):
        c = lax.axis_index("c")
        for w in range(n):
            copy(outs, sems, w, c).start()

    def finish(_, outs, sems):
        c = lax.axis_index("c")
        for w in range(n):
            copy(outs, sems, w, 1 - c).wait_recv()
        for w in range(n):
            copy(outs, sems, w, c).wait_send()

    return _Host(ins=list(bufs), out_shapes=[jax.ShapeDtypeStruct(b.shape, b.dtype) for b in bufs],
                 aliases={i: i for i in range(n)},
                 sems=[pltpu.SemaphoreType.DMA((n,)), pltpu.SemaphoreType.DMA((n,))], start=start, mid=None, finish=finish)


def _pack_host(pack):
    rows = pack.shape[0]

    def me_of():
        x, y, c = _coords()
        return x, y, c, 4 * x + 2 * y + c

    def copy(ins, outs, sems, k, slot):
        x, y, c, _ = me_of()
        return pltpu.make_async_remote_copy(src_ref=ins[0], dst_ref=outs[0].at[slot], send_sem=sems[0].at[k - 1],
                                            recv_sem=sems[1].at[k - 1], device_id=_peer(x, y, c, k), device_id_type=MESH)

    def start(ins, outs, sems):
        me = me_of()[3]
        pltpu.make_async_copy(ins[0], outs[0].at[me], sems[2]).start()
        for k in range(1, 8):
            copy(ins, outs, sems, k, me).start()

    def finish(ins, outs, sems):
        x, y, c, me = me_of()
        for k in range(1, 8):
            px, py, pc = _peer(x, y, c, k)
            copy(ins, outs, sems, k, 4 * px + 2 * py + pc).wait_recv()
        for k in range(1, 8):
            copy(ins, outs, sems, k, me).wait_send()
        pltpu.make_async_copy(ins[0], outs[0].at[me], sems[2]).wait()

    return _Host(ins=[pack], out_shapes=[jax.ShapeDtypeStruct((8, rows, 128), F32)], aliases={},
                 sems=[pltpu.SemaphoreType.DMA((7,)), pltpu.SemaphoreType.DMA((7,)), pltpu.SemaphoreType.DMA],
                 start=start, mid=None, finish=finish)


def _join_hosts(hosts):
    ins, outs, sems, aliases, spans = [], [], [], {}, []
    for h in hosts:
        spans.append((len(ins), len(h.ins), len(outs), len(h.out_shapes), len(sems), len(h.sems)))
        for i_in, i_out in h.aliases.items():
            aliases[len(ins) + i_in] = len(outs) + i_out
        ins, outs, sems = ins + list(h.ins), outs + list(h.out_shapes), sems + list(h.sems)

    def phase(which):
        fns = [getattr(h, which) for h in hosts]
        if all(f is None for f in fns):
            return None

        def run(i_refs, o_refs, s_refs):
            for f, (i0, n_i, o0, n_o, s0, n_s) in zip(fns, spans):
                if f is not None:
                    f(i_refs[i0:i0 + n_i], o_refs[o0:o0 + n_o], s_refs[s0:s0 + n_s])

        return run

    return _Host(ins, outs, aliases, sems, phase("start"), phase("mid"), phase("finish"))


def _allgather_pack(pack):
    rows = pack.shape[0]

    def body(p_ref, o_ref, send, recv, local):
        x, y, c = _coords()
        me = 4 * x + 2 * y + c
        mine = pltpu.make_async_copy(p_ref, o_ref.at[me], local)
        mine.start()

        def copy(k, slot):
            return pltpu.make_async_remote_copy(src_ref=p_ref, dst_ref=o_ref.at[slot], send_sem=send.at[k - 1],
                                                recv_sem=recv.at[k - 1], device_id=_peer(x, y, c, k), device_id_type=MESH)

        for k in range(1, 8):
            copy(k, me).start()
        for k in range(1, 8):
            px, py, pc = _peer(x, y, c, k)
            copy(k, 4 * px + 2 * py + pc).wait_recv()
        for k in range(1, 8):
            copy(k, me).wait_send()
        mine.wait()

    return pl.pallas_call(
        body, name="allgather_pack", in_specs=[pl.BlockSpec(memory_space=pltpu.VMEM)], out_specs=ANY,
        out_shape=jax.ShapeDtypeStruct((8, rows, 128), F32),
        scratch_shapes=[pltpu.SemaphoreType.DMA((7,)), pltpu.SemaphoreType.DMA((7,)), pltpu.SemaphoreType.DMA],
        compiler_params=pltpu.CompilerParams(vmem_limit_bytes=VMEM_LIMIT),
    )(pack)


SMALL = ("norm1_g", "conv_b", "rg_wa", "rg_ba", "rg_wx", "rg_bx", "rg_lambda", "hg_lb_logits", "hg_norm_g",
         "norm2_g", "norm_f_g")
SHARDED_SMALL = ("meta_tokens", "conv_w")
SEG = 1024


PACK_LATE = ("norm1_g", "meta_tokens")
PACK_EARLY = tuple(k for k in SMALL + SHARDED_SMALL if k not in PACK_LATE)


def _pack(vals, names):
    parts = []
    for k in names:
        v = vals[k].reshape(-1).astype(F32)
        parts.append(jnp.pad(v, (0, -v.shape[0] % SEG)))
    return jnp.concatenate(parts).reshape(-1, 128)


def _unpack(flat, shapes, names):
    out, off = {}, 0
    flat = flat.reshape(-1)
    for k in names:
        size = 1
        for s in shapes[k]:
            size *= s
        out[k] = flat[off:off + size].reshape(shapes[k])
        off += size + (-size % SEG)
    return out


ORDER = ("meta_tokens", "norm1_g", "w_in", "conv_w", "conv_b", "rg_wa", "rg_ba", "rg_wx", "rg_bx", "rg_lambda",
         "hg_lb_logits", "hg_norm_g", "w_proj_a", "w_proj_b", "w_out", "norm2_g", "w_ffn_in", "w_ffn_down", "norm_f_g")


def kernel(x, meta_tokens, norm1_g, w_in, conv_w, conv_b, rg_wa, rg_ba, rg_wx, rg_bx, rg_lambda, hg_lb_logits, hg_norm_g, w_proj_a, w_proj_b, w_out, norm2_g, w_ffn_in, w_ffn_down, norm_f_g, loss_target, m_meta_tokens, m_norm1_g, m_w_in, m_conv_w, m_conv_b, m_rg_wa, m_rg_ba, m_rg_wx, m_rg_bx, m_rg_lambda, m_hg_lb_logits, m_hg_norm_g, m_w_proj_a, m_w_proj_b, m_w_out, m_norm2_g, m_w_ffn_in, m_w_ffn_down, m_norm_f_g, v_meta_tokens, v_norm1_g, v_w_in, v_conv_w, v_conv_b, v_rg_wa, v_rg_ba, v_rg_wx, v_rg_bx, v_rg_lambda, v_hg_lb_logits, v_hg_norm_g, v_w_proj_a, v_w_proj_b, v_w_out, v_norm2_g, v_w_ffn_in, v_w_ffn_down, v_norm_f_g):
    args = dict(locals())
    w = {k: args[k] for k in ORDER}
    m = {k: args["m_" + k] for k in ORDER}
    v = {k: args["v_" + k] for k in ORDER}
    xi, yi, ci = _coords()
    chip = 2 * xi + yi
    slot = jnp.reshape(chip, (1,)).astype(jnp.int32)
    xyc = jnp.stack([xi, yi, ci]).astype(jnp.int32)

    def full_matrix(k, gth):
        if k == "w_ffn_in":
            return jnp.concatenate([gth[0], gth[2], gth[1], gth[3]], axis=1)
        if k in COL_SHARDED:
            return jnp.concatenate([gth[j] for j in range(4)], axis=1)
        return gth.reshape(-1, gth.shape[2])

    tiny = jnp.zeros((32, 384), F32)
    tiny = tiny.at[0:N_META, 0:256].set(meta_tokens).at[N_META:N_META + 4, 0:320].set(conv_w[0])
    slots = {k: _place_own(w[k][0], slot, BF16, "place_" + k) for k in BIG}
    w_in_all, tiny_all = _run_host(_gather_host([slots["w_in"], _place_own(tiny, slot, F32, "place_tiny")]),
                                   "allgather_w_in")
    later = [k for k in BIG if k != "w_in"]
    meta_full = jnp.transpose(tiny_all[:, 0:N_META, 0:256], (1, 0, 2)).reshape(N_META, D)
    conv_w_full = jnp.transpose(tiny_all[:, N_META:N_META + 4, 0:320], (1, 0, 2)).reshape(4, RGW)

    small = dict(norm1_g=norm1_g, conv_w=conv_w_full, conv_b=conv_b, rg_wa=rg_wa[0], rg_ba=rg_ba, rg_wx=rg_wx[0],
                 rg_bx=rg_bx, rg_lambda=rg_lambda, hg_lb_logits=hg_lb_logits, hg_norm_g=hg_norm_g,
                 norm2_g=norm2_g, norm_f_g=norm_f_g.reshape(1, D))

    hosts = {
        "mm_proj": lambda g: _gather_host([slots[k] for k in later]),
        "mm_d_act": lambda g: _exchange_host(["w_ffn_down"], [g["w_ffn_down"]]),
        "mm_d_hn2": lambda g: _exchange_host(["w_ffn_in"], [g["w_ffn_in"]]),
        "mm_d_win": lambda g: _join_hosts([
            _exchange_host(["w_out", "w_proj_a", "w_proj_b"], [g["w_out"], g["w_proj_a"], g["w_proj_b"]]),
            _pack_host(_pack(g, PACK_EARLY))]),
        "mm_d_hn1": lambda g: _exchange_host(["w_in"], [_unrotate_g_in(g["w_in"])]),
    }
    h0 = jnp.concatenate([jnp.zeros((PAD, D), F32), meta_full, x[0]], axis=0)
    loss_blk, grad_x, g, carried = _local_step(
        h0, loss_target[0], {"w_in": _rotate_w_in(full_matrix("w_in", w_in_all))}, small, hosts=hosts,
        finalize=lambda gathered: {k: full_matrix(k, gth) for k, gth in zip(later, gathered)})
    loss = lax.psum(loss_blk[0, 0], ("x", "y", "c"))

    halves = {}
    sources = {"mm_d_act": ["w_ffn_down"], "mm_d_hn2": ["w_ffn_in"], "mm_d_win": ["w_out", "w_proj_a", "w_proj_b"],
               "mm_d_hn1": ["w_in"]}
    for name, keys in sources.items():
        partials, received = carried[name]
        for k, part, rec in zip(keys, partials, received):
            halves[k] = _sum_place(rec, part, k, xyc, "sum_" + k)
    g_big = dict(zip(BIG, _run_host(_sibling_host([halves[k] for k in BIG]), "sibling_swap")))
    small_shapes = {k: w[k].shape for k in SMALL}
    small_shapes["meta_tokens"], small_shapes["conv_w"] = (N_META, D), (1, 4, RGW)
    recv_packs = {PACK_EARLY: carried["mm_d_win"][1][3], PACK_LATE: _allgather_pack(_pack(g, PACK_LATE))}

    w_sm = {k: w[k] for k in SMALL}
    m_sm = {k: m[k] for k in SMALL}
    v_sm = {k: v[k] for k in SMALL}
    for d in (w_sm, m_sm, v_sm):
        d["meta_tokens"], d["conv_w"] = jnp.zeros((N_META, D), F32), jnp.zeros((1, 4, RGW), F32)
    outs_sm = [{}, {}, {}, {}]
    for names, rec in recv_packs.items():
        res = _adamw_small(rec, _pack(w_sm, names), _pack(m_sm, names), _pack(v_sm, names), "adamw_" + names[0])
        for acc, o in zip(outs_sm, res):
            acc.update(_unpack(o, small_shapes, names))
    grad, delta, new_m, new_v = ({k: o[k] for k in SMALL} for o in outs_sm)
    g_small_full = outs_sm[0]
    g_shard = {"meta_tokens": lax.dynamic_slice(g_small_full["meta_tokens"], (0, chip * 256), (N_META, 256)),
               "conv_w": lax.dynamic_slice(g_small_full["conv_w"], (0, 0, chip * 320), (1, 4, 320))[0]}
    for k in BIG + SHARDED_SMALL:
        gk = g_big[k] if k in BIG else g_shard[k]
        wk, mk, vk = (d[k].reshape(gk.shape) for d in (w, m, v))
        dk, mk, vk = _adamw(wk, gk, mk, vk, "adamw_" + k)
        grad[k], delta[k], new_m[k], new_v[k] = (a.reshape(w[k].shape) for a in (gk, dk, mk, vk))

    return (loss, grad_x[None], *[grad[k] for k in ORDER], *[delta[k] for k in ORDER],
            *[new_m[k] for k in ORDER], *[new_v[k] for k in ORDER])
```

```python
import functools

import jax
import jax.numpy as jnp
from jax import lax
from jax.experimental import pallas as pl
from jax.experimental.pallas import tpu as pltpu

F32, BF16 = jnp.float32, jnp.bfloat16
D = 1024
N_META = 16
RGW = 1280
RG_BLOCKS, RG_BLOCK = 16, 80
RG_C = 8.0
HEADS, HD = 8, 128
HGW = HEADS * HD
DFF = 2816
D_IN = 2 * RGW + 6 * D
ROT = 2 * RGW
COL_PAD = 256
D_INP = D_IN + COL_PAD
EPS = 1e-6
CH = 64
TM = 256
ROW0 = TM
PAD = ROW0 - N_META
CT = RGW
RS = 16
NCT = RGW // CT
EXP_CLAMP = 80.0
VMEM_LIMIT = 56 * 1024 * 1024

LR, B1, B2, ADAM_EPS, WD, STEP = 0.001, 0.9, 0.999, 1e-08, 0.01, 10
MESH = pl.DeviceIdType.MESH
ANY = pl.BlockSpec(memory_space=pl.ANY)


def _cparams(sem):
    return pltpu.CompilerParams(dimension_semantics=sem, vmem_limit_bytes=VMEM_LIMIT)


def _pick(n, prefs):
    for p in prefs:
        if n % p == 0:
            return p
    return n


def _sig(x):
    return 0.5 * jnp.tanh(0.5 * x) + 0.5


def _dot(a, b, dims, precision=None):
    return lax.dot_general(a, b, (dims, ((), ())), preferred_element_type=F32, precision=precision)


NN, NT, TN_ = ((1,), (0,)), ((1,), (1,)), ((0,), (0,))


class _Host:
    def __init__(self, ins, out_shapes, aliases, sems, start, mid, finish):
        self.ins, self.out_shapes, self.aliases, self.sems = ins, out_shapes, aliases, sems
        self.start, self.mid, self.finish = start, mid, finish


class _Epi:
    def __init__(self, ins, in_specs, out_shapes, out_specs, fn):
        self.ins, self.in_specs, self.out_shapes, self.out_specs, self.fn = ins, in_specs, out_shapes, out_specs, fn


def _mm(a, b, mode, out_dtype, name, resid=None, host=None, epi=None, tiles=None, sparse=None):
    if mode == "nn":
        (m, kd), n = a.shape, b.shape[1]
    elif mode == "nt":
        (m, kd), n = a.shape, b.shape[0]
    else:
        (kd, m), n = a.shape, b.shape[1]
    if mode == "tn":
        tm = _pick(m, (1024, 1280, 1408, 640, 512, 256, 128))
        tk = _pick(kd, (1408, 768, 512, 256, 128))
    else:
        tm = _pick(m, (768, 512, 640, 256, 128))
        tk = kd if kd <= 2816 else _pick(kd, (1792, 1408, 1024, 512, 256, 128))
    tn = _pick(n, (1792, 1408, 1280, 1024, 512, 256, 128))
    if tiles is not None:
        tm, tn = tiles
    a_map = (lambda i, j, k: (k, i)) if mode == "tn" else (lambda i, j, k: (i, k))
    b_map = (lambda i, j, k: (j, k)) if mode == "nt" else (lambda i, j, k: (k, j))
    o_map = lambda i, j, k: (i, j)
    if sparse is not None:
        (ni, nj, nk), (tm, tn, tk), (a_map, b_map, o_map) = sparse
    else:
        assert m % tm == 0 and n % tn == 0 and kd % tk == 0, (name, m, n, kd, tm, tn, tk)
        ni, nj, nk = m // tm, n // tn, kd // tk
    dims = {"nn": NN, "nt": NT, "tn": TN_}[mode]

    n_hin = len(host.ins) if host else 0
    n_hout = len(host.out_shapes) if host else 0
    n_res = 0 if resid is None else 1
    n_ein = len(epi.ins) if epi else 0
    n_out = len(epi.out_shapes) if epi else 1

    def finish(r, r_ref, e_in, o_refs):
        if resid is not None:
            r = r + r_ref[...]
        if epi:
            epi.fn(r, e_in, o_refs)
        else:
            o_refs[0][...] = r.astype(out_dtype)

    def body(*refs):
        a_ref, b_ref = refs[:2]
        r_ref = refs[2] if resid is not None else None
        pos = 2 + n_res
        e_in = refs[pos:pos + n_ein]
        pos += n_ein
        h_in = refs[pos:pos + n_hin]
        pos += n_hin
        o_ref = refs[pos:pos + n_out]
        pos += n_out
        h_out = refs[pos:pos + n_hout]
        scratch = refs[pos + n_hout:]
        h_sems = scratch[(1 if nk > 1 else 0):]
        i, j, k = pl.program_id(0), pl.program_id(1), pl.program_id(2)
        if host:
            @pl.when((i == 0) & (j == 0) & (k == 0))
            def _():
                host.start(h_in, h_out, h_sems)

            if host.mid is not None:
                @pl.when((i == (3 * ni) // 4) & (j == 0) & (k == 0))
                def _():
                    host.mid(h_in, h_out, h_sems)

        if nk == 1:
            finish(_dot(a_ref[...], b_ref[...], dims), r_ref, e_in, o_ref)
        else:
            acc = scratch[0]

            @pl.when(k == 0)
            def _():
                acc[...] = jnp.zeros_like(acc)

            acc[...] += _dot(a_ref[...], b_ref[...], dims)

            @pl.when(k == nk - 1)
            def _():
                finish(acc[...], r_ref, e_in, o_ref)

        if host:
            @pl.when((i == ni - 1) & (j == nj - 1) & (k == nk - 1))
            def _():
                host.finish(h_in, h_out, h_sems)

    a_spec = pl.BlockSpec((tk, tm) if mode == "tn" else (tm, tk), a_map)
    b_spec = pl.BlockSpec((tn, tk) if mode == "nt" else (tk, tn), b_map)
    o_spec = pl.BlockSpec((tm, tn), o_map)
    in_specs, args = [a_spec, b_spec], [a, b]
    if resid is not None:
        in_specs.append(o_spec)
        args.append(resid)
    out_shapes, out_specs = [jax.ShapeDtypeStruct((m, n), out_dtype)], [o_spec]
    if epi:
        in_specs += list(epi.in_specs)
        args += list(epi.ins)
        out_shapes, out_specs = list(epi.out_shapes), list(epi.out_specs)
    scratch = [pltpu.VMEM((tm, tn), F32)] if nk > 1 else []
    if not host:
        outs = pl.pallas_call(
            body, name=name, grid=(ni, nj, nk), in_specs=in_specs, out_specs=out_specs, out_shape=out_shapes,
            scratch_shapes=scratch, compiler_params=_cparams(("parallel", "parallel", "arbitrary")),
        )(*args)
        return list(outs) if epi else outs[0]
    outs = pl.pallas_call(
        body, name=name, grid=(ni, nj, nk),
        in_specs=in_specs + [ANY] * n_hin, out_specs=out_specs + [ANY] * n_hout,
        out_shape=out_shapes + list(host.out_shapes),
        scratch_shapes=scratch + list(host.sems),
        input_output_aliases={2 + n_res + n_ein + i_in: n_out + i_out for i_in, i_out in host.aliases.items()},
        compiler_params=_cparams(("arbitrary", "arbitrary", "arbitrary")),
    )(*args, *host.ins)
    return (list(outs[:n_out]) if epi else outs[0]), list(outs[n_out:])


def _rms(x):
    r = lax.rsqrt(jnp.mean(x * x, axis=-1, keepdims=True) + EPS)
    return x * r, r


def _rmsnorm_fwd(h, g, name):
    t = h.shape[0]

    def body(h_ref, g_ref, o_ref):
        xh, _ = _rms(h_ref[...])
        o_ref[...] = (xh * g_ref[...]).astype(BF16)

    return pl.pallas_call(
        body, name=name, grid=(t // TM,),
        in_specs=[pl.BlockSpec((TM, D), lambda i: (i, 0)), pl.BlockSpec((1, D), lambda i: (0, 0))],
        out_specs=pl.BlockSpec((TM, D), lambda i: (i, 0)),
        out_shape=jax.ShapeDtypeStruct((t, D), BF16),
        compiler_params=_cparams(("parallel",)),
    )(h, g)


def _rms_bwd_math(dhn, x, g):
    xh, r = _rms(x)
    dxh = dhn * g
    dx = r * (dxh - xh * jnp.mean(dxh * xh, axis=-1, keepdims=True))
    return dx, jnp.sum(dhn * xh, axis=0, keepdims=True)


def _final_loss(h2, gf, target):
    t = h2.shape[0]

    def body(h_ref, g_ref, t_ref, dh_ref, dhb_ref, loss_ref, dg_ref):
        i = pl.program_id(0)

        @pl.when(i == 0)
        def _():
            loss_ref[...] = jnp.zeros_like(loss_ref)
            dg_ref[...] = jnp.zeros_like(dg_ref)
            dh_ref[...] = jnp.zeros_like(dh_ref)
            dhb_ref[...] = jnp.zeros_like(dhb_ref)

        @pl.when(i > 0)
        def _():
            x = h_ref[...]
            g = g_ref[...]
            xh, _ = _rms(x)
            err = xh * g - t_ref[...]
            loss_ref[...] += 0.5 * jnp.sum(jnp.sum(err * err, axis=-1, keepdims=True) * (1.0 / D))
            dx, dg = _rms_bwd_math(err * (1.0 / D), x, g)
            dh_ref[...] = dx
            dhb_ref[...] = dx.astype(BF16)
            dg_ref[...] += dg

    row = pl.BlockSpec((TM, D), lambda i: (i, 0))
    return pl.pallas_call(
        body, name="final_loss", grid=(t // TM,),
        in_specs=[row, pl.BlockSpec((1, D), lambda i: (0, 0)),
                  pl.BlockSpec((TM, D), lambda i: (jnp.maximum(i - 1, 0), 0))],
        out_specs=[row, row, pl.BlockSpec((8, 128), lambda i: (0, 0)), pl.BlockSpec((1, D), lambda i: (0, 0))],
        out_shape=[jax.ShapeDtypeStruct((t, D), F32), jax.ShapeDtypeStruct((t, D), BF16),
                   jax.ShapeDtypeStruct((8, 128), F32), jax.ShapeDtypeStruct((1, D), F32)],
        compiler_params=_cparams(("arbitrary",)),
    )(h2, gf, target)


def _rmsnorm_bwd(dhn, h, g, dres, name):
    t = h.shape[0]

    def body(dhn_ref, h_ref, g_ref, dres_ref, dh_ref, dhb_ref, dg_ref):
        @pl.when(pl.program_id(0) == 0)
        def _():
            dg_ref[...] = jnp.zeros_like(dg_ref)

        dx, dg = _rms_bwd_math(dhn_ref[...], h_ref[...], g_ref[...])
        dh = dres_ref[...] + dx
        dh_ref[...] = dh
        dhb_ref[...] = dh.astype(BF16)
        dg_ref[...] += dg

    row = pl.BlockSpec((TM, D), lambda i: (i, 0))
    vec = pl.BlockSpec((1, D), lambda i: (0, 0))
    return pl.pallas_call(
        body, name=name, grid=(t // TM,),
        in_specs=[row, row, vec, row], out_specs=[row, row, vec],
        out_shape=[jax.ShapeDtypeStruct((t, D), F32), jax.ShapeDtypeStruct((t, D), BF16),
                   jax.ShapeDtypeStruct((1, D), F32)],
        compiler_params=_cparams(("arbitrary",)),
    )(dhn, h, g, dres)


def _rmsnorm1_bwd(dhn, h, g, dres):
    t = h.shape[0]

    def body(dhn_ref, h_ref, g_ref, dres_ref, dx_ref, dmeta_ref, dg_ref):
        i = pl.program_id(0)

        @pl.when(i == 0)
        def _():
            dg_ref[...] = jnp.zeros_like(dg_ref)

        dx, dg = _rms_bwd_math(dhn_ref[...], h_ref[...], g_ref[...])
        dh = dres_ref[...] + dx
        dg_ref[...] += dg
        dx_ref[...] = dh

        @pl.when(i == 0)
        def _():
            dmeta_ref[...] = dh[PAD:, :]

    row = pl.BlockSpec((TM, D), lambda i: (i, 0))
    vec = pl.BlockSpec((1, D), lambda i: (0, 0))
    return pl.pallas_call(
        body, name="rmsnorm1_bwd", grid=(t // TM,),
        in_specs=[row, row, vec, row],
        out_specs=[pl.BlockSpec((TM, D), lambda i: (jnp.maximum(i - 1, 0), 0)),
                   pl.BlockSpec((N_META, D), lambda i: (0, 0)), vec],
        out_shape=[jax.ShapeDtypeStruct((t - ROW0, D), F32), jax.ShapeDtypeStruct((N_META, D), F32),
                   jax.ShapeDtypeStruct((1, D), F32)],
        compiler_params=_cparams(("arbitrary",)),
    )(dhn, h, g, dres)


FH = DFF // 2
FM = 384


def _swiglu_epi(t):
    def fn(r, _, outs):
        g, u = r[:, :FH], r[:, FH:]
        outs[0][...] = r.astype(BF16)
        outs[1][...] = (g * _sig(g) * u).astype(BF16)

    return _Epi(ins=[], in_specs=[],
                out_shapes=[jax.ShapeDtypeStruct((t, 2 * DFF), BF16), jax.ShapeDtypeStruct((t, DFF), BF16)],
                out_specs=[pl.BlockSpec((FM, 2 * FH), lambda i, j, k: (i, j)), pl.BlockSpec((FM, FH), lambda i, j, k: (i, j))],
                fn=fn)


def _swiglu_bwd_epi(gu):
    def fn(d, ins, outs):
        gu_t = ins[0][...].astype(F32)
        g, u = gu_t[:, :FH], gu_t[:, FH:]
        s = _sig(g)
        outs[0][:, :FH] = (d * u * (s * (1.0 + g * (1.0 - s)))).astype(BF16)
        outs[0][:, FH:] = (d * (g * s)).astype(BF16)

    spec = pl.BlockSpec((FM, 2 * FH), lambda i, j, k: (i, j))
    return _Epi(ins=[gu], in_specs=[spec], out_shapes=[jax.ShapeDtypeStruct(gu.shape, BF16)], out_specs=[spec], fn=fn)


COL_GA, COL_GB = 4, 5
COL_AX, COL_AG = (6 * D + COL_PAD) // CT, (6 * D + COL_PAD + RGW) // CT


def _merge_fwd(pa, pb, proj):
    t = pa.shape[0]

    def body(pa_ref, pb_ref, ga_ref, gb_ref, o_ref):
        o_ref[...] = (_sig(ga_ref[...]) * pa_ref[...] + _sig(gb_ref[...]) * pb_ref[...]).astype(BF16)

    row = pl.BlockSpec((TM, D), lambda i: (i, 0))
    return pl.pallas_call(
        body, name="merge_fwd", grid=(t // TM,),
        in_specs=[row, row, pl.BlockSpec((TM, D), lambda i: (i, COL_GA)), pl.BlockSpec((TM, D), lambda i: (i, COL_GB))],
        out_specs=row, out_shape=jax.ShapeDtypeStruct((t, D), BF16),
        compiler_params=_cparams(("parallel",)),
    )(pa, pb, proj, proj)


def _merge_bwd(dm, pa, pb, proj):
    t = pa.shape[0]

    def body(dm_ref, pa_ref, pb_ref, ga_ref, gb_ref, dpa_ref, dpb_ref, dproj_ref):
        d = dm_ref[...]
        sa, sb = _sig(ga_ref[...]), _sig(gb_ref[...])
        dpa_ref[...] = (sa * d).astype(BF16)
        dpb_ref[...] = (sb * d).astype(BF16)
        dproj_ref[:, :D] = (d * pa_ref[...] * sa * (1.0 - sa)).astype(BF16)
        dproj_ref[:, D:] = (d * pb_ref[...] * sb * (1.0 - sb)).astype(BF16)

    row = pl.BlockSpec((TM, D), lambda i: (i, 0))
    return pl.pallas_call(
        body, name="merge_bwd", grid=(t // TM,),
        in_specs=[row, row, row, pl.BlockSpec((TM, D), lambda i: (i, COL_GA)), pl.BlockSpec((TM, D), lambda i: (i, COL_GB))],
        out_specs=[row, row, pl.BlockSpec((TM, 2 * D), lambda i: (i, 2))],
        out_shape=[jax.ShapeDtypeStruct((t, D), BF16), jax.ShapeDtypeStruct((t, D), BF16),
                   jax.ShapeDtypeStruct((t, D_INP), BF16)],
        compiler_params=_cparams(("parallel",)),
    )(dm, pa, pb, proj, proj)


def _zero_pad_cols(dproj):
    t = dproj.shape[0]
    tz = _pick(t, (768, 256))

    def body(_, o_ref):
        o_ref[...] = jnp.zeros_like(o_ref)

    return pl.pallas_call(
        body, name="dproj_pad", grid=(t // tz,), in_specs=[ANY],
        out_specs=pl.BlockSpec((tz, COL_PAD), lambda i: (i, (6 * D) // COL_PAD)),
        out_shape=jax.ShapeDtypeStruct(dproj.shape, BF16), input_output_aliases={0: 0},
        compiler_params=_cparams(("parallel",)),
    )(dproj)


def _prev_halo(col0):
    return pl.BlockSpec((8, CT), lambda c, i: (jnp.maximum(i * (TM // 8) - 1, 0), col0 + c))


def _conv_fwd(proj, w, b):
    t = proj.shape[0]

    def body(x_ref, halo_ref, w_ref, b_ref, o_ref, ob_ref, buf):
        i = pl.program_id(1)
        buf[0:8, :] = jnp.where(i > 0, halo_ref[...], 0.0)
        buf[8:, :] = x_ref[...]
        wv = w_ref[...]
        y = b_ref[...] + wv[3:4, :] * x_ref[...]
        for j in range(3):
            y = y + wv[j:j + 1, :] * buf[pl.ds(5 + j, TM), :]
        o_ref[...] = y
        ob_ref[...] = y.astype(BF16)

    blk = pl.BlockSpec((TM, CT), lambda c, i: (i, c))
    return pl.pallas_call(
        body, name="conv_fwd", grid=(NCT, t // TM),
        in_specs=[pl.BlockSpec((TM, CT), lambda c, i: (i, COL_AX + c)), _prev_halo(COL_AX),
                  pl.BlockSpec((4, CT), lambda c, i: (0, c)), pl.BlockSpec((1, CT), lambda c, i: (0, c))],
        out_specs=[blk, blk],
        out_shape=[jax.ShapeDtypeStruct((t, RGW), F32), jax.ShapeDtypeStruct((t, RGW), BF16)],
        scratch_shapes=[pltpu.VMEM((TM + 8, CT), F32)],
        compiler_params=_cparams(("parallel", "parallel")),
    )(proj, proj, w, b)


def _conv_bwd(dxc, proj, w, dproj):
    t = proj.shape[0]
    nt = t // TM

    def body(d_ref, dn_ref, x_ref, halo_ref, w_ref, _, dx_ref, dw_ref, db_ref, bufd, bufx):
        i = pl.program_id(1)

        @pl.when(i == 0)
        def _():
            dw_ref[...] = jnp.zeros_like(dw_ref)
            db_ref[...] = jnp.zeros_like(db_ref)

        d = d_ref[...]
        bufd[0:TM, :] = d
        bufd[TM:, :] = jnp.where(i < nt - 1, dn_ref[...], 0.0)
        bufx[0:8, :] = jnp.where(i > 0, halo_ref[...], 0.0)
        bufx[8:, :] = x_ref[...]
        wv = w_ref[...]
        dx = wv[3:4, :] * d
        for j in range(3):
            dx = dx + wv[j:j + 1, :] * bufd[pl.ds(3 - j, TM), :]
            dw_ref[j:j + 1, :] += jnp.sum(d * bufx[pl.ds(5 + j, TM), :], axis=0, keepdims=True)
        dw_ref[3:4, :] += jnp.sum(d * x_ref[...], axis=0, keepdims=True)
        db_ref[...] += jnp.sum(d, axis=0, keepdims=True)
        dx_ref[...] = dx.astype(BF16)

    return pl.pallas_call(
        body, name="conv_bwd", grid=(NCT, nt),
        in_specs=[pl.BlockSpec((TM, CT), lambda c, i: (i, c)),
                  pl.BlockSpec((8, CT), lambda c, i: (jnp.minimum((i + 1) * (TM // 8), t // 8 - 1), c)),
                  pl.BlockSpec((TM, CT), lambda c, i: (i, COL_AX + c)), _prev_halo(COL_AX),
                  pl.BlockSpec((4, CT), lambda c, i: (0, c)), ANY],
        out_specs=[pl.BlockSpec((TM, CT), lambda c, i: (i, COL_AX + c)),
                   pl.BlockSpec((4, CT), lambda c, i: (0, c)), pl.BlockSpec((1, CT), lambda c, i: (0, c))],
        out_shape=[jax.ShapeDtypeStruct((t, D_INP), BF16), jax.ShapeDtypeStruct((4, RGW), F32),
                   jax.ShapeDtypeStruct((1, RGW), F32)],
        scratch_shapes=[pltpu.VMEM((TM + 8, CT), F32), pltpu.VMEM((TM + 8, CT), F32)],
        input_output_aliases={5: 0},
        compiler_params=_cparams(("parallel", "arbitrary")),
    )(dxc, dxc, proj, proj, w, dproj)


def _gelu(x):
    c = 0.7978845608028654
    th = jnp.tanh(c * (x + 0.044715 * x * x * x))
    return 0.5 * x * (1.0 + th), th


def _rg_gates(gr, gi, xc, ba, bx, lam, row0):
    r = _sig(gr + ba)
    ig = _sig(gi + bx)
    sp = jax.nn.softplus(-lam)
    a = jnp.exp(-RG_C * r * sp)
    s2 = jnp.maximum(1.0 - a * a, 1e-30)
    rs = lax.rsqrt(s2)
    s = s2 * rs
    rows = row0 + lax.broadcasted_iota(jnp.int32, gr.shape, 0)
    live = rows >= PAD
    u = jnp.where(live, s * ig * xc, 0.0)
    return r, ig, sp, a, s, rs, u, live


def _rg_fwd(gri, xc, proj, ba, bx, lam):
    t = gri.shape[0]

    def body(gr_ref, gi_ref, xc_ref, ag_ref, ba_ref, bx_ref, lam_ref, h_ref, ya_ref, hc):
        i = pl.program_id(1)

        @pl.when(i == 0)
        def _():
            hc[...] = jnp.zeros_like(hc)

        ba, bx, lam = ba_ref[...], bx_ref[...], lam_ref[...]
        sub = lax.broadcasted_iota(jnp.int32, (8, CT), 0)

        def strip(k, h):
            off = pl.multiple_of(k * RS, RS)
            rows = pl.ds(off, RS)
            _, _, _, a, _, _, u, _ = _rg_gates(gr_ref[rows, :], gi_ref[rows, :], xc_ref[rows, :], ba, bx, lam,
                                               i * TM + off)
            outs = []
            for half in range(RS // 8):
                out = jnp.zeros((8, CT), F32)
                for r in range(8):
                    h = a[8 * half + r:8 * half + r + 1, :] * h + u[8 * half + r:8 * half + r + 1, :]
                    out = jnp.where(sub == r, h, out)
                outs.append(out)
            hs = jnp.concatenate(outs, axis=0)
            h_ref[rows, :] = hs
            ge, _ = _gelu(ag_ref[rows, :])
            ya_ref[rows, :] = (hs * ge).astype(BF16)
            return h

        hc[...] = lax.fori_loop(0, TM // RS, strip, hc[...])

    blk_ = pl.BlockSpec((TM, CT), lambda c, i: (i, c))
    vec = pl.BlockSpec((1, CT), lambda c, i: (0, c))
    return pl.pallas_call(
        body, name="rg_fwd", grid=(NCT, t // TM),
        in_specs=[blk_, pl.BlockSpec((TM, CT), lambda c, i: (i, NCT + c)), blk_,
                  pl.BlockSpec((TM, CT), lambda c, i: (i, COL_AG + c)), vec, vec, vec],
        out_specs=[blk_, blk_],
        out_shape=[jax.ShapeDtypeStruct((t, RGW), F32), jax.ShapeDtypeStruct((t, RGW), BF16)],
        scratch_shapes=[pltpu.VMEM((1, CT), F32)],
        compiler_params=_cparams(("parallel", "arbitrary")),
    )(gri, gri, xc, proj, ba, bx, lam)


def _rg_bwd(dya, h, gri, xc, proj, ba, bx, lam, dproj):
    t = gri.shape[0]
    nt = t // TM
    assert NCT == 1

    def body(dya_ref, h_ref, hh_ref, gr_ref, gi_ref, xc_ref, ag_ref, ba_ref, bx_ref, lam_ref, _,
             dgri_ref, dxc_ref, dag_ref, dba_ref, dbx_ref, dlam_ref, hbuf, acc, cc):
        i = pl.program_id(1)
        ri = nt - 1 - i

        @pl.when(i == 0)
        def _():
            cc[...] = jnp.zeros_like(cc)
            dba_ref[...] = jnp.zeros_like(dba_ref)
            dbx_ref[...] = jnp.zeros_like(dbx_ref)
            dlam_ref[...] = jnp.zeros_like(dlam_ref)

        acc[...] = jnp.zeros_like(acc)
        ba, bx, lam = ba_ref[...], bx_ref[...], lam_ref[...]
        halo_last = jnp.where(ri > 0, hh_ref[7:8, :], 0.0)
        sub = lax.broadcasted_iota(jnp.int32, (8, CT), 0)
        c0 = 0.7978845608028654

        def strip(kk, c):
            k = TM // RS - 1 - kk
            off = pl.multiple_of(k * RS, RS)
            rows = pl.ds(off, RS)
            xc, ag, dya, hv = xc_ref[rows, :], ag_ref[rows, :], dya_ref[rows, :], h_ref[rows, :]
            r, ig, sp, a, s, rs, _, live = _rg_gates(gr_ref[rows, :], gi_ref[rows, :], xc, ba, bx, lam, ri * TM + off)
            ge, th = _gelu(ag)
            dge = 0.5 * (1.0 + th) + 0.5 * ag * (1.0 - th * th) * c0 * (1.0 + 3.0 * 0.044715 * ag * ag)
            dag_ref[rows, :] = (dya * hv * dge).astype(BF16)
            d = dya * ge
            outs = []
            for half in range(RS // 8 - 1, -1, -1):
                out = jnp.zeros((8, CT), F32)
                for rr in range(7, -1, -1):
                    g = d[8 * half + rr:8 * half + rr + 1, :] + c
                    c = a[8 * half + rr:8 * half + rr + 1, :] * g
                    out = jnp.where(sub == rr, g, out)
                outs.insert(0, out)
            g = jnp.concatenate(outs, axis=0)
            before = h_ref[pl.ds(pl.multiple_of(jnp.maximum(off - RS, 0), RS), RS), :]
            hbuf[7:8, :] = jnp.where(k == 0, halo_last, before[RS - 1:RS, :])
            hbuf[8:, :] = hv
            hprev = hbuf[pl.ds(7, RS), :]
            du = jnp.where(live, g, 0.0)
            ds = du * (ig * xc)
            dm = du * s
            dla = (g * hprev - a * ds * rs) * a
            dr = dla * (-RG_C * sp) * r * (1.0 - r)
            di = dm * xc * ig * (1.0 - ig)
            dgri_ref[rows, :CT] = dr.astype(BF16)
            dgri_ref[rows, CT:] = di.astype(BF16)
            dxc_ref[rows, :] = dm * ig
            dl = dla * (-RG_C * r)
            for half in range(RS // 8):
                part = slice(8 * half, 8 * half + 8)
                acc[0] += dr[part]
                acc[1] += di[part]
                acc[2] += dl[part]
            return c

        cc[...] = lax.fori_loop(0, TM // RS, strip, cc[...])
        dba_ref[...] += jnp.sum(acc[0], axis=0, keepdims=True)
        dbx_ref[...] += jnp.sum(acc[1], axis=0, keepdims=True)
        dlam_ref[...] += jnp.sum(acc[2], axis=0, keepdims=True) * (-_sig(-lam))

    rblk = pl.BlockSpec((TM, CT), lambda c, i: (nt - 1 - i, c))
    vec = pl.BlockSpec((1, CT), lambda c, i: (0, c))
    hh = pl.BlockSpec((8, CT), lambda c, i: (jnp.maximum((nt - 1 - i) * (TM // 8) - 1, 0), c))
    agb = pl.BlockSpec((TM, CT), lambda c, i: (nt - 1 - i, COL_AG + c))
    return pl.pallas_call(
        body, name="rg_bwd", grid=(NCT, nt),
        in_specs=[rblk, rblk, hh, rblk, pl.BlockSpec((TM, CT), lambda c, i: (nt - 1 - i, NCT + c)), rblk, agb,
                  vec, vec, vec, ANY],
        out_specs=[pl.BlockSpec((TM, 2 * CT), lambda c, i: (nt - 1 - i, c)), rblk, agb, vec, vec, vec],
        out_shape=[jax.ShapeDtypeStruct((t, 2 * RGW), BF16),
                   jax.ShapeDtypeStruct((t, RGW), F32), jax.ShapeDtypeStruct((t, D_INP), BF16),
                   jax.ShapeDtypeStruct((1, RGW), F32), jax.ShapeDtypeStruct((1, RGW), F32),
                   jax.ShapeDtypeStruct((1, RGW), F32)],
        scratch_shapes=[pltpu.VMEM((RS + 8, CT), F32), pltpu.VMEM((3, 8, CT), F32), pltpu.VMEM((1, CT), F32)],
        input_output_aliases={10: 2},
        compiler_params=_cparams(("parallel", "arbitrary")),
    )(dya, h, h, gri, gri, xc, proj, ba, bx, lam, dproj)


NCH = TM // CH
HI = lax.Precision.HIGHEST


def _tri_dot(tri, x):
    hi = x.astype(BF16)
    r1 = x - hi.astype(F32)
    mid = r1.astype(BF16)
    lo = (r1 - mid.astype(F32)).astype(BF16)
    return _dot(tri, hi, NN) + _dot(tri, mid, NN) + _dot(tri, lo, NN)


def _hg_chunk(qr, fr, lb):
    sf = _sig(fr)
    fg = lb + (1.0 - lb) * sf
    k = (1.0 - lb) * (1.0 - sf)
    sq = _sig(qr)
    q = qr * sq
    ri = lax.broadcasted_iota(jnp.int32, (CH, CH), 0)
    ci = lax.broadcasted_iota(jnp.int32, (CH, CH), 1)
    b = _tri_dot((ri >= ci).astype(BF16), jnp.log(fg))
    bm, bl = b[CH // 2 - 1:CH // 2, :], b[CH - 1:CH, :]
    eb = jnp.exp(b)
    ebm = jnp.exp(b - bm)
    ekm = jnp.exp(jnp.minimum(bm - b, EXP_CLAMP))
    ekl = jnp.exp(bl - b)
    return dict(sf=sf, fg=fg, k=k, sq=sq, q=q, eb=eb, ebm=ebm, ekm=ekm, ekl=ekl, ebl=jnp.exp(bl),
                qe=q * eb, qh=q * ebm, kh=k * ekm, kd=k * ekl, causal=ri >= ci, upper=(ci >= ri).astype(BF16))


def _hgrn_fwd(proj, lbl, gn):
    t = proj.shape[0]
    nt = t // TM

    def body(q_ref, f_ref, v_ref, g_ref, lbl_ref, gn_ref, yb_ref, o_ref, st_ref, st):
        @pl.when(pl.program_id(0) == 0)
        def _():
            st[...] = jnp.zeros_like(st)

        l = lbl_ref[...]
        lb = _sig(l[0:1, :] - l[1:2, :])
        gnv = gn_ref[...]

        def chunk(c, carry):
            off = pl.multiple_of(c * CH, CH)
            rows = pl.ds(off, CH)
            z = _hg_chunk(q_ref[rows, :], f_ref[rows, :], lb)
            v, gg = v_ref[rows, :], g_ref[rows, :]
            for hh in range(HEADS):
                sl = slice(hh * HD, (hh + 1) * HD)
                s_prev = st[hh]
                st_ref[c, hh] = s_prev
                vb = v[:, sl].astype(BF16)
                att = jnp.where(z["causal"], _dot(z["qh"][:, sl].astype(BF16), z["kh"][:, sl].astype(BF16), NT), 0.0)
                o = _dot(z["qe"][:, sl].astype(BF16), s_prev.astype(BF16), NT) + _dot(att.astype(BF16), vb, NN)
                st[hh] = s_prev * z["ebl"][:, sl] + _dot(vb, z["kd"][:, sl].astype(BF16), TN_)
                xh, _ = _rms(o)
                gh = gg[:, sl]
                o_ref[rows, sl] = o
                yb_ref[rows, sl] = (xh * gnv * gh * _sig(gh)).astype(BF16)
            return carry

        lax.fori_loop(0, NCH, chunk, 0)

    def col(j):
        return pl.BlockSpec((TM, HGW), lambda i, j=j: (i, j))

    return pl.pallas_call(
        body, name="hgrn_fwd", grid=(nt,),
        in_specs=[col(0), col(1), col(2), col(3), pl.BlockSpec((2, HGW), lambda i: (0, 0)),
                  pl.BlockSpec((1, HD), lambda i: (0, 0))],
        out_specs=[col(0), col(0), pl.BlockSpec((NCH, HEADS, HD, HD), lambda i: (i, 0, 0, 0))],
        out_shape=[jax.ShapeDtypeStruct((t, HGW), BF16), jax.ShapeDtypeStruct((t, HGW), F32),
                   jax.ShapeDtypeStruct((t // CH, HEADS, HD, HD), F32)],
        scratch_shapes=[pltpu.VMEM((HEADS, HD, HD), F32)],
        compiler_params=_cparams(("arbitrary",)),
    )(proj, proj, proj, proj, lbl, gn)


def _hgrn_bwd(dyb, proj, o, states, lbl, gn, dproj):
    t = proj.shape[0]
    nt = t // TM

    def body(dy_ref, q_ref, f_ref, v_ref, g_ref, o_ref, st_ref, lbl_ref, gn_ref, _,
             dp_ref, dgn_ref, dl_ref, dst, dlb):
        i = pl.program_id(0)

        @pl.when(i == 0)
        def _():
            dst[...] = jnp.zeros_like(dst)
            dlb[...] = jnp.zeros_like(dlb)
            dgn_ref[...] = jnp.zeros_like(dgn_ref)

        l = lbl_ref[...]
        lb = _sig(l[0:1, :] - l[1:2, :])
        gnv = gn_ref[...]
        last = lax.broadcasted_iota(jnp.int32, (CH, HD), 0) == CH - 1

        def chunk(cc, carry):
            c = NCH - 1 - cc
            off = pl.multiple_of(c * CH, CH)
            rows = pl.ds(off, CH)
            qr, fr = q_ref[rows, :], f_ref[rows, :]
            z = _hg_chunk(qr, fr, lb)
            v, gg, ov, dy = v_ref[rows, :], g_ref[rows, :], o_ref[rows, :], dy_ref[rows, :]
            dqs, dks, dbs, dvs, dgs = [], [], [], [], []
            dgn = jnp.zeros((1, HD), F32)
            for hh in range(HEADS):
                sl = slice(hh * HD, (hh + 1) * HD)
                s_prev, ds_new = st_ref[c, hh], dst[hh]
                qe, qh, kh, kd = z["qe"][:, sl], z["qh"][:, sl], z["kh"][:, sl], z["kd"][:, sl]
                ebl = z["ebl"][:, sl]
                gh, dyh = gg[:, sl], dy[:, sl]
                xh, rr = _rms(ov[:, sl])
                sg = _sig(gh)
                dyn = dyh * (gh * sg)
                dgs.append(dyh * (xh * gnv) * (sg * (1.0 + gh * (1.0 - sg))))
                dgn = dgn + jnp.sum(dyn * xh, axis=0, keepdims=True)
                dxh = dyn * gnv
                do = (rr * (dxh - xh * jnp.mean(dxh * xh, axis=-1, keepdims=True))).astype(BF16)
                vb, dsb = v[:, sl].astype(BF16), ds_new.astype(BF16)
                qeb, qhb, khb, kdb = (a.astype(BF16) for a in (qe, qh, kh, kd))
                att = jnp.where(z["causal"], _dot(qhb, khb, NT), 0.0).astype(BF16)
                datt = jnp.where(z["causal"], _dot(do, vb, NT), 0.0).astype(BF16)
                dvs.append(_dot(att, do, TN_) + _dot(kdb, dsb, NT))
                dqe = _dot(do, s_prev.astype(BF16), NN)
                dqh = _dot(datt, khb, NN)
                dkh = _dot(datt, qhb, TN_)
                dkd = _dot(vb, dsb, NN)
                qe, qh, kh, kd = (a.astype(F32) for a in (qeb, qhb, khb, kdb))
                dbl = (jnp.sum(dkd * kd, axis=0, keepdims=True)
                       + jnp.sum(ds_new * s_prev, axis=0, keepdims=True) * ebl)
                dqs.append(dqe * z["eb"][:, sl] + dqh * z["ebm"][:, sl])
                dks.append(dkh * z["ekm"][:, sl] + dkd * z["ekl"][:, sl])
                dbs.append(dqe * qe + dqh * qh - dkh * kh - dkd * kd + jnp.where(last, dbl, 0.0))
                dst[hh] = _dot(do, qeb, TN_) + ds_new * ebl
            dgn_ref[...] += dgn
            dq, dk, db = (jnp.concatenate(x, axis=1) for x in (dqs, dks, dbs))
            dlf = _tri_dot(z["upper"], db)
            sf, fg, sq = z["sf"], z["fg"], z["sq"]
            dmix = dlf / fg - dk
            dsf = dmix * (1.0 - lb)
            dlb[...] += jnp.sum(dmix * (1.0 - sf), axis=0, keepdims=True)
            dp_ref[rows, 0:HGW] = (dq * (sq * (1.0 + qr * (1.0 - sq)))).astype(BF16)
            dp_ref[rows, HGW:2 * HGW] = (dsf * sf * (1.0 - sf)).astype(BF16)
            dp_ref[rows, 2 * HGW:3 * HGW] = jnp.concatenate(dvs, axis=1).astype(BF16)
            dp_ref[rows, 3 * HGW:4 * HGW] = jnp.concatenate(dgs, axis=1).astype(BF16)
            return carry

        lax.fori_loop(0, NCH, chunk, 0)
        dl0 = dlb[...] * lb * (1.0 - lb)
        dl_ref[0:1, :] = dl0
        dl_ref[1:2, :] = -dl0

    def col(j):
        return pl.BlockSpec((TM, HGW), lambda i, j=j: (nt - 1 - i, j))

    return pl.pallas_call(
        body, name="hgrn_bwd", grid=(nt,),
        in_specs=[col(0), col(0), col(1), col(2), col(3), col(0),
                  pl.BlockSpec((NCH, HEADS, HD, HD), lambda i: (nt - 1 - i, 0, 0, 0)),
                  pl.BlockSpec((2, HGW), lambda i: (0, 0)), pl.BlockSpec((1, HD), lambda i: (0, 0)), ANY],
        out_specs=[pl.BlockSpec((TM, 4 * HGW), lambda i: (nt - 1 - i, 0)),
                   pl.BlockSpec((1, HD), lambda i: (0, 0)), pl.BlockSpec((2, HGW), lambda i: (0, 0))],
        out_shape=[jax.ShapeDtypeStruct((t, D_INP), BF16), jax.ShapeDtypeStruct((1, HD), F32),
                   jax.ShapeDtypeStruct((2, HGW), F32)],
        scratch_shapes=[pltpu.VMEM((HEADS, HD, HD), F32), pltpu.VMEM((1, HGW), F32)],
        input_output_aliases={9: 0},
        compiler_params=_cparams(("arbitrary",)),
    )(dyb, proj, proj, proj, proj, o, states, lbl, gn, dproj)


def _rotate_w_in(w):
    return jnp.concatenate([w[:, ROT:], jnp.zeros((w.shape[0], COL_PAD), w.dtype), w[:, :ROT]], axis=1)


def _unrotate_g_in(g):
    return jnp.concatenate([g[:, D_INP - ROT:], g[:, :D_IN - ROT]], axis=1)


SB = RGW // 2


def _bd_gates(t):
    tm = _pick(t, (768, 256))
    return (t // tm, 4, 1), (tm, SB, SB), (lambda i, j, k: (i, j % 2), lambda i, j, k: (j % 2, j), lambda i, j, k: (i, j))


def _bd_dx(t):
    tm = _pick(t, (768, 256))
    return ((t // tm, 2, 2), (tm, SB, SB),
            (lambda i, j, k: (i, j + 2 * k), lambda i, j, k: (j, j + 2 * k), lambda i, j, k: (i, j)))


def _bd_dw(t):
    tk = _pick(t, (1408, 768))
    return (1, 4, t // tk), (SB, SB, tk), (lambda i, j, k: (k, j % 2), lambda i, j, k: (k, j), lambda i, j, k: (j % 2, j))


def _block_diag(w):
    eye = jnp.eye(RG_BLOCKS, dtype=w.dtype)
    return (w[:, :, None, :] * eye[:, None, :, None]).reshape(RGW, RGW)


def _diag_blocks(wd):
    w4 = wd.reshape(RG_BLOCKS, RG_BLOCK, RG_BLOCKS, RG_BLOCK)
    return jnp.stack([w4[n, :, n, :] for n in range(RG_BLOCKS)])


def _local_step(h0, target, wts, small, hosts=None, finalize=None):
    hosts = hosts or {}
    carried = {}
    g = {}
    t = h0.shape[0]

    def mm(a, b, mode, out_dtype, name, **kw):
        if name not in hosts:
            return _mm(a, b, mode, out_dtype, name, **kw)
        host = hosts[name](g)
        out, res = _mm(a, b, mode, out_dtype, name, host=host, **kw)
        carried[name] = (host.ins, res)
        return out

    w_in = wts["w_in"]
    wax_d = jnp.concatenate([_block_diag(small["rg_wa"]), _block_diag(small["rg_wx"])], axis=1).astype(BF16)
    ba, bx, lam = small["rg_ba"], small["rg_bx"], small["rg_lambda"]
    lbl, gn = small["hg_lb_logits"], small["hg_norm_g"]
    conv_w, conv_b = small["conv_w"], small["conv_b"]

    hn1 = _rmsnorm_fwd(h0, small["norm1_g"], "rmsnorm1")
    proj = mm(hn1, w_in, "nn", F32, "mm_proj")
    if finalize is not None:
        wts = finalize(carried["mm_proj"][1])
    w_pa, w_pb, w_out, w_fi, w_fd = (wts[k] for k in ("w_proj_a", "w_proj_b", "w_out", "w_ffn_in", "w_ffn_down"))
    xc, xcb = _conv_fwd(proj, conv_w, conv_b)
    gri = mm(xcb, wax_d, "nn", F32, "mm_rg_gates", sparse=_bd_gates(t))
    hrg, ya = _rg_fwd(gri, xc, proj, ba, bx, lam)
    yb, o, states = _hgrn_fwd(proj, lbl, gn)
    pa = mm(ya, w_pa, "nn", F32, "mm_pa")
    pb = mm(yb, w_pb, "nn", F32, "mm_pb")
    merged = _merge_fwd(pa, pb, proj)
    h1 = mm(merged, w_out, "nn", F32, "mm_out", resid=h0)
    hn2 = _rmsnorm_fwd(h1, small["norm2_g"], "rmsnorm2")
    gu, act = mm(hn2, w_fi, "nn", BF16, "mm_ffn_in", epi=_swiglu_epi(h0.shape[0]), tiles=(FM, 2 * FH))
    h2 = mm(act, w_fd, "nn", F32, "mm_ffn_down", resid=h1)
    dh2, dh2b, loss, g["norm_f_g"] = _final_loss(h2, small["norm_f_g"], target)

    g["w_ffn_down"] = mm(act, dh2b, "tn", BF16, "mm_d_wfd")
    (dgu,) = mm(dh2b, w_fd, "nt", BF16, "mm_d_act", epi=_swiglu_bwd_epi(gu), tiles=(FM, FH))
    g["w_ffn_in"] = mm(hn2, dgu, "tn", BF16, "mm_d_wfi")
    dhn2 = mm(dgu, w_fi, "nt", F32, "mm_d_hn2")
    dh1, dh1b, g["norm2_g"] = _rmsnorm_bwd(dhn2, h1, small["norm2_g"], dh2, "rmsnorm2_bwd")
    g["w_out"] = mm(merged, dh1b, "tn", BF16, "mm_d_wout")
    dmerged = mm(dh1b, w_out, "nt", F32, "mm_d_merged")
    dpa, dpb, dproj = _merge_bwd(dmerged, pa, pb, proj)
    dproj = _zero_pad_cols(dproj)
    g["w_proj_a"] = mm(ya, dpa, "tn", BF16, "mm_d_wpa")
    g["w_proj_b"] = mm(yb, dpb, "tn", BF16, "mm_d_wpb")
    dya = mm(dpa, w_pa, "nt", F32, "mm_d_ya")
    dyb = mm(dpb, w_pb, "nt", F32, "mm_d_yb")
    dproj, g["hg_norm_g"], g["hg_lb_logits"] = _hgrn_bwd(dyb, proj, o, states, lbl, gn, dproj)
    dgri, dxc, dproj, g["rg_ba"], g["rg_bx"], g["rg_lambda"] = _rg_bwd(dya, hrg, gri, xc, proj, ba, bx, lam, dproj)
    dwax = mm(xcb, dgri, "tn", F32, "mm_d_wax", sparse=_bd_dw(t))
    g["rg_wa"], g["rg_wx"] = _diag_blocks(dwax[:, :RGW]), _diag_blocks(dwax[:, RGW:])
    dxc = mm(dgri, wax_d, "nt", F32, "mm_d_xc", resid=dxc, sparse=_bd_dx(t))
    dproj, g["conv_w"], g["conv_b"] = _conv_bwd(dxc, proj, conv_w, dproj)
    g["w_in"] = mm(hn1, dproj, "tn", BF16, "mm_d_win")
    dhn1 = mm(dproj, w_in, "nt", F32, "mm_d_hn1")
    grad_x, g["meta_tokens"], g["norm1_g"] = _rmsnorm1_bwd(dhn1, h0, small["norm1_g"], dh1)
    return loss, grad_x, g, carried


def _adamw_math(w, g, m, v):
    m = B1 * m + (1.0 - B1) * g
    v = B2 * v + (1.0 - B2) * (g * g)
    m_hat = m / (1.0 - B1 ** STEP)
    v_hat = v / (1.0 - B2 ** STEP)
    return -LR * (m_hat / (jnp.sqrt(v_hat) + ADAM_EPS) + WD * w), m, v


def _adamw(w, g, m, v, name):
    r, c = w.shape
    tr = _pick(r, (256, 352, 320, 128, 64, 32, 16, 8))

    def body(w_ref, g_ref, m_ref, v_ref, d_ref, mo_ref, vo_ref):
        d_ref[...], mo_ref[...], vo_ref[...] = _adamw_math(w_ref[...], g_ref[...], m_ref[...], v_ref[...])

    blk = pl.BlockSpec((tr, c), lambda i: (i, 0))
    return pl.pallas_call(
        body, name=name, grid=(r // tr,), in_specs=[blk] * 4, out_specs=[blk] * 3,
        out_shape=[jax.ShapeDtypeStruct((r, c), F32)] * 3,
        compiler_params=_cparams(("parallel",)),
    )(w, g, m, v)


def _adamw_small(recv, w, m, v, name):
    rows = w.shape[0]
    tr = rows

    def body(r_ref, w_ref, m_ref, v_ref, g_ref, d_ref, mo_ref, vo_ref):
        g = r_ref[0]
        for p in range(1, 8):
            g = g + r_ref[p]
        g_ref[...] = g
        d_ref[...], mo_ref[...], vo_ref[...] = _adamw_math(w_ref[...], g, m_ref[...], v_ref[...])

    blk = pl.BlockSpec((tr, 128), lambda i: (i, 0))
    return pl.pallas_call(
        body, name=name, grid=(rows // tr,),
        in_specs=[pl.BlockSpec((8, tr, 128), lambda i: (0, i, 0)), blk, blk, blk], out_specs=[blk] * 4,
        out_shape=[jax.ShapeDtypeStruct((rows, 128), F32)] * 4,
        compiler_params=_cparams(("parallel",)),
    )(recv, w, m, v)


def _col_block(weight, px, py):
    return 2 * py + px if weight == "w_ffn_in" else 2 * px + py


def _sum_place(recv, gfull, weight, xyc, name):
    _, r, c = recv.shape
    tr = _pick(r, (128, 176, 160, 64, 32, 16))
    nb = r // tr

    def body(_, own_ref, r_ref, o_ref):
        g = own_ref[...].astype(F32)
        for p in range(7):
            g = g + r_ref[p].astype(F32)
        o_ref[...] = g

    if weight in COL_SHARDED:
        own = pl.BlockSpec((tr, c), lambda i, s: (s[2] * nb + i, _col_block(weight, s[0], s[1])))
    else:
        own = pl.BlockSpec((tr, c), lambda i, s: ((4 * s[0] + 2 * s[1] + s[2]) * nb + i, 0))
    return pl.pallas_call(
        body, name=name,
        grid_spec=pltpu.PrefetchScalarGridSpec(
            num_scalar_prefetch=1, grid=(nb,),
            in_specs=[own, pl.BlockSpec((7, tr, c), lambda i, s: (0, i, 0))],
            out_specs=pl.BlockSpec((tr, c), lambda i, s: (s[2] * nb + i, 0))),
        out_shape=jax.ShapeDtypeStruct((2 * r, c), F32),
        compiler_params=_cparams(("arbitrary",)),
    )(xyc, gfull, recv)


def _place_own(shard, slot, dtype, name):
    r, c = shard.shape
    tr = _pick(r, (256, 352, 320, 128, 32))

    def body(_, x_ref, o_ref):
        o_ref[...] = x_ref[...].astype(dtype)

    return pl.pallas_call(
        body, name=name,
        grid_spec=pltpu.PrefetchScalarGridSpec(
            num_scalar_prefetch=1, grid=(r // tr,),
            in_specs=[pl.BlockSpec((tr, c), lambda i, s: (i, 0))],
            out_specs=pl.BlockSpec((None, tr, c), lambda i, s: (s[0], i, 0))),
        out_shape=jax.ShapeDtypeStruct((4, r, c), dtype),
        compiler_params=_cparams(("arbitrary",)),
    )(slot, shard)


BIG = ("w_in", "w_proj_a", "w_proj_b", "w_out", "w_ffn_in", "w_ffn_down")
COL_SHARDED = ("w_in", "w_ffn_in")


def _coords():
    return lax.axis_index("x"), lax.axis_index("y"), lax.axis_index("c")


def _gather_host(bufs):
    n = len(bufs)

    def place():
        x, y, c = _coords()
        return x, y, c, [(1 - x, y), (x, 1 - y), (1 - x, 1 - y)]

    def slab(b, chip, half_of):
        half = b.shape[1] // 2
        return b.at[chip, pl.ds(pl.multiple_of(half_of * half, 8), half)]

    def copy(b, sems, w, k, chip, half_of, to):
        s = slab(b, chip, half_of)
        return pltpu.make_async_remote_copy(src_ref=s, dst_ref=s, send_sem=sems[0].at[w, k], recv_sem=sems[1].at[w, k],
                                            device_id=to, device_id_type=MESH)

    def start(_, outs, sems):
        x, y, c, chips = place()
        for w, b in enumerate(outs):
            for k, (cx, cy) in enumerate(chips):
                copy(b, sems, w, k, 2 * x + y, c, (cx, cy, c)).start()

    def mid(_, outs, sems):
        x, y, c, chips = place()
        for w, b in enumerate(outs):
            for k, (cx, cy) in enumerate(chips):
                copy(b, sems, w, k, 2 * cx + cy, c, (cx, cy, c)).wait_recv()
                copy(b, sems, w, 3 + k, 2 * cx + cy, c, (x, y, 1 - c)).start()

    def finish(_, outs, sems):
        x, y, c, chips = place()
        for w, b in enumerate(outs):
            for k, (cx, cy) in enumerate(chips):
                copy(b, sems, w, 3 + k, 2 * cx + cy, 1 - c, (x, y, 1 - c)).wait_recv()
        for w, b in enumerate(outs):
            for k, (cx, cy) in enumerate(chips):
                copy(b, sems, w, k, 2 * x + y, c, (cx, cy, c)).wait_send()
                copy(b, sems, w, 3 + k, 2 * cx + cy, c, (x, y, 1 - c)).wait_send()

    return _Host(ins=list(bufs), out_shapes=[jax.ShapeDtypeStruct(b.shape, b.dtype) for b in bufs],
                 aliases={i: i for i in range(n)},
                 sems=[pltpu.SemaphoreType.DMA((n, 6)), pltpu.SemaphoreType.DMA((n, 6))],
                 start=start, mid=mid, finish=finish)


def _run_host(host, name):
    n_in, n_out = len(host.ins), len(host.out_shapes)

    def body(*refs):
        ins, outs, sems = refs[:n_in], refs[n_in:n_in + n_out], refs[n_in + n_out:]
        host.start(ins, outs, sems)
        if host.mid is not None:
            host.mid(ins, outs, sems)
        host.finish(ins, outs, sems)

    return pl.pallas_call(
        body, name=name, in_specs=[ANY] * n_in, out_specs=[ANY] * n_out, out_shape=list(host.out_shapes),
        scratch_shapes=list(host.sems), input_output_aliases=dict(host.aliases),
    )(*host.ins)


def _peer(x, y, c, k):
    fx, fy, fc = (k >> 2) & 1, (k >> 1) & 1, k & 1
    return (x ^ fx if fx else x, y ^ fy if fy else y, c ^ fc if fc else c)


def _sub_shape(name, full_shape):
    r, c = full_shape
    return (r // 2, c // 4) if name in COL_SHARDED else (r // 8, c)


def _exchange_host(names, grads):
    n = len(names)
    shapes = [_sub_shape(k, g.shape) for k, g in zip(names, grads)]

    def copy(ins, outs, sems, w, k):
        x, y, c = _coords()
        px, py, pc = _peer(x, y, c, k)
        sr, sc = shapes[w]
        if names[w] in COL_SHARDED:
            col = _col_block(names[w], px, py) * sc
            src = ins[w].at[pl.ds(pl.multiple_of(pc * sr, 16), sr), pl.ds(pl.multiple_of(col, 128), sc)]
        else:
            src = ins[w].at[pl.ds(pl.multiple_of((4 * px + 2 * py + pc) * sr, 16), sr)]
        return pltpu.make_async_remote_copy(
            src_ref=src, dst_ref=outs[w].at[k - 1], send_sem=sems[0].at[w, k - 1], recv_sem=sems[1].at[w, k - 1],
            device_id=(px, py, pc), device_id_type=MESH)

    def start(ins, outs, sems):
        for w in range(n):
            for k in range(1, 8):
                copy(ins, outs, sems, w, k).start()

    def finish(ins, outs, sems):
        for w in range(n):
            for k in range(1, 8):
                copy(ins, outs, sems, w, k).wait_recv()
        for w in range(n):
            for k in range(1, 8):
                copy(ins, outs, sems, w, k).wait_send()

    return _Host(ins=list(grads), out_shapes=[jax.ShapeDtypeStruct((7,) + s, g.dtype) for s, g in zip(shapes, grads)],
                 aliases={}, sems=[pltpu.SemaphoreType.DMA((n, 7)), pltpu.SemaphoreType.DMA((n, 7))],
                 start=start, mid=None, finish=finish)


def _sibling_host(bufs):
    n = len(bufs)

    def copy(outs, sems, w, half_of):
        x, y, c = _coords()
        half = outs[w].shape[0] // 2
        rows = outs[w].at[pl.ds(pl.multiple_of(half_of * half, 8), half)]
        return pltpu.make_async_remote_copy(src_ref=rows, dst_ref=rows, send_sem=sems[0].at[w], recv_sem=sems[1].at[w],
                                            device_id=(x, y, 1 - c), device_id_type=MESH)

    def start(_, outs, sems):
        c = lax.axis_index("c")
        for w in range(n):
            copy(outs, sems, w, c).start()

    def finish(_, outs, sems):
        c = lax.axis_index("c")
        for w in range(n):
            copy(outs, sems, w, 1 - c).wait_recv()
        for w in range(n):
            copy(outs, sems, w, c).wait_send()

    return _Host(ins=list(bufs), out_shapes=[jax.ShapeDtypeStruct(b.shape, b.dtype) for b in bufs],
                 aliases={i: i for i in range(n)},
                 sems=[pltpu.SemaphoreType.DMA((n,)), pltpu.SemaphoreType.DMA((n,))], start=start, mid=None, finish=finish)


def _pack_host(pack):
    rows = pack.shape[0]

    def me_of():
        x, y, c = _coords()
        return x, y, c, 4 * x + 2 * y + c

    def copy(ins, outs, sems, k, slot):
        x, y, c, _ = me_of()
        return pltpu.make_async_remote_copy(src_ref=ins[0], dst_ref=outs[0].at[slot], send_sem=sems[0].at[k - 1],
                                            recv_sem=sems[1].at[k - 1], device_id=_peer(x, y, c, k), device_id_type=MESH)

    def start(ins, outs, sems):
        me = me_of()[3]
        pltpu.make_async_copy(ins[0], outs[0].at[me], sems[2]).start()
        for k in range(1, 8):
            copy(ins, outs, sems, k, me).start()

    def finish(ins, outs, sems):
        x, y, c, me = me_of()
        for k in range(1, 8):
            px, py, pc = _peer(x, y, c, k)
            copy(ins, outs, sems, k, 4 * px + 2 * py + pc).wait_recv()
        for k in range(1, 8):
            copy(ins, outs, sems, k, me).wait_send()
        pltpu.make_async_copy(ins[0], outs[0].at[me], sems[2]).wait()

    return _Host(ins=[pack], out_shapes=[jax.ShapeDtypeStruct((8, rows, 128), F32)], aliases={},
                 sems=[pltpu.SemaphoreType.DMA((7,)), pltpu.SemaphoreType.DMA((7,)), pltpu.SemaphoreType.DMA],
                 start=start, mid=None, finish=finish)


def _join_hosts(hosts):
    ins, outs, sems, aliases, spans = [], [], [], {}, []
    for h in hosts:
        spans.append((len(ins), len(h.ins), len(outs), len(h.out_shapes), len(sems), len(h.sems)))
        for i_in, i_out in h.aliases.items():
            aliases[len(ins) + i_in] = len(outs) + i_out
        ins, outs, sems = ins + list(h.ins), outs + list(h.out_shapes), sems + list(h.sems)

    def phase(which):
        fns = [getattr(h, which) for h in hosts]
        if all(f is None for f in fns):
            return None

        def run(i_refs, o_refs, s_refs):
            for f, (i0, n_i, o0, n_o, s0, n_s) in zip(fns, spans):
                if f is not None:
                    f(i_refs[i0:i0 + n_i], o_refs[o0:o0 + n_o], s_refs[s0:s0 + n_s])

        return run

    return _Host(ins, outs, aliases, sems, phase("start"), phase("mid"), phase("finish"))


def _allgather_pack(pack):
    rows = pack.shape[0]

    def body(p_ref, o_ref, send, recv, local):
        x, y, c = _coords()
        me = 4 * x + 2 * y + c
        mine = pltpu.make_async_copy(p_ref, o_ref.at[me], local)
        mine.start()

        def copy(k, slot):
            return pltpu.make_async_remote_copy(src_ref=p_ref, dst_ref=o_ref.at[slot], send_sem=send.at[k - 1],
                                                recv_sem=recv.at[k - 1], device_id=_peer(x, y, c, k), device_id_type=MESH)

        for k in range(1, 8):
            copy(k, me).start()
        for k in range(1, 8):
            px, py, pc = _peer(x, y, c, k)
            copy(k, 4 * px + 2 * py + pc).wait_recv()
        for k in range(1, 8):
            copy(k, me).wait_send()
        mine.wait()

    return pl.pallas_call(
        body, name="allgather_pack", in_specs=[pl.BlockSpec(memory_space=pltpu.VMEM)], out_specs=ANY,
        out_shape=jax.ShapeDtypeStruct((8, rows, 128), F32),
        scratch_shapes=[pltpu.SemaphoreType.DMA((7,)), pltpu.SemaphoreType.DMA((7,)), pltpu.SemaphoreType.DMA],
        compiler_params=pltpu.CompilerParams(vmem_limit_bytes=VMEM_LIMIT),
    )(pack)


SMALL = ("norm1_g", "conv_b", "rg_wa", "rg_ba", "rg_wx", "rg_bx", "rg_lambda", "hg_lb_logits", "hg_norm_g",
         "norm2_g", "norm_f_g")
SHARDED_SMALL = ("meta_tokens", "conv_w")
SEG = 1024


PACK_LATE = ("norm1_g", "meta_tokens", "loss")
PACK_EARLY = tuple(k for k in SMALL + SHARDED_SMALL if k not in PACK_LATE)


def _pack(vals, names):
    parts = []
    for k in names:
        v = vals[k].reshape(-1).astype(F32)
        parts.append(jnp.pad(v, (0, -v.shape[0] % SEG)))
    return jnp.concatenate(parts).reshape(-1, 128)


def _unpack(flat, shapes, names):
    out, off = {}, 0
    flat = flat.reshape(-1)
    for k in names:
        size = 1
        for s in shapes[k]:
            size *= s
        out[k] = flat[off:off + size].reshape(shapes[k])
        off += size + (-size % SEG)
    return out


ORDER = ("meta_tokens", "norm1_g", "w_in", "conv_w", "conv_b", "rg_wa", "rg_ba", "rg_wx", "rg_bx", "rg_lambda",
         "hg_lb_logits", "hg_norm_g", "w_proj_a", "w_proj_b", "w_out", "norm2_g", "w_ffn_in", "w_ffn_down", "norm_f_g")


def kernel(x, meta_tokens, norm1_g, w_in, conv_w, conv_b, rg_wa, rg_ba, rg_wx, rg_bx, rg_lambda, hg_lb_logits, hg_norm_g, w_proj_a, w_proj_b, w_out, norm2_g, w_ffn_in, w_ffn_down, norm_f_g, loss_target, m_meta_tokens, m_norm1_g, m_w_in, m_conv_w, m_conv_b, m_rg_wa, m_rg_ba, m_rg_wx, m_rg_bx, m_rg_lambda, m_hg_lb_logits, m_hg_norm_g, m_w_proj_a, m_w_proj_b, m_w_out, m_norm2_g, m_w_ffn_in, m_w_ffn_down, m_norm_f_g, v_meta_tokens, v_norm1_g, v_w_in, v_conv_w, v_conv_b, v_rg_wa, v_rg_ba, v_rg_wx, v_rg_bx, v_rg_lambda, v_hg_lb_logits, v_hg_norm_g, v_w_proj_a, v_w_proj_b, v_w_out, v_norm2_g, v_w_ffn_in, v_w_ffn_down, v_norm_f_g):
    args = dict(locals())
    w = {k: args[k] for k in ORDER}
    m = {k: args["m_" + k] for k in ORDER}
    v = {k: args["v_" + k] for k in ORDER}
    xi, yi, ci = _coords()
    chip = 2 * xi + yi
    slot = jnp.reshape(chip, (1,)).astype(jnp.int32)
    xyc = jnp.stack([xi, yi, ci]).astype(jnp.int32)

    def full_matrix(k, gth):
        if k == "w_ffn_in":
            return jnp.concatenate([gth[0], gth[2], gth[1], gth[3]], axis=1)
        if k in COL_SHARDED:
            return jnp.concatenate([gth[j] for j in range(4)], axis=1)
        return gth.reshape(-1, gth.shape[2])

    tiny = jnp.zeros((32, 384), F32)
    tiny = tiny.at[0:N_META, 0:256].set(meta_tokens).at[N_META:N_META + 4, 0:320].set(conv_w[0])
    slots = {k: _place_own(w[k][0], slot, BF16, "place_" + k) for k in BIG}
    w_in_all, tiny_all = _run_host(_gather_host([slots["w_in"], _place_own(tiny, slot, F32, "place_tiny")]),
                                   "allgather_w_in")
    later = [k for k in BIG if k != "w_in"]
    meta_full = jnp.transpose(tiny_all[:, 0:N_META, 0:256], (1, 0, 2)).reshape(N_META, D)
    conv_w_full = jnp.transpose(tiny_all[:, N_META:N_META + 4, 0:320], (1, 0, 2)).reshape(4, RGW)

    small = dict(norm1_g=norm1_g, conv_w=conv_w_full, conv_b=conv_b, rg_wa=rg_wa[0], rg_ba=rg_ba, rg_wx=rg_wx[0],
                 rg_bx=rg_bx, rg_lambda=rg_lambda, hg_lb_logits=hg_lb_logits, hg_norm_g=hg_norm_g,
                 norm2_g=norm2_g, norm_f_g=norm_f_g.reshape(1, D))

    hosts = {
        "mm_proj": lambda g: _gather_host([slots[k] for k in later]),
        "mm_d_act": lambda g: _exchange_host(["w_ffn_down"], [g["w_ffn_down"]]),
        "mm_d_hn2": lambda g: _exchange_host(["w_ffn_in"], [g["w_ffn_in"]]),
        "mm_d_win": lambda g: _join_hosts([
            _exchange_host(["w_out", "w_proj_a", "w_proj_b"], [g["w_out"], g["w_proj_a"], g["w_proj_b"]]),
            _pack_host(_pack(g, PACK_EARLY))]),
        "mm_d_hn1": lambda g: _exchange_host(["w_in"], [_unrotate_g_in(g["w_in"])]),
    }
    h0 = jnp.concatenate([jnp.zeros((PAD, D), F32), meta_full, x[0]], axis=0)
    loss_blk, grad_x, g, carried = _local_step(
        h0, loss_target[0], {"w_in": _rotate_w_in(full_matrix("w_in", w_in_all))}, small, hosts=hosts,
        finalize=lambda gathered: {k: full_matrix(k, gth) for k, gth in zip(later, gathered)})
    g["loss"] = loss_blk[0:1, 0:1]

    halves = {}
    sources = {"mm_d_act": ["w_ffn_down"], "mm_d_hn2": ["w_ffn_in"], "mm_d_win": ["w_out", "w_proj_a", "w_proj_b"],
               "mm_d_hn1": ["w_in"]}
    for name, keys in sources.items():
        partials, received = carried[name]
        for k, part, rec in zip(keys, partials, received):
            halves[k] = _sum_place(rec, part, k, xyc, "sum_" + k)
    g_big = dict(zip(BIG, _run_host(_sibling_host([halves[k] for k in BIG]), "sibling_swap")))
    small_shapes = {k: w[k].shape for k in SMALL}
    small_shapes["meta_tokens"], small_shapes["conv_w"], small_shapes["loss"] = (N_META, D), (1, 4, RGW), (1,)
    recv_packs = {PACK_EARLY: carried["mm_d_win"][1][3], PACK_LATE: _allgather_pack(_pack(g, PACK_LATE))}

    w_sm = {k: w[k] for k in SMALL}
    m_sm = {k: m[k] for k in SMALL}
    v_sm = {k: v[k] for k in SMALL}
    for d in (w_sm, m_sm, v_sm):
        d["meta_tokens"], d["conv_w"], d["loss"] = jnp.zeros((N_META, D), F32), jnp.zeros((1, 4, RGW), F32), jnp.zeros((1,), F32)
    outs_sm = [{}, {}, {}, {}]
    for names, rec in recv_packs.items():
        res = _adamw_small(rec, _pack(w_sm, names), _pack(m_sm, names), _pack(v_sm, names), "adamw_" + names[0])
        for acc, o in zip(outs_sm, res):
            acc.update(_unpack(o, small_shapes, names))
    grad, delta, new_m, new_v = ({k: o[k] for k in SMALL} for o in outs_sm)
    g_small_full = outs_sm[0]
    loss = g_small_full["loss"][0]
    g_shard = {"meta_tokens": lax.dynamic_slice(g_small_full["meta_tokens"], (0, chip * 256), (N_META, 256)),
               "conv_w": lax.dynamic_slice(g_small_full["conv_w"], (0, 0, chip * 320), (1, 4, 320))[0]}
    for k in BIG + SHARDED_SMALL:
        gk = g_big[k] if k in BIG else g_shard[k]
        wk, mk, vk = (d[k].reshape(gk.shape) for d in (w, m, v))
        dk, mk, vk = _adamw(wk, gk, mk, vk, "adamw_" + k)
        grad[k], delta[k], new_m[k], new_v[k] = (a.reshape(w[k].shape) for a in (gk, dk, mk, vk))

    return (loss, grad_x[None], *[grad[k] for k in ORDER], *[delta[k] for k in ORDER],
            *[new_m[k] for k in ORDER], *[new_v[k] for k in ORDER])
```

```python
import functools

import jax
import jax.numpy as jnp
from jax import lax
from jax.experimental import pallas as pl
from jax.experimental.pallas import tpu as pltpu

F32, BF16 = jnp.float32, jnp.bfloat16
D = 1024
N_META = 16
RGW = 1280
RG_BLOCKS, RG_BLOCK = 16, 80
RG_C = 8.0
HEADS, HD = 8, 128
HGW = HEADS * HD
DFF = 2816
D_IN = 2 * RGW + 6 * D
ROT = 2 * RGW
COL_PAD = 256
D_INP = D_IN + COL_PAD
EPS = 1e-6
CH = 64
TM = 256
ROW0 = TM
PAD = ROW0 - N_META
CT = RGW
RS = 16
NCT = RGW // CT
EXP_CLAMP = 80.0
VMEM_LIMIT = 56 * 1024 * 1024

LR, B1, B2, ADAM_EPS, WD, STEP = 0.001, 0.9, 0.999, 1e-08, 0.01, 10
MESH = pl.DeviceIdType.MESH
ANY = pl.BlockSpec(memory_space=pl.ANY)


def _cparams(sem):
    return pltpu.CompilerParams(dimension_semantics=sem, vmem_limit_bytes=VMEM_LIMIT)


def _pick(n, prefs):
    for p in prefs:
        if n % p == 0:
            return p
    return n


def _sig(x):
    return 0.5 * jnp.tanh(0.5 * x) + 0.5


def _dot(a, b, dims, precision=None):
    return lax.dot_general(a, b, (dims, ((), ())), preferred_element_type=F32, precision=precision)


NN, NT, TN_ = ((1,), (0,)), ((1,), (1,)), ((0,), (0,))


class _Host:
    def __init__(self, ins, out_shapes, aliases, sems, start, mid, finish):
        self.ins, self.out_shapes, self.aliases, self.sems = ins, out_shapes, aliases, sems
        self.start, self.mid, self.finish = start, mid, finish


class _Epi:
    def __init__(self, ins, in_specs, out_shapes, out_specs, fn):
        self.ins, self.in_specs, self.out_shapes, self.out_specs, self.fn = ins, in_specs, out_shapes, out_specs, fn


def _mm(a, b, mode, out_dtype, name, resid=None, host=None, epi=None, tiles=None, sparse=None):
    if mode == "nn":
        (m, kd), n = a.shape, b.shape[1]
    elif mode == "nt":
        (m, kd), n = a.shape, b.shape[0]
    else:
        (kd, m), n = a.shape, b.shape[1]
    if mode == "tn":
        tm = _pick(m, (1024, 1280, 1408, 640, 512, 256, 128))
        tk = _pick(kd, (1408, 768, 512, 256, 128))
    else:
        tm = _pick(m, (768, 512, 640, 256, 128))
        tk = kd if kd <= 2816 else _pick(kd, (1792, 1408, 1024, 512, 256, 128))
    tn = _pick(n, (1792, 1408, 1280, 1024, 512, 256, 128))
    if tiles is not None:
        tm, tn = tiles
    a_map = (lambda i, j, k: (k, i)) if mode == "tn" else (lambda i, j, k: (i, k))
    b_map = (lambda i, j, k: (j, k)) if mode == "nt" else (lambda i, j, k: (k, j))
    o_map = lambda i, j, k: (i, j)
    if sparse is not None:
        (ni, nj, nk), (tm, tn, tk), (a_map, b_map, o_map) = sparse
    else:
        assert m % tm == 0 and n % tn == 0 and kd % tk == 0, (name, m, n, kd, tm, tn, tk)
        ni, nj, nk = m // tm, n // tn, kd // tk
    dims = {"nn": NN, "nt": NT, "tn": TN_}[mode]

    n_hin = len(host.ins) if host else 0
    n_hout = len(host.out_shapes) if host else 0
    n_res = 0 if resid is None else 1
    n_ein = len(epi.ins) if epi else 0
    n_out = len(epi.out_shapes) if epi else 1

    def finish(r, r_ref, e_in, o_refs):
        if resid is not None:
            r = r + r_ref[...]
        if epi:
            epi.fn(r, e_in, o_refs)
        else:
            o_refs[0][...] = r.astype(out_dtype)

    def body(*refs):
        a_ref, b_ref = refs[:2]
        r_ref = refs[2] if resid is not None else None
        pos = 2 + n_res
        e_in = refs[pos:pos + n_ein]
        pos += n_ein
        h_in = refs[pos:pos + n_hin]
        pos += n_hin
        o_ref = refs[pos:pos + n_out]
        pos += n_out
        h_out = refs[pos:pos + n_hout]
        scratch = refs[pos + n_hout:]
        h_sems = scratch[(1 if nk > 1 else 0):]
        i, j, k = pl.program_id(0), pl.program_id(1), pl.program_id(2)
        if host:
            @pl.when((i == 0) & (j == 0) & (k == 0))
            def _():
                host.start(h_in, h_out, h_sems)

            if host.mid is not None:
                @pl.when((i == (3 * ni) // 4) & (j == 0) & (k == 0))
                def _():
                    host.mid(h_in, h_out, h_sems)

        if nk == 1:
            finish(_dot(a_ref[...], b_ref[...], dims), r_ref, e_in, o_ref)
        else:
            acc = scratch[0]

            @pl.when(k == 0)
            def _():
                acc[...] = jnp.zeros_like(acc)

            acc[...] += _dot(a_ref[...], b_ref[...], dims)

            @pl.when(k == nk - 1)
            def _():
                finish(acc[...], r_ref, e_in, o_ref)

        if host:
            @pl.when((i == ni - 1) & (j == nj - 1) & (k == nk - 1))
            def _():
                host.finish(h_in, h_out, h_sems)

    a_spec = pl.BlockSpec((tk, tm) if mode == "tn" else (tm, tk), a_map)
    b_spec = pl.BlockSpec((tn, tk) if mode == "nt" else (tk, tn), b_map)
    o_spec = pl.BlockSpec((tm, tn), o_map)
    in_specs, args = [a_spec, b_spec], [a, b]
    if resid is not None:
        in_specs.append(o_spec)
        args.append(resid)
    out_shapes, out_specs = [jax.ShapeDtypeStruct((m, n), out_dtype)], [o_spec]
    if epi:
        in_specs += list(epi.in_specs)
        args += list(epi.ins)
        out_shapes, out_specs = list(epi.out_shapes), list(epi.out_specs)
    scratch = [pltpu.VMEM((tm, tn), F32)] if nk > 1 else []
    if not host:
        outs = pl.pallas_call(
            body, name=name, grid=(ni, nj, nk), in_specs=in_specs, out_specs=out_specs, out_shape=out_shapes,
            scratch_shapes=scratch, compiler_params=_cparams(("parallel", "parallel", "arbitrary")),
        )(*args)
        return list(outs) if epi else outs[0]
    outs = pl.pallas_call(
        body, name=name, grid=(ni, nj, nk),
        in_specs=in_specs + [ANY] * n_hin, out_specs=out_specs + [ANY] * n_hout,
        out_shape=out_shapes + list(host.out_shapes),
        scratch_shapes=scratch + list(host.sems),
        input_output_aliases={2 + n_res + n_ein + i_in: n_out + i_out for i_in, i_out in host.aliases.items()},
        compiler_params=_cparams(("arbitrary", "arbitrary", "arbitrary")),
    )(*args, *host.ins)
    return (list(outs[:n_out]) if epi else outs[0]), list(outs[n_out:])


def _rms(x):
    r = lax.rsqrt(jnp.mean(x * x, axis=-1, keepdims=True) + EPS)
    return x * r, r


def _rmsnorm_fwd(h, g, name):
    t = h.shape[0]

    def body(h_ref, g_ref, o_ref):
        xh, _ = _rms(h_ref[...])
        o_ref[...] = (xh * g_ref[...]).astype(BF16)

    return pl.pallas_call(
        body, name=name, grid=(t // TM,),
        in_specs=[pl.BlockSpec((TM, D), lambda i: (i, 0)), pl.BlockSpec((1, D), lambda i: (0, 0))],
        out_specs=pl.BlockSpec((TM, D), lambda i: (i, 0)),
        out_shape=jax.ShapeDtypeStruct((t, D), BF16),
        compiler_params=_cparams(("parallel",)),
    )(h, g)


def _rms_bwd_math(dhn, x, g):
    xh, r = _rms(x)
    dxh = dhn * g
    dx = r * (dxh - xh * jnp.mean(dxh * xh, axis=-1, keepdims=True))
    return dx, jnp.sum(dhn * xh, axis=0, keepdims=True)


def _final_loss(h2, gf, target):
    t = h2.shape[0]

    def body(h_ref, g_ref, t_ref, dh_ref, dhb_ref, loss_ref, dg_ref):
        i = pl.program_id(0)

        @pl.when(i == 0)
        def _():
            loss_ref[...] = jnp.zeros_like(loss_ref)
            dg_ref[...] = jnp.zeros_like(dg_ref)
            dh_ref[...] = jnp.zeros_like(dh_ref)
            dhb_ref[...] = jnp.zeros_like(dhb_ref)

        @pl.when(i > 0)
        def _():
            x = h_ref[...]
            g = g_ref[...]
            xh, _ = _rms(x)
            err = xh * g - t_ref[...]
            loss_ref[...] += 0.5 * jnp.sum(jnp.sum(err * err, axis=-1, keepdims=True) * (1.0 / D))
            dx, dg = _rms_bwd_math(err * (1.0 / D), x, g)
            dh_ref[...] = dx
            dhb_ref[...] = dx.astype(BF16)
            dg_ref[...] += dg

    row = pl.BlockSpec((TM, D), lambda i: (i, 0))
    return pl.pallas_call(
        body, name="final_loss", grid=(t // TM,),
        in_specs=[row, pl.BlockSpec((1, D), lambda i: (0, 0)),
                  pl.BlockSpec((TM, D), lambda i: (jnp.maximum(i - 1, 0), 0))],
        out_specs=[row, row, pl.BlockSpec((8, 128), lambda i: (0, 0)), pl.BlockSpec((1, D), lambda i: (0, 0))],
        out_shape=[jax.ShapeDtypeStruct((t, D), F32), jax.ShapeDtypeStruct((t, D), BF16),
                   jax.ShapeDtypeStruct((8, 128), F32), jax.ShapeDtypeStruct((1, D), F32)],
        compiler_params=_cparams(("arbitrary",)),
    )(h2, gf, target)


def _rmsnorm_bwd(dhn, h, g, dres, name):
    t = h.shape[0]

    def body(dhn_ref, h_ref, g_ref, dres_ref, dh_ref, dhb_ref, dg_ref):
        @pl.when(pl.program_id(0) == 0)
        def _():
            dg_ref[...] = jnp.zeros_like(dg_ref)

        dx, dg = _rms_bwd_math(dhn_ref[...].astype(F32), h_ref[...], g_ref[...])
        dh = dres_ref[...] + dx
        dh_ref[...] = dh
        dhb_ref[...] = dh.astype(BF16)
        dg_ref[...] += dg

    row = pl.BlockSpec((TM, D), lambda i: (i, 0))
    vec = pl.BlockSpec((1, D), lambda i: (0, 0))
    return pl.pallas_call(
        body, name=name, grid=(t // TM,),
        in_specs=[row, row, vec, row], out_specs=[row, row, vec],
        out_shape=[jax.ShapeDtypeStruct((t, D), F32), jax.ShapeDtypeStruct((t, D), BF16),
                   jax.ShapeDtypeStruct((1, D), F32)],
        compiler_params=_cparams(("arbitrary",)),
    )(dhn, h, g, dres)


def _rmsnorm1_bwd(dhn, h, g, dres):
    t = h.shape[0]

    def body(dhn_ref, h_ref, g_ref, dres_ref, dx_ref, dmeta_ref, dg_ref):
        i = pl.program_id(0)

        @pl.when(i == 0)
        def _():
            dg_ref[...] = jnp.zeros_like(dg_ref)

        dx, dg = _rms_bwd_math(dhn_ref[...].astype(F32), h_ref[...], g_ref[...])
        dh = dres_ref[...] + dx
        dg_ref[...] += dg
        dx_ref[...] = dh

        @pl.when(i == 0)
        def _():
            dmeta_ref[...] = dh[PAD:, :]

    row = pl.BlockSpec((TM, D), lambda i: (i, 0))
    vec = pl.BlockSpec((1, D), lambda i: (0, 0))
    return pl.pallas_call(
        body, name="rmsnorm1_bwd", grid=(t // TM,),
        in_specs=[row, row, vec, row],
        out_specs=[pl.BlockSpec((TM, D), lambda i: (jnp.maximum(i - 1, 0), 0)),
                   pl.BlockSpec((N_META, D), lambda i: (0, 0)), vec],
        out_shape=[jax.ShapeDtypeStruct((t - ROW0, D), F32), jax.ShapeDtypeStruct((N_META, D), F32),
                   jax.ShapeDtypeStruct((1, D), F32)],
        compiler_params=_cparams(("arbitrary",)),
    )(dhn, h, g, dres)


FH = DFF // 2
FM = 384


def _swiglu_epi(t):
    def fn(r, _, outs):
        g, u = r[:, :FH], r[:, FH:]
        outs[0][...] = r.astype(BF16)
        outs[1][...] = (g * _sig(g) * u).astype(BF16)

    return _Epi(ins=[], in_specs=[],
                out_shapes=[jax.ShapeDtypeStruct((t, 2 * DFF), BF16), jax.ShapeDtypeStruct((t, DFF), BF16)],
                out_specs=[pl.BlockSpec((FM, 2 * FH), lambda i, j, k: (i, j)), pl.BlockSpec((FM, FH), lambda i, j, k: (i, j))],
                fn=fn)


def _swiglu_bwd_epi(gu):
    def fn(d, ins, outs):
        gu_t = ins[0][...].astype(F32)
        g, u = gu_t[:, :FH], gu_t[:, FH:]
        s = _sig(g)
        outs[0][:, :FH] = (d * u * (s * (1.0 + g * (1.0 - s)))).astype(BF16)
        outs[0][:, FH:] = (d * (g * s)).astype(BF16)

    spec = pl.BlockSpec((FM, 2 * FH), lambda i, j, k: (i, j))
    return _Epi(ins=[gu], in_specs=[spec], out_shapes=[jax.ShapeDtypeStruct(gu.shape, BF16)], out_specs=[spec], fn=fn)


COL_GA, COL_GB = 4, 5
COL_AX, COL_AG = (6 * D + COL_PAD) // CT, (6 * D + COL_PAD + RGW) // CT


def _merge_fwd(pa, pb, proj):
    t = pa.shape[0]

    def body(pa_ref, pb_ref, ga_ref, gb_ref, o_ref):
        ga, gb, pa, pb = (r[...].astype(F32) for r in (ga_ref, gb_ref, pa_ref, pb_ref))
        o_ref[...] = (_sig(ga) * pa + _sig(gb) * pb).astype(BF16)

    row = pl.BlockSpec((TM, D), lambda i: (i, 0))
    return pl.pallas_call(
        body, name="merge_fwd", grid=(t // TM,),
        in_specs=[row, row, pl.BlockSpec((TM, D), lambda i: (i, COL_GA)), pl.BlockSpec((TM, D), lambda i: (i, COL_GB))],
        out_specs=row, out_shape=jax.ShapeDtypeStruct((t, D), BF16),
        compiler_params=_cparams(("parallel",)),
    )(pa, pb, proj, proj)


def _merge_bwd(dm, pa, pb, proj):
    t = pa.shape[0]

    def body(dm_ref, pa_ref, pb_ref, ga_ref, gb_ref, dpa_ref, dpb_ref, dproj_ref):
        d, ga, gb, pa, pb = (r[...].astype(F32) for r in (dm_ref, ga_ref, gb_ref, pa_ref, pb_ref))
        sa, sb = _sig(ga), _sig(gb)
        dpa_ref[...] = (sa * d).astype(BF16)
        dpb_ref[...] = (sb * d).astype(BF16)
        dproj_ref[:, :D] = (d * pa * sa * (1.0 - sa)).astype(BF16)
        dproj_ref[:, D:] = (d * pb * sb * (1.0 - sb)).astype(BF16)

    row = pl.BlockSpec((TM, D), lambda i: (i, 0))
    return pl.pallas_call(
        body, name="merge_bwd", grid=(t // TM,),
        in_specs=[row, row, row, pl.BlockSpec((TM, D), lambda i: (i, COL_GA)), pl.BlockSpec((TM, D), lambda i: (i, COL_GB))],
        out_specs=[row, row, pl.BlockSpec((TM, 2 * D), lambda i: (i, 2))],
        out_shape=[jax.ShapeDtypeStruct((t, D), BF16), jax.ShapeDtypeStruct((t, D), BF16),
                   jax.ShapeDtypeStruct((t, D_INP), BF16)],
        compiler_params=_cparams(("parallel",)),
    )(dm, pa, pb, proj, proj)


def _zero_pad_cols(dproj):
    t = dproj.shape[0]
    tz = _pick(t, (768, 256))

    def body(_, o_ref):
        o_ref[...] = jnp.zeros_like(o_ref)

    return pl.pallas_call(
        body, name="dproj_pad", grid=(t // tz,), in_specs=[ANY],
        out_specs=pl.BlockSpec((tz, COL_PAD), lambda i: (i, (6 * D) // COL_PAD)),
        out_shape=jax.ShapeDtypeStruct(dproj.shape, BF16), input_output_aliases={0: 0},
        compiler_params=_cparams(("parallel",)),
    )(dproj)


HALO = 16


def _prev_halo(col0):
    return pl.BlockSpec((HALO, CT), lambda c, i: (jnp.maximum(i * (TM // HALO) - 1, 0), col0 + c))


def _conv_fwd(proj, w, b):
    t = proj.shape[0]

    def body(x_ref, halo_ref, w_ref, b_ref, o_ref, ob_ref, buf):
        i = pl.program_id(1)
        x = x_ref[...].astype(F32)
        buf[0:HALO, :] = jnp.where(i > 0, halo_ref[...].astype(F32), 0.0)
        buf[HALO:, :] = x
        wv = w_ref[...]
        y = b_ref[...] + wv[3:4, :] * x
        for j in range(3):
            y = y + wv[j:j + 1, :] * buf[pl.ds(HALO - 3 + j, TM), :]
        o_ref[...] = y
        ob_ref[...] = y.astype(BF16)

    blk = pl.BlockSpec((TM, CT), lambda c, i: (i, c))
    return pl.pallas_call(
        body, name="conv_fwd", grid=(NCT, t // TM),
        in_specs=[pl.BlockSpec((TM, CT), lambda c, i: (i, COL_AX + c)), _prev_halo(COL_AX),
                  pl.BlockSpec((4, CT), lambda c, i: (0, c)), pl.BlockSpec((1, CT), lambda c, i: (0, c))],
        out_specs=[blk, blk],
        out_shape=[jax.ShapeDtypeStruct((t, RGW), F32), jax.ShapeDtypeStruct((t, RGW), BF16)],
        scratch_shapes=[pltpu.VMEM((TM + HALO, CT), F32)],
        compiler_params=_cparams(("parallel", "parallel")),
    )(proj, proj, w, b)


def _conv_bwd(dxc, proj, w, dproj):
    t = proj.shape[0]
    nt = t // TM

    def body(d_ref, dn_ref, x_ref, halo_ref, w_ref, _, dx_ref, dw_ref, db_ref, bufd, bufx):
        i = pl.program_id(1)

        @pl.when(i == 0)
        def _():
            dw_ref[...] = jnp.zeros_like(dw_ref)
            db_ref[...] = jnp.zeros_like(db_ref)

        d = d_ref[...]
        bufd[0:TM, :] = d
        bufd[TM:, :] = jnp.where(i < nt - 1, dn_ref[...], 0.0)
        x = x_ref[...].astype(F32)
        bufx[0:HALO, :] = jnp.where(i > 0, halo_ref[...].astype(F32), 0.0)
        bufx[HALO:, :] = x
        wv = w_ref[...]
        dx = wv[3:4, :] * d
        for j in range(3):
            dx = dx + wv[j:j + 1, :] * bufd[pl.ds(3 - j, TM), :]
            dw_ref[j:j + 1, :] += jnp.sum(d * bufx[pl.ds(HALO - 3 + j, TM), :], axis=0, keepdims=True)
        dw_ref[3:4, :] += jnp.sum(d * x, axis=0, keepdims=True)
        db_ref[...] += jnp.sum(d, axis=0, keepdims=True)
        dx_ref[...] = dx.astype(BF16)

    return pl.pallas_call(
        body, name="conv_bwd", grid=(NCT, nt),
        in_specs=[pl.BlockSpec((TM, CT), lambda c, i: (i, c)),
                  pl.BlockSpec((8, CT), lambda c, i: (jnp.minimum((i + 1) * (TM // 8), t // 8 - 1), c)),
                  pl.BlockSpec((TM, CT), lambda c, i: (i, COL_AX + c)), _prev_halo(COL_AX),
                  pl.BlockSpec((4, CT), lambda c, i: (0, c)), ANY],
        out_specs=[pl.BlockSpec((TM, CT), lambda c, i: (i, COL_AX + c)),
                   pl.BlockSpec((4, CT), lambda c, i: (0, c)), pl.BlockSpec((1, CT), lambda c, i: (0, c))],
        out_shape=[jax.ShapeDtypeStruct((t, D_INP), BF16), jax.ShapeDtypeStruct((4, RGW), F32),
                   jax.ShapeDtypeStruct((1, RGW), F32)],
        scratch_shapes=[pltpu.VMEM((TM + 8, CT), F32), pltpu.VMEM((TM + HALO, CT), F32)],
        input_output_aliases={5: 0},
        compiler_params=_cparams(("parallel", "arbitrary")),
    )(dxc, dxc, proj, proj, w, dproj)


def _gelu(x):
    c = 0.7978845608028654
    th = jnp.tanh(c * (x + 0.044715 * x * x * x))
    return 0.5 * x * (1.0 + th), th


def _rg_gates(gr, gi, xc, ba, bx, lam, row0):
    r = _sig(gr + ba)
    ig = _sig(gi + bx)
    sp = jax.nn.softplus(-lam)
    a = jnp.exp(-RG_C * r * sp)
    s2 = jnp.maximum(1.0 - a * a, 1e-30)
    rs = lax.rsqrt(s2)
    s = s2 * rs
    rows = row0 + lax.broadcasted_iota(jnp.int32, gr.shape, 0)
    live = rows >= PAD
    u = jnp.where(live, s * ig * xc, 0.0)
    return r, ig, sp, a, s, rs, u, live


def _rg_fwd(gri, xc, proj, ba, bx, lam):
    t = gri.shape[0]

    def body(gr_ref, gi_ref, xc_ref, ag_ref, ba_ref, bx_ref, lam_ref, h_ref, ya_ref, hc):
        i = pl.program_id(1)

        @pl.when(i == 0)
        def _():
            hc[...] = jnp.zeros_like(hc)

        ba, bx, lam = ba_ref[...], bx_ref[...], lam_ref[...]
        sub = lax.broadcasted_iota(jnp.int32, (8, CT), 0)

        def strip(k, h):
            off = pl.multiple_of(k * RS, RS)
            rows = pl.ds(off, RS)
            _, _, _, a, _, _, u, _ = _rg_gates(gr_ref[rows, :], gi_ref[rows, :], xc_ref[rows, :], ba, bx, lam,
                                               i * TM + off)
            outs = []
            for half in range(RS // 8):
                out = jnp.zeros((8, CT), F32)
                for r in range(8):
                    h = a[8 * half + r:8 * half + r + 1, :] * h + u[8 * half + r:8 * half + r + 1, :]
                    out = jnp.where(sub == r, h, out)
                outs.append(out)
            hs = jnp.concatenate(outs, axis=0)
            h_ref[rows, :] = hs
            ge, _ = _gelu(ag_ref[rows, :].astype(F32))
            ya_ref[rows, :] = (hs * ge).astype(BF16)
            return h

        hc[...] = lax.fori_loop(0, TM // RS, strip, hc[...])

    blk_ = pl.BlockSpec((TM, CT), lambda c, i: (i, c))
    vec = pl.BlockSpec((1, CT), lambda c, i: (0, c))
    return pl.pallas_call(
        body, name="rg_fwd", grid=(NCT, t // TM),
        in_specs=[blk_, pl.BlockSpec((TM, CT), lambda c, i: (i, NCT + c)), blk_,
                  pl.BlockSpec((TM, CT), lambda c, i: (i, COL_AG + c)), vec, vec, vec],
        out_specs=[blk_, blk_],
        out_shape=[jax.ShapeDtypeStruct((t, RGW), F32), jax.ShapeDtypeStruct((t, RGW), BF16)],
        scratch_shapes=[pltpu.VMEM((1, CT), F32)],
        compiler_params=_cparams(("parallel", "arbitrary")),
    )(gri, gri, xc, proj, ba, bx, lam)


def _rg_bwd(dya, h, gri, xc, proj, ba, bx, lam, dproj):
    t = gri.shape[0]
    nt = t // TM
    assert NCT == 1

    def body(dya_ref, h_ref, hh_ref, gr_ref, gi_ref, xc_ref, ag_ref, ba_ref, bx_ref, lam_ref, _,
             dgri_ref, dxc_ref, dag_ref, dba_ref, dbx_ref, dlam_ref, hbuf, acc, cc):
        i = pl.program_id(1)
        ri = nt - 1 - i

        @pl.when(i == 0)
        def _():
            cc[...] = jnp.zeros_like(cc)
            dba_ref[...] = jnp.zeros_like(dba_ref)
            dbx_ref[...] = jnp.zeros_like(dbx_ref)
            dlam_ref[...] = jnp.zeros_like(dlam_ref)

        acc[...] = jnp.zeros_like(acc)
        ba, bx, lam = ba_ref[...], bx_ref[...], lam_ref[...]
        halo_last = jnp.where(ri > 0, hh_ref[7:8, :], 0.0)
        sub = lax.broadcasted_iota(jnp.int32, (8, CT), 0)
        c0 = 0.7978845608028654

        def strip(kk, c):
            k = TM // RS - 1 - kk
            off = pl.multiple_of(k * RS, RS)
            rows = pl.ds(off, RS)
            xc, hv = xc_ref[rows, :], h_ref[rows, :]
            ag, dya = ag_ref[rows, :].astype(F32), dya_ref[rows, :].astype(F32)
            r, ig, sp, a, s, rs, _, live = _rg_gates(gr_ref[rows, :], gi_ref[rows, :], xc, ba, bx, lam, ri * TM + off)
            ge, th = _gelu(ag)
            dge = 0.5 * (1.0 + th) + 0.5 * ag * (1.0 - th * th) * c0 * (1.0 + 3.0 * 0.044715 * ag * ag)
            dag_ref[rows, :] = (dya * hv * dge).astype(BF16)
            d = dya * ge
            outs = []
            for half in range(RS // 8 - 1, -1, -1):
                out = jnp.zeros((8, CT), F32)
                for rr in range(7, -1, -1):
                    g = d[8 * half + rr:8 * half + rr + 1, :] + c
                    c = a[8 * half + rr:8 * half + rr + 1, :] * g
                    out = jnp.where(sub == rr, g, out)
                outs.insert(0, out)
            g = jnp.concatenate(outs, axis=0)
            before = h_ref[pl.ds(pl.multiple_of(jnp.maximum(off - RS, 0), RS), RS), :]
            hbuf[7:8, :] = jnp.where(k == 0, halo_last, before[RS - 1:RS, :])
            hbuf[8:, :] = hv
            hprev = hbuf[pl.ds(7, RS), :]
            du = jnp.where(live, g, 0.0)
            ds = du * (ig * xc)
            dm = du * s
            dla = (g * hprev - a * ds * rs) * a
            dr = dla * (-RG_C * sp) * r * (1.0 - r)
            di = dm * xc * ig * (1.0 - ig)
            dgri_ref[rows, :CT] = dr.astype(BF16)
            dgri_ref[rows, CT:] = di.astype(BF16)
            dxc_ref[rows, :] = dm * ig
            dl = dla * (-RG_C * r)
            for half in range(RS // 8):
                part = slice(8 * half, 8 * half + 8)
                acc[0] += dr[part]
                acc[1] += di[part]
                acc[2] += dl[part]
            return c

        cc[...] = lax.fori_loop(0, TM // RS, strip, cc[...])
        dba_ref[...] += jnp.sum(acc[0], axis=0, keepdims=True)
        dbx_ref[...] += jnp.sum(acc[1], axis=0, keepdims=True)
        dlam_ref[...] += jnp.sum(acc[2], axis=0, keepdims=True) * (-_sig(-lam))

    rblk = pl.BlockSpec((TM, CT), lambda c, i: (nt - 1 - i, c))
    vec = pl.BlockSpec((1, CT), lambda c, i: (0, c))
    hh = pl.BlockSpec((8, CT), lambda c, i: (jnp.maximum((nt - 1 - i) * (TM // 8) - 1, 0), c))
    agb = pl.BlockSpec((TM, CT), lambda c, i: (nt - 1 - i, COL_AG + c))
    return pl.pallas_call(
        body, name="rg_bwd", grid=(NCT, nt),
        in_specs=[rblk, rblk, hh, rblk, pl.BlockSpec((TM, CT), lambda c, i: (nt - 1 - i, NCT + c)), rblk, agb,
                  vec, vec, vec, ANY],
        out_specs=[pl.BlockSpec((TM, 2 * CT), lambda c, i: (nt - 1 - i, c)), rblk, agb, vec, vec, vec],
        out_shape=[jax.ShapeDtypeStruct((t, 2 * RGW), BF16),
                   jax.ShapeDtypeStruct((t, RGW), F32), jax.ShapeDtypeStruct((t, D_INP), BF16),
                   jax.ShapeDtypeStruct((1, RGW), F32), jax.ShapeDtypeStruct((1, RGW), F32),
                   jax.ShapeDtypeStruct((1, RGW), F32)],
        scratch_shapes=[pltpu.VMEM((RS + 8, CT), F32), pltpu.VMEM((3, 8, CT), F32), pltpu.VMEM((1, CT), F32)],
        input_output_aliases={10: 2},
        compiler_params=_cparams(("parallel", "arbitrary")),
    )(dya, h, h, gri, gri, xc, proj, ba, bx, lam, dproj)


NCH = TM // CH
HI = lax.Precision.HIGHEST


def _tri_dot(tri, x):
    hi = x.astype(BF16)
    r1 = x - hi.astype(F32)
    mid = r1.astype(BF16)
    lo = (r1 - mid.astype(F32)).astype(BF16)
    return _dot(tri, hi, NN) + _dot(tri, mid, NN) + _dot(tri, lo, NN)


def _hg_chunk(qr, fr, lb):
    sf = _sig(fr)
    fg = lb + (1.0 - lb) * sf
    k = (1.0 - lb) * (1.0 - sf)
    sq = _sig(qr)
    q = qr * sq
    ri = lax.broadcasted_iota(jnp.int32, (CH, CH), 0)
    ci = lax.broadcasted_iota(jnp.int32, (CH, CH), 1)
    b = _tri_dot((ri >= ci).astype(BF16), jnp.log(fg))
    bm, bl = b[CH // 2 - 1:CH // 2, :], b[CH - 1:CH, :]
    eb = jnp.exp(b)
    ebm = jnp.exp(b - bm)
    ekm = jnp.exp(jnp.minimum(bm - b, EXP_CLAMP))
    ekl = jnp.exp(bl - b)
    return dict(sf=sf, fg=fg, k=k, sq=sq, q=q, eb=eb, ebm=ebm, ekm=ekm, ekl=ekl, ebl=jnp.exp(bl),
                qe=q * eb, qh=q * ebm, kh=k * ekm, kd=k * ekl, causal=ri >= ci, upper=(ci >= ri).astype(BF16))


def _hgrn_fwd(proj, lbl, gn):
    t = proj.shape[0]
    nt = t // TM

    def body(q_ref, f_ref, v_ref, g_ref, lbl_ref, gn_ref, yb_ref, o_ref, st_ref, st):
        @pl.when(pl.program_id(0) == 0)
        def _():
            st[...] = jnp.zeros_like(st)

        l = lbl_ref[...]
        lb = _sig(l[0:1, :] - l[1:2, :])
        gnv = gn_ref[...]

        def chunk(c, carry):
            off = pl.multiple_of(c * CH, CH)
            rows = pl.ds(off, CH)
            z = _hg_chunk(q_ref[rows, :].astype(F32), f_ref[rows, :].astype(F32), lb)
            v, gg = v_ref[rows, :], g_ref[rows, :].astype(F32)
            for hh in range(HEADS):
                sl = slice(hh * HD, (hh + 1) * HD)
                s_prev = st[hh]
                st_ref[c, hh] = s_prev
                vb = v[:, sl].astype(BF16)
                att = jnp.where(z["causal"], _dot(z["qh"][:, sl].astype(BF16), z["kh"][:, sl].astype(BF16), NT), 0.0)
                o = _dot(z["qe"][:, sl].astype(BF16), s_prev.astype(BF16), NT) + _dot(att.astype(BF16), vb, NN)
                st[hh] = s_prev * z["ebl"][:, sl] + _dot(vb, z["kd"][:, sl].astype(BF16), TN_)
                xh, _ = _rms(o)
                gh = gg[:, sl]
                o_ref[rows, sl] = o
                yb_ref[rows, sl] = (xh * gnv * gh * _sig(gh)).astype(BF16)
            return carry

        lax.fori_loop(0, NCH, chunk, 0)

    def col(j):
        return pl.BlockSpec((TM, HGW), lambda i, j=j: (i, j))

    return pl.pallas_call(
        body, name="hgrn_fwd", grid=(nt,),
        in_specs=[col(0), col(1), col(2), col(3), pl.BlockSpec((2, HGW), lambda i: (0, 0)),
                  pl.BlockSpec((1, HD), lambda i: (0, 0))],
        out_specs=[col(0), col(0), pl.BlockSpec((NCH, HEADS, HD, HD), lambda i: (i, 0, 0, 0))],
        out_shape=[jax.ShapeDtypeStruct((t, HGW), BF16), jax.ShapeDtypeStruct((t, HGW), F32),
                   jax.ShapeDtypeStruct((t // CH, HEADS, HD, HD), F32)],
        scratch_shapes=[pltpu.VMEM((HEADS, HD, HD), F32)],
        compiler_params=_cparams(("arbitrary",)),
    )(proj, proj, proj, proj, lbl, gn)


def _hgrn_bwd(dyb, proj, o, states, lbl, gn, dproj):
    t = proj.shape[0]
    nt = t // TM

    def body(dy_ref, q_ref, f_ref, v_ref, g_ref, o_ref, st_ref, lbl_ref, gn_ref, _,
             dp_ref, dgn_ref, dl_ref, dst, dlb):
        i = pl.program_id(0)

        @pl.when(i == 0)
        def _():
            dst[...] = jnp.zeros_like(dst)
            dlb[...] = jnp.zeros_like(dlb)
            dgn_ref[...] = jnp.zeros_like(dgn_ref)

        l = lbl_ref[...]
        lb = _sig(l[0:1, :] - l[1:2, :])
        gnv = gn_ref[...]
        last = lax.broadcasted_iota(jnp.int32, (CH, HD), 0) == CH - 1

        def chunk(cc, carry):
            c = NCH - 1 - cc
            off = pl.multiple_of(c * CH, CH)
            rows = pl.ds(off, CH)
            qr, fr = q_ref[rows, :].astype(F32), f_ref[rows, :].astype(F32)
            z = _hg_chunk(qr, fr, lb)
            v, gg, ov, dy = v_ref[rows, :], g_ref[rows, :].astype(F32), o_ref[rows, :], dy_ref[rows, :].astype(F32)
            dqs, dks, dbs, dvs, dgs = [], [], [], [], []
            dgn = jnp.zeros((1, HD), F32)
            for hh in range(HEADS):
                sl = slice(hh * HD, (hh + 1) * HD)
                s_prev, ds_new = st_ref[c, hh], dst[hh]
                qe, qh, kh, kd = z["qe"][:, sl], z["qh"][:, sl], z["kh"][:, sl], z["kd"][:, sl]
                ebl = z["ebl"][:, sl]
                gh, dyh = gg[:, sl], dy[:, sl]
                xh, rr = _rms(ov[:, sl])
                sg = _sig(gh)
                dyn = dyh * (gh * sg)
                dgs.append(dyh * (xh * gnv) * (sg * (1.0 + gh * (1.0 - sg))))
                dgn = dgn + jnp.sum(dyn * xh, axis=0, keepdims=True)
                dxh = dyn * gnv
                do = (rr * (dxh - xh * jnp.mean(dxh * xh, axis=-1, keepdims=True))).astype(BF16)
                vb, dsb = v[:, sl].astype(BF16), ds_new.astype(BF16)
                qeb, qhb, khb, kdb = (a.astype(BF16) for a in (qe, qh, kh, kd))
                att = jnp.where(z["causal"], _dot(qhb, khb, NT), 0.0).astype(BF16)
                datt = jnp.where(z["causal"], _dot(do, vb, NT), 0.0).astype(BF16)
                dvs.append(_dot(att, do, TN_) + _dot(kdb, dsb, NT))
                dqe = _dot(do, s_prev.astype(BF16), NN)
                dqh = _dot(datt, khb, NN)
                dkh = _dot(datt, qhb, TN_)
                dkd = _dot(vb, dsb, NN)
                qe, qh, kh, kd = (a.astype(F32) for a in (qeb, qhb, khb, kdb))
                dbl = (jnp.sum(dkd * kd, axis=0, keepdims=True)
                       + jnp.sum(ds_new * s_prev, axis=0, keepdims=True) * ebl)
                dqs.append(dqe * z["eb"][:, sl] + dqh * z["ebm"][:, sl])
                dks.append(dkh * z["ekm"][:, sl] + dkd * z["ekl"][:, sl])
                dbs.append(dqe * qe + dqh * qh - dkh * kh - dkd * kd + jnp.where(last, dbl, 0.0))
                dst[hh] = _dot(do, qeb, TN_) + ds_new * ebl
            dgn_ref[...] += dgn
            dq, dk, db = (jnp.concatenate(x, axis=1) for x in (dqs, dks, dbs))
            dlf = _tri_dot(z["upper"], db)
            sf, fg, sq = z["sf"], z["fg"], z["sq"]
            dmix = dlf / fg - dk
            dsf = dmix * (1.0 - lb)
            dlb[...] += jnp.sum(dmix * (1.0 - sf), axis=0, keepdims=True)
            dp_ref[rows, 0:HGW] = (dq * (sq * (1.0 + qr * (1.0 - sq)))).astype(BF16)
            dp_ref[rows, HGW:2 * HGW] = (dsf * sf * (1.0 - sf)).astype(BF16)
            dp_ref[rows, 2 * HGW:3 * HGW] = jnp.concatenate(dvs, axis=1).astype(BF16)
            dp_ref[rows, 3 * HGW:4 * HGW] = jnp.concatenate(dgs, axis=1).astype(BF16)
            return carry

        lax.fori_loop(0, NCH, chunk, 0)
        dl0 = dlb[...] * lb * (1.0 - lb)
        dl_ref[0:1, :] = dl0
        dl_ref[1:2, :] = -dl0

    def col(j):
        return pl.BlockSpec((TM, HGW), lambda i, j=j: (nt - 1 - i, j))

    return pl.pallas_call(
        body, name="hgrn_bwd", grid=(nt,),
        in_specs=[col(0), col(0), col(1), col(2), col(3), col(0),
                  pl.BlockSpec((NCH, HEADS, HD, HD), lambda i: (nt - 1 - i, 0, 0, 0)),
                  pl.BlockSpec((2, HGW), lambda i: (0, 0)), pl.BlockSpec((1, HD), lambda i: (0, 0)), ANY],
        out_specs=[pl.BlockSpec((TM, 4 * HGW), lambda i: (nt - 1 - i, 0)),
                   pl.BlockSpec((1, HD), lambda i: (0, 0)), pl.BlockSpec((2, HGW), lambda i: (0, 0))],
        out_shape=[jax.ShapeDtypeStruct((t, D_INP), BF16), jax.ShapeDtypeStruct((1, HD), F32),
                   jax.ShapeDtypeStruct((2, HGW), F32)],
        scratch_shapes=[pltpu.VMEM((HEADS, HD, HD), F32), pltpu.VMEM((1, HGW), F32)],
        input_output_aliases={9: 0},
        compiler_params=_cparams(("arbitrary",)),
    )(dyb, proj, proj, proj, proj, o, states, lbl, gn, dproj)


def _rotate_w_in(w):
    return jnp.concatenate([w[:, ROT:], jnp.zeros((w.shape[0], COL_PAD), w.dtype), w[:, :ROT]], axis=1)


def _unrotate_g_in(g):
    return jnp.concatenate([g[:, D_INP - ROT:], g[:, :D_IN - ROT]], axis=1)


SB = RGW // 2


def _bd_gates(t):
    tm = _pick(t, (768, 256))
    return (t // tm, 4, 1), (tm, SB, SB), (lambda i, j, k: (i, j % 2), lambda i, j, k: (j % 2, j), lambda i, j, k: (i, j))


def _bd_dw(t):
    tk = _pick(t, (1408, 768))
    return (1, 4, t // tk), (SB, SB, tk), (lambda i, j, k: (k, j % 2), lambda i, j, k: (k, j), lambda i, j, k: (j % 2, j))


def _block_diag(w):
    eye = jnp.eye(RG_BLOCKS, dtype=w.dtype)
    return (w[:, :, None, :] * eye[:, None, :, None]).reshape(RGW, RGW)


def _diag_blocks(wd):
    w4 = wd.reshape(RG_BLOCKS, RG_BLOCK, RG_BLOCKS, RG_BLOCK)
    return jnp.stack([w4[n, :, n, :] for n in range(RG_BLOCKS)])


def _local_step(h0, target, wts, small, hosts=None, finalize=None):
    hosts = hosts or {}
    carried = {}
    g = {}
    t = h0.shape[0]

    def mm(a, b, mode, out_dtype, name, **kw):
        if name not in hosts:
            return _mm(a, b, mode, out_dtype, name, **kw)
        host = hosts[name](g)
        out, res = _mm(a, b, mode, out_dtype, name, host=host, **kw)
        carried[name] = (host.ins, res)
        return out

    w_in = wts["w_in"]
    wax_d = jnp.concatenate([_block_diag(small["rg_wa"]), _block_diag(small["rg_wx"])], axis=1).astype(BF16)
    ba, bx, lam = small["rg_ba"], small["rg_bx"], small["rg_lambda"]
    lbl, gn = small["hg_lb_logits"], small["hg_norm_g"]
    conv_w, conv_b = small["conv_w"], small["conv_b"]

    hn1 = _rmsnorm_fwd(h0, small["norm1_g"], "rmsnorm1")
    proj = mm(hn1, w_in, "nn", BF16, "mm_proj")
    if finalize is not None:
        wts = finalize(carried["mm_proj"][1])
    w_pa, w_pb, w_out, w_fi, w_fd = (wts[k] for k in ("w_proj_a", "w_proj_b", "w_out", "w_ffn_in", "w_ffn_down"))
    xc, xcb = _conv_fwd(proj, conv_w, conv_b)
    gri = mm(xcb, wax_d, "nn", F32, "mm_rg_gates", sparse=_bd_gates(t))
    hrg, ya = _rg_fwd(gri, xc, proj, ba, bx, lam)
    yb, o, states = _hgrn_fwd(proj, lbl, gn)
    pa = mm(ya, w_pa, "nn", BF16, "mm_pa")
    pb = mm(yb, w_pb, "nn", BF16, "mm_pb")
    merged = _merge_fwd(pa, pb, proj)
    h1 = mm(merged, w_out, "nn", F32, "mm_out", resid=h0)
    hn2 = _rmsnorm_fwd(h1, small["norm2_g"], "rmsnorm2")
    gu, act = mm(hn2, w_fi, "nn", BF16, "mm_ffn_in", epi=_swiglu_epi(h0.shape[0]), tiles=(FM, 2 * FH))
    h2 = mm(act, w_fd, "nn", F32, "mm_ffn_down", resid=h1)
    dh2, dh2b, loss, g["norm_f_g"] = _final_loss(h2, small["norm_f_g"], target)

    g["w_ffn_down"] = mm(act, dh2b, "tn", BF16, "mm_d_wfd")
    (dgu,) = mm(dh2b, w_fd, "nt", BF16, "mm_d_act", epi=_swiglu_bwd_epi(gu), tiles=(FM, FH))
    g["w_ffn_in"] = mm(hn2, dgu, "tn", BF16, "mm_d_wfi")
    dhn2 = mm(dgu, w_fi, "nt", BF16, "mm_d_hn2")
    dh1, dh1b, g["norm2_g"] = _rmsnorm_bwd(dhn2, h1, small["norm2_g"], dh2, "rmsnorm2_bwd")
    g["w_out"] = mm(merged, dh1b, "tn", BF16, "mm_d_wout")
    dmerged = mm(dh1b, w_out, "nt", BF16, "mm_d_merged")
    dpa, dpb, dproj = _merge_bwd(dmerged, pa, pb, proj)
    dproj = _zero_pad_cols(dproj)
    g["w_proj_a"] = mm(ya, dpa, "tn", BF16, "mm_d_wpa")
    g["w_proj_b"] = mm(yb, dpb, "tn", BF16, "mm_d_wpb")
    dya = mm(dpa, w_pa, "nt", BF16, "mm_d_ya")
    dyb = mm(dpb, w_pb, "nt", BF16, "mm_d_yb")
    dproj, g["hg_norm_g"], g["hg_lb_logits"] = _hgrn_bwd(dyb, proj, o, states, lbl, gn, dproj)
    dgri, dxc, dproj, g["rg_ba"], g["rg_bx"], g["rg_lambda"] = _rg_bwd(dya, hrg, gri, xc, proj, ba, bx, lam, dproj)
    dwax = mm(xcb, dgri, "tn", F32, "mm_d_wax", sparse=_bd_dw(t))
    g["rg_wa"], g["rg_wx"] = _diag_blocks(dwax[:, :RGW]), _diag_blocks(dwax[:, RGW:])
    dxc = mm(dgri, wax_d, "nt", F32, "mm_d_xc", resid=dxc)
    dproj, g["conv_w"], g["conv_b"] = _conv_bwd(dxc, proj, conv_w, dproj)
    g["w_in"] = mm(hn1, dproj, "tn", BF16, "mm_d_win")
    dhn1 = mm(dproj, w_in, "nt", BF16, "mm_d_hn1")
    grad_x, g["meta_tokens"], g["norm1_g"] = _rmsnorm1_bwd(dhn1, h0, small["norm1_g"], dh1)
    return loss, grad_x, g, carried


def _adamw_math(w, g, m, v):
    m = B1 * m + (1.0 - B1) * g
    v = B2 * v + (1.0 - B2) * (g * g)
    m_hat = m / (1.0 - B1 ** STEP)
    v_hat = v / (1.0 - B2 ** STEP)
    return -LR * (m_hat / (jnp.sqrt(v_hat) + ADAM_EPS) + WD * w), m, v


def _adamw(w, g, m, v, name):
    r, c = w.shape
    tr = _pick(r, (256, 352, 320, 128, 64, 32, 16, 8))

    def body(w_ref, g_ref, m_ref, v_ref, d_ref, mo_ref, vo_ref):
        d_ref[...], mo_ref[...], vo_ref[...] = _adamw_math(w_ref[...], g_ref[...], m_ref[...], v_ref[...])

    blk = pl.BlockSpec((tr, c), lambda i: (i, 0))
    return pl.pallas_call(
        body, name=name, grid=(r // tr,), in_specs=[blk] * 4, out_specs=[blk] * 3,
        out_shape=[jax.ShapeDtypeStruct((r, c), F32)] * 3,
        compiler_params=_cparams(("parallel",)),
    )(w, g, m, v)


def _adamw_small(recv, w, m, v, name):
    rows = w.shape[0]
    tr = rows

    def body(r_ref, w_ref, m_ref, v_ref, g_ref, d_ref, mo_ref, vo_ref):
        g = r_ref[0]
        for p in range(1, 8):
            g = g + r_ref[p]
        g_ref[...] = g
        d_ref[...], mo_ref[...], vo_ref[...] = _adamw_math(w_ref[...], g, m_ref[...], v_ref[...])

    blk = pl.BlockSpec((tr, 128), lambda i: (i, 0))
    return pl.pallas_call(
        body, name=name, grid=(rows // tr,),
        in_specs=[pl.BlockSpec((8, tr, 128), lambda i: (0, i, 0)), blk, blk, blk], out_specs=[blk] * 4,
        out_shape=[jax.ShapeDtypeStruct((rows, 128), F32)] * 4,
        compiler_params=_cparams(("parallel",)),
    )(recv, w, m, v)


def _col_block(weight, px, py):
    return 2 * py + px if weight == "w_ffn_in" else 2 * px + py


def _sum_place(recv, gfull, weight, xyc, name):
    _, r, c = recv.shape
    tr = _pick(r, (128, 176, 160, 64, 32, 16))
    nb = r // tr

    def body(_, own_ref, r_ref, o_ref):
        g = own_ref[...].astype(F32)
        for p in range(7):
            g = g + r_ref[p].astype(F32)
        o_ref[...] = g

    if weight in COL_SHARDED:
        own = pl.BlockSpec((tr, c), lambda i, s: (s[2] * nb + i, _col_block(weight, s[0], s[1])))
    else:
        own = pl.BlockSpec((tr, c), lambda i, s: ((4 * s[0] + 2 * s[1] + s[2]) * nb + i, 0))
    return pl.pallas_call(
        body, name=name,
        grid_spec=pltpu.PrefetchScalarGridSpec(
            num_scalar_prefetch=1, grid=(nb,),
            in_specs=[own, pl.BlockSpec((7, tr, c), lambda i, s: (0, i, 0))],
            out_specs=pl.BlockSpec((tr, c), lambda i, s: (s[2] * nb + i, 0))),
        out_shape=jax.ShapeDtypeStruct((2 * r, c), F32),
        compiler_params=_cparams(("arbitrary",)),
    )(xyc, gfull, recv)


def _place_own(shard, slot, dtype, name):
    r, c = shard.shape
    tr = _pick(r, (256, 352, 320, 128, 32))

    def body(_, x_ref, o_ref):
        o_ref[...] = x_ref[...].astype(dtype)

    return pl.pallas_call(
        body, name=name,
        grid_spec=pltpu.PrefetchScalarGridSpec(
            num_scalar_prefetch=1, grid=(r // tr,),
            in_specs=[pl.BlockSpec((tr, c), lambda i, s: (i, 0))],
            out_specs=pl.BlockSpec((None, tr, c), lambda i, s: (s[0], i, 0))),
        out_shape=jax.ShapeDtypeStruct((4, r, c), dtype),
        compiler_params=_cparams(("arbitrary",)),
    )(slot, shard)


BIG = ("w_in", "w_proj_a", "w_proj_b", "w_out", "w_ffn_in", "w_ffn_down")
COL_SHARDED = ("w_in", "w_ffn_in")


def _coords():
    return lax.axis_index("x"), lax.axis_index("y"), lax.axis_index("c")


def _gather_host(bufs):
    n = len(bufs)

    def place():
        x, y, c = _coords()
        return x, y, c, [(1 - x, y), (x, 1 - y), (1 - x, 1 - y)]

    def slab(b, chip, half_of):
        half = b.shape[1] // 2
        return b.at[chip, pl.ds(pl.multiple_of(half_of * half, 8), half)]

    def copy(b, sems, w, k, chip, half_of, to):
        s = slab(b, chip, half_of)
        return pltpu.make_async_remote_copy(src_ref=s, dst_ref=s, send_sem=sems[0].at[w, k], recv_sem=sems[1].at[w, k],
                                            device_id=to, device_id_type=MESH)

    def start(_, outs, sems):
        x, y, c, chips = place()
        for w, b in enumerate(outs):
            for k, (cx, cy) in enumerate(chips):
                copy(b, sems, w, k, 2 * x + y, c, (cx, cy, c)).start()

    def mid(_, outs, sems):
        x, y, c, chips = place()
        for w, b in enumerate(outs):
            for k, (cx, cy) in enumerate(chips):
                copy(b, sems, w, k, 2 * cx + cy, c, (cx, cy, c)).wait_recv()
                copy(b, sems, w, 3 + k, 2 * cx + cy, c, (x, y, 1 - c)).start()

    def finish(_, outs, sems):
        x, y, c, chips = place()
        for w, b in enumerate(outs):
            for k, (cx, cy) in enumerate(chips):
                copy(b, sems, w, 3 + k, 2 * cx + cy, 1 - c, (x, y, 1 - c)).wait_recv()
        for w, b in enumerate(outs):
            for k, (cx, cy) in enumerate(chips):
                copy(b, sems, w, k, 2 * x + y, c, (cx, cy, c)).wait_send()
                copy(b, sems, w, 3 + k, 2 * cx + cy, c, (x, y, 1 - c)).wait_send()

    return _Host(ins=list(bufs), out_shapes=[jax.ShapeDtypeStruct(b.shape, b.dtype) for b in bufs],
                 aliases={i: i for i in range(n)},
                 sems=[pltpu.SemaphoreType.DMA((n, 6)), pltpu.SemaphoreType.DMA((n, 6))],
                 start=start, mid=mid, finish=finish)


def _run_host(host, name):
    n_in, n_out = len(host.ins), len(host.out_shapes)

    def body(*refs):
        ins, outs, sems = refs[:n_in], refs[n_in:n_in + n_out], refs[n_in + n_out:]
        host.start(ins, outs, sems)
        if host.mid is not None:
            host.mid(ins, outs, sems)
        host.finish(ins, outs, sems)

    return pl.pallas_call(
        body, name=name, in_specs=[ANY] * n_in, out_specs=[ANY] * n_out, out_shape=list(host.out_shapes),
        scratch_shapes=list(host.sems), input_output_aliases=dict(host.aliases),
    )(*host.ins)


def _peer(x, y, c, k):
    fx, fy, fc = (k >> 2) & 1, (k >> 1) & 1, k & 1
    return (x ^ fx if fx else x, y ^ fy if fy else y, c ^ fc if fc else c)


def _sub_shape(name, full_shape):
    r, c = full_shape
    return (r // 2, c // 4) if name in COL_SHARDED else (r // 8, c)


def _exchange_host(names, grads):
    n = len(names)
    shapes = [_sub_shape(k, g.shape) for k, g in zip(names, grads)]

    def copy(ins, outs, sems, w, k):
        x, y, c = _coords()
        px, py, pc = _peer(x, y, c, k)
        sr, sc = shapes[w]
        if names[w] in COL_SHARDED:
            col = _col_block(names[w], px, py) * sc
            src = ins[w].at[pl.ds(pl.multiple_of(pc * sr, 16), sr), pl.ds(pl.multiple_of(col, 128), sc)]
        else:
            src = ins[w].at[pl.ds(pl.multiple_of((4 * px + 2 * py + pc) * sr, 16), sr)]
        return pltpu.make_async_remote_copy(
            src_ref=src, dst_ref=outs[w].at[k - 1], send_sem=sems[0].at[w, k - 1], recv_sem=sems[1].at[w, k - 1],
            device_id=(px, py, pc), device_id_type=MESH)

    def start(ins, outs, sems):
        for w in range(n):
            for k in range(1, 8):
                copy(ins, outs, sems, w, k).start()

    def finish(ins, outs, sems):
        for w in range(n):
            for k in range(1, 8):
                copy(ins, outs, sems, w, k).wait_recv()
        for w in range(n):
            for k in range(1, 8):
                copy(ins, outs, sems, w, k).wait_send()

    return _Host(ins=list(grads), out_shapes=[jax.ShapeDtypeStruct((7,) + s, g.dtype) for s, g in zip(shapes, grads)],
                 aliases={}, sems=[pltpu.SemaphoreType.DMA((n, 7)), pltpu.SemaphoreType.DMA((n, 7))],
                 start=start, mid=None, finish=finish)


def _sibling_host(bufs):
    n = len(bufs)

    def copy(outs, sems, w, half_of):
        x, y, c = _coords()
        half = outs[w].shape[0] // 2
        rows = outs[w].at[pl.ds(pl.multiple_of(half_of * half, 8), half)]
        return pltpu.make_async_remote_copy(src_ref=rows, dst_ref=rows, send_sem=sems[0].at[w], recv_sem=sems[1].at[w],
                                            device_id=(x, y, 1 - c), device_id_type=MESH)

    def start(_, outs, sems):
        c = lax.axis_index("c")
        for w in range(n):
            copy(outs, sems, w, c).start()

    def finish(_, outs, sems):
        c = lax.axis_index("c")
        for w in range(n):
            copy(outs, sems, w, 1 - c).wait_recv()
        for w in range(n):
            copy(outs, sems, w, c).wait_send()

    return _Host(ins=list(bufs), out_shapes=[jax.ShapeDtypeStruct(b.shape, b.dtype) for b in bufs],
                 aliases={i: i for i in range(n)},
                 sems=[pltpu.SemaphoreType.DMA((n,)), pltpu.SemaphoreType.DMA((n,))], start=start, mid=None, finish=finish)


def _pack_host(pack):
    rows = pack.shape[0]

    def me_of():
        x, y, c = _coords()
        return x, y, c, 4 * x + 2 * y + c

    def copy(ins, outs, sems, k, slot):
        x, y, c, _ = me_of()
        return pltpu.make_async_remote_copy(src_ref=ins[0], dst_ref=outs[0].at[slot], send_sem=sems[0].at[k - 1],
                                            recv_sem=sems[1].at[k - 1], device_id=_peer(x, y, c, k), device_id_type=MESH)

    def start(ins, outs, sems):
        me = me_of()[3]
        pltpu.make_async_copy(ins[0], outs[0].at[me], sems[2]).start()
        for k in range(1, 8):
            copy(ins, outs, sems, k, me).start()

    def finish(ins, outs, sems):
        x, y, c, me = me_of()
        for k in range(1, 8):
            px, py, pc = _peer(x, y, c, k)
            copy(ins, outs, sems, k, 4 * px + 2 * py + pc).wait_recv()
        for k in range(1, 8):
            copy(ins, outs, sems, k, me).wait_send()
        pltpu.make_async_copy(ins[0], outs[0].at[me], sems[2]).wait()

    return _Host(ins=[pack], out_shapes=[jax.ShapeDtypeStruct((8, rows, 128), F32)], aliases={},
                 sems=[pltpu.SemaphoreType.DMA((7,)), pltpu.SemaphoreType.DMA((7,)), pltpu.SemaphoreType.DMA],
                 start=start, mid=None, finish=finish)


def _join_hosts(hosts):
    ins, outs, sems, aliases, spans = [], [], [], {}, []
    for h in hosts:
        spans.append((len(ins), len(h.ins), len(outs), len(h.out_shapes), len(sems), len(h.sems)))
        for i_in, i_out in h.aliases.items():
            aliases[len(ins) + i_in] = len(outs) + i_out
        ins, outs, sems = ins + list(h.ins), outs + list(h.out_shapes), sems + list(h.sems)

    def phase(which):
        fns = [getattr(h, which) for h in hosts]
        if all(f is None for f in fns):
            return None

        def run(i_refs, o_refs, s_refs):
            for f, (i0, n_i, o0, n_o, s0, n_s) in zip(fns, spans):
                if f is not None:
                    f(i_refs[i0:i0 + n_i], o_refs[o0:o0 + n_o], s_refs[s0:s0 + n_s])

        return run

    return _Host(ins, outs, aliases, sems, phase("start"), phase("mid"), phase("finish"))


def _allgather_pack(pack):
    rows = pack.shape[0]

    def body(p_ref, o_ref, send, recv, local):
        x, y, c = _coords()
        me = 4 * x + 2 * y + c
        mine = pltpu.make_async_copy(p_ref, o_ref.at[me], local)
        mine.start()

        def copy(k, slot):
            return pltpu.make_async_remote_copy(src_ref=p_ref, dst_ref=o_ref.at[slot], send_sem=send.at[k - 1],
                                                recv_sem=recv.at[k - 1], device_id=_peer(x, y, c, k), device_id_type=MESH)

        for k in range(1, 8):
            copy(k, me).start()
        for k in range(1, 8):
            px, py, pc = _peer(x, y, c, k)
            copy(k, 4 * px + 2 * py + pc).wait_recv()
        for k in range(1, 8):
            copy(k, me).wait_send()
        mine.wait()

    return pl.pallas_call(
        body, name="allgather_pack", in_specs=[pl.BlockSpec(memory_space=pltpu.VMEM)], out_specs=ANY,
        out_shape=jax.ShapeDtypeStruct((8, rows, 128), F32),
        scratch_shapes=[pltpu.SemaphoreType.DMA((7,)), pltpu.SemaphoreType.DMA((7,)), pltpu.SemaphoreType.DMA],
        compiler_params=pltpu.CompilerParams(vmem_limit_bytes=VMEM_LIMIT),
    )(pack)


SMALL = ("norm1_g", "conv_b", "rg_wa", "rg_ba", "rg_wx", "rg_bx", "rg_lambda", "hg_lb_logits", "hg_norm_g",
         "norm2_g", "norm_f_g")
SHARDED_SMALL = ("meta_tokens", "conv_w")
SEG = 1024


PACK_LATE = ("norm1_g", "meta_tokens", "loss")
PACK_EARLY = tuple(k for k in SMALL + SHARDED_SMALL if k not in PACK_LATE)


def _pack(vals, names):
    parts = []
    for k in names:
        v = vals[k].reshape(-1).astype(F32)
        parts.append(jnp.pad(v, (0, -v.shape[0] % SEG)))
    return jnp.concatenate(parts).reshape(-1, 128)


def _unpack(flat, shapes, names):
    out, off = {}, 0
    flat = flat.reshape(-1)
    for k in names:
        size = 1
        for s in shapes[k]:
            size *= s
        out[k] = flat[off:off + size].reshape(shapes[k])
        off += size + (-size % SEG)
    return out


ORDER = ("meta_tokens", "norm1_g", "w_in", "conv_w", "conv_b", "rg_wa", "rg_ba", "rg_wx", "rg_bx", "rg_lambda",
         "hg_lb_logits", "hg_norm_g", "w_proj_a", "w_proj_b", "w_out", "norm2_g", "w_ffn_in", "w_ffn_down", "norm_f_g")


def kernel(x, meta_tokens, norm1_g, w_in, conv_w, conv_b, rg_wa, rg_ba, rg_wx, rg_bx, rg_lambda, hg_lb_logits, hg_norm_g, w_proj_a, w_proj_b, w_out, norm2_g, w_ffn_in, w_ffn_down, norm_f_g, loss_target, m_meta_tokens, m_norm1_g, m_w_in, m_conv_w, m_conv_b, m_rg_wa, m_rg_ba, m_rg_wx, m_rg_bx, m_rg_lambda, m_hg_lb_logits, m_hg_norm_g, m_w_proj_a, m_w_proj_b, m_w_out, m_norm2_g, m_w_ffn_in, m_w_ffn_down, m_norm_f_g, v_meta_tokens, v_norm1_g, v_w_in, v_conv_w, v_conv_b, v_rg_wa, v_rg_ba, v_rg_wx, v_rg_bx, v_rg_lambda, v_hg_lb_logits, v_hg_norm_g, v_w_proj_a, v_w_proj_b, v_w_out, v_norm2_g, v_w_ffn_in, v_w_ffn_down, v_norm_f_g):
    args = dict(locals())
    w = {k: args[k] for k in ORDER}
    m = {k: args["m_" + k] for k in ORDER}
    v = {k: args["v_" + k] for k in ORDER}
    xi, yi, ci = _coords()
    chip = 2 * xi + yi
    slot = jnp.reshape(chip, (1,)).astype(jnp.int32)
    xyc = jnp.stack([xi, yi, ci]).astype(jnp.int32)

    def full_matrix(k, gth):
        if k == "w_ffn_in":
            return jnp.concatenate([gth[0], gth[2], gth[1], gth[3]], axis=1)
        if k in COL_SHARDED:
            return jnp.concatenate([gth[j] for j in range(4)], axis=1)
        return gth.reshape(-1, gth.shape[2])

    tiny = jnp.zeros((32, 384), F32)
    tiny = tiny.at[0:N_META, 0:256].set(meta_tokens).at[N_META:N_META + 4, 0:320].set(conv_w[0])
    slots = {k: _place_own(w[k][0], slot, BF16, "place_" + k) for k in BIG}
    w_in_all, tiny_all = _run_host(_gather_host([slots["w_in"], _place_own(tiny, slot, F32, "place_tiny")]),
                                   "allgather_w_in")
    later = [k for k in BIG if k != "w_in"]
    meta_full = jnp.transpose(tiny_all[:, 0:N_META, 0:256], (1, 0, 2)).reshape(N_META, D)
    conv_w_full = jnp.transpose(tiny_all[:, N_META:N_META + 4, 0:320], (1, 0, 2)).reshape(4, RGW)

    small = dict(norm1_g=norm1_g, conv_w=conv_w_full, conv_b=conv_b, rg_wa=rg_wa[0], rg_ba=rg_ba, rg_wx=rg_wx[0],
                 rg_bx=rg_bx, rg_lambda=rg_lambda, hg_lb_logits=hg_lb_logits, hg_norm_g=hg_norm_g,
                 norm2_g=norm2_g, norm_f_g=norm_f_g.reshape(1, D))

    hosts = {
        "mm_proj": lambda g: _gather_host([slots[k] for k in later]),
        "mm_d_act": lambda g: _exchange_host(["w_ffn_down"], [g["w_ffn_down"]]),
        "mm_d_hn2": lambda g: _exchange_host(["w_ffn_in"], [g["w_ffn_in"]]),
        "mm_d_win": lambda g: _join_hosts([
            _exchange_host(["w_out", "w_proj_a", "w_proj_b"], [g["w_out"], g["w_proj_a"], g["w_proj_b"]]),
            _pack_host(_pack(g, PACK_EARLY))]),
        "mm_d_hn1": lambda g: _exchange_host(["w_in"], [_unrotate_g_in(g["w_in"])]),
    }
    h0 = jnp.concatenate([jnp.zeros((PAD, D), F32), meta_full, x[0]], axis=0)
    loss_blk, grad_x, g, carried = _local_step(
        h0, loss_target[0], {"w_in": _rotate_w_in(full_matrix("w_in", w_in_all))}, small, hosts=hosts,
        finalize=lambda gathered: {k: full_matrix(k, gth) for k, gth in zip(later, gathered)})
    g["loss"] = loss_blk[0:1, 0:1]

    halves = {}
    sources = {"mm_d_act": ["w_ffn_down"], "mm_d_hn2": ["w_ffn_in"], "mm_d_win": ["w_out", "w_proj_a", "w_proj_b"],
               "mm_d_hn1": ["w_in"]}
    for name, keys in sources.items():
        partials, received = carried[name]
        for k, part, rec in zip(keys, partials, received):
            halves[k] = _sum_place(rec, part, k, xyc, "sum_" + k)
    g_big = dict(zip(BIG, _run_host(_sibling_host([halves[k] for k in BIG]), "sibling_swap")))
    small_shapes = {k: w[k].shape for k in SMALL}
    small_shapes["meta_tokens"], small_shapes["conv_w"], small_shapes["loss"] = (N_META, D), (1, 4, RGW), (1,)
    recv_packs = {PACK_EARLY: carried["mm_d_win"][1][3], PACK_LATE: _allgather_pack(_pack(g, PACK_LATE))}

    w_sm = {k: w[k] for k in SMALL}
    m_sm = {k: m[k] for k in SMALL}
    v_sm = {k: v[k] for k in SMALL}
    for d in (w_sm, m_sm, v_sm):
        d["meta_tokens"], d["conv_w"], d["loss"] = jnp.zeros((N_META, D), F32), jnp.zeros((1, 4, RGW), F32), jnp.zeros((1,), F32)
    outs_sm = [{}, {}, {}, {}]
    for names, rec in recv_packs.items():
        res = _adamw_small(rec, _pack(w_sm, names), _pack(m_sm, names), _pack(v_sm, names), "adamw_" + names[0])
        for acc, o in zip(outs_sm, res):
            acc.update(_unpack(o, small_shapes, names))
    grad, delta, new_m, new_v = ({k: o[k] for k in SMALL} for o in outs_sm)
    g_small_full = outs_sm[0]
    loss = g_small_full["loss"][0]
    g_shard = {"meta_tokens": lax.dynamic_slice(g_small_full["meta_tokens"], (0, chip * 256), (N_META, 256)),
               "conv_w": lax.dynamic_slice(g_small_full["conv_w"], (0, 0, chip * 320), (1, 4, 320))[0]}
    for k in BIG + SHARDED_SMALL:
        gk = g_big[k] if k in BIG else g_shard[k]
        wk, mk, vk = (d[k].reshape(gk.shape) for d in (w, m, v))
        dk, mk, vk = _adamw(wk, gk, mk, vk, "adamw_" + k)
        grad[k], delta[k], new_m[k], new_v[k] = (a.reshape(w[k].shape) for a in (gk, dk, mk, vk))

    return (loss, grad_x[None], *[grad[k] for k in ORDER], *[delta[k] for k in ORDER],
            *[new_m[k] for k in ORDER], *[new_v[k] for k in ORDER])
```

```python
import functools

import jax
import jax.numpy as jnp
from jax import lax
from jax.experimental import pallas as pl
from jax.experimental.pallas import tpu as pltpu

F32, BF16 = jnp.float32, jnp.bfloat16
D = 1024
N_META = 16
RGW = 1280
RG_BLOCKS, RG_BLOCK = 16, 80
RG_C = 8.0
HEADS, HD = 8, 128
HGW = HEADS * HD
DFF = 2816
D_IN = 2 * RGW + 6 * D
ROT = 2 * RGW
COL_PAD = 256
D_INP = D_IN + COL_PAD
EPS = 1e-6
CH = 64
TM = 256
ROW0 = TM
PAD = ROW0 - N_META
CT = RGW
RS = 16
NCT = RGW // CT
EXP_CLAMP = 80.0
VMEM_LIMIT = 56 * 1024 * 1024

LR, B1, B2, ADAM_EPS, WD, STEP = 0.001, 0.9, 0.999, 1e-08, 0.01, 10
MESH = pl.DeviceIdType.MESH
ANY = pl.BlockSpec(memory_space=pl.ANY)


def _cparams(sem):
    return pltpu.CompilerParams(dimension_semantics=sem, vmem_limit_bytes=VMEM_LIMIT)


def _pick(n, prefs):
    for p in prefs:
        if n % p == 0:
            return p
    return n


def _sig(x):
    return 0.5 * jnp.tanh(0.5 * x) + 0.5


def _dot(a, b, dims, precision=None):
    return lax.dot_general(a, b, (dims, ((), ())), preferred_element_type=F32, precision=precision)


NN, NT, TN_ = ((1,), (0,)), ((1,), (1,)), ((0,), (0,))


class _Host:
    def __init__(self, ins, out_shapes, aliases, sems, start, mid, finish):
        self.ins, self.out_shapes, self.aliases, self.sems = ins, out_shapes, aliases, sems
        self.start, self.mid, self.finish = start, mid, finish


class _Epi:
    def __init__(self, ins, in_specs, out_shapes, out_specs, fn, sequential=False):
        self.ins, self.in_specs, self.out_shapes, self.out_specs, self.fn = ins, in_specs, out_shapes, out_specs, fn
        self.sequential = sequential


def _mm(a, b, mode, out_dtype, name, resid=None, host=None, epi=None, tiles=None, sparse=None, norm_gain=None):
    if mode == "nn":
        (m, kd), n = a.shape, b.shape[1]
    elif mode == "nt":
        (m, kd), n = a.shape, b.shape[0]
    else:
        (kd, m), n = a.shape, b.shape[1]
    if mode == "tn":
        tm = _pick(m, (1024, 1280, 1408, 640, 512, 256, 128))
        tk = _pick(kd, (1408, 768, 512, 256, 128))
    else:
        tm = _pick(m, (768, 512, 640, 256, 128))
        tk = kd if kd <= 2816 else _pick(kd, (1792, 1408, 1024, 512, 256, 128))
    tn = _pick(n, (1792, 1408, 1280, 1024, 512, 256, 128))
    if tiles is not None:
        tm, tn = tiles
    a_map = (lambda i, j, k: (k, i)) if mode == "tn" else (lambda i, j, k: (i, k))
    b_map = (lambda i, j, k: (j, k)) if mode == "nt" else (lambda i, j, k: (k, j))
    o_map = lambda i, j, k: (i, j)
    if sparse is not None:
        (ni, nj, nk), (tm, tn, tk), (a_map, b_map, o_map) = sparse
    else:
        assert m % tm == 0 and n % tn == 0 and kd % tk == 0, (name, m, n, kd, tm, tn, tk)
        ni, nj, nk = m // tm, n // tn, kd // tk
    dims = {"nn": NN, "nt": NT, "tn": TN_}[mode]

    n_hin = len(host.ins) if host else 0
    n_hout = len(host.out_shapes) if host else 0
    n_res = 0 if resid is None else 1
    n_pro = 0 if norm_gain is None else 1
    n_ein = len(epi.ins) if epi else 0
    n_out = len(epi.out_shapes) if epi else 1
    assert not n_pro or (mode == "nn" and nk == 1)

    def finish(r, r_ref, e_in, o_refs, i):
        if resid is not None:
            r = r + r_ref[...]
        if epi:
            epi.fn(r, e_in, o_refs, i)
        else:
            o_refs[0][...] = r.astype(out_dtype)

    def body(*refs):
        a_ref, b_ref = refs[:2]
        r_ref = refs[2] if resid is not None else None
        pos = 2 + n_res
        g_ref = refs[pos] if n_pro else None
        pos += n_pro
        e_in = refs[pos:pos + n_ein]
        pos += n_ein
        h_in = refs[pos:pos + n_hin]
        pos += n_hin
        o_ref = refs[pos:pos + n_out]
        pos += n_out
        hn_ref = refs[pos] if n_pro else None
        pos += n_pro
        h_out = refs[pos:pos + n_hout]
        scratch = refs[pos + n_hout:]
        n_acc = 1 if nk > 1 else 0
        h_sems = scratch[n_acc + n_pro:]
        i, j, k = pl.program_id(0), pl.program_id(1), pl.program_id(2)
        if host:
            @pl.when((i == 0) & (j == 0) & (k == 0))
            def _():
                host.start(h_in, h_out, h_sems)

            if host.mid is not None:
                @pl.when((i == (3 * ni) // 4) & (j == 0) & (k == 0))
                def _():
                    host.mid(h_in, h_out, h_sems)

        if n_pro:
            a_s = scratch[n_acc]

            @pl.when(j == 0)
            def _():
                xh, _ = _rms(a_ref[...])
                a_s[...] = (xh * g_ref[...]).astype(BF16)
                hn_ref[...] = a_s[...]

            a_val = a_s[...]
        else:
            a_val = a_ref[...]

        if nk == 1:
            finish(_dot(a_val, b_ref[...], dims), r_ref, e_in, o_ref, i)
        else:
            acc = scratch[0]

            @pl.when(k == 0)
            def _():
                acc[...] = jnp.zeros_like(acc)

            acc[...] += _dot(a_val, b_ref[...], dims)

            @pl.when(k == nk - 1)
            def _():
                finish(acc[...], r_ref, e_in, o_ref, i)

        if host:
            @pl.when((i == ni - 1) & (j == nj - 1) & (k == nk - 1))
            def _():
                host.finish(h_in, h_out, h_sems)

    a_spec = pl.BlockSpec((tk, tm) if mode == "tn" else (tm, tk), a_map)
    b_spec = pl.BlockSpec((tn, tk) if mode == "nt" else (tk, tn), b_map)
    o_spec = pl.BlockSpec((tm, tn), o_map)
    in_specs, args = [a_spec, b_spec], [a, b]
    if resid is not None:
        in_specs.append(o_spec)
        args.append(resid)
    if n_pro:
        in_specs.append(pl.BlockSpec((1, kd), lambda i, j, k: (0, 0)))
        args.append(norm_gain)
    out_shapes, out_specs = [jax.ShapeDtypeStruct((m, n), out_dtype)], [o_spec]
    if epi:
        in_specs += list(epi.in_specs)
        args += list(epi.ins)
        out_shapes, out_specs = list(epi.out_shapes), list(epi.out_specs)
    if n_pro:
        out_shapes.append(jax.ShapeDtypeStruct((m, kd), BF16))
        out_specs.append(pl.BlockSpec((tm, kd), lambda i, j, k: (i, 0)))
    scratch = ([pltpu.VMEM((tm, tn), F32)] if nk > 1 else []) + ([pltpu.VMEM((tm, kd), BF16)] if n_pro else [])
    sequential = host or n_pro or (epi and epi.sequential)
    outs = pl.pallas_call(
        body, name=name, grid=(ni, nj, nk),
        in_specs=in_specs + [ANY] * n_hin, out_specs=out_specs + [ANY] * n_hout,
        out_shape=out_shapes + (list(host.out_shapes) if host else []),
        scratch_shapes=scratch + (list(host.sems) if host else []),
        input_output_aliases=({2 + n_res + n_pro + n_ein + i_in: n_out + n_pro + i_out
                               for i_in, i_out in host.aliases.items()} if host else {}),
        compiler_params=_cparams(("arbitrary",) * 3 if sequential else ("parallel", "parallel", "arbitrary")),
    )(*args, *(host.ins if host else []))
    main = list(outs[:n_out]) if epi else outs[0]
    if n_pro:
        main = (main, outs[n_out])
    return (main, list(outs[n_out + n_pro:])) if host else main


def _rms(x):
    r = lax.rsqrt(jnp.mean(x * x, axis=-1, keepdims=True) + EPS)
    return x * r, r


def _rms_bwd_math(dhn, x, g):
    xh, r = _rms(x)
    dxh = dhn * g
    dx = r * (dxh - xh * jnp.mean(dxh * xh, axis=-1, keepdims=True))
    return dx, jnp.sum(dhn * xh, axis=0, keepdims=True)


def _row_specs(tm):
    return pl.BlockSpec((tm, D), lambda i, j, k: (i, 0)), pl.BlockSpec((1, D), lambda i, j, k: (0, 0))


def _final_loss_epi(gf, target, t):
    def fn(x, ins, outs, i):
        g_ref, t_ref = ins
        dh_ref, dhb_ref, loss_ref, dg_ref = outs

        @pl.when(i == 0)
        def _():
            loss_ref[...] = jnp.zeros_like(loss_ref)
            dg_ref[...] = jnp.zeros_like(dg_ref)

        g = g_ref[...]
        xh, _ = _rms(x)
        err = jnp.where(i > 0, xh * g - t_ref[...], 0.0)
        loss_ref[...] += 0.5 * jnp.sum(jnp.sum(err * err, axis=-1, keepdims=True) * (1.0 / D))
        dx, dg = _rms_bwd_math(err * (1.0 / D), x, g)
        dh_ref[...] = dx
        dhb_ref[...] = dx.astype(BF16)
        dg_ref[...] += dg

    row, vec = _row_specs(TM)
    return _Epi(ins=[gf, target], in_specs=[vec, pl.BlockSpec((TM, D), lambda i, j, k: (jnp.maximum(i - 1, 0), 0))],
                out_shapes=[jax.ShapeDtypeStruct((t, D), F32), jax.ShapeDtypeStruct((t, D), BF16),
                            jax.ShapeDtypeStruct((8, 128), F32), jax.ShapeDtypeStruct((1, D), F32)],
                out_specs=[row, row, pl.BlockSpec((8, 128), lambda i, j, k: (0, 0)), vec], fn=fn, sequential=True)


def _rms_bwd_epi(h, g, dres, tm):
    t = h.shape[0]

    def fn(dhn, ins, outs, i):
        h_ref, g_ref, dres_ref = ins
        dh_ref, dhb_ref, dg_ref = outs

        @pl.when(i == 0)
        def _():
            dg_ref[...] = jnp.zeros_like(dg_ref)

        dx, dg = _rms_bwd_math(dhn, h_ref[...], g_ref[...])
        dh = dres_ref[...] + dx
        dh_ref[...] = dh
        dhb_ref[...] = dh.astype(BF16)
        dg_ref[...] += dg

    row, vec = _row_specs(tm)
    return _Epi(ins=[h, g, dres], in_specs=[row, vec, row],
                out_shapes=[jax.ShapeDtypeStruct((t, D), F32), jax.ShapeDtypeStruct((t, D), BF16),
                            jax.ShapeDtypeStruct((1, D), F32)],
                out_specs=[row, row, vec], fn=fn, sequential=True)


def _rmsnorm1_bwd(dhn, h, g, dres):
    t = h.shape[0]

    def body(dhn_ref, h_ref, g_ref, dres_ref, dx_ref, dmeta_ref, dg_ref):
        i = pl.program_id(0)

        @pl.when(i == 0)
        def _():
            dg_ref[...] = jnp.zeros_like(dg_ref)

        dx, dg = _rms_bwd_math(dhn_ref[...].astype(F32), h_ref[...], g_ref[...])
        dh = dres_ref[...] + dx
        dg_ref[...] += dg
        dx_ref[...] = dh

        @pl.when(i == 0)
        def _():
            dmeta_ref[...] = dh[PAD:, :]

    row = pl.BlockSpec((TM, D), lambda i: (i, 0))
    vec = pl.BlockSpec((1, D), lambda i: (0, 0))
    return pl.pallas_call(
        body, name="rmsnorm1_bwd", grid=(t // TM,),
        in_specs=[row, row, vec, row],
        out_specs=[pl.BlockSpec((TM, D), lambda i: (jnp.maximum(i - 1, 0), 0)),
                   pl.BlockSpec((N_META, D), lambda i: (0, 0)), vec],
        out_shape=[jax.ShapeDtypeStruct((t - ROW0, D), F32), jax.ShapeDtypeStruct((N_META, D), F32),
                   jax.ShapeDtypeStruct((1, D), F32)],
        compiler_params=_cparams(("arbitrary",)),
    )(dhn, h, g, dres)


FH = DFF // 2
FM = 384


def _swiglu_epi(t):
    def fn(r, _, outs, i):
        g, u = r[:, :FH], r[:, FH:]
        outs[0][...] = r.astype(BF16)
        outs[1][...] = (g * _sig(g) * u).astype(BF16)

    return _Epi(ins=[], in_specs=[],
                out_shapes=[jax.ShapeDtypeStruct((t, 2 * DFF), BF16), jax.ShapeDtypeStruct((t, DFF), BF16)],
                out_specs=[pl.BlockSpec((FM, 2 * FH), lambda i, j, k: (i, j)), pl.BlockSpec((FM, FH), lambda i, j, k: (i, j))],
                fn=fn)


def _swiglu_bwd_epi(gu):
    def fn(d, ins, outs, i):
        gu_t = ins[0][...].astype(F32)
        g, u = gu_t[:, :FH], gu_t[:, FH:]
        s = _sig(g)
        outs[0][:, :FH] = (d * u * (s * (1.0 + g * (1.0 - s)))).astype(BF16)
        outs[0][:, FH:] = (d * (g * s)).astype(BF16)

    spec = pl.BlockSpec((FM, 2 * FH), lambda i, j, k: (i, j))
    return _Epi(ins=[gu], in_specs=[spec], out_shapes=[jax.ShapeDtypeStruct(gu.shape, BF16)], out_specs=[spec], fn=fn)


COL_GA, COL_GB = 4, 5
COL_AX, COL_AG = (6 * D + COL_PAD) // CT, (6 * D + COL_PAD + RGW) // CT


def _merge_fwd(pa, pb, proj):
    t = pa.shape[0]

    def body(pa_ref, pb_ref, ga_ref, gb_ref, o_ref):
        ga, gb, pa, pb = (r[...].astype(F32) for r in (ga_ref, gb_ref, pa_ref, pb_ref))
        o_ref[...] = (_sig(ga) * pa + _sig(gb) * pb).astype(BF16)

    row = pl.BlockSpec((TM, D), lambda i: (i, 0))
    return pl.pallas_call(
        body, name="merge_fwd", grid=(t // TM,),
        in_specs=[row, row, pl.BlockSpec((TM, D), lambda i: (i, COL_GA)), pl.BlockSpec((TM, D), lambda i: (i, COL_GB))],
        out_specs=row, out_shape=jax.ShapeDtypeStruct((t, D), BF16),
        compiler_params=_cparams(("parallel",)),
    )(pa, pb, proj, proj)


def _merge_bwd_epi(pa, pb, proj, tm):
    t = pa.shape[0]

    def fn(d, ins, outs, i):
        pa, pb, ga, gb = (r[...].astype(F32) for r in ins)
        dpa_ref, dpb_ref, dproj_ref = outs
        sa, sb = _sig(ga), _sig(gb)
        dpa_ref[...] = (sa * d).astype(BF16)
        dpb_ref[...] = (sb * d).astype(BF16)
        dproj_ref[:, :D] = (d * pa * sa * (1.0 - sa)).astype(BF16)
        dproj_ref[:, D:] = (d * pb * sb * (1.0 - sb)).astype(BF16)

    row, _ = _row_specs(tm)
    return _Epi(ins=[pa, pb, proj, proj],
                in_specs=[row, row, pl.BlockSpec((tm, D), lambda i, j, k: (i, COL_GA)),
                          pl.BlockSpec((tm, D), lambda i, j, k: (i, COL_GB))],
                out_shapes=[jax.ShapeDtypeStruct((t, D), BF16), jax.ShapeDtypeStruct((t, D), BF16),
                            jax.ShapeDtypeStruct((t, D_INP), BF16)],
                out_specs=[row, row, pl.BlockSpec((tm, 2 * D), lambda i, j, k: (i, 2))], fn=fn)


def _zero_pad_cols(dproj):
    t = dproj.shape[0]
    tz = _pick(t, (768, 256))

    def body(_, o_ref):
        o_ref[...] = jnp.zeros_like(o_ref)

    return pl.pallas_call(
        body, name="dproj_pad", grid=(t // tz,), in_specs=[ANY],
        out_specs=pl.BlockSpec((tz, COL_PAD), lambda i: (i, (6 * D) // COL_PAD)),
        out_shape=jax.ShapeDtypeStruct(dproj.shape, BF16), input_output_aliases={0: 0},
        compiler_params=_cparams(("parallel",)),
    )(dproj)


HALO = 16


def _prev_halo(col0):
    return pl.BlockSpec((HALO, CT), lambda c, i: (jnp.maximum(i * (TM // HALO) - 1, 0), col0 + c))


def _conv_fwd(proj, w, b):
    t = proj.shape[0]

    def body(x_ref, halo_ref, w_ref, b_ref, o_ref, ob_ref, buf):
        i = pl.program_id(1)
        x = x_ref[...].astype(F32)
        buf[0:HALO, :] = jnp.where(i > 0, halo_ref[...].astype(F32), 0.0)
        buf[HALO:, :] = x
        wv = w_ref[...]
        y = b_ref[...] + wv[3:4, :] * x
        for j in range(3):
            y = y + wv[j:j + 1, :] * buf[pl.ds(HALO - 3 + j, TM), :]
        o_ref[...] = y
        ob_ref[...] = y.astype(BF16)

    blk = pl.BlockSpec((TM, CT), lambda c, i: (i, c))
    return pl.pallas_call(
        body, name="conv_fwd", grid=(NCT, t // TM),
        in_specs=[pl.BlockSpec((TM, CT), lambda c, i: (i, COL_AX + c)), _prev_halo(COL_AX),
                  pl.BlockSpec((4, CT), lambda c, i: (0, c)), pl.BlockSpec((1, CT), lambda c, i: (0, c))],
        out_specs=[blk, blk],
        out_shape=[jax.ShapeDtypeStruct((t, RGW), F32), jax.ShapeDtypeStruct((t, RGW), BF16)],
        scratch_shapes=[pltpu.VMEM((TM + HALO, CT), F32)],
        compiler_params=_cparams(("parallel", "parallel")),
    )(proj, proj, w, b)


def _conv_bwd(dxc, proj, w, dproj):
    t = proj.shape[0]
    nt = t // TM

    def body(d_ref, dn_ref, x_ref, halo_ref, w_ref, _, dx_ref, dw_ref, db_ref, bufd, bufx):
        i = pl.program_id(1)

        @pl.when(i == 0)
        def _():
            dw_ref[...] = jnp.zeros_like(dw_ref)
            db_ref[...] = jnp.zeros_like(db_ref)

        d = d_ref[...]
        bufd[0:TM, :] = d
        bufd[TM:, :] = jnp.where(i < nt - 1, dn_ref[...], 0.0)
        x = x_ref[...].astype(F32)
        bufx[0:HALO, :] = jnp.where(i > 0, halo_ref[...].astype(F32), 0.0)
        bufx[HALO:, :] = x
        wv = w_ref[...]
        dx = wv[3:4, :] * d
        for j in range(3):
            dx = dx + wv[j:j + 1, :] * bufd[pl.ds(3 - j, TM), :]
            dw_ref[j:j + 1, :] += jnp.sum(d * bufx[pl.ds(HALO - 3 + j, TM), :], axis=0, keepdims=True)
        dw_ref[3:4, :] += jnp.sum(d * x, axis=0, keepdims=True)
        db_ref[...] += jnp.sum(d, axis=0, keepdims=True)
        dx_ref[...] = dx.astype(BF16)

    return pl.pallas_call(
        body, name="conv_bwd", grid=(NCT, nt),
        in_specs=[pl.BlockSpec((TM, CT), lambda c, i: (i, c)),
                  pl.BlockSpec((8, CT), lambda c, i: (jnp.minimum((i + 1) * (TM // 8), t // 8 - 1), c)),
                  pl.BlockSpec((TM, CT), lambda c, i: (i, COL_AX + c)), _prev_halo(COL_AX),
                  pl.BlockSpec((4, CT), lambda c, i: (0, c)), ANY],
        out_specs=[pl.BlockSpec((TM, CT), lambda c, i: (i, COL_AX + c)),
                   pl.BlockSpec((4, CT), lambda c, i: (0, c)), pl.BlockSpec((1, CT), lambda c, i: (0, c))],
        out_shape=[jax.ShapeDtypeStruct((t, D_INP), BF16), jax.ShapeDtypeStruct((4, RGW), F32),
                   jax.ShapeDtypeStruct((1, RGW), F32)],
        scratch_shapes=[pltpu.VMEM((TM + 8, CT), F32), pltpu.VMEM((TM + HALO, CT), F32)],
        input_output_aliases={5: 0},
        compiler_params=_cparams(("parallel", "arbitrary")),
    )(dxc, dxc, proj, proj, w, dproj)


def _gelu(x):
    c = 0.7978845608028654
    th = jnp.tanh(c * (x + 0.044715 * x * x * x))
    return 0.5 * x * (1.0 + th), th


def _rg_gates(gr, gi, xc, ba, bx, lam, row0):
    r = _sig(gr + ba)
    ig = _sig(gi + bx)
    sp = jax.nn.softplus(-lam)
    a = jnp.exp(-RG_C * r * sp)
    s2 = jnp.maximum(1.0 - a * a, 1e-30)
    rs = lax.rsqrt(s2)
    s = s2 * rs
    rows = row0 + lax.broadcasted_iota(jnp.int32, gr.shape, 0)
    live = rows >= PAD
    u = jnp.where(live, s * ig * xc, 0.0)
    return r, ig, sp, a, s, rs, u, live


def _rg_fwd(gri, xc, proj, ba, bx, lam):
    t = gri.shape[0]

    def body(gr_ref, gi_ref, xc_ref, ag_ref, ba_ref, bx_ref, lam_ref, h_ref, ya_ref, hc):
        i = pl.program_id(1)

        @pl.when(i == 0)
        def _():
            hc[...] = jnp.zeros_like(hc)

        ba, bx, lam = ba_ref[...], bx_ref[...], lam_ref[...]
        sub = lax.broadcasted_iota(jnp.int32, (8, CT), 0)

        def strip(k, h):
            off = pl.multiple_of(k * RS, RS)
            rows = pl.ds(off, RS)
            _, _, _, a, _, _, u, _ = _rg_gates(gr_ref[rows, :], gi_ref[rows, :], xc_ref[rows, :], ba, bx, lam,
                                               i * TM + off)
            outs = []
            for half in range(RS // 8):
                out = jnp.zeros((8, CT), F32)
                for r in range(8):
                    h = a[8 * half + r:8 * half + r + 1, :] * h + u[8 * half + r:8 * half + r + 1, :]
                    out = jnp.where(sub == r, h, out)
                outs.append(out)
            hs = jnp.concatenate(outs, axis=0)
            h_ref[rows, :] = hs
            ge, _ = _gelu(ag_ref[rows, :].astype(F32))
            ya_ref[rows, :] = (hs * ge).astype(BF16)
            return h

        hc[...] = lax.fori_loop(0, TM // RS, strip, hc[...])

    blk_ = pl.BlockSpec((TM, CT), lambda c, i: (i, c))
    vec = pl.BlockSpec((1, CT), lambda c, i: (0, c))
    return pl.pallas_call(
        body, name="rg_fwd", grid=(NCT, t // TM),
        in_specs=[blk_, pl.BlockSpec((TM, CT), lambda c, i: (i, NCT + c)), blk_,
                  pl.BlockSpec((TM, CT), lambda c, i: (i, COL_AG + c)), vec, vec, vec],
        out_specs=[blk_, blk_],
        out_shape=[jax.ShapeDtypeStruct((t, RGW), F32), jax.ShapeDtypeStruct((t, RGW), BF16)],
        scratch_shapes=[pltpu.VMEM((1, CT), F32)],
        compiler_params=_cparams(("parallel", "arbitrary")),
    )(gri, gri, xc, proj, ba, bx, lam)


def _rg_bwd(dya, h, gri, xc, proj, ba, bx, lam, dproj):
    t = gri.shape[0]
    nt = t // TM
    assert NCT == 1

    def body(dya_ref, h_ref, hh_ref, gr_ref, gi_ref, xc_ref, ag_ref, ba_ref, bx_ref, lam_ref, _,
             dgri_ref, dxc_ref, dag_ref, dba_ref, dbx_ref, dlam_ref, hbuf, acc, cc):
        i = pl.program_id(1)
        ri = nt - 1 - i

        @pl.when(i == 0)
        def _():
            cc[...] = jnp.zeros_like(cc)
            dba_ref[...] = jnp.zeros_like(dba_ref)
            dbx_ref[...] = jnp.zeros_like(dbx_ref)
            dlam_ref[...] = jnp.zeros_like(dlam_ref)

        acc[...] = jnp.zeros_like(acc)
        ba, bx, lam = ba_ref[...], bx_ref[...], lam_ref[...]
        halo_last = jnp.where(ri > 0, hh_ref[7:8, :], 0.0)
        sub = lax.broadcasted_iota(jnp.int32, (8, CT), 0)
        c0 = 0.7978845608028654

        def strip(kk, c):
            k = TM // RS - 1 - kk
            off = pl.multiple_of(k * RS, RS)
            rows = pl.ds(off, RS)
            xc, hv = xc_ref[rows, :], h_ref[rows, :]
            ag, dya = ag_ref[rows, :].astype(F32), dya_ref[rows, :].astype(F32)
            r, ig, sp, a, s, rs, _, live = _rg_gates(gr_ref[rows, :], gi_ref[rows, :], xc, ba, bx, lam, ri * TM + off)
            ge, th = _gelu(ag)
            dge = 0.5 * (1.0 + th) + 0.5 * ag * (1.0 - th * th) * c0 * (1.0 + 3.0 * 0.044715 * ag * ag)
            dag_ref[rows, :] = (dya * hv * dge).astype(BF16)
            d = dya * ge
            outs = []
            for half in range(RS // 8 - 1, -1, -1):
                out = jnp.zeros((8, CT), F32)
                for rr in range(7, -1, -1):
                    g = d[8 * half + rr:8 * half + rr + 1, :] + c
                    c = a[8 * half + rr:8 * half + rr + 1, :] * g
                    out = jnp.where(sub == rr, g, out)
                outs.insert(0, out)
            g = jnp.concatenate(outs, axis=0)
            before = h_ref[pl.ds(pl.multiple_of(jnp.maximum(off - RS, 0), RS), RS), :]
            hbuf[7:8, :] = jnp.where(k == 0, halo_last, before[RS - 1:RS, :])
            hbuf[8:, :] = hv
            hprev = hbuf[pl.ds(7, RS), :]
            du = jnp.where(live, g, 0.0)
            ds = du * (ig * xc)
            dm = du * s
            dla = (g * hprev - a * ds * rs) * a
            dr = dla * (-RG_C * sp) * r * (1.0 - r)
            di = dm * xc * ig * (1.0 - ig)
            dgri_ref[rows, :CT] = dr.astype(BF16)
            dgri_ref[rows, CT:] = di.astype(BF16)
            dxc_ref[rows, :] = dm * ig
            dl = dla * (-RG_C * r)
            for half in range(RS // 8):
                part = slice(8 * half, 8 * half + 8)
                acc[0] += dr[part]
                acc[1] += di[part]
                acc[2] += dl[part]
            return c

        cc[...] = lax.fori_loop(0, TM // RS, strip, cc[...])
        dba_ref[...] += jnp.sum(acc[0], axis=0, keepdims=True)
        dbx_ref[...] += jnp.sum(acc[1], axis=0, keepdims=True)
        dlam_ref[...] += jnp.sum(acc[2], axis=0, keepdims=True) * (-_sig(-lam))

    rblk = pl.BlockSpec((TM, CT), lambda c, i: (nt - 1 - i, c))
    vec = pl.BlockSpec((1, CT), lambda c, i: (0, c))
    hh = pl.BlockSpec((8, CT), lambda c, i: (jnp.maximum((nt - 1 - i) * (TM // 8) - 1, 0), c))
    agb = pl.BlockSpec((TM, CT), lambda c, i: (nt - 1 - i, COL_AG + c))
    return pl.pallas_call(
        body, name="rg_bwd", grid=(NCT, nt),
        in_specs=[rblk, rblk, hh, rblk, pl.BlockSpec((TM, CT), lambda c, i: (nt - 1 - i, NCT + c)), rblk, agb,
                  vec, vec, vec, ANY],
        out_specs=[pl.BlockSpec((TM, 2 * CT), lambda c, i: (nt - 1 - i, c)), rblk, agb, vec, vec, vec],
        out_shape=[jax.ShapeDtypeStruct((t, 2 * RGW), BF16),
                   jax.ShapeDtypeStruct((t, RGW), F32), jax.ShapeDtypeStruct((t, D_INP), BF16),
                   jax.ShapeDtypeStruct((1, RGW), F32), jax.ShapeDtypeStruct((1, RGW), F32),
                   jax.ShapeDtypeStruct((1, RGW), F32)],
        scratch_shapes=[pltpu.VMEM((RS + 8, CT), F32), pltpu.VMEM((3, 8, CT), F32), pltpu.VMEM((1, CT), F32)],
        input_output_aliases={10: 2},
        compiler_params=_cparams(("parallel", "arbitrary")),
    )(dya, h, h, gri, gri, xc, proj, ba, bx, lam, dproj)


NCH = TM // CH
HI = lax.Precision.HIGHEST


def _tri_dot(tri, x):
    hi = x.astype(BF16)
    r1 = x - hi.astype(F32)
    mid = r1.astype(BF16)
    lo = (r1 - mid.astype(F32)).astype(BF16)
    return _dot(tri, hi, NN) + _dot(tri, mid, NN) + _dot(tri, lo, NN)


def _hg_chunk(qr, fr, lb):
    sf = _sig(fr)
    fg = lb + (1.0 - lb) * sf
    k = (1.0 - lb) * (1.0 - sf)
    sq = _sig(qr)
    q = qr * sq
    ri = lax.broadcasted_iota(jnp.int32, (CH, CH), 0)
    ci = lax.broadcasted_iota(jnp.int32, (CH, CH), 1)
    b = _tri_dot((ri >= ci).astype(BF16), jnp.log(fg))
    bm, bl = b[CH // 2 - 1:CH // 2, :], b[CH - 1:CH, :]
    eb = jnp.exp(b)
    ebm = jnp.exp(b - bm)
    ekm = jnp.exp(jnp.minimum(bm - b, EXP_CLAMP))
    ekl = jnp.exp(bl - b)
    return dict(sf=sf, fg=fg, k=k, sq=sq, q=q, eb=eb, ebm=ebm, ekm=ekm, ekl=ekl, ebl=jnp.exp(bl),
                qe=q * eb, qh=q * ebm, kh=k * ekm, kd=k * ekl, causal=ri >= ci, upper=(ci >= ri).astype(BF16))


def _hgrn_fwd(proj, lbl, gn):
    t = proj.shape[0]
    nt = t // TM

    def body(q_ref, f_ref, v_ref, g_ref, lbl_ref, gn_ref, yb_ref, o_ref, st_ref, st):
        @pl.when(pl.program_id(0) == 0)
        def _():
            st[...] = jnp.zeros_like(st)

        l = lbl_ref[...]
        lb = _sig(l[0:1, :] - l[1:2, :])
        gnv = gn_ref[...]

        def chunk(c, carry):
            off = pl.multiple_of(c * CH, CH)
            rows = pl.ds(off, CH)
            z = _hg_chunk(q_ref[rows, :].astype(F32), f_ref[rows, :].astype(F32), lb)
            v, gg = v_ref[rows, :], g_ref[rows, :].astype(F32)
            for hh in range(HEADS):
                sl = slice(hh * HD, (hh + 1) * HD)
                s_prev = st[hh]
                st_ref[c, hh] = s_prev
                vb = v[:, sl].astype(BF16)
                att = jnp.where(z["causal"], _dot(z["qh"][:, sl].astype(BF16), z["kh"][:, sl].astype(BF16), NT), 0.0)
                o = _dot(z["qe"][:, sl].astype(BF16), s_prev.astype(BF16), NT) + _dot(att.astype(BF16), vb, NN)
                st[hh] = s_prev * z["ebl"][:, sl] + _dot(vb, z["kd"][:, sl].astype(BF16), TN_)
                xh, _ = _rms(o)
                gh = gg[:, sl]
                o_ref[rows, sl] = o
                yb_ref[rows, sl] = (xh * gnv * gh * _sig(gh)).astype(BF16)
            return carry

        lax.fori_loop(0, NCH, chunk, 0, unroll=True)

    def col(j):
        return pl.BlockSpec((TM, HGW), lambda i, j=j: (i, j))

    return pl.pallas_call(
        body, name="hgrn_fwd", grid=(nt,),
        in_specs=[col(0), col(1), col(2), col(3), pl.BlockSpec((2, HGW), lambda i: (0, 0)),
                  pl.BlockSpec((1, HD), lambda i: (0, 0))],
        out_specs=[col(0), col(0), pl.BlockSpec((NCH, HEADS, HD, HD), lambda i: (i, 0, 0, 0))],
        out_shape=[jax.ShapeDtypeStruct((t, HGW), BF16), jax.ShapeDtypeStruct((t, HGW), F32),
                   jax.ShapeDtypeStruct((t // CH, HEADS, HD, HD), F32)],
        scratch_shapes=[pltpu.VMEM((HEADS, HD, HD), F32)],
        compiler_params=_cparams(("arbitrary",)),
    )(proj, proj, proj, proj, lbl, gn)


def _hgrn_bwd(dyb, proj, o, states, lbl, gn, dproj):
    t = proj.shape[0]
    nt = t // TM

    def body(dy_ref, q_ref, f_ref, v_ref, g_ref, o_ref, st_ref, lbl_ref, gn_ref, _,
             dp_ref, dgn_ref, dl_ref, dst, dlb):
        i = pl.program_id(0)

        @pl.when(i == 0)
        def _():
            dst[...] = jnp.zeros_like(dst)
            dlb[...] = jnp.zeros_like(dlb)
            dgn_ref[...] = jnp.zeros_like(dgn_ref)

        l = lbl_ref[...]
        lb = _sig(l[0:1, :] - l[1:2, :])
        gnv = gn_ref[...]
        last = lax.broadcasted_iota(jnp.int32, (CH, HD), 0) == CH - 1

        def chunk(cc, carry):
            c = NCH - 1 - cc
            off = pl.multiple_of(c * CH, CH)
            rows = pl.ds(off, CH)
            qr, fr = q_ref[rows, :].astype(F32), f_ref[rows, :].astype(F32)
            z = _hg_chunk(qr, fr, lb)
            v, gg, ov, dy = v_ref[rows, :], g_ref[rows, :].astype(F32), o_ref[rows, :], dy_ref[rows, :].astype(F32)
            dqs, dks, dbs, dvs, dgs = [], [], [], [], []
            dgn = jnp.zeros((1, HD), F32)
            for hh in range(HEADS):
                sl = slice(hh * HD, (hh + 1) * HD)
                s_prev, ds_new = st_ref[c, hh], dst[hh]
                qe, qh, kh, kd = z["qe"][:, sl], z["qh"][:, sl], z["kh"][:, sl], z["kd"][:, sl]
                ebl = z["ebl"][:, sl]
                gh, dyh = gg[:, sl], dy[:, sl]
                xh, rr = _rms(ov[:, sl])
                sg = _sig(gh)
                dyn = dyh * (gh * sg)
                dgs.append(dyh * (xh * gnv) * (sg * (1.0 + gh * (1.0 - sg))))
                dgn = dgn + jnp.sum(dyn * xh, axis=0, keepdims=True)
                dxh = dyn * gnv
                do = (rr * (dxh - xh * jnp.mean(dxh * xh, axis=-1, keepdims=True))).astype(BF16)
                vb, dsb = v[:, sl].astype(BF16), ds_new.astype(BF16)
                qeb, qhb, khb, kdb = (a.astype(BF16) for a in (qe, qh, kh, kd))
                att = jnp.where(z["causal"], _dot(qhb, khb, NT), 0.0).astype(BF16)
                datt = jnp.where(z["causal"], _dot(do, vb, NT), 0.0).astype(BF16)
                dvs.append(_dot(att, do, TN_) + _dot(kdb, dsb, NT))
                dqe = _dot(do, s_prev.astype(BF16), NN)
                dqh = _dot(datt, khb, NN)
                dkh = _dot(datt, qhb, TN_)
                dkd = _dot(vb, dsb, NN)
                qe, qh, kh, kd = (a.astype(F32) for a in (qeb, qhb, khb, kdb))
                dbl = (jnp.sum(dkd * kd, axis=0, keepdims=True)
                       + jnp.sum(ds_new * s_prev, axis=0, keepdims=True) * ebl)
                dqs.append(dqe * z["eb"][:, sl] + dqh * z["ebm"][:, sl])
                dks.append(dkh * z["ekm"][:, sl] + dkd * z["ekl"][:, sl])
                dbs.append(dqe * qe + dqh * qh - dkh * kh - dkd * kd + jnp.where(last, dbl, 0.0))
                dst[hh] = _dot(do, qeb, TN_) + ds_new * ebl
            dgn_ref[...] += dgn
            dq, dk, db = (jnp.concatenate(x, axis=1) for x in (dqs, dks, dbs))
            dlf = _tri_dot(z["upper"], db)
            sf, fg, sq = z["sf"], z["fg"], z["sq"]
            dmix = dlf / fg - dk
            dsf = dmix * (1.0 - lb)
            dlb[...] += jnp.sum(dmix * (1.0 - sf), axis=0, keepdims=True)
            dp_ref[rows, 0:HGW] = (dq * (sq * (1.0 + qr * (1.0 - sq)))).astype(BF16)
            dp_ref[rows, HGW:2 * HGW] = (dsf * sf * (1.0 - sf)).astype(BF16)
            dp_ref[rows, 2 * HGW:3 * HGW] = jnp.concatenate(dvs, axis=1).astype(BF16)
            dp_ref[rows, 3 * HGW:4 * HGW] = jnp.concatenate(dgs, axis=1).astype(BF16)
            return carry

        lax.fori_loop(0, NCH, chunk, 0, unroll=True)
        dl0 = dlb[...] * lb * (1.0 - lb)
        dl_ref[0:1, :] = dl0
        dl_ref[1:2, :] = -dl0

    def col(j):
        return pl.BlockSpec((TM, HGW), lambda i, j=j: (nt - 1 - i, j))

    return pl.pallas_call(
        body, name="hgrn_bwd", grid=(nt,),
        in_specs=[col(0), col(0), col(1), col(2), col(3), col(0),
                  pl.BlockSpec((NCH, HEADS, HD, HD), lambda i: (nt - 1 - i, 0, 0, 0)),
                  pl.BlockSpec((2, HGW), lambda i: (0, 0)), pl.BlockSpec((1, HD), lambda i: (0, 0)), ANY],
        out_specs=[pl.BlockSpec((TM, 4 * HGW), lambda i: (nt - 1 - i, 0)),
                   pl.BlockSpec((1, HD), lambda i: (0, 0)), pl.BlockSpec((2, HGW), lambda i: (0, 0))],
        out_shape=[jax.ShapeDtypeStruct((t, D_INP), BF16), jax.ShapeDtypeStruct((1, HD), F32),
                   jax.ShapeDtypeStruct((2, HGW), F32)],
        scratch_shapes=[pltpu.VMEM((HEADS, HD, HD), F32), pltpu.VMEM((1, HGW), F32)],
        input_output_aliases={9: 0},
        compiler_params=_cparams(("arbitrary",)),
    )(dyb, proj, proj, proj, proj, o, states, lbl, gn, dproj)


def _rotate_w_in(w):
    return jnp.concatenate([w[:, ROT:], jnp.zeros((w.shape[0], COL_PAD), w.dtype), w[:, :ROT]], axis=1)


def _unrotate_g_in(g):
    return jnp.concatenate([g[:, D_INP - ROT:], g[:, :D_IN - ROT]], axis=1)


SB = RGW // 2


def _bd_gates(t):
    tm = _pick(t, (768, 256))
    return (t // tm, 4, 1), (tm, SB, SB), (lambda i, j, k: (i, j % 2), lambda i, j, k: (j % 2, j), lambda i, j, k: (i, j))


def _bd_dw(t):
    tk = _pick(t, (1408, 768))
    return (1, 4, t // tk), (SB, SB, tk), (lambda i, j, k: (k, j % 2), lambda i, j, k: (k, j), lambda i, j, k: (j % 2, j))


def _block_diag(w):
    eye = jnp.eye(RG_BLOCKS, dtype=w.dtype)
    return (w[:, :, None, :] * eye[:, None, :, None]).reshape(RGW, RGW)


def _diag_blocks(wd):
    w4 = wd.reshape(RG_BLOCKS, RG_BLOCK, RG_BLOCKS, RG_BLOCK)
    return jnp.stack([w4[n, :, n, :] for n in range(RG_BLOCKS)])


def _local_step(h0, target, wts, small, hosts=None, finalize=None):
    hosts = hosts or {}
    carried = {}
    g = {}
    t = h0.shape[0]

    def mm(a, b, mode, out_dtype, name, **kw):
        if name not in hosts:
            return _mm(a, b, mode, out_dtype, name, **kw)
        host = hosts[name](g)
        out, res = _mm(a, b, mode, out_dtype, name, host=host, **kw)
        carried[name] = (host.ins, res)
        return out

    w_in = wts["w_in"]
    wax_d = jnp.concatenate([_block_diag(small["rg_wa"]), _block_diag(small["rg_wx"])], axis=1).astype(BF16)
    ba, bx, lam = small["rg_ba"], small["rg_bx"], small["rg_lambda"]
    lbl, gn = small["hg_lb_logits"], small["hg_norm_g"]
    conv_w, conv_b = small["conv_w"], small["conv_b"]

    tmm = _pick(t, (768, 256))
    proj, hn1 = mm(h0, w_in, "nn", BF16, "mm_proj", norm_gain=small["norm1_g"])
    if finalize is not None:
        wts = finalize(carried["mm_proj"][1])
    w_pa, w_pb, w_out, w_fi, w_fd = (wts[k] for k in ("w_proj_a", "w_proj_b", "w_out", "w_ffn_in", "w_ffn_down"))
    xc, xcb = _conv_fwd(proj, conv_w, conv_b)
    gri = mm(xcb, wax_d, "nn", F32, "mm_rg_gates", sparse=_bd_gates(t))
    hrg, ya = _rg_fwd(gri, xc, proj, ba, bx, lam)
    yb, o, states = _hgrn_fwd(proj, lbl, gn)
    pa = mm(ya, w_pa, "nn", BF16, "mm_pa")
    pb = mm(yb, w_pb, "nn", BF16, "mm_pb")
    merged = _merge_fwd(pa, pb, proj)
    h1 = mm(merged, w_out, "nn", F32, "mm_out", resid=h0)
    (gu, act), hn2 = mm(h1, w_fi, "nn", BF16, "mm_ffn_in", norm_gain=small["norm2_g"], epi=_swiglu_epi(t),
                        tiles=(FM, 2 * FH))
    dh2, dh2b, loss, g["norm_f_g"] = mm(act, w_fd, "nn", F32, "mm_ffn_down", resid=h1,
                                        epi=_final_loss_epi(small["norm_f_g"], target, t), tiles=(TM, D))

    g["w_ffn_down"] = mm(act, dh2b, "tn", BF16, "mm_d_wfd")
    (dgu,) = mm(dh2b, w_fd, "nt", BF16, "mm_d_act", epi=_swiglu_bwd_epi(gu), tiles=(FM, FH))
    g["w_ffn_in"] = mm(hn2, dgu, "tn", BF16, "mm_d_wfi")
    dh1, dh1b, g["norm2_g"] = mm(dgu, w_fi, "nt", BF16, "mm_d_hn2", epi=_rms_bwd_epi(h1, small["norm2_g"], dh2, tmm),
                                 tiles=(tmm, D))
    g["w_out"] = mm(merged, dh1b, "tn", BF16, "mm_d_wout")
    dpa, dpb, dproj = mm(dh1b, w_out, "nt", BF16, "mm_d_merged", epi=_merge_bwd_epi(pa, pb, proj, TM), tiles=(TM, D))
    dproj = _zero_pad_cols(dproj)
    g["w_proj_a"] = mm(ya, dpa, "tn", BF16, "mm_d_wpa")
    g["w_proj_b"] = mm(yb, dpb, "tn", BF16, "mm_d_wpb")
    dya = mm(dpa, w_pa, "nt", BF16, "mm_d_ya")
    dyb = mm(dpb, w_pb, "nt", BF16, "mm_d_yb")
    dproj, g["hg_norm_g"], g["hg_lb_logits"] = _hgrn_bwd(dyb, proj, o, states, lbl, gn, dproj)
    dgri, dxc, dproj, g["rg_ba"], g["rg_bx"], g["rg_lambda"] = _rg_bwd(dya, hrg, gri, xc, proj, ba, bx, lam, dproj)
    dwax = mm(xcb, dgri, "tn", F32, "mm_d_wax", sparse=_bd_dw(t))
    g["rg_wa"], g["rg_wx"] = _diag_blocks(dwax[:, :RGW]), _diag_blocks(dwax[:, RGW:])
    dxc = mm(dgri, wax_d, "nt", F32, "mm_d_xc", resid=dxc)
    dproj, g["conv_w"], g["conv_b"] = _conv_bwd(dxc, proj, conv_w, dproj)
    g["w_in"] = mm(hn1, dproj, "tn", BF16, "mm_d_win")
    dhn1 = mm(dproj, w_in, "nt", BF16, "mm_d_hn1")
    grad_x, g["meta_tokens"], g["norm1_g"] = _rmsnorm1_bwd(dhn1, h0, small["norm1_g"], dh1)
    return loss, grad_x, g, carried


def _adamw_math(w, g, m, v):
    m = B1 * m + (1.0 - B1) * g
    v = B2 * v + (1.0 - B2) * (g * g)
    m_hat = m / (1.0 - B1 ** STEP)
    v_hat = v / (1.0 - B2 ** STEP)
    return -LR * (m_hat / (jnp.sqrt(v_hat) + ADAM_EPS) + WD * w), m, v


def _adamw(w, g, m, v, name):
    r, c = w.shape
    tr = _pick(r, (256, 352, 320, 128, 64, 32, 16, 8))

    def body(w_ref, g_ref, m_ref, v_ref, d_ref, mo_ref, vo_ref):
        d_ref[...], mo_ref[...], vo_ref[...] = _adamw_math(w_ref[...], g_ref[...], m_ref[...], v_ref[...])

    blk = pl.BlockSpec((tr, c), lambda i: (i, 0))
    return pl.pallas_call(
        body, name=name, grid=(r // tr,), in_specs=[blk] * 4, out_specs=[blk] * 3,
        out_shape=[jax.ShapeDtypeStruct((r, c), F32)] * 3,
        compiler_params=_cparams(("parallel",)),
    )(w, g, m, v)


def _adamw_small(recv, w, m, v, name):
    rows = w.shape[0]
    tr = rows

    def body(r_ref, w_ref, m_ref, v_ref, g_ref, d_ref, mo_ref, vo_ref):
        g = r_ref[0]
        for p in range(1, 8):
            g = g + r_ref[p]
        g_ref[...] = g
        d_ref[...], mo_ref[...], vo_ref[...] = _adamw_math(w_ref[...], g, m_ref[...], v_ref[...])

    blk = pl.BlockSpec((tr, 128), lambda i: (i, 0))
    return pl.pallas_call(
        body, name=name, grid=(rows // tr,),
        in_specs=[pl.BlockSpec((8, tr, 128), lambda i: (0, i, 0)), blk, blk, blk], out_specs=[blk] * 4,
        out_shape=[jax.ShapeDtypeStruct((rows, 128), F32)] * 4,
        compiler_params=_cparams(("parallel",)),
    )(recv, w, m, v)


def _col_block(weight, px, py):
    return 2 * py + px if weight == "w_ffn_in" else 2 * px + py


def _sum_place(recv, gfull, weight, xyc, name):
    _, r, c = recv.shape
    tr = _pick(r, (128, 176, 160, 64, 32, 16))
    nb = r // tr

    def body(_, own_ref, r_ref, o_ref):
        g = own_ref[...].astype(F32)
        for p in range(7):
            g = g + r_ref[p].astype(F32)
        o_ref[...] = g

    if weight in COL_SHARDED:
        own = pl.BlockSpec((tr, c), lambda i, s: (s[2] * nb + i, _col_block(weight, s[0], s[1])))
    else:
        own = pl.BlockSpec((tr, c), lambda i, s: ((4 * s[0] + 2 * s[1] + s[2]) * nb + i, 0))
    return pl.pallas_call(
        body, name=name,
        grid_spec=pltpu.PrefetchScalarGridSpec(
            num_scalar_prefetch=1, grid=(nb,),
            in_specs=[own, pl.BlockSpec((7, tr, c), lambda i, s: (0, i, 0))],
            out_specs=pl.BlockSpec((tr, c), lambda i, s: (s[2] * nb + i, 0))),
        out_shape=jax.ShapeDtypeStruct((2 * r, c), F32),
        compiler_params=_cparams(("arbitrary",)),
    )(xyc, gfull, recv)


def _place_own(shard, slot, dtype, name):
    r, c = shard.shape
    tr = _pick(r, (256, 352, 320, 128, 32))

    def body(_, x_ref, o_ref):
        o_ref[...] = x_ref[...].astype(dtype)

    return pl.pallas_call(
        body, name=name,
        grid_spec=pltpu.PrefetchScalarGridSpec(
            num_scalar_prefetch=1, grid=(r // tr,),
            in_specs=[pl.BlockSpec((tr, c), lambda i, s: (i, 0))],
            out_specs=pl.BlockSpec((None, tr, c), lambda i, s: (s[0], i, 0))),
        out_shape=jax.ShapeDtypeStruct((4, r, c), dtype),
        compiler_params=_cparams(("arbitrary",)),
    )(slot, shard)


BIG = ("w_in", "w_proj_a", "w_proj_b", "w_out", "w_ffn_in", "w_ffn_down")
COL_SHARDED = ("w_in", "w_ffn_in")


def _coords():
    return lax.axis_index("x"), lax.axis_index("y"), lax.axis_index("c")


def _gather_host(bufs):
    n = len(bufs)

    def place():
        x, y, c = _coords()
        return x, y, c, [(1 - x, y), (x, 1 - y), (1 - x, 1 - y)]

    def slab(b, chip, half_of):
        half = b.shape[1] // 2
        return b.at[chip, pl.ds(pl.multiple_of(half_of * half, 8), half)]

    def copy(b, sems, w, k, chip, half_of, to):
        s = slab(b, chip, half_of)
        return pltpu.make_async_remote_copy(src_ref=s, dst_ref=s, send_sem=sems[0].at[w, k], recv_sem=sems[1].at[w, k],
                                            device_id=to, device_id_type=MESH)

    def start(_, outs, sems):
        x, y, c, chips = place()
        for w, b in enumerate(outs):
            for k, (cx, cy) in enumerate(chips):
                copy(b, sems, w, k, 2 * x + y, c, (cx, cy, c)).start()

    def mid(_, outs, sems):
        x, y, c, chips = place()
        for w, b in enumerate(outs):
            for k, (cx, cy) in enumerate(chips):
                copy(b, sems, w, k, 2 * cx + cy, c, (cx, cy, c)).wait_recv()
                copy(b, sems, w, 3 + k, 2 * cx + cy, c, (x, y, 1 - c)).start()

    def finish(_, outs, sems):
        x, y, c, chips = place()
        for w, b in enumerate(outs):
            for k, (cx, cy) in enumerate(chips):
                copy(b, sems, w, 3 + k, 2 * cx + cy, 1 - c, (x, y, 1 - c)).wait_recv()
        for w, b in enumerate(outs):
            for k, (cx, cy) in enumerate(chips):
                copy(b, sems, w, k, 2 * x + y, c, (cx, cy, c)).wait_send()
                copy(b, sems, w, 3 + k, 2 * cx + cy, c, (x, y, 1 - c)).wait_send()

    return _Host(ins=list(bufs), out_shapes=[jax.ShapeDtypeStruct(b.shape, b.dtype) for b in bufs],
                 aliases={i: i for i in range(n)},
                 sems=[pltpu.SemaphoreType.DMA((n, 6)), pltpu.SemaphoreType.DMA((n, 6))],
                 start=start, mid=mid, finish=finish)


def _run_host(host, name):
    n_in, n_out = len(host.ins), len(host.out_shapes)

    def body(*refs):
        ins, outs, sems = refs[:n_in], refs[n_in:n_in + n_out], refs[n_in + n_out:]
        host.start(ins, outs, sems)
        if host.mid is not None:
            host.mid(ins, outs, sems)
        host.finish(ins, outs, sems)

    return pl.pallas_call(
        body, name=name, in_specs=[ANY] * n_in, out_specs=[ANY] * n_out, out_shape=list(host.out_shapes),
        scratch_shapes=list(host.sems), input_output_aliases=dict(host.aliases),
    )(*host.ins)


def _peer(x, y, c, k):
    fx, fy, fc = (k >> 2) & 1, (k >> 1) & 1, k & 1
    return (x ^ fx if fx else x, y ^ fy if fy else y, c ^ fc if fc else c)


def _sub_shape(name, full_shape):
    r, c = full_shape
    return (r // 2, c // 4) if name in COL_SHARDED else (r // 8, c)


def _exchange_host(names, grads):
    n = len(names)
    shapes = [_sub_shape(k, g.shape) for k, g in zip(names, grads)]

    def copy(ins, outs, sems, w, k):
        x, y, c = _coords()
        px, py, pc = _peer(x, y, c, k)
        sr, sc = shapes[w]
        if names[w] in COL_SHARDED:
            col = _col_block(names[w], px, py) * sc
            src = ins[w].at[pl.ds(pl.multiple_of(pc * sr, 16), sr), pl.ds(pl.multiple_of(col, 128), sc)]
        else:
            src = ins[w].at[pl.ds(pl.multiple_of((4 * px + 2 * py + pc) * sr, 16), sr)]
        return pltpu.make_async_remote_copy(
            src_ref=src, dst_ref=outs[w].at[k - 1], send_sem=sems[0].at[w, k - 1], recv_sem=sems[1].at[w, k - 1],
            device_id=(px, py, pc), device_id_type=MESH)

    def start(ins, outs, sems):
        for w in range(n):
            for k in range(1, 8):
                copy(ins, outs, sems, w, k).start()

    def finish(ins, outs, sems):
        for w in range(n):
            for k in range(1, 8):
                copy(ins, outs, sems, w, k).wait_recv()
        for w in range(n):
            for k in range(1, 8):
                copy(ins, outs, sems, w, k).wait_send()

    return _Host(ins=list(grads), out_shapes=[jax.ShapeDtypeStruct((7,) + s, g.dtype) for s, g in zip(shapes, grads)],
                 aliases={}, sems=[pltpu.SemaphoreType.DMA((n, 7)), pltpu.SemaphoreType.DMA((n, 7))],
                 start=start, mid=None, finish=finish)


def _sibling_host(bufs):
    n = len(bufs)

    def copy(outs, sems, w, half_of):
        x, y, c = _coords()
        half = outs[w].shape[0] // 2
        rows = outs[w].at[pl.ds(pl.multiple_of(half_of * half, 8), half)]
        return pltpu.make_async_remote_copy(src_ref=rows, dst_ref=rows, send_sem=sems[0].at[w], recv_sem=sems[1].at[w],
                                            device_id=(x, y, 1 - c), device_id_type=MESH)

    def start(_, outs, sems):
        c = lax.axis_index("c")
        for w in range(n):
            copy(outs, sems, w, c).start()

    def finish(_, outs, sems):
        c = lax.axis_index("c")
        for w in range(n):
            copy(outs, sems, w, 1 - c).wait_recv()
        for w in range(n):
            copy(outs, sems, w, c).wait_send()

    return _Host(ins=list(bufs), out_shapes=[jax.ShapeDtypeStruct(b.shape, b.dtype) for b in bufs],
                 aliases={i: i for i in range(n)},
                 sems=[pltpu.SemaphoreType.DMA((n,)), pltpu.SemaphoreType.DMA((n,))], start=start, mid=None, finish=finish)


def _pack_host(pack):
    rows = pack.shape[0]

    def me_of():
        x, y, c = _coords()
        return x, y, c, 4 * x + 2 * y + c

    def copy(ins, outs, sems, k, slot):
        x, y, c, _ = me_of()
        return pltpu.make_async_remote_copy(src_ref=ins[0], dst_ref=outs[0].at[slot], send_sem=sems[0].at[k - 1],
                                            recv_sem=sems[1].at[k - 1], device_id=_peer(x, y, c, k), device_id_type=MESH)

    def start(ins, outs, sems):
        me = me_of()[3]
        pltpu.make_async_copy(ins[0], outs[0].at[me], sems[2]).start()
        for k in range(1, 8):
            copy(ins, outs, sems, k, me).start()

    def finish(ins, outs, sems):
        x, y, c, me = me_of()
        for k in range(1, 8):
            px, py, pc = _peer(x, y, c, k)
            copy(ins, outs, sems, k, 4 * px + 2 * py + pc).wait_recv()
        for k in range(1, 8):
            copy(ins, outs, sems, k, me).wait_send()
        pltpu.make_async_copy(ins[0], outs[0].at[me], sems[2]).wait()

    return _Host(ins=[pack], out_shapes=[jax.ShapeDtypeStruct((8, rows, 128), F32)], aliases={},
                 sems=[pltpu.SemaphoreType.DMA((7,)), pltpu.SemaphoreType.DMA((7,)), pltpu.SemaphoreType.DMA],
                 start=start, mid=None, finish=finish)


def _join_hosts(hosts):
    ins, outs, sems, aliases, spans = [], [], [], {}, []
    for h in hosts:
        spans.append((len(ins), len(h.ins), len(outs), len(h.out_shapes), len(sems), len(h.sems)))
        for i_in, i_out in h.aliases.items():
            aliases[len(ins) + i_in] = len(outs) + i_out
        ins, outs, sems = ins + list(h.ins), outs + list(h.out_shapes), sems + list(h.sems)

    def phase(which):
        fns = [getattr(h, which) for h in hosts]
        if all(f is None for f in fns):
            return None

        def run(i_refs, o_refs, s_refs):
            for f, (i0, n_i, o0, n_o, s0, n_s) in zip(fns, spans):
                if f is not None:
                    f(i_refs[i0:i0 + n_i], o_refs[o0:o0 + n_o], s_refs[s0:s0 + n_s])

        return run

    return _Host(ins, outs, aliases, sems, phase("start"), phase("mid"), phase("finish"))


def _allgather_pack(pack):
    rows = pack.shape[0]

    def body(p_ref, o_ref, send, recv, local):
        x, y, c = _coords()
        me = 4 * x + 2 * y + c
        mine = pltpu.make_async_copy(p_ref, o_ref.at[me], local)
        mine.start()

        def copy(k, slot):
            return pltpu.make_async_remote_copy(src_ref=p_ref, dst_ref=o_ref.at[slot], send_sem=send.at[k - 1],
                                                recv_sem=recv.at[k - 1], device_id=_peer(x, y, c, k), device_id_type=MESH)

        for k in range(1, 8):
            copy(k, me).start()
        for k in range(1, 8):
            px, py, pc = _peer(x, y, c, k)
            copy(k, 4 * px + 2 * py + pc).wait_recv()
        for k in range(1, 8):
            copy(k, me).wait_send()
        mine.wait()

    return pl.pallas_call(
        body, name="allgather_pack", in_specs=[pl.BlockSpec(memory_space=pltpu.VMEM)], out_specs=ANY,
        out_shape=jax.ShapeDtypeStruct((8, rows, 128), F32),
        scratch_shapes=[pltpu.SemaphoreType.DMA((7,)), pltpu.SemaphoreType.DMA((7,)), pltpu.SemaphoreType.DMA],
        compiler_params=pltpu.CompilerParams(vmem_limit_bytes=VMEM_LIMIT),
    )(pack)


SMALL = ("norm1_g", "conv_b", "rg_wa", "rg_ba", "rg_wx", "rg_bx", "rg_lambda", "hg_lb_logits", "hg_norm_g",
         "norm2_g", "norm_f_g")
SHARDED_SMALL = ("meta_tokens", "conv_w")
SEG = 1024


PACK_LATE = ("norm1_g", "meta_tokens", "loss")
PACK_EARLY = tuple(k for k in SMALL + SHARDED_SMALL if k not in PACK_LATE)


def _pack(vals, names):
    parts = []
    for k in names:
        v = vals[k].reshape(-1).astype(F32)
        parts.append(jnp.pad(v, (0, -v.shape[0] % SEG)))
    return jnp.concatenate(parts).reshape(-1, 128)


def _unpack(flat, shapes, names):
    out, off = {}, 0
    flat = flat.reshape(-1)
    for k in names:
        size = 1
        for s in shapes[k]:
            size *= s
        out[k] = flat[off:off + size].reshape(shapes[k])
        off += size + (-size % SEG)
    return out


ORDER = ("meta_tokens", "norm1_g", "w_in", "conv_w", "conv_b", "rg_wa", "rg_ba", "rg_wx", "rg_bx", "rg_lambda",
         "hg_lb_logits", "hg_norm_g", "w_proj_a", "w_proj_b", "w_out", "norm2_g", "w_ffn_in", "w_ffn_down", "norm_f_g")


def kernel(x, meta_tokens, norm1_g, w_in, conv_w, conv_b, rg_wa, rg_ba, rg_wx, rg_bx, rg_lambda, hg_lb_logits, hg_norm_g, w_proj_a, w_proj_b, w_out, norm2_g, w_ffn_in, w_ffn_down, norm_f_g, loss_target, m_meta_tokens, m_norm1_g, m_w_in, m_conv_w, m_conv_b, m_rg_wa, m_rg_ba, m_rg_wx, m_rg_bx, m_rg_lambda, m_hg_lb_logits, m_hg_norm_g, m_w_proj_a, m_w_proj_b, m_w_out, m_norm2_g, m_w_ffn_in, m_w_ffn_down, m_norm_f_g, v_meta_tokens, v_norm1_g, v_w_in, v_conv_w, v_conv_b, v_rg_wa, v_rg_ba, v_rg_wx, v_rg_bx, v_rg_lambda, v_hg_lb_logits, v_hg_norm_g, v_w_proj_a, v_w_proj_b, v_w_out, v_norm2_g, v_w_ffn_in, v_w_ffn_down, v_norm_f_g):
    args = dict(locals())
    w = {k: args[k] for k in ORDER}
    m = {k: args["m_" + k] for k in ORDER}
    v = {k: args["v_" + k] for k in ORDER}
    xi, yi, ci = _coords()
    chip = 2 * xi + yi
    slot = jnp.reshape(chip, (1,)).astype(jnp.int32)
    xyc = jnp.stack([xi, yi, ci]).astype(jnp.int32)

    def full_matrix(k, gth):
        if k == "w_ffn_in":
            return jnp.concatenate([gth[0], gth[2], gth[1], gth[3]], axis=1)
        if k in COL_SHARDED:
            return jnp.concatenate([gth[j] for j in range(4)], axis=1)
        return gth.reshape(-1, gth.shape[2])

    tiny = jnp.zeros((32, 384), F32)
    tiny = tiny.at[0:N_META, 0:256].set(meta_tokens).at[N_META:N_META + 4, 0:320].set(conv_w[0])
    slots = {k: _place_own(w[k][0], slot, BF16, "place_" + k) for k in BIG}
    w_in_all, tiny_all = _run_host(_gather_host([slots["w_in"], _place_own(tiny, slot, F32, "place_tiny")]),
                                   "allgather_w_in")
    later = [k for k in BIG if k != "w_in"]
    meta_full = jnp.transpose(tiny_all[:, 0:N_META, 0:256], (1, 0, 2)).reshape(N_META, D)
    conv_w_full = jnp.transpose(tiny_all[:, N_META:N_META + 4, 0:320], (1, 0, 2)).reshape(4, RGW)

    small = dict(norm1_g=norm1_g, conv_w=conv_w_full, conv_b=conv_b, rg_wa=rg_wa[0], rg_ba=rg_ba, rg_wx=rg_wx[0],
                 rg_bx=rg_bx, rg_lambda=rg_lambda, hg_lb_logits=hg_lb_logits, hg_norm_g=hg_norm_g,
                 norm2_g=norm2_g, norm_f_g=norm_f_g.reshape(1, D))

    hosts = {
        "mm_proj": lambda g: _gather_host([slots[k] for k in later]),
        "mm_d_act": lambda g: _exchange_host(["w_ffn_down"], [g["w_ffn_down"]]),
        "mm_d_hn2": lambda g: _exchange_host(["w_ffn_in"], [g["w_ffn_in"]]),
        "mm_d_win": lambda g: _join_hosts([
            _exchange_host(["w_out", "w_proj_a", "w_proj_b"], [g["w_out"], g["w_proj_a"], g["w_proj_b"]]),
            _pack_host(_pack(g, PACK_EARLY))]),
        "mm_d_hn1": lambda g: _exchange_host(["w_in"], [_unrotate_g_in(g["w_in"])]),
    }
    h0 = jnp.concatenate([jnp.zeros((PAD, D), F32), meta_full, x[0]], axis=0)
    loss_blk, grad_x, g, carried = _local_step(
        h0, loss_target[0], {"w_in": _rotate_w_in(full_matrix("w_in", w_in_all))}, small, hosts=hosts,
        finalize=lambda gathered: {k: full_matrix(k, gth) for k, gth in zip(later, gathered)})
    g["loss"] = loss_blk[0:1, 0:1]

    halves = {}
    sources = {"mm_d_act": ["w_ffn_down"], "mm_d_hn2": ["w_ffn_in"], "mm_d_win": ["w_out", "w_proj_a", "w_proj_b"],
               "mm_d_hn1": ["w_in"]}
    for name, keys in sources.items():
        partials, received = carried[name]
        for k, part, rec in zip(keys, partials, received):
            halves[k] = _sum_place(rec, part, k, xyc, "sum_" + k)
    g_big = dict(zip(BIG, _run_host(_sibling_host([halves[k] for k in BIG]), "sibling_swap")))
    small_shapes = {k: w[k].shape for k in SMALL}
    small_shapes["meta_tokens"], small_shapes["conv_w"], small_shapes["loss"] = (N_META, D), (1, 4, RGW), (1,)
    recv_packs = {PACK_EARLY: carried["mm_d_win"][1][3], PACK_LATE: _allgather_pack(_pack(g, PACK_LATE))}

    w_sm = {k: w[k] for k in SMALL}
    m_sm = {k: m[k] for k in SMALL}
    v_sm = {k: v[k] for k in SMALL}
    for d in (w_sm, m_sm, v_sm):
        d["meta_tokens"], d["conv_w"], d["loss"] = jnp.zeros((N_META, D), F32), jnp.zeros((1, 4, RGW), F32), jnp.zeros((1,), F32)
    outs_sm = [{}, {}, {}, {}]
    for names, rec in recv_packs.items():
        res = _adamw_small(rec, _pack(w_sm, names), _pack(m_sm, names), _pack(v_sm, names), "adamw_" + names[0])
        for acc, o in zip(outs_sm, res):
            acc.update(_unpack(o, small_shapes, names))
    grad, delta, new_m, new_v = ({k: o[k] for k in SMALL} for o in outs_sm)
    g_small_full = outs_sm[0]
    loss = g_small_full["loss"][0]
    g_shard = {"meta_tokens": lax.dynamic_slice(g_small_full["meta_tokens"], (0, chip * 256), (N_META, 256)),
               "conv_w": lax.dynamic_slice(g_small_full["conv_w"], (0, 0, chip * 320), (1, 4, 320))[0]}
    for k in BIG + SHARDED_SMALL:
        gk = g_big[k] if k in BIG else g_shard[k]
        wk, mk, vk = (d[k].reshape(gk.shape) for d in (w, m, v))
        dk, mk, vk = _adamw(wk, gk, mk, vk, "adamw_" + k)
        grad[k], delta[k], new_m[k], new_v[k] = (a.reshape(w[k].shape) for a in (gk, dk, mk, vk))

    return (loss, grad_x[None], *[grad[k] for k in ORDER], *[delta[k] for k in ORDER],
            *[new_m[k] for k in ORDER], *[new_v[k] for k in ORDER])
```

```python
import functools

import jax
import jax.numpy as jnp
from jax import lax
from jax.experimental import pallas as pl
from jax.experimental.pallas import tpu as pltpu

F32, BF16 = jnp.float32, jnp.bfloat16
D = 1024
N_META = 16
RGW = 1280
RG_BLOCKS, RG_BLOCK = 16, 80
RG_C = 8.0
HEADS, HD = 8, 128
HGW = HEADS * HD
DFF = 2816
D_IN = 2 * RGW + 6 * D
ROT = 2 * RGW
COL_PAD = 256
D_INP = D_IN + COL_PAD
EPS = 1e-6
CH = 64
TM = 256
ROW0 = TM
PAD = ROW0 - N_META
CT = RGW
RS = 16
NCT = RGW // CT
EXP_CLAMP = 80.0
VMEM_LIMIT = 56 * 1024 * 1024

LR, B1, B2, ADAM_EPS, WD, STEP = 0.001, 0.9, 0.999, 1e-08, 0.01, 10
MESH = pl.DeviceIdType.MESH
ANY = pl.BlockSpec(memory_space=pl.ANY)


def _cparams(sem):
    return pltpu.CompilerParams(dimension_semantics=sem, vmem_limit_bytes=VMEM_LIMIT)


def _pick(n, prefs):
    for p in prefs:
        if n % p == 0:
            return p
    return n


def _sig(x):
    return 0.5 * jnp.tanh(0.5 * x) + 0.5


def _dot(a, b, dims, precision=None):
    return lax.dot_general(a, b, (dims, ((), ())), preferred_element_type=F32, precision=precision)


NN, NT, TN_ = ((1,), (0,)), ((1,), (1,)), ((0,), (0,))


class _Host:
    def __init__(self, ins, out_shapes, aliases, sems, start, mid, finish):
        self.ins, self.out_shapes, self.aliases, self.sems = ins, out_shapes, aliases, sems
        self.start, self.mid, self.finish = start, mid, finish


class _Epi:
    def __init__(self, ins, in_specs, out_shapes, out_specs, fn, sequential=False):
        self.ins, self.in_specs, self.out_shapes, self.out_specs, self.fn = ins, in_specs, out_shapes, out_specs, fn
        self.sequential = sequential


def _mm(a, b, mode, out_dtype, name, resid=None, host=None, epi=None, tiles=None, sparse=None, norm_gain=None):
    if mode == "nn":
        (m, kd), n = a.shape, b.shape[1]
    elif mode == "nt":
        (m, kd), n = a.shape, b.shape[0]
    else:
        (kd, m), n = a.shape, b.shape[1]
    if mode == "tn":
        tm = _pick(m, (1024, 1280, 1408, 640, 512, 256, 128))
        tk = _pick(kd, (1408, 768, 512, 256, 128))
    else:
        tm = _pick(m, (768, 512, 640, 256, 128))
        tk = kd if kd <= 2816 else _pick(kd, (1792, 1408, 1024, 512, 256, 128))
    tn = _pick(n, (1792, 1408, 1280, 1024, 512, 256, 128))
    if tiles is not None:
        tm, tn = tiles
    a_map = (lambda i, j, k: (k, i)) if mode == "tn" else (lambda i, j, k: (i, k))
    b_map = (lambda i, j, k: (j, k)) if mode == "nt" else (lambda i, j, k: (k, j))
    o_map = lambda i, j, k: (i, j)
    if sparse is not None:
        (ni, nj, nk), (tm, tn, tk), (a_map, b_map, o_map) = sparse
    else:
        assert m % tm == 0 and n % tn == 0 and kd % tk == 0, (name, m, n, kd, tm, tn, tk)
        ni, nj, nk = m // tm, n // tn, kd // tk
    dims = {"nn": NN, "nt": NT, "tn": TN_}[mode]

    n_hin = len(host.ins) if host else 0
    n_hout = len(host.out_shapes) if host else 0
    n_res = 0 if resid is None else 1
    n_pro = 0 if norm_gain is None else 1
    n_ein = len(epi.ins) if epi else 0
    n_out = len(epi.out_shapes) if epi else 1
    assert not n_pro or (mode == "nn" and nk == 1)

    def finish(r, r_ref, e_in, o_refs, i):
        if resid is not None:
            r = r + r_ref[...]
        if epi:
            epi.fn(r, e_in, o_refs, i)
        else:
            o_refs[0][...] = r.astype(out_dtype)

    def body(*refs):
        a_ref, b_ref = refs[:2]
        r_ref = refs[2] if resid is not None else None
        pos = 2 + n_res
        g_ref = refs[pos] if n_pro else None
        pos += n_pro
        e_in = refs[pos:pos + n_ein]
        pos += n_ein
        h_in = refs[pos:pos + n_hin]
        pos += n_hin
        o_ref = refs[pos:pos + n_out]
        pos += n_out
        hn_ref = refs[pos] if n_pro else None
        pos += n_pro
        h_out = refs[pos:pos + n_hout]
        scratch = refs[pos + n_hout:]
        n_acc = 1 if nk > 1 else 0
        h_sems = scratch[n_acc + n_pro:]
        i, j, k = pl.program_id(0), pl.program_id(1), pl.program_id(2)
        if host:
            @pl.when((i == 0) & (j == 0) & (k == 0))
            def _():
                host.start(h_in, h_out, h_sems)

            if host.mid is not None:
                @pl.when((i == (3 * ni) // 4) & (j == 0) & (k == 0))
                def _():
                    host.mid(h_in, h_out, h_sems)

        if n_pro:
            a_s = scratch[n_acc]

            @pl.when(j == 0)
            def _():
                xh, _ = _rms(a_ref[...])
                a_s[...] = (xh * g_ref[...]).astype(BF16)
                hn_ref[...] = a_s[...]

            a_val = a_s[...]
        else:
            a_val = a_ref[...]

        if nk == 1:
            finish(_dot(a_val, b_ref[...], dims), r_ref, e_in, o_ref, i)
        else:
            acc = scratch[0]

            @pl.when(k == 0)
            def _():
                acc[...] = jnp.zeros_like(acc)

            acc[...] += _dot(a_val, b_ref[...], dims)

            @pl.when(k == nk - 1)
            def _():
                finish(acc[...], r_ref, e_in, o_ref, i)

        if host:
            @pl.when((i == ni - 1) & (j == nj - 1) & (k == nk - 1))
            def _():
                host.finish(h_in, h_out, h_sems)

    a_spec = pl.BlockSpec((tk, tm) if mode == "tn" else (tm, tk), a_map)
    b_spec = pl.BlockSpec((tn, tk) if mode == "nt" else (tk, tn), b_map)
    o_spec = pl.BlockSpec((tm, tn), o_map)
    in_specs, args = [a_spec, b_spec], [a, b]
    if resid is not None:
        in_specs.append(o_spec)
        args.append(resid)
    if n_pro:
        in_specs.append(pl.BlockSpec((1, kd), lambda i, j, k: (0, 0)))
        args.append(norm_gain)
    out_shapes, out_specs = [jax.ShapeDtypeStruct((m, n), out_dtype)], [o_spec]
    if epi:
        in_specs += list(epi.in_specs)
        args += list(epi.ins)
        out_shapes, out_specs = list(epi.out_shapes), list(epi.out_specs)
    if n_pro:
        out_shapes.append(jax.ShapeDtypeStruct((m, kd), BF16))
        out_specs.append(pl.BlockSpec((tm, kd), lambda i, j, k: (i, 0)))
    scratch = ([pltpu.VMEM((tm, tn), F32)] if nk > 1 else []) + ([pltpu.VMEM((tm, kd), BF16)] if n_pro else [])
    sequential = host or n_pro or (epi and epi.sequential)
    outs = pl.pallas_call(
        body, name=name, grid=(ni, nj, nk),
        in_specs=in_specs + [ANY] * n_hin, out_specs=out_specs + [ANY] * n_hout,
        out_shape=out_shapes + (list(host.out_shapes) if host else []),
        scratch_shapes=scratch + (list(host.sems) if host else []),
        input_output_aliases=({2 + n_res + n_pro + n_ein + i_in: n_out + n_pro + i_out
                               for i_in, i_out in host.aliases.items()} if host else {}),
        compiler_params=_cparams(("arbitrary",) * 3 if sequential else ("parallel", "parallel", "arbitrary")),
    )(*args, *(host.ins if host else []))
    main = list(outs[:n_out]) if epi else outs[0]
    if n_pro:
        main = (main, outs[n_out])
    return (main, list(outs[n_out + n_pro:])) if host else main


def _rms(x):
    r = lax.rsqrt(jnp.mean(x * x, axis=-1, keepdims=True) + EPS)
    return x * r, r


def _rms_bwd_math(dhn, x, g):
    xh, r = _rms(x)
    dxh = dhn * g
    dx = r * (dxh - xh * jnp.mean(dxh * xh, axis=-1, keepdims=True))
    return dx, jnp.sum(dhn * xh, axis=0, keepdims=True)


def _row_specs(tm):
    return pl.BlockSpec((tm, D), lambda i, j, k: (i, 0)), pl.BlockSpec((1, D), lambda i, j, k: (0, 0))


def _final_loss_epi(gf, target, t):
    def fn(x, ins, outs, i):
        g_ref, t_ref = ins
        dh_ref, dhb_ref, loss_ref, dg_ref = outs

        @pl.when(i == 0)
        def _():
            loss_ref[...] = jnp.zeros_like(loss_ref)
            dg_ref[...] = jnp.zeros_like(dg_ref)

        g = g_ref[...]
        xh, _ = _rms(x)
        err = jnp.where(i > 0, xh * g - t_ref[...], 0.0)
        loss_ref[...] += 0.5 * jnp.sum(jnp.sum(err * err, axis=-1, keepdims=True) * (1.0 / D))
        dx, dg = _rms_bwd_math(err * (1.0 / D), x, g)
        dh_ref[...] = dx
        dhb_ref[...] = dx.astype(BF16)
        dg_ref[...] += dg

    row, vec = _row_specs(TM)
    return _Epi(ins=[gf, target], in_specs=[vec, pl.BlockSpec((TM, D), lambda i, j, k: (jnp.maximum(i - 1, 0), 0))],
                out_shapes=[jax.ShapeDtypeStruct((t, D), F32), jax.ShapeDtypeStruct((t, D), BF16),
                            jax.ShapeDtypeStruct((8, 128), F32), jax.ShapeDtypeStruct((1, D), F32)],
                out_specs=[row, row, pl.BlockSpec((8, 128), lambda i, j, k: (0, 0)), vec], fn=fn, sequential=True)


def _rms_bwd_epi(h, g, dres, tm):
    t = h.shape[0]

    def fn(dhn, ins, outs, i):
        h_ref, g_ref, dres_ref = ins
        dh_ref, dhb_ref, dg_ref = outs

        @pl.when(i == 0)
        def _():
            dg_ref[...] = jnp.zeros_like(dg_ref)

        dx, dg = _rms_bwd_math(dhn, h_ref[...], g_ref[...])
        dh = dres_ref[...] + dx
        dh_ref[...] = dh
        dhb_ref[...] = dh.astype(BF16)
        dg_ref[...] += dg

    row, vec = _row_specs(tm)
    return _Epi(ins=[h, g, dres], in_specs=[row, vec, row],
                out_shapes=[jax.ShapeDtypeStruct((t, D), F32), jax.ShapeDtypeStruct((t, D), BF16),
                            jax.ShapeDtypeStruct((1, D), F32)],
                out_specs=[row, row, vec], fn=fn, sequential=True)


def _rmsnorm1_bwd(dhn, h, g, dres):
    t = h.shape[0]

    def body(dhn_ref, h_ref, g_ref, dres_ref, dx_ref, dmeta_ref, dg_ref):
        i = pl.program_id(0)

        @pl.when(i == 0)
        def _():
            dg_ref[...] = jnp.zeros_like(dg_ref)

        dx, dg = _rms_bwd_math(dhn_ref[...].astype(F32), h_ref[...], g_ref[...])
        dh = dres_ref[...] + dx
        dg_ref[...] += dg
        dx_ref[...] = dh

        @pl.when(i == 0)
        def _():
            dmeta_ref[...] = dh[PAD:, :]

    row = pl.BlockSpec((TM, D), lambda i: (i, 0))
    vec = pl.BlockSpec((1, D), lambda i: (0, 0))
    return pl.pallas_call(
        body, name="rmsnorm1_bwd", grid=(t // TM,),
        in_specs=[row, row, vec, row],
        out_specs=[pl.BlockSpec((TM, D), lambda i: (jnp.maximum(i - 1, 0), 0)),
                   pl.BlockSpec((N_META, D), lambda i: (0, 0)), vec],
        out_shape=[jax.ShapeDtypeStruct((t - ROW0, D), F32), jax.ShapeDtypeStruct((N_META, D), F32),
                   jax.ShapeDtypeStruct((1, D), F32)],
        compiler_params=_cparams(("arbitrary",)),
    )(dhn, h, g, dres)


FH = DFF // 2
FM = 384


def _swiglu_epi(t):
    def fn(r, _, outs, i):
        g, u = r[:, :FH], r[:, FH:]
        outs[0][...] = r.astype(BF16)
        outs[1][...] = (g * _sig(g) * u).astype(BF16)

    return _Epi(ins=[], in_specs=[],
                out_shapes=[jax.ShapeDtypeStruct((t, 2 * DFF), BF16), jax.ShapeDtypeStruct((t, DFF), BF16)],
                out_specs=[pl.BlockSpec((FM, 2 * FH), lambda i, j, k: (i, j)), pl.BlockSpec((FM, FH), lambda i, j, k: (i, j))],
                fn=fn)


def _swiglu_bwd_epi(gu):
    def fn(d, ins, outs, i):
        gu_t = ins[0][...].astype(F32)
        g, u = gu_t[:, :FH], gu_t[:, FH:]
        s = _sig(g)
        outs[0][:, :FH] = (d * u * (s * (1.0 + g * (1.0 - s)))).astype(BF16)
        outs[0][:, FH:] = (d * (g * s)).astype(BF16)

    spec = pl.BlockSpec((FM, 2 * FH), lambda i, j, k: (i, j))
    return _Epi(ins=[gu], in_specs=[spec], out_shapes=[jax.ShapeDtypeStruct(gu.shape, BF16)], out_specs=[spec], fn=fn)


COL_GA, COL_GB = 4, 5
COL_AX, COL_AG = (6 * D + COL_PAD) // CT, (6 * D + COL_PAD + RGW) // CT


def _merge_fwd(pa, pb, proj):
    t = pa.shape[0]

    def body(pa_ref, pb_ref, ga_ref, gb_ref, o_ref):
        ga, gb, pa, pb = (r[...].astype(F32) for r in (ga_ref, gb_ref, pa_ref, pb_ref))
        o_ref[...] = (_sig(ga) * pa + _sig(gb) * pb).astype(BF16)

    row = pl.BlockSpec((TM, D), lambda i: (i, 0))
    return pl.pallas_call(
        body, name="merge_fwd", grid=(t // TM,),
        in_specs=[row, row, pl.BlockSpec((TM, D), lambda i: (i, COL_GA)), pl.BlockSpec((TM, D), lambda i: (i, COL_GB))],
        out_specs=row, out_shape=jax.ShapeDtypeStruct((t, D), BF16),
        compiler_params=_cparams(("parallel",)),
    )(pa, pb, proj, proj)


def _merge_bwd_epi(pa, pb, proj, tm):
    t = pa.shape[0]

    def fn(d, ins, outs, i):
        pa, pb, ga, gb = (r[...].astype(F32) for r in ins)
        dpa_ref, dpb_ref, dproj_ref = outs
        sa, sb = _sig(ga), _sig(gb)
        dpa_ref[...] = (sa * d).astype(BF16)
        dpb_ref[...] = (sb * d).astype(BF16)
        dproj_ref[:, :D] = (d * pa * sa * (1.0 - sa)).astype(BF16)
        dproj_ref[:, D:] = (d * pb * sb * (1.0 - sb)).astype(BF16)

    row, _ = _row_specs(tm)
    return _Epi(ins=[pa, pb, proj, proj],
                in_specs=[row, row, pl.BlockSpec((tm, D), lambda i, j, k: (i, COL_GA)),
                          pl.BlockSpec((tm, D), lambda i, j, k: (i, COL_GB))],
                out_shapes=[jax.ShapeDtypeStruct((t, D), BF16), jax.ShapeDtypeStruct((t, D), BF16),
                            jax.ShapeDtypeStruct((t, D_INP), BF16)],
                out_specs=[row, row, pl.BlockSpec((tm, 2 * D), lambda i, j, k: (i, 2))], fn=fn)


def _zero_pad_cols(dproj):
    t = dproj.shape[0]
    tz = _pick(t, (768, 256))

    def body(_, o_ref):
        o_ref[...] = jnp.zeros_like(o_ref)

    return pl.pallas_call(
        body, name="dproj_pad", grid=(t // tz,), in_specs=[ANY],
        out_specs=pl.BlockSpec((tz, COL_PAD), lambda i: (i, (6 * D) // COL_PAD)),
        out_shape=jax.ShapeDtypeStruct(dproj.shape, BF16), input_output_aliases={0: 0},
        compiler_params=_cparams(("parallel",)),
    )(dproj)


HALO = 16


def _prev_halo(col0):
    return pl.BlockSpec((HALO, CT), lambda c, i: (jnp.maximum(i * (TM // HALO) - 1, 0), col0 + c))


def _conv_fwd(proj, w, b):
    t = proj.shape[0]

    def body(x_ref, halo_ref, w_ref, b_ref, o_ref, ob_ref, buf):
        i = pl.program_id(1)
        x = x_ref[...].astype(F32)
        buf[0:HALO, :] = jnp.where(i > 0, halo_ref[...].astype(F32), 0.0)
        buf[HALO:, :] = x
        wv = w_ref[...]
        y = b_ref[...] + wv[3:4, :] * x
        for j in range(3):
            y = y + wv[j:j + 1, :] * buf[pl.ds(HALO - 3 + j, TM), :]
        o_ref[...] = y
        ob_ref[...] = y.astype(BF16)

    blk = pl.BlockSpec((TM, CT), lambda c, i: (i, c))
    return pl.pallas_call(
        body, name="conv_fwd", grid=(NCT, t // TM),
        in_specs=[pl.BlockSpec((TM, CT), lambda c, i: (i, COL_AX + c)), _prev_halo(COL_AX),
                  pl.BlockSpec((4, CT), lambda c, i: (0, c)), pl.BlockSpec((1, CT), lambda c, i: (0, c))],
        out_specs=[blk, blk],
        out_shape=[jax.ShapeDtypeStruct((t, RGW), F32), jax.ShapeDtypeStruct((t, RGW), BF16)],
        scratch_shapes=[pltpu.VMEM((TM + HALO, CT), F32)],
        compiler_params=_cparams(("parallel", "parallel")),
    )(proj, proj, w, b)


def _conv_bwd(dxc, proj, w, dproj):
    t = proj.shape[0]
    nt = t // TM

    def body(d_ref, dn_ref, x_ref, halo_ref, w_ref, _, dx_ref, dw_ref, db_ref, bufd, bufx):
        i = pl.program_id(1)

        @pl.when(i == 0)
        def _():
            dw_ref[...] = jnp.zeros_like(dw_ref)
            db_ref[...] = jnp.zeros_like(db_ref)

        d = d_ref[...]
        bufd[0:TM, :] = d
        bufd[TM:, :] = jnp.where(i < nt - 1, dn_ref[...], 0.0)
        x = x_ref[...].astype(F32)
        bufx[0:HALO, :] = jnp.where(i > 0, halo_ref[...].astype(F32), 0.0)
        bufx[HALO:, :] = x
        wv = w_ref[...]
        dx = wv[3:4, :] * d
        for j in range(3):
            dx = dx + wv[j:j + 1, :] * bufd[pl.ds(3 - j, TM), :]
            dw_ref[j:j + 1, :] += jnp.sum(d * bufx[pl.ds(HALO - 3 + j, TM), :], axis=0, keepdims=True)
        dw_ref[3:4, :] += jnp.sum(d * x, axis=0, keepdims=True)
        db_ref[...] += jnp.sum(d, axis=0, keepdims=True)
        dx_ref[...] = dx.astype(BF16)

    return pl.pallas_call(
        body, name="conv_bwd", grid=(NCT, nt),
        in_specs=[pl.BlockSpec((TM, CT), lambda c, i: (i, c)),
                  pl.BlockSpec((8, CT), lambda c, i: (jnp.minimum((i + 1) * (TM // 8), t // 8 - 1), c)),
                  pl.BlockSpec((TM, CT), lambda c, i: (i, COL_AX + c)), _prev_halo(COL_AX),
                  pl.BlockSpec((4, CT), lambda c, i: (0, c)), ANY],
        out_specs=[pl.BlockSpec((TM, CT), lambda c, i: (i, COL_AX + c)),
                   pl.BlockSpec((4, CT), lambda c, i: (0, c)), pl.BlockSpec((1, CT), lambda c, i: (0, c))],
        out_shape=[jax.ShapeDtypeStruct((t, D_INP), BF16), jax.ShapeDtypeStruct((4, RGW), F32),
                   jax.ShapeDtypeStruct((1, RGW), F32)],
        scratch_shapes=[pltpu.VMEM((TM + 8, CT), F32), pltpu.VMEM((TM + HALO, CT), F32)],
        input_output_aliases={5: 0},
        compiler_params=_cparams(("parallel", "arbitrary")),
    )(dxc, dxc, proj, proj, w, dproj)


def _gelu(x):
    c = 0.7978845608028654
    th = jnp.tanh(c * (x + 0.044715 * x * x * x))
    return 0.5 * x * (1.0 + th), th


def _rg_gates(gr, gi, xc, ba, bx, lam, row0):
    r = _sig(gr + ba)
    ig = _sig(gi + bx)
    sp = jax.nn.softplus(-lam)
    a = jnp.exp(-RG_C * r * sp)
    s2 = jnp.maximum(1.0 - a * a, 1e-30)
    rs = lax.rsqrt(s2)
    s = s2 * rs
    rows = row0 + lax.broadcasted_iota(jnp.int32, gr.shape, 0)
    live = rows >= PAD
    u = jnp.where(live, s * ig * xc, 0.0)
    return r, ig, sp, a, s, rs, u, live


def _rg_fwd(gri, xc, proj, ba, bx, lam):
    t = gri.shape[0]

    def body(gr_ref, gi_ref, xc_ref, ag_ref, ba_ref, bx_ref, lam_ref, h_ref, ya_ref, hc):
        i = pl.program_id(1)

        @pl.when(i == 0)
        def _():
            hc[...] = jnp.zeros_like(hc)

        ba, bx, lam = ba_ref[...], bx_ref[...], lam_ref[...]
        sub = lax.broadcasted_iota(jnp.int32, (8, CT), 0)

        def strip(k, h):
            off = pl.multiple_of(k * RS, RS)
            rows = pl.ds(off, RS)
            _, _, _, a, _, _, u, _ = _rg_gates(gr_ref[rows, :], gi_ref[rows, :], xc_ref[rows, :], ba, bx, lam,
                                               i * TM + off)
            outs = []
            for half in range(RS // 8):
                out = jnp.zeros((8, CT), F32)
                for r in range(8):
                    h = a[8 * half + r:8 * half + r + 1, :] * h + u[8 * half + r:8 * half + r + 1, :]
                    out = jnp.where(sub == r, h, out)
                outs.append(out)
            hs = jnp.concatenate(outs, axis=0)
            h_ref[rows, :] = hs
            ge, _ = _gelu(ag_ref[rows, :].astype(F32))
            ya_ref[rows, :] = (hs * ge).astype(BF16)
            return h

        hc[...] = lax.fori_loop(0, TM // RS, strip, hc[...])

    blk_ = pl.BlockSpec((TM, CT), lambda c, i: (i, c))
    vec = pl.BlockSpec((1, CT), lambda c, i: (0, c))
    return pl.pallas_call(
        body, name="rg_fwd", grid=(NCT, t // TM),
        in_specs=[blk_, pl.BlockSpec((TM, CT), lambda c, i: (i, NCT + c)), blk_,
                  pl.BlockSpec((TM, CT), lambda c, i: (i, COL_AG + c)), vec, vec, vec],
        out_specs=[blk_, blk_],
        out_shape=[jax.ShapeDtypeStruct((t, RGW), F32), jax.ShapeDtypeStruct((t, RGW), BF16)],
        scratch_shapes=[pltpu.VMEM((1, CT), F32)],
        compiler_params=_cparams(("parallel", "arbitrary")),
    )(gri, gri, xc, proj, ba, bx, lam)


def _rg_bwd(dya, h, gri, xc, proj, ba, bx, lam, dproj):
    t = gri.shape[0]
    nt = t // TM
    assert NCT == 1

    def body(dya_ref, h_ref, hh_ref, gr_ref, gi_ref, xc_ref, ag_ref, ba_ref, bx_ref, lam_ref, _,
             dgri_ref, dxc_ref, dag_ref, dba_ref, dbx_ref, dlam_ref, hbuf, acc, cc):
        i = pl.program_id(1)
        ri = nt - 1 - i

        @pl.when(i == 0)
        def _():
            cc[...] = jnp.zeros_like(cc)
            dba_ref[...] = jnp.zeros_like(dba_ref)
            dbx_ref[...] = jnp.zeros_like(dbx_ref)
            dlam_ref[...] = jnp.zeros_like(dlam_ref)

        acc[...] = jnp.zeros_like(acc)
        ba, bx, lam = ba_ref[...], bx_ref[...], lam_ref[...]
        halo_last = jnp.where(ri > 0, hh_ref[7:8, :], 0.0)
        sub = lax.broadcasted_iota(jnp.int32, (8, CT), 0)
        c0 = 0.7978845608028654

        def strip(kk, c):
            k = TM // RS - 1 - kk
            off = pl.multiple_of(k * RS, RS)
            rows = pl.ds(off, RS)
            xc, hv = xc_ref[rows, :], h_ref[rows, :]
            ag, dya = ag_ref[rows, :].astype(F32), dya_ref[rows, :].astype(F32)
            r, ig, sp, a, s, rs, _, live = _rg_gates(gr_ref[rows, :], gi_ref[rows, :], xc, ba, bx, lam, ri * TM + off)
            ge, th = _gelu(ag)
            dge = 0.5 * (1.0 + th) + 0.5 * ag * (1.0 - th * th) * c0 * (1.0 + 3.0 * 0.044715 * ag * ag)
            dag_ref[rows, :] = (dya * hv * dge).astype(BF16)
            d = dya * ge
            outs = []
            for half in range(RS // 8 - 1, -1, -1):
                out = jnp.zeros((8, CT), F32)
                for rr in range(7, -1, -1):
                    g = d[8 * half + rr:8 * half + rr + 1, :] + c
                    c = a[8 * half + rr:8 * half + rr + 1, :] * g
                    out = jnp.where(sub == rr, g, out)
                outs.insert(0, out)
            g = jnp.concatenate(outs, axis=0)
            before = h_ref[pl.ds(pl.multiple_of(jnp.maximum(off - RS, 0), RS), RS), :]
            hbuf[7:8, :] = jnp.where(k == 0, halo_last, before[RS - 1:RS, :])
            hbuf[8:, :] = hv
            hprev = hbuf[pl.ds(7, RS), :]
            du = jnp.where(live, g, 0.0)
            ds = du * (ig * xc)
            dm = du * s
            dla = (g * hprev - a * ds * rs) * a
            dr = dla * (-RG_C * sp) * r * (1.0 - r)
            di = dm * xc * ig * (1.0 - ig)
            dgri_ref[rows, :CT] = dr.astype(BF16)
            dgri_ref[rows, CT:] = di.astype(BF16)
            dxc_ref[rows, :] = dm * ig
            dl = dla * (-RG_C * r)
            for half in range(RS // 8):
                part = slice(8 * half, 8 * half + 8)
                acc[0] += dr[part]
                acc[1] += di[part]
                acc[2] += dl[part]
            return c

        cc[...] = lax.fori_loop(0, TM // RS, strip, cc[...])
        dba_ref[...] += jnp.sum(acc[0], axis=0, keepdims=True)
        dbx_ref[...] += jnp.sum(acc[1], axis=0, keepdims=True)
        dlam_ref[...] += jnp.sum(acc[2], axis=0, keepdims=True) * (-_sig(-lam))

    rblk = pl.BlockSpec((TM, CT), lambda c, i: (nt - 1 - i, c))
    vec = pl.BlockSpec((1, CT), lambda c, i: (0, c))
    hh = pl.BlockSpec((8, CT), lambda c, i: (jnp.maximum((nt - 1 - i) * (TM // 8) - 1, 0), c))
    agb = pl.BlockSpec((TM, CT), lambda c, i: (nt - 1 - i, COL_AG + c))
    return pl.pallas_call(
        body, name="rg_bwd", grid=(NCT, nt),
        in_specs=[rblk, rblk, hh, rblk, pl.BlockSpec((TM, CT), lambda c, i: (nt - 1 - i, NCT + c)), rblk, agb,
                  vec, vec, vec, ANY],
        out_specs=[pl.BlockSpec((TM, 2 * CT), lambda c, i: (nt - 1 - i, c)), rblk, agb, vec, vec, vec],
        out_shape=[jax.ShapeDtypeStruct((t, 2 * RGW), BF16),
                   jax.ShapeDtypeStruct((t, RGW), F32), jax.ShapeDtypeStruct((t, D_INP), BF16),
                   jax.ShapeDtypeStruct((1, RGW), F32), jax.ShapeDtypeStruct((1, RGW), F32),
                   jax.ShapeDtypeStruct((1, RGW), F32)],
        scratch_shapes=[pltpu.VMEM((RS + 8, CT), F32), pltpu.VMEM((3, 8, CT), F32), pltpu.VMEM((1, CT), F32)],
        input_output_aliases={10: 2},
        compiler_params=_cparams(("parallel", "arbitrary")),
    )(dya, h, h, gri, gri, xc, proj, ba, bx, lam, dproj)


NCH = TM // CH
HI = lax.Precision.HIGHEST


def _tri_dot(tri, x):
    hi = x.astype(BF16)
    r1 = x - hi.astype(F32)
    mid = r1.astype(BF16)
    lo = (r1 - mid.astype(F32)).astype(BF16)
    return _dot(tri, hi, NN) + _dot(tri, mid, NN) + _dot(tri, lo, NN)


def _hg_chunk(qr, fr, lb):
    sf = _sig(fr)
    fg = lb + (1.0 - lb) * sf
    k = (1.0 - lb) * (1.0 - sf)
    sq = _sig(qr)
    q = qr * sq
    ri = lax.broadcasted_iota(jnp.int32, (CH, CH), 0)
    ci = lax.broadcasted_iota(jnp.int32, (CH, CH), 1)
    b = _tri_dot((ri >= ci).astype(BF16), jnp.log(fg))
    bm, bl = b[CH // 2 - 1:CH // 2, :], b[CH - 1:CH, :]
    eb = jnp.exp(b)
    ebm = jnp.exp(b - bm)
    ekm = jnp.exp(jnp.minimum(bm - b, EXP_CLAMP))
    ekl = jnp.exp(bl - b)
    return dict(sf=sf, fg=fg, k=k, sq=sq, q=q, eb=eb, ebm=ebm, ekm=ekm, ekl=ekl, ebl=jnp.exp(bl),
                qe=q * eb, qh=q * ebm, kh=k * ekm, kd=k * ekl, causal=ri >= ci, upper=(ci >= ri).astype(BF16))


def _hgrn_fwd(proj, lbl, gn):
    t = proj.shape[0]
    nt = t // TM

    def body(q_ref, f_ref, v_ref, g_ref, lbl_ref, gn_ref, yb_ref, o_ref, st_ref, st):
        @pl.when(pl.program_id(0) == 0)
        def _():
            st[...] = jnp.zeros_like(st)

        l = lbl_ref[...]
        lb = _sig(l[0:1, :] - l[1:2, :])
        gnv = gn_ref[...]

        def chunk(c, carry):
            off = pl.multiple_of(c * CH, CH)
            rows = pl.ds(off, CH)
            z = _hg_chunk(q_ref[rows, :].astype(F32), f_ref[rows, :].astype(F32), lb)
            v, gg = v_ref[rows, :], g_ref[rows, :].astype(F32)
            for hh in range(HEADS):
                sl = slice(hh * HD, (hh + 1) * HD)
                s_prev = st[hh]
                st_ref[c, hh] = s_prev
                vb = v[:, sl].astype(BF16)
                att = jnp.where(z["causal"], _dot(z["qh"][:, sl].astype(BF16), z["kh"][:, sl].astype(BF16), NT), 0.0)
                o = _dot(z["qe"][:, sl].astype(BF16), s_prev.astype(BF16), NT) + _dot(att.astype(BF16), vb, NN)
                st[hh] = s_prev * z["ebl"][:, sl] + _dot(vb, z["kd"][:, sl].astype(BF16), TN_)
                xh, _ = _rms(o)
                gh = gg[:, sl]
                o_ref[rows, sl] = o
                yb_ref[rows, sl] = (xh * gnv * gh * _sig(gh)).astype(BF16)
            return carry

        lax.fori_loop(0, NCH, chunk, 0, unroll=True)

    def col(j):
        return pl.BlockSpec((TM, HGW), lambda i, j=j: (i, j))

    return pl.pallas_call(
        body, name="hgrn_fwd", grid=(nt,),
        in_specs=[col(0), col(1), col(2), col(3), pl.BlockSpec((2, HGW), lambda i: (0, 0)),
                  pl.BlockSpec((1, HD), lambda i: (0, 0))],
        out_specs=[col(0), col(0), pl.BlockSpec((NCH, HEADS, HD, HD), lambda i: (i, 0, 0, 0))],
        out_shape=[jax.ShapeDtypeStruct((t, HGW), BF16), jax.ShapeDtypeStruct((t, HGW), F32),
                   jax.ShapeDtypeStruct((t // CH, HEADS, HD, HD), F32)],
        scratch_shapes=[pltpu.VMEM((HEADS, HD, HD), F32)],
        compiler_params=_cparams(("arbitrary",)),
    )(proj, proj, proj, proj, lbl, gn)


def _hgrn_bwd(dyb, proj, o, states, lbl, gn, dproj):
    t = proj.shape[0]
    nt = t // TM

    def body(dy_ref, q_ref, f_ref, v_ref, g_ref, o_ref, st_ref, lbl_ref, gn_ref, _,
             dp_ref, dgn_ref, dl_ref, dst, dlb):
        i = pl.program_id(0)

        @pl.when(i == 0)
        def _():
            dst[...] = jnp.zeros_like(dst)
            dlb[...] = jnp.zeros_like(dlb)
            dgn_ref[...] = jnp.zeros_like(dgn_ref)

        l = lbl_ref[...]
        lb = _sig(l[0:1, :] - l[1:2, :])
        gnv = gn_ref[...]
        last = lax.broadcasted_iota(jnp.int32, (CH, HD), 0) == CH - 1

        def chunk(cc, carry):
            c = NCH - 1 - cc
            off = pl.multiple_of(c * CH, CH)
            rows = pl.ds(off, CH)
            qr, fr = q_ref[rows, :].astype(F32), f_ref[rows, :].astype(F32)
            z = _hg_chunk(qr, fr, lb)
            v, gg, ov, dy = v_ref[rows, :], g_ref[rows, :].astype(F32), o_ref[rows, :], dy_ref[rows, :].astype(F32)
            dqs, dks, dbs, dvs, dgs = [], [], [], [], []
            dgn = jnp.zeros((1, HD), F32)
            for hh in range(HEADS):
                sl = slice(hh * HD, (hh + 1) * HD)
                s_prev, ds_new = st_ref[c, hh], dst[hh]
                qe, qh, kh, kd = z["qe"][:, sl], z["qh"][:, sl], z["kh"][:, sl], z["kd"][:, sl]
                ebl = z["ebl"][:, sl]
                gh, dyh = gg[:, sl], dy[:, sl]
                xh, rr = _rms(ov[:, sl])
                sg = _sig(gh)
                dyn = dyh * (gh * sg)
                dgs.append(dyh * (xh * gnv) * (sg * (1.0 + gh * (1.0 - sg))))
                dgn = dgn + jnp.sum(dyn * xh, axis=0, keepdims=True)
                dxh = dyn * gnv
                do = (rr * (dxh - xh * jnp.mean(dxh * xh, axis=-1, keepdims=True))).astype(BF16)
                vb, dsb = v[:, sl].astype(BF16), ds_new.astype(BF16)
                qeb, qhb, khb, kdb = (a.astype(BF16) for a in (qe, qh, kh, kd))
                att = jnp.where(z["causal"], _dot(qhb, khb, NT), 0.0).astype(BF16)
                datt = jnp.where(z["causal"], _dot(do, vb, NT), 0.0).astype(BF16)
                dvs.append(_dot(att, do, TN_) + _dot(kdb, dsb, NT))
                dqe = _dot(do, s_prev.astype(BF16), NN)
                dqh = _dot(datt, khb, NN)
                dkh = _dot(datt, qhb, TN_)
                dkd = _dot(vb, dsb, NN)
                qe, qh, kh, kd = (a.astype(F32) for a in (qeb, qhb, khb, kdb))
                dbl = (jnp.sum(dkd * kd, axis=0, keepdims=True)
                       + jnp.sum(ds_new * s_prev, axis=0, keepdims=True) * ebl)
                dqs.append(dqe * z["eb"][:, sl] + dqh * z["ebm"][:, sl])
                dks.append(dkh * z["ekm"][:, sl] + dkd * z["ekl"][:, sl])
                dbs.append(dqe * qe + dqh * qh - dkh * kh - dkd * kd + jnp.where(last, dbl, 0.0))
                dst[hh] = _dot(do, qeb, TN_) + ds_new * ebl
            dgn_ref[...] += dgn
            dq, dk, db = (jnp.concatenate(x, axis=1) for x in (dqs, dks, dbs))
            dlf = _tri_dot(z["upper"], db)
            sf, fg, sq = z["sf"], z["fg"], z["sq"]
            dmix = dlf / fg - dk
            dsf = dmix * (1.0 - lb)
            dlb[...] += jnp.sum(dmix * (1.0 - sf), axis=0, keepdims=True)
            dp_ref[rows, 0:HGW] = (dq * (sq * (1.0 + qr * (1.0 - sq)))).astype(BF16)
            dp_ref[rows, HGW:2 * HGW] = (dsf * sf * (1.0 - sf)).astype(BF16)
            dp_ref[rows, 2 * HGW:3 * HGW] = jnp.concatenate(dvs, axis=1).astype(BF16)
            dp_ref[rows, 3 * HGW:4 * HGW] = jnp.concatenate(dgs, axis=1).astype(BF16)
            return carry

        lax.fori_loop(0, NCH, chunk, 0, unroll=True)
        dl0 = dlb[...] * lb * (1.0 - lb)
        dl_ref[0:1, :] = dl0
        dl_ref[1:2, :] = -dl0

    def col(j):
        return pl.BlockSpec((TM, HGW), lambda i, j=j: (nt - 1 - i, j))

    return pl.pallas_call(
        body, name="hgrn_bwd", grid=(nt,),
        in_specs=[col(0), col(0), col(1), col(2), col(3), col(0),
                  pl.BlockSpec((NCH, HEADS, HD, HD), lambda i: (nt - 1 - i, 0, 0, 0)),
                  pl.BlockSpec((2, HGW), lambda i: (0, 0)), pl.BlockSpec((1, HD), lambda i: (0, 0)), ANY],
        out_specs=[pl.BlockSpec((TM, 4 * HGW), lambda i: (nt - 1 - i, 0)),
                   pl.BlockSpec((1, HD), lambda i: (0, 0)), pl.BlockSpec((2, HGW), lambda i: (0, 0))],
        out_shape=[jax.ShapeDtypeStruct((t, D_INP), BF16), jax.ShapeDtypeStruct((1, HD), F32),
                   jax.ShapeDtypeStruct((2, HGW), F32)],
        scratch_shapes=[pltpu.VMEM((HEADS, HD, HD), F32), pltpu.VMEM((1, HGW), F32)],
        input_output_aliases={9: 0},
        compiler_params=_cparams(("arbitrary",)),
    )(dyb, proj, proj, proj, proj, o, states, lbl, gn, dproj)


def _rotate_w_in(w):
    return jnp.concatenate([w[:, ROT:], jnp.zeros((w.shape[0], COL_PAD), w.dtype), w[:, :ROT]], axis=1)


def _unrotate_g_in(g):
    return jnp.concatenate([g[:, D_INP - ROT:], g[:, :D_IN - ROT]], axis=1)


SB = RGW // 2


def _bd_gates(t):
    tm = _pick(t, (768, 256))
    return (t // tm, 4, 1), (tm, SB, SB), (lambda i, j, k: (i, j % 2), lambda i, j, k: (j % 2, j), lambda i, j, k: (i, j))


def _bd_dw(t):
    tk = _pick(t, (1408, 768))
    return (1, 4, t // tk), (SB, SB, tk), (lambda i, j, k: (k, j % 2), lambda i, j, k: (k, j), lambda i, j, k: (j % 2, j))


def _block_diag(w):
    eye = jnp.eye(RG_BLOCKS, dtype=w.dtype)
    return (w[:, :, None, :] * eye[:, None, :, None]).reshape(RGW, RGW)


def _diag_blocks(wd):
    w4 = wd.reshape(RG_BLOCKS, RG_BLOCK, RG_BLOCKS, RG_BLOCK)
    return jnp.stack([w4[n, :, n, :] for n in range(RG_BLOCKS)])


def _local_step(h0, target, wts, small, hosts=None, finalize=None):
    hosts = hosts or {}
    carried = {}
    g = {}
    t = h0.shape[0]

    def mm(a, b, mode, out_dtype, name, **kw):
        if name not in hosts:
            return _mm(a, b, mode, out_dtype, name, **kw)
        host = hosts[name](g)
        out, res = _mm(a, b, mode, out_dtype, name, host=host, **kw)
        carried[name] = (host.ins, res)
        return out

    w_in = wts["w_in"]
    wax_d = jnp.concatenate([_block_diag(small["rg_wa"]), _block_diag(small["rg_wx"])], axis=1).astype(BF16)
    ba, bx, lam = small["rg_ba"], small["rg_bx"], small["rg_lambda"]
    lbl, gn = small["hg_lb_logits"], small["hg_norm_g"]
    conv_w, conv_b = small["conv_w"], small["conv_b"]

    tmm = _pick(t, (768, 256))
    proj, hn1 = mm(h0, w_in, "nn", BF16, "mm_proj", norm_gain=small["norm1_g"])
    if finalize is not None:
        wts = finalize(carried["mm_proj"][1])
    w_pa, w_pb, w_out, w_fi, w_fd = (wts[k] for k in ("w_proj_a", "w_proj_b", "w_out", "w_ffn_in", "w_ffn_down"))
    xc, xcb = _conv_fwd(proj, conv_w, conv_b)
    gri = mm(xcb, wax_d, "nn", F32, "mm_rg_gates", sparse=_bd_gates(t))
    hrg, ya = _rg_fwd(gri, xc, proj, ba, bx, lam)
    yb, o, states = _hgrn_fwd(proj, lbl, gn)
    pa = mm(ya, w_pa, "nn", BF16, "mm_pa")
    pb = mm(yb, w_pb, "nn", BF16, "mm_pb")
    merged = _merge_fwd(pa, pb, proj)
    h1 = mm(merged, w_out, "nn", F32, "mm_out", resid=h0)
    (gu, act), hn2 = mm(h1, w_fi, "nn", BF16, "mm_ffn_in", norm_gain=small["norm2_g"], epi=_swiglu_epi(t),
                        tiles=(FM, 2 * FH))
    dh2, dh2b, loss, g["norm_f_g"] = mm(act, w_fd, "nn", F32, "mm_ffn_down", resid=h1,
                                        epi=_final_loss_epi(small["norm_f_g"], target, t), tiles=(TM, D))

    g["w_ffn_down"] = mm(act, dh2b, "tn", BF16, "mm_d_wfd")
    (dgu,) = mm(dh2b, w_fd, "nt", BF16, "mm_d_act", epi=_swiglu_bwd_epi(gu), tiles=(FM, FH))
    g["w_ffn_in"] = mm(hn2, dgu, "tn", BF16, "mm_d_wfi")
    dh1, dh1b, g["norm2_g"] = mm(dgu, w_fi, "nt", BF16, "mm_d_hn2", epi=_rms_bwd_epi(h1, small["norm2_g"], dh2, tmm),
                                 tiles=(tmm, D))
    g["w_out"] = mm(merged, dh1b, "tn", BF16, "mm_d_wout")
    dpa, dpb, dproj = mm(dh1b, w_out, "nt", BF16, "mm_d_merged", epi=_merge_bwd_epi(pa, pb, proj, TM), tiles=(TM, D))
    dproj = _zero_pad_cols(dproj)
    g["w_proj_a"] = mm(ya, dpa, "tn", BF16, "mm_d_wpa")
    g["w_proj_b"] = mm(yb, dpb, "tn", BF16, "mm_d_wpb")
    dya = mm(dpa, w_pa, "nt", BF16, "mm_d_ya")
    dyb = mm(dpb, w_pb, "nt", BF16, "mm_d_yb")
    dproj, g["hg_norm_g"], g["hg_lb_logits"] = _hgrn_bwd(dyb, proj, o, states, lbl, gn, dproj)
    dgri, dxc, dproj, g["rg_ba"], g["rg_bx"], g["rg_lambda"] = _rg_bwd(dya, hrg, gri, xc, proj, ba, bx, lam, dproj)
    dwax = mm(xcb, dgri, "tn", F32, "mm_d_wax", sparse=_bd_dw(t))
    g["rg_wa"], g["rg_wx"] = _diag_blocks(dwax[:, :RGW]), _diag_blocks(dwax[:, RGW:])
    dxc = mm(dgri, wax_d, "nt", F32, "mm_d_xc", resid=dxc)
    dproj, g["conv_w"], g["conv_b"] = _conv_bwd(dxc, proj, conv_w, dproj)
    g["w_in"] = mm(hn1, dproj, "tn", BF16, "mm_d_win")
    dhn1 = mm(dproj, w_in, "nt", BF16, "mm_d_hn1")
    grad_x, g["meta_tokens"], g["norm1_g"] = _rmsnorm1_bwd(dhn1, h0, small["norm1_g"], dh1)
    return loss, grad_x, g, carried


def _adamw_math(w, g, m, v):
    m = B1 * m + (1.0 - B1) * g
    v = B2 * v + (1.0 - B2) * (g * g)
    m_hat = m / (1.0 - B1 ** STEP)
    v_hat = v / (1.0 - B2 ** STEP)
    return -LR * (m_hat / (jnp.sqrt(v_hat) + ADAM_EPS) + WD * w), m, v


def _adamw(w, g, m, v, name):
    r, c = w.shape
    tr = _pick(r, (256, 352, 320, 128, 64, 32, 16, 8))

    def body(w_ref, g_ref, m_ref, v_ref, d_ref, mo_ref, vo_ref):
        d_ref[...], mo_ref[...], vo_ref[...] = _adamw_math(w_ref[...], g_ref[...], m_ref[...], v_ref[...])

    blk = pl.BlockSpec((tr, c), lambda i: (i, 0))
    return pl.pallas_call(
        body, name=name, grid=(r // tr,), in_specs=[blk] * 4, out_specs=[blk] * 3,
        out_shape=[jax.ShapeDtypeStruct((r, c), F32)] * 3,
        compiler_params=_cparams(("parallel",)),
    )(w, g, m, v)


def _adamw_rows(recv, params, extras, name):
    _, _, width = recv.shape
    n_par = len(params)

    def body(*refs):
        r_ref = refs[0]
        ins = refs[1:1 + 3 * n_par]
        outs = refs[1 + 3 * n_par:]
        g_all = r_ref[0]
        for p in range(1, 8):
            g_all = g_all + r_ref[p]
        for q, (row0, w, _, _) in enumerate(params):
            g = g_all[row0:row0 + w.shape[0], :]
            w_ref, m_ref, v_ref = ins[3 * q:3 * q + 3]
            g_ref, d_ref, mo_ref, vo_ref = outs[4 * q:4 * q + 4]
            g_ref[...] = g
            d_ref[...], mo_ref[...], vo_ref[...] = _adamw_math(w_ref[...], g, m_ref[...], v_ref[...])
        for q, (row0, n) in enumerate(extras):
            outs[4 * n_par + q][...] = g_all[row0:row0 + n, :]

    shapes = [jax.ShapeDtypeStruct(w.shape, F32) for _, w, _, _ in params for _ in range(4)]
    shapes += [jax.ShapeDtypeStruct((n, width), F32) for _, n in extras]
    vm = pl.BlockSpec(memory_space=pltpu.VMEM)
    outs = pl.pallas_call(
        body, name=name, in_specs=[vm] * (1 + 3 * n_par), out_specs=[vm] * len(shapes), out_shape=shapes,
        compiler_params=pltpu.CompilerParams(vmem_limit_bytes=VMEM_LIMIT),
    )(recv, *[a for _, w, m, v in params for a in (w, m, v)])
    return [tuple(outs[4 * q:4 * q + 4]) for q in range(n_par)], list(outs[4 * n_par:])


def _col_block(weight, px, py):
    return 2 * py + px if weight == "w_ffn_in" else 2 * px + py


def _sum_place(recv, gfull, weight, xyc, name):
    _, r, c = recv.shape
    tr = _pick(r, (128, 176, 160, 64, 32, 16))
    nb = r // tr

    def body(_, own_ref, r_ref, o_ref):
        g = own_ref[...].astype(F32)
        for p in range(7):
            g = g + r_ref[p].astype(F32)
        o_ref[...] = g

    if weight in COL_SHARDED:
        own = pl.BlockSpec((tr, c), lambda i, s: (s[2] * nb + i, _col_block(weight, s[0], s[1])))
    else:
        own = pl.BlockSpec((tr, c), lambda i, s: ((4 * s[0] + 2 * s[1] + s[2]) * nb + i, 0))
    return pl.pallas_call(
        body, name=name,
        grid_spec=pltpu.PrefetchScalarGridSpec(
            num_scalar_prefetch=1, grid=(nb,),
            in_specs=[own, pl.BlockSpec((7, tr, c), lambda i, s: (0, i, 0))],
            out_specs=pl.BlockSpec((tr, c), lambda i, s: (s[2] * nb + i, 0))),
        out_shape=jax.ShapeDtypeStruct((2 * r, c), F32),
        compiler_params=_cparams(("arbitrary",)),
    )(xyc, gfull, recv)


def _place_own(shard, slot, dtype, name):
    r, c = shard.shape
    tr = _pick(r, (256, 352, 320, 128, 32))

    def body(_, x_ref, o_ref):
        o_ref[...] = x_ref[...].astype(dtype)

    return pl.pallas_call(
        body, name=name,
        grid_spec=pltpu.PrefetchScalarGridSpec(
            num_scalar_prefetch=1, grid=(r // tr,),
            in_specs=[pl.BlockSpec((tr, c), lambda i, s: (i, 0))],
            out_specs=pl.BlockSpec((None, tr, c), lambda i, s: (s[0], i, 0))),
        out_shape=jax.ShapeDtypeStruct((4, r, c), dtype),
        compiler_params=_cparams(("arbitrary",)),
    )(slot, shard)


BIG = ("w_in", "w_proj_a", "w_proj_b", "w_out", "w_ffn_in", "w_ffn_down")
COL_SHARDED = ("w_in", "w_ffn_in")


def _coords():
    return lax.axis_index("x"), lax.axis_index("y"), lax.axis_index("c")


def _gather_host(bufs):
    n = len(bufs)

    def place():
        x, y, c = _coords()
        return x, y, c, [(1 - x, y), (x, 1 - y), (1 - x, 1 - y)]

    def slab(b, chip, half_of):
        half = b.shape[1] // 2
        return b.at[chip, pl.ds(pl.multiple_of(half_of * half, 8), half)]

    def copy(b, sems, w, k, chip, half_of, to):
        s = slab(b, chip, half_of)
        return pltpu.make_async_remote_copy(src_ref=s, dst_ref=s, send_sem=sems[0].at[w, k], recv_sem=sems[1].at[w, k],
                                            device_id=to, device_id_type=MESH)

    def start(_, outs, sems):
        x, y, c, chips = place()
        for w, b in enumerate(outs):
            for k, (cx, cy) in enumerate(chips):
                copy(b, sems, w, k, 2 * x + y, c, (cx, cy, c)).start()

    def mid(_, outs, sems):
        x, y, c, chips = place()
        for w, b in enumerate(outs):
            for k, (cx, cy) in enumerate(chips):
                copy(b, sems, w, k, 2 * cx + cy, c, (cx, cy, c)).wait_recv()
                copy(b, sems, w, 3 + k, 2 * cx + cy, c, (x, y, 1 - c)).start()

    def finish(_, outs, sems):
        x, y, c, chips = place()
        for w, b in enumerate(outs):
            for k, (cx, cy) in enumerate(chips):
                copy(b, sems, w, 3 + k, 2 * cx + cy, 1 - c, (x, y, 1 - c)).wait_recv()
        for w, b in enumerate(outs):
            for k, (cx, cy) in enumerate(chips):
                copy(b, sems, w, k, 2 * x + y, c, (cx, cy, c)).wait_send()
                copy(b, sems, w, 3 + k, 2 * cx + cy, c, (x, y, 1 - c)).wait_send()

    return _Host(ins=list(bufs), out_shapes=[jax.ShapeDtypeStruct(b.shape, b.dtype) for b in bufs],
                 aliases={i: i for i in range(n)},
                 sems=[pltpu.SemaphoreType.DMA((n, 6)), pltpu.SemaphoreType.DMA((n, 6))],
                 start=start, mid=mid, finish=finish)


def _run_host(host, name):
    n_in, n_out = len(host.ins), len(host.out_shapes)

    def body(*refs):
        ins, outs, sems = refs[:n_in], refs[n_in:n_in + n_out], refs[n_in + n_out:]
        host.start(ins, outs, sems)
        if host.mid is not None:
            host.mid(ins, outs, sems)
        host.finish(ins, outs, sems)

    return pl.pallas_call(
        body, name=name, in_specs=[ANY] * n_in, out_specs=[ANY] * n_out, out_shape=list(host.out_shapes),
        scratch_shapes=list(host.sems), input_output_aliases=dict(host.aliases),
    )(*host.ins)


def _peer(x, y, c, k):
    fx, fy, fc = (k >> 2) & 1, (k >> 1) & 1, k & 1
    return (x ^ fx if fx else x, y ^ fy if fy else y, c ^ fc if fc else c)


def _sub_shape(name, full_shape):
    r, c = full_shape
    return (r // 2, c // 4) if name in COL_SHARDED else (r // 8, c)


def _exchange_host(names, grads):
    n = len(names)
    shapes = [_sub_shape(k, g.shape) for k, g in zip(names, grads)]

    def copy(ins, outs, sems, w, k):
        x, y, c = _coords()
        px, py, pc = _peer(x, y, c, k)
        sr, sc = shapes[w]
        if names[w] in COL_SHARDED:
            col = _col_block(names[w], px, py) * sc
            src = ins[w].at[pl.ds(pl.multiple_of(pc * sr, 16), sr), pl.ds(pl.multiple_of(col, 128), sc)]
        else:
            src = ins[w].at[pl.ds(pl.multiple_of((4 * px + 2 * py + pc) * sr, 16), sr)]
        return pltpu.make_async_remote_copy(
            src_ref=src, dst_ref=outs[w].at[k - 1], send_sem=sems[0].at[w, k - 1], recv_sem=sems[1].at[w, k - 1],
            device_id=(px, py, pc), device_id_type=MESH)

    def start(ins, outs, sems):
        for w in range(n):
            for k in range(1, 8):
                copy(ins, outs, sems, w, k).start()

    def finish(ins, outs, sems):
        for w in range(n):
            for k in range(1, 8):
                copy(ins, outs, sems, w, k).wait_recv()
        for w in range(n):
            for k in range(1, 8):
                copy(ins, outs, sems, w, k).wait_send()

    return _Host(ins=list(grads), out_shapes=[jax.ShapeDtypeStruct((7,) + s, g.dtype) for s, g in zip(shapes, grads)],
                 aliases={}, sems=[pltpu.SemaphoreType.DMA((n, 7)), pltpu.SemaphoreType.DMA((n, 7))],
                 start=start, mid=None, finish=finish)


def _sibling_host(bufs):
    n = len(bufs)

    def copy(outs, sems, w, half_of):
        x, y, c = _coords()
        half = outs[w].shape[0] // 2
        rows = outs[w].at[pl.ds(pl.multiple_of(half_of * half, 8), half)]
        return pltpu.make_async_remote_copy(src_ref=rows, dst_ref=rows, send_sem=sems[0].at[w], recv_sem=sems[1].at[w],
                                            device_id=(x, y, 1 - c), device_id_type=MESH)

    def start(_, outs, sems):
        c = lax.axis_index("c")
        for w in range(n):
            copy(outs, sems, w, c).start()

    def finish(_, outs, sems):
        c = lax.axis_index("c")
        for w in range(n):
            copy(outs, sems, w, 1 - c).wait_recv()
        for w in range(n):
            copy(outs, sems, w, c).wait_send()

    return _Host(ins=list(bufs), out_shapes=[jax.ShapeDtypeStruct(b.shape, b.dtype) for b in bufs],
                 aliases={i: i for i in range(n)},
                 sems=[pltpu.SemaphoreType.DMA((n,)), pltpu.SemaphoreType.DMA((n,))], start=start, mid=None, finish=finish)


def _pack_host(pack):
    def me_of():
        x, y, c = _coords()
        return x, y, c, 4 * x + 2 * y + c

    def copy(ins, outs, sems, k, slot):
        x, y, c, _ = me_of()
        return pltpu.make_async_remote_copy(src_ref=ins[0], dst_ref=outs[0].at[slot], send_sem=sems[0].at[k - 1],
                                            recv_sem=sems[1].at[k - 1], device_id=_peer(x, y, c, k), device_id_type=MESH)

    def start(ins, outs, sems):
        me = me_of()[3]
        pltpu.make_async_copy(ins[0], outs[0].at[me], sems[2]).start()
        for k in range(1, 8):
            copy(ins, outs, sems, k, me).start()

    def finish(ins, outs, sems):
        x, y, c, me = me_of()
        for k in range(1, 8):
            px, py, pc = _peer(x, y, c, k)
            copy(ins, outs, sems, k, 4 * px + 2 * py + pc).wait_recv()
        for k in range(1, 8):
            copy(ins, outs, sems, k, me).wait_send()
        pltpu.make_async_copy(ins[0], outs[0].at[me], sems[2]).wait()

    return _Host(ins=[pack], out_shapes=[jax.ShapeDtypeStruct((8,) + pack.shape, F32)], aliases={},
                 sems=[pltpu.SemaphoreType.DMA((7,)), pltpu.SemaphoreType.DMA((7,)), pltpu.SemaphoreType.DMA],
                 start=start, mid=None, finish=finish)


def _join_hosts(hosts):
    ins, outs, sems, aliases, spans = [], [], [], {}, []
    for h in hosts:
        spans.append((len(ins), len(h.ins), len(outs), len(h.out_shapes), len(sems), len(h.sems)))
        for i_in, i_out in h.aliases.items():
            aliases[len(ins) + i_in] = len(outs) + i_out
        ins, outs, sems = ins + list(h.ins), outs + list(h.out_shapes), sems + list(h.sems)

    def phase(which):
        fns = [getattr(h, which) for h in hosts]
        if all(f is None for f in fns):
            return None

        def run(i_refs, o_refs, s_refs):
            for f, (i0, n_i, o0, n_o, s0, n_s) in zip(fns, spans):
                if f is not None:
                    f(i_refs[i0:i0 + n_i], o_refs[o0:o0 + n_o], s_refs[s0:s0 + n_s])

        return run

    return _Host(ins, outs, aliases, sems, phase("start"), phase("mid"), phase("finish"))


def _allgather_pack(pack):
    rows = pack.shape[0]

    def body(p_ref, o_ref, send, recv, local):
        x, y, c = _coords()
        me = 4 * x + 2 * y + c
        mine = pltpu.make_async_copy(p_ref, o_ref.at[me], local)
        mine.start()

        def copy(k, slot):
            return pltpu.make_async_remote_copy(src_ref=p_ref, dst_ref=o_ref.at[slot], send_sem=send.at[k - 1],
                                                recv_sem=recv.at[k - 1], device_id=_peer(x, y, c, k), device_id_type=MESH)

        for k in range(1, 8):
            copy(k, me).start()
        for k in range(1, 8):
            px, py, pc = _peer(x, y, c, k)
            copy(k, 4 * px + 2 * py + pc).wait_recv()
        for k in range(1, 8):
            copy(k, me).wait_send()
        mine.wait()

    return pl.pallas_call(
        body, name="allgather_pack", in_specs=[pl.BlockSpec(memory_space=pltpu.VMEM)], out_specs=ANY,
        out_shape=jax.ShapeDtypeStruct((8,) + pack.shape, F32),
        scratch_shapes=[pltpu.SemaphoreType.DMA((7,)), pltpu.SemaphoreType.DMA((7,)), pltpu.SemaphoreType.DMA],
        compiler_params=pltpu.CompilerParams(vmem_limit_bytes=VMEM_LIMIT),
    )(pack)


SMALL = ("norm1_g", "conv_b", "rg_wa", "rg_ba", "rg_wx", "rg_bx", "rg_lambda", "hg_lb_logits", "hg_norm_g",
         "norm2_g", "norm_f_g")
SHARDED_SMALL = ("meta_tokens", "conv_w")

ROWS = {"conv_b": (1, RGW), "rg_ba": (1, RGW), "rg_bx": (1, RGW), "rg_lambda": (1, RGW), "conv_w": (4, RGW),
        "norm2_g": (1, D), "norm_f_g": (1, D), "hg_lb_logits": (2, D),
        "rg_wa": (RGW * RG_BLOCK // 128, 128), "rg_wx": (RGW * RG_BLOCK // 128, 128), "hg_norm_g": (1, 128),
        "norm1_g": (1, D), "meta_tokens": (N_META, D), "loss": (1, D)}
PACKS = {"early_wide": ("conv_b", "rg_ba", "rg_bx", "rg_lambda", "conv_w"),
         "early_model": ("norm2_g", "norm_f_g", "hg_lb_logits"),
         "early_lane": ("rg_wa", "rg_wx", "hg_norm_g"),
         "late": ("norm1_g", "meta_tokens", "loss")}
EARLY = ("early_wide", "early_model", "early_lane")
NO_UPDATE = ("conv_w", "meta_tokens", "loss")


def _pack_rows(vals, names):
    parts, first, row = [], {}, 0
    for k in names:
        first[k] = row
        parts.append(vals[k].reshape(ROWS[k]).astype(F32))
        row += ROWS[k][0]
    if row % 8:
        parts.append(jnp.zeros((-row % 8, ROWS[names[0]][1]), F32))
    return jnp.concatenate(parts, axis=0), first


ORDER = ("meta_tokens", "norm1_g", "w_in", "conv_w", "conv_b", "rg_wa", "rg_ba", "rg_wx", "rg_bx", "rg_lambda",
         "hg_lb_logits", "hg_norm_g", "w_proj_a", "w_proj_b", "w_out", "norm2_g", "w_ffn_in", "w_ffn_down", "norm_f_g")


def kernel(x, meta_tokens, norm1_g, w_in, conv_w, conv_b, rg_wa, rg_ba, rg_wx, rg_bx, rg_lambda, hg_lb_logits, hg_norm_g, w_proj_a, w_proj_b, w_out, norm2_g, w_ffn_in, w_ffn_down, norm_f_g, loss_target, m_meta_tokens, m_norm1_g, m_w_in, m_conv_w, m_conv_b, m_rg_wa, m_rg_ba, m_rg_wx, m_rg_bx, m_rg_lambda, m_hg_lb_logits, m_hg_norm_g, m_w_proj_a, m_w_proj_b, m_w_out, m_norm2_g, m_w_ffn_in, m_w_ffn_down, m_norm_f_g, v_meta_tokens, v_norm1_g, v_w_in, v_conv_w, v_conv_b, v_rg_wa, v_rg_ba, v_rg_wx, v_rg_bx, v_rg_lambda, v_hg_lb_logits, v_hg_norm_g, v_w_proj_a, v_w_proj_b, v_w_out, v_norm2_g, v_w_ffn_in, v_w_ffn_down, v_norm_f_g):
    args = dict(locals())
    w = {k: args[k] for k in ORDER}
    m = {k: args["m_" + k] for k in ORDER}
    v = {k: args["v_" + k] for k in ORDER}
    xi, yi, ci = _coords()
    chip = 2 * xi + yi
    slot = jnp.reshape(chip, (1,)).astype(jnp.int32)
    xyc = jnp.stack([xi, yi, ci]).astype(jnp.int32)

    def full_matrix(k, gth):
        if k == "w_ffn_in":
            return jnp.concatenate([gth[0], gth[2], gth[1], gth[3]], axis=1)
        if k in COL_SHARDED:
            return jnp.concatenate([gth[j] for j in range(4)], axis=1)
        return gth.reshape(-1, gth.shape[2])

    tiny = jnp.zeros((32, 384), F32)
    tiny = tiny.at[0:N_META, 0:256].set(meta_tokens).at[N_META:N_META + 4, 0:320].set(conv_w[0])
    slots = {k: _place_own(w[k][0], slot, BF16, "place_" + k) for k in BIG}
    w_in_all, tiny_all = _run_host(_gather_host([slots["w_in"], _place_own(tiny, slot, F32, "place_tiny")]),
                                   "allgather_w_in")
    later = [k for k in BIG if k != "w_in"]
    meta_full = jnp.transpose(tiny_all[:, 0:N_META, 0:256], (1, 0, 2)).reshape(N_META, D)
    conv_w_full = jnp.transpose(tiny_all[:, N_META:N_META + 4, 0:320], (1, 0, 2)).reshape(4, RGW)

    small = dict(norm1_g=norm1_g, conv_w=conv_w_full, conv_b=conv_b, rg_wa=rg_wa[0], rg_ba=rg_ba, rg_wx=rg_wx[0],
                 rg_bx=rg_bx, rg_lambda=rg_lambda, hg_lb_logits=hg_lb_logits, hg_norm_g=hg_norm_g,
                 norm2_g=norm2_g, norm_f_g=norm_f_g.reshape(1, D))

    hosts = {
        "mm_proj": lambda g: _gather_host([slots[k] for k in later]),
        "mm_d_act": lambda g: _exchange_host(["w_ffn_down"], [g["w_ffn_down"]]),
        "mm_d_hn2": lambda g: _exchange_host(["w_ffn_in"], [g["w_ffn_in"]]),
        "mm_d_win": lambda g: _join_hosts([
            _exchange_host(["w_out", "w_proj_a", "w_proj_b"], [g["w_out"], g["w_proj_a"], g["w_proj_b"]]),
            *[_pack_host(_pack_rows(g, PACKS[p])[0]) for p in EARLY]]),
        "mm_d_hn1": lambda g: _exchange_host(["w_in"], [_unrotate_g_in(g["w_in"])]),
    }
    h0 = jnp.concatenate([jnp.zeros((PAD, D), F32), meta_full, x[0]], axis=0)
    loss_blk, grad_x, g, carried = _local_step(
        h0, loss_target[0], {"w_in": _rotate_w_in(full_matrix("w_in", w_in_all))}, small, hosts=hosts,
        finalize=lambda gathered: {k: full_matrix(k, gth) for k, gth in zip(later, gathered)})
    g["loss"] = jnp.broadcast_to(loss_blk[0:1, 0:1], (1, D))

    halves = {}
    sources = {"mm_d_act": ["w_ffn_down"], "mm_d_hn2": ["w_ffn_in"], "mm_d_win": ["w_out", "w_proj_a", "w_proj_b"],
               "mm_d_hn1": ["w_in"]}
    for name, keys in sources.items():
        partials, received = carried[name]
        for k, part, rec in zip(keys, partials, received):
            halves[k] = _sum_place(rec, part, k, xyc, "sum_" + k)
    g_big = dict(zip(BIG, _run_host(_sibling_host([halves[k] for k in BIG]), "sibling_swap")))
    late, _ = _pack_rows(g, PACKS["late"])
    recv_packs = dict(zip(EARLY, carried["mm_d_win"][1][3:]), late=_allgather_pack(late))

    grad, delta, new_m, new_v, summed = {}, {}, {}, {}, {}
    for pack, names in PACKS.items():
        row, params, extras = 0, [], []
        for k in names:
            if k in NO_UPDATE:
                extras.append((row, ROWS[k][0]))
            else:
                params.append((row,) + tuple(d[k].reshape(ROWS[k]) for d in (w, m, v)))
            row += ROWS[k][0]
        updated, rows_only = _adamw_rows(recv_packs[pack], params, extras, "adamw_" + pack)
        for k, res in zip([k for k in names if k not in NO_UPDATE], updated):
            grad[k], delta[k], new_m[k], new_v[k] = (a.reshape(w[k].shape) for a in res)
        summed.update(zip([k for k in names if k in NO_UPDATE], rows_only))
    loss = summed["loss"][0, 0]
    g_shard = {"meta_tokens": lax.dynamic_slice(summed["meta_tokens"], (0, chip * 256), (N_META, 256)),
               "conv_w": lax.dynamic_slice(summed["conv_w"], (0, chip * 320), (4, 320))}
    for k in BIG + SHARDED_SMALL:
        gk = g_big[k] if k in BIG else g_shard[k]
        wk, mk, vk = (d[k].reshape(gk.shape) for d in (w, m, v))
        dk, mk, vk = _adamw(wk, gk, mk, vk, "adamw_" + k)
        grad[k], delta[k], new_m[k], new_v[k] = (a.reshape(w[k].shape) for a in (gk, dk, mk, vk))

    return (loss, grad_x[None], *[grad[k] for k in ORDER], *[delta[k] for k in ORDER],
            *[new_m[k] for k in ORDER], *[new_v[k] for k in ORDER])
```

```python
import functools

import jax
import jax.numpy as jnp
from jax import lax
from jax.experimental import pallas as pl
from jax.experimental.pallas import tpu as pltpu

F32, BF16 = jnp.float32, jnp.bfloat16
D = 1024
N_META = 16
RGW = 1280
RG_BLOCKS, RG_BLOCK = 16, 80
RG_C = 8.0
HEADS, HD = 8, 128
HGW = HEADS * HD
DFF = 2816
D_IN = 2 * RGW + 6 * D
ROT = 2 * RGW
COL_PAD = 256
D_INP = D_IN + COL_PAD
EPS = 1e-6
CH = 64
TM = 256
ROW0 = TM
PAD = ROW0 - N_META
CT = RGW
RS = 16
NCT = RGW // CT
EXP_CLAMP = 80.0
VMEM_LIMIT = 56 * 1024 * 1024

LR, B1, B2, ADAM_EPS, WD, STEP = 0.001, 0.9, 0.999, 1e-08, 0.01, 10
MESH = pl.DeviceIdType.MESH
ANY = pl.BlockSpec(memory_space=pl.ANY)


def _cparams(sem):
    return pltpu.CompilerParams(dimension_semantics=sem, vmem_limit_bytes=VMEM_LIMIT)


def _pick(n, prefs):
    for p in prefs:
        if n % p == 0:
            return p
    return n


def _sig(x):
    return 0.5 * jnp.tanh(0.5 * x) + 0.5


def _dot(a, b, dims, precision=None):
    return lax.dot_general(a, b, (dims, ((), ())), preferred_element_type=F32, precision=precision)


NN, NT, TN_ = ((1,), (0,)), ((1,), (1,)), ((0,), (0,))


class _Host:
    def __init__(self, ins, out_shapes, aliases, sems, start, mid, finish):
        self.ins, self.out_shapes, self.aliases, self.sems = ins, out_shapes, aliases, sems
        self.start, self.mid, self.finish = start, mid, finish


class _Epi:
    def __init__(self, ins, in_specs, out_shapes, out_specs, fn, sequential=False, split=2):
        self.ins, self.in_specs, self.out_shapes, self.out_specs, self.fn = ins, in_specs, out_shapes, out_specs, fn
        self.sequential = sequential
        self.split = split


def _mm(a, b, mode, out_dtype, name, resid=None, host=None, epi=None, tiles=None, sparse=None, norm_gain=None):
    if mode == "nn":
        (m, kd), n = a.shape, b.shape[1]
    elif mode == "nt":
        (m, kd), n = a.shape, b.shape[0]
    else:
        (kd, m), n = a.shape, b.shape[1]
    if mode == "tn":
        tm = _pick(m, (1024, 1280, 1408, 640, 512, 256, 128))
        tk = _pick(kd, (1408, 768, 512, 256, 128))
    else:
        tm = _pick(m, (768, 512, 640, 256, 128))
        tk = kd if kd <= 2816 else _pick(kd, (1792, 1408, 1024, 512, 256, 128))
    tn = _pick(n, (1792, 1408, 1280, 1024, 512, 256, 128))
    if tiles is not None:
        tm, tn = tiles
    a_map = (lambda i, j, k: (k, i)) if mode == "tn" else (lambda i, j, k: (i, k))
    b_map = (lambda i, j, k: (j, k)) if mode == "nt" else (lambda i, j, k: (k, j))
    o_map = lambda i, j, k: (i, j)
    if sparse is not None:
        (ni, nj, nk), (tm, tn, tk), (a_map, b_map, o_map) = sparse
    else:
        assert m % tm == 0 and n % tn == 0 and kd % tk == 0, (name, m, n, kd, tm, tn, tk)
        ni, nj, nk = m // tm, n // tn, kd // tk
    dims = {"nn": NN, "nt": NT, "tn": TN_}[mode]

    n_hin = len(host.ins) if host else 0
    n_hout = len(host.out_shapes) if host else 0
    n_res = 0 if resid is None else 1
    n_pro = 0 if norm_gain is None else 1
    n_ein = len(epi.ins) if epi else 0
    n_out = len(epi.out_shapes) if epi else 1
    assert not n_pro or (mode == "nn" and nk == 1)

    def finish(r, r_ref, e_in, o_refs, i, rows=slice(None), first=True):
        if resid is not None:
            r = r + r_ref[rows, :]
        if epi:
            epi.fn(r, e_in, o_refs, i, rows, first)
        else:
            o_refs[0][rows, :] = r.astype(out_dtype)

    def body(*refs):
        a_ref, b_ref = refs[:2]
        r_ref = refs[2] if resid is not None else None
        pos = 2 + n_res
        g_ref = refs[pos] if n_pro else None
        pos += n_pro
        e_in = refs[pos:pos + n_ein]
        pos += n_ein
        h_in = refs[pos:pos + n_hin]
        pos += n_hin
        o_ref = refs[pos:pos + n_out]
        pos += n_out
        hn_ref = refs[pos] if n_pro else None
        pos += n_pro
        h_out = refs[pos:pos + n_hout]
        scratch = refs[pos + n_hout:]
        n_acc = 1 if nk > 1 else 0
        h_sems = scratch[n_acc + n_pro:]
        i, j, k = pl.program_id(0), pl.program_id(1), pl.program_id(2)
        if host:
            @pl.when((i == 0) & (j == 0) & (k == 0))
            def _():
                host.start(h_in, h_out, h_sems)

            if host.mid is not None:
                @pl.when((i == (3 * ni) // 4) & (j == 0) & (k == 0))
                def _():
                    host.mid(h_in, h_out, h_sems)

        if n_pro:
            a_s = scratch[n_acc]

            @pl.when(j == 0)
            def _():
                xh, _ = _rms(a_ref[...])
                a_s[...] = (xh * g_ref[...]).astype(BF16)
                hn_ref[...] = a_s[...]

            a_val = a_s[...]
        else:
            a_val = a_ref[...]

        if nk == 1 and epi and epi.split > 1 and mode != "tn":
            part = tm // epi.split
            for p in range(epi.split):
                rows = slice(p * part, (p + 1) * part)
                finish(_dot(a_val[rows, :], b_ref[...], dims), r_ref, e_in, o_ref, i, rows, p == 0)
        elif nk == 1:
            finish(_dot(a_val, b_ref[...], dims), r_ref, e_in, o_ref, i)
        else:
            acc = scratch[0]

            @pl.when(k == 0)
            def _():
                acc[...] = jnp.zeros_like(acc)

            acc[...] += _dot(a_val, b_ref[...], dims)

            @pl.when(k == nk - 1)
            def _():
                finish(acc[...], r_ref, e_in, o_ref, i)

        if host:
            @pl.when((i == ni - 1) & (j == nj - 1) & (k == nk - 1))
            def _():
                host.finish(h_in, h_out, h_sems)

    a_spec = pl.BlockSpec((tk, tm) if mode == "tn" else (tm, tk), a_map)
    b_spec = pl.BlockSpec((tn, tk) if mode == "nt" else (tk, tn), b_map)
    o_spec = pl.BlockSpec((tm, tn), o_map)
    in_specs, args = [a_spec, b_spec], [a, b]
    if resid is not None:
        in_specs.append(o_spec)
        args.append(resid)
    if n_pro:
        in_specs.append(pl.BlockSpec((1, kd), lambda i, j, k: (0, 0)))
        args.append(norm_gain)
    out_shapes, out_specs = [jax.ShapeDtypeStruct((m, n), out_dtype)], [o_spec]
    if epi:
        in_specs += list(epi.in_specs)
        args += list(epi.ins)
        out_shapes, out_specs = list(epi.out_shapes), list(epi.out_specs)
    if n_pro:
        out_shapes.append(jax.ShapeDtypeStruct((m, kd), BF16))
        out_specs.append(pl.BlockSpec((tm, kd), lambda i, j, k: (i, 0)))
    scratch = ([pltpu.VMEM((tm, tn), F32)] if nk > 1 else []) + ([pltpu.VMEM((tm, kd), BF16)] if n_pro else [])
    sequential = host or n_pro or (epi and epi.sequential)
    outs = pl.pallas_call(
        body, name=name, grid=(ni, nj, nk),
        in_specs=in_specs + [ANY] * n_hin, out_specs=out_specs + [ANY] * n_hout,
        out_shape=out_shapes + (list(host.out_shapes) if host else []),
        scratch_shapes=scratch + (list(host.sems) if host else []),
        input_output_aliases=({2 + n_res + n_pro + n_ein + i_in: n_out + n_pro + i_out
                               for i_in, i_out in host.aliases.items()} if host else {}),
        compiler_params=_cparams(("arbitrary",) * 3 if sequential else ("parallel", "parallel", "arbitrary")),
    )(*args, *(host.ins if host else []))
    main = list(outs[:n_out]) if epi else outs[0]
    if n_pro:
        main = (main, outs[n_out])
    return (main, list(outs[n_out + n_pro:])) if host else main


def _rms(x):
    r = lax.rsqrt(jnp.mean(x * x, axis=-1, keepdims=True) + EPS)
    return x * r, r


def _rms_bwd_math(dhn, x, g):
    xh, r = _rms(x)
    dxh = dhn * g
    dx = r * (dxh - xh * jnp.mean(dxh * xh, axis=-1, keepdims=True))
    return dx, jnp.sum(dhn * xh, axis=0, keepdims=True)


def _row_specs(tm):
    return pl.BlockSpec((tm, D), lambda i, j, k: (i, 0)), pl.BlockSpec((1, D), lambda i, j, k: (0, 0))


def _final_loss_epi(gf, target, t):
    def fn(x, ins, outs, i, rows, first):
        g_ref, t_ref = ins
        dh_ref, dhb_ref, loss_ref, dg_ref = outs

        if first:
            @pl.when(i == 0)
            def _():
                loss_ref[...] = jnp.zeros_like(loss_ref)
                dg_ref[...] = jnp.zeros_like(dg_ref)

        g = g_ref[...]
        xh, _ = _rms(x)
        err = jnp.where(i > 0, xh * g - t_ref[rows, :], 0.0)
        loss_ref[...] += 0.5 * jnp.sum(jnp.sum(err * err, axis=-1, keepdims=True) * (1.0 / D))
        dx, dg = _rms_bwd_math(err * (1.0 / D), x, g)
        dh_ref[rows, :] = dx
        dhb_ref[rows, :] = dx.astype(BF16)
        dg_ref[...] += dg

    row, vec = _row_specs(TM)
    return _Epi(ins=[gf, target], in_specs=[vec, pl.BlockSpec((TM, D), lambda i, j, k: (jnp.maximum(i - 1, 0), 0))],
                out_shapes=[jax.ShapeDtypeStruct((t, D), F32), jax.ShapeDtypeStruct((t, D), BF16),
                            jax.ShapeDtypeStruct((8, 128), F32), jax.ShapeDtypeStruct((1, D), F32)],
                out_specs=[row, row, pl.BlockSpec((8, 128), lambda i, j, k: (0, 0)), vec], fn=fn, sequential=True)


def _rms_bwd_epi(h, g, dres, tm):
    t = h.shape[0]

    def fn(dhn, ins, outs, i, rows, first):
        h_ref, g_ref, dres_ref = ins
        dh_ref, dhb_ref, dg_ref = outs

        if first:
            @pl.when(i == 0)
            def _():
                dg_ref[...] = jnp.zeros_like(dg_ref)

        dx, dg = _rms_bwd_math(dhn, h_ref[rows, :], g_ref[...])
        dh = dres_ref[rows, :] + dx
        dh_ref[rows, :] = dh
        dhb_ref[rows, :] = dh.astype(BF16)
        dg_ref[...] += dg

    row, vec = _row_specs(tm)
    return _Epi(ins=[h, g, dres], in_specs=[row, vec, row],
                out_shapes=[jax.ShapeDtypeStruct((t, D), F32), jax.ShapeDtypeStruct((t, D), BF16),
                            jax.ShapeDtypeStruct((1, D), F32)],
                out_specs=[row, row, vec], fn=fn, sequential=True)


def _rmsnorm1_bwd(dhn, h, g, dres):
    t = h.shape[0]

    def body(dhn_ref, h_ref, g_ref, dres_ref, dx_ref, dmeta_ref, dg_ref):
        i = pl.program_id(0)

        @pl.when(i == 0)
        def _():
            dg_ref[...] = jnp.zeros_like(dg_ref)

        dx, dg = _rms_bwd_math(dhn_ref[...].astype(F32), h_ref[...], g_ref[...])
        dh = dres_ref[...] + dx
        dg_ref[...] += dg
        dx_ref[...] = dh

        @pl.when(i == 0)
        def _():
            dmeta_ref[...] = dh[PAD:, :]

    row = pl.BlockSpec((TM, D), lambda i: (i, 0))
    vec = pl.BlockSpec((1, D), lambda i: (0, 0))
    return pl.pallas_call(
        body, name="rmsnorm1_bwd", grid=(t // TM,),
        in_specs=[row, row, vec, row],
        out_specs=[pl.BlockSpec((TM, D), lambda i: (jnp.maximum(i - 1, 0), 0)),
                   pl.BlockSpec((N_META, D), lambda i: (0, 0)), vec],
        out_shape=[jax.ShapeDtypeStruct((t - ROW0, D), F32), jax.ShapeDtypeStruct((N_META, D), F32),
                   jax.ShapeDtypeStruct((1, D), F32)],
        compiler_params=_cparams(("arbitrary",)),
    )(dhn, h, g, dres)


FH = DFF // 2
FM = 384


def _swiglu_epi(t):
    def fn(r, _, outs, i, rows, first):
        g, u = r[:, :FH], r[:, FH:]
        outs[0][rows, :] = r.astype(BF16)
        outs[1][rows, :] = (g * _sig(g) * u).astype(BF16)

    return _Epi(ins=[], in_specs=[],
                out_shapes=[jax.ShapeDtypeStruct((t, 2 * DFF), BF16), jax.ShapeDtypeStruct((t, DFF), BF16)],
                out_specs=[pl.BlockSpec((FM, 2 * FH), lambda i, j, k: (i, j)), pl.BlockSpec((FM, FH), lambda i, j, k: (i, j))],
                fn=fn)


def _swiglu_bwd_epi(gu):
    def fn(d, ins, outs, i, rows, first):
        gu_t = ins[0][rows, :].astype(F32)
        g, u = gu_t[:, :FH], gu_t[:, FH:]
        s = _sig(g)
        outs[0][rows, :FH] = (d * u * (s * (1.0 + g * (1.0 - s)))).astype(BF16)
        outs[0][rows, FH:] = (d * (g * s)).astype(BF16)

    spec = pl.BlockSpec((FM, 2 * FH), lambda i, j, k: (i, j))
    return _Epi(ins=[gu], in_specs=[spec], out_shapes=[jax.ShapeDtypeStruct(gu.shape, BF16)], out_specs=[spec], fn=fn)


COL_GA, COL_GB = 4, 5
COL_AX, COL_AG = (6 * D + COL_PAD) // CT, (6 * D + COL_PAD + RGW) // CT


def _merge_fwd(pa, pb, proj):
    t = pa.shape[0]

    def body(pa_ref, pb_ref, ga_ref, gb_ref, o_ref):
        ga, gb, pa, pb = (r[...].astype(F32) for r in (ga_ref, gb_ref, pa_ref, pb_ref))
        o_ref[...] = (_sig(ga) * pa + _sig(gb) * pb).astype(BF16)

    row = pl.BlockSpec((TM, D), lambda i: (i, 0))
    return pl.pallas_call(
        body, name="merge_fwd", grid=(t // TM,),
        in_specs=[row, row, pl.BlockSpec((TM, D), lambda i: (i, COL_GA)), pl.BlockSpec((TM, D), lambda i: (i, COL_GB))],
        out_specs=row, out_shape=jax.ShapeDtypeStruct((t, D), BF16),
        compiler_params=_cparams(("parallel",)),
    )(pa, pb, proj, proj)


def _merge_bwd_epi(pa, pb, proj, tm):
    t = pa.shape[0]

    def fn(d, ins, outs, i, rows, first):
        pa, pb, ga, gb = (r[rows, :].astype(F32) for r in ins)
        dpa_ref, dpb_ref, dproj_ref = outs
        sa, sb = _sig(ga), _sig(gb)
        dpa_ref[rows, :] = (sa * d).astype(BF16)
        dpb_ref[rows, :] = (sb * d).astype(BF16)
        dproj_ref[rows, :D] = (d * pa * sa * (1.0 - sa)).astype(BF16)
        dproj_ref[rows, D:] = (d * pb * sb * (1.0 - sb)).astype(BF16)

    row, _ = _row_specs(tm)
    return _Epi(ins=[pa, pb, proj, proj],
                in_specs=[row, row, pl.BlockSpec((tm, D), lambda i, j, k: (i, COL_GA)),
                          pl.BlockSpec((tm, D), lambda i, j, k: (i, COL_GB))],
                out_shapes=[jax.ShapeDtypeStruct((t, D), BF16), jax.ShapeDtypeStruct((t, D), BF16),
                            jax.ShapeDtypeStruct((t, D_INP), BF16)],
                out_specs=[row, row, pl.BlockSpec((tm, 2 * D), lambda i, j, k: (i, 2))], fn=fn)


def _zero_pad_cols(dproj):
    t = dproj.shape[0]
    tz = _pick(t, (768, 256))

    def body(_, o_ref):
        o_ref[...] = jnp.zeros_like(o_ref)

    return pl.pallas_call(
        body, name="dproj_pad", grid=(t // tz,), in_specs=[ANY],
        out_specs=pl.BlockSpec((tz, COL_PAD), lambda i: (i, (6 * D) // COL_PAD)),
        out_shape=jax.ShapeDtypeStruct(dproj.shape, BF16), input_output_aliases={0: 0},
        compiler_params=_cparams(("parallel",)),
    )(dproj)


HALO = 16


def _rows_before(cur, before, s):
    n = cur.shape[0]
    return jnp.concatenate([before, cur], axis=0)[n - s:2 * n - s, :]


def _rows_after(cur, after, s):
    n = cur.shape[0]
    return jnp.concatenate([cur, after], axis=0)[s:n + s, :]


def _prev_halo(col0):
    return pl.BlockSpec((HALO, CT), lambda c, i: (jnp.maximum(i * (TM // HALO) - 1, 0), col0 + c))


def _conv_fwd(proj, w, b):
    t = proj.shape[0]

    def body(x_ref, halo_ref, w_ref, b_ref, o_ref, ob_ref):
        i = pl.program_id(1)
        wv, bv = w_ref[...], b_ref[...]
        first = jnp.where(i > 0, halo_ref[...].astype(F32), 0.0)

        def strip(k, before):
            rows = pl.ds(pl.multiple_of(k * HALO, HALO), HALO)
            x = x_ref[rows, :].astype(F32)
            y = bv + wv[3:4, :] * x
            for j in range(3):
                y = y + wv[j:j + 1, :] * _rows_before(x, before, 3 - j)
            o_ref[rows, :] = y
            ob_ref[rows, :] = y.astype(BF16)
            return x

        lax.fori_loop(0, TM // HALO, strip, first)

    blk = pl.BlockSpec((TM, CT), lambda c, i: (i, c))
    return pl.pallas_call(
        body, name="conv_fwd", grid=(NCT, t // TM),
        in_specs=[pl.BlockSpec((TM, CT), lambda c, i: (i, COL_AX + c)), _prev_halo(COL_AX),
                  pl.BlockSpec((4, CT), lambda c, i: (0, c)), pl.BlockSpec((1, CT), lambda c, i: (0, c))],
        out_specs=[blk, blk],
        out_shape=[jax.ShapeDtypeStruct((t, RGW), F32), jax.ShapeDtypeStruct((t, RGW), BF16)],
        compiler_params=_cparams(("parallel", "parallel")),
    )(proj, proj, w, b)


def _conv_bwd(dxc, proj, w, dproj):
    t = proj.shape[0]
    nt = t // TM

    def body(d_ref, dn_ref, x_ref, halo_ref, w_ref, _, dx_ref, dw_ref, db_ref, acc):
        i = pl.program_id(1)

        @pl.when(i == 0)
        def _():
            dw_ref[...] = jnp.zeros_like(dw_ref)
            db_ref[...] = jnp.zeros_like(db_ref)

        acc[...] = jnp.zeros_like(acc)
        wv = w_ref[...]
        first = jnp.where(i > 0, halo_ref[...].astype(F32), 0.0)
        last = jnp.where(i < nt - 1, dn_ref[...], 0.0)
        ns = TM // HALO

        def fold(v):
            return v[0:8, :] + v[8:16, :]

        def strip(k, before):
            off = pl.multiple_of(k * HALO, HALO)
            rows = pl.ds(off, HALO)
            x, d = x_ref[rows, :].astype(F32), d_ref[rows, :]
            nxt = d_ref[pl.ds(pl.multiple_of(jnp.minimum(off + HALO, TM - HALO), HALO), HALO), :]
            after = jnp.where(k == ns - 1, last, nxt)
            dx = wv[3:4, :] * d
            for j in range(3):
                dx = dx + wv[j:j + 1, :] * _rows_after(d, after, 3 - j)
                acc[j] += fold(d * _rows_before(x, before, 3 - j))
            acc[3] += fold(d * x)
            acc[4] += fold(d)
            dx_ref[rows, :] = dx.astype(BF16)
            return x

        lax.fori_loop(0, ns, strip, first)
        for j in range(4):
            dw_ref[j:j + 1, :] += jnp.sum(acc[j], axis=0, keepdims=True)
        db_ref[...] += jnp.sum(acc[4], axis=0, keepdims=True)

    return pl.pallas_call(
        body, name="conv_bwd", grid=(NCT, nt),
        in_specs=[pl.BlockSpec((TM, CT), lambda c, i: (i, c)),
                  pl.BlockSpec((HALO, CT), lambda c, i: (jnp.minimum((i + 1) * (TM // HALO), t // HALO - 1), c)),
                  pl.BlockSpec((TM, CT), lambda c, i: (i, COL_AX + c)), _prev_halo(COL_AX),
                  pl.BlockSpec((4, CT), lambda c, i: (0, c)), ANY],
        out_specs=[pl.BlockSpec((TM, CT), lambda c, i: (i, COL_AX + c)),
                   pl.BlockSpec((4, CT), lambda c, i: (0, c)), pl.BlockSpec((1, CT), lambda c, i: (0, c))],
        out_shape=[jax.ShapeDtypeStruct((t, D_INP), BF16), jax.ShapeDtypeStruct((4, RGW), F32),
                   jax.ShapeDtypeStruct((1, RGW), F32)],
        scratch_shapes=[pltpu.VMEM((5, 8, CT), F32)],
        input_output_aliases={5: 0},
        compiler_params=_cparams(("parallel", "arbitrary")),
    )(dxc, dxc, proj, proj, w, dproj)


def _gelu(x):
    c = 0.7978845608028654
    th = jnp.tanh(c * (x + 0.044715 * x * x * x))
    return 0.5 * x * (1.0 + th), th


def _rg_gates(gr, gi, xc, ba, bx, lam, row0):
    r = _sig(gr + ba)
    ig = _sig(gi + bx)
    sp = jax.nn.softplus(-lam)
    a = jnp.exp(-RG_C * r * sp)
    s2 = jnp.maximum(1.0 - a * a, 1e-30)
    rs = lax.rsqrt(s2)
    s = s2 * rs
    rows = row0 + lax.broadcasted_iota(jnp.int32, gr.shape, 0)
    live = rows >= PAD
    u = jnp.where(live, s * ig * xc, 0.0)
    return r, ig, sp, a, s, rs, u, live


def _rg_fwd(gri, xc, proj, ba, bx, lam):
    t = gri.shape[0]

    def body(gr_ref, gi_ref, xc_ref, ag_ref, ba_ref, bx_ref, lam_ref, h_ref, ya_ref, hc):
        i = pl.program_id(1)

        @pl.when(i == 0)
        def _():
            hc[...] = jnp.zeros_like(hc)

        ba, bx, lam = ba_ref[...], bx_ref[...], lam_ref[...]
        sub = lax.broadcasted_iota(jnp.int32, (8, CT), 0)

        def strip(k, h):
            off = pl.multiple_of(k * RS, RS)
            rows = pl.ds(off, RS)
            _, _, _, a, _, _, u, _ = _rg_gates(gr_ref[rows, :], gi_ref[rows, :], xc_ref[rows, :], ba, bx, lam,
                                               i * TM + off)
            outs = []
            for half in range(RS // 8):
                out = jnp.zeros((8, CT), F32)
                for r in range(8):
                    h = a[8 * half + r:8 * half + r + 1, :] * h + u[8 * half + r:8 * half + r + 1, :]
                    out = jnp.where(sub == r, h, out)
                outs.append(out)
            hs = jnp.concatenate(outs, axis=0)
            h_ref[rows, :] = hs
            ge, _ = _gelu(ag_ref[rows, :].astype(F32))
            ya_ref[rows, :] = (hs * ge).astype(BF16)
            return h

        hc[...] = lax.fori_loop(0, TM // RS, strip, hc[...])

    blk_ = pl.BlockSpec((TM, CT), lambda c, i: (i, c))
    vec = pl.BlockSpec((1, CT), lambda c, i: (0, c))
    return pl.pallas_call(
        body, name="rg_fwd", grid=(NCT, t // TM),
        in_specs=[blk_, pl.BlockSpec((TM, CT), lambda c, i: (i, NCT + c)), blk_,
                  pl.BlockSpec((TM, CT), lambda c, i: (i, COL_AG + c)), vec, vec, vec],
        out_specs=[blk_, blk_],
        out_shape=[jax.ShapeDtypeStruct((t, RGW), F32), jax.ShapeDtypeStruct((t, RGW), BF16)],
        scratch_shapes=[pltpu.VMEM((1, CT), F32)],
        compiler_params=_cparams(("parallel", "arbitrary")),
    )(gri, gri, xc, proj, ba, bx, lam)


def _rg_bwd(dya, h, gri, xc, proj, ba, bx, lam, dproj):
    t = gri.shape[0]
    nt = t // TM
    assert NCT == 1

    def body(dya_ref, h_ref, hh_ref, gr_ref, gi_ref, xc_ref, ag_ref, ba_ref, bx_ref, lam_ref, _,
             dgri_ref, dxc_ref, dag_ref, dba_ref, dbx_ref, dlam_ref, hbuf, acc, cc):
        i = pl.program_id(1)
        ri = nt - 1 - i

        @pl.when(i == 0)
        def _():
            cc[...] = jnp.zeros_like(cc)
            dba_ref[...] = jnp.zeros_like(dba_ref)
            dbx_ref[...] = jnp.zeros_like(dbx_ref)
            dlam_ref[...] = jnp.zeros_like(dlam_ref)

        acc[...] = jnp.zeros_like(acc)
        ba, bx, lam = ba_ref[...], bx_ref[...], lam_ref[...]
        halo_last = jnp.where(ri > 0, hh_ref[7:8, :], 0.0)
        sub = lax.broadcasted_iota(jnp.int32, (8, CT), 0)
        c0 = 0.7978845608028654

        def strip(kk, c):
            k = TM // RS - 1 - kk
            off = pl.multiple_of(k * RS, RS)
            rows = pl.ds(off, RS)
            xc, hv = xc_ref[rows, :], h_ref[rows, :]
            ag, dya = ag_ref[rows, :].astype(F32), dya_ref[rows, :].astype(F32)
            r, ig, sp, a, s, rs, _, live = _rg_gates(gr_ref[rows, :], gi_ref[rows, :], xc, ba, bx, lam, ri * TM + off)
            ge, th = _gelu(ag)
            dge = 0.5 * (1.0 + th) + 0.5 * ag * (1.0 - th * th) * c0 * (1.0 + 3.0 * 0.044715 * ag * ag)
            dag_ref[rows, :] = (dya * hv * dge).astype(BF16)
            d = dya * ge
            outs = []
            for half in range(RS // 8 - 1, -1, -1):
                out = jnp.zeros((8, CT), F32)
                for rr in range(7, -1, -1):
                    g = d[8 * half + rr:8 * half + rr + 1, :] + c
                    c = a[8 * half + rr:8 * half + rr + 1, :] * g
                    out = jnp.where(sub == rr, g, out)
                outs.insert(0, out)
            g = jnp.concatenate(outs, axis=0)
            before = h_ref[pl.ds(pl.multiple_of(jnp.maximum(off - RS, 0), RS), RS), :]
            hbuf[7:8, :] = jnp.where(k == 0, halo_last, before[RS - 1:RS, :])
            hbuf[8:, :] = hv
            hprev = hbuf[pl.ds(7, RS), :]
            du = jnp.where(live, g, 0.0)
            ds = du * (ig * xc)
            dm = du * s
            dla = (g * hprev - a * ds * rs) * a
            dr = dla * (-RG_C * sp) * r * (1.0 - r)
            di = dm * xc * ig * (1.0 - ig)
            dgri_ref[rows, :CT] = dr.astype(BF16)
            dgri_ref[rows, CT:] = di.astype(BF16)
            dxc_ref[rows, :] = dm * ig
            dl = dla * (-RG_C * r)
            for half in range(RS // 8):
                part = slice(8 * half, 8 * half + 8)
                acc[0] += dr[part]
                acc[1] += di[part]
                acc[2] += dl[part]
            return c

        cc[...] = lax.fori_loop(0, TM // RS, strip, cc[...])
        dba_ref[...] += jnp.sum(acc[0], axis=0, keepdims=True)
        dbx_ref[...] += jnp.sum(acc[1], axis=0, keepdims=True)
        dlam_ref[...] += jnp.sum(acc[2], axis=0, keepdims=True) * (-_sig(-lam))

    rblk = pl.BlockSpec((TM, CT), lambda c, i: (nt - 1 - i, c))
    vec = pl.BlockSpec((1, CT), lambda c, i: (0, c))
    hh = pl.BlockSpec((8, CT), lambda c, i: (jnp.maximum((nt - 1 - i) * (TM // 8) - 1, 0), c))
    agb = pl.BlockSpec((TM, CT), lambda c, i: (nt - 1 - i, COL_AG + c))
    return pl.pallas_call(
        body, name="rg_bwd", grid=(NCT, nt),
        in_specs=[rblk, rblk, hh, rblk, pl.BlockSpec((TM, CT), lambda c, i: (nt - 1 - i, NCT + c)), rblk, agb,
                  vec, vec, vec, ANY],
        out_specs=[pl.BlockSpec((TM, 2 * CT), lambda c, i: (nt - 1 - i, c)), rblk, agb, vec, vec, vec],
        out_shape=[jax.ShapeDtypeStruct((t, 2 * RGW), BF16),
                   jax.ShapeDtypeStruct((t, RGW), F32), jax.ShapeDtypeStruct((t, D_INP), BF16),
                   jax.ShapeDtypeStruct((1, RGW), F32), jax.ShapeDtypeStruct((1, RGW), F32),
                   jax.ShapeDtypeStruct((1, RGW), F32)],
        scratch_shapes=[pltpu.VMEM((RS + 8, CT), F32), pltpu.VMEM((3, 8, CT), F32), pltpu.VMEM((1, CT), F32)],
        input_output_aliases={10: 2},
        compiler_params=_cparams(("parallel", "arbitrary")),
    )(dya, h, h, gri, gri, xc, proj, ba, bx, lam, dproj)


NCH = TM // CH
HI = lax.Precision.HIGHEST


def _tri_dot(tri, x):
    hi = x.astype(BF16)
    r1 = x - hi.astype(F32)
    mid = r1.astype(BF16)
    lo = (r1 - mid.astype(F32)).astype(BF16)
    return _dot(tri, hi, NN) + _dot(tri, mid, NN) + _dot(tri, lo, NN)


def _hg_chunk(qr, fr, lb):
    sf = _sig(fr)
    fg = lb + (1.0 - lb) * sf
    k = (1.0 - lb) * (1.0 - sf)
    sq = _sig(qr)
    q = qr * sq
    ri = lax.broadcasted_iota(jnp.int32, (CH, CH), 0)
    ci = lax.broadcasted_iota(jnp.int32, (CH, CH), 1)
    b = _tri_dot((ri >= ci).astype(BF16), jnp.log(fg))
    bm, bl = b[CH // 2 - 1:CH // 2, :], b[CH - 1:CH, :]
    eb = jnp.exp(b)
    ebm = jnp.exp(b - bm)
    ekm = jnp.exp(jnp.minimum(bm - b, EXP_CLAMP))
    ekl = jnp.exp(bl - b)
    return dict(sf=sf, fg=fg, k=k, sq=sq, q=q, eb=eb, ebm=ebm, ekm=ekm, ekl=ekl, ebl=jnp.exp(bl),
                qe=q * eb, qh=q * ebm, kh=k * ekm, kd=k * ekl, causal=ri >= ci, upper=(ci >= ri).astype(BF16))


def _hgrn_fwd(proj, lbl, gn):
    t = proj.shape[0]
    nt = t // TM

    def body(q_ref, f_ref, v_ref, g_ref, lbl_ref, gn_ref, yb_ref, o_ref, st_ref, st):
        @pl.when(pl.program_id(0) == 0)
        def _():
            st[...] = jnp.zeros_like(st)

        l = lbl_ref[...]
        lb = _sig(l[0:1, :] - l[1:2, :])
        gnv = gn_ref[...]

        def chunk(c, carry):
            off = pl.multiple_of(c * CH, CH)
            rows = pl.ds(off, CH)
            z = _hg_chunk(q_ref[rows, :].astype(F32), f_ref[rows, :].astype(F32), lb)
            v, gg = v_ref[rows, :], g_ref[rows, :].astype(F32)
            for hh in range(HEADS):
                sl = slice(hh * HD, (hh + 1) * HD)
                s_prev = st[hh]
                st_ref[c, hh] = s_prev
                vb = v[:, sl].astype(BF16)
                att = jnp.where(z["causal"], _dot(z["qh"][:, sl].astype(BF16), z["kh"][:, sl].astype(BF16), NT), 0.0)
                o = _dot(z["qe"][:, sl].astype(BF16), s_prev.astype(BF16), NT) + _dot(att.astype(BF16), vb, NN)
                st[hh] = s_prev * z["ebl"][:, sl] + _dot(vb, z["kd"][:, sl].astype(BF16), TN_)
                xh, _ = _rms(o)
                gh = gg[:, sl]
                o_ref[rows, sl] = o
                yb_ref[rows, sl] = (xh * gnv * gh * _sig(gh)).astype(BF16)
            return carry

        lax.fori_loop(0, NCH, chunk, 0, unroll=True)

    def col(j):
        return pl.BlockSpec((TM, HGW), lambda i, j=j: (i, j))

    return pl.pallas_call(
        body, name="hgrn_fwd", grid=(nt,),
        in_specs=[col(0), col(1), col(2), col(3), pl.BlockSpec((2, HGW), lambda i: (0, 0)),
                  pl.BlockSpec((1, HD), lambda i: (0, 0))],
        out_specs=[col(0), col(0), pl.BlockSpec((NCH, HEADS, HD, HD), lambda i: (i, 0, 0, 0))],
        out_shape=[jax.ShapeDtypeStruct((t, HGW), BF16), jax.ShapeDtypeStruct((t, HGW), F32),
                   jax.ShapeDtypeStruct((t // CH, HEADS, HD, HD), F32)],
        scratch_shapes=[pltpu.VMEM((HEADS, HD, HD), F32)],
        compiler_params=_cparams(("arbitrary",)),
    )(proj, proj, proj, proj, lbl, gn)


def _hgrn_bwd(dyb, proj, o, states, lbl, gn, dproj):
    t = proj.shape[0]
    nt = t // TM

    def body(dy_ref, q_ref, f_ref, v_ref, g_ref, o_ref, st_ref, lbl_ref, gn_ref, _,
             dp_ref, dgn_ref, dl_ref, dst, dlb):
        i = pl.program_id(0)

        @pl.when(i == 0)
        def _():
            dst[...] = jnp.zeros_like(dst)
            dlb[...] = jnp.zeros_like(dlb)
            dgn_ref[...] = jnp.zeros_like(dgn_ref)

        l = lbl_ref[...]
        lb = _sig(l[0:1, :] - l[1:2, :])
        gnv = gn_ref[...]
        last = lax.broadcasted_iota(jnp.int32, (CH, HD), 0) == CH - 1

        def chunk(cc, carry):
            c = NCH - 1 - cc
            off = pl.multiple_of(c * CH, CH)
            rows = pl.ds(off, CH)
            qr, fr = q_ref[rows, :].astype(F32), f_ref[rows, :].astype(F32)
            z = _hg_chunk(qr, fr, lb)
            v, gg, ov, dy = v_ref[rows, :], g_ref[rows, :].astype(F32), o_ref[rows, :], dy_ref[rows, :].astype(F32)
            dqs, dks, dbs, dvs, dgs = [], [], [], [], []
            dgn = jnp.zeros((1, HD), F32)
            for hh in range(HEADS):
                sl = slice(hh * HD, (hh + 1) * HD)
                s_prev, ds_new = st_ref[c, hh], dst[hh]
                qe, qh, kh, kd = z["qe"][:, sl], z["qh"][:, sl], z["kh"][:, sl], z["kd"][:, sl]
                ebl = z["ebl"][:, sl]
                gh, dyh = gg[:, sl], dy[:, sl]
                xh, rr = _rms(ov[:, sl])
                sg = _sig(gh)
                dyn = dyh * (gh * sg)
                dgs.append(dyh * (xh * gnv) * (sg * (1.0 + gh * (1.0 - sg))))
                dgn = dgn + jnp.sum(dyn * xh, axis=0, keepdims=True)
                dxh = dyn * gnv
                do = (rr * (dxh - xh * jnp.mean(dxh * xh, axis=-1, keepdims=True))).astype(BF16)
                vb, dsb = v[:, sl].astype(BF16), ds_new.astype(BF16)
                qeb, qhb, khb, kdb = (a.astype(BF16) for a in (qe, qh, kh, kd))
                att = jnp.where(z["causal"], _dot(qhb, khb, NT), 0.0).astype(BF16)
                datt = jnp.where(z["causal"], _dot(do, vb, NT), 0.0).astype(BF16)
                dvs.append(_dot(att, do, TN_) + _dot(kdb, dsb, NT))
                dqe = _dot(do, s_prev.astype(BF16), NN)
                dqh = _dot(datt, khb, NN)
                dkh = _dot(datt, qhb, TN_)
                dkd = _dot(vb, dsb, NN)
                qe, qh, kh, kd = (a.astype(F32) for a in (qeb, qhb, khb, kdb))
                dbl = (jnp.sum(dkd * kd, axis=0, keepdims=True)
                       + jnp.sum(ds_new * s_prev, axis=0, keepdims=True) * ebl)
                dqs.append(dqe * z["eb"][:, sl] + dqh * z["ebm"][:, sl])
                dks.append(dkh * z["ekm"][:, sl] + dkd * z["ekl"][:, sl])
                dbs.append(dqe * qe + dqh * qh - dkh * kh - dkd * kd + jnp.where(last, dbl, 0.0))
                dst[hh] = _dot(do, qeb, TN_) + ds_new * ebl
            dgn_ref[...] += dgn
            dq, dk, db = (jnp.concatenate(x, axis=1) for x in (dqs, dks, dbs))
            dlf = _tri_dot(z["upper"], db)
            sf, fg, sq = z["sf"], z["fg"], z["sq"]
            dmix = dlf / fg - dk
            dsf = dmix * (1.0 - lb)
            dlb[...] += jnp.sum(dmix * (1.0 - sf), axis=0, keepdims=True)
            dp_ref[rows, 0:HGW] = (dq * (sq * (1.0 + qr * (1.0 - sq)))).astype(BF16)
            dp_ref[rows, HGW:2 * HGW] = (dsf * sf * (1.0 - sf)).astype(BF16)
            dp_ref[rows, 2 * HGW:3 * HGW] = jnp.concatenate(dvs, axis=1).astype(BF16)
            dp_ref[rows, 3 * HGW:4 * HGW] = jnp.concatenate(dgs, axis=1).astype(BF16)
            return carry

        lax.fori_loop(0, NCH, chunk, 0, unroll=True)
        dl0 = dlb[...] * lb * (1.0 - lb)
        dl_ref[0:1, :] = dl0
        dl_ref[1:2, :] = -dl0

    def col(j):
        return pl.BlockSpec((TM, HGW), lambda i, j=j: (nt - 1 - i, j))

    return pl.pallas_call(
        body, name="hgrn_bwd", grid=(nt,),
        in_specs=[col(0), col(0), col(1), col(2), col(3), col(0),
                  pl.BlockSpec((NCH, HEADS, HD, HD), lambda i: (nt - 1 - i, 0, 0, 0)),
                  pl.BlockSpec((2, HGW), lambda i: (0, 0)), pl.BlockSpec((1, HD), lambda i: (0, 0)), ANY],
        out_specs=[pl.BlockSpec((TM, 4 * HGW), lambda i: (nt - 1 - i, 0)),
                   pl.BlockSpec((1, HD), lambda i: (0, 0)), pl.BlockSpec((2, HGW), lambda i: (0, 0))],
        out_shape=[jax.ShapeDtypeStruct((t, D_INP), BF16), jax.ShapeDtypeStruct((1, HD), F32),
                   jax.ShapeDtypeStruct((2, HGW), F32)],
        scratch_shapes=[pltpu.VMEM((HEADS, HD, HD), F32), pltpu.VMEM((1, HGW), F32)],
        input_output_aliases={9: 0},
        compiler_params=_cparams(("arbitrary",)),
    )(dyb, proj, proj, proj, proj, o, states, lbl, gn, dproj)


def _rotate_w_in(w):
    return jnp.concatenate([w[:, ROT:], jnp.zeros((w.shape[0], COL_PAD), w.dtype), w[:, :ROT]], axis=1)


def _unrotate_g_in(g):
    return jnp.concatenate([g[:, D_INP - ROT:], g[:, :D_IN - ROT]], axis=1)


SB = RGW // 2


def _bd_gates(t):
    tm = _pick(t, (768, 256))

    def col(j):
        return 2 * (j % 2) + j // 2

    return ((t // tm, 4, 1), (tm, SB, SB),
            (lambda i, j, k: (i, j // 2), lambda i, j, k: (j // 2, col(j)), lambda i, j, k: (i, col(j))))


def _bd_dw(t):
    tk = _pick(t, (1408, 768))
    return (1, 4, t // tk), (SB, SB, tk), (lambda i, j, k: (k, j % 2), lambda i, j, k: (k, j), lambda i, j, k: (j % 2, j))


def _block_diag(w):
    eye = jnp.eye(RG_BLOCKS, dtype=w.dtype)
    return (w[:, :, None, :] * eye[:, None, :, None]).reshape(RGW, RGW)


def _diag_blocks(wd):
    w4 = wd.reshape(RG_BLOCKS, RG_BLOCK, RG_BLOCKS, RG_BLOCK)
    return jnp.stack([w4[n, :, n, :] for n in range(RG_BLOCKS)])


def _local_step(h0, target, wts, small, hosts=None, finalize=None):
    hosts = hosts or {}
    carried = {}
    g = {}
    t = h0.shape[0]

    def mm(a, b, mode, out_dtype, name, **kw):
        if name not in hosts:
            return _mm(a, b, mode, out_dtype, name, **kw)
        host = hosts[name](g)
        out, res = _mm(a, b, mode, out_dtype, name, host=host, **kw)
        carried[name] = (host.ins, res)
        return out

    w_in = wts["w_in"]
    wax_d = jnp.concatenate([_block_diag(small["rg_wa"]), _block_diag(small["rg_wx"])], axis=1).astype(BF16)
    ba, bx, lam = small["rg_ba"], small["rg_bx"], small["rg_lambda"]
    lbl, gn = small["hg_lb_logits"], small["hg_norm_g"]
    conv_w, conv_b = small["conv_w"], small["conv_b"]

    tmm = _pick(t, (768, 256))
    proj, hn1 = mm(h0, w_in, "nn", BF16, "mm_proj", norm_gain=small["norm1_g"])
    if finalize is not None:
        wts = finalize(carried["mm_proj"][1])
    w_pa, w_pb, w_out, w_fi, w_fd = (wts[k] for k in ("w_proj_a", "w_proj_b", "w_out", "w_ffn_in", "w_ffn_down"))
    xc, xcb = _conv_fwd(proj, conv_w, conv_b)
    gri = mm(xcb, wax_d, "nn", F32, "mm_rg_gates", sparse=_bd_gates(t))
    hrg, ya = _rg_fwd(gri, xc, proj, ba, bx, lam)
    yb, o, states = _hgrn_fwd(proj, lbl, gn)
    pa = mm(ya, w_pa, "nn", BF16, "mm_pa")
    pb = mm(yb, w_pb, "nn", BF16, "mm_pb")
    merged = _merge_fwd(pa, pb, proj)
    h1 = mm(merged, w_out, "nn", F32, "mm_out", resid=h0)
    (gu, act), hn2 = mm(h1, w_fi, "nn", BF16, "mm_ffn_in", norm_gain=small["norm2_g"], epi=_swiglu_epi(t),
                        tiles=(FM, 2 * FH))
    dh2, dh2b, loss, g["norm_f_g"] = mm(act, w_fd, "nn", F32, "mm_ffn_down", resid=h1,
                                        epi=_final_loss_epi(small["norm_f_g"], target, t), tiles=(TM, D))

    g["w_ffn_down"] = mm(act, dh2b, "tn", BF16, "mm_d_wfd")
    (dgu,) = mm(dh2b, w_fd, "nt", BF16, "mm_d_act", epi=_swiglu_bwd_epi(gu), tiles=(FM, FH))
    g["w_ffn_in"] = mm(hn2, dgu, "tn", BF16, "mm_d_wfi")
    dh1, dh1b, g["norm2_g"] = mm(dgu, w_fi, "nt", BF16, "mm_d_hn2", epi=_rms_bwd_epi(h1, small["norm2_g"], dh2, tmm),
                                 tiles=(tmm, D))
    g["w_out"] = mm(merged, dh1b, "tn", BF16, "mm_d_wout")
    dpa, dpb, dproj = mm(dh1b, w_out, "nt", BF16, "mm_d_merged", epi=_merge_bwd_epi(pa, pb, proj, TM), tiles=(TM, D))
    dproj = _zero_pad_cols(dproj)
    g["w_proj_a"] = mm(ya, dpa, "tn", BF16, "mm_d_wpa")
    g["w_proj_b"] = mm(yb, dpb, "tn", BF16, "mm_d_wpb")
    dya = mm(dpa, w_pa, "nt", BF16, "mm_d_ya")
    dyb = mm(dpb, w_pb, "nt", BF16, "mm_d_yb")
    dproj, g["hg_norm_g"], g["hg_lb_logits"] = _hgrn_bwd(dyb, proj, o, states, lbl, gn, dproj)
    dgri, dxc, dproj, g["rg_ba"], g["rg_bx"], g["rg_lambda"] = _rg_bwd(dya, hrg, gri, xc, proj, ba, bx, lam, dproj)
    dwax = mm(xcb, dgri, "tn", F32, "mm_d_wax", sparse=_bd_dw(t))
    g["rg_wa"], g["rg_wx"] = _diag_blocks(dwax[:, :RGW]), _diag_blocks(dwax[:, RGW:])
    dxc = mm(dgri, wax_d, "nt", F32, "mm_d_xc", resid=dxc)
    dproj, g["conv_w"], g["conv_b"] = _conv_bwd(dxc, proj, conv_w, dproj)
    g["w_in"] = mm(hn1, dproj, "tn", BF16, "mm_d_win")
    dhn1 = mm(dproj, w_in, "nt", BF16, "mm_d_hn1")
    grad_x, g["meta_tokens"], g["norm1_g"] = _rmsnorm1_bwd(dhn1, h0, small["norm1_g"], dh1)
    return loss, grad_x, g, carried


def _adamw_math(w, g, m, v):
    m = B1 * m + (1.0 - B1) * g
    v = B2 * v + (1.0 - B2) * (g * g)
    m_hat = m / (1.0 - B1 ** STEP)
    v_hat = v / (1.0 - B2 ** STEP)
    return -LR * (m_hat / (jnp.sqrt(v_hat) + ADAM_EPS) + WD * w), m, v


def _adamw(w, g, m, v, name):
    r, c = w.shape
    tr = _pick(r, (256, 352, 320, 128, 64, 32, 16, 8))

    def body(w_ref, g_ref, m_ref, v_ref, d_ref, mo_ref, vo_ref):
        d_ref[...], mo_ref[...], vo_ref[...] = _adamw_math(w_ref[...], g_ref[...], m_ref[...], v_ref[...])

    blk = pl.BlockSpec((tr, c), lambda i: (i, 0))
    return pl.pallas_call(
        body, name=name, grid=(r // tr,), in_specs=[blk] * 4, out_specs=[blk] * 3,
        out_shape=[jax.ShapeDtypeStruct((r, c), F32)] * 3,
        compiler_params=_cparams(("parallel",)),
    )(w, g, m, v)


def _adamw_rows(recv, params, extras, name):
    _, _, width = recv.shape
    n_par = len(params)

    def body(*refs):
        r_ref = refs[0]
        ins = refs[1:1 + 3 * n_par]
        outs = refs[1 + 3 * n_par:]
        g_all = r_ref[0]
        for p in range(1, 8):
            g_all = g_all + r_ref[p]
        for q, (row0, w, _, _) in enumerate(params):
            g = g_all[row0:row0 + w.shape[0], :]
            w_ref, m_ref, v_ref = ins[3 * q:3 * q + 3]
            g_ref, d_ref, mo_ref, vo_ref = outs[4 * q:4 * q + 4]
            g_ref[...] = g
            d_ref[...], mo_ref[...], vo_ref[...] = _adamw_math(w_ref[...], g, m_ref[...], v_ref[...])
        for q, (row0, n) in enumerate(extras):
            outs[4 * n_par + q][...] = g_all[row0:row0 + n, :]

    shapes = [jax.ShapeDtypeStruct(w.shape, F32) for _, w, _, _ in params for _ in range(4)]
    shapes += [jax.ShapeDtypeStruct((n, width), F32) for _, n in extras]
    vm = pl.BlockSpec(memory_space=pltpu.VMEM)
    outs = pl.pallas_call(
        body, name=name, in_specs=[vm] * (1 + 3 * n_par), out_specs=[vm] * len(shapes), out_shape=shapes,
        compiler_params=pltpu.CompilerParams(vmem_limit_bytes=VMEM_LIMIT),
    )(recv, *[a for _, w, m, v in params for a in (w, m, v)])
    return [tuple(outs[4 * q:4 * q + 4]) for q in range(n_par)], list(outs[4 * n_par:])


def _col_block(weight, px, py):
    return 2 * py + px if weight == "w_ffn_in" else 2 * px + py


def _sum_place(recv, gfull, weight, xyc, name):
    _, r, c = recv.shape
    tr = _pick(r, (128, 176, 160, 64, 32, 16))
    nb = r // tr

    def body(_, own_ref, r_ref, o_ref):
        g = own_ref[...].astype(F32)
        for p in range(7):
            g = g + r_ref[p].astype(F32)
        o_ref[...] = g

    if weight in COL_SHARDED:
        own = pl.BlockSpec((tr, c), lambda i, s: (s[2] * nb + i, _col_block(weight, s[0], s[1])))
    else:
        own = pl.BlockSpec((tr, c), lambda i, s: ((4 * s[0] + 2 * s[1] + s[2]) * nb + i, 0))
    return pl.pallas_call(
        body, name=name,
        grid_spec=pltpu.PrefetchScalarGridSpec(
            num_scalar_prefetch=1, grid=(nb,),
            in_specs=[own, pl.BlockSpec((7, tr, c), lambda i, s: (0, i, 0))],
            out_specs=pl.BlockSpec((tr, c), lambda i, s: (s[2] * nb + i, 0))),
        out_shape=jax.ShapeDtypeStruct((2 * r, c), F32),
        compiler_params=_cparams(("arbitrary",)),
    )(xyc, gfull, recv)


def _place_own(shard, slot, dtype, name):
    r, c = shard.shape
    tr = _pick(r, (256, 352, 320, 128, 32))

    def body(_, x_ref, o_ref):
        o_ref[...] = x_ref[...].astype(dtype)

    return pl.pallas_call(
        body, name=name,
        grid_spec=pltpu.PrefetchScalarGridSpec(
            num_scalar_prefetch=1, grid=(r // tr,),
            in_specs=[pl.BlockSpec((tr, c), lambda i, s: (i, 0))],
            out_specs=pl.BlockSpec((None, tr, c), lambda i, s: (s[0], i, 0))),
        out_shape=jax.ShapeDtypeStruct((4, r, c), dtype),
        compiler_params=_cparams(("arbitrary",)),
    )(slot, shard)


BIG = ("w_in", "w_proj_a", "w_proj_b", "w_out", "w_ffn_in", "w_ffn_down")
COL_SHARDED = ("w_in", "w_ffn_in")


def _coords():
    return lax.axis_index("x"), lax.axis_index("y"), lax.axis_index("c")


def _gather_host(bufs):
    n = len(bufs)

    def place():
        x, y, c = _coords()
        return x, y, c, [(1 - x, y), (x, 1 - y), (1 - x, 1 - y)]

    def slab(b, chip, half_of):
        half = b.shape[1] // 2
        return b.at[chip, pl.ds(pl.multiple_of(half_of * half, 8), half)]

    def copy(b, sems, w, k, chip, half_of, to):
        s = slab(b, chip, half_of)
        return pltpu.make_async_remote_copy(src_ref=s, dst_ref=s, send_sem=sems[0].at[w, k], recv_sem=sems[1].at[w, k],
                                            device_id=to, device_id_type=MESH)

    def start(_, outs, sems):
        x, y, c, chips = place()
        for w, b in enumerate(outs):
            for k, (cx, cy) in enumerate(chips):
                copy(b, sems, w, k, 2 * x + y, c, (cx, cy, c)).start()

    def mid(_, outs, sems):
        x, y, c, chips = place()
        for w, b in enumerate(outs):
            for k, (cx, cy) in enumerate(chips):
                copy(b, sems, w, k, 2 * cx + cy, c, (cx, cy, c)).wait_recv()
                copy(b, sems, w, 3 + k, 2 * cx + cy, c, (x, y, 1 - c)).start()

    def finish(_, outs, sems):
        x, y, c, chips = place()
        for w, b in enumerate(outs):
            for k, (cx, cy) in enumerate(chips):
                copy(b, sems, w, 3 + k, 2 * cx + cy, 1 - c, (x, y, 1 - c)).wait_recv()
        for w, b in enumerate(outs):
            for k, (cx, cy) in enumerate(chips):
                copy(b, sems, w, k, 2 * x + y, c, (cx, cy, c)).wait_send()
                copy(b, sems, w, 3 + k, 2 * cx + cy, c, (x, y, 1 - c)).wait_send()

    return _Host(ins=list(bufs), out_shapes=[jax.ShapeDtypeStruct(b.shape, b.dtype) for b in bufs],
                 aliases={i: i for i in range(n)},
                 sems=[pltpu.SemaphoreType.DMA((n, 6)), pltpu.SemaphoreType.DMA((n, 6))],
                 start=start, mid=mid, finish=finish)


def _run_host(host, name):
    n_in, n_out = len(host.ins), len(host.out_shapes)

    def body(*refs):
        ins, outs, sems = refs[:n_in], refs[n_in:n_in + n_out], refs[n_in + n_out:]
        host.start(ins, outs, sems)
        if host.mid is not None:
            host.mid(ins, outs, sems)
        host.finish(ins, outs, sems)

    return pl.pallas_call(
        body, name=name, in_specs=[ANY] * n_in, out_specs=[ANY] * n_out, out_shape=list(host.out_shapes),
        scratch_shapes=list(host.sems), input_output_aliases=dict(host.aliases),
    )(*host.ins)


def _peer(x, y, c, k):
    fx, fy, fc = (k >> 2) & 1, (k >> 1) & 1, k & 1
    return (x ^ fx if fx else x, y ^ fy if fy else y, c ^ fc if fc else c)


def _sub_shape(name, full_shape):
    r, c = full_shape
    return (r // 2, c // 4) if name in COL_SHARDED else (r // 8, c)


def _exchange_host(names, grads):
    n = len(names)
    shapes = [_sub_shape(k, g.shape) for k, g in zip(names, grads)]

    def copy(ins, outs, sems, w, k):
        x, y, c = _coords()
        px, py, pc = _peer(x, y, c, k)
        sr, sc = shapes[w]
        if names[w] in COL_SHARDED:
            col = _col_block(names[w], px, py) * sc
            src = ins[w].at[pl.ds(pl.multiple_of(pc * sr, 16), sr), pl.ds(pl.multiple_of(col, 128), sc)]
        else:
            src = ins[w].at[pl.ds(pl.multiple_of((4 * px + 2 * py + pc) * sr, 16), sr)]
        return pltpu.make_async_remote_copy(
            src_ref=src, dst_ref=outs[w].at[k - 1], send_sem=sems[0].at[w, k - 1], recv_sem=sems[1].at[w, k - 1],
            device_id=(px, py, pc), device_id_type=MESH)

    def start(ins, outs, sems):
        for w in range(n):
            for k in range(1, 8):
                copy(ins, outs, sems, w, k).start()

    def finish(ins, outs, sems):
        for w in range(n):
            for k in range(1, 8):
                copy(ins, outs, sems, w, k).wait_recv()
        for w in range(n):
            for k in range(1, 8):
                copy(ins, outs, sems, w, k).wait_send()

    return _Host(ins=list(grads), out_shapes=[jax.ShapeDtypeStruct((7,) + s, g.dtype) for s, g in zip(shapes, grads)],
                 aliases={}, sems=[pltpu.SemaphoreType.DMA((n, 7)), pltpu.SemaphoreType.DMA((n, 7))],
                 start=start, mid=None, finish=finish)


def _sibling_host(bufs):
    n = len(bufs)

    def copy(outs, sems, w, half_of):
        x, y, c = _coords()
        half = outs[w].shape[0] // 2
        rows = outs[w].at[pl.ds(pl.multiple_of(half_of * half, 8), half)]
        return pltpu.make_async_remote_copy(src_ref=rows, dst_ref=rows, send_sem=sems[0].at[w], recv_sem=sems[1].at[w],
                                            device_id=(x, y, 1 - c), device_id_type=MESH)

    def start(_, outs, sems):
        c = lax.axis_index("c")
        for w in range(n):
            copy(outs, sems, w, c).start()

    def finish(_, outs, sems):
        c = lax.axis_index("c")
        for w in range(n):
            copy(outs, sems, w, 1 - c).wait_recv()
        for w in range(n):
            copy(outs, sems, w, c).wait_send()

    return _Host(ins=list(bufs), out_shapes=[jax.ShapeDtypeStruct(b.shape, b.dtype) for b in bufs],
                 aliases={i: i for i in range(n)},
                 sems=[pltpu.SemaphoreType.DMA((n,)), pltpu.SemaphoreType.DMA((n,))], start=start, mid=None, finish=finish)


def _pack_host(pack):
    def me_of():
        x, y, c = _coords()
        return x, y, c, 4 * x + 2 * y + c

    def copy(ins, outs, sems, k, slot):
        x, y, c, _ = me_of()
        return pltpu.make_async_remote_copy(src_ref=ins[0], dst_ref=outs[0].at[slot], send_sem=sems[0].at[k - 1],
                                            recv_sem=sems[1].at[k - 1], device_id=_peer(x, y, c, k), device_id_type=MESH)

    def start(ins, outs, sems):
        me = me_of()[3]
        pltpu.make_async_copy(ins[0], outs[0].at[me], sems[2]).start()
        for k in range(1, 8):
            copy(ins, outs, sems, k, me).start()

    def finish(ins, outs, sems):
        x, y, c, me = me_of()
        for k in range(1, 8):
            px, py, pc = _peer(x, y, c, k)
            copy(ins, outs, sems, k, 4 * px + 2 * py + pc).wait_recv()
        for k in range(1, 8):
            copy(ins, outs, sems, k, me).wait_send()
        pltpu.make_async_copy(ins[0], outs[0].at[me], sems[2]).wait()

    return _Host(ins=[pack], out_shapes=[jax.ShapeDtypeStruct((8,) + pack.shape, F32)], aliases={},
                 sems=[pltpu.SemaphoreType.DMA((7,)), pltpu.SemaphoreType.DMA((7,)), pltpu.SemaphoreType.DMA],
                 start=start, mid=None, finish=finish)


def _join_hosts(hosts):
    ins, outs, sems, aliases, spans = [], [], [], {}, []
    for h in hosts:
        spans.append((len(ins), len(h.ins), len(outs), len(h.out_shapes), len(sems), len(h.sems)))
        for i_in, i_out in h.aliases.items():
            aliases[len(ins) + i_in] = len(outs) + i_out
        ins, outs, sems = ins + list(h.ins), outs + list(h.out_shapes), sems + list(h.sems)

    def phase(which):
        fns = [getattr(h, which) for h in hosts]
        if all(f is None for f in fns):
            return None

        def run(i_refs, o_refs, s_refs):
            for f, (i0, n_i, o0, n_o, s0, n_s) in zip(fns, spans):
                if f is not None:
                    f(i_refs[i0:i0 + n_i], o_refs[o0:o0 + n_o], s_refs[s0:s0 + n_s])

        return run

    return _Host(ins, outs, aliases, sems, phase("start"), phase("mid"), phase("finish"))


def _allgather_pack(pack):
    rows = pack.shape[0]

    def body(p_ref, o_ref, send, recv, local):
        x, y, c = _coords()
        me = 4 * x + 2 * y + c
        mine = pltpu.make_async_copy(p_ref, o_ref.at[me], local)
        mine.start()

        def copy(k, slot):
            return pltpu.make_async_remote_copy(src_ref=p_ref, dst_ref=o_ref.at[slot], send_sem=send.at[k - 1],
                                                recv_sem=recv.at[k - 1], device_id=_peer(x, y, c, k), device_id_type=MESH)

        for k in range(1, 8):
            copy(k, me).start()
        for k in range(1, 8):
            px, py, pc = _peer(x, y, c, k)
            copy(k, 4 * px + 2 * py + pc).wait_recv()
        for k in range(1, 8):
            copy(k, me).wait_send()
        mine.wait()

    return pl.pallas_call(
        body, name="allgather_pack", in_specs=[pl.BlockSpec(memory_space=pltpu.VMEM)], out_specs=ANY,
        out_shape=jax.ShapeDtypeStruct((8,) + pack.shape, F32),
        scratch_shapes=[pltpu.SemaphoreType.DMA((7,)), pltpu.SemaphoreType.DMA((7,)), pltpu.SemaphoreType.DMA],
        compiler_params=pltpu.CompilerParams(vmem_limit_bytes=VMEM_LIMIT),
    )(pack)


SMALL = ("norm1_g", "conv_b", "rg_wa", "rg_ba", "rg_wx", "rg_bx", "rg_lambda", "hg_lb_logits", "hg_norm_g",
         "norm2_g", "norm_f_g")
SHARDED_SMALL = ("meta_tokens", "conv_w")

ROWS = {"conv_b": (1, RGW), "rg_ba": (1, RGW), "rg_bx": (1, RGW), "rg_lambda": (1, RGW), "conv_w": (4, RGW),
        "norm2_g": (1, D), "norm_f_g": (1, D), "hg_lb_logits": (2, D),
        "rg_wa": (RGW * RG_BLOCK // 128, 128), "rg_wx": (RGW * RG_BLOCK // 128, 128), "hg_norm_g": (1, 128),
        "norm1_g": (1, D), "meta_tokens": (N_META, D), "loss": (1, D)}
PACKS = {"early_wide": ("conv_b", "rg_ba", "rg_bx", "rg_lambda", "conv_w"),
         "early_model": ("norm2_g", "norm_f_g", "hg_lb_logits"),
         "early_lane": ("rg_wa", "rg_wx", "hg_norm_g"),
         "late": ("norm1_g", "meta_tokens", "loss")}
EARLY = ("early_wide", "early_model", "early_lane")
NO_UPDATE = ("conv_w", "meta_tokens", "loss")


def _pack_rows(vals, names):
    parts, first, row = [], {}, 0
    for k in names:
        first[k] = row
        parts.append(vals[k].reshape(ROWS[k]).astype(F32))
        row += ROWS[k][0]
    if row % 8:
        parts.append(jnp.zeros((-row % 8, ROWS[names[0]][1]), F32))
    return jnp.concatenate(parts, axis=0), first


ORDER = ("meta_tokens", "norm1_g", "w_in", "conv_w", "conv_b", "rg_wa", "rg_ba", "rg_wx", "rg_bx", "rg_lambda",
         "hg_lb_logits", "hg_norm_g", "w_proj_a", "w_proj_b", "w_out", "norm2_g", "w_ffn_in", "w_ffn_down", "norm_f_g")


def kernel(x, meta_tokens, norm1_g, w_in, conv_w, conv_b, rg_wa, rg_ba, rg_wx, rg_bx, rg_lambda, hg_lb_logits, hg_norm_g, w_proj_a, w_proj_b, w_out, norm2_g, w_ffn_in, w_ffn_down, norm_f_g, loss_target, m_meta_tokens, m_norm1_g, m_w_in, m_conv_w, m_conv_b, m_rg_wa, m_rg_ba, m_rg_wx, m_rg_bx, m_rg_lambda, m_hg_lb_logits, m_hg_norm_g, m_w_proj_a, m_w_proj_b, m_w_out, m_norm2_g, m_w_ffn_in, m_w_ffn_down, m_norm_f_g, v_meta_tokens, v_norm1_g, v_w_in, v_conv_w, v_conv_b, v_rg_wa, v_rg_ba, v_rg_wx, v_rg_bx, v_rg_lambda, v_hg_lb_logits, v_hg_norm_g, v_w_proj_a, v_w_proj_b, v_w_out, v_norm2_g, v_w_ffn_in, v_w_ffn_down, v_norm_f_g):
    args = dict(locals())
    w = {k: args[k] for k in ORDER}
    m = {k: args["m_" + k] for k in ORDER}
    v = {k: args["v_" + k] for k in ORDER}
    xi, yi, ci = _coords()
    chip = 2 * xi + yi
    slot = jnp.reshape(chip, (1,)).astype(jnp.int32)
    xyc = jnp.stack([xi, yi, ci]).astype(jnp.int32)

    def full_matrix(k, gth):
        if k == "w_ffn_in":
            return jnp.concatenate([gth[0], gth[2], gth[1], gth[3]], axis=1)
        if k in COL_SHARDED:
            return jnp.concatenate([gth[j] for j in range(4)], axis=1)
        return gth.reshape(-1, gth.shape[2])

    tiny = jnp.zeros((32, 384), F32)
    tiny = tiny.at[0:N_META, 0:256].set(meta_tokens).at[N_META:N_META + 4, 0:320].set(conv_w[0])
    slots = {k: _place_own(w[k][0], slot, BF16, "place_" + k) for k in BIG}
    w_in_all, tiny_all = _run_host(_gather_host([slots["w_in"], _place_own(tiny, slot, F32, "place_tiny")]),
                                   "allgather_w_in")
    later = [k for k in BIG if k != "w_in"]
    meta_full = jnp.transpose(tiny_all[:, 0:N_META, 0:256], (1, 0, 2)).reshape(N_META, D)
    conv_w_full = jnp.transpose(tiny_all[:, N_META:N_META + 4, 0:320], (1, 0, 2)).reshape(4, RGW)

    small = dict(norm1_g=norm1_g, conv_w=conv_w_full, conv_b=conv_b, rg_wa=rg_wa[0], rg_ba=rg_ba, rg_wx=rg_wx[0],
                 rg_bx=rg_bx, rg_lambda=rg_lambda, hg_lb_logits=hg_lb_logits, hg_norm_g=hg_norm_g,
                 norm2_g=norm2_g, norm_f_g=norm_f_g.reshape(1, D))

    hosts = {
        "mm_proj": lambda g: _gather_host([slots[k] for k in later]),
        "mm_d_act": lambda g: _exchange_host(["w_ffn_down"], [g["w_ffn_down"]]),
        "mm_d_hn2": lambda g: _exchange_host(["w_ffn_in"], [g["w_ffn_in"]]),
        "mm_d_win": lambda g: _join_hosts([
            _exchange_host(["w_out", "w_proj_a", "w_proj_b"], [g["w_out"], g["w_proj_a"], g["w_proj_b"]]),
            *[_pack_host(_pack_rows(g, PACKS[p])[0]) for p in EARLY]]),
        "mm_d_hn1": lambda g: _exchange_host(["w_in"], [_unrotate_g_in(g["w_in"])]),
    }
    h0 = jnp.concatenate([jnp.zeros((PAD, D), F32), meta_full, x[0]], axis=0)
    loss_blk, grad_x, g, carried = _local_step(
        h0, loss_target[0], {"w_in": _rotate_w_in(full_matrix("w_in", w_in_all))}, small, hosts=hosts,
        finalize=lambda gathered: {k: full_matrix(k, gth) for k, gth in zip(later, gathered)})
    g["loss"] = jnp.broadcast_to(loss_blk[0:1, 0:1], (1, D))

    halves = {}
    sources = {"mm_d_act": ["w_ffn_down"], "mm_d_hn2": ["w_ffn_in"], "mm_d_win": ["w_out", "w_proj_a", "w_proj_b"],
               "mm_d_hn1": ["w_in"]}
    for name, keys in sources.items():
        partials, received = carried[name]
        for k, part, rec in zip(keys, partials, received):
            halves[k] = _sum_place(rec, part, k, xyc, "sum_" + k)
    g_big = dict(zip(BIG, _run_host(_sibling_host([halves[k] for k in BIG]), "sibling_swap")))
    late, _ = _pack_rows(g, PACKS["late"])
    recv_packs = dict(zip(EARLY, carried["mm_d_win"][1][3:]), late=_allgather_pack(late))

    grad, delta, new_m, new_v, summed = {}, {}, {}, {}, {}
    for pack, names in PACKS.items():
        row, params, extras = 0, [], []
        for k in names:
            if k in NO_UPDATE:
                extras.append((row, ROWS[k][0]))
            else:
                params.append((row,) + tuple(d[k].reshape(ROWS[k]) for d in (w, m, v)))
            row += ROWS[k][0]
        updated, rows_only = _adamw_rows(recv_packs[pack], params, extras, "adamw_" + pack)
        for k, res in zip([k for k in names if k not in NO_UPDATE], updated):
            grad[k], delta[k], new_m[k], new_v[k] = (a.reshape(w[k].shape) for a in res)
        summed.update(zip([k for k in names if k in NO_UPDATE], rows_only))
    loss = summed["loss"][0, 0]
    g_shard = {"meta_tokens": lax.dynamic_slice(summed["meta_tokens"], (0, chip * 256), (N_META, 256)),
               "conv_w": lax.dynamic_slice(summed["conv_w"], (0, chip * 320), (4, 320))}
    for k in BIG + SHARDED_SMALL:
        gk = g_big[k] if k in BIG else g_shard[k]
        wk, mk, vk = (d[k].reshape(gk.shape) for d in (w, m, v))
        dk, mk, vk = _adamw(wk, gk, mk, vk, "adamw_" + k)
        grad[k], delta[k], new_m[k], new_v[k] = (a.reshape(w[k].shape) for a in (gk, dk, mk, vk))

    return (loss, grad_x[None], *[grad[k] for k in ORDER], *[delta[k] for k in ORDER],
            *[new_m[k] for k in ORDER], *[new_v[k] for k in ORDER])
```

```python
import functools

import jax
import jax.numpy as jnp
from jax import lax
from jax.experimental import pallas as pl
from jax.experimental.pallas import tpu as pltpu

F32, BF16 = jnp.float32, jnp.bfloat16
D = 1024
N_META = 16
RGW = 1280
RG_BLOCKS, RG_BLOCK = 16, 80
RG_C = 8.0
HEADS, HD = 8, 128
HGW = HEADS * HD
DFF = 2816
D_IN = 2 * RGW + 6 * D
ROT = 2 * RGW
COL_PAD = 256
D_INP = D_IN + COL_PAD
EPS = 1e-6
CH = 64
TM = 256
ROW0 = TM
PAD = ROW0 - N_META
CT = RGW
RS = 16
NCT = RGW // CT
EXP_CLAMP = 80.0
VMEM_LIMIT = 56 * 1024 * 1024

LR, B1, B2, ADAM_EPS, WD, STEP = 0.001, 0.9, 0.999, 1e-08, 0.01, 10
MESH = pl.DeviceIdType.MESH
ANY = pl.BlockSpec(memory_space=pl.ANY)


def _cparams(sem):
    return pltpu.CompilerParams(dimension_semantics=sem, vmem_limit_bytes=VMEM_LIMIT)


def _pick(n, prefs):
    for p in prefs:
        if n % p == 0:
            return p
    return n


def _sig(x):
    return 0.5 * jnp.tanh(0.5 * x) + 0.5


def _dot(a, b, dims, precision=None):
    return lax.dot_general(a, b, (dims, ((), ())), preferred_element_type=F32, precision=precision)


NN, NT, TN_ = ((1,), (0,)), ((1,), (1,)), ((0,), (0,))


class _Host:
    def __init__(self, ins, out_shapes, aliases, sems, start, mid, finish):
        self.ins, self.out_shapes, self.aliases, self.sems = ins, out_shapes, aliases, sems
        self.start, self.mid, self.finish = start, mid, finish


class _Epi:
    def __init__(self, ins, in_specs, out_shapes, out_specs, fn, sequential=False, split=1):
        self.ins, self.in_specs, self.out_shapes, self.out_specs, self.fn = ins, in_specs, out_shapes, out_specs, fn
        self.sequential = sequential
        self.split = split


def _mm(a, b, mode, out_dtype, name, resid=None, host=None, epi=None, tiles=None, sparse=None, norm_gain=None):
    if mode == "nn":
        (m, kd), n = a.shape, b.shape[1]
    elif mode == "nt":
        (m, kd), n = a.shape, b.shape[0]
    else:
        (kd, m), n = a.shape, b.shape[1]
    if mode == "tn":
        tm = _pick(m, (1024, 1280, 1408, 640, 512, 256, 128))
        tk = _pick(kd, (1408, 768, 512, 256, 128))
    else:
        tm = _pick(m, (768, 512, 640, 256, 128))
        tk = kd if kd <= 2816 else _pick(kd, (1792, 1408, 1024, 512, 256, 128))
    tn = _pick(n, (1792, 1408, 1280, 1024, 512, 256, 128))
    if tiles is not None:
        tm, tn = tiles
    a_map = (lambda i, j, k: (k, i)) if mode == "tn" else (lambda i, j, k: (i, k))
    b_map = (lambda i, j, k: (j, k)) if mode == "nt" else (lambda i, j, k: (k, j))
    o_map = lambda i, j, k: (i, j)
    if sparse is not None:
        (ni, nj, nk), (tm, tn, tk), (a_map, b_map, o_map) = sparse
    else:
        assert m % tm == 0 and n % tn == 0 and kd % tk == 0, (name, m, n, kd, tm, tn, tk)
        ni, nj, nk = m // tm, n // tn, kd // tk
    dims = {"nn": NN, "nt": NT, "tn": TN_}[mode]

    n_hin = len(host.ins) if host else 0
    n_hout = len(host.out_shapes) if host else 0
    n_res = 0 if resid is None else 1
    n_pro = 0 if norm_gain is None else 1
    n_ein = len(epi.ins) if epi else 0
    n_out = len(epi.out_shapes) if epi else 1
    assert not n_pro or (mode == "nn" and nk == 1)

    def finish(r, r_ref, e_in, o_refs, i, rows=slice(None), first=True):
        if resid is not None:
            r = r + r_ref[rows, :]
        if epi:
            epi.fn(r, e_in, o_refs, i, rows, first)
        else:
            o_refs[0][rows, :] = r.astype(out_dtype)

    def body(*refs):
        a_ref, b_ref = refs[:2]
        r_ref = refs[2] if resid is not None else None
        pos = 2 + n_res
        g_ref = refs[pos] if n_pro else None
        pos += n_pro
        e_in = refs[pos:pos + n_ein]
        pos += n_ein
        h_in = refs[pos:pos + n_hin]
        pos += n_hin
        o_ref = refs[pos:pos + n_out]
        pos += n_out
        hn_ref = refs[pos] if n_pro else None
        pos += n_pro
        h_out = refs[pos:pos + n_hout]
        scratch = refs[pos + n_hout:]
        n_acc = 1 if nk > 1 else 0
        h_sems = scratch[n_acc + n_pro:]
        i, j, k = pl.program_id(0), pl.program_id(1), pl.program_id(2)
        if host:
            @pl.when((i == 0) & (j == 0) & (k == 0))
            def _():
                host.start(h_in, h_out, h_sems)

            if host.mid is not None:
                @pl.when((i == (3 * ni) // 4) & (j == 0) & (k == 0))
                def _():
                    host.mid(h_in, h_out, h_sems)

        if n_pro:
            a_s = scratch[n_acc]

            @pl.when(j == 0)
            def _():
                xh, _ = _rms(a_ref[...])
                a_s[...] = (xh * g_ref[...]).astype(BF16)
                hn_ref[...] = a_s[...]

            a_val = a_s[...]
        else:
            a_val = a_ref[...]

        if nk == 1 and epi and epi.split > 1 and mode != "tn":
            part = tm // epi.split
            for p in range(epi.split):
                rows = slice(p * part, (p + 1) * part)
                finish(_dot(a_val[rows, :], b_ref[...], dims), r_ref, e_in, o_ref, i, rows, p == 0)
        elif nk == 1:
            finish(_dot(a_val, b_ref[...], dims), r_ref, e_in, o_ref, i)
        else:
            acc = scratch[0]

            @pl.when(k == 0)
            def _():
                acc[...] = jnp.zeros_like(acc)

            acc[...] += _dot(a_val, b_ref[...], dims)

            @pl.when(k == nk - 1)
            def _():
                finish(acc[...], r_ref, e_in, o_ref, i)

        if host:
            @pl.when((i == ni - 1) & (j == nj - 1) & (k == nk - 1))
            def _():
                host.finish(h_in, h_out, h_sems)

    a_spec = pl.BlockSpec((tk, tm) if mode == "tn" else (tm, tk), a_map)
    b_spec = pl.BlockSpec((tn, tk) if mode == "nt" else (tk, tn), b_map)
    o_spec = pl.BlockSpec((tm, tn), o_map)
    in_specs, args = [a_spec, b_spec], [a, b]
    if resid is not None:
        in_specs.append(o_spec)
        args.append(resid)
    if n_pro:
        in_specs.append(pl.BlockSpec((1, kd), lambda i, j, k: (0, 0)))
        args.append(norm_gain)
    out_shapes, out_specs = [jax.ShapeDtypeStruct((m, n), out_dtype)], [o_spec]
    if epi:
        in_specs += list(epi.in_specs)
        args += list(epi.ins)
        out_shapes, out_specs = list(epi.out_shapes), list(epi.out_specs)
    if n_pro:
        out_shapes.append(jax.ShapeDtypeStruct((m, kd), BF16))
        out_specs.append(pl.BlockSpec((tm, kd), lambda i, j, k: (i, 0)))
    scratch = ([pltpu.VMEM((tm, tn), F32)] if nk > 1 else []) + ([pltpu.VMEM((tm, kd), BF16)] if n_pro else [])
    sequential = host or n_pro or (epi and epi.sequential)
    outs = pl.pallas_call(
        body, name=name, grid=(ni, nj, nk),
        in_specs=in_specs + [ANY] * n_hin, out_specs=out_specs + [ANY] * n_hout,
        out_shape=out_shapes + (list(host.out_shapes) if host else []),
        scratch_shapes=scratch + (list(host.sems) if host else []),
        input_output_aliases=({2 + n_res + n_pro + n_ein + i_in: n_out + n_pro + i_out
                               for i_in, i_out in host.aliases.items()} if host else {}),
        compiler_params=_cparams(("arbitrary",) * 3 if sequential else ("parallel", "parallel", "arbitrary")),
    )(*args, *(host.ins if host else []))
    main = list(outs[:n_out]) if epi else outs[0]
    if n_pro:
        main = (main, outs[n_out])
    return (main, list(outs[n_out + n_pro:])) if host else main


def _rms(x):
    r = lax.rsqrt(jnp.mean(x * x, axis=-1, keepdims=True) + EPS)
    return x * r, r


def _rms_bwd_math(dhn, x, g):
    xh, r = _rms(x)
    dxh = dhn * g
    dx = r * (dxh - xh * jnp.mean(dxh * xh, axis=-1, keepdims=True))
    return dx, jnp.sum(dhn * xh, axis=0, keepdims=True)


def _row_specs(tm):
    return pl.BlockSpec((tm, D), lambda i, j, k: (i, 0)), pl.BlockSpec((1, D), lambda i, j, k: (0, 0))


def _final_loss_epi(gf, target, t):
    def fn(x, ins, outs, i, rows, first):
        g_ref, t_ref = ins
        dh_ref, dhb_ref, loss_ref, dg_ref = outs

        if first:
            @pl.when(i == 0)
            def _():
                loss_ref[...] = jnp.zeros_like(loss_ref)
                dg_ref[...] = jnp.zeros_like(dg_ref)

        g = g_ref[...]
        xh, _ = _rms(x)
        err = jnp.where(i > 0, xh * g - t_ref[rows, :], 0.0)
        loss_ref[...] += 0.5 * jnp.sum(jnp.sum(err * err, axis=-1, keepdims=True) * (1.0 / D))
        dx, dg = _rms_bwd_math(err * (1.0 / D), x, g)
        dh_ref[rows, :] = dx
        dhb_ref[rows, :] = dx.astype(BF16)
        dg_ref[...] += dg

    row, vec = _row_specs(TM)
    return _Epi(ins=[gf, target], in_specs=[vec, pl.BlockSpec((TM, D), lambda i, j, k: (jnp.maximum(i - 1, 0), 0))],
                out_shapes=[jax.ShapeDtypeStruct((t, D), F32), jax.ShapeDtypeStruct((t, D), BF16),
                            jax.ShapeDtypeStruct((8, 128), F32), jax.ShapeDtypeStruct((1, D), F32)],
                out_specs=[row, row, pl.BlockSpec((8, 128), lambda i, j, k: (0, 0)), vec], fn=fn, sequential=True)


def _rms_bwd_epi(h, g, dres, tm):
    t = h.shape[0]

    def fn(dhn, ins, outs, i, rows, first):
        h_ref, g_ref, dres_ref = ins
        dh_ref, dhb_ref, dg_ref = outs

        if first:
            @pl.when(i == 0)
            def _():
                dg_ref[...] = jnp.zeros_like(dg_ref)

        dx, dg = _rms_bwd_math(dhn, h_ref[rows, :], g_ref[...])
        dh = dres_ref[rows, :] + dx
        dh_ref[rows, :] = dh
        dhb_ref[rows, :] = dh.astype(BF16)
        dg_ref[...] += dg

    row, vec = _row_specs(tm)
    return _Epi(ins=[h, g, dres], in_specs=[row, vec, row],
                out_shapes=[jax.ShapeDtypeStruct((t, D), F32), jax.ShapeDtypeStruct((t, D), BF16),
                            jax.ShapeDtypeStruct((1, D), F32)],
                out_specs=[row, row, vec], fn=fn, sequential=True)


def _rmsnorm1_bwd(dhn, h, g, dres):
    t = h.shape[0]

    def body(dhn_ref, h_ref, g_ref, dres_ref, dx_ref, dmeta_ref, dg_ref):
        i = pl.program_id(0)

        @pl.when(i == 0)
        def _():
            dg_ref[...] = jnp.zeros_like(dg_ref)

        dx, dg = _rms_bwd_math(dhn_ref[...].astype(F32), h_ref[...], g_ref[...])
        dh = dres_ref[...] + dx
        dg_ref[...] += dg
        dx_ref[...] = dh

        @pl.when(i == 0)
        def _():
            dmeta_ref[...] = dh[PAD:, :]

    row = pl.BlockSpec((TM, D), lambda i: (i, 0))
    vec = pl.BlockSpec((1, D), lambda i: (0, 0))
    return pl.pallas_call(
        body, name="rmsnorm1_bwd", grid=(t // TM,),
        in_specs=[row, row, vec, row],
        out_specs=[pl.BlockSpec((TM, D), lambda i: (jnp.maximum(i - 1, 0), 0)),
                   pl.BlockSpec((N_META, D), lambda i: (0, 0)), vec],
        out_shape=[jax.ShapeDtypeStruct((t - ROW0, D), F32), jax.ShapeDtypeStruct((N_META, D), F32),
                   jax.ShapeDtypeStruct((1, D), F32)],
        compiler_params=_cparams(("arbitrary",)),
    )(dhn, h, g, dres)


FH = DFF // 2
FM = 384


def _swiglu_epi(t):
    def fn(r, _, outs, i, rows, first):
        g, u = r[:, :FH], r[:, FH:]
        outs[0][rows, :] = r.astype(BF16)
        outs[1][rows, :] = (g * _sig(g) * u).astype(BF16)

    return _Epi(ins=[], in_specs=[],
                out_shapes=[jax.ShapeDtypeStruct((t, 2 * DFF), BF16), jax.ShapeDtypeStruct((t, DFF), BF16)],
                out_specs=[pl.BlockSpec((FM, 2 * FH), lambda i, j, k: (i, j)), pl.BlockSpec((FM, FH), lambda i, j, k: (i, j))],
                fn=fn)


def _swiglu_bwd_epi(gu):
    def fn(d, ins, outs, i, rows, first):
        gu_t = ins[0][rows, :].astype(F32)
        g, u = gu_t[:, :FH], gu_t[:, FH:]
        s = _sig(g)
        outs[0][rows, :FH] = (d * u * (s * (1.0 + g * (1.0 - s)))).astype(BF16)
        outs[0][rows, FH:] = (d * (g * s)).astype(BF16)

    spec = pl.BlockSpec((FM, 2 * FH), lambda i, j, k: (i, j))
    return _Epi(ins=[gu], in_specs=[spec], out_shapes=[jax.ShapeDtypeStruct(gu.shape, BF16)], out_specs=[spec], fn=fn)


COL_GA, COL_GB = 4, 5
COL_AX, COL_AG = (6 * D + COL_PAD) // CT, (6 * D + COL_PAD + RGW) // CT


def _merge_fwd(pa, pb, proj):
    t = pa.shape[0]

    def body(pa_ref, pb_ref, ga_ref, gb_ref, o_ref):
        ga, gb, pa, pb = (r[...].astype(F32) for r in (ga_ref, gb_ref, pa_ref, pb_ref))
        o_ref[...] = (_sig(ga) * pa + _sig(gb) * pb).astype(BF16)

    row = pl.BlockSpec((TM, D), lambda i: (i, 0))
    return pl.pallas_call(
        body, name="merge_fwd", grid=(t // TM,),
        in_specs=[row, row, pl.BlockSpec((TM, D), lambda i: (i, COL_GA)), pl.BlockSpec((TM, D), lambda i: (i, COL_GB))],
        out_specs=row, out_shape=jax.ShapeDtypeStruct((t, D), BF16),
        compiler_params=_cparams(("parallel",)),
    )(pa, pb, proj, proj)


def _merge_bwd_epi(pa, pb, proj, tm):
    t = pa.shape[0]

    def fn(d, ins, outs, i, rows, first):
        pa, pb, ga, gb = (r[rows, :].astype(F32) for r in ins)
        dpa_ref, dpb_ref, dproj_ref = outs
        sa, sb = _sig(ga), _sig(gb)
        dpa_ref[rows, :] = (sa * d).astype(BF16)
        dpb_ref[rows, :] = (sb * d).astype(BF16)
        dproj_ref[rows, :D] = (d * pa * sa * (1.0 - sa)).astype(BF16)
        dproj_ref[rows, D:] = (d * pb * sb * (1.0 - sb)).astype(BF16)

    row, _ = _row_specs(tm)
    return _Epi(ins=[pa, pb, proj, proj],
                in_specs=[row, row, pl.BlockSpec((tm, D), lambda i, j, k: (i, COL_GA)),
                          pl.BlockSpec((tm, D), lambda i, j, k: (i, COL_GB))],
                out_shapes=[jax.ShapeDtypeStruct((t, D), BF16), jax.ShapeDtypeStruct((t, D), BF16),
                            jax.ShapeDtypeStruct((t, D_INP), BF16)],
                out_specs=[row, row, pl.BlockSpec((tm, 2 * D), lambda i, j, k: (i, 2))], fn=fn)


def _zero_pad_cols(dproj):
    t = dproj.shape[0]
    tz = _pick(t, (768, 256))

    def body(_, o_ref):
        o_ref[...] = jnp.zeros_like(o_ref)

    return pl.pallas_call(
        body, name="dproj_pad", grid=(t // tz,), in_specs=[ANY],
        out_specs=pl.BlockSpec((tz, COL_PAD), lambda i: (i, (6 * D) // COL_PAD)),
        out_shape=jax.ShapeDtypeStruct(dproj.shape, BF16), input_output_aliases={0: 0},
        compiler_params=_cparams(("parallel",)),
    )(dproj)


HALO = 16


def _rows_before(cur, before, s):
    n = cur.shape[0]
    return jnp.concatenate([before, cur], axis=0)[n - s:2 * n - s, :]


def _rows_after(cur, after, s):
    n = cur.shape[0]
    return jnp.concatenate([cur, after], axis=0)[s:n + s, :]


def _prev_halo(col0):
    return pl.BlockSpec((HALO, CT), lambda c, i: (jnp.maximum(i * (TM // HALO) - 1, 0), col0 + c))


def _conv_fwd(proj, w, b):
    t = proj.shape[0]

    def body(x_ref, halo_ref, w_ref, b_ref, o_ref, ob_ref):
        i = pl.program_id(1)
        wv, bv = w_ref[...], b_ref[...]
        first = jnp.where(i > 0, halo_ref[...].astype(F32), 0.0)

        def strip(k, before):
            rows = pl.ds(pl.multiple_of(k * HALO, HALO), HALO)
            x = x_ref[rows, :].astype(F32)
            y = bv + wv[3:4, :] * x
            for j in range(3):
                y = y + wv[j:j + 1, :] * _rows_before(x, before, 3 - j)
            o_ref[rows, :] = y
            ob_ref[rows, :] = y.astype(BF16)
            return x

        lax.fori_loop(0, TM // HALO, strip, first)

    blk = pl.BlockSpec((TM, CT), lambda c, i: (i, c))
    return pl.pallas_call(
        body, name="conv_fwd", grid=(NCT, t // TM),
        in_specs=[pl.BlockSpec((TM, CT), lambda c, i: (i, COL_AX + c)), _prev_halo(COL_AX),
                  pl.BlockSpec((4, CT), lambda c, i: (0, c)), pl.BlockSpec((1, CT), lambda c, i: (0, c))],
        out_specs=[blk, blk],
        out_shape=[jax.ShapeDtypeStruct((t, RGW), F32), jax.ShapeDtypeStruct((t, RGW), BF16)],
        compiler_params=_cparams(("parallel", "parallel")),
    )(proj, proj, w, b)


def _conv_bwd(dxc, proj, w, dproj):
    t = proj.shape[0]
    nt = t // TM

    def body(d_ref, dn_ref, x_ref, halo_ref, w_ref, _, dx_ref, dw_ref, db_ref, acc):
        i = pl.program_id(1)

        @pl.when(i == 0)
        def _():
            dw_ref[...] = jnp.zeros_like(dw_ref)
            db_ref[...] = jnp.zeros_like(db_ref)

        acc[...] = jnp.zeros_like(acc)
        wv = w_ref[...]
        first = jnp.where(i > 0, halo_ref[...].astype(F32), 0.0)
        last = jnp.where(i < nt - 1, dn_ref[...], 0.0)
        ns = TM // HALO

        def fold(v):
            return v[0:8, :] + v[8:16, :]

        def strip(k, before):
            off = pl.multiple_of(k * HALO, HALO)
            rows = pl.ds(off, HALO)
            x, d = x_ref[rows, :].astype(F32), d_ref[rows, :]
            nxt = d_ref[pl.ds(pl.multiple_of(jnp.minimum(off + HALO, TM - HALO), HALO), HALO), :]
            after = jnp.where(k == ns - 1, last, nxt)
            dx = wv[3:4, :] * d
            for j in range(3):
                dx = dx + wv[j:j + 1, :] * _rows_after(d, after, 3 - j)
                acc[j] += fold(d * _rows_before(x, before, 3 - j))
            acc[3] += fold(d * x)
            acc[4] += fold(d)
            dx_ref[rows, :] = dx.astype(BF16)
            return x

        lax.fori_loop(0, ns, strip, first)
        for j in range(4):
            dw_ref[j:j + 1, :] += jnp.sum(acc[j], axis=0, keepdims=True)
        db_ref[...] += jnp.sum(acc[4], axis=0, keepdims=True)

    return pl.pallas_call(
        body, name="conv_bwd", grid=(NCT, nt),
        in_specs=[pl.BlockSpec((TM, CT), lambda c, i: (i, c)),
                  pl.BlockSpec((HALO, CT), lambda c, i: (jnp.minimum((i + 1) * (TM // HALO), t // HALO - 1), c)),
                  pl.BlockSpec((TM, CT), lambda c, i: (i, COL_AX + c)), _prev_halo(COL_AX),
                  pl.BlockSpec((4, CT), lambda c, i: (0, c)), ANY],
        out_specs=[pl.BlockSpec((TM, CT), lambda c, i: (i, COL_AX + c)),
                   pl.BlockSpec((4, CT), lambda c, i: (0, c)), pl.BlockSpec((1, CT), lambda c, i: (0, c))],
        out_shape=[jax.ShapeDtypeStruct((t, D_INP), BF16), jax.ShapeDtypeStruct((4, RGW), F32),
                   jax.ShapeDtypeStruct((1, RGW), F32)],
        scratch_shapes=[pltpu.VMEM((5, 8, CT), F32)],
        input_output_aliases={5: 0},
        compiler_params=_cparams(("parallel", "arbitrary")),
    )(dxc, dxc, proj, proj, w, dproj)


def _gelu(x):
    c = 0.7978845608028654
    th = jnp.tanh(c * (x + 0.044715 * x * x * x))
    return 0.5 * x * (1.0 + th), th


def _rg_gates(gr, gi, xc, ba, bx, lam, row0):
    r = _sig(gr + ba)
    ig = _sig(gi + bx)
    sp = jax.nn.softplus(-lam)
    a = jnp.exp(-RG_C * r * sp)
    s2 = jnp.maximum(1.0 - a * a, 1e-30)
    rs = lax.rsqrt(s2)
    s = s2 * rs
    rows = row0 + lax.broadcasted_iota(jnp.int32, gr.shape, 0)
    live = rows >= PAD
    u = jnp.where(live, s * ig * xc, 0.0)
    return r, ig, sp, a, s, rs, u, live


def _rg_fwd(gri, xc, proj, ba, bx, lam):
    t = gri.shape[0]

    def body(gr_ref, gi_ref, xc_ref, ag_ref, ba_ref, bx_ref, lam_ref, h_ref, ya_ref, hc):
        i = pl.program_id(1)

        @pl.when(i == 0)
        def _():
            hc[...] = jnp.zeros_like(hc)

        ba, bx, lam = ba_ref[...], bx_ref[...], lam_ref[...]
        sub = lax.broadcasted_iota(jnp.int32, (8, CT), 0)

        def strip(k, h):
            off = pl.multiple_of(k * RS, RS)
            rows = pl.ds(off, RS)
            _, _, _, a, _, _, u, _ = _rg_gates(gr_ref[rows, :], gi_ref[rows, :], xc_ref[rows, :], ba, bx, lam,
                                               i * TM + off)
            outs = []
            for half in range(RS // 8):
                out = jnp.zeros((8, CT), F32)
                for r in range(8):
                    h = a[8 * half + r:8 * half + r + 1, :] * h + u[8 * half + r:8 * half + r + 1, :]
                    out = jnp.where(sub == r, h, out)
                outs.append(out)
            hs = jnp.concatenate(outs, axis=0)
            h_ref[rows, :] = hs
            ge, _ = _gelu(ag_ref[rows, :].astype(F32))
            ya_ref[rows, :] = (hs * ge).astype(BF16)
            return h

        hc[...] = lax.fori_loop(0, TM // RS, strip, hc[...])

    blk_ = pl.BlockSpec((TM, CT), lambda c, i: (i, c))
    vec = pl.BlockSpec((1, CT), lambda c, i: (0, c))
    return pl.pallas_call(
        body, name="rg_fwd", grid=(NCT, t // TM),
        in_specs=[blk_, pl.BlockSpec((TM, CT), lambda c, i: (i, NCT + c)), blk_,
                  pl.BlockSpec((TM, CT), lambda c, i: (i, COL_AG + c)), vec, vec, vec],
        out_specs=[blk_, blk_],
        out_shape=[jax.ShapeDtypeStruct((t, RGW), F32), jax.ShapeDtypeStruct((t, RGW), BF16)],
        scratch_shapes=[pltpu.VMEM((1, CT), F32)],
        compiler_params=_cparams(("parallel", "arbitrary")),
    )(gri, gri, xc, proj, ba, bx, lam)


def _rg_bwd(dya, h, gri, xc, proj, ba, bx, lam, dproj):
    t = gri.shape[0]
    nt = t // TM
    assert NCT == 1

    def body(dya_ref, h_ref, hh_ref, gr_ref, gi_ref, xc_ref, ag_ref, ba_ref, bx_ref, lam_ref, _,
             dgri_ref, dxc_ref, dag_ref, dba_ref, dbx_ref, dlam_ref, hbuf, acc, cc):
        i = pl.program_id(1)
        ri = nt - 1 - i

        @pl.when(i == 0)
        def _():
            cc[...] = jnp.zeros_like(cc)
            dba_ref[...] = jnp.zeros_like(dba_ref)
            dbx_ref[...] = jnp.zeros_like(dbx_ref)
            dlam_ref[...] = jnp.zeros_like(dlam_ref)

        acc[...] = jnp.zeros_like(acc)
        ba, bx, lam = ba_ref[...], bx_ref[...], lam_ref[...]
        halo_last = jnp.where(ri > 0, hh_ref[7:8, :], 0.0)
        sub = lax.broadcasted_iota(jnp.int32, (8, CT), 0)
        c0 = 0.7978845608028654

        def strip(kk, c):
            k = TM // RS - 1 - kk
            off = pl.multiple_of(k * RS, RS)
            rows = pl.ds(off, RS)
            xc, hv = xc_ref[rows, :], h_ref[rows, :]
            ag, dya = ag_ref[rows, :].astype(F32), dya_ref[rows, :].astype(F32)
            r, ig, sp, a, s, rs, _, live = _rg_gates(gr_ref[rows, :], gi_ref[rows, :], xc, ba, bx, lam, ri * TM + off)
            ge, th = _gelu(ag)
            dge = 0.5 * (1.0 + th) + 0.5 * ag * (1.0 - th * th) * c0 * (1.0 + 3.0 * 0.044715 * ag * ag)
            dag_ref[rows, :] = (dya * hv * dge).astype(BF16)
            d = dya * ge
            outs = []
            for half in range(RS // 8 - 1, -1, -1):
                out = jnp.zeros((8, CT), F32)
                for rr in range(7, -1, -1):
                    g = d[8 * half + rr:8 * half + rr + 1, :] + c
                    c = a[8 * half + rr:8 * half + rr + 1, :] * g
                    out = jnp.where(sub == rr, g, out)
                outs.insert(0, out)
            g = jnp.concatenate(outs, axis=0)
            before = h_ref[pl.ds(pl.multiple_of(jnp.maximum(off - RS, 0), RS), RS), :]
            hbuf[7:8, :] = jnp.where(k == 0, halo_last, before[RS - 1:RS, :])
            hbuf[8:, :] = hv
            hprev = hbuf[pl.ds(7, RS), :]
            du = jnp.where(live, g, 0.0)
            ds = du * (ig * xc)
            dm = du * s
            dla = (g * hprev - a * ds * rs) * a
            dr = dla * (-RG_C * sp) * r * (1.0 - r)
            di = dm * xc * ig * (1.0 - ig)
            dgri_ref[rows, :CT] = dr.astype(BF16)
            dgri_ref[rows, CT:] = di.astype(BF16)
            dxc_ref[rows, :] = dm * ig
            dl = dla * (-RG_C * r)
            for half in range(RS // 8):
                part = slice(8 * half, 8 * half + 8)
                acc[0] += dr[part]
                acc[1] += di[part]
                acc[2] += dl[part]
            return c

        cc[...] = lax.fori_loop(0, TM // RS, strip, cc[...])
        dba_ref[...] += jnp.sum(acc[0], axis=0, keepdims=True)
        dbx_ref[...] += jnp.sum(acc[1], axis=0, keepdims=True)
        dlam_ref[...] += jnp.sum(acc[2], axis=0, keepdims=True) * (-_sig(-lam))

    rblk = pl.BlockSpec((TM, CT), lambda c, i: (nt - 1 - i, c))
    vec = pl.BlockSpec((1, CT), lambda c, i: (0, c))
    hh = pl.BlockSpec((8, CT), lambda c, i: (jnp.maximum((nt - 1 - i) * (TM // 8) - 1, 0), c))
    agb = pl.BlockSpec((TM, CT), lambda c, i: (nt - 1 - i, COL_AG + c))
    return pl.pallas_call(
        body, name="rg_bwd", grid=(NCT, nt),
        in_specs=[rblk, rblk, hh, rblk, pl.BlockSpec((TM, CT), lambda c, i: (nt - 1 - i, NCT + c)), rblk, agb,
                  vec, vec, vec, ANY],
        out_specs=[pl.BlockSpec((TM, 2 * CT), lambda c, i: (nt - 1 - i, c)), rblk, agb, vec, vec, vec],
        out_shape=[jax.ShapeDtypeStruct((t, 2 * RGW), BF16),
                   jax.ShapeDtypeStruct((t, RGW), F32), jax.ShapeDtypeStruct((t, D_INP), BF16),
                   jax.ShapeDtypeStruct((1, RGW), F32), jax.ShapeDtypeStruct((1, RGW), F32),
                   jax.ShapeDtypeStruct((1, RGW), F32)],
        scratch_shapes=[pltpu.VMEM((RS + 8, CT), F32), pltpu.VMEM((3, 8, CT), F32), pltpu.VMEM((1, CT), F32)],
        input_output_aliases={10: 2},
        compiler_params=_cparams(("parallel", "arbitrary")),
    )(dya, h, h, gri, gri, xc, proj, ba, bx, lam, dproj)


NCH = TM // CH
HI = lax.Precision.HIGHEST


def _tri_dot(tri, x):
    hi = x.astype(BF16)
    r1 = x - hi.astype(F32)
    mid = r1.astype(BF16)
    lo = (r1 - mid.astype(F32)).astype(BF16)
    return _dot(tri, hi, NN) + _dot(tri, mid, NN) + _dot(tri, lo, NN)


def _hg_chunk(qr, fr, lb):
    sf = _sig(fr)
    fg = lb + (1.0 - lb) * sf
    k = (1.0 - lb) * (1.0 - sf)
    sq = _sig(qr)
    q = qr * sq
    ri = lax.broadcasted_iota(jnp.int32, (CH, CH), 0)
    ci = lax.broadcasted_iota(jnp.int32, (CH, CH), 1)
    b = _tri_dot((ri >= ci).astype(BF16), jnp.log(fg))
    bm, bl = b[CH // 2 - 1:CH // 2, :], b[CH - 1:CH, :]
    ebm = jnp.exp(jnp.minimum(b - bm, EXP_CLAMP))
    ekm = jnp.exp(jnp.minimum(bm - b, EXP_CLAMP))
    eb = ebm * jnp.exp(bm)
    ekl = ekm * jnp.exp(bl - bm)
    return dict(sf=sf, fg=fg, k=k, sq=sq, q=q, eb=eb, ebm=ebm, ekm=ekm, ekl=ekl, ebl=jnp.exp(bl),
                qe=q * eb, qh=q * ebm, kh=k * ekm, kd=k * ekl, causal=ri >= ci, upper=(ci >= ri).astype(BF16))


def _hgrn_fwd(proj, lbl, gn):
    t = proj.shape[0]
    nt = t // TM

    def body(q_ref, f_ref, v_ref, g_ref, lbl_ref, gn_ref, yb_ref, o_ref, st_ref, st):
        @pl.when(pl.program_id(0) == 0)
        def _():
            st[...] = jnp.zeros_like(st)

        l = lbl_ref[...]
        lb = _sig(l[0:1, :] - l[1:2, :])
        gnv = gn_ref[...]

        def chunk(c, carry):
            off = pl.multiple_of(c * CH, CH)
            rows = pl.ds(off, CH)
            z = _hg_chunk(q_ref[rows, :].astype(F32), f_ref[rows, :].astype(F32), lb)
            v, gg = v_ref[rows, :], g_ref[rows, :].astype(F32)
            for hh in range(HEADS):
                sl = slice(hh * HD, (hh + 1) * HD)
                s_prev = st[hh]
                st_ref[c, hh] = s_prev
                vb = v[:, sl].astype(BF16)
                att = jnp.where(z["causal"], _dot(z["qh"][:, sl].astype(BF16), z["kh"][:, sl].astype(BF16), NT), 0.0)
                o = _dot(z["qe"][:, sl].astype(BF16), s_prev.astype(BF16), NT) + _dot(att.astype(BF16), vb, NN)
                st[hh] = s_prev * z["ebl"][:, sl] + _dot(vb, z["kd"][:, sl].astype(BF16), TN_)
                xh, _ = _rms(o)
                gh = gg[:, sl]
                o_ref[rows, sl] = o
                yb_ref[rows, sl] = (xh * gnv * gh * _sig(gh)).astype(BF16)
            return carry

        lax.fori_loop(0, NCH, chunk, 0, unroll=True)

    def col(j):
        return pl.BlockSpec((TM, HGW), lambda i, j=j: (i, j))

    return pl.pallas_call(
        body, name="hgrn_fwd", grid=(nt,),
        in_specs=[col(0), col(1), col(2), col(3), pl.BlockSpec((2, HGW), lambda i: (0, 0)),
                  pl.BlockSpec((1, HD), lambda i: (0, 0))],
        out_specs=[col(0), col(0), pl.BlockSpec((NCH, HEADS, HD, HD), lambda i: (i, 0, 0, 0))],
        out_shape=[jax.ShapeDtypeStruct((t, HGW), BF16), jax.ShapeDtypeStruct((t, HGW), F32),
                   jax.ShapeDtypeStruct((t // CH, HEADS, HD, HD), F32)],
        scratch_shapes=[pltpu.VMEM((HEADS, HD, HD), F32)],
        compiler_params=_cparams(("arbitrary",)),
    )(proj, proj, proj, proj, lbl, gn)


def _hgrn_bwd(dyb, proj, o, states, lbl, gn, dproj):
    t = proj.shape[0]
    nt = t // TM

    def body(dy_ref, q_ref, f_ref, v_ref, g_ref, o_ref, st_ref, lbl_ref, gn_ref, _,
             dp_ref, dgn_ref, dl_ref, dst, dlb):
        i = pl.program_id(0)

        @pl.when(i == 0)
        def _():
            dst[...] = jnp.zeros_like(dst)
            dlb[...] = jnp.zeros_like(dlb)
            dgn_ref[...] = jnp.zeros_like(dgn_ref)

        l = lbl_ref[...]
        lb = _sig(l[0:1, :] - l[1:2, :])
        gnv = gn_ref[...]
        last = lax.broadcasted_iota(jnp.int32, (CH, HD), 0) == CH - 1

        def chunk(cc, carry):
            c = NCH - 1 - cc
            off = pl.multiple_of(c * CH, CH)
            rows = pl.ds(off, CH)
            qr, fr = q_ref[rows, :].astype(F32), f_ref[rows, :].astype(F32)
            z = _hg_chunk(qr, fr, lb)
            v, gg, ov, dy = v_ref[rows, :], g_ref[rows, :].astype(F32), o_ref[rows, :], dy_ref[rows, :].astype(F32)
            dqs, dks, dbs, dvs, dgs = [], [], [], [], []
            dgn = jnp.zeros((1, HD), F32)
            for hh in range(HEADS):
                sl = slice(hh * HD, (hh + 1) * HD)
                s_prev, ds_new = st_ref[c, hh], dst[hh]
                qe, qh, kh, kd = z["qe"][:, sl], z["qh"][:, sl], z["kh"][:, sl], z["kd"][:, sl]
                ebl = z["ebl"][:, sl]
                gh, dyh = gg[:, sl], dy[:, sl]
                xh, rr = _rms(ov[:, sl])
                sg = _sig(gh)
                dyn = dyh * (gh * sg)
                dgs.append(dyh * (xh * gnv) * (sg * (1.0 + gh * (1.0 - sg))))
                dgn = dgn + jnp.sum(dyn * xh, axis=0, keepdims=True)
                dxh = dyn * gnv
                do = (rr * (dxh - xh * jnp.mean(dxh * xh, axis=-1, keepdims=True))).astype(BF16)
                vb, dsb = v[:, sl].astype(BF16), ds_new.astype(BF16)
                qeb, qhb, khb, kdb = (a.astype(BF16) for a in (qe, qh, kh, kd))
                att = jnp.where(z["causal"], _dot(qhb, khb, NT), 0.0).astype(BF16)
                datt = jnp.where(z["causal"], _dot(do, vb, NT), 0.0).astype(BF16)
                dvs.append(_dot(att, do, TN_) + _dot(kdb, dsb, NT))
                dqe = _dot(do, s_prev.astype(BF16), NN)
                dqh = _dot(datt, khb, NN)
                dkh = _dot(datt, qhb, TN_)
                dkd = _dot(vb, dsb, NN)
                qe, qh, kh, kd = (a.astype(F32) for a in (qeb, qhb, khb, kdb))
                dbl = (jnp.sum(dkd * kd, axis=0, keepdims=True)
                       + jnp.sum(ds_new * s_prev, axis=0, keepdims=True) * ebl)
                dqs.append(dqe * z["eb"][:, sl] + dqh * z["ebm"][:, sl])
                dks.append(dkh * z["ekm"][:, sl] + dkd * z["ekl"][:, sl])
                dbs.append(dqe * qe + dqh * qh - dkh * kh - dkd * kd + jnp.where(last, dbl, 0.0))
                dst[hh] = _dot(do, qeb, TN_) + ds_new * ebl
            dgn_ref[...] += dgn
            dq, dk, db = (jnp.concatenate(x, axis=1) for x in (dqs, dks, dbs))
            dlf = _tri_dot(z["upper"], db)
            sf, fg, sq = z["sf"], z["fg"], z["sq"]
            dmix = dlf / fg - dk
            dsf = dmix * (1.0 - lb)
            dlb[...] += jnp.sum(dmix * (1.0 - sf), axis=0, keepdims=True)
            dp_ref[rows, 0:HGW] = (dq * (sq * (1.0 + qr * (1.0 - sq)))).astype(BF16)
            dp_ref[rows, HGW:2 * HGW] = (dsf * sf * (1.0 - sf)).astype(BF16)
            dp_ref[rows, 2 * HGW:3 * HGW] = jnp.concatenate(dvs, axis=1).astype(BF16)
            dp_ref[rows, 3 * HGW:4 * HGW] = jnp.concatenate(dgs, axis=1).astype(BF16)
            return carry

        lax.fori_loop(0, NCH, chunk, 0, unroll=True)
        dl0 = dlb[...] * lb * (1.0 - lb)
        dl_ref[0:1, :] = dl0
        dl_ref[1:2, :] = -dl0

    def col(j):
        return pl.BlockSpec((TM, HGW), lambda i, j=j: (nt - 1 - i, j))

    return pl.pallas_call(
        body, name="hgrn_bwd", grid=(nt,),
        in_specs=[col(0), col(0), col(1), col(2), col(3), col(0),
                  pl.BlockSpec((NCH, HEADS, HD, HD), lambda i: (nt - 1 - i, 0, 0, 0)),
                  pl.BlockSpec((2, HGW), lambda i: (0, 0)), pl.BlockSpec((1, HD), lambda i: (0, 0)), ANY],
        out_specs=[pl.BlockSpec((TM, 4 * HGW), lambda i: (nt - 1 - i, 0)),
                   pl.BlockSpec((1, HD), lambda i: (0, 0)), pl.BlockSpec((2, HGW), lambda i: (0, 0))],
        out_shape=[jax.ShapeDtypeStruct((t, D_INP), BF16), jax.ShapeDtypeStruct((1, HD), F32),
                   jax.ShapeDtypeStruct((2, HGW), F32)],
        scratch_shapes=[pltpu.VMEM((HEADS, HD, HD), F32), pltpu.VMEM((1, HGW), F32)],
        input_output_aliases={9: 0},
        compiler_params=_cparams(("arbitrary",)),
    )(dyb, proj, proj, proj, proj, o, states, lbl, gn, dproj)


def _rotate_w_in(w):
    return jnp.concatenate([w[:, ROT:], jnp.zeros((w.shape[0], COL_PAD), w.dtype), w[:, :ROT]], axis=1)


def _unrotate_g_in(g):
    return jnp.concatenate([g[:, D_INP - ROT:], g[:, :D_IN - ROT]], axis=1)


SB = RGW // 2


def _bd_gates(t):
    tm = _pick(t, (768, 256))

    def col(j):
        return 2 * (j % 2) + j // 2

    return ((t // tm, 4, 1), (tm, SB, SB),
            (lambda i, j, k: (i, j // 2), lambda i, j, k: (j // 2, col(j)), lambda i, j, k: (i, col(j))))


def _bd_dw(t):
    tk = _pick(t, (1408, 768))
    return (1, 4, t // tk), (SB, SB, tk), (lambda i, j, k: (k, j % 2), lambda i, j, k: (k, j), lambda i, j, k: (j % 2, j))


def _block_diag(w):
    eye = jnp.eye(RG_BLOCKS, dtype=w.dtype)
    return (w[:, :, None, :] * eye[:, None, :, None]).reshape(RGW, RGW)


def _diag_blocks(wd):
    w4 = wd.reshape(RG_BLOCKS, RG_BLOCK, RG_BLOCKS, RG_BLOCK)
    return jnp.stack([w4[n, :, n, :] for n in range(RG_BLOCKS)])


def _local_step(h0, target, wts, small, hosts=None, finalize=None):
    hosts = hosts or {}
    carried = {}
    g = {}
    t = h0.shape[0]

    def mm(a, b, mode, out_dtype, name, **kw):
        if name not in hosts:
            return _mm(a, b, mode, out_dtype, name, **kw)
        host = hosts[name](g)
        out, res = _mm(a, b, mode, out_dtype, name, host=host, **kw)
        carried[name] = (host.ins, res)
        return out

    w_in = wts["w_in"]
    wax_d = jnp.concatenate([_block_diag(small["rg_wa"]), _block_diag(small["rg_wx"])], axis=1).astype(BF16)
    ba, bx, lam = small["rg_ba"], small["rg_bx"], small["rg_lambda"]
    lbl, gn = small["hg_lb_logits"], small["hg_norm_g"]
    conv_w, conv_b = small["conv_w"], small["conv_b"]

    tmm = _pick(t, (768, 256))
    proj, hn1 = mm(h0, w_in, "nn", BF16, "mm_proj", norm_gain=small["norm1_g"])
    if finalize is not None:
        wts = finalize(carried["mm_proj"][1])
    w_pa, w_pb, w_out, w_fi, w_fd = (wts[k] for k in ("w_proj_a", "w_proj_b", "w_out", "w_ffn_in", "w_ffn_down"))
    xc, xcb = _conv_fwd(proj, conv_w, conv_b)
    gri = mm(xcb, wax_d, "nn", F32, "mm_rg_gates", sparse=_bd_gates(t))
    hrg, ya = _rg_fwd(gri, xc, proj, ba, bx, lam)
    yb, o, states = _hgrn_fwd(proj, lbl, gn)
    pa = mm(ya, w_pa, "nn", BF16, "mm_pa")
    pb = mm(yb, w_pb, "nn", BF16, "mm_pb")
    merged = _merge_fwd(pa, pb, proj)
    h1 = mm(merged, w_out, "nn", F32, "mm_out", resid=h0)
    (gu, act), hn2 = mm(h1, w_fi, "nn", BF16, "mm_ffn_in", norm_gain=small["norm2_g"], epi=_swiglu_epi(t),
                        tiles=(FM, 2 * FH))
    dh2, dh2b, loss, g["norm_f_g"] = mm(act, w_fd, "nn", F32, "mm_ffn_down", resid=h1,
                                        epi=_final_loss_epi(small["norm_f_g"], target, t), tiles=(TM, D))

    g["w_ffn_down"] = mm(act, dh2b, "tn", BF16, "mm_d_wfd")
    (dgu,) = mm(dh2b, w_fd, "nt", BF16, "mm_d_act", epi=_swiglu_bwd_epi(gu), tiles=(FM, FH))
    g["w_ffn_in"] = mm(hn2, dgu, "tn", BF16, "mm_d_wfi")
    dh1, dh1b, g["norm2_g"] = mm(dgu, w_fi, "nt", BF16, "mm_d_hn2", epi=_rms_bwd_epi(h1, small["norm2_g"], dh2, tmm),
                                 tiles=(tmm, D))
    g["w_out"] = mm(merged, dh1b, "tn", BF16, "mm_d_wout")
    dpa, dpb, dproj = mm(dh1b, w_out, "nt", BF16, "mm_d_merged", epi=_merge_bwd_epi(pa, pb, proj, TM), tiles=(TM, D))
    dproj = _zero_pad_cols(dproj)
    g["w_proj_a"] = mm(ya, dpa, "tn", BF16, "mm_d_wpa")
    g["w_proj_b"] = mm(yb, dpb, "tn", BF16, "mm_d_wpb")
    dya = mm(dpa, w_pa, "nt", BF16, "mm_d_ya")
    dyb = mm(dpb, w_pb, "nt", BF16, "mm_d_yb")
    dproj, g["hg_norm_g"], g["hg_lb_logits"] = _hgrn_bwd(dyb, proj, o, states, lbl, gn, dproj)
    dgri, dxc, dproj, g["rg_ba"], g["rg_bx"], g["rg_lambda"] = _rg_bwd(dya, hrg, gri, xc, proj, ba, bx, lam, dproj)
    dwax = mm(xcb, dgri, "tn", F32, "mm_d_wax", sparse=_bd_dw(t))
    g["rg_wa"], g["rg_wx"] = _diag_blocks(dwax[:, :RGW]), _diag_blocks(dwax[:, RGW:])
    dxc = mm(dgri, wax_d, "nt", F32, "mm_d_xc", resid=dxc)
    dproj, g["conv_w"], g["conv_b"] = _conv_bwd(dxc, proj, conv_w, dproj)
    g["w_in"] = mm(hn1, dproj, "tn", BF16, "mm_d_win")
    dhn1 = mm(dproj, w_in, "nt", BF16, "mm_d_hn1")
    grad_x, g["meta_tokens"], g["norm1_g"] = _rmsnorm1_bwd(dhn1, h0, small["norm1_g"], dh1)
    return loss, grad_x, g, carried


def _adamw_math(w, g, m, v):
    m = B1 * m + (1.0 - B1) * g
    v = B2 * v + (1.0 - B2) * (g * g)
    m_hat = m / (1.0 - B1 ** STEP)
    v_hat = v / (1.0 - B2 ** STEP)
    return -LR * (m_hat / (jnp.sqrt(v_hat) + ADAM_EPS) + WD * w), m, v


def _adamw(w, g, m, v, name):
    r, c = w.shape
    tr = _pick(r, (256, 352, 320, 128, 64, 32, 16, 8))

    def body(w_ref, g_ref, m_ref, v_ref, d_ref, mo_ref, vo_ref):
        d_ref[...], mo_ref[...], vo_ref[...] = _adamw_math(w_ref[...], g_ref[...], m_ref[...], v_ref[...])

    blk = pl.BlockSpec((tr, c), lambda i: (i, 0))
    return pl.pallas_call(
        body, name=name, grid=(r // tr,), in_specs=[blk] * 4, out_specs=[blk] * 3,
        out_shape=[jax.ShapeDtypeStruct((r, c), F32)] * 3,
        compiler_params=_cparams(("parallel",)),
    )(w, g, m, v)


def _adamw_rows(recv, params, extras, name):
    _, _, width = recv.shape
    n_par = len(params)

    def body(*refs):
        r_ref = refs[0]
        ins = refs[1:1 + 3 * n_par]
        outs = refs[1 + 3 * n_par:]
        g_all = r_ref[0]
        for p in range(1, 8):
            g_all = g_all + r_ref[p]
        for q, (row0, w, _, _) in enumerate(params):
            g = g_all[row0:row0 + w.shape[0], :]
            w_ref, m_ref, v_ref = ins[3 * q:3 * q + 3]
            g_ref, d_ref, mo_ref, vo_ref = outs[4 * q:4 * q + 4]
            g_ref[...] = g
            d_ref[...], mo_ref[...], vo_ref[...] = _adamw_math(w_ref[...], g, m_ref[...], v_ref[...])
        for q, (row0, n) in enumerate(extras):
            outs[4 * n_par + q][...] = g_all[row0:row0 + n, :]

    shapes = [jax.ShapeDtypeStruct(w.shape, F32) for _, w, _, _ in params for _ in range(4)]
    shapes += [jax.ShapeDtypeStruct((n, width), F32) for _, n in extras]
    vm = pl.BlockSpec(memory_space=pltpu.VMEM)
    outs = pl.pallas_call(
        body, name=name, in_specs=[vm] * (1 + 3 * n_par), out_specs=[vm] * len(shapes), out_shape=shapes,
        compiler_params=pltpu.CompilerParams(vmem_limit_bytes=VMEM_LIMIT),
    )(recv, *[a for _, w, m, v in params for a in (w, m, v)])
    return [tuple(outs[4 * q:4 * q + 4]) for q in range(n_par)], list(outs[4 * n_par:])


def _col_block(weight, px, py):
    return 2 * py + px if weight == "w_ffn_in" else 2 * px + py


def _sum_place(recv, gfull, weight, xyc, name):
    _, r, c = recv.shape
    tr = _pick(r, (128, 176, 160, 64, 32, 16))
    nb = r // tr

    def body(_, own_ref, r_ref, o_ref):
        g = own_ref[...].astype(F32)
        for p in range(7):
            g = g + r_ref[p].astype(F32)
        o_ref[...] = g

    if weight in COL_SHARDED:
        own = pl.BlockSpec((tr, c), lambda i, s: (s[2] * nb + i, _col_block(weight, s[0], s[1])))
    else:
        own = pl.BlockSpec((tr, c), lambda i, s: ((4 * s[0] + 2 * s[1] + s[2]) * nb + i, 0))
    return pl.pallas_call(
        body, name=name,
        grid_spec=pltpu.PrefetchScalarGridSpec(
            num_scalar_prefetch=1, grid=(nb,),
            in_specs=[own, pl.BlockSpec((7, tr, c), lambda i, s: (0, i, 0))],
            out_specs=pl.BlockSpec((tr, c), lambda i, s: (s[2] * nb + i, 0))),
        out_shape=jax.ShapeDtypeStruct((2 * r, c), F32),
        compiler_params=_cparams(("arbitrary",)),
    )(xyc, gfull, recv)


def _place_own(shard, slot, dtype, name):
    r, c = shard.shape
    tr = _pick(r, (256, 352, 320, 128, 32))

    def body(_, x_ref, o_ref):
        o_ref[...] = x_ref[...].astype(dtype)

    return pl.pallas_call(
        body, name=name,
        grid_spec=pltpu.PrefetchScalarGridSpec(
            num_scalar_prefetch=1, grid=(r // tr,),
            in_specs=[pl.BlockSpec((tr, c), lambda i, s: (i, 0))],
            out_specs=pl.BlockSpec((None, tr, c), lambda i, s: (s[0], i, 0))),
        out_shape=jax.ShapeDtypeStruct((4, r, c), dtype),
        compiler_params=_cparams(("arbitrary",)),
    )(slot, shard)


BIG = ("w_in", "w_proj_a", "w_proj_b", "w_out", "w_ffn_in", "w_ffn_down")
COL_SHARDED = ("w_in", "w_ffn_in")


def _coords():
    return lax.axis_index("x"), lax.axis_index("y"), lax.axis_index("c")


def _gather_host(bufs):
    n = len(bufs)

    def place():
        x, y, c = _coords()
        return x, y, c, [(1 - x, y), (x, 1 - y), (1 - x, 1 - y)]

    def slab(b, chip, half_of):
        half = b.shape[1] // 2
        return b.at[chip, pl.ds(pl.multiple_of(half_of * half, 8), half)]

    def copy(b, sems, w, k, chip, half_of, to):
        s = slab(b, chip, half_of)
        return pltpu.make_async_remote_copy(src_ref=s, dst_ref=s, send_sem=sems[0].at[w, k], recv_sem=sems[1].at[w, k],
                                            device_id=to, device_id_type=MESH)

    def start(_, outs, sems):
        x, y, c, chips = place()
        for w, b in enumerate(outs):
            for k, (cx, cy) in enumerate(chips):
                copy(b, sems, w, k, 2 * x + y, c, (cx, cy, c)).start()

    def mid(_, outs, sems):
        x, y, c, chips = place()
        for w, b in enumerate(outs):
            for k, (cx, cy) in enumerate(chips):
                copy(b, sems, w, k, 2 * cx + cy, c, (cx, cy, c)).wait_recv()
                copy(b, sems, w, 3 + k, 2 * cx + cy, c, (x, y, 1 - c)).start()

    def finish(_, outs, sems):
        x, y, c, chips = place()
        for w, b in enumerate(outs):
            for k, (cx, cy) in enumerate(chips):
                copy(b, sems, w, 3 + k, 2 * cx + cy, 1 - c, (x, y, 1 - c)).wait_recv()
        for w, b in enumerate(outs):
            for k, (cx, cy) in enumerate(chips):
                copy(b, sems, w, k, 2 * x + y, c, (cx, cy, c)).wait_send()
                copy(b, sems, w, 3 + k, 2 * cx + cy, c, (x, y, 1 - c)).wait_send()

    return _Host(ins=list(bufs), out_shapes=[jax.ShapeDtypeStruct(b.shape, b.dtype) for b in bufs],
                 aliases={i: i for i in range(n)},
                 sems=[pltpu.SemaphoreType.DMA((n, 6)), pltpu.SemaphoreType.DMA((n, 6))],
                 start=start, mid=mid, finish=finish)


def _run_host(host, name):
    n_in, n_out = len(host.ins), len(host.out_shapes)

    def body(*refs):
        ins, outs, sems = refs[:n_in], refs[n_in:n_in + n_out], refs[n_in + n_out:]
        host.start(ins, outs, sems)
        if host.mid is not None:
            host.mid(ins, outs, sems)
        host.finish(ins, outs, sems)

    return pl.pallas_call(
        body, name=name, in_specs=[ANY] * n_in, out_specs=[ANY] * n_out, out_shape=list(host.out_shapes),
        scratch_shapes=list(host.sems), input_output_aliases=dict(host.aliases),
    )(*host.ins)


def _peer(x, y, c, k):
    fx, fy, fc = (k >> 2) & 1, (k >> 1) & 1, k & 1
    return (x ^ fx if fx else x, y ^ fy if fy else y, c ^ fc if fc else c)


def _sub_shape(name, full_shape):
    r, c = full_shape
    return (r // 2, c // 4) if name in COL_SHARDED else (r // 8, c)


def _exchange_host(names, grads):
    n = len(names)
    shapes = [_sub_shape(k, g.shape) for k, g in zip(names, grads)]

    def copy(ins, outs, sems, w, k):
        x, y, c = _coords()
        px, py, pc = _peer(x, y, c, k)
        sr, sc = shapes[w]
        if names[w] in COL_SHARDED:
            col = _col_block(names[w], px, py) * sc
            src = ins[w].at[pl.ds(pl.multiple_of(pc * sr, 16), sr), pl.ds(pl.multiple_of(col, 128), sc)]
        else:
            src = ins[w].at[pl.ds(pl.multiple_of((4 * px + 2 * py + pc) * sr, 16), sr)]
        return pltpu.make_async_remote_copy(
            src_ref=src, dst_ref=outs[w].at[k - 1], send_sem=sems[0].at[w, k - 1], recv_sem=sems[1].at[w, k - 1],
            device_id=(px, py, pc), device_id_type=MESH)

    def start(ins, outs, sems):
        for w in range(n):
            for k in range(1, 8):
                copy(ins, outs, sems, w, k).start()

    def finish(ins, outs, sems):
        for w in range(n):
            for k in range(1, 8):
                copy(ins, outs, sems, w, k).wait_recv()
        for w in range(n):
            for k in range(1, 8):
                copy(ins, outs, sems, w, k).wait_send()

    return _Host(ins=list(grads), out_shapes=[jax.ShapeDtypeStruct((7,) + s, g.dtype) for s, g in zip(shapes, grads)],
                 aliases={}, sems=[pltpu.SemaphoreType.DMA((n, 7)), pltpu.SemaphoreType.DMA((n, 7))],
                 start=start, mid=None, finish=finish)


def _sibling_host(bufs):
    n = len(bufs)

    def copy(outs, sems, w, half_of):
        x, y, c = _coords()
        half = outs[w].shape[0] // 2
        rows = outs[w].at[pl.ds(pl.multiple_of(half_of * half, 8), half)]
        return pltpu.make_async_remote_copy(src_ref=rows, dst_ref=rows, send_sem=sems[0].at[w], recv_sem=sems[1].at[w],
                                            device_id=(x, y, 1 - c), device_id_type=MESH)

    def start(_, outs, sems):
        c = lax.axis_index("c")
        for w in range(n):
            copy(outs, sems, w, c).start()

    def finish(_, outs, sems):
        c = lax.axis_index("c")
        for w in range(n):
            copy(outs, sems, w, 1 - c).wait_recv()
        for w in range(n):
            copy(outs, sems, w, c).wait_send()

    return _Host(ins=list(bufs), out_shapes=[jax.ShapeDtypeStruct(b.shape, b.dtype) for b in bufs],
                 aliases={i: i for i in range(n)},
                 sems=[pltpu.SemaphoreType.DMA((n,)), pltpu.SemaphoreType.DMA((n,))], start=start, mid=None, finish=finish)


def _pack_host(pack):
    def me_of():
        x, y, c = _coords()
        return x, y, c, 4 * x + 2 * y + c

    def copy(ins, outs, sems, k, slot):
        x, y, c, _ = me_of()
        return pltpu.make_async_remote_copy(src_ref=ins[0], dst_ref=outs[0].at[slot], send_sem=sems[0].at[k - 1],
                                            recv_sem=sems[1].at[k - 1], device_id=_peer(x, y, c, k), device_id_type=MESH)

    def start(ins, outs, sems):
        me = me_of()[3]
        pltpu.make_async_copy(ins[0], outs[0].at[me], sems[2]).start()
        for k in range(1, 8):
            copy(ins, outs, sems, k, me).start()

    def finish(ins, outs, sems):
        x, y, c, me = me_of()
        for k in range(1, 8):
            px, py, pc = _peer(x, y, c, k)
            copy(ins, outs, sems, k, 4 * px + 2 * py + pc).wait_recv()
        for k in range(1, 8):
            copy(ins, outs, sems, k, me).wait_send()
        pltpu.make_async_copy(ins[0], outs[0].at[me], sems[2]).wait()

    return _Host(ins=[pack], out_shapes=[jax.ShapeDtypeStruct((8,) + pack.shape, F32)], aliases={},
                 sems=[pltpu.SemaphoreType.DMA((7,)), pltpu.SemaphoreType.DMA((7,)), pltpu.SemaphoreType.DMA],
                 start=start, mid=None, finish=finish)


def _join_hosts(hosts):
    ins, outs, sems, aliases, spans = [], [], [], {}, []
    for h in hosts:
        spans.append((len(ins), len(h.ins), len(outs), len(h.out_shapes), len(sems), len(h.sems)))
        for i_in, i_out in h.aliases.items():
            aliases[len(ins) + i_in] = len(outs) + i_out
        ins, outs, sems = ins + list(h.ins), outs + list(h.out_shapes), sems + list(h.sems)

    def phase(which):
        fns = [getattr(h, which) for h in hosts]
        if all(f is None for f in fns):
            return None

        def run(i_refs, o_refs, s_refs):
            for f, (i0, n_i, o0, n_o, s0, n_s) in zip(fns, spans):
                if f is not None:
                    f(i_refs[i0:i0 + n_i], o_refs[o0:o0 + n_o], s_refs[s0:s0 + n_s])

        return run

    return _Host(ins, outs, aliases, sems, phase("start"), phase("mid"), phase("finish"))


def _allgather_pack(pack):
    rows = pack.shape[0]

    def body(p_ref, o_ref, send, recv, local):
        x, y, c = _coords()
        me = 4 * x + 2 * y + c
        mine = pltpu.make_async_copy(p_ref, o_ref.at[me], local)
        mine.start()

        def copy(k, slot):
            return pltpu.make_async_remote_copy(src_ref=p_ref, dst_ref=o_ref.at[slot], send_sem=send.at[k - 1],
                                                recv_sem=recv.at[k - 1], device_id=_peer(x, y, c, k), device_id_type=MESH)

        for k in range(1, 8):
            copy(k, me).start()
        for k in range(1, 8):
            px, py, pc = _peer(x, y, c, k)
            copy(k, 4 * px + 2 * py + pc).wait_recv()
        for k in range(1, 8):
            copy(k, me).wait_send()
        mine.wait()

    return pl.pallas_call(
        body, name="allgather_pack", in_specs=[pl.BlockSpec(memory_space=pltpu.VMEM)], out_specs=ANY,
        out_shape=jax.ShapeDtypeStruct((8,) + pack.shape, F32),
        scratch_shapes=[pltpu.SemaphoreType.DMA((7,)), pltpu.SemaphoreType.DMA((7,)), pltpu.SemaphoreType.DMA],
        compiler_params=pltpu.CompilerParams(vmem_limit_bytes=VMEM_LIMIT),
    )(pack)


SMALL = ("norm1_g", "conv_b", "rg_wa", "rg_ba", "rg_wx", "rg_bx", "rg_lambda", "hg_lb_logits", "hg_norm_g",
         "norm2_g", "norm_f_g")
SHARDED_SMALL = ("meta_tokens", "conv_w")

ROWS = {"conv_b": (1, RGW), "rg_ba": (1, RGW), "rg_bx": (1, RGW), "rg_lambda": (1, RGW), "conv_w": (4, RGW),
        "norm2_g": (1, D), "norm_f_g": (1, D), "hg_lb_logits": (2, D),
        "rg_wa": (RGW * RG_BLOCK // 128, 128), "rg_wx": (RGW * RG_BLOCK // 128, 128), "hg_norm_g": (1, 128),
        "norm1_g": (1, D), "meta_tokens": (N_META, D), "loss": (1, D)}
PACKS = {"early_wide": ("conv_b", "rg_ba", "rg_bx", "rg_lambda", "conv_w"),
         "early_model": ("norm2_g", "norm_f_g", "hg_lb_logits"),
         "early_lane": ("rg_wa", "rg_wx", "hg_norm_g"),
         "late": ("norm1_g", "meta_tokens", "loss")}
EARLY = ("early_wide", "early_model", "early_lane")
NO_UPDATE = ("conv_w", "meta_tokens", "loss")


def _pack_rows(vals, names):
    parts, first, row = [], {}, 0
    for k in names:
        first[k] = row
        parts.append(vals[k].reshape(ROWS[k]).astype(F32))
        row += ROWS[k][0]
    if row % 8:
        parts.append(jnp.zeros((-row % 8, ROWS[names[0]][1]), F32))
    return jnp.concatenate(parts, axis=0), first


ORDER = ("meta_tokens", "norm1_g", "w_in", "conv_w", "conv_b", "rg_wa", "rg_ba", "rg_wx", "rg_bx", "rg_lambda",
         "hg_lb_logits", "hg_norm_g", "w_proj_a", "w_proj_b", "w_out", "norm2_g", "w_ffn_in", "w_ffn_down", "norm_f_g")


def kernel(x, meta_tokens, norm1_g, w_in, conv_w, conv_b, rg_wa, rg_ba, rg_wx, rg_bx, rg_lambda, hg_lb_logits, hg_norm_g, w_proj_a, w_proj_b, w_out, norm2_g, w_ffn_in, w_ffn_down, norm_f_g, loss_target, m_meta_tokens, m_norm1_g, m_w_in, m_conv_w, m_conv_b, m_rg_wa, m_rg_ba, m_rg_wx, m_rg_bx, m_rg_lambda, m_hg_lb_logits, m_hg_norm_g, m_w_proj_a, m_w_proj_b, m_w_out, m_norm2_g, m_w_ffn_in, m_w_ffn_down, m_norm_f_g, v_meta_tokens, v_norm1_g, v_w_in, v_conv_w, v_conv_b, v_rg_wa, v_rg_ba, v_rg_wx, v_rg_bx, v_rg_lambda, v_hg_lb_logits, v_hg_norm_g, v_w_proj_a, v_w_proj_b, v_w_out, v_norm2_g, v_w_ffn_in, v_w_ffn_down, v_norm_f_g):
    args = dict(locals())
    w = {k: args[k] for k in ORDER}
    m = {k: args["m_" + k] for k in ORDER}
    v = {k: args["v_" + k] for k in ORDER}
    xi, yi, ci = _coords()
    chip = 2 * xi + yi
    slot = jnp.reshape(chip, (1,)).astype(jnp.int32)
    xyc = jnp.stack([xi, yi, ci]).astype(jnp.int32)

    def full_matrix(k, gth):
        if k == "w_ffn_in":
            return jnp.concatenate([gth[0], gth[2], gth[1], gth[3]], axis=1)
        if k in COL_SHARDED:
            return jnp.concatenate([gth[j] for j in range(4)], axis=1)
        return gth.reshape(-1, gth.shape[2])

    tiny = jnp.zeros((32, 384), F32)
    tiny = tiny.at[0:N_META, 0:256].set(meta_tokens).at[N_META:N_META + 4, 0:320].set(conv_w[0])
    slots = {k: _place_own(w[k][0], slot, BF16, "place_" + k) for k in BIG}
    w_in_all, tiny_all = _run_host(_gather_host([slots["w_in"], _place_own(tiny, slot, F32, "place_tiny")]),
                                   "allgather_w_in")
    later = [k for k in BIG if k != "w_in"]
    meta_full = jnp.transpose(tiny_all[:, 0:N_META, 0:256], (1, 0, 2)).reshape(N_META, D)
    conv_w_full = jnp.transpose(tiny_all[:, N_META:N_META + 4, 0:320], (1, 0, 2)).reshape(4, RGW)

    small = dict(norm1_g=norm1_g, conv_w=conv_w_full, conv_b=conv_b, rg_wa=rg_wa[0], rg_ba=rg_ba, rg_wx=rg_wx[0],
                 rg_bx=rg_bx, rg_lambda=rg_lambda, hg_lb_logits=hg_lb_logits, hg_norm_g=hg_norm_g,
                 norm2_g=norm2_g, norm_f_g=norm_f_g.reshape(1, D))

    hosts = {
        "mm_proj": lambda g: _gather_host([slots[k] for k in later]),
        "mm_d_act": lambda g: _exchange_host(["w_ffn_down"], [g["w_ffn_down"]]),
        "mm_d_hn2": lambda g: _exchange_host(["w_ffn_in"], [g["w_ffn_in"]]),
        "mm_d_win": lambda g: _join_hosts([
            _exchange_host(["w_out", "w_proj_a", "w_proj_b"], [g["w_out"], g["w_proj_a"], g["w_proj_b"]]),
            *[_pack_host(_pack_rows(g, PACKS[p])[0]) for p in EARLY]]),
        "mm_d_hn1": lambda g: _exchange_host(["w_in"], [_unrotate_g_in(g["w_in"])]),
    }
    h0 = jnp.concatenate([jnp.zeros((PAD, D), F32), meta_full, x[0]], axis=0)
    loss_blk, grad_x, g, carried = _local_step(
        h0, loss_target[0], {"w_in": _rotate_w_in(full_matrix("w_in", w_in_all))}, small, hosts=hosts,
        finalize=lambda gathered: {k: full_matrix(k, gth) for k, gth in zip(later, gathered)})
    g["loss"] = jnp.broadcast_to(loss_blk[0:1, 0:1], (1, D))

    halves = {}
    sources = {"mm_d_act": ["w_ffn_down"], "mm_d_hn2": ["w_ffn_in"], "mm_d_win": ["w_out", "w_proj_a", "w_proj_b"],
               "mm_d_hn1": ["w_in"]}
    for name, keys in sources.items():
        partials, received = carried[name]
        for k, part, rec in zip(keys, partials, received):
            halves[k] = _sum_place(rec, part, k, xyc, "sum_" + k)
    g_big = dict(zip(BIG, _run_host(_sibling_host([halves[k] for k in BIG]), "sibling_swap")))
    late, _ = _pack_rows(g, PACKS["late"])
    recv_packs = dict(zip(EARLY, carried["mm_d_win"][1][3:]), late=_allgather_pack(late))

    grad, delta, new_m, new_v, summed = {}, {}, {}, {}, {}
    for pack, names in PACKS.items():
        row, params, extras = 0, [], []
        for k in names:
            if k in NO_UPDATE:
                extras.append((row, ROWS[k][0]))
            else:
                params.append((row,) + tuple(d[k].reshape(ROWS[k]) for d in (w, m, v)))
            row += ROWS[k][0]
        updated, rows_only = _adamw_rows(recv_packs[pack], params, extras, "adamw_" + pack)
        for k, res in zip([k for k in names if k not in NO_UPDATE], updated):
            grad[k], delta[k], new_m[k], new_v[k] = (a.reshape(w[k].shape) for a in res)
        summed.update(zip([k for k in names if k in NO_UPDATE], rows_only))
    loss = summed["loss"][0, 0]
    g_shard = {"meta_tokens": lax.dynamic_slice(summed["meta_tokens"], (0, chip * 256), (N_META, 256)),
               "conv_w": lax.dynamic_slice(summed["conv_w"], (0, chip * 320), (4, 320))}
    for k in BIG + SHARDED_SMALL:
        gk = g_big[k] if k in BIG else g_shard[k]
        wk, mk, vk = (d[k].reshape(gk.shape) for d in (w, m, v))
        dk, mk, vk = _adamw(wk, gk, mk, vk, "adamw_" + k)
        grad[k], delta[k], new_m[k], new_v[k] = (a.reshape(w[k].shape) for a in (gk, dk, mk, vk))

    return (loss, grad_x[None], *[grad[k] for k in ORDER], *[delta[k] for k in ORDER],
            *[new_m[k] for k in ORDER], *[new_v[k] for k in ORDER])
```

```python
import functools

import jax
import jax.numpy as jnp
from jax import lax
from jax.experimental import pallas as pl
from jax.experimental.pallas import tpu as pltpu

F32, BF16 = jnp.float32, jnp.bfloat16
D = 1024
N_META = 16
RGW = 1280
RG_BLOCKS, RG_BLOCK = 16, 80
RG_C = 8.0
HEADS, HD = 8, 128
HGW = HEADS * HD
DFF = 2816
D_IN = 2 * RGW + 6 * D
ROT = 2 * RGW
COL_PAD = 256
D_INP = D_IN + COL_PAD
EPS = 1e-6
CH = 64
TM = 256
ROW0 = TM
PAD = ROW0 - N_META
CT = RGW
RS = 16
NCT = RGW // CT
EXP_CLAMP = 80.0
VMEM_LIMIT = 56 * 1024 * 1024

LR, B1, B2, ADAM_EPS, WD, STEP = 0.001, 0.9, 0.999, 1e-08, 0.01, 10
MESH = pl.DeviceIdType.MESH
ANY = pl.BlockSpec(memory_space=pl.ANY)


def _cparams(sem):
    return pltpu.CompilerParams(dimension_semantics=sem, vmem_limit_bytes=VMEM_LIMIT)


def _pick(n, prefs):
    for p in prefs:
        if n % p == 0:
            return p
    return n


def _sig(x):
    return 0.5 * jnp.tanh(0.5 * x) + 0.5


def _dot(a, b, dims):
    return lax.dot_general(a, b, (dims, ((), ())), preferred_element_type=F32)


NN, NT, TN_ = ((1,), (0,)), ((1,), (1,)), ((0,), (0,))


class _Host:
    def __init__(self, ins, out_shapes, aliases, sems, start, mid, finish):
        self.ins, self.out_shapes, self.aliases, self.sems = ins, out_shapes, aliases, sems
        self.start, self.mid, self.finish = start, mid, finish


class _Epi:
    def __init__(self, ins, in_specs, out_shapes, out_specs, fn, sequential=False, split=1):
        self.ins, self.in_specs, self.out_shapes, self.out_specs, self.fn = ins, in_specs, out_shapes, out_specs, fn
        self.sequential = sequential
        self.split = split


class _Pro:
    def __init__(self, ins, in_specs, fn):
        self.ins, self.in_specs, self.fn = ins, in_specs, fn


def _norm_pro(gain):
    def fn(a_ref, ins):
        xh, _ = _rms(a_ref[...])
        return (xh * ins[0][...]).astype(BF16)

    return _Pro([gain], [pl.BlockSpec((1, gain.shape[1]), lambda i, j, k: (0, 0))], fn)


def _mm(a, b, mode, out_dtype, name, resid=None, host=None, epi=None, tiles=None, sparse=None, pro=None):
    if mode == "nn":
        (m, kd), n = a.shape, b.shape[1]
    elif mode == "nt":
        (m, kd), n = a.shape, b.shape[0]
    else:
        (kd, m), n = a.shape, b.shape[1]
    if mode == "tn":
        tm = _pick(m, (1024, 1280, 1408, 640, 512, 256, 128))
        tk = _pick(kd, (1408, 768, 512, 256, 128))
    else:
        tm = _pick(m, (768, 512, 640, 256, 128))
        tk = kd if kd <= 2816 else _pick(kd, (1792, 1408, 1024, 512, 256, 128))
    tn = _pick(n, (1792, 1408, 1280, 1024, 512, 256, 128))
    if tiles is not None:
        tm, tn = tiles
    a_map = (lambda i, j, k: (k, i)) if mode == "tn" else (lambda i, j, k: (i, k))
    b_map = (lambda i, j, k: (j, k)) if mode == "nt" else (lambda i, j, k: (k, j))
    o_map = lambda i, j, k: (i, j)
    if sparse is not None:
        (ni, nj, nk), (tm, tn, tk), (a_map, b_map, o_map) = sparse
    else:
        assert m % tm == 0 and n % tn == 0 and kd % tk == 0, (name, m, n, kd, tm, tn, tk)
        ni, nj, nk = m // tm, n // tn, kd // tk
    dims = {"nn": NN, "nt": NT, "tn": TN_}[mode]

    n_hin = len(host.ins) if host else 0
    n_hout = len(host.out_shapes) if host else 0
    n_res = 0 if resid is None else 1
    n_pro = 0 if pro is None else 1
    n_pin = len(pro.ins) if pro else 0
    n_ein = len(epi.ins) if epi else 0
    n_out = len(epi.out_shapes) if epi else 1
    assert not n_pro or (mode == "nn" and nk == 1)

    def finish(r, r_ref, e_in, o_refs, i, rows=slice(None), first=True):
        if resid is not None:
            r = r + r_ref[rows, :]
        if epi:
            epi.fn(r, e_in, o_refs, i, rows, first)
        else:
            o_refs[0][rows, :] = r.astype(out_dtype)

    def body(*refs):
        a_ref, b_ref = refs[:2]
        r_ref = refs[2] if resid is not None else None
        pos = 2 + n_res
        p_in = refs[pos:pos + n_pin]
        pos += n_pin
        e_in = refs[pos:pos + n_ein]
        pos += n_ein
        h_in = refs[pos:pos + n_hin]
        pos += n_hin
        o_ref = refs[pos:pos + n_out]
        pos += n_out
        hn_ref = refs[pos] if n_pro else None
        pos += n_pro
        h_out = refs[pos:pos + n_hout]
        scratch = refs[pos + n_hout:]
        n_acc = 1 if nk > 1 else 0
        h_sems = scratch[n_acc + n_pro:]
        i, j, k = pl.program_id(0), pl.program_id(1), pl.program_id(2)
        if host:
            @pl.when((i == 0) & (j == 0) & (k == 0))
            def _():
                host.start(h_in, h_out, h_sems)

            if host.mid is not None:
                @pl.when((i == (3 * ni) // 4) & (j == 0) & (k == 0))
                def _():
                    host.mid(h_in, h_out, h_sems)

        if n_pro:
            a_s = scratch[n_acc]

            @pl.when(j == 0)
            def _():
                a_s[...] = pro.fn(a_ref, p_in)
                hn_ref[...] = a_s[...]

            a_val = a_s[...]
        else:
            a_val = a_ref[...]

        if nk == 1 and epi and epi.split > 1 and mode != "tn":
            part = tm // epi.split
            for p in range(epi.split):
                rows = slice(p * part, (p + 1) * part)
                finish(_dot(a_val[rows, :], b_ref[...], dims), r_ref, e_in, o_ref, i, rows, p == 0)
        elif nk == 1:
            finish(_dot(a_val, b_ref[...], dims), r_ref, e_in, o_ref, i)
        else:
            acc = scratch[0]

            @pl.when(k == 0)
            def _():
                acc[...] = jnp.zeros_like(acc)

            acc[...] += _dot(a_val, b_ref[...], dims)

            @pl.when(k == nk - 1)
            def _():
                finish(acc[...], r_ref, e_in, o_ref, i)

        if host:
            @pl.when((i == ni - 1) & (j == nj - 1) & (k == nk - 1))
            def _():
                host.finish(h_in, h_out, h_sems)

    a_spec = pl.BlockSpec((tk, tm) if mode == "tn" else (tm, tk), a_map)
    b_spec = pl.BlockSpec((tn, tk) if mode == "nt" else (tk, tn), b_map)
    o_spec = pl.BlockSpec((tm, tn), o_map)
    in_specs, args = [a_spec, b_spec], [a, b]
    if resid is not None:
        in_specs.append(o_spec)
        args.append(resid)
    if n_pro:
        in_specs += list(pro.in_specs)
        args += list(pro.ins)
    out_shapes, out_specs = [jax.ShapeDtypeStruct((m, n), out_dtype)], [o_spec]
    if epi:
        in_specs += list(epi.in_specs)
        args += list(epi.ins)
        out_shapes, out_specs = list(epi.out_shapes), list(epi.out_specs)
    if n_pro:
        out_shapes.append(jax.ShapeDtypeStruct((m, kd), BF16))
        out_specs.append(pl.BlockSpec((tm, kd), lambda i, j, k: (i, 0)))
    scratch = ([pltpu.VMEM((tm, tn), F32)] if nk > 1 else []) + ([pltpu.VMEM((tm, kd), BF16)] if n_pro else [])
    sequential = host or n_pro or (epi and epi.sequential)
    outs = pl.pallas_call(
        body, name=name, grid=(ni, nj, nk),
        in_specs=in_specs + [ANY] * n_hin, out_specs=out_specs + [ANY] * n_hout,
        out_shape=out_shapes + (list(host.out_shapes) if host else []),
        scratch_shapes=scratch + (list(host.sems) if host else []),
        input_output_aliases=({2 + n_res + n_pin + n_ein + i_in: n_out + n_pro + i_out
                               for i_in, i_out in host.aliases.items()} if host else {}),
        compiler_params=_cparams(("arbitrary",) * 3 if sequential else ("parallel", "parallel", "arbitrary")),
    )(*args, *(host.ins if host else []))
    main = list(outs[:n_out]) if epi else outs[0]
    if n_pro:
        main = (main, outs[n_out])
    return (main, list(outs[n_out + n_pro:])) if host else main


def _rms(x):
    r = lax.rsqrt(jnp.mean(x * x, axis=-1, keepdims=True) + EPS)
    return x * r, r


def _rms_bwd_math(dhn, x, g):
    xh, r = _rms(x)
    dxh = dhn * g
    dx = r * (dxh - xh * jnp.mean(dxh * xh, axis=-1, keepdims=True))
    return dx, jnp.sum(dhn * xh, axis=0, keepdims=True)


def _row_specs(tm):
    return pl.BlockSpec((tm, D), lambda i, j, k: (i, 0)), pl.BlockSpec((1, D), lambda i, j, k: (0, 0))


def _final_loss_epi(gf, target, t):
    def fn(x, ins, outs, i, rows, first):
        g_ref, t_ref = ins
        dh_ref, dhb_ref, loss_ref, dg_ref = outs

        if first:
            @pl.when(i == 0)
            def _():
                loss_ref[...] = jnp.zeros_like(loss_ref)
                dg_ref[...] = jnp.zeros_like(dg_ref)

        g = g_ref[...]
        xh, _ = _rms(x)
        err = jnp.where(i > 0, xh * g - t_ref[rows, :], 0.0)
        loss_ref[...] += 0.5 * jnp.sum(jnp.sum(err * err, axis=-1, keepdims=True) * (1.0 / D))
        dx, dg = _rms_bwd_math(err * (1.0 / D), x, g)
        dh_ref[rows, :] = dx
        dhb_ref[rows, :] = dx.astype(BF16)
        dg_ref[...] += dg

    row, vec = _row_specs(TM)
    return _Epi(ins=[gf, target], in_specs=[vec, pl.BlockSpec((TM, D), lambda i, j, k: (jnp.maximum(i - 1, 0), 0))],
                out_shapes=[jax.ShapeDtypeStruct((t, D), F32), jax.ShapeDtypeStruct((t, D), BF16),
                            jax.ShapeDtypeStruct((8, 128), F32), jax.ShapeDtypeStruct((1, D), F32)],
                out_specs=[row, row, pl.BlockSpec((8, 128), lambda i, j, k: (0, 0)), vec], fn=fn, sequential=True)


def _rms_bwd_epi(h, g, dres, tm):
    t = h.shape[0]

    def fn(dhn, ins, outs, i, rows, first):
        h_ref, g_ref, dres_ref = ins
        dh_ref, dhb_ref, dg_ref = outs

        if first:
            @pl.when(i == 0)
            def _():
                dg_ref[...] = jnp.zeros_like(dg_ref)

        dx, dg = _rms_bwd_math(dhn, h_ref[rows, :], g_ref[...])
        dh = dres_ref[rows, :] + dx
        dh_ref[rows, :] = dh
        dhb_ref[rows, :] = dh.astype(BF16)
        dg_ref[...] += dg

    row, vec = _row_specs(tm)
    return _Epi(ins=[h, g, dres], in_specs=[row, vec, row],
                out_shapes=[jax.ShapeDtypeStruct((t, D), F32), jax.ShapeDtypeStruct((t, D), BF16),
                            jax.ShapeDtypeStruct((1, D), F32)],
                out_specs=[row, row, vec], fn=fn, sequential=True)


def _rmsnorm1_bwd(dhn, h, g, dres):
    t = h.shape[0]

    def body(dhn_ref, h_ref, g_ref, dres_ref, dx_ref, dmeta_ref, dg_ref):
        i = pl.program_id(0)

        @pl.when(i == 0)
        def _():
            dg_ref[...] = jnp.zeros_like(dg_ref)

        dx, dg = _rms_bwd_math(dhn_ref[...].astype(F32), h_ref[...], g_ref[...])
        dh = dres_ref[...] + dx
        dg_ref[...] += dg
        dx_ref[...] = dh

        @pl.when(i == 0)
        def _():
            dmeta_ref[...] = dh[PAD:, :]

    row = pl.BlockSpec((TM, D), lambda i: (i, 0))
    vec = pl.BlockSpec((1, D), lambda i: (0, 0))
    return pl.pallas_call(
        body, name="rmsnorm1_bwd", grid=(t // TM,),
        in_specs=[row, row, vec, row],
        out_specs=[pl.BlockSpec((TM, D), lambda i: (jnp.maximum(i - 1, 0), 0)),
                   pl.BlockSpec((N_META, D), lambda i: (0, 0)), vec],
        out_shape=[jax.ShapeDtypeStruct((t - ROW0, D), F32), jax.ShapeDtypeStruct((N_META, D), F32),
                   jax.ShapeDtypeStruct((1, D), F32)],
        compiler_params=_cparams(("arbitrary",)),
    )(dhn, h, g, dres)


FH = DFF // 2
FM = 384


def _swiglu_epi(t):
    def fn(r, _, outs, i, rows, first):
        g, u = r[:, :FH], r[:, FH:]
        outs[0][rows, :] = r.astype(BF16)
        outs[1][rows, :] = (g * _sig(g) * u).astype(BF16)

    return _Epi(ins=[], in_specs=[],
                out_shapes=[jax.ShapeDtypeStruct((t, 2 * DFF), BF16), jax.ShapeDtypeStruct((t, DFF), BF16)],
                out_specs=[pl.BlockSpec((FM, 2 * FH), lambda i, j, k: (i, j)), pl.BlockSpec((FM, FH), lambda i, j, k: (i, j))],
                fn=fn)


def _swiglu_bwd_epi(gu):
    def fn(d, ins, outs, i, rows, first):
        gu_t = ins[0][rows, :].astype(F32)
        g, u = gu_t[:, :FH], gu_t[:, FH:]
        s = _sig(g)
        outs[0][rows, :FH] = (d * u * (s * (1.0 + g * (1.0 - s)))).astype(BF16)
        outs[0][rows, FH:] = (d * (g * s)).astype(BF16)

    spec = pl.BlockSpec((FM, 2 * FH), lambda i, j, k: (i, j))
    return _Epi(ins=[gu], in_specs=[spec], out_shapes=[jax.ShapeDtypeStruct(gu.shape, BF16)], out_specs=[spec], fn=fn)


COL_GA, COL_GB = 4, 5
COL_AX, COL_AG = (6 * D + COL_PAD) // CT, (6 * D + COL_PAD + RGW) // CT


def _merge_pro(pb, proj, tm):
    def fn(a_ref, ins):
        pa, pb_, ga, gb = (r[...].astype(F32) for r in (a_ref,) + tuple(ins))
        return (_sig(ga) * pa + _sig(gb) * pb_).astype(BF16)

    row, _ = _row_specs(tm)
    return _Pro([pb, proj, proj], [row, pl.BlockSpec((tm, D), lambda i, j, k: (i, COL_GA)),
                                   pl.BlockSpec((tm, D), lambda i, j, k: (i, COL_GB))], fn)


def _merge_bwd_epi(pa, pb, proj, tm):
    t = pa.shape[0]

    def fn(d, ins, outs, i, rows, first):
        pa, pb, ga, gb = (r[rows, :].astype(F32) for r in ins)
        dpa_ref, dpb_ref, dproj_ref = outs
        sa, sb = _sig(ga), _sig(gb)
        dpa_ref[rows, :] = (sa * d).astype(BF16)
        dpb_ref[rows, :] = (sb * d).astype(BF16)
        dproj_ref[rows, :D] = (d * pa * sa * (1.0 - sa)).astype(BF16)
        dproj_ref[rows, D:] = (d * pb * sb * (1.0 - sb)).astype(BF16)

    row, _ = _row_specs(tm)
    return _Epi(ins=[pa, pb, proj, proj],
                in_specs=[row, row, pl.BlockSpec((tm, D), lambda i, j, k: (i, COL_GA)),
                          pl.BlockSpec((tm, D), lambda i, j, k: (i, COL_GB))],
                out_shapes=[jax.ShapeDtypeStruct((t, D), BF16), jax.ShapeDtypeStruct((t, D), BF16),
                            jax.ShapeDtypeStruct((t, D_INP), BF16)],
                out_specs=[row, row, pl.BlockSpec((tm, 2 * D), lambda i, j, k: (i, 2))], fn=fn)


def _zero_pad_cols(dproj):
    t = dproj.shape[0]
    tz = _pick(t, (768, 256))

    def body(_, o_ref):
        o_ref[...] = jnp.zeros_like(o_ref)

    return pl.pallas_call(
        body, name="dproj_pad", grid=(t // tz,), in_specs=[ANY],
        out_specs=pl.BlockSpec((tz, COL_PAD), lambda i: (i, (6 * D) // COL_PAD)),
        out_shape=jax.ShapeDtypeStruct(dproj.shape, BF16), input_output_aliases={0: 0},
        compiler_params=_cparams(("parallel",)),
    )(dproj)


HALO = 16


def _rows_before(cur, before, s):
    n = cur.shape[0]
    return jnp.concatenate([before, cur], axis=0)[n - s:2 * n - s, :]


def _rows_after(cur, after, s):
    n = cur.shape[0]
    return jnp.concatenate([cur, after], axis=0)[s:n + s, :]


def _prev_halo(col0):
    return pl.BlockSpec((HALO, CT), lambda c, i: (jnp.maximum(i * (TM // HALO) - 1, 0), col0 + c))


def _conv_fwd(proj, w, b):
    t = proj.shape[0]

    def body(x_ref, halo_ref, w_ref, b_ref, o_ref, ob_ref):
        i = pl.program_id(1)
        wv, bv = w_ref[...], b_ref[...]
        first = jnp.where(i > 0, halo_ref[...].astype(F32), 0.0)

        def strip(k, before):
            rows = pl.ds(pl.multiple_of(k * HALO, HALO), HALO)
            x = x_ref[rows, :].astype(F32)
            y = bv + wv[3:4, :] * x
            for j in range(3):
                y = y + wv[j:j + 1, :] * _rows_before(x, before, 3 - j)
            o_ref[rows, :] = y
            ob_ref[rows, :] = y.astype(BF16)
            return x

        lax.fori_loop(0, TM // HALO, strip, first)

    blk = pl.BlockSpec((TM, CT), lambda c, i: (i, c))
    return pl.pallas_call(
        body, name="conv_fwd", grid=(NCT, t // TM),
        in_specs=[pl.BlockSpec((TM, CT), lambda c, i: (i, COL_AX + c)), _prev_halo(COL_AX),
                  pl.BlockSpec((4, CT), lambda c, i: (0, c)), pl.BlockSpec((1, CT), lambda c, i: (0, c))],
        out_specs=[blk, blk],
        out_shape=[jax.ShapeDtypeStruct((t, RGW), F32), jax.ShapeDtypeStruct((t, RGW), BF16)],
        compiler_params=_cparams(("parallel", "parallel")),
    )(proj, proj, w, b)


def _conv_bwd(dxc, proj, w, dproj):
    t = proj.shape[0]
    nt = t // TM

    def body(d_ref, dn_ref, x_ref, halo_ref, w_ref, _, dx_ref, dw_ref, db_ref, acc):
        i = pl.program_id(1)

        @pl.when(i == 0)
        def _():
            dw_ref[...] = jnp.zeros_like(dw_ref)
            db_ref[...] = jnp.zeros_like(db_ref)

        acc[...] = jnp.zeros_like(acc)
        wv = w_ref[...]
        first = jnp.where(i > 0, halo_ref[...].astype(F32), 0.0)
        last = jnp.where(i < nt - 1, dn_ref[...], 0.0)
        ns = TM // HALO

        def fold(v):
            return v[0:8, :] + v[8:16, :]

        def strip(k, before):
            off = pl.multiple_of(k * HALO, HALO)
            rows = pl.ds(off, HALO)
            x, d = x_ref[rows, :].astype(F32), d_ref[rows, :]
            nxt = d_ref[pl.ds(pl.multiple_of(jnp.minimum(off + HALO, TM - HALO), HALO), HALO), :]
            after = jnp.where(k == ns - 1, last, nxt)
            dx = wv[3:4, :] * d
            for j in range(3):
                dx = dx + wv[j:j + 1, :] * _rows_after(d, after, 3 - j)
                acc[j] += fold(d * _rows_before(x, before, 3 - j))
            acc[3] += fold(d * x)
            acc[4] += fold(d)
            dx_ref[rows, :] = dx.astype(BF16)
            return x

        lax.fori_loop(0, ns, strip, first)
        for j in range(4):
            dw_ref[j:j + 1, :] += jnp.sum(acc[j], axis=0, keepdims=True)
        db_ref[...] += jnp.sum(acc[4], axis=0, keepdims=True)

    return pl.pallas_call(
        body, name="conv_bwd", grid=(NCT, nt),
        in_specs=[pl.BlockSpec((TM, CT), lambda c, i: (i, c)),
                  pl.BlockSpec((HALO, CT), lambda c, i: (jnp.minimum((i + 1) * (TM // HALO), t // HALO - 1), c)),
                  pl.BlockSpec((TM, CT), lambda c, i: (i, COL_AX + c)), _prev_halo(COL_AX),
                  pl.BlockSpec((4, CT), lambda c, i: (0, c)), ANY],
        out_specs=[pl.BlockSpec((TM, CT), lambda c, i: (i, COL_AX + c)),
                   pl.BlockSpec((4, CT), lambda c, i: (0, c)), pl.BlockSpec((1, CT), lambda c, i: (0, c))],
        out_shape=[jax.ShapeDtypeStruct((t, D_INP), BF16), jax.ShapeDtypeStruct((4, RGW), F32),
                   jax.ShapeDtypeStruct((1, RGW), F32)],
        scratch_shapes=[pltpu.VMEM((5, 8, CT), F32)],
        input_output_aliases={5: 0},
        compiler_params=_cparams(("parallel", "arbitrary")),
    )(dxc, dxc, proj, proj, w, dproj)


def _gelu(x):
    c = 0.7978845608028654
    th = jnp.tanh(c * (x + 0.044715 * x * x * x))
    return 0.5 * x * (1.0 + th), th


def _rg_gates(gr, gi, xc, ba, bx, lam, row0):
    r = _sig(gr + ba)
    ig = _sig(gi + bx)
    sp = jax.nn.softplus(-lam)
    a = jnp.exp(-RG_C * r * sp)
    s2 = jnp.maximum(1.0 - a * a, 1e-30)
    rs = lax.rsqrt(s2)
    s = s2 * rs
    rows = row0 + lax.broadcasted_iota(jnp.int32, gr.shape, 0)
    live = rows >= PAD
    u = jnp.where(live, s * ig * xc, 0.0)
    return r, ig, sp, a, s, rs, u, live


def _rg_fwd(gri, xc, proj, ba, bx, lam):
    t = gri.shape[0]

    def body(gr_ref, gi_ref, xc_ref, ag_ref, ba_ref, bx_ref, lam_ref, h_ref, ya_ref, hc):
        i = pl.program_id(1)

        @pl.when(i == 0)
        def _():
            hc[...] = jnp.zeros_like(hc)

        ba, bx, lam = ba_ref[...], bx_ref[...], lam_ref[...]
        sub = lax.broadcasted_iota(jnp.int32, (8, CT), 0)

        def strip(k, h):
            off = pl.multiple_of(k * RS, RS)
            rows = pl.ds(off, RS)
            _, _, _, a, _, _, u, _ = _rg_gates(gr_ref[rows, :], gi_ref[rows, :], xc_ref[rows, :], ba, bx, lam,
                                               i * TM + off)
            outs = []
            for half in range(RS // 8):
                out = jnp.zeros((8, CT), F32)
                for r in range(8):
                    h = a[8 * half + r:8 * half + r + 1, :] * h + u[8 * half + r:8 * half + r + 1, :]
                    out = jnp.where(sub == r, h, out)
                outs.append(out)
            hs = jnp.concatenate(outs, axis=0)
            h_ref[rows, :] = hs
            ge, _ = _gelu(ag_ref[rows, :].astype(F32))
            ya_ref[rows, :] = (hs * ge).astype(BF16)
            return h

        hc[...] = lax.fori_loop(0, TM // RS, strip, hc[...])

    blk_ = pl.BlockSpec((TM, CT), lambda c, i: (i, c))
    vec = pl.BlockSpec((1, CT), lambda c, i: (0, c))
    return pl.pallas_call(
        body, name="rg_fwd", grid=(NCT, t // TM),
        in_specs=[blk_, pl.BlockSpec((TM, CT), lambda c, i: (i, NCT + c)), blk_,
                  pl.BlockSpec((TM, CT), lambda c, i: (i, COL_AG + c)), vec, vec, vec],
        out_specs=[blk_, blk_],
        out_shape=[jax.ShapeDtypeStruct((t, RGW), F32), jax.ShapeDtypeStruct((t, RGW), BF16)],
        scratch_shapes=[pltpu.VMEM((1, CT), F32)],
        compiler_params=_cparams(("parallel", "arbitrary")),
    )(gri, gri, xc, proj, ba, bx, lam)


def _rg_bwd(dya, h, gri, xc, proj, ba, bx, lam, dproj):
    t = gri.shape[0]
    nt = t // TM
    assert NCT == 1

    def body(dya_ref, h_ref, hh_ref, gr_ref, gi_ref, xc_ref, ag_ref, ba_ref, bx_ref, lam_ref, _,
             dgri_ref, dxc_ref, dag_ref, dba_ref, dbx_ref, dlam_ref, hbuf, acc, cc):
        i = pl.program_id(1)
        ri = nt - 1 - i

        @pl.when(i == 0)
        def _():
            cc[...] = jnp.zeros_like(cc)
            dba_ref[...] = jnp.zeros_like(dba_ref)
            dbx_ref[...] = jnp.zeros_like(dbx_ref)
            dlam_ref[...] = jnp.zeros_like(dlam_ref)

        acc[...] = jnp.zeros_like(acc)
        ba, bx, lam = ba_ref[...], bx_ref[...], lam_ref[...]
        halo_last = jnp.where(ri > 0, hh_ref[7:8, :], 0.0)
        sub = lax.broadcasted_iota(jnp.int32, (8, CT), 0)
        c0 = 0.7978845608028654

        def strip(kk, c):
            k = TM // RS - 1 - kk
            off = pl.multiple_of(k * RS, RS)
            rows = pl.ds(off, RS)
            xc, hv = xc_ref[rows, :], h_ref[rows, :]
            ag, dya = ag_ref[rows, :].astype(F32), dya_ref[rows, :].astype(F32)
            r, ig, sp, a, s, rs, _, live = _rg_gates(gr_ref[rows, :], gi_ref[rows, :], xc, ba, bx, lam, ri * TM + off)
            ge, th = _gelu(ag)
            dge = 0.5 * (1.0 + th) + 0.5 * ag * (1.0 - th * th) * c0 * (1.0 + 3.0 * 0.044715 * ag * ag)
            dag_ref[rows, :] = (dya * hv * dge).astype(BF16)
            d = dya * ge
            outs = []
            for half in range(RS // 8 - 1, -1, -1):
                out = jnp.zeros((8, CT), F32)
                for rr in range(7, -1, -1):
                    g = d[8 * half + rr:8 * half + rr + 1, :] + c
                    c = a[8 * half + rr:8 * half + rr + 1, :] * g
                    out = jnp.where(sub == rr, g, out)
                outs.insert(0, out)
            g = jnp.concatenate(outs, axis=0)
            before = h_ref[pl.ds(pl.multiple_of(jnp.maximum(off - RS, 0), RS), RS), :]
            hbuf[7:8, :] = jnp.where(k == 0, halo_last, before[RS - 1:RS, :])
            hbuf[8:, :] = hv
            hprev = hbuf[pl.ds(7, RS), :]
            du = jnp.where(live, g, 0.0)
            ds = du * (ig * xc)
            dm = du * s
            dla = (g * hprev - a * ds * rs) * a
            dr = dla * (-RG_C * sp) * r * (1.0 - r)
            di = dm * xc * ig * (1.0 - ig)
            dgri_ref[rows, :CT] = dr.astype(BF16)
            dgri_ref[rows, CT:] = di.astype(BF16)
            dxc_ref[rows, :] = dm * ig
            dl = dla * (-RG_C * r)
            for half in range(RS // 8):
                part = slice(8 * half, 8 * half + 8)
                acc[0] += dr[part]
                acc[1] += di[part]
                acc[2] += dl[part]
            return c

        cc[...] = lax.fori_loop(0, TM // RS, strip, cc[...])
        dba_ref[...] += jnp.sum(acc[0], axis=0, keepdims=True)
        dbx_ref[...] += jnp.sum(acc[1], axis=0, keepdims=True)
        dlam_ref[...] += jnp.sum(acc[2], axis=0, keepdims=True) * (-_sig(-lam))

    rblk = pl.BlockSpec((TM, CT), lambda c, i: (nt - 1 - i, c))
    vec = pl.BlockSpec((1, CT), lambda c, i: (0, c))
    hh = pl.BlockSpec((8, CT), lambda c, i: (jnp.maximum((nt - 1 - i) * (TM // 8) - 1, 0), c))
    agb = pl.BlockSpec((TM, CT), lambda c, i: (nt - 1 - i, COL_AG + c))
    return pl.pallas_call(
        body, name="rg_bwd", grid=(NCT, nt),
        in_specs=[rblk, rblk, hh, rblk, pl.BlockSpec((TM, CT), lambda c, i: (nt - 1 - i, NCT + c)), rblk, agb,
                  vec, vec, vec, ANY],
        out_specs=[pl.BlockSpec((TM, 2 * CT), lambda c, i: (nt - 1 - i, c)), rblk, agb, vec, vec, vec],
        out_shape=[jax.ShapeDtypeStruct((t, 2 * RGW), BF16),
                   jax.ShapeDtypeStruct((t, RGW), F32), jax.ShapeDtypeStruct((t, D_INP), BF16),
                   jax.ShapeDtypeStruct((1, RGW), F32), jax.ShapeDtypeStruct((1, RGW), F32),
                   jax.ShapeDtypeStruct((1, RGW), F32)],
        scratch_shapes=[pltpu.VMEM((RS + 8, CT), F32), pltpu.VMEM((3, 8, CT), F32), pltpu.VMEM((1, CT), F32)],
        input_output_aliases={10: 2},
        compiler_params=_cparams(("parallel", "arbitrary")),
    )(dya, h, h, gri, gri, xc, proj, ba, bx, lam, dproj)


NCH = TM // CH

def _tri_dot(tri, x):
    hi = x.astype(BF16)
    r1 = x - hi.astype(F32)
    mid = r1.astype(BF16)
    lo = (r1 - mid.astype(F32)).astype(BF16)
    return _dot(tri, hi, NN) + _dot(tri, mid, NN) + _dot(tri, lo, NN)


def _hg_chunk(qr, fr, lb):
    sf = _sig(fr)
    fg = lb + (1.0 - lb) * sf
    k = (1.0 - lb) * (1.0 - sf)
    sq = _sig(qr)
    q = qr * sq
    ri = lax.broadcasted_iota(jnp.int32, (CH, CH), 0)
    ci = lax.broadcasted_iota(jnp.int32, (CH, CH), 1)
    b = _tri_dot((ri >= ci).astype(BF16), jnp.log(fg))
    bm, bl = b[CH // 2 - 1:CH // 2, :], b[CH - 1:CH, :]
    ebm = jnp.exp(jnp.minimum(b - bm, EXP_CLAMP))
    ekm = jnp.exp(jnp.minimum(bm - b, EXP_CLAMP))
    eb = ebm * jnp.exp(bm)
    ekl = ekm * jnp.exp(bl - bm)
    return dict(sf=sf, fg=fg, k=k, sq=sq, q=q, eb=eb, ebm=ebm, ekm=ekm, ekl=ekl, ebl=jnp.exp(bl),
                qe=q * eb, qh=q * ebm, kh=k * ekm, kd=k * ekl, causal=ri >= ci, upper=(ci >= ri).astype(BF16))


def _hgrn_fwd(proj, lbl, gn):
    t = proj.shape[0]
    nt = t // TM

    def body(q_ref, f_ref, v_ref, g_ref, lbl_ref, gn_ref, yb_ref, o_ref, st_ref, st):
        @pl.when(pl.program_id(0) == 0)
        def _():
            st[...] = jnp.zeros_like(st)

        l = lbl_ref[...]
        lb = _sig(l[0:1, :] - l[1:2, :])
        gnv = gn_ref[...]

        def chunk(c, carry):
            off = pl.multiple_of(c * CH, CH)
            rows = pl.ds(off, CH)
            z = _hg_chunk(q_ref[rows, :].astype(F32), f_ref[rows, :].astype(F32), lb)
            v, gg = v_ref[rows, :], g_ref[rows, :].astype(F32)
            for hh in range(HEADS):
                sl = slice(hh * HD, (hh + 1) * HD)
                s_prev = st[hh]
                st_ref[c, hh] = s_prev
                vb = v[:, sl].astype(BF16)
                att = jnp.where(z["causal"], _dot(z["qh"][:, sl].astype(BF16), z["kh"][:, sl].astype(BF16), NT), 0.0)
                o = _dot(z["qe"][:, sl].astype(BF16), s_prev.astype(BF16), NT) + _dot(att.astype(BF16), vb, NN)
                st[hh] = s_prev * z["ebl"][:, sl] + _dot(vb, z["kd"][:, sl].astype(BF16), TN_)
                xh, _ = _rms(o)
                gh = gg[:, sl]
                o_ref[rows, sl] = o
                yb_ref[rows, sl] = (xh * gnv * gh * _sig(gh)).astype(BF16)
            return carry

        lax.fori_loop(0, NCH, chunk, 0, unroll=True)

    def col(j):
        return pl.BlockSpec((TM, HGW), lambda i, j=j: (i, j))

    return pl.pallas_call(
        body, name="hgrn_fwd", grid=(nt,),
        in_specs=[col(0), col(1), col(2), col(3), pl.BlockSpec((2, HGW), lambda i: (0, 0)),
                  pl.BlockSpec((1, HD), lambda i: (0, 0))],
        out_specs=[col(0), col(0), pl.BlockSpec((NCH, HEADS, HD, HD), lambda i: (i, 0, 0, 0))],
        out_shape=[jax.ShapeDtypeStruct((t, HGW), BF16), jax.ShapeDtypeStruct((t, HGW), F32),
                   jax.ShapeDtypeStruct((t // CH, HEADS, HD, HD), F32)],
        scratch_shapes=[pltpu.VMEM((HEADS, HD, HD), F32)],
        compiler_params=_cparams(("arbitrary",)),
    )(proj, proj, proj, proj, lbl, gn)


def _hgrn_bwd(dyb, proj, o, states, lbl, gn, dproj):
    t = proj.shape[0]
    nt = t // TM

    def body(dy_ref, q_ref, f_ref, v_ref, g_ref, o_ref, st_ref, lbl_ref, gn_ref, _,
             dp_ref, dgn_ref, dl_ref, dst, dlb):
        i = pl.program_id(0)

        @pl.when(i == 0)
        def _():
            dst[...] = jnp.zeros_like(dst)
            dlb[...] = jnp.zeros_like(dlb)
            dgn_ref[...] = jnp.zeros_like(dgn_ref)

        l = lbl_ref[...]
        lb = _sig(l[0:1, :] - l[1:2, :])
        gnv = gn_ref[...]
        last = lax.broadcasted_iota(jnp.int32, (CH, HD), 0) == CH - 1

        def chunk(cc, carry):
            c = NCH - 1 - cc
            off = pl.multiple_of(c * CH, CH)
            rows = pl.ds(off, CH)
            qr, fr = q_ref[rows, :].astype(F32), f_ref[rows, :].astype(F32)
            z = _hg_chunk(qr, fr, lb)
            v, gg, ov, dy = v_ref[rows, :], g_ref[rows, :].astype(F32), o_ref[rows, :], dy_ref[rows, :].astype(F32)
            dqs, dks, dbs, dvs, dgs = [], [], [], [], []
            dgn = jnp.zeros((1, HD), F32)
            for hh in range(HEADS):
                sl = slice(hh * HD, (hh + 1) * HD)
                s_prev, ds_new = st_ref[c, hh], dst[hh]
                qe, qh, kh, kd = z["qe"][:, sl], z["qh"][:, sl], z["kh"][:, sl], z["kd"][:, sl]
                ebl = z["ebl"][:, sl]
                gh, dyh = gg[:, sl], dy[:, sl]
                xh, rr = _rms(ov[:, sl])
                sg = _sig(gh)
                dyn = dyh * (gh * sg)
                dgs.append(dyh * (xh * gnv) * (sg * (1.0 + gh * (1.0 - sg))))
                dgn = dgn + jnp.sum(dyn * xh, axis=0, keepdims=True)
                dxh = dyn * gnv
                do = (rr * (dxh - xh * jnp.mean(dxh * xh, axis=-1, keepdims=True))).astype(BF16)
                vb, dsb = v[:, sl].astype(BF16), ds_new.astype(BF16)
                qeb, qhb, khb, kdb = (a.astype(BF16) for a in (qe, qh, kh, kd))
                att = jnp.where(z["causal"], _dot(qhb, khb, NT), 0.0).astype(BF16)
                datt = jnp.where(z["causal"], _dot(do, vb, NT), 0.0).astype(BF16)
                dvs.append(_dot(att, do, TN_) + _dot(kdb, dsb, NT))
                dqe = _dot(do, s_prev.astype(BF16), NN)
                dqh = _dot(datt, khb, NN)
                dkh = _dot(datt, qhb, TN_)
                dkd = _dot(vb, dsb, NN)
                qe, qh, kh, kd = (a.astype(F32) for a in (qeb, qhb, khb, kdb))
                dbl = (jnp.sum(dkd * kd, axis=0, keepdims=True)
                       + jnp.sum(ds_new * s_prev, axis=0, keepdims=True) * ebl)
                dqs.append(dqe * z["eb"][:, sl] + dqh * z["ebm"][:, sl])
                dks.append(dkh * z["ekm"][:, sl] + dkd * z["ekl"][:, sl])
                dbs.append(dqe * qe + dqh * qh - dkh * kh - dkd * kd + jnp.where(last, dbl, 0.0))
                dst[hh] = _dot(do, qeb, TN_) + ds_new * ebl
            dgn_ref[...] += dgn
            dq, dk, db = (jnp.concatenate(x, axis=1) for x in (dqs, dks, dbs))
            dlf = _tri_dot(z["upper"], db)
            sf, fg, sq = z["sf"], z["fg"], z["sq"]
            dmix = dlf / fg - dk
            dsf = dmix * (1.0 - lb)
            dlb[...] += jnp.sum(dmix * (1.0 - sf), axis=0, keepdims=True)
            dp_ref[rows, 0:HGW] = (dq * (sq * (1.0 + qr * (1.0 - sq)))).astype(BF16)
            dp_ref[rows, HGW:2 * HGW] = (dsf * sf * (1.0 - sf)).astype(BF16)
            dp_ref[rows, 2 * HGW:3 * HGW] = jnp.concatenate(dvs, axis=1).astype(BF16)
            dp_ref[rows, 3 * HGW:4 * HGW] = jnp.concatenate(dgs, axis=1).astype(BF16)
            return carry

        lax.fori_loop(0, NCH, chunk, 0, unroll=True)
        dl0 = dlb[...] * lb * (1.0 - lb)
        dl_ref[0:1, :] = dl0
        dl_ref[1:2, :] = -dl0

    def col(j):
        return pl.BlockSpec((TM, HGW), lambda i, j=j: (nt - 1 - i, j))

    return pl.pallas_call(
        body, name="hgrn_bwd", grid=(nt,),
        in_specs=[col(0), col(0), col(1), col(2), col(3), col(0),
                  pl.BlockSpec((NCH, HEADS, HD, HD), lambda i: (nt - 1 - i, 0, 0, 0)),
                  pl.BlockSpec((2, HGW), lambda i: (0, 0)), pl.BlockSpec((1, HD), lambda i: (0, 0)), ANY],
        out_specs=[pl.BlockSpec((TM, 4 * HGW), lambda i: (nt - 1 - i, 0)),
                   pl.BlockSpec((1, HD), lambda i: (0, 0)), pl.BlockSpec((2, HGW), lambda i: (0, 0))],
        out_shape=[jax.ShapeDtypeStruct((t, D_INP), BF16), jax.ShapeDtypeStruct((1, HD), F32),
                   jax.ShapeDtypeStruct((2, HGW), F32)],
        scratch_shapes=[pltpu.VMEM((HEADS, HD, HD), F32), pltpu.VMEM((1, HGW), F32)],
        input_output_aliases={9: 0},
        compiler_params=_cparams(("arbitrary",)),
    )(dyb, proj, proj, proj, proj, o, states, lbl, gn, dproj)


def _rotate_w_in(w):
    return jnp.concatenate([w[:, ROT:], jnp.zeros((w.shape[0], COL_PAD), w.dtype), w[:, :ROT]], axis=1)


def _unrotate_g_in(g):
    return jnp.concatenate([g[:, D_INP - ROT:], g[:, :D_IN - ROT]], axis=1)


SB = RGW // 2


def _bd_gates(t):
    tm = _pick(t, (768, 256))

    def col(j):
        return 2 * (j % 2) + j // 2

    return ((t // tm, 4, 1), (tm, SB, SB),
            (lambda i, j, k: (i, j // 2), lambda i, j, k: (j // 2, col(j)), lambda i, j, k: (i, col(j))))


def _bd_dw(t):
    tk = _pick(t, (1408, 768))
    return (1, 4, t // tk), (SB, SB, tk), (lambda i, j, k: (k, j % 2), lambda i, j, k: (k, j), lambda i, j, k: (j % 2, j))


def _block_diag(w):
    eye = jnp.eye(RG_BLOCKS, dtype=w.dtype)
    return (w[:, :, None, :] * eye[:, None, :, None]).reshape(RGW, RGW)


def _diag_blocks(wd):
    w4 = wd.reshape(RG_BLOCKS, RG_BLOCK, RG_BLOCKS, RG_BLOCK)
    return jnp.stack([w4[n, :, n, :] for n in range(RG_BLOCKS)])


def _local_step(h0, target, wts, small, hosts=None, finalize=None):
    hosts = hosts or {}
    carried = {}
    g = {}
    t = h0.shape[0]

    def mm(a, b, mode, out_dtype, name, **kw):
        if name not in hosts:
            return _mm(a, b, mode, out_dtype, name, **kw)
        host = hosts[name](g)
        out, res = _mm(a, b, mode, out_dtype, name, host=host, **kw)
        carried[name] = (host.ins, res)
        return out

    w_in = wts["w_in"]
    wax_d = jnp.concatenate([_block_diag(small["rg_wa"]), _block_diag(small["rg_wx"])], axis=1).astype(BF16)
    ba, bx, lam = small["rg_ba"], small["rg_bx"], small["rg_lambda"]
    lbl, gn = small["hg_lb_logits"], small["hg_norm_g"]
    conv_w, conv_b = small["conv_w"], small["conv_b"]

    tmm = _pick(t, (768, 256))
    proj, hn1 = mm(h0, w_in, "nn", BF16, "mm_proj", pro=_norm_pro(small["norm1_g"]))
    if finalize is not None:
        wts = finalize(carried["mm_proj"][1])
    w_pa, w_pb, w_out, w_fi, w_fd = (wts[k] for k in ("w_proj_a", "w_proj_b", "w_out", "w_ffn_in", "w_ffn_down"))
    xc, xcb = _conv_fwd(proj, conv_w, conv_b)
    gri = mm(xcb, wax_d, "nn", F32, "mm_rg_gates", sparse=_bd_gates(t))
    hrg, ya = _rg_fwd(gri, xc, proj, ba, bx, lam)
    yb, o, states = _hgrn_fwd(proj, lbl, gn)
    pa = mm(ya, w_pa, "nn", BF16, "mm_pa")
    pb = mm(yb, w_pb, "nn", BF16, "mm_pb")
    h1, merged = mm(pa, w_out, "nn", F32, "mm_out", resid=h0, pro=_merge_pro(pb, proj, tmm), tiles=(tmm, D))
    (gu, act), hn2 = mm(h1, w_fi, "nn", BF16, "mm_ffn_in", pro=_norm_pro(small["norm2_g"]), epi=_swiglu_epi(t),
                        tiles=(FM, 2 * FH))
    dh2, dh2b, loss, g["norm_f_g"] = mm(act, w_fd, "nn", F32, "mm_ffn_down", resid=h1,
                                        epi=_final_loss_epi(small["norm_f_g"], target, t), tiles=(TM, D))

    g["w_ffn_down"] = mm(act, dh2b, "tn", BF16, "mm_d_wfd")
    (dgu,) = mm(dh2b, w_fd, "nt", BF16, "mm_d_act", epi=_swiglu_bwd_epi(gu), tiles=(FM, FH))
    g["w_ffn_in"] = mm(hn2, dgu, "tn", BF16, "mm_d_wfi")
    dh1, dh1b, g["norm2_g"] = mm(dgu, w_fi, "nt", BF16, "mm_d_hn2", epi=_rms_bwd_epi(h1, small["norm2_g"], dh2, tmm),
                                 tiles=(tmm, D))
    g["w_out"] = mm(merged, dh1b, "tn", BF16, "mm_d_wout")
    dpa, dpb, dproj = mm(dh1b, w_out, "nt", BF16, "mm_d_merged", epi=_merge_bwd_epi(pa, pb, proj, TM), tiles=(TM, D))
    dproj = _zero_pad_cols(dproj)
    g["w_proj_a"] = mm(ya, dpa, "tn", BF16, "mm_d_wpa")
    g["w_proj_b"] = mm(yb, dpb, "tn", BF16, "mm_d_wpb")
    dya = mm(dpa, w_pa, "nt", BF16, "mm_d_ya")
    dyb = mm(dpb, w_pb, "nt", BF16, "mm_d_yb")
    dproj, g["hg_norm_g"], g["hg_lb_logits"] = _hgrn_bwd(dyb, proj, o, states, lbl, gn, dproj)
    dgri, dxc, dproj, g["rg_ba"], g["rg_bx"], g["rg_lambda"] = _rg_bwd(dya, hrg, gri, xc, proj, ba, bx, lam, dproj)
    dwax = mm(xcb, dgri, "tn", F32, "mm_d_wax", sparse=_bd_dw(t))
    g["rg_wa"], g["rg_wx"] = _diag_blocks(dwax[:, :RGW]), _diag_blocks(dwax[:, RGW:])
    dxc = mm(dgri, wax_d, "nt", F32, "mm_d_xc", resid=dxc)
    dproj, g["conv_w"], g["conv_b"] = _conv_bwd(dxc, proj, conv_w, dproj)
    g["w_in"] = mm(hn1, dproj, "tn", BF16, "mm_d_win")
    dhn1 = mm(dproj, w_in, "nt", BF16, "mm_d_hn1")
    grad_x, g["meta_tokens"], g["norm1_g"] = _rmsnorm1_bwd(dhn1, h0, small["norm1_g"], dh1)
    return loss, grad_x, g, carried


def _adamw_math(w, g, m, v):
    m = B1 * m + (1.0 - B1) * g
    v = B2 * v + (1.0 - B2) * (g * g)
    m_hat = m / (1.0 - B1 ** STEP)
    v_hat = v / (1.0 - B2 ** STEP)
    return -LR * (m_hat / (jnp.sqrt(v_hat) + ADAM_EPS) + WD * w), m, v


def _adamw(w, g, m, v, name):
    r, c = w.shape
    tr = _pick(r, (256, 352, 320, 128, 64, 32, 16, 8))

    def body(w_ref, g_ref, m_ref, v_ref, d_ref, mo_ref, vo_ref):
        d_ref[...], mo_ref[...], vo_ref[...] = _adamw_math(w_ref[...], g_ref[...], m_ref[...], v_ref[...])

    blk = pl.BlockSpec((tr, c), lambda i: (i, 0))
    return pl.pallas_call(
        body, name=name, grid=(r // tr,), in_specs=[blk] * 4, out_specs=[blk] * 3,
        out_shape=[jax.ShapeDtypeStruct((r, c), F32)] * 3,
        compiler_params=_cparams(("parallel",)),
    )(w, g, m, v)


def _adamw_rows(recv, params, extras, name):
    _, _, width = recv.shape
    n_par = len(params)

    def body(*refs):
        r_ref = refs[0]
        ins = refs[1:1 + 3 * n_par]
        outs = refs[1 + 3 * n_par:]
        g_all = r_ref[0]
        for p in range(1, 8):
            g_all = g_all + r_ref[p]
        for q, (row0, w, _, _) in enumerate(params):
            g = g_all[row0:row0 + w.shape[0], :]
            w_ref, m_ref, v_ref = ins[3 * q:3 * q + 3]
            g_ref, d_ref, mo_ref, vo_ref = outs[4 * q:4 * q + 4]
            g_ref[...] = g
            d_ref[...], mo_ref[...], vo_ref[...] = _adamw_math(w_ref[...], g, m_ref[...], v_ref[...])
        for q, (row0, n) in enumerate(extras):
            outs[4 * n_par + q][...] = g_all[row0:row0 + n, :]

    shapes = [jax.ShapeDtypeStruct(w.shape, F32) for _, w, _, _ in params for _ in range(4)]
    shapes += [jax.ShapeDtypeStruct((n, width), F32) for _, n in extras]
    vm = pl.BlockSpec(memory_space=pltpu.VMEM)
    outs = pl.pallas_call(
        body, name=name, in_specs=[vm] * (1 + 3 * n_par), out_specs=[vm] * len(shapes), out_shape=shapes,
        compiler_params=pltpu.CompilerParams(vmem_limit_bytes=VMEM_LIMIT),
    )(recv, *[a for _, w, m, v in params for a in (w, m, v)])
    return [tuple(outs[4 * q:4 * q + 4]) for q in range(n_par)], list(outs[4 * n_par:])


def _col_block(weight, px, py):
    return 2 * py + px if weight == "w_ffn_in" else 2 * px + py


def _sum_place(recv, gfull, weight, xyc, name):
    _, r, c = recv.shape
    tr = _pick(r, (128, 176, 160, 64, 32, 16))
    nb = r // tr

    def body(_, own_ref, r_ref, o_ref):
        g = own_ref[...].astype(F32)
        for p in range(7):
            g = g + r_ref[p].astype(F32)
        o_ref[...] = g

    if weight in COL_SHARDED:
        own = pl.BlockSpec((tr, c), lambda i, s: (s[2] * nb + i, _col_block(weight, s[0], s[1])))
    else:
        own = pl.BlockSpec((tr, c), lambda i, s: ((4 * s[0] + 2 * s[1] + s[2]) * nb + i, 0))
    return pl.pallas_call(
        body, name=name,
        grid_spec=pltpu.PrefetchScalarGridSpec(
            num_scalar_prefetch=1, grid=(nb,),
            in_specs=[own, pl.BlockSpec((7, tr, c), lambda i, s: (0, i, 0))],
            out_specs=pl.BlockSpec((tr, c), lambda i, s: (s[2] * nb + i, 0))),
        out_shape=jax.ShapeDtypeStruct((2 * r, c), F32),
        compiler_params=_cparams(("arbitrary",)),
    )(xyc, gfull, recv)


def _place_own(shard, slot, dtype, name):
    r, c = shard.shape
    tr = _pick(r, (256, 352, 320, 128, 32))

    def body(_, x_ref, o_ref):
        o_ref[...] = x_ref[...].astype(dtype)

    return pl.pallas_call(
        body, name=name,
        grid_spec=pltpu.PrefetchScalarGridSpec(
            num_scalar_prefetch=1, grid=(r // tr,),
            in_specs=[pl.BlockSpec((tr, c), lambda i, s: (i, 0))],
            out_specs=pl.BlockSpec((None, tr, c), lambda i, s: (s[0], i, 0))),
        out_shape=jax.ShapeDtypeStruct((4, r, c), dtype),
        compiler_params=_cparams(("arbitrary",)),
    )(slot, shard)


BIG = ("w_in", "w_proj_a", "w_proj_b", "w_out", "w_ffn_in", "w_ffn_down")
COL_SHARDED = ("w_in", "w_ffn_in")


def _coords():
    return lax.axis_index("x"), lax.axis_index("y"), lax.axis_index("c")


def _gather_host(bufs):
    n = len(bufs)

    def place():
        x, y, c = _coords()
        return x, y, c, [(1 - x, y), (x, 1 - y), (1 - x, 1 - y)]

    def slab(b, chip, half_of):
        half = b.shape[1] // 2
        return b.at[chip, pl.ds(pl.multiple_of(half_of * half, 8), half)]

    def copy(b, sems, w, k, chip, half_of, to):
        s = slab(b, chip, half_of)
        return pltpu.make_async_remote_copy(src_ref=s, dst_ref=s, send_sem=sems[0].at[w, k], recv_sem=sems[1].at[w, k],
                                            device_id=to, device_id_type=MESH)

    def start(_, outs, sems):
        x, y, c, chips = place()
        for w, b in enumerate(outs):
            for k, (cx, cy) in enumerate(chips):
                copy(b, sems, w, k, 2 * x + y, c, (cx, cy, c)).start()

    def mid(_, outs, sems):
        x, y, c, chips = place()
        for w, b in enumerate(outs):
            for k, (cx, cy) in enumerate(chips):
                copy(b, sems, w, k, 2 * cx + cy, c, (cx, cy, c)).wait_recv()
                copy(b, sems, w, 3 + k, 2 * cx + cy, c, (x, y, 1 - c)).start()

    def finish(_, outs, sems):
        x, y, c, chips = place()
        for w, b in enumerate(outs):
            for k, (cx, cy) in enumerate(chips):
                copy(b, sems, w, 3 + k, 2 * cx + cy, 1 - c, (x, y, 1 - c)).wait_recv()
        for w, b in enumerate(outs):
            for k, (cx, cy) in enumerate(chips):
                copy(b, sems, w, k, 2 * x + y, c, (cx, cy, c)).wait_send()
                copy(b, sems, w, 3 + k, 2 * cx + cy, c, (x, y, 1 - c)).wait_send()

    return _Host(ins=list(bufs), out_shapes=[jax.ShapeDtypeStruct(b.shape, b.dtype) for b in bufs],
                 aliases={i: i for i in range(n)},
                 sems=[pltpu.SemaphoreType.DMA((n, 6)), pltpu.SemaphoreType.DMA((n, 6))],
                 start=start, mid=mid, finish=finish)


def _run_host(host, name):
    n_in, n_out = len(host.ins), len(host.out_shapes)

    def body(*refs):
        ins, outs, sems = refs[:n_in], refs[n_in:n_in + n_out], refs[n_in + n_out:]
        host.start(ins, outs, sems)
        if host.mid is not None:
            host.mid(ins, outs, sems)
        host.finish(ins, outs, sems)

    return pl.pallas_call(
        body, name=name, in_specs=[ANY] * n_in, out_specs=[ANY] * n_out, out_shape=list(host.out_shapes),
        scratch_shapes=list(host.sems), input_output_aliases=dict(host.aliases),
    )(*host.ins)


def _peer(x, y, c, k):
    fx, fy, fc = (k >> 2) & 1, (k >> 1) & 1, k & 1
    return (x ^ fx if fx else x, y ^ fy if fy else y, c ^ fc if fc else c)


def _sub_shape(name, full_shape):
    r, c = full_shape
    return (r // 2, c // 4) if name in COL_SHARDED else (r // 8, c)


def _exchange_host(names, grads):
    n = len(names)
    shapes = [_sub_shape(k, g.shape) for k, g in zip(names, grads)]

    def copy(ins, outs, sems, w, k):
        x, y, c = _coords()
        px, py, pc = _peer(x, y, c, k)
        sr, sc = shapes[w]
        if names[w] in COL_SHARDED:
            col = _col_block(names[w], px, py) * sc
            src = ins[w].at[pl.ds(pl.multiple_of(pc * sr, 16), sr), pl.ds(pl.multiple_of(col, 128), sc)]
        else:
            src = ins[w].at[pl.ds(pl.multiple_of((4 * px + 2 * py + pc) * sr, 16), sr)]
        return pltpu.make_async_remote_copy(
            src_ref=src, dst_ref=outs[w].at[k - 1], send_sem=sems[0].at[w, k - 1], recv_sem=sems[1].at[w, k - 1],
            device_id=(px, py, pc), device_id_type=MESH)

    def start(ins, outs, sems):
        for w in range(n):
            for k in range(1, 8):
                copy(ins, outs, sems, w, k).start()

    def finish(ins, outs, sems):
        for w in range(n):
            for k in range(1, 8):
                copy(ins, outs, sems, w, k).wait_recv()
        for w in range(n):
            for k in range(1, 8):
                copy(ins, outs, sems, w, k).wait_send()

    return _Host(ins=list(grads), out_shapes=[jax.ShapeDtypeStruct((7,) + s, g.dtype) for s, g in zip(shapes, grads)],
                 aliases={}, sems=[pltpu.SemaphoreType.DMA((n, 7)), pltpu.SemaphoreType.DMA((n, 7))],
                 start=start, mid=None, finish=finish)


def _sibling_host(bufs):
    n = len(bufs)

    def copy(outs, sems, w, half_of):
        x, y, c = _coords()
        half = outs[w].shape[0] // 2
        rows = outs[w].at[pl.ds(pl.multiple_of(half_of * half, 8), half)]
        return pltpu.make_async_remote_copy(src_ref=rows, dst_ref=rows, send_sem=sems[0].at[w], recv_sem=sems[1].at[w],
                                            device_id=(x, y, 1 - c), device_id_type=MESH)

    def start(_, outs, sems):
        c = lax.axis_index("c")
        for w in range(n):
            copy(outs, sems, w, c).start()

    def finish(_, outs, sems):
        c = lax.axis_index("c")
        for w in range(n):
            copy(outs, sems, w, 1 - c).wait_recv()
        for w in range(n):
            copy(outs, sems, w, c).wait_send()

    return _Host(ins=list(bufs), out_shapes=[jax.ShapeDtypeStruct(b.shape, b.dtype) for b in bufs],
                 aliases={i: i for i in range(n)},
                 sems=[pltpu.SemaphoreType.DMA((n,)), pltpu.SemaphoreType.DMA((n,))], start=start, mid=None, finish=finish)


def _pack_host(pack):
    def me_of():
        x, y, c = _coords()
        return x, y, c, 4 * x + 2 * y + c

    def copy(ins, outs, sems, k, slot):
        x, y, c, _ = me_of()
        return pltpu.make_async_remote_copy(src_ref=ins[0], dst_ref=outs[0].at[slot], send_sem=sems[0].at[k - 1],
                                            recv_sem=sems[1].at[k - 1], device_id=_peer(x, y, c, k), device_id_type=MESH)

    def start(ins, outs, sems):
        me = me_of()[3]
        pltpu.make_async_copy(ins[0], outs[0].at[me], sems[2]).start()
        for k in range(1, 8):
            copy(ins, outs, sems, k, me).start()

    def finish(ins, outs, sems):
        x, y, c, me = me_of()
        for k in range(1, 8):
            px, py, pc = _peer(x, y, c, k)
            copy(ins, outs, sems, k, 4 * px + 2 * py + pc).wait_recv()
        for k in range(1, 8):
            copy(ins, outs, sems, k, me).wait_send()
        pltpu.make_async_copy(ins[0], outs[0].at[me], sems[2]).wait()

    return _Host(ins=[pack], out_shapes=[jax.ShapeDtypeStruct((8,) + pack.shape, F32)], aliases={},
                 sems=[pltpu.SemaphoreType.DMA((7,)), pltpu.SemaphoreType.DMA((7,)), pltpu.SemaphoreType.DMA],
                 start=start, mid=None, finish=finish)


def _join_hosts(hosts):
    ins, outs, sems, aliases, spans = [], [], [], {}, []
    for h in hosts:
        spans.append((len(ins), len(h.ins), len(outs), len(h.out_shapes), len(sems), len(h.sems)))
        for i_in, i_out in h.aliases.items():
            aliases[len(ins) + i_in] = len(outs) + i_out
        ins, outs, sems = ins + list(h.ins), outs + list(h.out_shapes), sems + list(h.sems)

    def phase(which):
        fns = [getattr(h, which) for h in hosts]
        if all(f is None for f in fns):
            return None

        def run(i_refs, o_refs, s_refs):
            for f, (i0, n_i, o0, n_o, s0, n_s) in zip(fns, spans):
                if f is not None:
                    f(i_refs[i0:i0 + n_i], o_refs[o0:o0 + n_o], s_refs[s0:s0 + n_s])

        return run

    return _Host(ins, outs, aliases, sems, phase("start"), phase("mid"), phase("finish"))


def _allgather_pack(pack):
    rows = pack.shape[0]

    def body(p_ref, o_ref, send, recv, local):
        x, y, c = _coords()
        me = 4 * x + 2 * y + c
        mine = pltpu.make_async_copy(p_ref, o_ref.at[me], local)
        mine.start()

        def copy(k, slot):
            return pltpu.make_async_remote_copy(src_ref=p_ref, dst_ref=o_ref.at[slot], send_sem=send.at[k - 1],
                                                recv_sem=recv.at[k - 1], device_id=_peer(x, y, c, k), device_id_type=MESH)

        for k in range(1, 8):
            copy(k, me).start()
        for k in range(1, 8):
            px, py, pc = _peer(x, y, c, k)
            copy(k, 4 * px + 2 * py + pc).wait_recv()
        for k in range(1, 8):
            copy(k, me).wait_send()
        mine.wait()

    return pl.pallas_call(
        body, name="allgather_pack", in_specs=[pl.BlockSpec(memory_space=pltpu.VMEM)], out_specs=ANY,
        out_shape=jax.ShapeDtypeStruct((8,) + pack.shape, F32),
        scratch_shapes=[pltpu.SemaphoreType.DMA((7,)), pltpu.SemaphoreType.DMA((7,)), pltpu.SemaphoreType.DMA],
        compiler_params=pltpu.CompilerParams(vmem_limit_bytes=VMEM_LIMIT),
    )(pack)


SHARDED_SMALL = ("meta_tokens", "conv_w")

ROWS = {"conv_b": (1, RGW), "rg_ba": (1, RGW), "rg_bx": (1, RGW), "rg_lambda": (1, RGW), "conv_w": (4, RGW),
        "norm2_g": (1, D), "norm_f_g": (1, D), "hg_lb_logits": (2, D),
        "rg_wa": (RGW * RG_BLOCK // 128, 128), "rg_wx": (RGW * RG_BLOCK // 128, 128), "hg_norm_g": (1, 128),
        "norm1_g": (1, D), "meta_tokens": (N_META, D), "loss": (1, D)}
PACKS = {"early_wide": ("conv_b", "rg_ba", "rg_bx", "rg_lambda", "conv_w"),
         "early_model": ("norm2_g", "norm_f_g", "hg_lb_logits"),
         "early_lane": ("rg_wa", "rg_wx", "hg_norm_g"),
         "late": ("norm1_g", "meta_tokens", "loss")}
EARLY = ("early_wide", "early_model", "early_lane")
NO_UPDATE = ("conv_w", "meta_tokens", "loss")


def _pack_rows(vals, names):
    parts, first, row = [], {}, 0
    for k in names:
        first[k] = row
        parts.append(vals[k].reshape(ROWS[k]).astype(F32))
        row += ROWS[k][0]
    if row % 8:
        parts.append(jnp.zeros((-row % 8, ROWS[names[0]][1]), F32))
    return jnp.concatenate(parts, axis=0), first


ORDER = ("meta_tokens", "norm1_g", "w_in", "conv_w", "conv_b", "rg_wa", "rg_ba", "rg_wx", "rg_bx", "rg_lambda",
         "hg_lb_logits", "hg_norm_g", "w_proj_a", "w_proj_b", "w_out", "norm2_g", "w_ffn_in", "w_ffn_down", "norm_f_g")


def kernel(x, meta_tokens, norm1_g, w_in, conv_w, conv_b, rg_wa, rg_ba, rg_wx, rg_bx, rg_lambda, hg_lb_logits, hg_norm_g, w_proj_a, w_proj_b, w_out, norm2_g, w_ffn_in, w_ffn_down, norm_f_g, loss_target, m_meta_tokens, m_norm1_g, m_w_in, m_conv_w, m_conv_b, m_rg_wa, m_rg_ba, m_rg_wx, m_rg_bx, m_rg_lambda, m_hg_lb_logits, m_hg_norm_g, m_w_proj_a, m_w_proj_b, m_w_out, m_norm2_g, m_w_ffn_in, m_w_ffn_down, m_norm_f_g, v_meta_tokens, v_norm1_g, v_w_in, v_conv_w, v_conv_b, v_rg_wa, v_rg_ba, v_rg_wx, v_rg_bx, v_rg_lambda, v_hg_lb_logits, v_hg_norm_g, v_w_proj_a, v_w_proj_b, v_w_out, v_norm2_g, v_w_ffn_in, v_w_ffn_down, v_norm_f_g):
    args = dict(locals())
    w = {k: args[k] for k in ORDER}
    m = {k: args["m_" + k] for k in ORDER}
    v = {k: args["v_" + k] for k in ORDER}
    xi, yi, ci = _coords()
    chip = 2 * xi + yi
    slot = jnp.reshape(chip, (1,)).astype(jnp.int32)
    xyc = jnp.stack([xi, yi, ci]).astype(jnp.int32)

    def full_matrix(k, gth):
        if k == "w_ffn_in":
            return jnp.concatenate([gth[0], gth[2], gth[1], gth[3]], axis=1)
        if k in COL_SHARDED:
            return jnp.concatenate([gth[j] for j in range(4)], axis=1)
        return gth.reshape(-1, gth.shape[2])

    tiny = jnp.zeros((32, 384), F32)
    tiny = tiny.at[0:N_META, 0:256].set(meta_tokens).at[N_META:N_META + 4, 0:320].set(conv_w[0])
    slots = {k: _place_own(w[k][0], slot, BF16, "place_" + k) for k in BIG}
    w_in_all, tiny_all = _run_host(_gather_host([slots["w_in"], _place_own(tiny, slot, F32, "place_tiny")]),
                                   "allgather_w_in")
    later = [k for k in BIG if k != "w_in"]
    meta_full = jnp.transpose(tiny_all[:, 0:N_META, 0:256], (1, 0, 2)).reshape(N_META, D)
    conv_w_full = jnp.transpose(tiny_all[:, N_META:N_META + 4, 0:320], (1, 0, 2)).reshape(4, RGW)

    small = dict(norm1_g=norm1_g, conv_w=conv_w_full, conv_b=conv_b, rg_wa=rg_wa[0], rg_ba=rg_ba, rg_wx=rg_wx[0],
                 rg_bx=rg_bx, rg_lambda=rg_lambda, hg_lb_logits=hg_lb_logits, hg_norm_g=hg_norm_g,
                 norm2_g=norm2_g, norm_f_g=norm_f_g.reshape(1, D))

    hosts = {
        "mm_proj": lambda g: _gather_host([slots[k] for k in later]),
        "mm_d_act": lambda g: _exchange_host(["w_ffn_down"], [g["w_ffn_down"]]),
        "mm_d_hn2": lambda g: _exchange_host(["w_ffn_in"], [g["w_ffn_in"]]),
        "mm_d_win": lambda g: _join_hosts([
            _exchange_host(["w_out", "w_proj_a", "w_proj_b"], [g["w_out"], g["w_proj_a"], g["w_proj_b"]]),
            *[_pack_host(_pack_rows(g, PACKS[p])[0]) for p in EARLY]]),
        "mm_d_hn1": lambda g: _exchange_host(["w_in"], [_unrotate_g_in(g["w_in"])]),
    }
    h0 = jnp.concatenate([jnp.zeros((PAD, D), F32), meta_full, x[0]], axis=0)
    loss_blk, grad_x, g, carried = _local_step(
        h0, loss_target[0], {"w_in": _rotate_w_in(full_matrix("w_in", w_in_all))}, small, hosts=hosts,
        finalize=lambda gathered: {k: full_matrix(k, gth) for k, gth in zip(later, gathered)})
    g["loss"] = jnp.broadcast_to(loss_blk[0:1, 0:1], (1, D))

    halves = {}
    sources = {"mm_d_act": ["w_ffn_down"], "mm_d_hn2": ["w_ffn_in"], "mm_d_win": ["w_out", "w_proj_a", "w_proj_b"],
               "mm_d_hn1": ["w_in"]}
    for name, keys in sources.items():
        partials, received = carried[name]
        for k, part, rec in zip(keys, partials, received):
            halves[k] = _sum_place(rec, part, k, xyc, "sum_" + k)
    g_big = dict(zip(BIG, _run_host(_sibling_host([halves[k] for k in BIG]), "sibling_swap")))
    late, _ = _pack_rows(g, PACKS["late"])
    recv_packs = dict(zip(EARLY, carried["mm_d_win"][1][3:]), late=_allgather_pack(late))

    grad, delta, new_m, new_v, summed = {}, {}, {}, {}, {}
    for pack, names in PACKS.items():
        row, params, extras = 0, [], []
        for k in names:
            if k in NO_UPDATE:
                extras.append((row, ROWS[k][0]))
            else:
                params.append((row,) + tuple(d[k].reshape(ROWS[k]) for d in (w, m, v)))
            row += ROWS[k][0]
        updated, rows_only = _adamw_rows(recv_packs[pack], params, extras, "adamw_" + pack)
        for k, res in zip([k for k in names if k not in NO_UPDATE], updated):
            grad[k], delta[k], new_m[k], new_v[k] = (a.reshape(w[k].shape) for a in res)
        summed.update(zip([k for k in names if k in NO_UPDATE], rows_only))
    loss = summed["loss"][0, 0]
    g_shard = {"meta_tokens": lax.dynamic_slice(summed["meta_tokens"], (0, chip * 256), (N_META, 256)),
               "conv_w": lax.dynamic_slice(summed["conv_w"], (0, chip * 320), (4, 320))}
    for k in BIG + SHARDED_SMALL:
        gk = g_big[k] if k in BIG else g_shard[k]
        wk, mk, vk = (d[k].reshape(gk.shape) for d in (w, m, v))
        dk, mk, vk = _adamw(wk, gk, mk, vk, "adamw_" + k)
        grad[k], delta[k], new_m[k], new_v[k] = (a.reshape(w[k].shape) for a in (gk, dk, mk, vk))

    return (loss, grad_x[None], *[grad[k] for k in ORDER], *[delta[k] for k in ORDER],
            *[new_m[k] for k in ORDER], *[new_v[k] for k in ORDER])
```

```python
import functools

import jax
import jax.numpy as jnp
from jax import lax
from jax.experimental import pallas as pl
from jax.experimental.pallas import tpu as pltpu

F32, BF16 = jnp.float32, jnp.bfloat16
D = 1024
N_META = 16
RGW = 1280
RG_BLOCKS, RG_BLOCK = 16, 80
RG_C = 8.0
HEADS, HD = 8, 128
HGW = HEADS * HD
DFF = 2816
D_IN = 2 * RGW + 6 * D
ROT = 2 * RGW
COL_PAD = 256
D_INP = D_IN + COL_PAD
EPS = 1e-6
CH = 64
TM = 256
ROW0 = TM
PAD = ROW0 - N_META
CT = RGW
RS = 16
NCT = RGW // CT
EXP_CLAMP = 80.0
VMEM_LIMIT = 56 * 1024 * 1024

LR, B1, B2, ADAM_EPS, WD, STEP = 0.001, 0.9, 0.999, 1e-08, 0.01, 10
MESH = pl.DeviceIdType.MESH
ANY = pl.BlockSpec(memory_space=pl.ANY)


def _cparams(sem):
    return pltpu.CompilerParams(dimension_semantics=sem, vmem_limit_bytes=VMEM_LIMIT)


def _pick(n, prefs):
    for p in prefs:
        if n % p == 0:
            return p
    return n


def _sig(x):
    return 0.5 * jnp.tanh(0.5 * x) + 0.5


def _dot(a, b, dims):
    return lax.dot_general(a, b, (dims, ((), ())), preferred_element_type=F32)


NN, NT, TN_ = ((1,), (0,)), ((1,), (1,)), ((0,), (0,))


class _Host:
    def __init__(self, ins, out_shapes, aliases, sems, start, mid, finish):
        self.ins, self.out_shapes, self.aliases, self.sems = ins, out_shapes, aliases, sems
        self.start, self.mid, self.finish = start, mid, finish


class _Epi:
    def __init__(self, ins, in_specs, out_shapes, out_specs, fn, sequential=False, split=1):
        self.ins, self.in_specs, self.out_shapes, self.out_specs, self.fn = ins, in_specs, out_shapes, out_specs, fn
        self.sequential = sequential
        self.split = split


class _Pro:
    def __init__(self, ins, in_specs, fn):
        self.ins, self.in_specs, self.fn = ins, in_specs, fn


def _norm_pro(gain):
    def fn(a_ref, ins):
        xh, _ = _rms(a_ref[...])
        return (xh * ins[0][...]).astype(BF16)

    return _Pro([gain], [pl.BlockSpec((1, gain.shape[1]), lambda i, j, k: (0, 0))], fn)


def _mm(a, b, mode, out_dtype, name, resid=None, host=None, epi=None, tiles=None, sparse=None, pro=None):
    if mode == "nn":
        (m, kd), n = a.shape, b.shape[1]
    elif mode == "nt":
        (m, kd), n = a.shape, b.shape[0]
    else:
        (kd, m), n = a.shape, b.shape[1]
    if mode == "tn":
        tm = _pick(m, (1024, 1280, 1408, 640, 512, 256, 128))
        tk = _pick(kd, (1408, 768, 512, 256, 128))
    else:
        tm = _pick(m, (768, 512, 640, 256, 128))
        tk = kd if kd <= 2816 else _pick(kd, (1792, 1408, 1024, 512, 256, 128))
    tn = _pick(n, (1792, 1408, 1280, 1024, 512, 256, 128))
    if tiles is not None:
        tm, tn = tiles
    a_map = (lambda i, j, k: (k, i)) if mode == "tn" else (lambda i, j, k: (i, k))
    b_map = (lambda i, j, k: (j, k)) if mode == "nt" else (lambda i, j, k: (k, j))
    o_map = lambda i, j, k: (i, j)
    if sparse is not None:
        (ni, nj, nk), (tm, tn, tk), (a_map, b_map, o_map) = sparse
    else:
        assert m % tm == 0 and n % tn == 0 and kd % tk == 0, (name, m, n, kd, tm, tn, tk)
        ni, nj, nk = m // tm, n // tn, kd // tk
    dims = {"nn": NN, "nt": NT, "tn": TN_}[mode]

    n_hin = len(host.ins) if host else 0
    n_hout = len(host.out_shapes) if host else 0
    n_res = 0 if resid is None else 1
    n_pro = 0 if pro is None else 1
    n_pin = len(pro.ins) if pro else 0
    n_ein = len(epi.ins) if epi else 0
    n_out = len(epi.out_shapes) if epi else 1
    assert not n_pro or (mode == "nn" and nk == 1)

    def finish(r, r_ref, e_in, o_refs, i, rows=slice(None), first=True):
        if resid is not None:
            r = r + r_ref[rows, :]
        if epi:
            epi.fn(r, e_in, o_refs, i, rows, first)
        else:
            o_refs[0][rows, :] = r.astype(out_dtype)

    def body(*refs):
        a_ref, b_ref = refs[:2]
        r_ref = refs[2] if resid is not None else None
        pos = 2 + n_res
        p_in = refs[pos:pos + n_pin]
        pos += n_pin
        e_in = refs[pos:pos + n_ein]
        pos += n_ein
        h_in = refs[pos:pos + n_hin]
        pos += n_hin
        o_ref = refs[pos:pos + n_out]
        pos += n_out
        hn_ref = refs[pos] if n_pro else None
        pos += n_pro
        h_out = refs[pos:pos + n_hout]
        scratch = refs[pos + n_hout:]
        n_acc = 1 if nk > 1 else 0
        h_sems = scratch[n_acc + n_pro:]
        i, j, k = pl.program_id(0), pl.program_id(1), pl.program_id(2)
        if host:
            @pl.when((i == 0) & (j == 0) & (k == 0))
            def _():
                host.start(h_in, h_out, h_sems)

            if host.mid is not None:
                @pl.when((i == (3 * ni) // 4) & (j == 0) & (k == 0))
                def _():
                    host.mid(h_in, h_out, h_sems)

        if n_pro:
            a_s = scratch[n_acc]

            @pl.when(j == 0)
            def _():
                a_s[...] = pro.fn(a_ref, p_in)
                hn_ref[...] = a_s[...]

            a_val = a_s[...]
        else:
            a_val = a_ref[...]

        if nk == 1 and epi and epi.split > 1 and mode != "tn":
            part = tm // epi.split
            for p in range(epi.split):
                rows = slice(p * part, (p + 1) * part)
                finish(_dot(a_val[rows, :], b_ref[...], dims), r_ref, e_in, o_ref, i, rows, p == 0)
        elif nk == 1:
            finish(_dot(a_val, b_ref[...], dims), r_ref, e_in, o_ref, i)
        else:
            acc = scratch[0]

            @pl.when(k == 0)
            def _():
                acc[...] = jnp.zeros_like(acc)

            acc[...] += _dot(a_val, b_ref[...], dims)

            @pl.when(k == nk - 1)
            def _():
                finish(acc[...], r_ref, e_in, o_ref, i)

        if host:
            @pl.when((i == ni - 1) & (j == nj - 1) & (k == nk - 1))
            def _():
                host.finish(h_in, h_out, h_sems)

    a_spec = pl.BlockSpec((tk, tm) if mode == "tn" else (tm, tk), a_map)
    b_spec = pl.BlockSpec((tn, tk) if mode == "nt" else (tk, tn), b_map)
    o_spec = pl.BlockSpec((tm, tn), o_map)
    in_specs, args = [a_spec, b_spec], [a, b]
    if resid is not None:
        in_specs.append(o_spec)
        args.append(resid)
    if n_pro:
        in_specs += list(pro.in_specs)
        args += list(pro.ins)
    out_shapes, out_specs = [jax.ShapeDtypeStruct((m, n), out_dtype)], [o_spec]
    if epi:
        in_specs += list(epi.in_specs)
        args += list(epi.ins)
        out_shapes, out_specs = list(epi.out_shapes), list(epi.out_specs)
    if n_pro:
        out_shapes.append(jax.ShapeDtypeStruct((m, kd), BF16))
        out_specs.append(pl.BlockSpec((tm, kd), lambda i, j, k: (i, 0)))
    scratch = ([pltpu.VMEM((tm, tn), F32)] if nk > 1 else []) + ([pltpu.VMEM((tm, kd), BF16)] if n_pro else [])
    sequential = host or n_pro or (epi and epi.sequential)
    outs = pl.pallas_call(
        body, name=name, grid=(ni, nj, nk),
        in_specs=in_specs + [ANY] * n_hin, out_specs=out_specs + [ANY] * n_hout,
        out_shape=out_shapes + (list(host.out_shapes) if host else []),
        scratch_shapes=scratch + (list(host.sems) if host else []),
        input_output_aliases=({2 + n_res + n_pin + n_ein + i_in: n_out + n_pro + i_out
                               for i_in, i_out in host.aliases.items()} if host else {}),
        compiler_params=_cparams(("arbitrary",) * 3 if sequential else ("parallel", "parallel", "arbitrary")),
    )(*args, *(host.ins if host else []))
    main = list(outs[:n_out]) if epi else outs[0]
    if n_pro:
        main = (main, outs[n_out])
    return (main, list(outs[n_out + n_pro:])) if host else main


def _rms(x):
    r = lax.rsqrt(jnp.mean(x * x, axis=-1, keepdims=True) + EPS)
    return x * r, r


def _rms_bwd_math(dhn, x, g):
    xh, r = _rms(x)
    dxh = dhn * g
    dx = r * (dxh - xh * jnp.mean(dxh * xh, axis=-1, keepdims=True))
    return dx, jnp.sum(dhn * xh, axis=0, keepdims=True)


def _row_specs(tm):
    return pl.BlockSpec((tm, D), lambda i, j, k: (i, 0)), pl.BlockSpec((1, D), lambda i, j, k: (0, 0))


def _final_loss_epi(gf, target, t):
    def fn(x, ins, outs, i, rows, first):
        g_ref, t_ref = ins
        dh_ref, dhb_ref, loss_ref, dg_ref = outs

        if first:
            @pl.when(i == 0)
            def _():
                loss_ref[...] = jnp.zeros_like(loss_ref)
                dg_ref[...] = jnp.zeros_like(dg_ref)

        g = g_ref[...]
        xh, _ = _rms(x)
        err = jnp.where(i > 0, xh * g - t_ref[rows, :], 0.0)
        loss_ref[...] += 0.5 * jnp.sum(jnp.sum(err * err, axis=-1, keepdims=True) * (1.0 / D))
        dx, dg = _rms_bwd_math(err * (1.0 / D), x, g)
        dh_ref[rows, :] = dx
        dhb_ref[rows, :] = dx.astype(BF16)
        dg_ref[...] += dg

    row, vec = _row_specs(TM)
    return _Epi(ins=[gf, target], in_specs=[vec, pl.BlockSpec((TM, D), lambda i, j, k: (jnp.maximum(i - 1, 0), 0))],
                out_shapes=[jax.ShapeDtypeStruct((t, D), F32), jax.ShapeDtypeStruct((t, D), BF16),
                            jax.ShapeDtypeStruct((8, 128), F32), jax.ShapeDtypeStruct((1, D), F32)],
                out_specs=[row, row, pl.BlockSpec((8, 128), lambda i, j, k: (0, 0)), vec], fn=fn, sequential=True)


def _rms_bwd_epi(h, g, dres, tm):
    t = h.shape[0]

    def fn(dhn, ins, outs, i, rows, first):
        h_ref, g_ref, dres_ref = ins
        dh_ref, dhb_ref, dg_ref = outs

        if first:
            @pl.when(i == 0)
            def _():
                dg_ref[...] = jnp.zeros_like(dg_ref)

        dx, dg = _rms_bwd_math(dhn, h_ref[rows, :], g_ref[...])
        dh = dres_ref[rows, :] + dx
        dh_ref[rows, :] = dh
        dhb_ref[rows, :] = dh.astype(BF16)
        dg_ref[...] += dg

    row, vec = _row_specs(tm)
    return _Epi(ins=[h, g, dres], in_specs=[row, vec, row],
                out_shapes=[jax.ShapeDtypeStruct((t, D), F32), jax.ShapeDtypeStruct((t, D), BF16),
                            jax.ShapeDtypeStruct((1, D), F32)],
                out_specs=[row, row, vec], fn=fn, sequential=True)


def _rmsnorm1_bwd(dhn, h, g, dres):
    t = h.shape[0]

    def body(dhn_ref, h_ref, g_ref, dres_ref, dx_ref, dmeta_ref, dg_ref):
        i = pl.program_id(0)

        @pl.when(i == 0)
        def _():
            dg_ref[...] = jnp.zeros_like(dg_ref)

        dx, dg = _rms_bwd_math(dhn_ref[...].astype(F32), h_ref[...], g_ref[...])
        dh = dres_ref[...] + dx
        dg_ref[...] += dg
        dx_ref[...] = dh

        @pl.when(i == 0)
        def _():
            dmeta_ref[...] = dh[PAD:, :]

    row = pl.BlockSpec((TM, D), lambda i: (i, 0))
    vec = pl.BlockSpec((1, D), lambda i: (0, 0))
    return pl.pallas_call(
        body, name="rmsnorm1_bwd", grid=(t // TM,),
        in_specs=[row, row, vec, row],
        out_specs=[pl.BlockSpec((TM, D), lambda i: (jnp.maximum(i - 1, 0), 0)),
                   pl.BlockSpec((N_META, D), lambda i: (0, 0)), vec],
        out_shape=[jax.ShapeDtypeStruct((t - ROW0, D), F32), jax.ShapeDtypeStruct((N_META, D), F32),
                   jax.ShapeDtypeStruct((1, D), F32)],
        compiler_params=_cparams(("arbitrary",)),
    )(dhn, h, g, dres)


FH = DFF // 2
FM = 384


def _swiglu_epi(t):
    def fn(r, _, outs, i, rows, first):
        g, u = r[:, :FH], r[:, FH:]
        outs[0][rows, :] = r.astype(BF16)
        outs[1][rows, :] = (g * _sig(g) * u).astype(BF16)

    return _Epi(ins=[], in_specs=[],
                out_shapes=[jax.ShapeDtypeStruct((t, 2 * DFF), BF16), jax.ShapeDtypeStruct((t, DFF), BF16)],
                out_specs=[pl.BlockSpec((FM, 2 * FH), lambda i, j, k: (i, j)), pl.BlockSpec((FM, FH), lambda i, j, k: (i, j))],
                fn=fn)


def _swiglu_bwd_epi(gu):
    def fn(d, ins, outs, i, rows, first):
        gu_t = ins[0][rows, :].astype(F32)
        g, u = gu_t[:, :FH], gu_t[:, FH:]
        s = _sig(g)
        outs[0][rows, :FH] = (d * u * (s * (1.0 + g * (1.0 - s)))).astype(BF16)
        outs[0][rows, FH:] = (d * (g * s)).astype(BF16)

    spec = pl.BlockSpec((FM, 2 * FH), lambda i, j, k: (i, j))
    return _Epi(ins=[gu], in_specs=[spec], out_shapes=[jax.ShapeDtypeStruct(gu.shape, BF16)], out_specs=[spec], fn=fn)


COL_GA, COL_GB = 4, 5
COL_AX, COL_AG = (6 * D + COL_PAD) // CT, (6 * D + COL_PAD + RGW) // CT


def _merge_pro(pb, proj, tm):
    def fn(a_ref, ins):
        pa, pb_, ga, gb = (r[...].astype(F32) for r in (a_ref,) + tuple(ins))
        return (_sig(ga) * pa + _sig(gb) * pb_).astype(BF16)

    row, _ = _row_specs(tm)
    return _Pro([pb, proj, proj], [row, pl.BlockSpec((tm, D), lambda i, j, k: (i, COL_GA)),
                                   pl.BlockSpec((tm, D), lambda i, j, k: (i, COL_GB))], fn)


def _merge_bwd_epi(pa, pb, proj, tm):
    t = pa.shape[0]

    def fn(d, ins, outs, i, rows, first):
        pa, pb, ga, gb = (r[rows, :].astype(F32) for r in ins)
        dpa_ref, dpb_ref, dproj_ref = outs
        sa, sb = _sig(ga), _sig(gb)
        dpa_ref[rows, :] = (sa * d).astype(BF16)
        dpb_ref[rows, :] = (sb * d).astype(BF16)
        dproj_ref[rows, :D] = (d * pa * sa * (1.0 - sa)).astype(BF16)
        dproj_ref[rows, D:] = (d * pb * sb * (1.0 - sb)).astype(BF16)

    row, _ = _row_specs(tm)
    return _Epi(ins=[pa, pb, proj, proj],
                in_specs=[row, row, pl.BlockSpec((tm, D), lambda i, j, k: (i, COL_GA)),
                          pl.BlockSpec((tm, D), lambda i, j, k: (i, COL_GB))],
                out_shapes=[jax.ShapeDtypeStruct((t, D), BF16), jax.ShapeDtypeStruct((t, D), BF16),
                            jax.ShapeDtypeStruct((t, D_INP), BF16)],
                out_specs=[row, row, pl.BlockSpec((tm, 2 * D), lambda i, j, k: (i, 2))], fn=fn)


def _zero_pad_cols(dproj):
    t = dproj.shape[0]
    tz = _pick(t, (768, 256))

    def body(_, o_ref):
        o_ref[...] = jnp.zeros_like(o_ref)

    return pl.pallas_call(
        body, name="dproj_pad", grid=(t // tz,), in_specs=[ANY],
        out_specs=pl.BlockSpec((tz, COL_PAD), lambda i: (i, (6 * D) // COL_PAD)),
        out_shape=jax.ShapeDtypeStruct(dproj.shape, BF16), input_output_aliases={0: 0},
        compiler_params=_cparams(("parallel",)),
    )(dproj)


HALO = 16


def _rows_before(cur, before, s):
    n = cur.shape[0]
    return jnp.concatenate([before, cur], axis=0)[n - s:2 * n - s, :]


def _rows_after(cur, after, s):
    n = cur.shape[0]
    return jnp.concatenate([cur, after], axis=0)[s:n + s, :]


def _prev_halo(col0):
    return pl.BlockSpec((HALO, CT), lambda c, i: (jnp.maximum(i * (TM // HALO) - 1, 0), col0 + c))


def _conv_fwd(proj, w, b):
    t = proj.shape[0]

    def body(x_ref, halo_ref, w_ref, b_ref, o_ref, ob_ref):
        i = pl.program_id(1)
        wv, bv = w_ref[...], b_ref[...]
        first = jnp.where(i > 0, halo_ref[...].astype(F32), 0.0)

        def strip(k, before):
            rows = pl.ds(pl.multiple_of(k * HALO, HALO), HALO)
            x = x_ref[rows, :].astype(F32)
            y = bv + wv[3:4, :] * x
            for j in range(3):
                y = y + wv[j:j + 1, :] * _rows_before(x, before, 3 - j)
            o_ref[rows, :] = y
            ob_ref[rows, :] = y.astype(BF16)
            return x

        lax.fori_loop(0, TM // HALO, strip, first)

    blk = pl.BlockSpec((TM, CT), lambda c, i: (i, c))
    return pl.pallas_call(
        body, name="conv_fwd", grid=(NCT, t // TM),
        in_specs=[pl.BlockSpec((TM, CT), lambda c, i: (i, COL_AX + c)), _prev_halo(COL_AX),
                  pl.BlockSpec((4, CT), lambda c, i: (0, c)), pl.BlockSpec((1, CT), lambda c, i: (0, c))],
        out_specs=[blk, blk],
        out_shape=[jax.ShapeDtypeStruct((t, RGW), F32), jax.ShapeDtypeStruct((t, RGW), BF16)],
        compiler_params=_cparams(("parallel", "parallel")),
    )(proj, proj, w, b)


def _conv_bwd(dxc, proj, w, dproj):
    t = proj.shape[0]
    nt = t // TM

    def body(d_ref, dn_ref, x_ref, halo_ref, w_ref, _, dx_ref, dw_ref, db_ref, acc):
        i = pl.program_id(1)

        @pl.when(i == 0)
        def _():
            dw_ref[...] = jnp.zeros_like(dw_ref)
            db_ref[...] = jnp.zeros_like(db_ref)

        acc[...] = jnp.zeros_like(acc)
        wv = w_ref[...]
        first = jnp.where(i > 0, halo_ref[...].astype(F32), 0.0)
        last = jnp.where(i < nt - 1, dn_ref[...], 0.0)
        ns = TM // HALO

        def fold(v):
            return v[0:8, :] + v[8:16, :]

        def strip(k, before):
            off = pl.multiple_of(k * HALO, HALO)
            rows = pl.ds(off, HALO)
            x, d = x_ref[rows, :].astype(F32), d_ref[rows, :]
            nxt = d_ref[pl.ds(pl.multiple_of(jnp.minimum(off + HALO, TM - HALO), HALO), HALO), :]
            after = jnp.where(k == ns - 1, last, nxt)
            dx = wv[3:4, :] * d
            for j in range(3):
                dx = dx + wv[j:j + 1, :] * _rows_after(d, after, 3 - j)
                acc[j] += fold(d * _rows_before(x, before, 3 - j))
            acc[3] += fold(d * x)
            acc[4] += fold(d)
            dx_ref[rows, :] = dx.astype(BF16)
            return x

        lax.fori_loop(0, ns, strip, first)
        for j in range(4):
            dw_ref[j:j + 1, :] += jnp.sum(acc[j], axis=0, keepdims=True)
        db_ref[...] += jnp.sum(acc[4], axis=0, keepdims=True)

    return pl.pallas_call(
        body, name="conv_bwd", grid=(NCT, nt),
        in_specs=[pl.BlockSpec((TM, CT), lambda c, i: (i, c)),
                  pl.BlockSpec((HALO, CT), lambda c, i: (jnp.minimum((i + 1) * (TM // HALO), t // HALO - 1), c)),
                  pl.BlockSpec((TM, CT), lambda c, i: (i, COL_AX + c)), _prev_halo(COL_AX),
                  pl.BlockSpec((4, CT), lambda c, i: (0, c)), ANY],
        out_specs=[pl.BlockSpec((TM, CT), lambda c, i: (i, COL_AX + c)),
                   pl.BlockSpec((4, CT), lambda c, i: (0, c)), pl.BlockSpec((1, CT), lambda c, i: (0, c))],
        out_shape=[jax.ShapeDtypeStruct((t, D_INP), BF16), jax.ShapeDtypeStruct((4, RGW), F32),
                   jax.ShapeDtypeStruct((1, RGW), F32)],
        scratch_shapes=[pltpu.VMEM((5, 8, CT), F32)],
        input_output_aliases={5: 0},
        compiler_params=_cparams(("parallel", "arbitrary")),
    )(dxc, dxc, proj, proj, w, dproj)


def _gelu(x):
    c = 0.7978845608028654
    th = jnp.tanh(c * (x + 0.044715 * x * x * x))
    return 0.5 * x * (1.0 + th), th


def _rg_gates(gr, gi, xc, ba, bx, lam, row0):
    r = _sig(gr + ba)
    ig = _sig(gi + bx)
    sp = jax.nn.softplus(-lam)
    a = jnp.exp(-RG_C * r * sp)
    s2 = jnp.maximum(1.0 - a * a, 1e-30)
    rs = lax.rsqrt(s2)
    s = s2 * rs
    rows = row0 + lax.broadcasted_iota(jnp.int32, gr.shape, 0)
    live = rows >= PAD
    u = jnp.where(live, s * ig * xc, 0.0)
    return r, ig, sp, a, s, rs, u, live


def _rg_fwd(gri, xc, proj, ba, bx, lam):
    t = gri.shape[0]

    def body(gr_ref, gi_ref, xc_ref, ag_ref, ba_ref, bx_ref, lam_ref, h_ref, ya_ref, hc):
        i = pl.program_id(1)

        @pl.when(i == 0)
        def _():
            hc[...] = jnp.zeros_like(hc)

        ba, bx, lam = ba_ref[...], bx_ref[...], lam_ref[...]
        sub = lax.broadcasted_iota(jnp.int32, (8, CT), 0)

        def strip(k, h):
            off = pl.multiple_of(k * RS, RS)
            rows = pl.ds(off, RS)
            _, _, _, a, _, _, u, _ = _rg_gates(gr_ref[rows, :], gi_ref[rows, :], xc_ref[rows, :], ba, bx, lam,
                                               i * TM + off)
            outs = []
            for half in range(RS // 8):
                out = jnp.zeros((8, CT), F32)
                for r in range(8):
                    h = a[8 * half + r:8 * half + r + 1, :] * h + u[8 * half + r:8 * half + r + 1, :]
                    out = jnp.where(sub == r, h, out)
                outs.append(out)
            hs = jnp.concatenate(outs, axis=0)
            h_ref[rows, :] = hs
            ge, _ = _gelu(ag_ref[rows, :].astype(F32))
            ya_ref[rows, :] = (hs * ge).astype(BF16)
            return h

        hc[...] = lax.fori_loop(0, TM // RS, strip, hc[...])

    blk_ = pl.BlockSpec((TM, CT), lambda c, i: (i, c))
    vec = pl.BlockSpec((1, CT), lambda c, i: (0, c))
    return pl.pallas_call(
        body, name="rg_fwd", grid=(NCT, t // TM),
        in_specs=[blk_, pl.BlockSpec((TM, CT), lambda c, i: (i, NCT + c)), blk_,
                  pl.BlockSpec((TM, CT), lambda c, i: (i, COL_AG + c)), vec, vec, vec],
        out_specs=[blk_, blk_],
        out_shape=[jax.ShapeDtypeStruct((t, RGW), F32), jax.ShapeDtypeStruct((t, RGW), BF16)],
        scratch_shapes=[pltpu.VMEM((1, CT), F32)],
        compiler_params=_cparams(("parallel", "arbitrary")),
    )(gri, gri, xc, proj, ba, bx, lam)


def _rg_bwd(dya, h, gri, xc, proj, ba, bx, lam, dproj):
    t = gri.shape[0]
    nt = t // TM
    assert NCT == 1

    def body(dya_ref, h_ref, hh_ref, gr_ref, gi_ref, xc_ref, ag_ref, ba_ref, bx_ref, lam_ref, _,
             dgri_ref, dxc_ref, dag_ref, dba_ref, dbx_ref, dlam_ref, hbuf, acc, cc):
        i = pl.program_id(1)
        ri = nt - 1 - i

        @pl.when(i == 0)
        def _():
            cc[...] = jnp.zeros_like(cc)
            dba_ref[...] = jnp.zeros_like(dba_ref)
            dbx_ref[...] = jnp.zeros_like(dbx_ref)
            dlam_ref[...] = jnp.zeros_like(dlam_ref)

        acc[...] = jnp.zeros_like(acc)
        ba, bx, lam = ba_ref[...], bx_ref[...], lam_ref[...]
        halo_last = jnp.where(ri > 0, hh_ref[7:8, :], 0.0)
        sub = lax.broadcasted_iota(jnp.int32, (8, CT), 0)
        c0 = 0.7978845608028654

        def strip(kk, c):
            k = TM // RS - 1 - kk
            off = pl.multiple_of(k * RS, RS)
            rows = pl.ds(off, RS)
            xc, hv = xc_ref[rows, :], h_ref[rows, :]
            ag, dya = ag_ref[rows, :].astype(F32), dya_ref[rows, :].astype(F32)
            r, ig, sp, a, s, rs, _, live = _rg_gates(gr_ref[rows, :], gi_ref[rows, :], xc, ba, bx, lam, ri * TM + off)
            ge, th = _gelu(ag)
            dge = 0.5 * (1.0 + th) + 0.5 * ag * (1.0 - th * th) * c0 * (1.0 + 3.0 * 0.044715 * ag * ag)
            dag_ref[rows, :] = (dya * hv * dge).astype(BF16)
            d = dya * ge
            outs = []
            for half in range(RS // 8 - 1, -1, -1):
                out = jnp.zeros((8, CT), F32)
                for rr in range(7, -1, -1):
                    g = d[8 * half + rr:8 * half + rr + 1, :] + c
                    c = a[8 * half + rr:8 * half + rr + 1, :] * g
                    out = jnp.where(sub == rr, g, out)
                outs.insert(0, out)
            g = jnp.concatenate(outs, axis=0)
            before = h_ref[pl.ds(pl.multiple_of(jnp.maximum(off - RS, 0), RS), RS), :]
            hbuf[7:8, :] = jnp.where(k == 0, halo_last, before[RS - 1:RS, :])
            hbuf[8:, :] = hv
            hprev = hbuf[pl.ds(7, RS), :]
            du = jnp.where(live, g, 0.0)
            ds = du * (ig * xc)
            dm = du * s
            dla = (g * hprev - a * ds * rs) * a
            dr = dla * (-RG_C * sp) * r * (1.0 - r)
            di = dm * xc * ig * (1.0 - ig)
            dgri_ref[rows, :CT] = dr.astype(BF16)
            dgri_ref[rows, CT:] = di.astype(BF16)
            dxc_ref[rows, :] = dm * ig
            dl = dla * (-RG_C * r)
            for half in range(RS // 8):
                part = slice(8 * half, 8 * half + 8)
                acc[0] += dr[part]
                acc[1] += di[part]
                acc[2] += dl[part]
            return c

        cc[...] = lax.fori_loop(0, TM // RS, strip, cc[...])
        dba_ref[...] += jnp.sum(acc[0], axis=0, keepdims=True)
        dbx_ref[...] += jnp.sum(acc[1], axis=0, keepdims=True)
        dlam_ref[...] += jnp.sum(acc[2], axis=0, keepdims=True) * (-_sig(-lam))

    rblk = pl.BlockSpec((TM, CT), lambda c, i: (nt - 1 - i, c))
    vec = pl.BlockSpec((1, CT), lambda c, i: (0, c))
    hh = pl.BlockSpec((8, CT), lambda c, i: (jnp.maximum((nt - 1 - i) * (TM // 8) - 1, 0), c))
    agb = pl.BlockSpec((TM, CT), lambda c, i: (nt - 1 - i, COL_AG + c))
    return pl.pallas_call(
        body, name="rg_bwd", grid=(NCT, nt),
        in_specs=[rblk, rblk, hh, rblk, pl.BlockSpec((TM, CT), lambda c, i: (nt - 1 - i, NCT + c)), rblk, agb,
                  vec, vec, vec, ANY],
        out_specs=[pl.BlockSpec((TM, 2 * CT), lambda c, i: (nt - 1 - i, c)), rblk, agb, vec, vec, vec],
        out_shape=[jax.ShapeDtypeStruct((t, 2 * RGW), BF16),
                   jax.ShapeDtypeStruct((t, RGW), F32), jax.ShapeDtypeStruct((t, D_INP), BF16),
                   jax.ShapeDtypeStruct((1, RGW), F32), jax.ShapeDtypeStruct((1, RGW), F32),
                   jax.ShapeDtypeStruct((1, RGW), F32)],
        scratch_shapes=[pltpu.VMEM((RS + 8, CT), F32), pltpu.VMEM((3, 8, CT), F32), pltpu.VMEM((1, CT), F32)],
        input_output_aliases={10: 2},
        compiler_params=_cparams(("parallel", "arbitrary")),
    )(dya, h, h, gri, gri, xc, proj, ba, bx, lam, dproj)


NCH = TM // CH

def _tri_dot(tri, x):
    hi = x.astype(BF16)
    r1 = x - hi.astype(F32)
    mid = r1.astype(BF16)
    lo = (r1 - mid.astype(F32)).astype(BF16)
    return _dot(tri, hi, NN) + _dot(tri, mid, NN) + _dot(tri, lo, NN)


def _hg_chunk(qr, fr, lb):
    sf = _sig(fr)
    fg = lb + (1.0 - lb) * sf
    k = (1.0 - lb) * (1.0 - sf)
    sq = _sig(qr)
    q = qr * sq
    ri = lax.broadcasted_iota(jnp.int32, (CH, CH), 0)
    ci = lax.broadcasted_iota(jnp.int32, (CH, CH), 1)
    b = _tri_dot((ri >= ci).astype(BF16), jnp.log(fg))
    bm, bl = b[CH // 2 - 1:CH // 2, :], b[CH - 1:CH, :]
    ebm = jnp.exp(jnp.minimum(b - bm, EXP_CLAMP))
    ekm = jnp.exp(jnp.minimum(bm - b, EXP_CLAMP))
    eb = ebm * jnp.exp(bm)
    ekl = ekm * jnp.exp(bl - bm)
    return dict(sf=sf, fg=fg, k=k, sq=sq, q=q, eb=eb, ebm=ebm, ekm=ekm, ekl=ekl, ebl=jnp.exp(bl),
                qe=q * eb, qh=q * ebm, kh=k * ekm, kd=k * ekl, causal=ri >= ci, upper=(ci >= ri).astype(BF16))


def _hgrn_fwd(proj, lbl, gn):
    t = proj.shape[0]
    nt = t // TM

    def body(q_ref, f_ref, v_ref, g_ref, lbl_ref, gn_ref, yb_ref, o_ref, st_ref, st):
        @pl.when(pl.program_id(0) == 0)
        def _():
            st[...] = jnp.zeros_like(st)

        l = lbl_ref[...]
        lb = _sig(l[0:1, :] - l[1:2, :])
        gnv = gn_ref[...]

        def chunk(c, carry):
            off = pl.multiple_of(c * CH, CH)
            rows = pl.ds(off, CH)
            z = _hg_chunk(q_ref[rows, :].astype(F32), f_ref[rows, :].astype(F32), lb)
            v, gg = v_ref[rows, :], g_ref[rows, :].astype(F32)
            for hh in range(HEADS):
                sl = slice(hh * HD, (hh + 1) * HD)
                s_prev = st[hh]
                st_ref[c, hh] = s_prev
                vb = v[:, sl].astype(BF16)
                att = jnp.where(z["causal"], _dot(z["qh"][:, sl].astype(BF16), z["kh"][:, sl].astype(BF16), NT), 0.0)
                o = _dot(z["qe"][:, sl].astype(BF16), s_prev.astype(BF16), NT) + _dot(att.astype(BF16), vb, NN)
                st[hh] = s_prev * z["ebl"][:, sl] + _dot(vb, z["kd"][:, sl].astype(BF16), TN_)
                xh, _ = _rms(o)
                gh = gg[:, sl]
                o_ref[rows, sl] = o
                yb_ref[rows, sl] = (xh * gnv * gh * _sig(gh)).astype(BF16)
            return carry

        lax.fori_loop(0, NCH, chunk, 0, unroll=True)

    def col(j):
        return pl.BlockSpec((TM, HGW), lambda i, j=j: (i, j))

    return pl.pallas_call(
        body, name="hgrn_fwd", grid=(nt,),
        in_specs=[col(0), col(1), col(2), col(3), pl.BlockSpec((2, HGW), lambda i: (0, 0)),
                  pl.BlockSpec((1, HD), lambda i: (0, 0))],
        out_specs=[col(0), col(0), pl.BlockSpec((NCH, HEADS, HD, HD), lambda i: (i, 0, 0, 0))],
        out_shape=[jax.ShapeDtypeStruct((t, HGW), BF16), jax.ShapeDtypeStruct((t, HGW), F32),
                   jax.ShapeDtypeStruct((t // CH, HEADS, HD, HD), F32)],
        scratch_shapes=[pltpu.VMEM((HEADS, HD, HD), F32)],
        compiler_params=_cparams(("arbitrary",)),
    )(proj, proj, proj, proj, lbl, gn)


def _hgrn_bwd(dyb, proj, o, states, lbl, gn, dproj):
    t = proj.shape[0]
    nt = t // TM

    def body(dy_ref, q_ref, f_ref, v_ref, g_ref, o_ref, st_ref, lbl_ref, gn_ref, _,
             dp_ref, dgn_ref, dl_ref, dst, dlb):
        i = pl.program_id(0)

        @pl.when(i == 0)
        def _():
            dst[...] = jnp.zeros_like(dst)
            dlb[...] = jnp.zeros_like(dlb)
            dgn_ref[...] = jnp.zeros_like(dgn_ref)

        l = lbl_ref[...]
        lb = _sig(l[0:1, :] - l[1:2, :])
        gnv = gn_ref[...]
        last = lax.broadcasted_iota(jnp.int32, (CH, HD), 0) == CH - 1

        def chunk(cc, carry):
            c = NCH - 1 - cc
            off = pl.multiple_of(c * CH, CH)
            rows = pl.ds(off, CH)
            qr, fr = q_ref[rows, :].astype(F32), f_ref[rows, :].astype(F32)
            z = _hg_chunk(qr, fr, lb)
            v, gg, ov, dy = v_ref[rows, :], g_ref[rows, :].astype(F32), o_ref[rows, :], dy_ref[rows, :].astype(F32)
            dqs, dks, dbs, dvs, dgs = [], [], [], [], []
            dgn = jnp.zeros((1, HD), F32)
            for hh in range(HEADS):
                sl = slice(hh * HD, (hh + 1) * HD)
                s_prev, ds_new = st_ref[c, hh], dst[hh]
                qe, qh, kh, kd = z["qe"][:, sl], z["qh"][:, sl], z["kh"][:, sl], z["kd"][:, sl]
                ebl = z["ebl"][:, sl]
                gh, dyh = gg[:, sl], dy[:, sl]
                xh, rr = _rms(ov[:, sl])
                sg = _sig(gh)
                dyn = dyh * (gh * sg)
                dgs.append(dyh * (xh * gnv) * (sg * (1.0 + gh * (1.0 - sg))))
                dgn = dgn + jnp.sum(dyn * xh, axis=0, keepdims=True)
                dxh = dyn * gnv
                do = (rr * (dxh - xh * jnp.mean(dxh * xh, axis=-1, keepdims=True))).astype(BF16)
                vb, dsb = v[:, sl].astype(BF16), ds_new.astype(BF16)
                qeb, qhb, khb, kdb = (a.astype(BF16) for a in (qe, qh, kh, kd))
                att = jnp.where(z["causal"], _dot(qhb, khb, NT), 0.0).astype(BF16)
                datt = jnp.where(z["causal"], _dot(do, vb, NT), 0.0).astype(BF16)
                dvs.append(_dot(att, do, TN_) + _dot(kdb, dsb, NT))
                dqe = _dot(do, s_prev.astype(BF16), NN)
                dqh = _dot(datt, khb, NN)
                dkh = _dot(datt, qhb, TN_)
                dkd = _dot(vb, dsb, NN)
                qe, qh, kh, kd = (a.astype(F32) for a in (qeb, qhb, khb, kdb))
                dbl = (jnp.sum(dkd * kd, axis=0, keepdims=True)
                       + jnp.sum(ds_new * s_prev, axis=0, keepdims=True) * ebl)
                dqs.append(dqe * z["eb"][:, sl] + dqh * z["ebm"][:, sl])
                dks.append(dkh * z["ekm"][:, sl] + dkd * z["ekl"][:, sl])
                dbs.append(dqe * qe + dqh * qh - dkh * kh - dkd * kd + jnp.where(last, dbl, 0.0))
                dst[hh] = _dot(do, qeb, TN_) + ds_new * ebl
            dgn_ref[...] += dgn
            dq, dk, db = (jnp.concatenate(x, axis=1) for x in (dqs, dks, dbs))
            dlf = _tri_dot(z["upper"], db)
            sf, fg, sq = z["sf"], z["fg"], z["sq"]
            dmix = dlf / fg - dk
            dsf = dmix * (1.0 - lb)
            dlb[...] += jnp.sum(dmix * (1.0 - sf), axis=0, keepdims=True)
            dp_ref[rows, 0:HGW] = (dq * (sq * (1.0 + qr * (1.0 - sq)))).astype(BF16)
            dp_ref[rows, HGW:2 * HGW] = (dsf * sf * (1.0 - sf)).astype(BF16)
            dp_ref[rows, 2 * HGW:3 * HGW] = jnp.concatenate(dvs, axis=1).astype(BF16)
            dp_ref[rows, 3 * HGW:4 * HGW] = jnp.concatenate(dgs, axis=1).astype(BF16)
            return carry

        lax.fori_loop(0, NCH, chunk, 0, unroll=True)
        dl0 = dlb[...] * lb * (1.0 - lb)
        dl_ref[0:1, :] = dl0
        dl_ref[1:2, :] = -dl0

    def col(j):
        return pl.BlockSpec((TM, HGW), lambda i, j=j: (nt - 1 - i, j))

    return pl.pallas_call(
        body, name="hgrn_bwd", grid=(nt,),
        in_specs=[col(0), col(0), col(1), col(2), col(3), col(0),
                  pl.BlockSpec((NCH, HEADS, HD, HD), lambda i: (nt - 1 - i, 0, 0, 0)),
                  pl.BlockSpec((2, HGW), lambda i: (0, 0)), pl.BlockSpec((1, HD), lambda i: (0, 0)), ANY],
        out_specs=[pl.BlockSpec((TM, 4 * HGW), lambda i: (nt - 1 - i, 0)),
                   pl.BlockSpec((1, HD), lambda i: (0, 0)), pl.BlockSpec((2, HGW), lambda i: (0, 0))],
        out_shape=[jax.ShapeDtypeStruct((t, D_INP), BF16), jax.ShapeDtypeStruct((1, HD), F32),
                   jax.ShapeDtypeStruct((2, HGW), F32)],
        scratch_shapes=[pltpu.VMEM((HEADS, HD, HD), F32), pltpu.VMEM((1, HGW), F32)],
        input_output_aliases={9: 0},
        compiler_params=_cparams(("arbitrary",)),
    )(dyb, proj, proj, proj, proj, o, states, lbl, gn, dproj)


def _rotate_w_in(w):
    return jnp.concatenate([w[:, ROT:], jnp.zeros((w.shape[0], COL_PAD), w.dtype), w[:, :ROT]], axis=1)


def _unrotate_g_in(g):
    return jnp.concatenate([g[:, D_INP - ROT:], g[:, :D_IN - ROT]], axis=1)


SB = RGW // 2


def _bd_gates(t):
    tm = _pick(t, (768, 256))

    def col(j):
        return 2 * (j % 2) + j // 2

    return ((t // tm, 4, 1), (tm, SB, SB),
            (lambda i, j, k: (i, j // 2), lambda i, j, k: (j // 2, col(j)), lambda i, j, k: (i, col(j))))


def _bd_dw(t):
    tk = _pick(t, (1408, 768))
    return (1, 4, t // tk), (SB, SB, tk), (lambda i, j, k: (k, j % 2), lambda i, j, k: (k, j), lambda i, j, k: (j % 2, j))


def _block_diag(w):
    eye = jnp.eye(RG_BLOCKS, dtype=w.dtype)
    return (w[:, :, None, :] * eye[:, None, :, None]).reshape(RGW, RGW)


def _diag_blocks(wd):
    w4 = wd.reshape(RG_BLOCKS, RG_BLOCK, RG_BLOCKS, RG_BLOCK)
    return jnp.stack([w4[n, :, n, :] for n in range(RG_BLOCKS)])


def _local_step(h0, target, wts, small, hosts=None, finalize=None):
    hosts = hosts or {}
    carried = {}
    g = {}
    t = h0.shape[0]

    def mm(a, b, mode, out_dtype, name, **kw):
        if name not in hosts:
            return _mm(a, b, mode, out_dtype, name, **kw)
        host = hosts[name](g)
        out, res = _mm(a, b, mode, out_dtype, name, host=host, **kw)
        carried[name] = (host.ins, res)
        return out

    w_in = wts["w_in"]
    wax_d = jnp.concatenate([_block_diag(small["rg_wa"]), _block_diag(small["rg_wx"])], axis=1).astype(BF16)
    ba, bx, lam = small["rg_ba"], small["rg_bx"], small["rg_lambda"]
    lbl, gn = small["hg_lb_logits"], small["hg_norm_g"]
    conv_w, conv_b = small["conv_w"], small["conv_b"]

    tmm = _pick(t, (768, 256))
    proj, hn1 = mm(h0, w_in, "nn", BF16, "mm_proj", pro=_norm_pro(small["norm1_g"]))
    if finalize is not None:
        wts = finalize(carried["mm_proj"][1])
    w_pa, w_pb, w_out, w_fi, w_fd = (wts[k] for k in ("w_proj_a", "w_proj_b", "w_out", "w_ffn_in", "w_ffn_down"))
    xc, xcb = _conv_fwd(proj, conv_w, conv_b)
    gri = mm(xcb, wax_d, "nn", F32, "mm_rg_gates", sparse=_bd_gates(t))
    hrg, ya = _rg_fwd(gri, xc, proj, ba, bx, lam)
    yb, o, states = _hgrn_fwd(proj, lbl, gn)
    pa = mm(ya, w_pa, "nn", BF16, "mm_pa")
    pb = mm(yb, w_pb, "nn", BF16, "mm_pb")
    h1, merged = mm(pa, w_out, "nn", F32, "mm_out", resid=h0, pro=_merge_pro(pb, proj, tmm), tiles=(tmm, D))
    (gu, act), hn2 = mm(h1, w_fi, "nn", BF16, "mm_ffn_in", pro=_norm_pro(small["norm2_g"]), epi=_swiglu_epi(t),
                        tiles=(FM, 2 * FH))
    dh2, dh2b, loss, g["norm_f_g"] = mm(act, w_fd, "nn", F32, "mm_ffn_down", resid=h1,
                                        epi=_final_loss_epi(small["norm_f_g"], target, t), tiles=(TM, D))

    g["w_ffn_down"] = mm(act, dh2b, "tn", BF16, "mm_d_wfd")
    (dgu,) = mm(dh2b, w_fd, "nt", BF16, "mm_d_act", epi=_swiglu_bwd_epi(gu), tiles=(FM, FH))
    g["w_ffn_in"] = mm(hn2, dgu, "tn", BF16, "mm_d_wfi")
    dh1, dh1b, g["norm2_g"] = mm(dgu, w_fi, "nt", BF16, "mm_d_hn2", epi=_rms_bwd_epi(h1, small["norm2_g"], dh2, tmm),
                                 tiles=(tmm, D))
    g["w_out"] = mm(merged, dh1b, "tn", BF16, "mm_d_wout")
    dpa, dpb, dproj = mm(dh1b, w_out, "nt", BF16, "mm_d_merged", epi=_merge_bwd_epi(pa, pb, proj, TM), tiles=(TM, D))
    dproj = _zero_pad_cols(dproj)
    g["w_proj_a"] = mm(ya, dpa, "tn", BF16, "mm_d_wpa")
    g["w_proj_b"] = mm(yb, dpb, "tn", BF16, "mm_d_wpb")
    dya = mm(dpa, w_pa, "nt", BF16, "mm_d_ya")
    dyb = mm(dpb, w_pb, "nt", BF16, "mm_d_yb")
    dproj, g["hg_norm_g"], g["hg_lb_logits"] = _hgrn_bwd(dyb, proj, o, states, lbl, gn, dproj)
    dgri, dxc, dproj, g["rg_ba"], g["rg_bx"], g["rg_lambda"] = _rg_bwd(dya, hrg, gri, xc, proj, ba, bx, lam, dproj)
    dwax = mm(xcb, dgri, "tn", F32, "mm_d_wax", sparse=_bd_dw(t))
    g["rg_wa"], g["rg_wx"] = _diag_blocks(dwax[:, :RGW]), _diag_blocks(dwax[:, RGW:])
    dxc = mm(dgri, wax_d, "nt", F32, "mm_d_xc", resid=dxc)
    dproj, g["conv_w"], g["conv_b"] = _conv_bwd(dxc, proj, conv_w, dproj)
    g["w_in"] = mm(hn1, dproj, "tn", BF16, "mm_d_win")
    dhn1 = mm(dproj, w_in, "nt", BF16, "mm_d_hn1")
    grad_x, g["meta_tokens"], g["norm1_g"] = _rmsnorm1_bwd(dhn1, h0, small["norm1_g"], dh1)
    return loss, grad_x, g, carried


def _adamw_math(w, g, m, v):
    m = B1 * m + (1.0 - B1) * g
    v = B2 * v + (1.0 - B2) * (g * g)
    m_hat = m / (1.0 - B1 ** STEP)
    v_hat = v / (1.0 - B2 ** STEP)
    return -LR * (m_hat / (jnp.sqrt(v_hat) + ADAM_EPS) + WD * w), m, v


def _adamw(w, g, m, v, name):
    r, c = w.shape
    tr = _pick(r, (256, 352, 320, 128, 64, 32, 16, 8))

    def body(w_ref, g_ref, m_ref, v_ref, d_ref, mo_ref, vo_ref):
        d_ref[...], mo_ref[...], vo_ref[...] = _adamw_math(w_ref[...], g_ref[...], m_ref[...], v_ref[...])

    blk = pl.BlockSpec((tr, c), lambda i: (i, 0))
    return pl.pallas_call(
        body, name=name, grid=(r // tr,), in_specs=[blk] * 4, out_specs=[blk] * 3,
        out_shape=[jax.ShapeDtypeStruct((r, c), F32)] * 3,
        compiler_params=_cparams(("parallel",)),
    )(w, g, m, v)


def _adamw_rows(recv, params, extras, name):
    _, _, width = recv.shape
    n_par = len(params)

    def body(*refs):
        r_ref = refs[0]
        ins = refs[1:1 + 3 * n_par]
        outs = refs[1 + 3 * n_par:]
        g_all = r_ref[0]
        for p in range(1, 8):
            g_all = g_all + r_ref[p]
        for q, (row0, w, _, _) in enumerate(params):
            g = g_all[row0:row0 + w.shape[0], :]
            w_ref, m_ref, v_ref = ins[3 * q:3 * q + 3]
            g_ref, d_ref, mo_ref, vo_ref = outs[4 * q:4 * q + 4]
            g_ref[...] = g
            d_ref[...], mo_ref[...], vo_ref[...] = _adamw_math(w_ref[...], g, m_ref[...], v_ref[...])
        for q, (row0, n) in enumerate(extras):
            outs[4 * n_par + q][...] = g_all[row0:row0 + n, :]

    shapes = [jax.ShapeDtypeStruct(w.shape, F32) for _, w, _, _ in params for _ in range(4)]
    shapes += [jax.ShapeDtypeStruct((n, width), F32) for _, n in extras]
    vm = pl.BlockSpec(memory_space=pltpu.VMEM)
    outs = pl.pallas_call(
        body, name=name, in_specs=[vm] * (1 + 3 * n_par), out_specs=[vm] * len(shapes), out_shape=shapes,
        compiler_params=pltpu.CompilerParams(vmem_limit_bytes=VMEM_LIMIT),
    )(recv, *[a for _, w, m, v in params for a in (w, m, v)])
    return [tuple(outs[4 * q:4 * q + 4]) for q in range(n_par)], list(outs[4 * n_par:])


def _col_block(weight, px, py):
    return 2 * py + px if weight == "w_ffn_in" else 2 * px + py


def _sum_place(recv, gfull, weight, xyc, name):
    _, r, c = recv.shape
    tr = _pick(r, (128, 176, 160, 64, 32, 16))
    nb = r // tr

    def body(_, own_ref, r_ref, o_ref):
        g = own_ref[...].astype(F32)
        for p in range(7):
            g = g + r_ref[p].astype(F32)
        o_ref[...] = g

    if weight in COL_SHARDED:
        own = pl.BlockSpec((tr, c), lambda i, s: (s[2] * nb + i, _col_block(weight, s[0], s[1])))
    else:
        own = pl.BlockSpec((tr, c), lambda i, s: ((4 * s[0] + 2 * s[1] + s[2]) * nb + i, 0))
    return pl.pallas_call(
        body, name=name,
        grid_spec=pltpu.PrefetchScalarGridSpec(
            num_scalar_prefetch=1, grid=(nb,),
            in_specs=[own, pl.BlockSpec((7, tr, c), lambda i, s: (0, i, 0))],
            out_specs=pl.BlockSpec((tr, c), lambda i, s: (s[2] * nb + i, 0))),
        out_shape=jax.ShapeDtypeStruct((2 * r, c), F32),
        compiler_params=_cparams(("arbitrary",)),
    )(xyc, gfull, recv)


def _place_own(shard, slot, dtype, name):
    r, c = shard.shape
    tr = _pick(r, (256, 352, 320, 128, 32))

    def body(_, x_ref, o_ref):
        o_ref[...] = x_ref[...].astype(dtype)

    return pl.pallas_call(
        body, name=name,
        grid_spec=pltpu.PrefetchScalarGridSpec(
            num_scalar_prefetch=1, grid=(r // tr,),
            in_specs=[pl.BlockSpec((tr, c), lambda i, s: (i, 0))],
            out_specs=pl.BlockSpec((None, tr, c), lambda i, s: (s[0], i, 0))),
        out_shape=jax.ShapeDtypeStruct((4, r, c), dtype),
        compiler_params=_cparams(("arbitrary",)),
    )(slot, shard)


def _place_cols(shard, slot, name):
    r, c = shard.shape
    tr = _pick(r, (256, 128))

    def body(_, x_ref, o_ref):
        o_ref[...] = x_ref[...].astype(BF16)

    return pl.pallas_call(
        body, name=name,
        grid_spec=pltpu.PrefetchScalarGridSpec(
            num_scalar_prefetch=1, grid=(r // tr,),
            in_specs=[pl.BlockSpec((tr, c), lambda i, s: (i, 0))],
            out_specs=pl.BlockSpec((tr, c), lambda i, s: (i, 2 * (s[0] % 2) + s[0] // 2))),
        out_shape=jax.ShapeDtypeStruct((r, 4 * c), BF16),
        compiler_params=_cparams(("arbitrary",)),
    )(slot, shard)


def _place_w_in(shard, slot):
    r, c = shard.shape
    nb, npad = c // 128, COL_PAD // 128
    rot_blocks, in_blocks = ROT // 128, D_IN // 128

    def out_block(j, s):
        nat = s[0] * nb + jnp.minimum(j, nb - 1)
        rot = jnp.where(nat >= rot_blocks, nat - rot_blocks, nat + (D_INP - ROT) // 128)
        return jnp.where(j < nb, rot, (in_blocks - rot_blocks) + (j - nb))

    def body(_, x_ref, o_ref):
        o_ref[...] = jnp.where(pl.program_id(0) < nb, x_ref[...], 0.0).astype(BF16)

    return pl.pallas_call(
        body, name="place_w_in",
        grid_spec=pltpu.PrefetchScalarGridSpec(
            num_scalar_prefetch=1, grid=(nb + npad,),
            in_specs=[pl.BlockSpec((r, 128), lambda j, s: (0, jnp.minimum(j, nb - 1)))],
            out_specs=pl.BlockSpec((r, 128), lambda j, s: (0, out_block(j, s)))),
        out_shape=jax.ShapeDtypeStruct((r, D_INP), BF16),
        compiler_params=_cparams(("arbitrary",)),
    )(slot, shard)


BIG = ("w_in", "w_proj_a", "w_proj_b", "w_out", "w_ffn_in", "w_ffn_down")
COL_SHARDED = ("w_in", "w_ffn_in")


def _coords():
    return lax.axis_index("x"), lax.axis_index("y"), lax.axis_index("c")


W_IN_HEAD = 384


def _shard_views(b, layout, chip, half_of):
    if layout == "slots":
        half = b.shape[1] // 2
        return [b.at[chip, pl.ds(pl.multiple_of(half_of * half, 8), half)]]
    half = b.shape[0] // 2
    rows = pl.ds(pl.multiple_of(half_of * half, 8), half)
    if layout == "cols":
        c = b.shape[1] // 4
        return [b.at[rows, pl.ds(pl.multiple_of((2 * (chip % 2) + chip // 2) * c, 128), c)]]
    c = D_IN // 4
    views = []
    for first, width in ((0, W_IN_HEAD), (W_IN_HEAD, c - W_IN_HEAD)):
        nat = chip * c + first
        rot = jnp.where(nat >= ROT, nat - ROT, nat + D_INP - ROT)
        views.append(b.at[rows, pl.ds(pl.multiple_of(rot, 128), width)])
    return views


def _gather_host(bufs, layouts):
    n = len(bufs)

    def place():
        x, y, c = _coords()
        return x, y, c, [(1 - x, y), (x, 1 - y), (1 - x, 1 - y)]

    def copies(b, sems, w, k, chip, half_of, to):
        return [pltpu.make_async_remote_copy(src_ref=v, dst_ref=v, send_sem=sems[0].at[w, k, p], recv_sem=sems[1].at[w, k, p],
                                             device_id=to, device_id_type=MESH)
                for p, v in enumerate(_shard_views(b, layouts[w], chip, half_of))]

    def start(_, outs, sems):
        x, y, c, chips = place()
        for w, b in enumerate(outs):
            for k, (cx, cy) in enumerate(chips):
                for cp in copies(b, sems, w, k, 2 * x + y, c, (cx, cy, c)):
                    cp.start()

    def mid(_, outs, sems):
        x, y, c, chips = place()
        for w, b in enumerate(outs):
            for k, (cx, cy) in enumerate(chips):
                for cp in copies(b, sems, w, k, 2 * cx + cy, c, (cx, cy, c)):
                    cp.wait_recv()
                for cp in copies(b, sems, w, 3 + k, 2 * cx + cy, c, (x, y, 1 - c)):
                    cp.start()

    def finish(_, outs, sems):
        x, y, c, chips = place()
        for w, b in enumerate(outs):
            for k, (cx, cy) in enumerate(chips):
                for cp in copies(b, sems, w, 3 + k, 2 * cx + cy, 1 - c, (x, y, 1 - c)):
                    cp.wait_recv()
        for w, b in enumerate(outs):
            for k, (cx, cy) in enumerate(chips):
                for cp in copies(b, sems, w, k, 2 * x + y, c, (cx, cy, c)) + copies(b, sems, w, 3 + k, 2 * cx + cy, c, (x, y, 1 - c)):
                    cp.wait_send()

    return _Host(ins=list(bufs), out_shapes=[jax.ShapeDtypeStruct(b.shape, b.dtype) for b in bufs],
                 aliases={i: i for i in range(n)},
                 sems=[pltpu.SemaphoreType.DMA((n, 6, 2)), pltpu.SemaphoreType.DMA((n, 6, 2))],
                 start=start, mid=mid, finish=finish)


def _run_host(host, name):
    n_in, n_out = len(host.ins), len(host.out_shapes)

    def body(*refs):
        ins, outs, sems = refs[:n_in], refs[n_in:n_in + n_out], refs[n_in + n_out:]
        host.start(ins, outs, sems)
        if host.mid is not None:
            host.mid(ins, outs, sems)
        host.finish(ins, outs, sems)

    return pl.pallas_call(
        body, name=name, in_specs=[ANY] * n_in, out_specs=[ANY] * n_out, out_shape=list(host.out_shapes),
        scratch_shapes=list(host.sems), input_output_aliases=dict(host.aliases),
    )(*host.ins)


def _peer(x, y, c, k):
    fx, fy, fc = (k >> 2) & 1, (k >> 1) & 1, k & 1
    return (x ^ fx if fx else x, y ^ fy if fy else y, c ^ fc if fc else c)


def _sub_shape(name, full_shape):
    r, c = full_shape
    return (r // 2, c // 4) if name in COL_SHARDED else (r // 8, c)


def _exchange_host(names, grads):
    n = len(names)
    shapes = [_sub_shape(k, g.shape) for k, g in zip(names, grads)]

    def copy(ins, outs, sems, w, k):
        x, y, c = _coords()
        px, py, pc = _peer(x, y, c, k)
        sr, sc = shapes[w]
        if names[w] in COL_SHARDED:
            col = _col_block(names[w], px, py) * sc
            src = ins[w].at[pl.ds(pl.multiple_of(pc * sr, 16), sr), pl.ds(pl.multiple_of(col, 128), sc)]
        else:
            src = ins[w].at[pl.ds(pl.multiple_of((4 * px + 2 * py + pc) * sr, 16), sr)]
        return pltpu.make_async_remote_copy(
            src_ref=src, dst_ref=outs[w].at[k - 1], send_sem=sems[0].at[w, k - 1], recv_sem=sems[1].at[w, k - 1],
            device_id=(px, py, pc), device_id_type=MESH)

    def start(ins, outs, sems):
        for w in range(n):
            for k in range(1, 8):
                copy(ins, outs, sems, w, k).start()

    def finish(ins, outs, sems):
        for w in range(n):
            for k in range(1, 8):
                copy(ins, outs, sems, w, k).wait_recv()
        for w in range(n):
            for k in range(1, 8):
                copy(ins, outs, sems, w, k).wait_send()

    return _Host(ins=list(grads), out_shapes=[jax.ShapeDtypeStruct((7,) + s, g.dtype) for s, g in zip(shapes, grads)],
                 aliases={}, sems=[pltpu.SemaphoreType.DMA((n, 7)), pltpu.SemaphoreType.DMA((n, 7))],
                 start=start, mid=None, finish=finish)


def _sibling_host(bufs):
    n = len(bufs)

    def copy(outs, sems, w, half_of):
        x, y, c = _coords()
        half = outs[w].shape[0] // 2
        rows = outs[w].at[pl.ds(pl.multiple_of(half_of * half, 8), half)]
        return pltpu.make_async_remote_copy(src_ref=rows, dst_ref=rows, send_sem=sems[0].at[w], recv_sem=sems[1].at[w],
                                            device_id=(x, y, 1 - c), device_id_type=MESH)

    def start(_, outs, sems):
        c = lax.axis_index("c")
        for w in range(n):
            copy(outs, sems, w, c).start()

    def finish(_, outs, sems):
        c = lax.axis_index("c")
        for w in range(n):
            copy(outs, sems, w, 1 - c).wait_recv()
        for w in range(n):
            copy(outs, sems, w, c).wait_send()

    return _Host(ins=list(bufs), out_shapes=[jax.ShapeDtypeStruct(b.shape, b.dtype) for b in bufs],
                 aliases={i: i for i in range(n)},
                 sems=[pltpu.SemaphoreType.DMA((n,)), pltpu.SemaphoreType.DMA((n,))], start=start, mid=None, finish=finish)


def _pack_host(pack):
    def me_of():
        x, y, c = _coords()
        return x, y, c, 4 * x + 2 * y + c

    def copy(ins, outs, sems, k, slot):
        x, y, c, _ = me_of()
        return pltpu.make_async_remote_copy(src_ref=ins[0], dst_ref=outs[0].at[slot], send_sem=sems[0].at[k - 1],
                                            recv_sem=sems[1].at[k - 1], device_id=_peer(x, y, c, k), device_id_type=MESH)

    def start(ins, outs, sems):
        me = me_of()[3]
        pltpu.make_async_copy(ins[0], outs[0].at[me], sems[2]).start()
        for k in range(1, 8):
            copy(ins, outs, sems, k, me).start()

    def finish(ins, outs, sems):
        x, y, c, me = me_of()
        for k in range(1, 8):
            px, py, pc = _peer(x, y, c, k)
            copy(ins, outs, sems, k, 4 * px + 2 * py + pc).wait_recv()
        for k in range(1, 8):
            copy(ins, outs, sems, k, me).wait_send()
        pltpu.make_async_copy(ins[0], outs[0].at[me], sems[2]).wait()

    return _Host(ins=[pack], out_shapes=[jax.ShapeDtypeStruct((8,) + pack.shape, F32)], aliases={},
                 sems=[pltpu.SemaphoreType.DMA((7,)), pltpu.SemaphoreType.DMA((7,)), pltpu.SemaphoreType.DMA],
                 start=start, mid=None, finish=finish)


def _join_hosts(hosts):
    ins, outs, sems, aliases, spans = [], [], [], {}, []
    for h in hosts:
        spans.append((len(ins), len(h.ins), len(outs), len(h.out_shapes), len(sems), len(h.sems)))
        for i_in, i_out in h.aliases.items():
            aliases[len(ins) + i_in] = len(outs) + i_out
        ins, outs, sems = ins + list(h.ins), outs + list(h.out_shapes), sems + list(h.sems)

    def phase(which):
        fns = [getattr(h, which) for h in hosts]
        if all(f is None for f in fns):
            return None

        def run(i_refs, o_refs, s_refs):
            for f, (i0, n_i, o0, n_o, s0, n_s) in zip(fns, spans):
                if f is not None:
                    f(i_refs[i0:i0 + n_i], o_refs[o0:o0 + n_o], s_refs[s0:s0 + n_s])

        return run

    return _Host(ins, outs, aliases, sems, phase("start"), phase("mid"), phase("finish"))


def _allgather_pack(pack):
    rows = pack.shape[0]

    def body(p_ref, o_ref, send, recv, local):
        x, y, c = _coords()
        me = 4 * x + 2 * y + c
        mine = pltpu.make_async_copy(p_ref, o_ref.at[me], local)
        mine.start()

        def copy(k, slot):
            return pltpu.make_async_remote_copy(src_ref=p_ref, dst_ref=o_ref.at[slot], send_sem=send.at[k - 1],
                                                recv_sem=recv.at[k - 1], device_id=_peer(x, y, c, k), device_id_type=MESH)

        for k in range(1, 8):
            copy(k, me).start()
        for k in range(1, 8):
            px, py, pc = _peer(x, y, c, k)
            copy(k, 4 * px + 2 * py + pc).wait_recv()
        for k in range(1, 8):
            copy(k, me).wait_send()
        mine.wait()

    return pl.pallas_call(
        body, name="allgather_pack", in_specs=[pl.BlockSpec(memory_space=pltpu.VMEM)], out_specs=ANY,
        out_shape=jax.ShapeDtypeStruct((8,) + pack.shape, F32),
        scratch_shapes=[pltpu.SemaphoreType.DMA((7,)), pltpu.SemaphoreType.DMA((7,)), pltpu.SemaphoreType.DMA],
        compiler_params=pltpu.CompilerParams(vmem_limit_bytes=VMEM_LIMIT),
    )(pack)


SHARDED_SMALL = ("meta_tokens", "conv_w")

ROWS = {"conv_b": (1, RGW), "rg_ba": (1, RGW), "rg_bx": (1, RGW), "rg_lambda": (1, RGW), "conv_w": (4, RGW),
        "norm2_g": (1, D), "norm_f_g": (1, D), "hg_lb_logits": (2, D),
        "rg_wa": (RGW * RG_BLOCK // 128, 128), "rg_wx": (RGW * RG_BLOCK // 128, 128), "hg_norm_g": (1, 128),
        "norm1_g": (1, D), "meta_tokens": (N_META, D), "loss": (1, D)}
PACKS = {"early_wide": ("conv_b", "rg_ba", "rg_bx", "rg_lambda", "conv_w"),
         "early_model": ("norm2_g", "norm_f_g", "hg_lb_logits"),
         "early_lane": ("rg_wa", "rg_wx", "hg_norm_g"),
         "late": ("norm1_g", "meta_tokens", "loss")}
EARLY = ("early_wide", "early_model", "early_lane")
NO_UPDATE = ("conv_w", "meta_tokens", "loss")


def _pack_rows(vals, names):
    parts, first, row = [], {}, 0
    for k in names:
        first[k] = row
        parts.append(vals[k].reshape(ROWS[k]).astype(F32))
        row += ROWS[k][0]
    if row % 8:
        parts.append(jnp.zeros((-row % 8, ROWS[names[0]][1]), F32))
    return jnp.concatenate(parts, axis=0), first


ORDER = ("meta_tokens", "norm1_g", "w_in", "conv_w", "conv_b", "rg_wa", "rg_ba", "rg_wx", "rg_bx", "rg_lambda",
         "hg_lb_logits", "hg_norm_g", "w_proj_a", "w_proj_b", "w_out", "norm2_g", "w_ffn_in", "w_ffn_down", "norm_f_g")


def kernel(x, meta_tokens, norm1_g, w_in, conv_w, conv_b, rg_wa, rg_ba, rg_wx, rg_bx, rg_lambda, hg_lb_logits, hg_norm_g, w_proj_a, w_proj_b, w_out, norm2_g, w_ffn_in, w_ffn_down, norm_f_g, loss_target, m_meta_tokens, m_norm1_g, m_w_in, m_conv_w, m_conv_b, m_rg_wa, m_rg_ba, m_rg_wx, m_rg_bx, m_rg_lambda, m_hg_lb_logits, m_hg_norm_g, m_w_proj_a, m_w_proj_b, m_w_out, m_norm2_g, m_w_ffn_in, m_w_ffn_down, m_norm_f_g, v_meta_tokens, v_norm1_g, v_w_in, v_conv_w, v_conv_b, v_rg_wa, v_rg_ba, v_rg_wx, v_rg_bx, v_rg_lambda, v_hg_lb_logits, v_hg_norm_g, v_w_proj_a, v_w_proj_b, v_w_out, v_norm2_g, v_w_ffn_in, v_w_ffn_down, v_norm_f_g):
    args = dict(locals())
    w = {k: args[k] for k in ORDER}
    m = {k: args["m_" + k] for k in ORDER}
    v = {k: args["v_" + k] for k in ORDER}
    xi, yi, ci = _coords()
    chip = 2 * xi + yi
    slot = jnp.reshape(chip, (1,)).astype(jnp.int32)
    xyc = jnp.stack([xi, yi, ci]).astype(jnp.int32)

    tiny = jnp.zeros((32, 384), F32)
    tiny = tiny.at[0:N_META, 0:256].set(meta_tokens).at[N_META:N_META + 4, 0:320].set(conv_w[0])
    later = [k for k in BIG if k != "w_in"]
    layouts = {k: "cols" if k in COL_SHARDED else "slots" for k in later}
    slots = {k: _place_cols(w[k][0], slot, "place_" + k) if layouts[k] == "cols"
             else _place_own(w[k][0], slot, BF16, "place_" + k) for k in later}
    w_in_full, tiny_all = _run_host(
        _gather_host([_place_w_in(w["w_in"][0], slot), _place_own(tiny, slot, F32, "place_tiny")], ["w_in", "slots"]),
        "allgather_w_in")

    def full_matrix(k, gth):
        return gth if layouts[k] == "cols" else gth.reshape(-1, gth.shape[2])

    meta_full = jnp.transpose(tiny_all[:, 0:N_META, 0:256], (1, 0, 2)).reshape(N_META, D)
    conv_w_full = jnp.transpose(tiny_all[:, N_META:N_META + 4, 0:320], (1, 0, 2)).reshape(4, RGW)

    small = dict(norm1_g=norm1_g, conv_w=conv_w_full, conv_b=conv_b, rg_wa=rg_wa[0], rg_ba=rg_ba, rg_wx=rg_wx[0],
                 rg_bx=rg_bx, rg_lambda=rg_lambda, hg_lb_logits=hg_lb_logits, hg_norm_g=hg_norm_g,
                 norm2_g=norm2_g, norm_f_g=norm_f_g.reshape(1, D))

    hosts = {
        "mm_proj": lambda g: _gather_host([slots[k] for k in later], [layouts[k] for k in later]),
        "mm_d_act": lambda g: _exchange_host(["w_ffn_down"], [g["w_ffn_down"]]),
        "mm_d_hn2": lambda g: _exchange_host(["w_ffn_in"], [g["w_ffn_in"]]),
        "mm_d_win": lambda g: _join_hosts([
            _exchange_host(["w_out", "w_proj_a", "w_proj_b"], [g["w_out"], g["w_proj_a"], g["w_proj_b"]]),
            *[_pack_host(_pack_rows(g, PACKS[p])[0]) for p in EARLY]]),
        "mm_d_hn1": lambda g: _exchange_host(["w_in"], [_unrotate_g_in(g["w_in"])]),
    }
    h0 = jnp.concatenate([jnp.zeros((PAD, D), F32), meta_full, x[0]], axis=0)
    loss_blk, grad_x, g, carried = _local_step(
        h0, loss_target[0], {"w_in": w_in_full}, small, hosts=hosts,
        finalize=lambda gathered: {k: full_matrix(k, gth) for k, gth in zip(later, gathered)})
    g["loss"] = jnp.broadcast_to(loss_blk[0:1, 0:1], (1, D))

    halves = {}
    sources = {"mm_d_act": ["w_ffn_down"], "mm_d_hn2": ["w_ffn_in"], "mm_d_win": ["w_out", "w_proj_a", "w_proj_b"],
               "mm_d_hn1": ["w_in"]}
    for name, keys in sources.items():
        partials, received = carried[name]
        for k, part, rec in zip(keys, partials, received):
            halves[k] = _sum_place(rec, part, k, xyc, "sum_" + k)
    g_big = dict(zip(BIG, _run_host(_sibling_host([halves[k] for k in BIG]), "sibling_swap")))
    late, _ = _pack_rows(g, PACKS["late"])
    recv_packs = dict(zip(EARLY, carried["mm_d_win"][1][3:]), late=_allgather_pack(late))

    grad, delta, new_m, new_v, summed = {}, {}, {}, {}, {}
    for pack, names in PACKS.items():
        row, params, extras = 0, [], []
        for k in names:
            if k in NO_UPDATE:
                extras.append((row, ROWS[k][0]))
            else:
                params.append((row,) + tuple(d[k].reshape(ROWS[k]) for d in (w, m, v)))
            row += ROWS[k][0]
        updated, rows_only = _adamw_rows(recv_packs[pack], params, extras, "adamw_" + pack)
        for k, res in zip([k for k in names if k not in NO_UPDATE], updated):
            grad[k], delta[k], new_m[k], new_v[k] = (a.reshape(w[k].shape) for a in res)
        summed.update(zip([k for k in names if k in NO_UPDATE], rows_only))
    loss = summed["loss"][0, 0]
    g_shard = {"meta_tokens": lax.dynamic_slice(summed["meta_tokens"], (0, chip * 256), (N_META, 256)),
               "conv_w": lax.dynamic_slice(summed["conv_w"], (0, chip * 320), (4, 320))}
    for k in BIG + SHARDED_SMALL:
        gk = g_big[k] if k in BIG else g_shard[k]
        wk, mk, vk = (d[k].reshape(gk.shape) for d in (w, m, v))
        dk, mk, vk = _adamw(wk, gk, mk, vk, "adamw_" + k)
        grad[k], delta[k], new_m[k], new_v[k] = (a.reshape(w[k].shape) for a in (gk, dk, mk, vk))

    return (loss, grad_x[None], *[grad[k] for k in ORDER], *[delta[k] for k in ORDER],
            *[new_m[k] for k in ORDER], *[new_v[k] for k in ORDER])
```

```python
import functools

import jax
import jax.numpy as jnp
from jax import lax
from jax.experimental import pallas as pl
from jax.experimental.pallas import tpu as pltpu

F32, BF16 = jnp.float32, jnp.bfloat16
D = 1024
N_META = 16
RGW = 1280
RG_BLOCKS, RG_BLOCK = 16, 80
RG_C = 8.0
HEADS, HD = 8, 128
HGW = HEADS * HD
DFF = 2816
D_IN = 2 * RGW + 6 * D
ROT = 2 * RGW
COL_PAD = 256
D_INP = D_IN + COL_PAD
EPS = 1e-6
CH = 64
TM = 256
ROW0 = TM
PAD = ROW0 - N_META
CT = RGW
RS = 16
NCT = RGW // CT
EXP_CLAMP = 80.0
VMEM_LIMIT = 56 * 1024 * 1024

LR, B1, B2, ADAM_EPS, WD, STEP = 0.001, 0.9, 0.999, 1e-08, 0.01, 10
MESH = pl.DeviceIdType.MESH
ANY = pl.BlockSpec(memory_space=pl.ANY)


def _cparams(sem):
    return pltpu.CompilerParams(dimension_semantics=sem, vmem_limit_bytes=VMEM_LIMIT)


def _pick(n, prefs):
    for p in prefs:
        if n % p == 0:
            return p
    return n


def _sig(x):
    return 0.5 * jnp.tanh(0.5 * x) + 0.5


def _dot(a, b, dims):
    return lax.dot_general(a, b, (dims, ((), ())), preferred_element_type=F32)


NN, NT, TN_ = ((1,), (0,)), ((1,), (1,)), ((0,), (0,))


class _Host:
    def __init__(self, ins, out_shapes, aliases, sems, start, mid, finish):
        self.ins, self.out_shapes, self.aliases, self.sems = ins, out_shapes, aliases, sems
        self.start, self.mid, self.finish = start, mid, finish


class _Epi:
    def __init__(self, ins, in_specs, out_shapes, out_specs, fn, sequential=False, split=1):
        self.ins, self.in_specs, self.out_shapes, self.out_specs, self.fn = ins, in_specs, out_shapes, out_specs, fn
        self.sequential = sequential
        self.split = split


class _Pro:
    def __init__(self, ins, in_specs, fn):
        self.ins, self.in_specs, self.fn = ins, in_specs, fn


def _norm_pro(gain):
    def fn(a_ref, ins):
        xh, _ = _rms(a_ref[...])
        return (xh * ins[0][...]).astype(BF16)

    return _Pro([gain], [pl.BlockSpec((1, gain.shape[1]), lambda i, j, k: (0, 0))], fn)


def _mm(a, b, mode, out_dtype, name, resid=None, host=None, epi=None, tiles=None, sparse=None, pro=None):
    if mode == "nn":
        (m, kd), n = a.shape, b.shape[1]
    elif mode == "nt":
        (m, kd), n = a.shape, b.shape[0]
    else:
        (kd, m), n = a.shape, b.shape[1]
    if mode == "tn":
        tm = _pick(m, (1024, 1280, 1408, 640, 512, 256, 128))
        tk = _pick(kd, (1408, 768, 512, 256, 128))
    else:
        tm = _pick(m, (768, 512, 640, 256, 128))
        tk = kd if kd <= 2816 else _pick(kd, (1792, 1408, 1024, 512, 256, 128))
    tn = _pick(n, (1792, 1408, 1280, 1024, 512, 256, 128))
    if tiles is not None:
        tm, tn = tiles
    a_map = (lambda i, j, k: (k, i)) if mode == "tn" else (lambda i, j, k: (i, k))
    b_map = (lambda i, j, k: (j, k)) if mode == "nt" else (lambda i, j, k: (k, j))
    o_map = lambda i, j, k: (i, j)
    if sparse is not None:
        (ni, nj, nk), (tm, tn, tk), (a_map, b_map, o_map) = sparse
    else:
        assert m % tm == 0 and n % tn == 0 and kd % tk == 0, (name, m, n, kd, tm, tn, tk)
        ni, nj, nk = m // tm, n // tn, kd // tk
    dims = {"nn": NN, "nt": NT, "tn": TN_}[mode]

    n_hin = len(host.ins) if host else 0
    n_hout = len(host.out_shapes) if host else 0
    n_res = 0 if resid is None else 1
    n_pro = 0 if pro is None else 1
    n_pin = len(pro.ins) if pro else 0
    n_ein = len(epi.ins) if epi else 0
    n_out = len(epi.out_shapes) if epi else 1
    assert not n_pro or (mode == "nn" and nk == 1)

    def finish(r, r_ref, e_in, o_refs, i, rows=slice(None), first=True):
        if resid is not None:
            r = r + r_ref[rows, :]
        if epi:
            epi.fn(r, e_in, o_refs, i, rows, first)
        else:
            o_refs[0][rows, :] = r.astype(out_dtype)

    def body(*refs):
        a_ref, b_ref = refs[:2]
        r_ref = refs[2] if resid is not None else None
        pos = 2 + n_res
        p_in = refs[pos:pos + n_pin]
        pos += n_pin
        e_in = refs[pos:pos + n_ein]
        pos += n_ein
        h_in = refs[pos:pos + n_hin]
        pos += n_hin
        o_ref = refs[pos:pos + n_out]
        pos += n_out
        hn_ref = refs[pos] if n_pro else None
        pos += n_pro
        h_out = refs[pos:pos + n_hout]
        scratch = refs[pos + n_hout:]
        n_acc = 1 if nk > 1 else 0
        h_sems = scratch[n_acc + n_pro:]
        i, j, k = pl.program_id(0), pl.program_id(1), pl.program_id(2)
        if host:
            @pl.when((i == 0) & (j == 0) & (k == 0))
            def _():
                host.start(h_in, h_out, h_sems)

            if host.mid is not None:
                @pl.when((i == (3 * ni) // 4) & (j == 0) & (k == 0))
                def _():
                    host.mid(h_in, h_out, h_sems)

        if n_pro:
            a_s = scratch[n_acc]

            @pl.when(j == 0)
            def _():
                a_s[...] = pro.fn(a_ref, p_in)
                hn_ref[...] = a_s[...]

            a_val = a_s[...]
        else:
            a_val = a_ref[...]

        if nk == 1 and epi and epi.split > 1 and mode != "tn":
            part = tm // epi.split
            for p in range(epi.split):
                rows = slice(p * part, (p + 1) * part)
                finish(_dot(a_val[rows, :], b_ref[...], dims), r_ref, e_in, o_ref, i, rows, p == 0)
        elif nk == 1:
            finish(_dot(a_val, b_ref[...], dims), r_ref, e_in, o_ref, i)
        else:
            acc = scratch[0]

            @pl.when(k == 0)
            def _():
                acc[...] = jnp.zeros_like(acc)

            acc[...] += _dot(a_val, b_ref[...], dims)

            @pl.when(k == nk - 1)
            def _():
                finish(acc[...], r_ref, e_in, o_ref, i)

        if host:
            @pl.when((i == ni - 1) & (j == nj - 1) & (k == nk - 1))
            def _():
                host.finish(h_in, h_out, h_sems)

    a_spec = pl.BlockSpec((tk, tm) if mode == "tn" else (tm, tk), a_map)
    b_spec = pl.BlockSpec((tn, tk) if mode == "nt" else (tk, tn), b_map)
    o_spec = pl.BlockSpec((tm, tn), o_map)
    in_specs, args = [a_spec, b_spec], [a, b]
    if resid is not None:
        in_specs.append(o_spec)
        args.append(resid)
    if n_pro:
        in_specs += list(pro.in_specs)
        args += list(pro.ins)
    out_shapes, out_specs = [jax.ShapeDtypeStruct((m, n), out_dtype)], [o_spec]
    if epi:
        in_specs += list(epi.in_specs)
        args += list(epi.ins)
        out_shapes, out_specs = list(epi.out_shapes), list(epi.out_specs)
    if n_pro:
        out_shapes.append(jax.ShapeDtypeStruct((m, kd), BF16))
        out_specs.append(pl.BlockSpec((tm, kd), lambda i, j, k: (i, 0)))
    scratch = ([pltpu.VMEM((tm, tn), F32)] if nk > 1 else []) + ([pltpu.VMEM((tm, kd), BF16)] if n_pro else [])
    sequential = host or n_pro or (epi and epi.sequential)
    outs = pl.pallas_call(
        body, name=name, grid=(ni, nj, nk),
        in_specs=in_specs + [ANY] * n_hin, out_specs=out_specs + [ANY] * n_hout,
        out_shape=out_shapes + (list(host.out_shapes) if host else []),
        scratch_shapes=scratch + (list(host.sems) if host else []),
        input_output_aliases=({2 + n_res + n_pin + n_ein + i_in: n_out + n_pro + i_out
                               for i_in, i_out in host.aliases.items()} if host else {}),
        compiler_params=_cparams(("arbitrary",) * 3 if sequential else ("parallel", "parallel", "arbitrary")),
    )(*args, *(host.ins if host else []))
    main = list(outs[:n_out]) if epi else outs[0]
    if n_pro:
        main = (main, outs[n_out])
    return (main, list(outs[n_out + n_pro:])) if host else main


def _rms(x):
    r = lax.rsqrt(jnp.mean(x * x, axis=-1, keepdims=True) + EPS)
    return x * r, r


def _rms_bwd_math(dhn, x, g):
    xh, r = _rms(x)
    dxh = dhn * g
    dx = r * (dxh - xh * jnp.mean(dxh * xh, axis=-1, keepdims=True))
    return dx, jnp.sum(dhn * xh, axis=0, keepdims=True)


def _row_specs(tm):
    return pl.BlockSpec((tm, D), lambda i, j, k: (i, 0)), pl.BlockSpec((1, D), lambda i, j, k: (0, 0))


def _final_loss_epi(gf, target, t):
    def fn(x, ins, outs, i, rows, first):
        g_ref, t_ref = ins
        dh_ref, dhb_ref, loss_ref, dg_ref = outs

        if first:
            @pl.when(i == 0)
            def _():
                loss_ref[...] = jnp.zeros_like(loss_ref)
                dg_ref[...] = jnp.zeros_like(dg_ref)

        g = g_ref[...]
        xh, _ = _rms(x)
        err = jnp.where(i > 0, xh * g - t_ref[rows, :], 0.0)
        loss_ref[...] += 0.5 * jnp.sum(jnp.sum(err * err, axis=-1, keepdims=True) * (1.0 / D))
        dx, dg = _rms_bwd_math(err * (1.0 / D), x, g)
        dh_ref[rows, :] = dx
        dhb_ref[rows, :] = dx.astype(BF16)
        dg_ref[...] += dg

    row, vec = _row_specs(TM)
    return _Epi(ins=[gf, target], in_specs=[vec, pl.BlockSpec((TM, D), lambda i, j, k: (jnp.maximum(i - 1, 0), 0))],
                out_shapes=[jax.ShapeDtypeStruct((t, D), F32), jax.ShapeDtypeStruct((t, D), BF16),
                            jax.ShapeDtypeStruct((8, 128), F32), jax.ShapeDtypeStruct((1, D), F32)],
                out_specs=[row, row, pl.BlockSpec((8, 128), lambda i, j, k: (0, 0)), vec], fn=fn, sequential=True)


def _rms_bwd_epi(h, g, dres, tm):
    t = h.shape[0]

    def fn(dhn, ins, outs, i, rows, first):
        h_ref, g_ref, dres_ref = ins
        dh_ref, dhb_ref, dg_ref = outs

        if first:
            @pl.when(i == 0)
            def _():
                dg_ref[...] = jnp.zeros_like(dg_ref)

        dx, dg = _rms_bwd_math(dhn, h_ref[rows, :], g_ref[...])
        dh = dres_ref[rows, :] + dx
        dh_ref[rows, :] = dh
        dhb_ref[rows, :] = dh.astype(BF16)
        dg_ref[...] += dg

    row, vec = _row_specs(tm)
    return _Epi(ins=[h, g, dres], in_specs=[row, vec, row],
                out_shapes=[jax.ShapeDtypeStruct((t, D), F32), jax.ShapeDtypeStruct((t, D), BF16),
                            jax.ShapeDtypeStruct((1, D), F32)],
                out_specs=[row, row, vec], fn=fn, sequential=True)


def _rmsnorm1_bwd(dhn, h, g, dres):
    t = h.shape[0]

    def body(dhn_ref, h_ref, g_ref, dres_ref, dx_ref, dmeta_ref, dg_ref):
        i = pl.program_id(0)

        @pl.when(i == 0)
        def _():
            dg_ref[...] = jnp.zeros_like(dg_ref)

        dx, dg = _rms_bwd_math(dhn_ref[...].astype(F32), h_ref[...], g_ref[...])
        dh = dres_ref[...] + dx
        dg_ref[...] += dg
        dx_ref[...] = dh

        @pl.when(i == 0)
        def _():
            dmeta_ref[...] = dh[PAD:, :]

    row = pl.BlockSpec((TM, D), lambda i: (i, 0))
    vec = pl.BlockSpec((1, D), lambda i: (0, 0))
    return pl.pallas_call(
        body, name="rmsnorm1_bwd", grid=(t // TM,),
        in_specs=[row, row, vec, row],
        out_specs=[pl.BlockSpec((TM, D), lambda i: (jnp.maximum(i - 1, 0), 0)),
                   pl.BlockSpec((N_META, D), lambda i: (0, 0)), vec],
        out_shape=[jax.ShapeDtypeStruct((t - ROW0, D), F32), jax.ShapeDtypeStruct((N_META, D), F32),
                   jax.ShapeDtypeStruct((1, D), F32)],
        compiler_params=_cparams(("arbitrary",)),
    )(dhn, h, g, dres)


FH = DFF // 2
FM = 384


def _swiglu_epi(t):
    def fn(r, _, outs, i, rows, first):
        g, u = r[:, :FH], r[:, FH:]
        outs[0][rows, :] = r.astype(BF16)
        outs[1][rows, :] = (g * _sig(g) * u).astype(BF16)

    return _Epi(ins=[], in_specs=[],
                out_shapes=[jax.ShapeDtypeStruct((t, 2 * DFF), BF16), jax.ShapeDtypeStruct((t, DFF), BF16)],
                out_specs=[pl.BlockSpec((FM, 2 * FH), lambda i, j, k: (i, j)), pl.BlockSpec((FM, FH), lambda i, j, k: (i, j))],
                fn=fn)


def _swiglu_bwd_epi(gu):
    def fn(d, ins, outs, i, rows, first):
        gu_t = ins[0][rows, :].astype(F32)
        g, u = gu_t[:, :FH], gu_t[:, FH:]
        s = _sig(g)
        outs[0][rows, :FH] = (d * u * (s * (1.0 + g * (1.0 - s)))).astype(BF16)
        outs[0][rows, FH:] = (d * (g * s)).astype(BF16)

    spec = pl.BlockSpec((FM, 2 * FH), lambda i, j, k: (i, j))
    return _Epi(ins=[gu], in_specs=[spec], out_shapes=[jax.ShapeDtypeStruct(gu.shape, BF16)], out_specs=[spec], fn=fn)


COL_GA, COL_GB = 4, 5
COL_AX, COL_AG = (6 * D + COL_PAD) // CT, (6 * D + COL_PAD + RGW) // CT


def _merge_pro(pb, proj, tm):
    def fn(a_ref, ins):
        pa, pb_, ga, gb = (r[...].astype(F32) for r in (a_ref,) + tuple(ins))
        return (_sig(ga) * pa + _sig(gb) * pb_).astype(BF16)

    row, _ = _row_specs(tm)
    return _Pro([pb, proj, proj], [row, pl.BlockSpec((tm, D), lambda i, j, k: (i, COL_GA)),
                                   pl.BlockSpec((tm, D), lambda i, j, k: (i, COL_GB))], fn)


def _merge_bwd_epi(pa, pb, proj, tm):
    t = pa.shape[0]

    def fn(d, ins, outs, i, rows, first):
        pa, pb, ga, gb = (r[rows, :].astype(F32) for r in ins)
        dpa_ref, dpb_ref, dproj_ref = outs
        sa, sb = _sig(ga), _sig(gb)
        dpa_ref[rows, :] = (sa * d).astype(BF16)
        dpb_ref[rows, :] = (sb * d).astype(BF16)
        dproj_ref[rows, :D] = (d * pa * sa * (1.0 - sa)).astype(BF16)
        dproj_ref[rows, D:] = (d * pb * sb * (1.0 - sb)).astype(BF16)

    row, _ = _row_specs(tm)
    return _Epi(ins=[pa, pb, proj, proj],
                in_specs=[row, row, pl.BlockSpec((tm, D), lambda i, j, k: (i, COL_GA)),
                          pl.BlockSpec((tm, D), lambda i, j, k: (i, COL_GB))],
                out_shapes=[jax.ShapeDtypeStruct((t, D), BF16), jax.ShapeDtypeStruct((t, D), BF16),
                            jax.ShapeDtypeStruct((t, D_INP), BF16)],
                out_specs=[row, row, pl.BlockSpec((tm, 2 * D), lambda i, j, k: (i, 2))], fn=fn)


def _zero_pad_cols(dproj):
    t = dproj.shape[0]
    tz = _pick(t, (768, 256))

    def body(_, o_ref):
        o_ref[...] = jnp.zeros_like(o_ref)

    return pl.pallas_call(
        body, name="dproj_pad", grid=(t // tz,), in_specs=[ANY],
        out_specs=pl.BlockSpec((tz, COL_PAD), lambda i: (i, (6 * D) // COL_PAD)),
        out_shape=jax.ShapeDtypeStruct(dproj.shape, BF16), input_output_aliases={0: 0},
        compiler_params=_cparams(("parallel",)),
    )(dproj)


HALO = 16


def _rows_before(cur, before, s):
    n = cur.shape[0]
    return jnp.concatenate([before, cur], axis=0)[n - s:2 * n - s, :]


def _rows_after(cur, after, s):
    n = cur.shape[0]
    return jnp.concatenate([cur, after], axis=0)[s:n + s, :]


def _prev_halo(col0):
    return pl.BlockSpec((HALO, CT), lambda c, i: (jnp.maximum(i * (TM // HALO) - 1, 0), col0 + c))


def _conv_fwd(proj, w, b):
    t = proj.shape[0]

    def body(x_ref, halo_ref, w_ref, b_ref, o_ref, ob_ref):
        i = pl.program_id(1)
        wv, bv = w_ref[...], b_ref[...]
        first = jnp.where(i > 0, halo_ref[...].astype(F32), 0.0)

        def strip(k, before):
            rows = pl.ds(pl.multiple_of(k * HALO, HALO), HALO)
            x = x_ref[rows, :].astype(F32)
            y = bv + wv[3:4, :] * x
            for j in range(3):
                y = y + wv[j:j + 1, :] * _rows_before(x, before, 3 - j)
            o_ref[rows, :] = y
            ob_ref[rows, :] = y.astype(BF16)
            return x

        lax.fori_loop(0, TM // HALO, strip, first)

    blk = pl.BlockSpec((TM, CT), lambda c, i: (i, c))
    return pl.pallas_call(
        body, name="conv_fwd", grid=(NCT, t // TM),
        in_specs=[pl.BlockSpec((TM, CT), lambda c, i: (i, COL_AX + c)), _prev_halo(COL_AX),
                  pl.BlockSpec((4, CT), lambda c, i: (0, c)), pl.BlockSpec((1, CT), lambda c, i: (0, c))],
        out_specs=[blk, blk],
        out_shape=[jax.ShapeDtypeStruct((t, RGW), F32), jax.ShapeDtypeStruct((t, RGW), BF16)],
        compiler_params=_cparams(("parallel", "parallel")),
    )(proj, proj, w, b)


def _conv_bwd(dxc, proj, w, dproj):
    t = proj.shape[0]
    nt = t // TM

    def body(d_ref, dn_ref, x_ref, halo_ref, w_ref, _, dx_ref, dw_ref, db_ref, acc):
        i = pl.program_id(1)

        @pl.when(i == 0)
        def _():
            dw_ref[...] = jnp.zeros_like(dw_ref)
            db_ref[...] = jnp.zeros_like(db_ref)

        acc[...] = jnp.zeros_like(acc)
        wv = w_ref[...]
        first = jnp.where(i > 0, halo_ref[...].astype(F32), 0.0)
        last = jnp.where(i < nt - 1, dn_ref[...], 0.0)
        ns = TM // HALO

        def fold(v):
            return v[0:8, :] + v[8:16, :]

        def strip(k, before):
            off = pl.multiple_of(k * HALO, HALO)
            rows = pl.ds(off, HALO)
            x, d = x_ref[rows, :].astype(F32), d_ref[rows, :]
            nxt = d_ref[pl.ds(pl.multiple_of(jnp.minimum(off + HALO, TM - HALO), HALO), HALO), :]
            after = jnp.where(k == ns - 1, last, nxt)
            dx = wv[3:4, :] * d
            for j in range(3):
                dx = dx + wv[j:j + 1, :] * _rows_after(d, after, 3 - j)
                acc[j] += fold(d * _rows_before(x, before, 3 - j))
            acc[3] += fold(d * x)
            acc[4] += fold(d)
            dx_ref[rows, :] = dx.astype(BF16)
            return x

        lax.fori_loop(0, ns, strip, first)
        for j in range(4):
            dw_ref[j:j + 1, :] += jnp.sum(acc[j], axis=0, keepdims=True)
        db_ref[...] += jnp.sum(acc[4], axis=0, keepdims=True)

    return pl.pallas_call(
        body, name="conv_bwd", grid=(NCT, nt),
        in_specs=[pl.BlockSpec((TM, CT), lambda c, i: (i, c)),
                  pl.BlockSpec((HALO, CT), lambda c, i: (jnp.minimum((i + 1) * (TM // HALO), t // HALO - 1), c)),
                  pl.BlockSpec((TM, CT), lambda c, i: (i, COL_AX + c)), _prev_halo(COL_AX),
                  pl.BlockSpec((4, CT), lambda c, i: (0, c)), ANY],
        out_specs=[pl.BlockSpec((TM, CT), lambda c, i: (i, COL_AX + c)),
                   pl.BlockSpec((4, CT), lambda c, i: (0, c)), pl.BlockSpec((1, CT), lambda c, i: (0, c))],
        out_shape=[jax.ShapeDtypeStruct((t, D_INP), BF16), jax.ShapeDtypeStruct((4, RGW), F32),
                   jax.ShapeDtypeStruct((1, RGW), F32)],
        scratch_shapes=[pltpu.VMEM((5, 8, CT), F32)],
        input_output_aliases={5: 0},
        compiler_params=_cparams(("parallel", "arbitrary")),
    )(dxc, dxc, proj, proj, w, dproj)


def _gelu(x):
    c = 0.7978845608028654
    th = jnp.tanh(c * (x + 0.044715 * x * x * x))
    return 0.5 * x * (1.0 + th), th


def _rg_gates(gr, gi, xc, ba, bx, lam, row0):
    r = _sig(gr + ba)
    ig = _sig(gi + bx)
    sp = jax.nn.softplus(-lam)
    a = jnp.exp(-RG_C * r * sp)
    s2 = jnp.maximum(1.0 - a * a, 1e-30)
    rs = lax.rsqrt(s2)
    s = s2 * rs
    rows = row0 + lax.broadcasted_iota(jnp.int32, gr.shape, 0)
    live = rows >= PAD
    u = jnp.where(live, s * ig * xc, 0.0)
    return r, ig, sp, a, s, rs, u, live


def _rg_fwd(gri, xc, proj, ba, bx, lam):
    t = gri.shape[0]

    def body(gr_ref, gi_ref, xc_ref, ag_ref, ba_ref, bx_ref, lam_ref, h_ref, ya_ref, hc):
        i = pl.program_id(1)

        @pl.when(i == 0)
        def _():
            hc[...] = jnp.zeros_like(hc)

        ba, bx, lam = ba_ref[...], bx_ref[...], lam_ref[...]
        sub = lax.broadcasted_iota(jnp.int32, (8, CT), 0)

        def strip(k, h):
            off = pl.multiple_of(k * RS, RS)
            rows = pl.ds(off, RS)
            _, _, _, a, _, _, u, _ = _rg_gates(gr_ref[rows, :], gi_ref[rows, :], xc_ref[rows, :], ba, bx, lam,
                                               i * TM + off)
            outs = []
            for half in range(RS // 8):
                out = jnp.zeros((8, CT), F32)
                for r in range(8):
                    h = a[8 * half + r:8 * half + r + 1, :] * h + u[8 * half + r:8 * half + r + 1, :]
                    out = jnp.where(sub == r, h, out)
                outs.append(out)
            hs = jnp.concatenate(outs, axis=0)
            h_ref[rows, :] = hs
            ge, _ = _gelu(ag_ref[rows, :].astype(F32))
            ya_ref[rows, :] = (hs * ge).astype(BF16)
            return h

        hc[...] = lax.fori_loop(0, TM // RS, strip, hc[...])

    blk_ = pl.BlockSpec((TM, CT), lambda c, i: (i, c))
    vec = pl.BlockSpec((1, CT), lambda c, i: (0, c))
    return pl.pallas_call(
        body, name="rg_fwd", grid=(NCT, t // TM),
        in_specs=[blk_, pl.BlockSpec((TM, CT), lambda c, i: (i, NCT + c)), blk_,
                  pl.BlockSpec((TM, CT), lambda c, i: (i, COL_AG + c)), vec, vec, vec],
        out_specs=[blk_, blk_],
        out_shape=[jax.ShapeDtypeStruct((t, RGW), F32), jax.ShapeDtypeStruct((t, RGW), BF16)],
        scratch_shapes=[pltpu.VMEM((1, CT), F32)],
        compiler_params=_cparams(("parallel", "arbitrary")),
    )(gri, gri, xc, proj, ba, bx, lam)


def _rg_bwd(dya, h, gri, xc, proj, ba, bx, lam, dproj):
    t = gri.shape[0]
    nt = t // TM
    assert NCT == 1

    def body(dya_ref, h_ref, hh_ref, gr_ref, gi_ref, xc_ref, ag_ref, ba_ref, bx_ref, lam_ref, _,
             dgri_ref, dxc_ref, dag_ref, dba_ref, dbx_ref, dlam_ref, hbuf, acc, cc):
        i = pl.program_id(1)
        ri = nt - 1 - i

        @pl.when(i == 0)
        def _():
            cc[...] = jnp.zeros_like(cc)
            dba_ref[...] = jnp.zeros_like(dba_ref)
            dbx_ref[...] = jnp.zeros_like(dbx_ref)
            dlam_ref[...] = jnp.zeros_like(dlam_ref)

        acc[...] = jnp.zeros_like(acc)
        ba, bx, lam = ba_ref[...], bx_ref[...], lam_ref[...]
        halo_last = jnp.where(ri > 0, hh_ref[7:8, :], 0.0)
        sub = lax.broadcasted_iota(jnp.int32, (8, CT), 0)
        c0 = 0.7978845608028654

        def strip(kk, c):
            k = TM // RS - 1 - kk
            off = pl.multiple_of(k * RS, RS)
            rows = pl.ds(off, RS)
            xc, hv = xc_ref[rows, :], h_ref[rows, :]
            ag, dya = ag_ref[rows, :].astype(F32), dya_ref[rows, :].astype(F32)
            r, ig, sp, a, s, rs, _, live = _rg_gates(gr_ref[rows, :], gi_ref[rows, :], xc, ba, bx, lam, ri * TM + off)
            ge, th = _gelu(ag)
            dge = 0.5 * (1.0 + th) + 0.5 * ag * (1.0 - th * th) * c0 * (1.0 + 3.0 * 0.044715 * ag * ag)
            dag_ref[rows, :] = (dya * hv * dge).astype(BF16)
            d = dya * ge
            outs = []
            for half in range(RS // 8 - 1, -1, -1):
                out = jnp.zeros((8, CT), F32)
                for rr in range(7, -1, -1):
                    g = d[8 * half + rr:8 * half + rr + 1, :] + c
                    c = a[8 * half + rr:8 * half + rr + 1, :] * g
                    out = jnp.where(sub == rr, g, out)
                outs.insert(0, out)
            g = jnp.concatenate(outs, axis=0)
            before = h_ref[pl.ds(pl.multiple_of(jnp.maximum(off - RS, 0), RS), RS), :]
            hbuf[7:8, :] = jnp.where(k == 0, halo_last, before[RS - 1:RS, :])
            hbuf[8:, :] = hv
            hprev = hbuf[pl.ds(7, RS), :]
            du = jnp.where(live, g, 0.0)
            ds = du * (ig * xc)
            dm = du * s
            dla = (g * hprev - a * ds * rs) * a
            dr = dla * (-RG_C * sp) * r * (1.0 - r)
            di = dm * xc * ig * (1.0 - ig)
            dgri_ref[rows, :CT] = dr.astype(BF16)
            dgri_ref[rows, CT:] = di.astype(BF16)
            dxc_ref[rows, :] = dm * ig
            dl = dla * (-RG_C * r)
            for half in range(RS // 8):
                part = slice(8 * half, 8 * half + 8)
                acc[0] += dr[part]
                acc[1] += di[part]
                acc[2] += dl[part]
            return c

        cc[...] = lax.fori_loop(0, TM // RS, strip, cc[...])
        dba_ref[...] += jnp.sum(acc[0], axis=0, keepdims=True)
        dbx_ref[...] += jnp.sum(acc[1], axis=0, keepdims=True)
        dlam_ref[...] += jnp.sum(acc[2], axis=0, keepdims=True) * (-_sig(-lam))

    rblk = pl.BlockSpec((TM, CT), lambda c, i: (nt - 1 - i, c))
    vec = pl.BlockSpec((1, CT), lambda c, i: (0, c))
    hh = pl.BlockSpec((8, CT), lambda c, i: (jnp.maximum((nt - 1 - i) * (TM // 8) - 1, 0), c))
    agb = pl.BlockSpec((TM, CT), lambda c, i: (nt - 1 - i, COL_AG + c))
    return pl.pallas_call(
        body, name="rg_bwd", grid=(NCT, nt),
        in_specs=[rblk, rblk, hh, rblk, pl.BlockSpec((TM, CT), lambda c, i: (nt - 1 - i, NCT + c)), rblk, agb,
                  vec, vec, vec, ANY],
        out_specs=[pl.BlockSpec((TM, 2 * CT), lambda c, i: (nt - 1 - i, c)), rblk, agb, vec, vec, vec],
        out_shape=[jax.ShapeDtypeStruct((t, 2 * RGW), BF16),
                   jax.ShapeDtypeStruct((t, RGW), F32), jax.ShapeDtypeStruct((t, D_INP), BF16),
                   jax.ShapeDtypeStruct((1, RGW), F32), jax.ShapeDtypeStruct((1, RGW), F32),
                   jax.ShapeDtypeStruct((1, RGW), F32)],
        scratch_shapes=[pltpu.VMEM((RS + 8, CT), F32), pltpu.VMEM((3, 8, CT), F32), pltpu.VMEM((1, CT), F32)],
        input_output_aliases={10: 2},
        compiler_params=_cparams(("parallel", "arbitrary")),
    )(dya, h, h, gri, gri, xc, proj, ba, bx, lam, dproj)


NCH = TM // CH

def _tri_dot(tri, x):
    hi = x.astype(BF16)
    r1 = x - hi.astype(F32)
    mid = r1.astype(BF16)
    lo = (r1 - mid.astype(F32)).astype(BF16)
    return _dot(tri, hi, NN) + _dot(tri, mid, NN) + _dot(tri, lo, NN)


def _hg_chunk(qr, fr, lb):
    sf = _sig(fr)
    fg = lb + (1.0 - lb) * sf
    k = (1.0 - lb) * (1.0 - sf)
    sq = _sig(qr)
    q = qr * sq
    ri = lax.broadcasted_iota(jnp.int32, (CH, CH), 0)
    ci = lax.broadcasted_iota(jnp.int32, (CH, CH), 1)
    b = _tri_dot((ri >= ci).astype(BF16), jnp.log(fg))
    bm, bl = b[CH // 2 - 1:CH // 2, :], b[CH - 1:CH, :]
    ebm = jnp.exp(jnp.minimum(b - bm, EXP_CLAMP))
    ekm = jnp.exp(jnp.minimum(bm - b, EXP_CLAMP))
    eb = ebm * jnp.exp(bm)
    ekl = ekm * jnp.exp(bl - bm)
    return dict(sf=sf, fg=fg, k=k, sq=sq, q=q, eb=eb, ebm=ebm, ekm=ekm, ekl=ekl, ebl=jnp.exp(bl),
                qe=q * eb, qh=q * ebm, kh=k * ekm, kd=k * ekl, causal=ri >= ci, upper=(ci >= ri).astype(BF16))


def _hgrn_fwd(proj, lbl, gn):
    t = proj.shape[0]
    nt = t // TM

    def body(q_ref, f_ref, v_ref, g_ref, lbl_ref, gn_ref, yb_ref, o_ref, st_ref, st):
        @pl.when(pl.program_id(0) == 0)
        def _():
            st[...] = jnp.zeros_like(st)

        l = lbl_ref[...]
        lb = _sig(l[0:1, :] - l[1:2, :])
        gnv = gn_ref[...]

        def chunk(c, carry):
            off = pl.multiple_of(c * CH, CH)
            rows = pl.ds(off, CH)
            z = _hg_chunk(q_ref[rows, :].astype(F32), f_ref[rows, :].astype(F32), lb)
            v, gg = v_ref[rows, :], g_ref[rows, :].astype(F32)
            for hh in range(HEADS):
                sl = slice(hh * HD, (hh + 1) * HD)
                s_prev = st[hh]
                st_ref[c, hh] = s_prev
                vb = v[:, sl].astype(BF16)
                att = jnp.where(z["causal"], _dot(z["qh"][:, sl].astype(BF16), z["kh"][:, sl].astype(BF16), NT), 0.0)
                o = _dot(z["qe"][:, sl].astype(BF16), s_prev.astype(BF16), NT) + _dot(att.astype(BF16), vb, NN)
                st[hh] = s_prev * z["ebl"][:, sl] + _dot(vb, z["kd"][:, sl].astype(BF16), TN_)
                xh, _ = _rms(o)
                gh = gg[:, sl]
                o_ref[rows, sl] = o
                yb_ref[rows, sl] = (xh * gnv * gh * _sig(gh)).astype(BF16)
            return carry

        lax.fori_loop(0, NCH, chunk, 0, unroll=True)

    def col(j):
        return pl.BlockSpec((TM, HGW), lambda i, j=j: (i, j))

    return pl.pallas_call(
        body, name="hgrn_fwd", grid=(nt,),
        in_specs=[col(0), col(1), col(2), col(3), pl.BlockSpec((2, HGW), lambda i: (0, 0)),
                  pl.BlockSpec((1, HD), lambda i: (0, 0))],
        out_specs=[col(0), col(0), pl.BlockSpec((NCH, HEADS, HD, HD), lambda i: (i, 0, 0, 0))],
        out_shape=[jax.ShapeDtypeStruct((t, HGW), BF16), jax.ShapeDtypeStruct((t, HGW), F32),
                   jax.ShapeDtypeStruct((t // CH, HEADS, HD, HD), F32)],
        scratch_shapes=[pltpu.VMEM((HEADS, HD, HD), F32)],
        compiler_params=_cparams(("arbitrary",)),
    )(proj, proj, proj, proj, lbl, gn)


def _hgrn_bwd(dyb, proj, o, states, lbl, gn, dproj):
    t = proj.shape[0]
    nt = t // TM

    def body(dy_ref, q_ref, f_ref, v_ref, g_ref, o_ref, st_ref, lbl_ref, gn_ref, _,
             dp_ref, dgn_ref, dl_ref, dst, dlb):
        i = pl.program_id(0)

        @pl.when(i == 0)
        def _():
            dst[...] = jnp.zeros_like(dst)
            dlb[...] = jnp.zeros_like(dlb)
            dgn_ref[...] = jnp.zeros_like(dgn_ref)

        l = lbl_ref[...]
        lb = _sig(l[0:1, :] - l[1:2, :])
        gnv = gn_ref[...]
        last = lax.broadcasted_iota(jnp.int32, (CH, HD), 0) == CH - 1

        def chunk(cc, carry):
            c = NCH - 1 - cc
            off = pl.multiple_of(c * CH, CH)
            rows = pl.ds(off, CH)
            qr, fr = q_ref[rows, :].astype(F32), f_ref[rows, :].astype(F32)
            z = _hg_chunk(qr, fr, lb)
            v, gg, ov, dy = v_ref[rows, :], g_ref[rows, :].astype(F32), o_ref[rows, :], dy_ref[rows, :].astype(F32)
            dqs, dks, dbs, dvs, dgs = [], [], [], [], []
            dgn = jnp.zeros((1, HD), F32)
            for hh in range(HEADS):
                sl = slice(hh * HD, (hh + 1) * HD)
                s_prev, ds_new = st_ref[c, hh], dst[hh]
                qe, qh, kh, kd = z["qe"][:, sl], z["qh"][:, sl], z["kh"][:, sl], z["kd"][:, sl]
                ebl = z["ebl"][:, sl]
                gh, dyh = gg[:, sl], dy[:, sl]
                xh, rr = _rms(ov[:, sl])
                sg = _sig(gh)
                dyn = dyh * (gh * sg)
                dgs.append(dyh * (xh * gnv) * (sg * (1.0 + gh * (1.0 - sg))))
                dgn = dgn + jnp.sum(dyn * xh, axis=0, keepdims=True)
                dxh = dyn * gnv
                do = (rr * (dxh - xh * jnp.mean(dxh * xh, axis=-1, keepdims=True))).astype(BF16)
                vb, dsb = v[:, sl].astype(BF16), ds_new.astype(BF16)
                qeb, qhb, khb, kdb = (a.astype(BF16) for a in (qe, qh, kh, kd))
                att = jnp.where(z["causal"], _dot(qhb, khb, NT), 0.0).astype(BF16)
                datt = jnp.where(z["causal"], _dot(do, vb, NT), 0.0).astype(BF16)
                dvs.append(_dot(att, do, TN_) + _dot(kdb, dsb, NT))
                dqe = _dot(do, s_prev.astype(BF16), NN)
                dqh = _dot(datt, khb, NN)
                dkh = _dot(datt, qhb, TN_)
                dkd = _dot(vb, dsb, NN)
                qe, qh, kh, kd = (a.astype(F32) for a in (qeb, qhb, khb, kdb))
                dbl = (jnp.sum(dkd * kd, axis=0, keepdims=True)
                       + jnp.sum(ds_new * s_prev, axis=0, keepdims=True) * ebl)
                dqs.append(dqe * z["eb"][:, sl] + dqh * z["ebm"][:, sl])
                dks.append(dkh * z["ekm"][:, sl] + dkd * z["ekl"][:, sl])
                dbs.append(dqe * qe + dqh * qh - dkh * kh - dkd * kd + jnp.where(last, dbl, 0.0))
                dst[hh] = _dot(do, qeb, TN_) + ds_new * ebl
            dgn_ref[...] += dgn
            dq, dk, db = (jnp.concatenate(x, axis=1) for x in (dqs, dks, dbs))
            dlf = _tri_dot(z["upper"], db)
            sf, fg, sq = z["sf"], z["fg"], z["sq"]
            dmix = dlf / fg - dk
            dsf = dmix * (1.0 - lb)
            dlb[...] += jnp.sum(dmix * (1.0 - sf), axis=0, keepdims=True)
            dp_ref[rows, 0:HGW] = (dq * (sq * (1.0 + qr * (1.0 - sq)))).astype(BF16)
            dp_ref[rows, HGW:2 * HGW] = (dsf * sf * (1.0 - sf)).astype(BF16)
            dp_ref[rows, 2 * HGW:3 * HGW] = jnp.concatenate(dvs, axis=1).astype(BF16)
            dp_ref[rows, 3 * HGW:4 * HGW] = jnp.concatenate(dgs, axis=1).astype(BF16)
            return carry

        lax.fori_loop(0, NCH, chunk, 0, unroll=True)
        dl0 = dlb[...] * lb * (1.0 - lb)
        dl_ref[0:1, :] = dl0
        dl_ref[1:2, :] = -dl0

    def col(j):
        return pl.BlockSpec((TM, HGW), lambda i, j=j: (nt - 1 - i, j))

    return pl.pallas_call(
        body, name="hgrn_bwd", grid=(nt,),
        in_specs=[col(0), col(0), col(1), col(2), col(3), col(0),
                  pl.BlockSpec((NCH, HEADS, HD, HD), lambda i: (nt - 1 - i, 0, 0, 0)),
                  pl.BlockSpec((2, HGW), lambda i: (0, 0)), pl.BlockSpec((1, HD), lambda i: (0, 0)), ANY],
        out_specs=[pl.BlockSpec((TM, 4 * HGW), lambda i: (nt - 1 - i, 0)),
                   pl.BlockSpec((1, HD), lambda i: (0, 0)), pl.BlockSpec((2, HGW), lambda i: (0, 0))],
        out_shape=[jax.ShapeDtypeStruct((t, D_INP), BF16), jax.ShapeDtypeStruct((1, HD), F32),
                   jax.ShapeDtypeStruct((2, HGW), F32)],
        scratch_shapes=[pltpu.VMEM((HEADS, HD, HD), F32), pltpu.VMEM((1, HGW), F32)],
        input_output_aliases={9: 0},
        compiler_params=_cparams(("arbitrary",)),
    )(dyb, proj, proj, proj, proj, o, states, lbl, gn, dproj)


def _rotate_w_in(w):
    return jnp.concatenate([w[:, ROT:], jnp.zeros((w.shape[0], COL_PAD), w.dtype), w[:, :ROT]], axis=1)


def _unrotate_g_in(g):
    return jnp.concatenate([g[:, D_INP - ROT:], g[:, :D_IN - ROT]], axis=1)


SB = RGW // 2


def _bd_gates(t):
    tm = _pick(t, (768, 256))

    def col(j):
        return 2 * (j % 2) + j // 2

    return ((t // tm, 4, 1), (tm, SB, SB),
            (lambda i, j, k: (i, j // 2), lambda i, j, k: (j // 2, col(j)), lambda i, j, k: (i, col(j))))


def _bd_dw(t):
    tk = _pick(t, (1408, 768))
    return (1, 4, t // tk), (SB, SB, tk), (lambda i, j, k: (k, j % 2), lambda i, j, k: (k, j), lambda i, j, k: (j % 2, j))


def _block_diag(w):
    eye = jnp.eye(RG_BLOCKS, dtype=w.dtype)
    return (w[:, :, None, :] * eye[:, None, :, None]).reshape(RGW, RGW)


def _diag_blocks(wd):
    w4 = wd.reshape(RG_BLOCKS, RG_BLOCK, RG_BLOCKS, RG_BLOCK)
    return jnp.stack([w4[n, :, n, :] for n in range(RG_BLOCKS)])


def _local_step(h0, target, wts, small, hosts=None, finalize=None):
    hosts = hosts or {}
    carried = {}
    g = {}
    t = h0.shape[0]

    def mm(a, b, mode, out_dtype, name, **kw):
        if name not in hosts:
            return _mm(a, b, mode, out_dtype, name, **kw)
        host = hosts[name](g)
        out, res = _mm(a, b, mode, out_dtype, name, host=host, **kw)
        carried[name] = (host.ins, res)
        return out

    w_in = wts["w_in"]
    wax_d = jnp.concatenate([_block_diag(small["rg_wa"]), _block_diag(small["rg_wx"])], axis=1).astype(BF16)
    ba, bx, lam = small["rg_ba"], small["rg_bx"], small["rg_lambda"]
    lbl, gn = small["hg_lb_logits"], small["hg_norm_g"]
    conv_w, conv_b = small["conv_w"], small["conv_b"]

    tmm = _pick(t, (768, 256))
    proj, hn1 = mm(h0, w_in, "nn", BF16, "mm_proj", pro=_norm_pro(small["norm1_g"]))
    if finalize is not None:
        wts = finalize(carried["mm_proj"][1])
    w_pa, w_pb, w_out, w_fi, w_fd = (wts[k] for k in ("w_proj_a", "w_proj_b", "w_out", "w_ffn_in", "w_ffn_down"))
    xc, xcb = _conv_fwd(proj, conv_w, conv_b)
    gri = mm(xcb, wax_d, "nn", F32, "mm_rg_gates", sparse=_bd_gates(t))
    hrg, ya = _rg_fwd(gri, xc, proj, ba, bx, lam)
    yb, o, states = _hgrn_fwd(proj, lbl, gn)
    pa = mm(ya, w_pa, "nn", BF16, "mm_pa")
    pb = mm(yb, w_pb, "nn", BF16, "mm_pb")
    h1, merged = mm(pa, w_out, "nn", F32, "mm_out", resid=h0, pro=_merge_pro(pb, proj, tmm), tiles=(tmm, D))
    (gu, act), hn2 = mm(h1, w_fi, "nn", BF16, "mm_ffn_in", pro=_norm_pro(small["norm2_g"]), epi=_swiglu_epi(t),
                        tiles=(FM, 2 * FH))
    dh2, dh2b, loss, g["norm_f_g"] = mm(act, w_fd, "nn", F32, "mm_ffn_down", resid=h1,
                                        epi=_final_loss_epi(small["norm_f_g"], target, t), tiles=(TM, D))

    g["w_ffn_down"] = mm(act, dh2b, "tn", BF16, "mm_d_wfd")
    (dgu,) = mm(dh2b, w_fd, "nt", BF16, "mm_d_act", epi=_swiglu_bwd_epi(gu), tiles=(FM, FH))
    g["w_ffn_in"] = mm(hn2, dgu, "tn", BF16, "mm_d_wfi")
    dh1, dh1b, g["norm2_g"] = mm(dgu, w_fi, "nt", BF16, "mm_d_hn2", epi=_rms_bwd_epi(h1, small["norm2_g"], dh2, tmm),
                                 tiles=(tmm, D))
    g["w_out"] = mm(merged, dh1b, "tn", BF16, "mm_d_wout")
    dpa, dpb, dproj = mm(dh1b, w_out, "nt", BF16, "mm_d_merged", epi=_merge_bwd_epi(pa, pb, proj, TM), tiles=(TM, D))
    dproj = _zero_pad_cols(dproj)
    g["w_proj_a"] = mm(ya, dpa, "tn", BF16, "mm_d_wpa")
    g["w_proj_b"] = mm(yb, dpb, "tn", BF16, "mm_d_wpb")
    dya = mm(dpa, w_pa, "nt", BF16, "mm_d_ya")
    dyb = mm(dpb, w_pb, "nt", BF16, "mm_d_yb")
    dproj, g["hg_norm_g"], g["hg_lb_logits"] = _hgrn_bwd(dyb, proj, o, states, lbl, gn, dproj)
    dgri, dxc, dproj, g["rg_ba"], g["rg_bx"], g["rg_lambda"] = _rg_bwd(dya, hrg, gri, xc, proj, ba, bx, lam, dproj)
    dwax = mm(xcb, dgri, "tn", F32, "mm_d_wax", sparse=_bd_dw(t))
    g["rg_wa"], g["rg_wx"] = _diag_blocks(dwax[:, :RGW]), _diag_blocks(dwax[:, RGW:])
    dxc = mm(dgri, wax_d, "nt", F32, "mm_d_xc", resid=dxc)
    dproj, g["conv_w"], g["conv_b"] = _conv_bwd(dxc, proj, conv_w, dproj)
    g["w_in"] = mm(hn1, dproj, "tn", BF16, "mm_d_win")
    dhn1 = mm(dproj, w_in, "nt", BF16, "mm_d_hn1")
    grad_x, g["meta_tokens"], g["norm1_g"] = _rmsnorm1_bwd(dhn1, h0, small["norm1_g"], dh1)
    return loss, grad_x, g, carried


def _adamw_math(w, g, m, v):
    m = B1 * m + (1.0 - B1) * g
    v = B2 * v + (1.0 - B2) * (g * g)
    m_hat = m / (1.0 - B1 ** STEP)
    v_hat = v / (1.0 - B2 ** STEP)
    return -LR * (m_hat / (jnp.sqrt(v_hat) + ADAM_EPS) + WD * w), m, v


def _adamw(w, g, m, v, name):
    r, c = w.shape
    tr = _pick(r, (256, 352, 320, 128, 64, 32, 16, 8))

    def body(w_ref, g_ref, m_ref, v_ref, d_ref, mo_ref, vo_ref):
        d_ref[...], mo_ref[...], vo_ref[...] = _adamw_math(w_ref[...], g_ref[...], m_ref[...], v_ref[...])

    blk = pl.BlockSpec((tr, c), lambda i: (i, 0))
    return pl.pallas_call(
        body, name=name, grid=(r // tr,), in_specs=[blk] * 4, out_specs=[blk] * 3,
        out_shape=[jax.ShapeDtypeStruct((r, c), F32)] * 3,
        compiler_params=_cparams(("parallel",)),
    )(w, g, m, v)


def _adamw_rows(recv, params, extras, name):
    _, _, width = recv.shape
    n_par = len(params)

    def body(*refs):
        r_ref = refs[0]
        ins = refs[1:1 + 3 * n_par]
        outs = refs[1 + 3 * n_par:]
        g_all = r_ref[0]
        for p in range(1, 8):
            g_all = g_all + r_ref[p]
        for q, (row0, w, _, _) in enumerate(params):
            g = g_all[row0:row0 + w.shape[0], :]
            w_ref, m_ref, v_ref = ins[3 * q:3 * q + 3]
            g_ref, d_ref, mo_ref, vo_ref = outs[4 * q:4 * q + 4]
            g_ref[...] = g
            d_ref[...], mo_ref[...], vo_ref[...] = _adamw_math(w_ref[...], g, m_ref[...], v_ref[...])
        for q, (row0, n) in enumerate(extras):
            outs[4 * n_par + q][...] = g_all[row0:row0 + n, :]

    shapes = [jax.ShapeDtypeStruct(w.shape, F32) for _, w, _, _ in params for _ in range(4)]
    shapes += [jax.ShapeDtypeStruct((n, width), F32) for _, n in extras]
    vm = pl.BlockSpec(memory_space=pltpu.VMEM)
    outs = pl.pallas_call(
        body, name=name, in_specs=[vm] * (1 + 3 * n_par), out_specs=[vm] * len(shapes), out_shape=shapes,
        compiler_params=pltpu.CompilerParams(vmem_limit_bytes=VMEM_LIMIT),
    )(recv, *[a for _, w, m, v in params for a in (w, m, v)])
    return [tuple(outs[4 * q:4 * q + 4]) for q in range(n_par)], list(outs[4 * n_par:])


def _col_block(weight, px, py):
    return 2 * py + px if weight == "w_ffn_in" else 2 * px + py


def _sum_place(recv, gfull, weight, xyc, name):
    _, r, c = recv.shape
    tr = _pick(r, (128, 176, 160, 64, 32, 16))
    nb = r // tr

    def body(_, own_ref, r_ref, o_ref):
        g = own_ref[...].astype(F32)
        for p in range(7):
            g = g + r_ref[p].astype(F32)
        o_ref[...] = g

    if weight in COL_SHARDED:
        own = pl.BlockSpec((tr, c), lambda i, s: (s[2] * nb + i, _col_block(weight, s[0], s[1])))
    else:
        own = pl.BlockSpec((tr, c), lambda i, s: ((4 * s[0] + 2 * s[1] + s[2]) * nb + i, 0))
    return pl.pallas_call(
        body, name=name,
        grid_spec=pltpu.PrefetchScalarGridSpec(
            num_scalar_prefetch=1, grid=(nb,),
            in_specs=[own, pl.BlockSpec((7, tr, c), lambda i, s: (0, i, 0))],
            out_specs=pl.BlockSpec((tr, c), lambda i, s: (s[2] * nb + i, 0))),
        out_shape=jax.ShapeDtypeStruct((2 * r, c), F32),
        compiler_params=_cparams(("arbitrary",)),
    )(xyc, gfull, recv)


def _place_own(shard, slot, dtype, name):
    r, c = shard.shape
    tr = _pick(r, (256, 352, 320, 128, 32))

    def body(_, x_ref, o_ref):
        o_ref[...] = x_ref[...].astype(dtype)

    return pl.pallas_call(
        body, name=name,
        grid_spec=pltpu.PrefetchScalarGridSpec(
            num_scalar_prefetch=1, grid=(r // tr,),
            in_specs=[pl.BlockSpec((tr, c), lambda i, s: (i, 0))],
            out_specs=pl.BlockSpec((None, tr, c), lambda i, s: (s[0], i, 0))),
        out_shape=jax.ShapeDtypeStruct((4, r, c), dtype),
        compiler_params=_cparams(("arbitrary",)),
    )(slot, shard)


def _place_cols(shard, slot, name):
    r, c = shard.shape
    tr = _pick(r, (256, 128))

    def body(_, x_ref, o_ref):
        o_ref[...] = x_ref[...].astype(BF16)

    return pl.pallas_call(
        body, name=name,
        grid_spec=pltpu.PrefetchScalarGridSpec(
            num_scalar_prefetch=1, grid=(r // tr,),
            in_specs=[pl.BlockSpec((tr, c), lambda i, s: (i, 0))],
            out_specs=pl.BlockSpec((tr, c), lambda i, s: (i, 2 * (s[0] % 2) + s[0] // 2))),
        out_shape=jax.ShapeDtypeStruct((r, 4 * c), BF16),
        compiler_params=_cparams(("arbitrary",)),
    )(slot, shard)


def _place_w_in(shard, slot):
    r, c = shard.shape
    nb, npad = c // 128, COL_PAD // 128
    rot_blocks, in_blocks = ROT // 128, D_IN // 128

    def out_block(j, s):
        nat = s[0] * nb + jnp.minimum(j, nb - 1)
        rot = jnp.where(nat >= rot_blocks, nat - rot_blocks, nat + (D_INP - ROT) // 128)
        return jnp.where(j < nb, rot, (in_blocks - rot_blocks) + (j - nb))

    def body(_, x_ref, o_ref):
        o_ref[...] = jnp.where(pl.program_id(0) < nb, x_ref[...], 0.0).astype(BF16)

    return pl.pallas_call(
        body, name="place_w_in",
        grid_spec=pltpu.PrefetchScalarGridSpec(
            num_scalar_prefetch=1, grid=(nb + npad,),
            in_specs=[pl.BlockSpec((r, 128), lambda j, s: (0, jnp.minimum(j, nb - 1)))],
            out_specs=pl.BlockSpec((r, 128), lambda j, s: (0, out_block(j, s)))),
        out_shape=jax.ShapeDtypeStruct((r, D_INP), BF16),
        compiler_params=_cparams(("arbitrary",)),
    )(slot, shard)


BIG = ("w_in", "w_proj_a", "w_proj_b", "w_out", "w_ffn_in", "w_ffn_down")
COL_SHARDED = ("w_in", "w_ffn_in")


def _coords():
    return lax.axis_index("x"), lax.axis_index("y"), lax.axis_index("c")


W_IN_HEAD = 384


def _shard_views(b, layout, chip, half_of):
    if layout == "slots":
        half = b.shape[1] // 2
        return [b.at[chip, pl.ds(pl.multiple_of(half_of * half, 8), half)]]
    half = b.shape[0] // 2
    rows = pl.ds(pl.multiple_of(half_of * half, 8), half)
    if layout == "cols":
        c = b.shape[1] // 4
        return [b.at[rows, pl.ds(pl.multiple_of((2 * (chip % 2) + chip // 2) * c, 128), c)]]
    c = D_IN // 4
    views = []
    for first, width in ((0, W_IN_HEAD), (W_IN_HEAD, c - W_IN_HEAD)):
        nat = chip * c + first
        rot = jnp.where(nat >= ROT, nat - ROT, nat + D_INP - ROT)
        views.append(b.at[rows, pl.ds(pl.multiple_of(rot, 128), width)])
    return views


def _gather_host(bufs, layouts):
    n = len(bufs)

    def place():
        x, y, c = _coords()
        return x, y, c, [(1 - x, y), (x, 1 - y), (1 - x, 1 - y)]

    def copies(b, sems, w, k, chip, half_of, to):
        return [pltpu.make_async_remote_copy(src_ref=v, dst_ref=v, send_sem=sems[0].at[w, k, p], recv_sem=sems[1].at[w, k, p],
                                             device_id=to, device_id_type=MESH)
                for p, v in enumerate(_shard_views(b, layouts[w], chip, half_of))]

    def start(_, outs, sems):
        x, y, c, chips = place()
        for w, b in enumerate(outs):
            for k, (cx, cy) in enumerate(chips):
                for cp in copies(b, sems, w, k, 2 * x + y, c, (cx, cy, c)):
                    cp.start()

    def mid(_, outs, sems):
        x, y, c, chips = place()
        for w, b in enumerate(outs):
            for k, (cx, cy) in enumerate(chips):
                for cp in copies(b, sems, w, k, 2 * cx + cy, c, (cx, cy, c)):
                    cp.wait_recv()
                for cp in copies(b, sems, w, 3 + k, 2 * cx + cy, c, (x, y, 1 - c)):
                    cp.start()

    def finish(_, outs, sems):
        x, y, c, chips = place()
        for w, b in enumerate(outs):
            for k, (cx, cy) in enumerate(chips):
                for cp in copies(b, sems, w, 3 + k, 2 * cx + cy, 1 - c, (x, y, 1 - c)):
                    cp.wait_recv()
        for w, b in enumerate(outs):
            for k, (cx, cy) in enumerate(chips):
                for cp in copies(b, sems, w, k, 2 * x + y, c, (cx, cy, c)) + copies(b, sems, w, 3 + k, 2 * cx + cy, c, (x, y, 1 - c)):
                    cp.wait_send()

    return _Host(ins=list(bufs), out_shapes=[jax.ShapeDtypeStruct(b.shape, b.dtype) for b in bufs],
                 aliases={i: i for i in range(n)},
                 sems=[pltpu.SemaphoreType.DMA((n, 6, 2)), pltpu.SemaphoreType.DMA((n, 6, 2))],
                 start=start, mid=mid, finish=finish)


def _run_host(host, name):
    n_in, n_out = len(host.ins), len(host.out_shapes)

    def body(*refs):
        ins, outs, sems = refs[:n_in], refs[n_in:n_in + n_out], refs[n_in + n_out:]
        host.start(ins, outs, sems)
        if host.mid is not None:
            host.mid(ins, outs, sems)
        host.finish(ins, outs, sems)

    return pl.pallas_call(
        body, name=name, in_specs=[ANY] * n_in, out_specs=[ANY] * n_out, out_shape=list(host.out_shapes),
        scratch_shapes=list(host.sems), input_output_aliases=dict(host.aliases),
    )(*host.ins)


def _peer(x, y, c, k):
    fx, fy, fc = (k >> 2) & 1, (k >> 1) & 1, k & 1
    return (x ^ fx if fx else x, y ^ fy if fy else y, c ^ fc if fc else c)


def _sub_shape(name, full_shape):
    r, c = full_shape
    return (r // 2, c // 4) if name in COL_SHARDED else (r // 8, c)


def _exchange_host(names, grads):
    n = len(names)
    shapes = [_sub_shape(k, g.shape) for k, g in zip(names, grads)]

    def copy(ins, outs, sems, w, k):
        x, y, c = _coords()
        px, py, pc = _peer(x, y, c, k)
        sr, sc = shapes[w]
        if names[w] in COL_SHARDED:
            col = _col_block(names[w], px, py) * sc
            src = ins[w].at[pl.ds(pl.multiple_of(pc * sr, 16), sr), pl.ds(pl.multiple_of(col, 128), sc)]
        else:
            src = ins[w].at[pl.ds(pl.multiple_of((4 * px + 2 * py + pc) * sr, 16), sr)]
        return pltpu.make_async_remote_copy(
            src_ref=src, dst_ref=outs[w].at[k - 1], send_sem=sems[0].at[w, k - 1], recv_sem=sems[1].at[w, k - 1],
            device_id=(px, py, pc), device_id_type=MESH)

    def start(ins, outs, sems):
        for w in range(n):
            for k in range(1, 8):
                copy(ins, outs, sems, w, k).start()

    def finish(ins, outs, sems):
        for w in range(n):
            for k in range(1, 8):
                copy(ins, outs, sems, w, k).wait_recv()
        for w in range(n):
            for k in range(1, 8):
                copy(ins, outs, sems, w, k).wait_send()

    return _Host(ins=list(grads), out_shapes=[jax.ShapeDtypeStruct((7,) + s, g.dtype) for s, g in zip(shapes, grads)],
                 aliases={}, sems=[pltpu.SemaphoreType.DMA((n, 7)), pltpu.SemaphoreType.DMA((n, 7))],
                 start=start, mid=None, finish=finish)


def _sibling_host(bufs):
    n = len(bufs)

    def copy(outs, sems, w, half_of):
        x, y, c = _coords()
        half = outs[w].shape[0] // 2
        rows = outs[w].at[pl.ds(pl.multiple_of(half_of * half, 8), half)]
        return pltpu.make_async_remote_copy(src_ref=rows, dst_ref=rows, send_sem=sems[0].at[w], recv_sem=sems[1].at[w],
                                            device_id=(x, y, 1 - c), device_id_type=MESH)

    def start(_, outs, sems):
        c = lax.axis_index("c")
        for w in range(n):
            copy(outs, sems, w, c).start()

    def finish(_, outs, sems):
        c = lax.axis_index("c")
        for w in range(n):
            copy(outs, sems, w, 1 - c).wait_recv()
        for w in range(n):
            copy(outs, sems, w, c).wait_send()

    return _Host(ins=list(bufs), out_shapes=[jax.ShapeDtypeStruct(b.shape, b.dtype) for b in bufs],
                 aliases={i: i for i in range(n)},
                 sems=[pltpu.SemaphoreType.DMA((n,)), pltpu.SemaphoreType.DMA((n,))], start=start, mid=None, finish=finish)


def _pack_host(pack):
    def me_of():
        x, y, c = _coords()
        return x, y, c, 4 * x + 2 * y + c

    def copy(ins, outs, sems, k, slot):
        x, y, c, _ = me_of()
        return pltpu.make_async_remote_copy(src_ref=ins[0], dst_ref=outs[0].at[slot], send_sem=sems[0].at[k - 1],
                                            recv_sem=sems[1].at[k - 1], device_id=_peer(x, y, c, k), device_id_type=MESH)

    def start(ins, outs, sems):
        me = me_of()[3]
        pltpu.make_async_copy(ins[0], outs[0].at[me], sems[2]).start()
        for k in range(1, 8):
            copy(ins, outs, sems, k, me).start()

    def finish(ins, outs, sems):
        x, y, c, me = me_of()
        for k in range(1, 8):
            px, py, pc = _peer(x, y, c, k)
            copy(ins, outs, sems, k, 4 * px + 2 * py + pc).wait_recv()
        for k in range(1, 8):
            copy(ins, outs, sems, k, me).wait_send()
        pltpu.make_async_copy(ins[0], outs[0].at[me], sems[2]).wait()

    return _Host(ins=[pack], out_shapes=[jax.ShapeDtypeStruct((8,) + pack.shape, F32)], aliases={},
                 sems=[pltpu.SemaphoreType.DMA((7,)), pltpu.SemaphoreType.DMA((7,)), pltpu.SemaphoreType.DMA],
                 start=start, mid=None, finish=finish)


def _join_hosts(hosts):
    ins, outs, sems, aliases, spans = [], [], [], {}, []
    for h in hosts:
        spans.append((len(ins), len(h.ins), len(outs), len(h.out_shapes), len(sems), len(h.sems)))
        for i_in, i_out in h.aliases.items():
            aliases[len(ins) + i_in] = len(outs) + i_out
        ins, outs, sems = ins + list(h.ins), outs + list(h.out_shapes), sems + list(h.sems)

    def phase(which):
        fns = [getattr(h, which) for h in hosts]
        if all(f is None for f in fns):
            return None

        def run(i_refs, o_refs, s_refs):
            for f, (i0, n_i, o0, n_o, s0, n_s) in zip(fns, spans):
                if f is not None:
                    f(i_refs[i0:i0 + n_i], o_refs[o0:o0 + n_o], s_refs[s0:s0 + n_s])

        return run

    return _Host(ins, outs, aliases, sems, phase("start"), phase("mid"), phase("finish"))


SHARDED_SMALL = ("meta_tokens", "conv_w")

ROWS = {"conv_b": (1, RGW), "rg_ba": (1, RGW), "rg_bx": (1, RGW), "rg_lambda": (1, RGW), "conv_w": (4, RGW),
        "norm2_g": (1, D), "norm_f_g": (1, D), "hg_lb_logits": (2, D),
        "rg_wa": (RGW * RG_BLOCK // 128, 128), "rg_wx": (RGW * RG_BLOCK // 128, 128), "hg_norm_g": (1, 128),
        "norm1_g": (1, D), "meta_tokens": (N_META, D), "loss": (1, D)}
PACKS = {"early_wide": ("conv_b", "rg_ba", "rg_bx", "rg_lambda", "conv_w"),
         "early_model": ("norm2_g", "norm_f_g", "hg_lb_logits"),
         "early_lane": ("rg_wa", "rg_wx", "hg_norm_g"),
         "late": ("norm1_g", "meta_tokens", "loss")}
EARLY = ("early_wide", "early_model", "early_lane")
NO_UPDATE = ("conv_w", "meta_tokens", "loss")


def _pack_rows(vals, names):
    parts, first, row = [], {}, 0
    for k in names:
        first[k] = row
        parts.append(vals[k].reshape(ROWS[k]).astype(F32))
        row += ROWS[k][0]
    if row % 8:
        parts.append(jnp.zeros((-row % 8, ROWS[names[0]][1]), F32))
    return jnp.concatenate(parts, axis=0), first


ORDER = ("meta_tokens", "norm1_g", "w_in", "conv_w", "conv_b", "rg_wa", "rg_ba", "rg_wx", "rg_bx", "rg_lambda",
         "hg_lb_logits", "hg_norm_g", "w_proj_a", "w_proj_b", "w_out", "norm2_g", "w_ffn_in", "w_ffn_down", "norm_f_g")


def kernel(x, meta_tokens, norm1_g, w_in, conv_w, conv_b, rg_wa, rg_ba, rg_wx, rg_bx, rg_lambda, hg_lb_logits, hg_norm_g, w_proj_a, w_proj_b, w_out, norm2_g, w_ffn_in, w_ffn_down, norm_f_g, loss_target, m_meta_tokens, m_norm1_g, m_w_in, m_conv_w, m_conv_b, m_rg_wa, m_rg_ba, m_rg_wx, m_rg_bx, m_rg_lambda, m_hg_lb_logits, m_hg_norm_g, m_w_proj_a, m_w_proj_b, m_w_out, m_norm2_g, m_w_ffn_in, m_w_ffn_down, m_norm_f_g, v_meta_tokens, v_norm1_g, v_w_in, v_conv_w, v_conv_b, v_rg_wa, v_rg_ba, v_rg_wx, v_rg_bx, v_rg_lambda, v_hg_lb_logits, v_hg_norm_g, v_w_proj_a, v_w_proj_b, v_w_out, v_norm2_g, v_w_ffn_in, v_w_ffn_down, v_norm_f_g):
    args = dict(locals())
    w = {k: args[k] for k in ORDER}
    m = {k: args["m_" + k] for k in ORDER}
    v = {k: args["v_" + k] for k in ORDER}
    xi, yi, ci = _coords()
    chip = 2 * xi + yi
    slot = jnp.reshape(chip, (1,)).astype(jnp.int32)
    xyc = jnp.stack([xi, yi, ci]).astype(jnp.int32)

    tiny = jnp.zeros((32, 384), F32)
    tiny = tiny.at[0:N_META, 0:256].set(meta_tokens).at[N_META:N_META + 4, 0:320].set(conv_w[0])
    later = [k for k in BIG if k != "w_in"]
    layouts = {k: "cols" if k in COL_SHARDED else "slots" for k in later}
    slots = {k: _place_cols(w[k][0], slot, "place_" + k) if layouts[k] == "cols"
             else _place_own(w[k][0], slot, BF16, "place_" + k) for k in later}
    w_in_full, tiny_all = _run_host(
        _gather_host([_place_w_in(w["w_in"][0], slot), _place_own(tiny, slot, F32, "place_tiny")], ["w_in", "slots"]),
        "allgather_w_in")

    def full_matrix(k, gth):
        return gth if layouts[k] == "cols" else gth.reshape(-1, gth.shape[2])

    meta_full = jnp.transpose(tiny_all[:, 0:N_META, 0:256], (1, 0, 2)).reshape(N_META, D)
    conv_w_full = jnp.transpose(tiny_all[:, N_META:N_META + 4, 0:320], (1, 0, 2)).reshape(4, RGW)

    small = dict(norm1_g=norm1_g, conv_w=conv_w_full, conv_b=conv_b, rg_wa=rg_wa[0], rg_ba=rg_ba, rg_wx=rg_wx[0],
                 rg_bx=rg_bx, rg_lambda=rg_lambda, hg_lb_logits=hg_lb_logits, hg_norm_g=hg_norm_g,
                 norm2_g=norm2_g, norm_f_g=norm_f_g.reshape(1, D))

    hosts = {
        "mm_proj": lambda g: _gather_host([slots[k] for k in later], [layouts[k] for k in later]),
        "mm_d_act": lambda g: _exchange_host(["w_ffn_down"], [g["w_ffn_down"]]),
        "mm_d_hn2": lambda g: _exchange_host(["w_ffn_in"], [g["w_ffn_in"]]),
        "mm_d_win": lambda g: _join_hosts([
            _exchange_host(["w_out", "w_proj_a", "w_proj_b"], [g["w_out"], g["w_proj_a"], g["w_proj_b"]]),
            *[_pack_host(_pack_rows(g, PACKS[p])[0]) for p in EARLY]]),
        "mm_d_hn1": lambda g: _exchange_host(["w_in"], [_unrotate_g_in(g["w_in"])]),
    }
    h0 = jnp.concatenate([jnp.zeros((PAD, D), F32), meta_full, x[0]], axis=0)
    loss_blk, grad_x, g, carried = _local_step(
        h0, loss_target[0], {"w_in": w_in_full}, small, hosts=hosts,
        finalize=lambda gathered: {k: full_matrix(k, gth) for k, gth in zip(later, gathered)})
    g["loss"] = jnp.broadcast_to(loss_blk[0:1, 0:1], (1, D))

    halves = {}
    sources = {"mm_d_act": ["w_ffn_down"], "mm_d_hn2": ["w_ffn_in"], "mm_d_win": ["w_out", "w_proj_a", "w_proj_b"],
               "mm_d_hn1": ["w_in"]}
    for name, keys in sources.items():
        partials, received = carried[name]
        for k, part, rec in zip(keys, partials, received):
            halves[k] = _sum_place(rec, part, k, xyc, "sum_" + k)
    late, _ = _pack_rows(g, PACKS["late"])
    tail = _run_host(_join_hosts([_sibling_host([halves[k] for k in BIG]), _pack_host(late)]), "swap_and_late_pack")
    g_big = dict(zip(BIG, tail[:len(BIG)]))
    recv_packs = dict(zip(EARLY, carried["mm_d_win"][1][3:]), late=tail[len(BIG)])

    grad, delta, new_m, new_v, summed = {}, {}, {}, {}, {}
    for pack, names in PACKS.items():
        row, params, extras = 0, [], []
        for k in names:
            if k in NO_UPDATE:
                extras.append((row, ROWS[k][0]))
            else:
                params.append((row,) + tuple(d[k].reshape(ROWS[k]) for d in (w, m, v)))
            row += ROWS[k][0]
        updated, rows_only = _adamw_rows(recv_packs[pack], params, extras, "adamw_" + pack)
        for k, res in zip([k for k in names if k not in NO_UPDATE], updated):
            grad[k], delta[k], new_m[k], new_v[k] = (a.reshape(w[k].shape) for a in res)
        summed.update(zip([k for k in names if k in NO_UPDATE], rows_only))
    loss = summed["loss"][0, 0]
    g_shard = {"meta_tokens": lax.dynamic_slice(summed["meta_tokens"], (0, chip * 256), (N_META, 256)),
               "conv_w": lax.dynamic_slice(summed["conv_w"], (0, chip * 320), (4, 320))}
    for k in BIG + SHARDED_SMALL:
        gk = g_big[k] if k in BIG else g_shard[k]
        wk, mk, vk = (d[k].reshape(gk.shape) for d in (w, m, v))
        dk, mk, vk = _adamw(wk, gk, mk, vk, "adamw_" + k)
        grad[k], delta[k], new_m[k], new_v[k] = (a.reshape(w[k].shape) for a in (gk, dk, mk, vk))

    return (loss, grad_x[None], *[grad[k] for k in ORDER], *[delta[k] for k in ORDER],
            *[new_m[k] for k in ORDER], *[new_v[k] for k in ORDER])
```

```python
import functools

import jax
import jax.numpy as jnp
from jax import lax
from jax.experimental import pallas as pl
from jax.experimental.pallas import tpu as pltpu

F32, BF16 = jnp.float32, jnp.bfloat16
D = 1024
N_META = 16
RGW = 1280
RG_BLOCKS, RG_BLOCK = 16, 80
RG_C = 8.0
HEADS, HD = 8, 128
HGW = HEADS * HD
DFF = 2816
D_IN = 2 * RGW + 6 * D
ROT = 2 * RGW
COL_PAD = 256
D_INP = D_IN + COL_PAD
EPS = 1e-6
CH = 64
TM = 256
ROW0 = TM
PAD = ROW0 - N_META
CT = RGW
RS = 16
NCT = RGW // CT
EXP_CLAMP = 80.0
VMEM_LIMIT = 56 * 1024 * 1024

LR, B1, B2, ADAM_EPS, WD, STEP = 0.001, 0.9, 0.999, 1e-08, 0.01, 10
MESH = pl.DeviceIdType.MESH
ANY = pl.BlockSpec(memory_space=pl.ANY)


def _cparams(sem):
    return pltpu.CompilerParams(dimension_semantics=sem, vmem_limit_bytes=VMEM_LIMIT)


def _pick(n, prefs):
    for p in prefs:
        if n % p == 0:
            return p
    return n


def _sig(x):
    return 0.5 * jnp.tanh(0.5 * x) + 0.5


def _dot(a, b, dims):
    return lax.dot_general(a, b, (dims, ((), ())), preferred_element_type=F32)


NN, NT, TN_ = ((1,), (0,)), ((1,), (1,)), ((0,), (0,))


class _Host:
    def __init__(self, ins, out_shapes, aliases, sems, start, mid, finish):
        self.ins, self.out_shapes, self.aliases, self.sems = ins, out_shapes, aliases, sems
        self.start, self.mid, self.finish = start, mid, finish


class _Epi:
    def __init__(self, ins, in_specs, out_shapes, out_specs, fn, sequential=False, split=1):
        self.ins, self.in_specs, self.out_shapes, self.out_specs, self.fn = ins, in_specs, out_shapes, out_specs, fn
        self.sequential = sequential
        self.split = split


class _Pro:
    def __init__(self, ins, in_specs, fn):
        self.ins, self.in_specs, self.fn = ins, in_specs, fn


def _norm_pro(gain):
    def fn(a_ref, ins):
        xh, _ = _rms(a_ref[...])
        return (xh * ins[0][...]).astype(BF16)

    return _Pro([gain], [pl.BlockSpec((1, gain.shape[1]), lambda i, j, k: (0, 0))], fn)


def _mm(a, b, mode, out_dtype, name, resid=None, host=None, epi=None, tiles=None, sparse=None, pro=None):
    if mode == "nn":
        (m, kd), n = a.shape, b.shape[1]
    elif mode == "nt":
        (m, kd), n = a.shape, b.shape[0]
    else:
        (kd, m), n = a.shape, b.shape[1]
    if mode == "tn":
        tm = _pick(m, (1024, 1280, 1408, 640, 512, 256, 128))
        tk = _pick(kd, (1408, 768, 512, 256, 128))
    else:
        tm = _pick(m, (768, 512, 640, 256, 128))
        tk = kd if kd <= 2816 else _pick(kd, (1792, 1408, 1024, 512, 256, 128))
    tn = _pick(n, (1792, 1408, 1280, 1024, 512, 256, 128))
    if tiles is not None:
        tm, tn = tiles
    a_map = (lambda i, j, k: (k, i)) if mode == "tn" else (lambda i, j, k: (i, k))
    b_map = (lambda i, j, k: (j, k)) if mode == "nt" else (lambda i, j, k: (k, j))
    o_map = lambda i, j, k: (i, j)
    if sparse is not None:
        (ni, nj, nk), (tm, tn, tk), (a_map, b_map, o_map) = sparse
    else:
        assert m % tm == 0 and n % tn == 0 and kd % tk == 0, (name, m, n, kd, tm, tn, tk)
        ni, nj, nk = m // tm, n // tn, kd // tk
    dims = {"nn": NN, "nt": NT, "tn": TN_}[mode]

    n_hin = len(host.ins) if host else 0
    n_hout = len(host.out_shapes) if host else 0
    n_res = 0 if resid is None else 1
    n_pro = 0 if pro is None else 1
    n_pin = len(pro.ins) if pro else 0
    n_ein = len(epi.ins) if epi else 0
    n_out = len(epi.out_shapes) if epi else 1
    assert not n_pro or (mode == "nn" and nk == 1)

    def finish(r, r_ref, e_in, o_refs, i, rows=slice(None), first=True):
        if resid is not None:
            r = r + r_ref[rows, :].astype(F32)
        if epi:
            epi.fn(r, e_in, o_refs, i, rows, first)
        else:
            o_refs[0][rows, :] = r.astype(out_dtype)

    def body(*refs):
        a_ref, b_ref = refs[:2]
        r_ref = refs[2] if resid is not None else None
        pos = 2 + n_res
        p_in = refs[pos:pos + n_pin]
        pos += n_pin
        e_in = refs[pos:pos + n_ein]
        pos += n_ein
        h_in = refs[pos:pos + n_hin]
        pos += n_hin
        o_ref = refs[pos:pos + n_out]
        pos += n_out
        hn_ref = refs[pos] if n_pro else None
        pos += n_pro
        h_out = refs[pos:pos + n_hout]
        scratch = refs[pos + n_hout:]
        n_acc = 1 if nk > 1 else 0
        h_sems = scratch[n_acc + n_pro:]
        i, j, k = pl.program_id(0), pl.program_id(1), pl.program_id(2)
        if host:
            @pl.when((i == 0) & (j == 0) & (k == 0))
            def _():
                host.start(h_in, h_out, h_sems)

            if host.mid is not None:
                @pl.when((i == (3 * ni) // 4) & (j == 0) & (k == 0))
                def _():
                    host.mid(h_in, h_out, h_sems)

        if n_pro:
            a_s = scratch[n_acc]

            @pl.when(j == 0)
            def _():
                a_s[...] = pro.fn(a_ref, p_in)
                hn_ref[...] = a_s[...]

            a_val = a_s[...]
        else:
            a_val = a_ref[...]

        if nk == 1 and epi and epi.split > 1 and mode != "tn":
            part = tm // epi.split
            for p in range(epi.split):
                rows = slice(p * part, (p + 1) * part)
                finish(_dot(a_val[rows, :], b_ref[...], dims), r_ref, e_in, o_ref, i, rows, p == 0)
        elif nk == 1:
            finish(_dot(a_val, b_ref[...], dims), r_ref, e_in, o_ref, i)
        else:
            acc = scratch[0]

            @pl.when(k == 0)
            def _():
                acc[...] = jnp.zeros_like(acc)

            acc[...] += _dot(a_val, b_ref[...], dims)

            @pl.when(k == nk - 1)
            def _():
                finish(acc[...], r_ref, e_in, o_ref, i)

        if host:
            @pl.when((i == ni - 1) & (j == nj - 1) & (k == nk - 1))
            def _():
                host.finish(h_in, h_out, h_sems)

    a_spec = pl.BlockSpec((tk, tm) if mode == "tn" else (tm, tk), a_map)
    b_spec = pl.BlockSpec((tn, tk) if mode == "nt" else (tk, tn), b_map)
    o_spec = pl.BlockSpec((tm, tn), o_map)
    in_specs, args = [a_spec, b_spec], [a, b]
    if resid is not None:
        in_specs.append(o_spec)
        args.append(resid)
    if n_pro:
        in_specs += list(pro.in_specs)
        args += list(pro.ins)
    out_shapes, out_specs = [jax.ShapeDtypeStruct((m, n), out_dtype)], [o_spec]
    if epi:
        in_specs += list(epi.in_specs)
        args += list(epi.ins)
        out_shapes, out_specs = list(epi.out_shapes), list(epi.out_specs)
    if n_pro:
        out_shapes.append(jax.ShapeDtypeStruct((m, kd), BF16))
        out_specs.append(pl.BlockSpec((tm, kd), lambda i, j, k: (i, 0)))
    scratch = ([pltpu.VMEM((tm, tn), F32)] if nk > 1 else []) + ([pltpu.VMEM((tm, kd), BF16)] if n_pro else [])
    sequential = host or n_pro or (epi and epi.sequential)
    outs = pl.pallas_call(
        body, name=name, grid=(ni, nj, nk),
        in_specs=in_specs + [ANY] * n_hin, out_specs=out_specs + [ANY] * n_hout,
        out_shape=out_shapes + (list(host.out_shapes) if host else []),
        scratch_shapes=scratch + (list(host.sems) if host else []),
        input_output_aliases=({2 + n_res + n_pin + n_ein + i_in: n_out + n_pro + i_out
                               for i_in, i_out in host.aliases.items()} if host else {}),
        compiler_params=_cparams(("arbitrary",) * 3 if sequential else ("parallel", "parallel", "arbitrary")),
    )(*args, *(host.ins if host else []))
    main = list(outs[:n_out]) if epi else outs[0]
    if n_pro:
        main = (main, outs[n_out])
    return (main, list(outs[n_out + n_pro:])) if host else main


def _rms(x):
    r = lax.rsqrt(jnp.mean(x * x, axis=-1, keepdims=True) + EPS)
    return x * r, r


def _rms_bwd_math(dhn, x, g):
    xh, r = _rms(x)
    dxh = dhn * g
    dx = r * (dxh - xh * jnp.mean(dxh * xh, axis=-1, keepdims=True))
    return dx, jnp.sum(dhn * xh, axis=0, keepdims=True)


def _row_specs(tm):
    return pl.BlockSpec((tm, D), lambda i, j, k: (i, 0)), pl.BlockSpec((1, D), lambda i, j, k: (0, 0))


def _final_loss_epi(gf, target, t):
    def fn(x, ins, outs, i, rows, first):
        g_ref, t_ref = ins
        dh_ref, dhb_ref, loss_ref, dg_ref = outs

        if first:
            @pl.when(i == 0)
            def _():
                loss_ref[...] = jnp.zeros_like(loss_ref)
                dg_ref[...] = jnp.zeros_like(dg_ref)

        g = g_ref[...]
        xh, _ = _rms(x)
        err = jnp.where(i > 0, xh * g - t_ref[rows, :], 0.0)
        loss_ref[...] += 0.5 * jnp.sum(jnp.sum(err * err, axis=-1, keepdims=True) * (1.0 / D))
        dx, dg = _rms_bwd_math(err * (1.0 / D), x, g)
        dh_ref[rows, :] = dx
        dhb_ref[rows, :] = dx.astype(BF16)
        dg_ref[...] += dg

    row, vec = _row_specs(TM)
    return _Epi(ins=[gf, target], in_specs=[vec, pl.BlockSpec((TM, D), lambda i, j, k: (jnp.maximum(i - 1, 0), 0))],
                out_shapes=[jax.ShapeDtypeStruct((t, D), F32), jax.ShapeDtypeStruct((t, D), BF16),
                            jax.ShapeDtypeStruct((8, 128), F32), jax.ShapeDtypeStruct((1, D), F32)],
                out_specs=[row, row, pl.BlockSpec((8, 128), lambda i, j, k: (0, 0)), vec], fn=fn, sequential=True)


def _rms_bwd_epi(h, g, dres, tm):
    t = h.shape[0]

    def fn(dhn, ins, outs, i, rows, first):
        h_ref, g_ref, dres_ref = ins
        dh_ref, dhb_ref, dg_ref = outs

        if first:
            @pl.when(i == 0)
            def _():
                dg_ref[...] = jnp.zeros_like(dg_ref)

        dx, dg = _rms_bwd_math(dhn, h_ref[rows, :], g_ref[...])
        dh = dres_ref[rows, :] + dx
        dh_ref[rows, :] = dh
        dhb_ref[rows, :] = dh.astype(BF16)
        dg_ref[...] += dg

    row, vec = _row_specs(tm)
    return _Epi(ins=[h, g, dres], in_specs=[row, vec, row],
                out_shapes=[jax.ShapeDtypeStruct((t, D), F32), jax.ShapeDtypeStruct((t, D), BF16),
                            jax.ShapeDtypeStruct((1, D), F32)],
                out_specs=[row, row, vec], fn=fn, sequential=True)


def _rmsnorm1_bwd(dhn, h, g, dres):
    t = h.shape[0]

    def body(dhn_ref, h_ref, g_ref, dres_ref, dx_ref, dmeta_ref, dg_ref):
        i = pl.program_id(0)

        @pl.when(i == 0)
        def _():
            dg_ref[...] = jnp.zeros_like(dg_ref)

        dx, dg = _rms_bwd_math(dhn_ref[...].astype(F32), h_ref[...], g_ref[...])
        dh = dres_ref[...] + dx
        dg_ref[...] += dg
        dx_ref[...] = dh

        @pl.when(i == 0)
        def _():
            dmeta_ref[...] = dh[PAD:, :]

    row = pl.BlockSpec((TM, D), lambda i: (i, 0))
    vec = pl.BlockSpec((1, D), lambda i: (0, 0))
    return pl.pallas_call(
        body, name="rmsnorm1_bwd", grid=(t // TM,),
        in_specs=[row, row, vec, row],
        out_specs=[pl.BlockSpec((TM, D), lambda i: (jnp.maximum(i - 1, 0), 0)),
                   pl.BlockSpec((N_META, D), lambda i: (0, 0)), vec],
        out_shape=[jax.ShapeDtypeStruct((t - ROW0, D), F32), jax.ShapeDtypeStruct((N_META, D), F32),
                   jax.ShapeDtypeStruct((1, D), F32)],
        compiler_params=_cparams(("arbitrary",)),
    )(dhn, h, g, dres)


FH = DFF // 2
FM = 384


def _swiglu_epi(t):
    def fn(r, _, outs, i, rows, first):
        g, u = r[:, :FH], r[:, FH:]
        outs[0][rows, :] = r.astype(BF16)
        outs[1][rows, :] = (g * _sig(g) * u).astype(BF16)

    return _Epi(ins=[], in_specs=[],
                out_shapes=[jax.ShapeDtypeStruct((t, 2 * DFF), BF16), jax.ShapeDtypeStruct((t, DFF), BF16)],
                out_specs=[pl.BlockSpec((FM, 2 * FH), lambda i, j, k: (i, j)), pl.BlockSpec((FM, FH), lambda i, j, k: (i, j))],
                fn=fn)


def _swiglu_bwd_epi(gu):
    def fn(d, ins, outs, i, rows, first):
        gu_t = ins[0][rows, :].astype(F32)
        g, u = gu_t[:, :FH], gu_t[:, FH:]
        s = _sig(g)
        outs[0][rows, :FH] = (d * u * (s * (1.0 + g * (1.0 - s)))).astype(BF16)
        outs[0][rows, FH:] = (d * (g * s)).astype(BF16)

    spec = pl.BlockSpec((FM, 2 * FH), lambda i, j, k: (i, j))
    return _Epi(ins=[gu], in_specs=[spec], out_shapes=[jax.ShapeDtypeStruct(gu.shape, BF16)], out_specs=[spec], fn=fn)


COL_GA, COL_GB = 4, 5
COL_AX, COL_AG = (6 * D + COL_PAD) // CT, (6 * D + COL_PAD + RGW) // CT


def _merge_pro(pb, proj, tm):
    def fn(a_ref, ins):
        pa, pb_, ga, gb = (r[...].astype(F32) for r in (a_ref,) + tuple(ins))
        return (_sig(ga) * pa + _sig(gb) * pb_).astype(BF16)

    row, _ = _row_specs(tm)
    return _Pro([pb, proj, proj], [row, pl.BlockSpec((tm, D), lambda i, j, k: (i, COL_GA)),
                                   pl.BlockSpec((tm, D), lambda i, j, k: (i, COL_GB))], fn)


def _merge_bwd_epi(pa, pb, proj, tm):
    t = pa.shape[0]

    def fn(d, ins, outs, i, rows, first):
        pa, pb, ga, gb = (r[rows, :].astype(F32) for r in ins)
        dpa_ref, dpb_ref, dproj_ref = outs
        sa, sb = _sig(ga), _sig(gb)
        dpa_ref[rows, :] = (sa * d).astype(BF16)
        dpb_ref[rows, :] = (sb * d).astype(BF16)
        dproj_ref[rows, :D] = (d * pa * sa * (1.0 - sa)).astype(BF16)
        dproj_ref[rows, D:] = (d * pb * sb * (1.0 - sb)).astype(BF16)

    row, _ = _row_specs(tm)
    return _Epi(ins=[pa, pb, proj, proj],
                in_specs=[row, row, pl.BlockSpec((tm, D), lambda i, j, k: (i, COL_GA)),
                          pl.BlockSpec((tm, D), lambda i, j, k: (i, COL_GB))],
                out_shapes=[jax.ShapeDtypeStruct((t, D), BF16), jax.ShapeDtypeStruct((t, D), BF16),
                            jax.ShapeDtypeStruct((t, D_INP), BF16)],
                out_specs=[row, row, pl.BlockSpec((tm, 2 * D), lambda i, j, k: (i, 2))], fn=fn)


def _zero_pad_cols(dproj):
    t = dproj.shape[0]
    tz = _pick(t, (768, 256))

    def body(_, o_ref):
        o_ref[...] = jnp.zeros_like(o_ref)

    return pl.pallas_call(
        body, name="dproj_pad", grid=(t // tz,), in_specs=[ANY],
        out_specs=pl.BlockSpec((tz, COL_PAD), lambda i: (i, (6 * D) // COL_PAD)),
        out_shape=jax.ShapeDtypeStruct(dproj.shape, BF16), input_output_aliases={0: 0},
        compiler_params=_cparams(("parallel",)),
    )(dproj)


HALO = 16


def _rows_before(cur, before, s):
    n = cur.shape[0]
    return jnp.concatenate([before, cur], axis=0)[n - s:2 * n - s, :]


def _rows_after(cur, after, s):
    n = cur.shape[0]
    return jnp.concatenate([cur, after], axis=0)[s:n + s, :]


def _prev_halo(col0):
    return pl.BlockSpec((HALO, CT), lambda c, i: (jnp.maximum(i * (TM // HALO) - 1, 0), col0 + c))


def _conv_fwd(proj, w, b):
    t = proj.shape[0]

    def body(x_ref, halo_ref, w_ref, b_ref, ob_ref):
        i = pl.program_id(1)
        wv, bv = w_ref[...], b_ref[...]
        first = jnp.where(i > 0, halo_ref[...].astype(F32), 0.0)

        def strip(k, before):
            rows = pl.ds(pl.multiple_of(k * HALO, HALO), HALO)
            x = x_ref[rows, :].astype(F32)
            y = bv + wv[3:4, :] * x
            for j in range(3):
                y = y + wv[j:j + 1, :] * _rows_before(x, before, 3 - j)
            ob_ref[rows, :] = y.astype(BF16)
            return x

        lax.fori_loop(0, TM // HALO, strip, first)

    blk = pl.BlockSpec((TM, CT), lambda c, i: (i, c))
    return pl.pallas_call(
        body, name="conv_fwd", grid=(NCT, t // TM),
        in_specs=[pl.BlockSpec((TM, CT), lambda c, i: (i, COL_AX + c)), _prev_halo(COL_AX),
                  pl.BlockSpec((4, CT), lambda c, i: (0, c)), pl.BlockSpec((1, CT), lambda c, i: (0, c))],
        out_specs=blk,
        out_shape=jax.ShapeDtypeStruct((t, RGW), BF16),
        compiler_params=_cparams(("parallel", "parallel")),
    )(proj, proj, w, b)


def _conv_bwd(dxc, proj, w, dproj):
    t = proj.shape[0]
    nt = t // TM

    def body(d_ref, dn_ref, x_ref, halo_ref, w_ref, _, dx_ref, dw_ref, db_ref, acc):
        i = pl.program_id(1)

        @pl.when(i == 0)
        def _():
            dw_ref[...] = jnp.zeros_like(dw_ref)
            db_ref[...] = jnp.zeros_like(db_ref)

        acc[...] = jnp.zeros_like(acc)
        wv = w_ref[...]
        first = jnp.where(i > 0, halo_ref[...].astype(F32), 0.0)
        last = jnp.where(i < nt - 1, dn_ref[...].astype(F32), 0.0)
        ns = TM // HALO

        def fold(v):
            return v[0:8, :] + v[8:16, :]

        def strip(k, before):
            off = pl.multiple_of(k * HALO, HALO)
            rows = pl.ds(off, HALO)
            x, d = x_ref[rows, :].astype(F32), d_ref[rows, :].astype(F32)
            nxt = d_ref[pl.ds(pl.multiple_of(jnp.minimum(off + HALO, TM - HALO), HALO), HALO), :].astype(F32)
            after = jnp.where(k == ns - 1, last, nxt)
            dx = wv[3:4, :] * d
            for j in range(3):
                dx = dx + wv[j:j + 1, :] * _rows_after(d, after, 3 - j)
                acc[j] += fold(d * _rows_before(x, before, 3 - j))
            acc[3] += fold(d * x)
            acc[4] += fold(d)
            dx_ref[rows, :] = dx.astype(BF16)
            return x

        lax.fori_loop(0, ns, strip, first)
        for j in range(4):
            dw_ref[j:j + 1, :] += jnp.sum(acc[j], axis=0, keepdims=True)
        db_ref[...] += jnp.sum(acc[4], axis=0, keepdims=True)

    return pl.pallas_call(
        body, name="conv_bwd", grid=(NCT, nt),
        in_specs=[pl.BlockSpec((TM, CT), lambda c, i: (i, c)),
                  pl.BlockSpec((HALO, CT), lambda c, i: (jnp.minimum((i + 1) * (TM // HALO), t // HALO - 1), c)),
                  pl.BlockSpec((TM, CT), lambda c, i: (i, COL_AX + c)), _prev_halo(COL_AX),
                  pl.BlockSpec((4, CT), lambda c, i: (0, c)), ANY],
        out_specs=[pl.BlockSpec((TM, CT), lambda c, i: (i, COL_AX + c)),
                   pl.BlockSpec((4, CT), lambda c, i: (0, c)), pl.BlockSpec((1, CT), lambda c, i: (0, c))],
        out_shape=[jax.ShapeDtypeStruct((t, D_INP), BF16), jax.ShapeDtypeStruct((4, RGW), F32),
                   jax.ShapeDtypeStruct((1, RGW), F32)],
        scratch_shapes=[pltpu.VMEM((5, 8, CT), F32)],
        input_output_aliases={5: 0},
        compiler_params=_cparams(("parallel", "arbitrary")),
    )(dxc, dxc, proj, proj, w, dproj)


def _gelu(x):
    c = 0.7978845608028654
    th = jnp.tanh(c * (x + 0.044715 * x * x * x))
    return 0.5 * x * (1.0 + th), th


def _rg_gates(gr, gi, xc, ba, bx, lam, row0):
    r = _sig(gr + ba)
    ig = _sig(gi + bx)
    sp = jax.nn.softplus(-lam)
    a = jnp.exp(-RG_C * r * sp)
    s2 = jnp.maximum(1.0 - a * a, 1e-30)
    rs = lax.rsqrt(s2)
    s = s2 * rs
    rows = row0 + lax.broadcasted_iota(jnp.int32, gr.shape, 0)
    live = rows >= PAD
    u = jnp.where(live, s * ig * xc, 0.0)
    return r, ig, sp, a, s, rs, u, live


def _rg_fwd(gri, xc, proj, ba, bx, lam):
    t = gri.shape[0]

    def body(gr_ref, gi_ref, xc_ref, ag_ref, ba_ref, bx_ref, lam_ref, h_ref, ya_ref, hc):
        i = pl.program_id(1)

        @pl.when(i == 0)
        def _():
            hc[...] = jnp.zeros_like(hc)

        ba, bx, lam = ba_ref[...], bx_ref[...], lam_ref[...]
        sub = lax.broadcasted_iota(jnp.int32, (8, CT), 0)

        def strip(k, h):
            off = pl.multiple_of(k * RS, RS)
            rows = pl.ds(off, RS)
            _, _, _, a, _, _, u, _ = _rg_gates(gr_ref[rows, :], gi_ref[rows, :], xc_ref[rows, :].astype(F32), ba, bx, lam,
                                               i * TM + off)
            outs = []
            for half in range(RS // 8):
                out = jnp.zeros((8, CT), F32)
                for r in range(8):
                    h = a[8 * half + r:8 * half + r + 1, :] * h + u[8 * half + r:8 * half + r + 1, :]
                    out = jnp.where(sub == r, h, out)
                outs.append(out)
            hs = jnp.concatenate(outs, axis=0)
            h_ref[rows, :] = hs
            ge, _ = _gelu(ag_ref[rows, :].astype(F32))
            ya_ref[rows, :] = (hs * ge).astype(BF16)
            return h

        hc[...] = lax.fori_loop(0, TM // RS, strip, hc[...])

    blk_ = pl.BlockSpec((TM, CT), lambda c, i: (i, c))
    vec = pl.BlockSpec((1, CT), lambda c, i: (0, c))
    return pl.pallas_call(
        body, name="rg_fwd", grid=(NCT, t // TM),
        in_specs=[blk_, pl.BlockSpec((TM, CT), lambda c, i: (i, NCT + c)), blk_,
                  pl.BlockSpec((TM, CT), lambda c, i: (i, COL_AG + c)), vec, vec, vec],
        out_specs=[blk_, blk_],
        out_shape=[jax.ShapeDtypeStruct((t, RGW), F32), jax.ShapeDtypeStruct((t, RGW), BF16)],
        scratch_shapes=[pltpu.VMEM((1, CT), F32)],
        compiler_params=_cparams(("parallel", "arbitrary")),
    )(gri, gri, xc, proj, ba, bx, lam)


def _rg_bwd(dya, h, gri, xc, proj, ba, bx, lam, dproj):
    t = gri.shape[0]
    nt = t // TM
    assert NCT == 1

    def body(dya_ref, h_ref, hh_ref, gr_ref, gi_ref, xc_ref, ag_ref, ba_ref, bx_ref, lam_ref, _,
             dgri_ref, dxc_ref, dag_ref, dba_ref, dbx_ref, dlam_ref, hbuf, acc, cc):
        i = pl.program_id(1)
        ri = nt - 1 - i

        @pl.when(i == 0)
        def _():
            cc[...] = jnp.zeros_like(cc)
            dba_ref[...] = jnp.zeros_like(dba_ref)
            dbx_ref[...] = jnp.zeros_like(dbx_ref)
            dlam_ref[...] = jnp.zeros_like(dlam_ref)

        acc[...] = jnp.zeros_like(acc)
        ba, bx, lam = ba_ref[...], bx_ref[...], lam_ref[...]
        halo_last = jnp.where(ri > 0, hh_ref[7:8, :], 0.0)
        sub = lax.broadcasted_iota(jnp.int32, (8, CT), 0)
        c0 = 0.7978845608028654

        def strip(kk, c):
            k = TM // RS - 1 - kk
            off = pl.multiple_of(k * RS, RS)
            rows = pl.ds(off, RS)
            xc, hv = xc_ref[rows, :].astype(F32), h_ref[rows, :]
            ag, dya = ag_ref[rows, :].astype(F32), dya_ref[rows, :].astype(F32)
            r, ig, sp, a, s, rs, _, live = _rg_gates(gr_ref[rows, :], gi_ref[rows, :], xc, ba, bx, lam, ri * TM + off)
            ge, th = _gelu(ag)
            dge = 0.5 * (1.0 + th) + 0.5 * ag * (1.0 - th * th) * c0 * (1.0 + 3.0 * 0.044715 * ag * ag)
            dag_ref[rows, :] = (dya * hv * dge).astype(BF16)
            d = dya * ge
            outs = []
            for half in range(RS // 8 - 1, -1, -1):
                out = jnp.zeros((8, CT), F32)
                for rr in range(7, -1, -1):
                    g = d[8 * half + rr:8 * half + rr + 1, :] + c
                    c = a[8 * half + rr:8 * half + rr + 1, :] * g
                    out = jnp.where(sub == rr, g, out)
                outs.insert(0, out)
            g = jnp.concatenate(outs, axis=0)
            before = h_ref[pl.ds(pl.multiple_of(jnp.maximum(off - RS, 0), RS), RS), :]
            hbuf[7:8, :] = jnp.where(k == 0, halo_last, before[RS - 1:RS, :])
            hbuf[8:, :] = hv
            hprev = hbuf[pl.ds(7, RS), :]
            du = jnp.where(live, g, 0.0)
            ds = du * (ig * xc)
            dm = du * s
            dla = (g * hprev - a * ds * rs) * a
            dr = dla * (-RG_C * sp) * r * (1.0 - r)
            di = dm * xc * ig * (1.0 - ig)
            dgri_ref[rows, :CT] = dr.astype(BF16)
            dgri_ref[rows, CT:] = di.astype(BF16)
            dxc_ref[rows, :] = (dm * ig).astype(BF16)
            dl = dla * (-RG_C * r)
            for half in range(RS // 8):
                part = slice(8 * half, 8 * half + 8)
                acc[0] += dr[part]
                acc[1] += di[part]
                acc[2] += dl[part]
            return c

        cc[...] = lax.fori_loop(0, TM // RS, strip, cc[...])
        dba_ref[...] += jnp.sum(acc[0], axis=0, keepdims=True)
        dbx_ref[...] += jnp.sum(acc[1], axis=0, keepdims=True)
        dlam_ref[...] += jnp.sum(acc[2], axis=0, keepdims=True) * (-_sig(-lam))

    rblk = pl.BlockSpec((TM, CT), lambda c, i: (nt - 1 - i, c))
    vec = pl.BlockSpec((1, CT), lambda c, i: (0, c))
    hh = pl.BlockSpec((8, CT), lambda c, i: (jnp.maximum((nt - 1 - i) * (TM // 8) - 1, 0), c))
    agb = pl.BlockSpec((TM, CT), lambda c, i: (nt - 1 - i, COL_AG + c))
    return pl.pallas_call(
        body, name="rg_bwd", grid=(NCT, nt),
        in_specs=[rblk, rblk, hh, rblk, pl.BlockSpec((TM, CT), lambda c, i: (nt - 1 - i, NCT + c)), rblk, agb,
                  vec, vec, vec, ANY],
        out_specs=[pl.BlockSpec((TM, 2 * CT), lambda c, i: (nt - 1 - i, c)), rblk, agb, vec, vec, vec],
        out_shape=[jax.ShapeDtypeStruct((t, 2 * RGW), BF16),
                   jax.ShapeDtypeStruct((t, RGW), BF16), jax.ShapeDtypeStruct((t, D_INP), BF16),
                   jax.ShapeDtypeStruct((1, RGW), F32), jax.ShapeDtypeStruct((1, RGW), F32),
                   jax.ShapeDtypeStruct((1, RGW), F32)],
        scratch_shapes=[pltpu.VMEM((RS + 8, CT), F32), pltpu.VMEM((3, 8, CT), F32), pltpu.VMEM((1, CT), F32)],
        input_output_aliases={10: 2},
        compiler_params=_cparams(("parallel", "arbitrary")),
    )(dya, h, h, gri, gri, xc, proj, ba, bx, lam, dproj)


NCH = TM // CH

def _tri_dot(tri, x):
    hi = x.astype(BF16)
    r1 = x - hi.astype(F32)
    mid = r1.astype(BF16)
    lo = (r1 - mid.astype(F32)).astype(BF16)
    return _dot(tri, hi, NN) + _dot(tri, mid, NN) + _dot(tri, lo, NN)


def _hg_chunk(qr, fr, lb):
    sf = _sig(fr)
    fg = lb + (1.0 - lb) * sf
    k = (1.0 - lb) * (1.0 - sf)
    sq = _sig(qr)
    q = qr * sq
    ri = lax.broadcasted_iota(jnp.int32, (CH, CH), 0)
    ci = lax.broadcasted_iota(jnp.int32, (CH, CH), 1)
    b = _tri_dot((ri >= ci).astype(BF16), jnp.log(fg))
    bm, bl = b[CH // 2 - 1:CH // 2, :], b[CH - 1:CH, :]
    ebm = jnp.exp(jnp.minimum(b - bm, EXP_CLAMP))
    ekm = jnp.exp(jnp.minimum(bm - b, EXP_CLAMP))
    eb = ebm * jnp.exp(bm)
    ekl = ekm * jnp.exp(bl - bm)
    return dict(sf=sf, fg=fg, k=k, sq=sq, q=q, eb=eb, ebm=ebm, ekm=ekm, ekl=ekl, ebl=jnp.exp(bl),
                qe=q * eb, qh=q * ebm, kh=k * ekm, kd=k * ekl, causal=ri >= ci, upper=(ci >= ri).astype(BF16))


def _hgrn_fwd(proj, lbl, gn):
    t = proj.shape[0]
    nt = t // TM

    def body(q_ref, f_ref, v_ref, g_ref, lbl_ref, gn_ref, yb_ref, o_ref, st_ref, st):
        @pl.when(pl.program_id(0) == 0)
        def _():
            st[...] = jnp.zeros_like(st)

        l = lbl_ref[...]
        lb = _sig(l[0:1, :] - l[1:2, :])
        gnv = gn_ref[...]

        def chunk(c, carry):
            off = pl.multiple_of(c * CH, CH)
            rows = pl.ds(off, CH)
            z = _hg_chunk(q_ref[rows, :].astype(F32), f_ref[rows, :].astype(F32), lb)
            v, gg = v_ref[rows, :], g_ref[rows, :].astype(F32)
            for hh in range(HEADS):
                sl = slice(hh * HD, (hh + 1) * HD)
                s_prev = st[hh]
                st_ref[c, hh] = s_prev
                vb = v[:, sl].astype(BF16)
                att = jnp.where(z["causal"], _dot(z["qh"][:, sl].astype(BF16), z["kh"][:, sl].astype(BF16), NT), 0.0)
                o = _dot(z["qe"][:, sl].astype(BF16), s_prev.astype(BF16), NT) + _dot(att.astype(BF16), vb, NN)
                st[hh] = s_prev * z["ebl"][:, sl] + _dot(vb, z["kd"][:, sl].astype(BF16), TN_)
                xh, _ = _rms(o)
                gh = gg[:, sl]
                o_ref[rows, sl] = o
                yb_ref[rows, sl] = (xh * gnv * gh * _sig(gh)).astype(BF16)
            return carry

        lax.fori_loop(0, NCH, chunk, 0, unroll=True)

    def col(j):
        return pl.BlockSpec((TM, HGW), lambda i, j=j: (i, j))

    return pl.pallas_call(
        body, name="hgrn_fwd", grid=(nt,),
        in_specs=[col(0), col(1), col(2), col(3), pl.BlockSpec((2, HGW), lambda i: (0, 0)),
                  pl.BlockSpec((1, HD), lambda i: (0, 0))],
        out_specs=[col(0), col(0), pl.BlockSpec((NCH, HEADS, HD, HD), lambda i: (i, 0, 0, 0))],
        out_shape=[jax.ShapeDtypeStruct((t, HGW), BF16), jax.ShapeDtypeStruct((t, HGW), F32),
                   jax.ShapeDtypeStruct((t // CH, HEADS, HD, HD), F32)],
        scratch_shapes=[pltpu.VMEM((HEADS, HD, HD), F32)],
        compiler_params=_cparams(("arbitrary",)),
    )(proj, proj, proj, proj, lbl, gn)


def _hgrn_bwd(dyb, proj, o, states, lbl, gn, dproj):
    t = proj.shape[0]
    nt = t // TM

    def body(dy_ref, q_ref, f_ref, v_ref, g_ref, o_ref, st_ref, lbl_ref, gn_ref, _,
             dp_ref, dgn_ref, dl_ref, dst, dlb):
        i = pl.program_id(0)

        @pl.when(i == 0)
        def _():
            dst[...] = jnp.zeros_like(dst)
            dlb[...] = jnp.zeros_like(dlb)
            dgn_ref[...] = jnp.zeros_like(dgn_ref)

        l = lbl_ref[...]
        lb = _sig(l[0:1, :] - l[1:2, :])
        gnv = gn_ref[...]
        last = lax.broadcasted_iota(jnp.int32, (CH, HD), 0) == CH - 1

        def chunk(cc, carry):
            c = NCH - 1 - cc
            off = pl.multiple_of(c * CH, CH)
            rows = pl.ds(off, CH)
            qr, fr = q_ref[rows, :].astype(F32), f_ref[rows, :].astype(F32)
            z = _hg_chunk(qr, fr, lb)
            v, gg, ov, dy = v_ref[rows, :], g_ref[rows, :].astype(F32), o_ref[rows, :], dy_ref[rows, :].astype(F32)
            dqs, dks, dbs, dvs, dgs = [], [], [], [], []
            dgn = jnp.zeros((1, HD), F32)
            for hh in range(HEADS):
                sl = slice(hh * HD, (hh + 1) * HD)
                s_prev, ds_new = st_ref[c, hh], dst[hh]
                qe, qh, kh, kd = z["qe"][:, sl], z["qh"][:, sl], z["kh"][:, sl], z["kd"][:, sl]
                ebl = z["ebl"][:, sl]
                gh, dyh = gg[:, sl], dy[:, sl]
                xh, rr = _rms(ov[:, sl])
                sg = _sig(gh)
                dyn = dyh * (gh * sg)
                dgs.append(dyh * (xh * gnv) * (sg * (1.0 + gh * (1.0 - sg))))
                dgn = dgn + jnp.sum(dyn * xh, axis=0, keepdims=True)
                dxh = dyn * gnv
                do = (rr * (dxh - xh * jnp.mean(dxh * xh, axis=-1, keepdims=True))).astype(BF16)
                vb, dsb = v[:, sl].astype(BF16), ds_new.astype(BF16)
                qeb, qhb, khb, kdb = (a.astype(BF16) for a in (qe, qh, kh, kd))
                att = jnp.where(z["causal"], _dot(qhb, khb, NT), 0.0).astype(BF16)
                datt = jnp.where(z["causal"], _dot(do, vb, NT), 0.0).astype(BF16)
                dvs.append(_dot(att, do, TN_) + _dot(kdb, dsb, NT))
                dqe = _dot(do, s_prev.astype(BF16), NN)
                dqh = _dot(datt, khb, NN)
                dkh = _dot(datt, qhb, TN_)
                dkd = _dot(vb, dsb, NN)
                qe, qh, kh, kd = (a.astype(F32) for a in (qeb, qhb, khb, kdb))
                dbl = (jnp.sum(dkd * kd, axis=0, keepdims=True)
                       + jnp.sum(ds_new * s_prev, axis=0, keepdims=True) * ebl)
                dqs.append(dqe * z["eb"][:, sl] + dqh * z["ebm"][:, sl])
                dks.append(dkh * z["ekm"][:, sl] + dkd * z["ekl"][:, sl])
                dbs.append(dqe * qe + dqh * qh - dkh * kh - dkd * kd + jnp.where(last, dbl, 0.0))
                dst[hh] = _dot(do, qeb, TN_) + ds_new * ebl
            dgn_ref[...] += dgn
            dq, dk, db = (jnp.concatenate(x, axis=1) for x in (dqs, dks, dbs))
            dlf = _tri_dot(z["upper"], db)
            sf, fg, sq = z["sf"], z["fg"], z["sq"]
            dmix = dlf / fg - dk
            dsf = dmix * (1.0 - lb)
            dlb[...] += jnp.sum(dmix * (1.0 - sf), axis=0, keepdims=True)
            dp_ref[rows, 0:HGW] = (dq * (sq * (1.0 + qr * (1.0 - sq)))).astype(BF16)
            dp_ref[rows, HGW:2 * HGW] = (dsf * sf * (1.0 - sf)).astype(BF16)
            dp_ref[rows, 2 * HGW:3 * HGW] = jnp.concatenate(dvs, axis=1).astype(BF16)
            dp_ref[rows, 3 * HGW:4 * HGW] = jnp.concatenate(dgs, axis=1).astype(BF16)
            return carry

        lax.fori_loop(0, NCH, chunk, 0, unroll=True)
        dl0 = dlb[...] * lb * (1.0 - lb)
        dl_ref[0:1, :] = dl0
        dl_ref[1:2, :] = -dl0

    def col(j):
        return pl.BlockSpec((TM, HGW), lambda i, j=j: (nt - 1 - i, j))

    return pl.pallas_call(
        body, name="hgrn_bwd", grid=(nt,),
        in_specs=[col(0), col(0), col(1), col(2), col(3), col(0),
                  pl.BlockSpec((NCH, HEADS, HD, HD), lambda i: (nt - 1 - i, 0, 0, 0)),
                  pl.BlockSpec((2, HGW), lambda i: (0, 0)), pl.BlockSpec((1, HD), lambda i: (0, 0)), ANY],
        out_specs=[pl.BlockSpec((TM, 4 * HGW), lambda i: (nt - 1 - i, 0)),
                   pl.BlockSpec((1, HD), lambda i: (0, 0)), pl.BlockSpec((2, HGW), lambda i: (0, 0))],
        out_shape=[jax.ShapeDtypeStruct((t, D_INP), BF16), jax.ShapeDtypeStruct((1, HD), F32),
                   jax.ShapeDtypeStruct((2, HGW), F32)],
        scratch_shapes=[pltpu.VMEM((HEADS, HD, HD), F32), pltpu.VMEM((1, HGW), F32)],
        input_output_aliases={9: 0},
        compiler_params=_cparams(("arbitrary",)),
    )(dyb, proj, proj, proj, proj, o, states, lbl, gn, dproj)


def _rotate_w_in(w):
    return jnp.concatenate([w[:, ROT:], jnp.zeros((w.shape[0], COL_PAD), w.dtype), w[:, :ROT]], axis=1)


def _unrotate_g_in(g):
    return jnp.concatenate([g[:, D_INP - ROT:], g[:, :D_IN - ROT]], axis=1)


SB = RGW // 2


def _bd_gates(t):
    tm = _pick(t, (768, 256))

    def col(j):
        return 2 * (j % 2) + j // 2

    return ((t // tm, 4, 1), (tm, SB, SB),
            (lambda i, j, k: (i, j // 2), lambda i, j, k: (j // 2, col(j)), lambda i, j, k: (i, col(j))))


def _bd_dw(t):
    tk = _pick(t, (1408, 768))
    return (1, 4, t // tk), (SB, SB, tk), (lambda i, j, k: (k, j % 2), lambda i, j, k: (k, j), lambda i, j, k: (j % 2, j))


def _block_diag(w):
    eye = jnp.eye(RG_BLOCKS, dtype=w.dtype)
    return (w[:, :, None, :] * eye[:, None, :, None]).reshape(RGW, RGW)


def _diag_blocks(wd):
    w4 = wd.reshape(RG_BLOCKS, RG_BLOCK, RG_BLOCKS, RG_BLOCK)
    return jnp.stack([w4[n, :, n, :] for n in range(RG_BLOCKS)])


def _local_step(h0, target, wts, small, hosts=None, finalize=None):
    hosts = hosts or {}
    carried = {}
    g = {}
    t = h0.shape[0]

    def mm(a, b, mode, out_dtype, name, **kw):
        if name not in hosts:
            return _mm(a, b, mode, out_dtype, name, **kw)
        host = hosts[name](g)
        out, res = _mm(a, b, mode, out_dtype, name, host=host, **kw)
        carried[name] = (host.ins, res)
        return out

    w_in = wts["w_in"]
    wax_d = jnp.concatenate([_block_diag(small["rg_wa"]), _block_diag(small["rg_wx"])], axis=1).astype(BF16)
    ba, bx, lam = small["rg_ba"], small["rg_bx"], small["rg_lambda"]
    lbl, gn = small["hg_lb_logits"], small["hg_norm_g"]
    conv_w, conv_b = small["conv_w"], small["conv_b"]

    tmm = _pick(t, (768, 256))
    proj, hn1 = mm(h0, w_in, "nn", BF16, "mm_proj", pro=_norm_pro(small["norm1_g"]))
    if finalize is not None:
        wts = finalize(carried["mm_proj"][1])
    w_pa, w_pb, w_out, w_fi, w_fd = (wts[k] for k in ("w_proj_a", "w_proj_b", "w_out", "w_ffn_in", "w_ffn_down"))
    xcb = _conv_fwd(proj, conv_w, conv_b)
    gri = mm(xcb, wax_d, "nn", F32, "mm_rg_gates", sparse=_bd_gates(t))
    hrg, ya = _rg_fwd(gri, xcb, proj, ba, bx, lam)
    yb, o, states = _hgrn_fwd(proj, lbl, gn)
    pa = mm(ya, w_pa, "nn", BF16, "mm_pa")
    pb = mm(yb, w_pb, "nn", BF16, "mm_pb")
    h1, merged = mm(pa, w_out, "nn", F32, "mm_out", resid=h0, pro=_merge_pro(pb, proj, tmm), tiles=(tmm, D))
    (gu, act), hn2 = mm(h1, w_fi, "nn", BF16, "mm_ffn_in", pro=_norm_pro(small["norm2_g"]), epi=_swiglu_epi(t),
                        tiles=(FM, 2 * FH))
    dh2, dh2b, loss, g["norm_f_g"] = mm(act, w_fd, "nn", F32, "mm_ffn_down", resid=h1,
                                        epi=_final_loss_epi(small["norm_f_g"], target, t), tiles=(TM, D))

    g["w_ffn_down"] = mm(act, dh2b, "tn", BF16, "mm_d_wfd")
    (dgu,) = mm(dh2b, w_fd, "nt", BF16, "mm_d_act", epi=_swiglu_bwd_epi(gu), tiles=(FM, FH))
    g["w_ffn_in"] = mm(hn2, dgu, "tn", BF16, "mm_d_wfi")
    dh1, dh1b, g["norm2_g"] = mm(dgu, w_fi, "nt", BF16, "mm_d_hn2", epi=_rms_bwd_epi(h1, small["norm2_g"], dh2, tmm),
                                 tiles=(tmm, D))
    g["w_out"] = mm(merged, dh1b, "tn", BF16, "mm_d_wout")
    dpa, dpb, dproj = mm(dh1b, w_out, "nt", BF16, "mm_d_merged", epi=_merge_bwd_epi(pa, pb, proj, TM), tiles=(TM, D))
    dproj = _zero_pad_cols(dproj)
    g["w_proj_a"] = mm(ya, dpa, "tn", BF16, "mm_d_wpa")
    g["w_proj_b"] = mm(yb, dpb, "tn", BF16, "mm_d_wpb")
    dya = mm(dpa, w_pa, "nt", BF16, "mm_d_ya")
    dyb = mm(dpb, w_pb, "nt", BF16, "mm_d_yb")
    dproj, g["hg_norm_g"], g["hg_lb_logits"] = _hgrn_bwd(dyb, proj, o, states, lbl, gn, dproj)
    dgri, dxc, dproj, g["rg_ba"], g["rg_bx"], g["rg_lambda"] = _rg_bwd(dya, hrg, gri, xcb, proj, ba, bx, lam, dproj)
    dwax = mm(xcb, dgri, "tn", F32, "mm_d_wax", sparse=_bd_dw(t))
    g["rg_wa"], g["rg_wx"] = _diag_blocks(dwax[:, :RGW]), _diag_blocks(dwax[:, RGW:])
    dxc = mm(dgri, wax_d, "nt", BF16, "mm_d_xc", resid=dxc)
    dproj, g["conv_w"], g["conv_b"] = _conv_bwd(dxc, proj, conv_w, dproj)
    g["w_in"] = mm(hn1, dproj, "tn", BF16, "mm_d_win")
    dhn1 = mm(dproj, w_in, "nt", BF16, "mm_d_hn1")
    grad_x, g["meta_tokens"], g["norm1_g"] = _rmsnorm1_bwd(dhn1, h0, small["norm1_g"], dh1)
    return loss, grad_x, g, carried


def _adamw_math(w, g, m, v):
    m = B1 * m + (1.0 - B1) * g
    v = B2 * v + (1.0 - B2) * (g * g)
    m_hat = m / (1.0 - B1 ** STEP)
    v_hat = v / (1.0 - B2 ** STEP)
    return -LR * (m_hat / (jnp.sqrt(v_hat) + ADAM_EPS) + WD * w), m, v


def _adamw(w, g, m, v, name):
    r, c = w.shape
    tr = _pick(r, (256, 352, 320, 128, 64, 32, 16, 8))

    def body(w_ref, g_ref, m_ref, v_ref, d_ref, mo_ref, vo_ref):
        d_ref[...], mo_ref[...], vo_ref[...] = _adamw_math(w_ref[...], g_ref[...], m_ref[...], v_ref[...])

    blk = pl.BlockSpec((tr, c), lambda i: (i, 0))
    return pl.pallas_call(
        body, name=name, grid=(r // tr,), in_specs=[blk] * 4, out_specs=[blk] * 3,
        out_shape=[jax.ShapeDtypeStruct((r, c), F32)] * 3,
        compiler_params=_cparams(("parallel",)),
    )(w, g, m, v)


def _adamw_rows(recv, params, extras, name):
    _, _, width = recv.shape
    n_par = len(params)

    def body(*refs):
        r_ref = refs[0]
        ins = refs[1:1 + 3 * n_par]
        outs = refs[1 + 3 * n_par:]
        g_all = r_ref[0]
        for p in range(1, 8):
            g_all = g_all + r_ref[p]
        for q, (row0, w, _, _) in enumerate(params):
            g = g_all[row0:row0 + w.shape[0], :]
            w_ref, m_ref, v_ref = ins[3 * q:3 * q + 3]
            g_ref, d_ref, mo_ref, vo_ref = outs[4 * q:4 * q + 4]
            g_ref[...] = g
            d_ref[...], mo_ref[...], vo_ref[...] = _adamw_math(w_ref[...], g, m_ref[...], v_ref[...])
        for q, (row0, n) in enumerate(extras):
            outs[4 * n_par + q][...] = g_all[row0:row0 + n, :]

    shapes = [jax.ShapeDtypeStruct(w.shape, F32) for _, w, _, _ in params for _ in range(4)]
    shapes += [jax.ShapeDtypeStruct((n, width), F32) for _, n in extras]
    vm = pl.BlockSpec(memory_space=pltpu.VMEM)
    outs = pl.pallas_call(
        body, name=name, in_specs=[vm] * (1 + 3 * n_par), out_specs=[vm] * len(shapes), out_shape=shapes,
        compiler_params=pltpu.CompilerParams(vmem_limit_bytes=VMEM_LIMIT),
    )(recv, *[a for _, w, m, v in params for a in (w, m, v)])
    return [tuple(outs[4 * q:4 * q + 4]) for q in range(n_par)], list(outs[4 * n_par:])


def _col_block(weight, px, py):
    return 2 * py + px if weight == "w_ffn_in" else 2 * px + py


def _sum_place(recv, gfull, weight, xyc, name):
    _, r, c = recv.shape
    tr = _pick(r, (128, 176, 160, 64, 32, 16))
    nb = r // tr

    def body(_, own_ref, r_ref, o_ref):
        g = own_ref[...].astype(F32)
        for p in range(7):
            g = g + r_ref[p].astype(F32)
        o_ref[...] = g

    if weight in COL_SHARDED:
        own = pl.BlockSpec((tr, c), lambda i, s: (s[2] * nb + i, _col_block(weight, s[0], s[1])))
    else:
        own = pl.BlockSpec((tr, c), lambda i, s: ((4 * s[0] + 2 * s[1] + s[2]) * nb + i, 0))
    return pl.pallas_call(
        body, name=name,
        grid_spec=pltpu.PrefetchScalarGridSpec(
            num_scalar_prefetch=1, grid=(nb,),
            in_specs=[own, pl.BlockSpec((7, tr, c), lambda i, s: (0, i, 0))],
            out_specs=pl.BlockSpec((tr, c), lambda i, s: (s[2] * nb + i, 0))),
        out_shape=jax.ShapeDtypeStruct((2 * r, c), F32),
        compiler_params=_cparams(("arbitrary",)),
    )(xyc, gfull, recv)


def _place_own(shard, slot, dtype, name):
    r, c = shard.shape
    tr = _pick(r, (256, 352, 320, 128, 32))

    def body(_, x_ref, o_ref):
        o_ref[...] = x_ref[...].astype(dtype)

    return pl.pallas_call(
        body, name=name,
        grid_spec=pltpu.PrefetchScalarGridSpec(
            num_scalar_prefetch=1, grid=(r // tr,),
            in_specs=[pl.BlockSpec((tr, c), lambda i, s: (i, 0))],
            out_specs=pl.BlockSpec((None, tr, c), lambda i, s: (s[0], i, 0))),
        out_shape=jax.ShapeDtypeStruct((4, r, c), dtype),
        compiler_params=_cparams(("arbitrary",)),
    )(slot, shard)


def _place_cols(shard, slot, name):
    r, c = shard.shape
    tr = _pick(r, (256, 128))

    def body(_, x_ref, o_ref):
        o_ref[...] = x_ref[...].astype(BF16)

    return pl.pallas_call(
        body, name=name,
        grid_spec=pltpu.PrefetchScalarGridSpec(
            num_scalar_prefetch=1, grid=(r // tr,),
            in_specs=[pl.BlockSpec((tr, c), lambda i, s: (i, 0))],
            out_specs=pl.BlockSpec((tr, c), lambda i, s: (i, 2 * (s[0] % 2) + s[0] // 2))),
        out_shape=jax.ShapeDtypeStruct((r, 4 * c), BF16),
        compiler_params=_cparams(("arbitrary",)),
    )(slot, shard)


def _place_w_in(shard, slot):
    r, c = shard.shape
    nb, npad = c // 128, COL_PAD // 128
    rot_blocks, in_blocks = ROT // 128, D_IN // 128

    def out_block(j, s):
        nat = s[0] * nb + jnp.minimum(j, nb - 1)
        rot = jnp.where(nat >= rot_blocks, nat - rot_blocks, nat + (D_INP - ROT) // 128)
        return jnp.where(j < nb, rot, (in_blocks - rot_blocks) + (j - nb))

    def body(_, x_ref, o_ref):
        o_ref[...] = jnp.where(pl.program_id(0) < nb, x_ref[...], 0.0).astype(BF16)

    return pl.pallas_call(
        body, name="place_w_in",
        grid_spec=pltpu.PrefetchScalarGridSpec(
            num_scalar_prefetch=1, grid=(nb + npad,),
            in_specs=[pl.BlockSpec((r, 128), lambda j, s: (0, jnp.minimum(j, nb - 1)))],
            out_specs=pl.BlockSpec((r, 128), lambda j, s: (0, out_block(j, s)))),
        out_shape=jax.ShapeDtypeStruct((r, D_INP), BF16),
        compiler_params=_cparams(("arbitrary",)),
    )(slot, shard)


BIG = ("w_in", "w_proj_a", "w_proj_b", "w_out", "w_ffn_in", "w_ffn_down")
COL_SHARDED = ("w_in", "w_ffn_in")


def _coords():
    return lax.axis_index("x"), lax.axis_index("y"), lax.axis_index("c")


W_IN_HEAD = 384


def _shard_views(b, layout, chip, half_of):
    if layout == "slots":
        half = b.shape[1] // 2
        return [b.at[chip, pl.ds(pl.multiple_of(half_of * half, 8), half)]]
    half = b.shape[0] // 2
    rows = pl.ds(pl.multiple_of(half_of * half, 8), half)
    if layout == "cols":
        c = b.shape[1] // 4
        return [b.at[rows, pl.ds(pl.multiple_of((2 * (chip % 2) + chip // 2) * c, 128), c)]]
    c = D_IN // 4
    views = []
    for first, width in ((0, W_IN_HEAD), (W_IN_HEAD, c - W_IN_HEAD)):
        nat = chip * c + first
        rot = jnp.where(nat >= ROT, nat - ROT, nat + D_INP - ROT)
        views.append(b.at[rows, pl.ds(pl.multiple_of(rot, 128), width)])
    return views


def _gather_host(bufs, layouts):
    n = len(bufs)

    def place():
        x, y, c = _coords()
        return x, y, c, [(1 - x, y), (x, 1 - y), (1 - x, 1 - y)]

    def copies(b, sems, w, k, chip, half_of, to):
        return [pltpu.make_async_remote_copy(src_ref=v, dst_ref=v, send_sem=sems[0].at[w, k, p], recv_sem=sems[1].at[w, k, p],
                                             device_id=to, device_id_type=MESH)
                for p, v in enumerate(_shard_views(b, layouts[w], chip, half_of))]

    def start(_, outs, sems):
        x, y, c, chips = place()
        for w, b in enumerate(outs):
            for k, (cx, cy) in enumerate(chips):
                for cp in copies(b, sems, w, k, 2 * x + y, c, (cx, cy, c)):
                    cp.start()

    def mid(_, outs, sems):
        x, y, c, chips = place()
        for w, b in enumerate(outs):
            for k, (cx, cy) in enumerate(chips):
                for cp in copies(b, sems, w, k, 2 * cx + cy, c, (cx, cy, c)):
                    cp.wait_recv()
                for cp in copies(b, sems, w, 3 + k, 2 * cx + cy, c, (x, y, 1 - c)):
                    cp.start()

    def finish(_, outs, sems):
        x, y, c, chips = place()
        for w, b in enumerate(outs):
            for k, (cx, cy) in enumerate(chips):
                for cp in copies(b, sems, w, 3 + k, 2 * cx + cy, 1 - c, (x, y, 1 - c)):
                    cp.wait_recv()
        for w, b in enumerate(outs):
            for k, (cx, cy) in enumerate(chips):
                for cp in copies(b, sems, w, k, 2 * x + y, c, (cx, cy, c)) + copies(b, sems, w, 3 + k, 2 * cx + cy, c, (x, y, 1 - c)):
                    cp.wait_send()

    return _Host(ins=list(bufs), out_shapes=[jax.ShapeDtypeStruct(b.shape, b.dtype) for b in bufs],
                 aliases={i: i for i in range(n)},
                 sems=[pltpu.SemaphoreType.DMA((n, 6, 2)), pltpu.SemaphoreType.DMA((n, 6, 2))],
                 start=start, mid=mid, finish=finish)


def _run_host(host, name):
    n_in, n_out = len(host.ins), len(host.out_shapes)

    def body(*refs):
        ins, outs, sems = refs[:n_in], refs[n_in:n_in + n_out], refs[n_in + n_out:]
        host.start(ins, outs, sems)
        if host.mid is not None:
            host.mid(ins, outs, sems)
        host.finish(ins, outs, sems)

    return pl.pallas_call(
        body, name=name, in_specs=[ANY] * n_in, out_specs=[ANY] * n_out, out_shape=list(host.out_shapes),
        scratch_shapes=list(host.sems), input_output_aliases=dict(host.aliases),
    )(*host.ins)


def _peer(x, y, c, k):
    fx, fy, fc = (k >> 2) & 1, (k >> 1) & 1, k & 1
    return (x ^ fx if fx else x, y ^ fy if fy else y, c ^ fc if fc else c)


def _sub_shape(name, full_shape):
    r, c = full_shape
    return (r // 2, c // 4) if name in COL_SHARDED else (r // 8, c)


def _exchange_host(names, grads):
    n = len(names)
    shapes = [_sub_shape(k, g.shape) for k, g in zip(names, grads)]

    def copy(ins, outs, sems, w, k):
        x, y, c = _coords()
        px, py, pc = _peer(x, y, c, k)
        sr, sc = shapes[w]
        if names[w] in COL_SHARDED:
            col = _col_block(names[w], px, py) * sc
            src = ins[w].at[pl.ds(pl.multiple_of(pc * sr, 16), sr), pl.ds(pl.multiple_of(col, 128), sc)]
        else:
            src = ins[w].at[pl.ds(pl.multiple_of((4 * px + 2 * py + pc) * sr, 16), sr)]
        return pltpu.make_async_remote_copy(
            src_ref=src, dst_ref=outs[w].at[k - 1], send_sem=sems[0].at[w, k - 1], recv_sem=sems[1].at[w, k - 1],
            device_id=(px, py, pc), device_id_type=MESH)

    def start(ins, outs, sems):
        for w in range(n):
            for k in range(1, 8):
                copy(ins, outs, sems, w, k).start()

    def finish(ins, outs, sems):
        for w in range(n):
            for k in range(1, 8):
                copy(ins, outs, sems, w, k).wait_recv()
        for w in range(n):
            for k in range(1, 8):
                copy(ins, outs, sems, w, k).wait_send()

    return _Host(ins=list(grads), out_shapes=[jax.ShapeDtypeStruct((7,) + s, g.dtype) for s, g in zip(shapes, grads)],
                 aliases={}, sems=[pltpu.SemaphoreType.DMA((n, 7)), pltpu.SemaphoreType.DMA((n, 7))],
                 start=start, mid=None, finish=finish)


def _sibling_host(bufs):
    n = len(bufs)

    def copy(outs, sems, w, half_of):
        x, y, c = _coords()
        half = outs[w].shape[0] // 2
        rows = outs[w].at[pl.ds(pl.multiple_of(half_of * half, 8), half)]
        return pltpu.make_async_remote_copy(src_ref=rows, dst_ref=rows, send_sem=sems[0].at[w], recv_sem=sems[1].at[w],
                                            device_id=(x, y, 1 - c), device_id_type=MESH)

    def start(_, outs, sems):
        c = lax.axis_index("c")
        for w in range(n):
            copy(outs, sems, w, c).start()

    def finish(_, outs, sems):
        c = lax.axis_index("c")
        for w in range(n):
            copy(outs, sems, w, 1 - c).wait_recv()
        for w in range(n):
            copy(outs, sems, w, c).wait_send()

    return _Host(ins=list(bufs), out_shapes=[jax.ShapeDtypeStruct(b.shape, b.dtype) for b in bufs],
                 aliases={i: i for i in range(n)},
                 sems=[pltpu.SemaphoreType.DMA((n,)), pltpu.SemaphoreType.DMA((n,))], start=start, mid=None, finish=finish)


def _pack_host(pack):
    def me_of():
        x, y, c = _coords()
        return x, y, c, 4 * x + 2 * y + c

    def copy(ins, outs, sems, k, slot):
        x, y, c, _ = me_of()
        return pltpu.make_async_remote_copy(src_ref=ins[0], dst_ref=outs[0].at[slot], send_sem=sems[0].at[k - 1],
                                            recv_sem=sems[1].at[k - 1], device_id=_peer(x, y, c, k), device_id_type=MESH)

    def start(ins, outs, sems):
        me = me_of()[3]
        pltpu.make_async_copy(ins[0], outs[0].at[me], sems[2]).start()
        for k in range(1, 8):
            copy(ins, outs, sems, k, me).start()

    def finish(ins, outs, sems):
        x, y, c, me = me_of()
        for k in range(1, 8):
            px, py, pc = _peer(x, y, c, k)
            copy(ins, outs, sems, k, 4 * px + 2 * py + pc).wait_recv()
        for k in range(1, 8):
            copy(ins, outs, sems, k, me).wait_send()
        pltpu.make_async_copy(ins[0], outs[0].at[me], sems[2]).wait()

    return _Host(ins=[pack], out_shapes=[jax.ShapeDtypeStruct((8,) + pack.shape, F32)], aliases={},
                 sems=[pltpu.SemaphoreType.DMA((7,)), pltpu.SemaphoreType.DMA((7,)), pltpu.SemaphoreType.DMA],
                 start=start, mid=None, finish=finish)


def _join_hosts(hosts):
    ins, outs, sems, aliases, spans = [], [], [], {}, []
    for h in hosts:
        spans.append((len(ins), len(h.ins), len(outs), len(h.out_shapes), len(sems), len(h.sems)))
        for i_in, i_out in h.aliases.items():
            aliases[len(ins) + i_in] = len(outs) + i_out
        ins, outs, sems = ins + list(h.ins), outs + list(h.out_shapes), sems + list(h.sems)

    def phase(which):
        fns = [getattr(h, which) for h in hosts]
        if all(f is None for f in fns):
            return None

        def run(i_refs, o_refs, s_refs):
            for f, (i0, n_i, o0, n_o, s0, n_s) in zip(fns, spans):
                if f is not None:
                    f(i_refs[i0:i0 + n_i], o_refs[o0:o0 + n_o], s_refs[s0:s0 + n_s])

        return run

    return _Host(ins, outs, aliases, sems, phase("start"), phase("mid"), phase("finish"))


SHARDED_SMALL = ("meta_tokens", "conv_w")

ROWS = {"conv_b": (1, RGW), "rg_ba": (1, RGW), "rg_bx": (1, RGW), "rg_lambda": (1, RGW), "conv_w": (4, RGW),
        "norm2_g": (1, D), "norm_f_g": (1, D), "hg_lb_logits": (2, D),
        "rg_wa": (RGW * RG_BLOCK // 128, 128), "rg_wx": (RGW * RG_BLOCK // 128, 128), "hg_norm_g": (1, 128),
        "norm1_g": (1, D), "meta_tokens": (N_META, D), "loss": (1, D)}
PACKS = {"early_wide": ("conv_b", "rg_ba", "rg_bx", "rg_lambda", "conv_w"),
         "early_model": ("norm2_g", "norm_f_g", "hg_lb_logits"),
         "early_lane": ("rg_wa", "rg_wx", "hg_norm_g"),
         "late": ("norm1_g", "meta_tokens", "loss")}
EARLY = ("early_wide", "early_model", "early_lane")
NO_UPDATE = ("conv_w", "meta_tokens", "loss")


def _pack_rows(vals, names):
    parts, first, row = [], {}, 0
    for k in names:
        first[k] = row
        parts.append(vals[k].reshape(ROWS[k]).astype(F32))
        row += ROWS[k][0]
    if row % 8:
        parts.append(jnp.zeros((-row % 8, ROWS[names[0]][1]), F32))
    return jnp.concatenate(parts, axis=0), first


ORDER = ("meta_tokens", "norm1_g", "w_in", "conv_w", "conv_b", "rg_wa", "rg_ba", "rg_wx", "rg_bx", "rg_lambda",
         "hg_lb_logits", "hg_norm_g", "w_proj_a", "w_proj_b", "w_out", "norm2_g", "w_ffn_in", "w_ffn_down", "norm_f_g")


def kernel(x, meta_tokens, norm1_g, w_in, conv_w, conv_b, rg_wa, rg_ba, rg_wx, rg_bx, rg_lambda, hg_lb_logits, hg_norm_g, w_proj_a, w_proj_b, w_out, norm2_g, w_ffn_in, w_ffn_down, norm_f_g, loss_target, m_meta_tokens, m_norm1_g, m_w_in, m_conv_w, m_conv_b, m_rg_wa, m_rg_ba, m_rg_wx, m_rg_bx, m_rg_lambda, m_hg_lb_logits, m_hg_norm_g, m_w_proj_a, m_w_proj_b, m_w_out, m_norm2_g, m_w_ffn_in, m_w_ffn_down, m_norm_f_g, v_meta_tokens, v_norm1_g, v_w_in, v_conv_w, v_conv_b, v_rg_wa, v_rg_ba, v_rg_wx, v_rg_bx, v_rg_lambda, v_hg_lb_logits, v_hg_norm_g, v_w_proj_a, v_w_proj_b, v_w_out, v_norm2_g, v_w_ffn_in, v_w_ffn_down, v_norm_f_g):
    args = dict(locals())
    w = {k: args[k] for k in ORDER}
    m = {k: args["m_" + k] for k in ORDER}
    v = {k: args["v_" + k] for k in ORDER}
    xi, yi, ci = _coords()
    chip = 2 * xi + yi
    slot = jnp.reshape(chip, (1,)).astype(jnp.int32)
    xyc = jnp.stack([xi, yi, ci]).astype(jnp.int32)

    tiny = jnp.zeros((32, 384), F32)
    tiny = tiny.at[0:N_META, 0:256].set(meta_tokens).at[N_META:N_META + 4, 0:320].set(conv_w[0])
    later = [k for k in BIG if k != "w_in"]
    layouts = {k: "cols" if k in COL_SHARDED else "slots" for k in later}
    slots = {k: _place_cols(w[k][0], slot, "place_" + k) if layouts[k] == "cols"
             else _place_own(w[k][0], slot, BF16, "place_" + k) for k in later}
    w_in_full, tiny_all = _run_host(
        _gather_host([_place_w_in(w["w_in"][0], slot), _place_own(tiny, slot, F32, "place_tiny")], ["w_in", "slots"]),
        "allgather_w_in")

    def full_matrix(k, gth):
        return gth if layouts[k] == "cols" else gth.reshape(-1, gth.shape[2])

    meta_full = jnp.transpose(tiny_all[:, 0:N_META, 0:256], (1, 0, 2)).reshape(N_META, D)
    conv_w_full = jnp.transpose(tiny_all[:, N_META:N_META + 4, 0:320], (1, 0, 2)).reshape(4, RGW)

    small = dict(norm1_g=norm1_g, conv_w=conv_w_full, conv_b=conv_b, rg_wa=rg_wa[0], rg_ba=rg_ba, rg_wx=rg_wx[0],
                 rg_bx=rg_bx, rg_lambda=rg_lambda, hg_lb_logits=hg_lb_logits, hg_norm_g=hg_norm_g,
                 norm2_g=norm2_g, norm_f_g=norm_f_g.reshape(1, D))

    hosts = {
        "mm_proj": lambda g: _gather_host([slots[k] for k in later], [layouts[k] for k in later]),
        "mm_d_act": lambda g: _exchange_host(["w_ffn_down"], [g["w_ffn_down"]]),
        "mm_d_hn2": lambda g: _exchange_host(["w_ffn_in"], [g["w_ffn_in"]]),
        "mm_d_win": lambda g: _join_hosts([
            _exchange_host(["w_out", "w_proj_a", "w_proj_b"], [g["w_out"], g["w_proj_a"], g["w_proj_b"]]),
            *[_pack_host(_pack_rows(g, PACKS[p])[0]) for p in EARLY]]),
        "mm_d_hn1": lambda g: _exchange_host(["w_in"], [_unrotate_g_in(g["w_in"])]),
    }
    h0 = jnp.concatenate([jnp.zeros((PAD, D), F32), meta_full, x[0]], axis=0)
    loss_blk, grad_x, g, carried = _local_step(
        h0, loss_target[0], {"w_in": w_in_full}, small, hosts=hosts,
        finalize=lambda gathered: {k: full_matrix(k, gth) for k, gth in zip(later, gathered)})
    g["loss"] = jnp.broadcast_to(loss_blk[0:1, 0:1], (1, D))

    halves = {}
    sources = {"mm_d_act": ["w_ffn_down"], "mm_d_hn2": ["w_ffn_in"], "mm_d_win": ["w_out", "w_proj_a", "w_proj_b"],
               "mm_d_hn1": ["w_in"]}
    for name, keys in sources.items():
        partials, received = carried[name]
        for k, part, rec in zip(keys, partials, received):
            halves[k] = _sum_place(rec, part, k, xyc, "sum_" + k)
    late, _ = _pack_rows(g, PACKS["late"])
    tail = _run_host(_join_hosts([_sibling_host([halves[k] for k in BIG]), _pack_host(late)]), "swap_and_late_pack")
    g_big = dict(zip(BIG, tail[:len(BIG)]))
    recv_packs = dict(zip(EARLY, carried["mm_d_win"][1][3:]), late=tail[len(BIG)])

    grad, delta, new_m, new_v, summed = {}, {}, {}, {}, {}
    for pack, names in PACKS.items():
        row, params, extras = 0, [], []
        for k in names:
            if k in NO_UPDATE:
                extras.append((row, ROWS[k][0]))
            else:
                params.append((row,) + tuple(d[k].reshape(ROWS[k]) for d in (w, m, v)))
            row += ROWS[k][0]
        updated, rows_only = _adamw_rows(recv_packs[pack], params, extras, "adamw_" + pack)
        for k, res in zip([k for k in names if k not in NO_UPDATE], updated):
            grad[k], delta[k], new_m[k], new_v[k] = (a.reshape(w[k].shape) for a in res)
        summed.update(zip([k for k in names if k in NO_UPDATE], rows_only))
    loss = summed["loss"][0, 0]
    g_shard = {"meta_tokens": lax.dynamic_slice(summed["meta_tokens"], (0, chip * 256), (N_META, 256)),
               "conv_w": lax.dynamic_slice(summed["conv_w"], (0, chip * 320), (4, 320))}
    for k in BIG + SHARDED_SMALL:
        gk = g_big[k] if k in BIG else g_shard[k]
        wk, mk, vk = (d[k].reshape(gk.shape) for d in (w, m, v))
        dk, mk, vk = _adamw(wk, gk, mk, vk, "adamw_" + k)
        grad[k], delta[k], new_m[k], new_v[k] = (a.reshape(w[k].shape) for a in (gk, dk, mk, vk))

    return (loss, grad_x[None], *[grad[k] for k in ORDER], *[delta[k] for k in ORDER],
            *[new_m[k] for k in ORDER], *[new_v[k] for k in ORDER])
```

```python
import functools

import jax
import jax.numpy as jnp
from jax import lax
from jax.experimental import pallas as pl
from jax.experimental.pallas import tpu as pltpu

F32, BF16 = jnp.float32, jnp.bfloat16
D = 1024
N_META = 16
RGW = 1280
RG_BLOCKS, RG_BLOCK = 16, 80
RG_C = 8.0
HEADS, HD = 8, 128
HGW = HEADS * HD
DFF = 2816
D_IN = 2 * RGW + 6 * D
ROT = 2 * RGW
COL_PAD = 256
D_INP = D_IN + COL_PAD
EPS = 1e-6
CH = 64
TM = 256
ROW0 = TM
PAD = ROW0 - N_META
CT = RGW
RS = 16
NCT = RGW // CT
EXP_CLAMP = 80.0
VMEM_LIMIT = 56 * 1024 * 1024

LR, B1, B2, ADAM_EPS, WD, STEP = 0.001, 0.9, 0.999, 1e-08, 0.01, 10
MESH = pl.DeviceIdType.MESH
ANY = pl.BlockSpec(memory_space=pl.ANY)


def _cparams(sem):
    return pltpu.CompilerParams(dimension_semantics=sem, vmem_limit_bytes=VMEM_LIMIT)


def _pick(n, prefs):
    for p in prefs:
        if n % p == 0:
            return p
    return n


def _sig(x):
    return 0.5 * jnp.tanh(0.5 * x) + 0.5


def _dot(a, b, dims):
    return lax.dot_general(a, b, (dims, ((), ())), preferred_element_type=F32)


NN, NT, TN_ = ((1,), (0,)), ((1,), (1,)), ((0,), (0,))


class _Host:
    def __init__(self, ins, out_shapes, aliases, sems, start, mid, finish):
        self.ins, self.out_shapes, self.aliases, self.sems = ins, out_shapes, aliases, sems
        self.start, self.mid, self.finish = start, mid, finish


class _Epi:
    def __init__(self, ins, in_specs, out_shapes, out_specs, fn, sequential=False, split=1):
        self.ins, self.in_specs, self.out_shapes, self.out_specs, self.fn = ins, in_specs, out_shapes, out_specs, fn
        self.sequential = sequential
        self.split = split


class _Pro:
    def __init__(self, ins, in_specs, fn):
        self.ins, self.in_specs, self.fn = ins, in_specs, fn


def _norm_pro(gain):
    def fn(a_ref, ins):
        xh, _ = _rms(a_ref[...])
        return (xh * ins[0][...]).astype(BF16)

    return _Pro([gain], [pl.BlockSpec((1, gain.shape[1]), lambda i, j, k: (0, 0))], fn)


def _mm(a, b, mode, out_dtype, name, resid=None, host=None, epi=None, tiles=None, sparse=None, pro=None):
    if mode == "nn":
        (m, kd), n = a.shape, b.shape[1]
    elif mode == "nt":
        (m, kd), n = a.shape, b.shape[0]
    else:
        (kd, m), n = a.shape, b.shape[1]
    if mode == "tn":
        tm = _pick(m, (1024, 1280, 1408, 640, 512, 256, 128))
        tk = _pick(kd, (1408, 768, 512, 256, 128))
    else:
        tm = _pick(m, (768, 512, 640, 256, 128))
        tk = kd if kd <= 2816 else _pick(kd, (1792, 1408, 1024, 512, 256, 128))
    tn = _pick(n, (1792, 1408, 1280, 1024, 512, 256, 128))
    if tiles is not None:
        tm, tn = tiles
    a_map = (lambda i, j, k: (k, i)) if mode == "tn" else (lambda i, j, k: (i, k))
    b_map = (lambda i, j, k: (j, k)) if mode == "nt" else (lambda i, j, k: (k, j))
    o_map = lambda i, j, k: (i, j)
    if sparse is not None:
        (ni, nj, nk), (tm, tn, tk), (a_map, b_map, o_map) = sparse
    else:
        assert m % tm == 0 and n % tn == 0 and kd % tk == 0, (name, m, n, kd, tm, tn, tk)
        ni, nj, nk = m // tm, n // tn, kd // tk
    dims = {"nn": NN, "nt": NT, "tn": TN_}[mode]

    n_hin = len(host.ins) if host else 0
    n_hout = len(host.out_shapes) if host else 0
    n_res = 0 if resid is None else 1
    n_pro = 0 if pro is None else 1
    n_pin = len(pro.ins) if pro else 0
    n_ein = len(epi.ins) if epi else 0
    n_out = len(epi.out_shapes) if epi else 1
    assert not n_pro or (mode == "nn" and nk == 1)

    def finish(r, r_ref, e_in, o_refs, i, rows=slice(None), first=True):
        if resid is not None:
            r = r + r_ref[rows, :].astype(F32)
        if epi:
            epi.fn(r, e_in, o_refs, i, rows, first)
        else:
            o_refs[0][rows, :] = r.astype(out_dtype)

    def body(*refs):
        a_ref, b_ref = refs[:2]
        r_ref = refs[2] if resid is not None else None
        pos = 2 + n_res
        p_in = refs[pos:pos + n_pin]
        pos += n_pin
        e_in = refs[pos:pos + n_ein]
        pos += n_ein
        h_in = refs[pos:pos + n_hin]
        pos += n_hin
        o_ref = refs[pos:pos + n_out]
        pos += n_out
        hn_ref = refs[pos] if n_pro else None
        pos += n_pro
        h_out = refs[pos:pos + n_hout]
        scratch = refs[pos + n_hout:]
        n_acc = 1 if nk > 1 else 0
        h_sems = scratch[n_acc + n_pro:]
        i, j, k = pl.program_id(0), pl.program_id(1), pl.program_id(2)
        if host:
            @pl.when((i == 0) & (j == 0) & (k == 0))
            def _():
                host.start(h_in, h_out, h_sems)

            if host.mid is not None:
                @pl.when((i == (3 * ni) // 4) & (j == 0) & (k == 0))
                def _():
                    host.mid(h_in, h_out, h_sems)

        if n_pro:
            a_s = scratch[n_acc]

            @pl.when(j == 0)
            def _():
                a_s[...] = pro.fn(a_ref, p_in)
                hn_ref[...] = a_s[...]

            a_val = a_s[...]
        else:
            a_val = a_ref[...]

        if nk == 1 and epi and epi.split > 1 and mode != "tn":
            part = tm // epi.split
            for p in range(epi.split):
                rows = slice(p * part, (p + 1) * part)
                finish(_dot(a_val[rows, :], b_ref[...], dims), r_ref, e_in, o_ref, i, rows, p == 0)
        elif nk == 1:
            finish(_dot(a_val, b_ref[...], dims), r_ref, e_in, o_ref, i)
        else:
            acc = scratch[0]

            @pl.when(k == 0)
            def _():
                acc[...] = jnp.zeros_like(acc)

            acc[...] += _dot(a_val, b_ref[...], dims)

            @pl.when(k == nk - 1)
            def _():
                finish(acc[...], r_ref, e_in, o_ref, i)

        if host:
            @pl.when((i == ni - 1) & (j == nj - 1) & (k == nk - 1))
            def _():
                host.finish(h_in, h_out, h_sems)

    a_spec = pl.BlockSpec((tk, tm) if mode == "tn" else (tm, tk), a_map)
    b_spec = pl.BlockSpec((tn, tk) if mode == "nt" else (tk, tn), b_map)
    o_spec = pl.BlockSpec((tm, tn), o_map)
    in_specs, args = [a_spec, b_spec], [a, b]
    if resid is not None:
        in_specs.append(o_spec)
        args.append(resid)
    if n_pro:
        in_specs += list(pro.in_specs)
        args += list(pro.ins)
    out_shapes, out_specs = [jax.ShapeDtypeStruct((m, n), out_dtype)], [o_spec]
    if epi:
        in_specs += list(epi.in_specs)
        args += list(epi.ins)
        out_shapes, out_specs = list(epi.out_shapes), list(epi.out_specs)
    if n_pro:
        out_shapes.append(jax.ShapeDtypeStruct((m, kd), BF16))
        out_specs.append(pl.BlockSpec((tm, kd), lambda i, j, k: (i, 0)))
    scratch = ([pltpu.VMEM((tm, tn), F32)] if nk > 1 else []) + ([pltpu.VMEM((tm, kd), BF16)] if n_pro else [])
    sequential = host or n_pro or (epi and epi.sequential)
    outs = pl.pallas_call(
        body, name=name, grid=(ni, nj, nk),
        in_specs=in_specs + [ANY] * n_hin, out_specs=out_specs + [ANY] * n_hout,
        out_shape=out_shapes + (list(host.out_shapes) if host else []),
        scratch_shapes=scratch + (list(host.sems) if host else []),
        input_output_aliases=({2 + n_res + n_pin + n_ein + i_in: n_out + n_pro + i_out
                               for i_in, i_out in host.aliases.items()} if host else {}),
        compiler_params=_cparams(("arbitrary",) * 3 if sequential else ("parallel", "parallel", "arbitrary")),
    )(*args, *(host.ins if host else []))
    main = list(outs[:n_out]) if epi else outs[0]
    if n_pro:
        main = (main, outs[n_out])
    return (main, list(outs[n_out + n_pro:])) if host else main


def _rms(x):
    r = lax.rsqrt(jnp.mean(x * x, axis=-1, keepdims=True) + EPS)
    return x * r, r


def _rms_bwd_math(dhn, x, g):
    xh, r = _rms(x)
    dxh = dhn * g
    dx = r * (dxh - xh * jnp.mean(dxh * xh, axis=-1, keepdims=True))
    return dx, jnp.sum(dhn * xh, axis=0, keepdims=True)


def _row_specs(tm):
    return pl.BlockSpec((tm, D), lambda i, j, k: (i, 0)), pl.BlockSpec((1, D), lambda i, j, k: (0, 0))


def _final_loss_epi(gf, target, t):
    def fn(x, ins, outs, i, rows, first):
        g_ref, t_ref = ins
        dh_ref, dhb_ref, loss_ref, dg_ref = outs

        if first:
            @pl.when(i == 0)
            def _():
                loss_ref[...] = jnp.zeros_like(loss_ref)
                dg_ref[...] = jnp.zeros_like(dg_ref)

        g = g_ref[...]
        xh, _ = _rms(x)
        err = jnp.where(i > 0, xh * g - t_ref[rows, :], 0.0)
        loss_ref[...] += 0.5 * jnp.sum(jnp.sum(err * err, axis=-1, keepdims=True) * (1.0 / D))
        dx, dg = _rms_bwd_math(err * (1.0 / D), x, g)
        dh_ref[rows, :] = dx
        dhb_ref[rows, :] = dx.astype(BF16)
        dg_ref[...] += dg

    row, vec = _row_specs(TM)
    return _Epi(ins=[gf, target], in_specs=[vec, pl.BlockSpec((TM, D), lambda i, j, k: (jnp.maximum(i - 1, 0), 0))],
                out_shapes=[jax.ShapeDtypeStruct((t, D), F32), jax.ShapeDtypeStruct((t, D), BF16),
                            jax.ShapeDtypeStruct((8, 128), F32), jax.ShapeDtypeStruct((1, D), F32)],
                out_specs=[row, row, pl.BlockSpec((8, 128), lambda i, j, k: (0, 0)), vec], fn=fn, sequential=True)


def _rms_bwd_epi(h, g, dres, tm):
    t = h.shape[0]

    def fn(dhn, ins, outs, i, rows, first):
        h_ref, g_ref, dres_ref = ins
        dh_ref, dhb_ref, dg_ref = outs

        if first:
            @pl.when(i == 0)
            def _():
                dg_ref[...] = jnp.zeros_like(dg_ref)

        dx, dg = _rms_bwd_math(dhn, h_ref[rows, :], g_ref[...])
        dh = dres_ref[rows, :] + dx
        dh_ref[rows, :] = dh
        dhb_ref[rows, :] = dh.astype(BF16)
        dg_ref[...] += dg

    row, vec = _row_specs(tm)
    return _Epi(ins=[h, g, dres], in_specs=[row, vec, row],
                out_shapes=[jax.ShapeDtypeStruct((t, D), F32), jax.ShapeDtypeStruct((t, D), BF16),
                            jax.ShapeDtypeStruct((1, D), F32)],
                out_specs=[row, row, vec], fn=fn, sequential=True)


def _rmsnorm1_bwd(dhn, h, g, dres):
    t = h.shape[0]

    def body(dhn_ref, h_ref, g_ref, dres_ref, dx_ref, dmeta_ref, dg_ref):
        i = pl.program_id(0)

        @pl.when(i == 0)
        def _():
            dg_ref[...] = jnp.zeros_like(dg_ref)

        dx, dg = _rms_bwd_math(dhn_ref[...].astype(F32), h_ref[...], g_ref[...])
        dh = dres_ref[...] + dx
        dg_ref[...] += dg
        dx_ref[...] = dh

        @pl.when(i == 0)
        def _():
            dmeta_ref[...] = dh[PAD:, :]

    row = pl.BlockSpec((TM, D), lambda i: (i, 0))
    vec = pl.BlockSpec((1, D), lambda i: (0, 0))
    return pl.pallas_call(
        body, name="rmsnorm1_bwd", grid=(t // TM,),
        in_specs=[row, row, vec, row],
        out_specs=[pl.BlockSpec((TM, D), lambda i: (jnp.maximum(i - 1, 0), 0)),
                   pl.BlockSpec((N_META, D), lambda i: (0, 0)), vec],
        out_shape=[jax.ShapeDtypeStruct((t - ROW0, D), F32), jax.ShapeDtypeStruct((N_META, D), F32),
                   jax.ShapeDtypeStruct((1, D), F32)],
        compiler_params=_cparams(("arbitrary",)),
    )(dhn, h, g, dres)


FH = DFF // 2
FM = 384


def _swiglu_epi(t):
    def fn(r, _, outs, i, rows, first):
        g, u = r[:, :FH], r[:, FH:]
        outs[0][rows, :] = r.astype(BF16)
        outs[1][rows, :] = (g * _sig(g) * u).astype(BF16)

    return _Epi(ins=[], in_specs=[],
                out_shapes=[jax.ShapeDtypeStruct((t, 2 * DFF), BF16), jax.ShapeDtypeStruct((t, DFF), BF16)],
                out_specs=[pl.BlockSpec((FM, 2 * FH), lambda i, j, k: (i, j)), pl.BlockSpec((FM, FH), lambda i, j, k: (i, j))],
                fn=fn)


def _swiglu_bwd_epi(gu):
    def fn(d, ins, outs, i, rows, first):
        gu_t = ins[0][rows, :].astype(F32)
        g, u = gu_t[:, :FH], gu_t[:, FH:]
        s = _sig(g)
        outs[0][rows, :FH] = (d * u * (s * (1.0 + g * (1.0 - s)))).astype(BF16)
        outs[0][rows, FH:] = (d * (g * s)).astype(BF16)

    spec = pl.BlockSpec((FM, 2 * FH), lambda i, j, k: (i, j))
    return _Epi(ins=[gu], in_specs=[spec], out_shapes=[jax.ShapeDtypeStruct(gu.shape, BF16)], out_specs=[spec], fn=fn)


COL_GA, COL_GB = 4, 5
COL_AX, COL_AG = (6 * D + COL_PAD) // CT, (6 * D + COL_PAD + RGW) // CT


def _merge_pro(pb, proj, tm):
    def fn(a_ref, ins):
        pa, pb_, ga, gb = (r[...].astype(F32) for r in (a_ref,) + tuple(ins))
        return (_sig(ga) * pa + _sig(gb) * pb_).astype(BF16)

    row, _ = _row_specs(tm)
    return _Pro([pb, proj, proj], [row, pl.BlockSpec((tm, D), lambda i, j, k: (i, COL_GA)),
                                   pl.BlockSpec((tm, D), lambda i, j, k: (i, COL_GB))], fn)


def _merge_bwd_epi(pa, pb, proj, tm):
    t = pa.shape[0]

    def fn(d, ins, outs, i, rows, first):
        pa, pb, ga, gb = (r[rows, :].astype(F32) for r in ins)
        dpa_ref, dpb_ref, dproj_ref = outs
        sa, sb = _sig(ga), _sig(gb)
        dpa_ref[rows, :] = (sa * d).astype(BF16)
        dpb_ref[rows, :] = (sb * d).astype(BF16)
        dproj_ref[rows, :D] = (d * pa * sa * (1.0 - sa)).astype(BF16)
        dproj_ref[rows, D:] = (d * pb * sb * (1.0 - sb)).astype(BF16)

    row, _ = _row_specs(tm)
    return _Epi(ins=[pa, pb, proj, proj],
                in_specs=[row, row, pl.BlockSpec((tm, D), lambda i, j, k: (i, COL_GA)),
                          pl.BlockSpec((tm, D), lambda i, j, k: (i, COL_GB))],
                out_shapes=[jax.ShapeDtypeStruct((t, D), BF16), jax.ShapeDtypeStruct((t, D), BF16),
                            jax.ShapeDtypeStruct((t, D_INP), BF16)],
                out_specs=[row, row, pl.BlockSpec((tm, 2 * D), lambda i, j, k: (i, 2))], fn=fn)


def _zero_pad_cols(dproj):
    t = dproj.shape[0]
    tz = _pick(t, (768, 256))

    def body(_, o_ref):
        o_ref[...] = jnp.zeros_like(o_ref)

    return pl.pallas_call(
        body, name="dproj_pad", grid=(t // tz,), in_specs=[ANY],
        out_specs=pl.BlockSpec((tz, COL_PAD), lambda i: (i, (6 * D) // COL_PAD)),
        out_shape=jax.ShapeDtypeStruct(dproj.shape, BF16), input_output_aliases={0: 0},
        compiler_params=_cparams(("parallel",)),
    )(dproj)


HALO = 16


def _rows_before(cur, before, s):
    n = cur.shape[0]
    return jnp.concatenate([before, cur], axis=0)[n - s:2 * n - s, :]


def _rows_after(cur, after, s):
    n = cur.shape[0]
    return jnp.concatenate([cur, after], axis=0)[s:n + s, :]


def _prev_halo(col0):
    return pl.BlockSpec((HALO, CT), lambda c, i: (jnp.maximum(i * (TM // HALO) - 1, 0), col0 + c))


def _conv_fwd(proj, w, b):
    t = proj.shape[0]

    def body(x_ref, halo_ref, w_ref, b_ref, ob_ref):
        i = pl.program_id(1)
        wv, bv = w_ref[...], b_ref[...]
        first = jnp.where(i > 0, halo_ref[...].astype(F32), 0.0)

        def strip(k, before):
            rows = pl.ds(pl.multiple_of(k * HALO, HALO), HALO)
            x = x_ref[rows, :].astype(F32)
            y = bv + wv[3:4, :] * x
            for j in range(3):
                y = y + wv[j:j + 1, :] * _rows_before(x, before, 3 - j)
            ob_ref[rows, :] = y.astype(BF16)
            return x

        lax.fori_loop(0, TM // HALO, strip, first)

    blk = pl.BlockSpec((TM, CT), lambda c, i: (i, c))
    return pl.pallas_call(
        body, name="conv_fwd", grid=(NCT, t // TM),
        in_specs=[pl.BlockSpec((TM, CT), lambda c, i: (i, COL_AX + c)), _prev_halo(COL_AX),
                  pl.BlockSpec((4, CT), lambda c, i: (0, c)), pl.BlockSpec((1, CT), lambda c, i: (0, c))],
        out_specs=blk,
        out_shape=jax.ShapeDtypeStruct((t, RGW), BF16),
        compiler_params=_cparams(("parallel", "parallel")),
    )(proj, proj, w, b)


def _conv_bwd(dxc, proj, w, dproj):
    t = proj.shape[0]
    nt = t // TM

    def body(d_ref, dn_ref, x_ref, halo_ref, w_ref, _, dx_ref, dw_ref, db_ref, acc):
        i = pl.program_id(1)

        @pl.when(i == 0)
        def _():
            dw_ref[...] = jnp.zeros_like(dw_ref)
            db_ref[...] = jnp.zeros_like(db_ref)

        acc[...] = jnp.zeros_like(acc)
        wv = w_ref[...]
        first = jnp.where(i > 0, halo_ref[...].astype(F32), 0.0)
        last = jnp.where(i < nt - 1, dn_ref[...].astype(F32), 0.0)
        ns = TM // HALO

        def fold(v):
            return v[0:8, :] + v[8:16, :]

        def strip(k, before):
            off = pl.multiple_of(k * HALO, HALO)
            rows = pl.ds(off, HALO)
            x, d = x_ref[rows, :].astype(F32), d_ref[rows, :].astype(F32)
            nxt = d_ref[pl.ds(pl.multiple_of(jnp.minimum(off + HALO, TM - HALO), HALO), HALO), :].astype(F32)
            after = jnp.where(k == ns - 1, last, nxt)
            dx = wv[3:4, :] * d
            for j in range(3):
                dx = dx + wv[j:j + 1, :] * _rows_after(d, after, 3 - j)
                acc[j] += fold(d * _rows_before(x, before, 3 - j))
            acc[3] += fold(d * x)
            acc[4] += fold(d)
            dx_ref[rows, :] = dx.astype(BF16)
            return x

        lax.fori_loop(0, ns, strip, first)
        for j in range(4):
            dw_ref[j:j + 1, :] += jnp.sum(acc[j], axis=0, keepdims=True)
        db_ref[...] += jnp.sum(acc[4], axis=0, keepdims=True)

    return pl.pallas_call(
        body, name="conv_bwd", grid=(NCT, nt),
        in_specs=[pl.BlockSpec((TM, CT), lambda c, i: (i, c)),
                  pl.BlockSpec((HALO, CT), lambda c, i: (jnp.minimum((i + 1) * (TM // HALO), t // HALO - 1), c)),
                  pl.BlockSpec((TM, CT), lambda c, i: (i, COL_AX + c)), _prev_halo(COL_AX),
                  pl.BlockSpec((4, CT), lambda c, i: (0, c)), ANY],
        out_specs=[pl.BlockSpec((TM, CT), lambda c, i: (i, COL_AX + c)),
                   pl.BlockSpec((4, CT), lambda c, i: (0, c)), pl.BlockSpec((1, CT), lambda c, i: (0, c))],
        out_shape=[jax.ShapeDtypeStruct((t, D_INP), BF16), jax.ShapeDtypeStruct((4, RGW), F32),
                   jax.ShapeDtypeStruct((1, RGW), F32)],
        scratch_shapes=[pltpu.VMEM((5, 8, CT), F32)],
        input_output_aliases={5: 0},
        compiler_params=_cparams(("parallel", "arbitrary")),
    )(dxc, dxc, proj, proj, w, dproj)


def _gelu(x):
    c = 0.7978845608028654
    th = jnp.tanh(c * (x + 0.044715 * x * x * x))
    return 0.5 * x * (1.0 + th), th


def _rg_gates(gr, gi, xc, ba, bx, lam, row0):
    r = _sig(gr + ba)
    ig = _sig(gi + bx)
    sp = jax.nn.softplus(-lam)
    a = jnp.exp(-RG_C * r * sp)
    s2 = jnp.maximum(1.0 - a * a, 1e-30)
    rs = lax.rsqrt(s2)
    s = s2 * rs
    rows = row0 + lax.broadcasted_iota(jnp.int32, gr.shape, 0)
    live = rows >= PAD
    u = jnp.where(live, s * ig * xc, 0.0)
    return r, ig, sp, a, s, rs, u, live


def _rg_fwd(gri, xc, proj, ba, bx, lam):
    t = gri.shape[0]

    def body(gr_ref, gi_ref, xc_ref, ag_ref, ba_ref, bx_ref, lam_ref, h_ref, ya_ref, hc):
        i = pl.program_id(1)

        @pl.when(i == 0)
        def _():
            hc[...] = jnp.zeros_like(hc)

        ba, bx, lam = ba_ref[...], bx_ref[...], lam_ref[...]
        sub = lax.broadcasted_iota(jnp.int32, (8, CT), 0)

        def strip(k, h):
            off = pl.multiple_of(k * RS, RS)
            rows = pl.ds(off, RS)
            _, _, _, a, _, _, u, _ = _rg_gates(gr_ref[rows, :], gi_ref[rows, :], xc_ref[rows, :].astype(F32), ba, bx, lam,
                                               i * TM + off)
            outs = []
            for half in range(RS // 8):
                out = jnp.zeros((8, CT), F32)
                for r in range(8):
                    h = a[8 * half + r:8 * half + r + 1, :] * h + u[8 * half + r:8 * half + r + 1, :]
                    out = jnp.where(sub == r, h, out)
                outs.append(out)
            hs = jnp.concatenate(outs, axis=0)
            h_ref[rows, :] = hs
            ge, _ = _gelu(ag_ref[rows, :].astype(F32))
            ya_ref[rows, :] = (hs * ge).astype(BF16)
            return h

        hc[...] = lax.fori_loop(0, TM // RS, strip, hc[...])

    blk_ = pl.BlockSpec((TM, CT), lambda c, i: (i, c))
    vec = pl.BlockSpec((1, CT), lambda c, i: (0, c))
    return pl.pallas_call(
        body, name="rg_fwd", grid=(NCT, t // TM),
        in_specs=[blk_, pl.BlockSpec((TM, CT), lambda c, i: (i, NCT + c)), blk_,
                  pl.BlockSpec((TM, CT), lambda c, i: (i, COL_AG + c)), vec, vec, vec],
        out_specs=[blk_, blk_],
        out_shape=[jax.ShapeDtypeStruct((t, RGW), F32), jax.ShapeDtypeStruct((t, RGW), BF16)],
        scratch_shapes=[pltpu.VMEM((1, CT), F32)],
        compiler_params=_cparams(("parallel", "arbitrary")),
    )(gri, gri, xc, proj, ba, bx, lam)


def _rg_bwd(dya, h, gri, xc, proj, ba, bx, lam, dproj):
    t = gri.shape[0]
    nt = t // TM
    assert NCT == 1

    def body(dya_ref, h_ref, hh_ref, gr_ref, gi_ref, xc_ref, ag_ref, ba_ref, bx_ref, lam_ref, _,
             dgri_ref, dxc_ref, dag_ref, dba_ref, dbx_ref, dlam_ref, hbuf, acc, cc):
        i = pl.program_id(1)
        ri = nt - 1 - i

        @pl.when(i == 0)
        def _():
            cc[...] = jnp.zeros_like(cc)
            dba_ref[...] = jnp.zeros_like(dba_ref)
            dbx_ref[...] = jnp.zeros_like(dbx_ref)
            dlam_ref[...] = jnp.zeros_like(dlam_ref)

        acc[...] = jnp.zeros_like(acc)
        ba, bx, lam = ba_ref[...], bx_ref[...], lam_ref[...]
        halo_last = jnp.where(ri > 0, hh_ref[7:8, :], 0.0)
        sub = lax.broadcasted_iota(jnp.int32, (8, CT), 0)
        c0 = 0.7978845608028654

        def strip(kk, c):
            k = TM // RS - 1 - kk
            off = pl.multiple_of(k * RS, RS)
            rows = pl.ds(off, RS)
            xc, hv = xc_ref[rows, :].astype(F32), h_ref[rows, :]
            ag, dya = ag_ref[rows, :].astype(F32), dya_ref[rows, :].astype(F32)
            r, ig, sp, a, s, rs, _, live = _rg_gates(gr_ref[rows, :], gi_ref[rows, :], xc, ba, bx, lam, ri * TM + off)
            ge, th = _gelu(ag)
            dge = 0.5 * (1.0 + th) + 0.5 * ag * (1.0 - th * th) * c0 * (1.0 + 3.0 * 0.044715 * ag * ag)
            dag_ref[rows, :] = (dya * hv * dge).astype(BF16)
            d = dya * ge
            outs = []
            for half in range(RS // 8 - 1, -1, -1):
                out = jnp.zeros((8, CT), F32)
                for rr in range(7, -1, -1):
                    g = d[8 * half + rr:8 * half + rr + 1, :] + c
                    c = a[8 * half + rr:8 * half + rr + 1, :] * g
                    out = jnp.where(sub == rr, g, out)
                outs.insert(0, out)
            g = jnp.concatenate(outs, axis=0)
            before = h_ref[pl.ds(pl.multiple_of(jnp.maximum(off - RS, 0), RS), RS), :]
            hbuf[7:8, :] = jnp.where(k == 0, halo_last, before[RS - 1:RS, :])
            hbuf[8:, :] = hv
            hprev = hbuf[pl.ds(7, RS), :]
            du = jnp.where(live, g, 0.0)
            ds = du * (ig * xc)
            dm = du * s
            dla = (g * hprev - a * ds * rs) * a
            dr = dla * (-RG_C * sp) * r * (1.0 - r)
            di = dm * xc * ig * (1.0 - ig)
            dgri_ref[rows, :CT] = dr.astype(BF16)
            dgri_ref[rows, CT:] = di.astype(BF16)
            dxc_ref[rows, :] = (dm * ig).astype(BF16)
            dl = dla * (-RG_C * r)
            for half in range(RS // 8):
                part = slice(8 * half, 8 * half + 8)
                acc[0] += dr[part]
                acc[1] += di[part]
                acc[2] += dl[part]
            return c

        cc[...] = lax.fori_loop(0, TM // RS, strip, cc[...])
        dba_ref[...] += jnp.sum(acc[0], axis=0, keepdims=True)
        dbx_ref[...] += jnp.sum(acc[1], axis=0, keepdims=True)
        dlam_ref[...] += jnp.sum(acc[2], axis=0, keepdims=True) * (-_sig(-lam))

    rblk = pl.BlockSpec((TM, CT), lambda c, i: (nt - 1 - i, c))
    vec = pl.BlockSpec((1, CT), lambda c, i: (0, c))
    hh = pl.BlockSpec((8, CT), lambda c, i: (jnp.maximum((nt - 1 - i) * (TM // 8) - 1, 0), c))
    agb = pl.BlockSpec((TM, CT), lambda c, i: (nt - 1 - i, COL_AG + c))
    return pl.pallas_call(
        body, name="rg_bwd", grid=(NCT, nt),
        in_specs=[rblk, rblk, hh, rblk, pl.BlockSpec((TM, CT), lambda c, i: (nt - 1 - i, NCT + c)), rblk, agb,
                  vec, vec, vec, ANY],
        out_specs=[pl.BlockSpec((TM, 2 * CT), lambda c, i: (nt - 1 - i, c)), rblk, agb, vec, vec, vec],
        out_shape=[jax.ShapeDtypeStruct((t, 2 * RGW), BF16),
                   jax.ShapeDtypeStruct((t, RGW), BF16), jax.ShapeDtypeStruct((t, D_INP), BF16),
                   jax.ShapeDtypeStruct((1, RGW), F32), jax.ShapeDtypeStruct((1, RGW), F32),
                   jax.ShapeDtypeStruct((1, RGW), F32)],
        scratch_shapes=[pltpu.VMEM((RS + 8, CT), F32), pltpu.VMEM((3, 8, CT), F32), pltpu.VMEM((1, CT), F32)],
        input_output_aliases={10: 2},
        compiler_params=_cparams(("parallel", "arbitrary")),
    )(dya, h, h, gri, gri, xc, proj, ba, bx, lam, dproj)


NCH = TM // CH

def _tri_dot(tri, x):
    hi = x.astype(BF16)
    r1 = x - hi.astype(F32)
    mid = r1.astype(BF16)
    lo = (r1 - mid.astype(F32)).astype(BF16)
    return _dot(tri, hi, NN) + _dot(tri, mid, NN) + _dot(tri, lo, NN)


def _hg_chunk(qr, fr, lb):
    sf = _sig(fr)
    fg = lb + (1.0 - lb) * sf
    k = (1.0 - lb) * (1.0 - sf)
    sq = _sig(qr)
    q = qr * sq
    ri = lax.broadcasted_iota(jnp.int32, (CH, CH), 0)
    ci = lax.broadcasted_iota(jnp.int32, (CH, CH), 1)
    b = _tri_dot((ri >= ci).astype(BF16), jnp.log(fg))
    bm, bl = b[CH // 2 - 1:CH // 2, :], b[CH - 1:CH, :]
    ebm = jnp.exp(jnp.minimum(b - bm, EXP_CLAMP))
    ekm = jnp.exp(jnp.minimum(bm - b, EXP_CLAMP))
    eb = ebm * jnp.exp(bm)
    ekl = ekm * jnp.exp(bl - bm)
    return dict(sf=sf, fg=fg, k=k, sq=sq, q=q, eb=eb, ebm=ebm, ekm=ekm, ekl=ekl, ebl=jnp.exp(bl),
                qe=q * eb, qh=q * ebm, kh=k * ekm, kd=k * ekl, causal=ri >= ci, upper=(ci >= ri).astype(BF16))


def _hgrn_fwd(proj, lbl, gn):
    t = proj.shape[0]
    nt = t // TM

    def body(q_ref, f_ref, v_ref, g_ref, lbl_ref, gn_ref, yb_ref, o_ref, st_ref, st):
        @pl.when(pl.program_id(0) == 0)
        def _():
            st[...] = jnp.zeros_like(st)

        l = lbl_ref[...]
        lb = _sig(l[0:1, :] - l[1:2, :])
        gnv = gn_ref[...]

        def chunk(c, carry):
            off = pl.multiple_of(c * CH, CH)
            rows = pl.ds(off, CH)
            z = _hg_chunk(q_ref[rows, :].astype(F32), f_ref[rows, :].astype(F32), lb)
            v, gg = v_ref[rows, :], g_ref[rows, :].astype(F32)
            for hh in range(HEADS):
                sl = slice(hh * HD, (hh + 1) * HD)
                s_prev = st[hh]
                st_ref[c, hh] = s_prev
                vb = v[:, sl].astype(BF16)
                att = jnp.where(z["causal"], _dot(z["qh"][:, sl].astype(BF16), z["kh"][:, sl].astype(BF16), NT), 0.0)
                o = _dot(z["qe"][:, sl].astype(BF16), s_prev.astype(BF16), NT) + _dot(att.astype(BF16), vb, NN)
                st[hh] = s_prev * z["ebl"][:, sl] + _dot(vb, z["kd"][:, sl].astype(BF16), TN_)
                xh, _ = _rms(o)
                gh = gg[:, sl]
                o_ref[rows, sl] = o
                yb_ref[rows, sl] = (xh * gnv * gh * _sig(gh)).astype(BF16)
            return carry

        lax.fori_loop(0, NCH, chunk, 0, unroll=True)

    def col(j):
        return pl.BlockSpec((TM, HGW), lambda i, j=j: (i, j))

    return pl.pallas_call(
        body, name="hgrn_fwd", grid=(nt,),
        in_specs=[col(0), col(1), col(2), col(3), pl.BlockSpec((2, HGW), lambda i: (0, 0)),
                  pl.BlockSpec((1, HD), lambda i: (0, 0))],
        out_specs=[col(0), col(0), pl.BlockSpec((NCH, HEADS, HD, HD), lambda i: (i, 0, 0, 0))],
        out_shape=[jax.ShapeDtypeStruct((t, HGW), BF16), jax.ShapeDtypeStruct((t, HGW), F32),
                   jax.ShapeDtypeStruct((t // CH, HEADS, HD, HD), F32)],
        scratch_shapes=[pltpu.VMEM((HEADS, HD, HD), F32)],
        compiler_params=_cparams(("arbitrary",)),
    )(proj, proj, proj, proj, lbl, gn)


def _hgrn_bwd(dyb, proj, o, states, lbl, gn, dproj):
    t = proj.shape[0]
    nt = t // TM

    def body(dy_ref, q_ref, f_ref, v_ref, g_ref, o_ref, st_ref, lbl_ref, gn_ref, _,
             dp_ref, dgn_ref, dl_ref, dst, dlb):
        i = pl.program_id(0)

        @pl.when(i == 0)
        def _():
            dst[...] = jnp.zeros_like(dst)
            dlb[...] = jnp.zeros_like(dlb)
            dgn_ref[...] = jnp.zeros_like(dgn_ref)

        l = lbl_ref[...]
        lb = _sig(l[0:1, :] - l[1:2, :])
        gnv = gn_ref[...]
        last = lax.broadcasted_iota(jnp.int32, (CH, HD), 0) == CH - 1

        def chunk(cc, carry):
            c = NCH - 1 - cc
            off = pl.multiple_of(c * CH, CH)
            rows = pl.ds(off, CH)
            qr, fr = q_ref[rows, :].astype(F32), f_ref[rows, :].astype(F32)
            z = _hg_chunk(qr, fr, lb)
            v, gg, ov, dy = v_ref[rows, :], g_ref[rows, :].astype(F32), o_ref[rows, :], dy_ref[rows, :].astype(F32)
            dqs, dks, dbs, dvs, dgs = [], [], [], [], []
            dgn = jnp.zeros((1, HD), F32)
            for hh in range(HEADS):
                sl = slice(hh * HD, (hh + 1) * HD)
                s_prev, ds_new = st_ref[c, hh], dst[hh]
                qe, qh, kh, kd = z["qe"][:, sl], z["qh"][:, sl], z["kh"][:, sl], z["kd"][:, sl]
                ebl = z["ebl"][:, sl]
                gh, dyh = gg[:, sl], dy[:, sl]
                xh, rr = _rms(ov[:, sl])
                sg = _sig(gh)
                dyn = dyh * (gh * sg)
                dgs.append(dyh * (xh * gnv) * (sg * (1.0 + gh * (1.0 - sg))))
                dgn = dgn + jnp.sum(dyn * xh, axis=0, keepdims=True)
                dxh = dyn * gnv
                do = (rr * (dxh - xh * jnp.mean(dxh * xh, axis=-1, keepdims=True))).astype(BF16)
                vb, dsb = v[:, sl].astype(BF16), ds_new.astype(BF16)
                qeb, qhb, khb, kdb = (a.astype(BF16) for a in (qe, qh, kh, kd))
                att = jnp.where(z["causal"], _dot(qhb, khb, NT), 0.0).astype(BF16)
                datt = jnp.where(z["causal"], _dot(do, vb, NT), 0.0).astype(BF16)
                dvs.append(_dot(att, do, TN_) + _dot(kdb, dsb, NT))
                dqe = _dot(do, s_prev.astype(BF16), NN)
                dqh = _dot(datt, khb, NN)
                dkh = _dot(datt, qhb, TN_)
                dkd = _dot(vb, dsb, NN)
                qe, qh, kh, kd = (a.astype(F32) for a in (qeb, qhb, khb, kdb))
                dbl = (jnp.sum(dkd * kd, axis=0, keepdims=True)
                       + jnp.sum(ds_new * s_prev, axis=0, keepdims=True) * ebl)
                dqs.append(dqe * z["eb"][:, sl] + dqh * z["ebm"][:, sl])
                dks.append(dkh * z["ekm"][:, sl] + dkd * z["ekl"][:, sl])
                dbs.append(dqe * qe + dqh * qh - dkh * kh - dkd * kd + jnp.where(last, dbl, 0.0))
                dst[hh] = _dot(do, qeb, TN_) + ds_new * ebl
            dgn_ref[...] += dgn
            dq, dk, db = (jnp.concatenate(x, axis=1) for x in (dqs, dks, dbs))
            dlf = _tri_dot(z["upper"], db)
            sf, fg, sq = z["sf"], z["fg"], z["sq"]
            dmix = dlf / fg - dk
            dsf = dmix * (1.0 - lb)
            dlb[...] += jnp.sum(dmix * (1.0 - sf), axis=0, keepdims=True)
            dp_ref[rows, 0:HGW] = (dq * (sq * (1.0 + qr * (1.0 - sq)))).astype(BF16)
            dp_ref[rows, HGW:2 * HGW] = (dsf * sf * (1.0 - sf)).astype(BF16)
            dp_ref[rows, 2 * HGW:3 * HGW] = jnp.concatenate(dvs, axis=1).astype(BF16)
            dp_ref[rows, 3 * HGW:4 * HGW] = jnp.concatenate(dgs, axis=1).astype(BF16)
            return carry

        lax.fori_loop(0, NCH, chunk, 0, unroll=True)
        dl0 = dlb[...] * lb * (1.0 - lb)
        dl_ref[0:1, :] = dl0
        dl_ref[1:2, :] = -dl0

    def col(j):
        return pl.BlockSpec((TM, HGW), lambda i, j=j: (nt - 1 - i, j))

    return pl.pallas_call(
        body, name="hgrn_bwd", grid=(nt,),
        in_specs=[col(0), col(0), col(1), col(2), col(3), col(0),
                  pl.BlockSpec((NCH, HEADS, HD, HD), lambda i: (nt - 1 - i, 0, 0, 0)),
                  pl.BlockSpec((2, HGW), lambda i: (0, 0)), pl.BlockSpec((1, HD), lambda i: (0, 0)), ANY],
        out_specs=[pl.BlockSpec((TM, 4 * HGW), lambda i: (nt - 1 - i, 0)),
                   pl.BlockSpec((1, HD), lambda i: (0, 0)), pl.BlockSpec((2, HGW), lambda i: (0, 0))],
        out_shape=[jax.ShapeDtypeStruct((t, D_INP), BF16), jax.ShapeDtypeStruct((1, HD), F32),
                   jax.ShapeDtypeStruct((2, HGW), F32)],
        scratch_shapes=[pltpu.VMEM((HEADS, HD, HD), F32), pltpu.VMEM((1, HGW), F32)],
        input_output_aliases={9: 0},
        compiler_params=_cparams(("arbitrary",)),
    )(dyb, proj, proj, proj, proj, o, states, lbl, gn, dproj)


def _rotate_w_in(w):
    return jnp.concatenate([w[:, ROT:], jnp.zeros((w.shape[0], COL_PAD), w.dtype), w[:, :ROT]], axis=1)


def _unrotate_g_in(g):
    return jnp.concatenate([g[:, D_INP - ROT:], g[:, :D_IN - ROT]], axis=1)


SB = RGW // 2


def _bd_gates(t):
    tm = _pick(t, (768, 256))

    def col(j):
        return 2 * (j % 2) + j // 2

    return ((t // tm, 4, 1), (tm, SB, SB),
            (lambda i, j, k: (i, j // 2), lambda i, j, k: (j // 2, col(j)), lambda i, j, k: (i, col(j))))


def _bd_dw(t):
    tk = _pick(t, (1408, 768))
    return (1, 4, t // tk), (SB, SB, tk), (lambda i, j, k: (k, j % 2), lambda i, j, k: (k, j), lambda i, j, k: (j % 2, j))


def _block_diag(w):
    eye = jnp.eye(RG_BLOCKS, dtype=w.dtype)
    return (w[:, :, None, :] * eye[:, None, :, None]).reshape(RGW, RGW)


def _diag_blocks(wd):
    w4 = wd.reshape(RG_BLOCKS, RG_BLOCK, RG_BLOCKS, RG_BLOCK)
    return jnp.stack([w4[n, :, n, :] for n in range(RG_BLOCKS)])


def _local_step(h0, target, wts, small, hosts=None, finalize=None):
    hosts = hosts or {}
    carried = {}
    g = {}
    t = h0.shape[0]

    def mm(a, b, mode, out_dtype, name, **kw):
        if name not in hosts:
            return _mm(a, b, mode, out_dtype, name, **kw)
        host = hosts[name](g)
        out, res = _mm(a, b, mode, out_dtype, name, host=host, **kw)
        carried[name] = (host.ins, res)
        return out

    w_in = wts["w_in"]
    wax_d = jnp.concatenate([_block_diag(small["rg_wa"]), _block_diag(small["rg_wx"])], axis=1).astype(BF16)
    ba, bx, lam = small["rg_ba"], small["rg_bx"], small["rg_lambda"]
    lbl, gn = small["hg_lb_logits"], small["hg_norm_g"]
    conv_w, conv_b = small["conv_w"], small["conv_b"]

    tmm = _pick(t, (768, 256))
    proj, hn1 = mm(h0, w_in, "nn", BF16, "mm_proj", pro=_norm_pro(small["norm1_g"]))
    if finalize is not None:
        wts = finalize(carried["mm_proj"][1])
    w_pa, w_pb, w_out, w_fi, w_fd = (wts[k] for k in ("w_proj_a", "w_proj_b", "w_out", "w_ffn_in", "w_ffn_down"))
    xcb = _conv_fwd(proj, conv_w, conv_b)
    gri = mm(xcb, wax_d, "nn", F32, "mm_rg_gates", sparse=_bd_gates(t))
    hrg, ya = _rg_fwd(gri, xcb, proj, ba, bx, lam)
    yb, o, states = _hgrn_fwd(proj, lbl, gn)
    pa = mm(ya, w_pa, "nn", BF16, "mm_pa")
    pb = mm(yb, w_pb, "nn", BF16, "mm_pb")
    h1, merged = mm(pa, w_out, "nn", F32, "mm_out", resid=h0, pro=_merge_pro(pb, proj, tmm), tiles=(tmm, D))
    (gu, act), hn2 = mm(h1, w_fi, "nn", BF16, "mm_ffn_in", pro=_norm_pro(small["norm2_g"]), epi=_swiglu_epi(t),
                        tiles=(FM, 2 * FH))
    dh2, dh2b, loss, g["norm_f_g"] = mm(act, w_fd, "nn", F32, "mm_ffn_down", resid=h1,
                                        epi=_final_loss_epi(small["norm_f_g"], target, t), tiles=(TM, D))

    g["w_ffn_down"] = mm(act, dh2b, "tn", BF16, "mm_d_wfd")
    (dgu,) = mm(dh2b, w_fd, "nt", BF16, "mm_d_act", epi=_swiglu_bwd_epi(gu), tiles=(FM, FH))
    g["w_ffn_in"] = mm(hn2, dgu, "tn", BF16, "mm_d_wfi")
    dh1, dh1b, g["norm2_g"] = mm(dgu, w_fi, "nt", BF16, "mm_d_hn2", epi=_rms_bwd_epi(h1, small["norm2_g"], dh2, tmm),
                                 tiles=(tmm, D))
    g["w_out"] = mm(merged, dh1b, "tn", BF16, "mm_d_wout")
    dpa, dpb, dproj = mm(dh1b, w_out, "nt", BF16, "mm_d_merged", epi=_merge_bwd_epi(pa, pb, proj, tmm), tiles=(tmm, D))
    dproj = _zero_pad_cols(dproj)
    g["w_proj_a"] = mm(ya, dpa, "tn", BF16, "mm_d_wpa")
    g["w_proj_b"] = mm(yb, dpb, "tn", BF16, "mm_d_wpb")
    dya = mm(dpa, w_pa, "nt", BF16, "mm_d_ya")
    dyb = mm(dpb, w_pb, "nt", BF16, "mm_d_yb")
    dproj, g["hg_norm_g"], g["hg_lb_logits"] = _hgrn_bwd(dyb, proj, o, states, lbl, gn, dproj)
    dgri, dxc, dproj, g["rg_ba"], g["rg_bx"], g["rg_lambda"] = _rg_bwd(dya, hrg, gri, xcb, proj, ba, bx, lam, dproj)
    dwax = mm(xcb, dgri, "tn", F32, "mm_d_wax", sparse=_bd_dw(t))
    g["rg_wa"], g["rg_wx"] = _diag_blocks(dwax[:, :RGW]), _diag_blocks(dwax[:, RGW:])
    dxc = mm(dgri, wax_d, "nt", BF16, "mm_d_xc", resid=dxc)
    dproj, g["conv_w"], g["conv_b"] = _conv_bwd(dxc, proj, conv_w, dproj)
    g["w_in"] = mm(hn1, dproj, "tn", BF16, "mm_d_win")
    dhn1 = mm(dproj, w_in, "nt", BF16, "mm_d_hn1")
    grad_x, g["meta_tokens"], g["norm1_g"] = _rmsnorm1_bwd(dhn1, h0, small["norm1_g"], dh1)
    return loss, grad_x, g, carried


def _adamw_math(w, g, m, v):
    m = B1 * m + (1.0 - B1) * g
    v = B2 * v + (1.0 - B2) * (g * g)
    m_hat = m / (1.0 - B1 ** STEP)
    v_hat = v / (1.0 - B2 ** STEP)
    return -LR * (m_hat / (jnp.sqrt(v_hat) + ADAM_EPS) + WD * w), m, v


def _adamw(w, g, m, v, name):
    r, c = w.shape
    tr = _pick(r, (256, 352, 320, 128, 64, 32, 16, 8))

    def body(w_ref, g_ref, m_ref, v_ref, d_ref, mo_ref, vo_ref):
        d_ref[...], mo_ref[...], vo_ref[...] = _adamw_math(w_ref[...], g_ref[...], m_ref[...], v_ref[...])

    blk = pl.BlockSpec((tr, c), lambda i: (i, 0))
    return pl.pallas_call(
        body, name=name, grid=(r // tr,), in_specs=[blk] * 4, out_specs=[blk] * 3,
        out_shape=[jax.ShapeDtypeStruct((r, c), F32)] * 3,
        compiler_params=_cparams(("parallel",)),
    )(w, g, m, v)


def _adamw_rows(recv, params, extras, name):
    _, _, width = recv.shape
    n_par = len(params)

    def body(*refs):
        r_ref = refs[0]
        ins = refs[1:1 + 3 * n_par]
        outs = refs[1 + 3 * n_par:]
        g_all = r_ref[0]
        for p in range(1, 8):
            g_all = g_all + r_ref[p]
        for q, (row0, w, _, _) in enumerate(params):
            g = g_all[row0:row0 + w.shape[0], :]
            w_ref, m_ref, v_ref = ins[3 * q:3 * q + 3]
            g_ref, d_ref, mo_ref, vo_ref = outs[4 * q:4 * q + 4]
            g_ref[...] = g
            d_ref[...], mo_ref[...], vo_ref[...] = _adamw_math(w_ref[...], g, m_ref[...], v_ref[...])
        for q, (row0, n) in enumerate(extras):
            outs[4 * n_par + q][...] = g_all[row0:row0 + n, :]

    shapes = [jax.ShapeDtypeStruct(w.shape, F32) for _, w, _, _ in params for _ in range(4)]
    shapes += [jax.ShapeDtypeStruct((n, width), F32) for _, n in extras]
    vm = pl.BlockSpec(memory_space=pltpu.VMEM)
    outs = pl.pallas_call(
        body, name=name, in_specs=[vm] * (1 + 3 * n_par), out_specs=[vm] * len(shapes), out_shape=shapes,
        compiler_params=pltpu.CompilerParams(vmem_limit_bytes=VMEM_LIMIT),
    )(recv, *[a for _, w, m, v in params for a in (w, m, v)])
    return [tuple(outs[4 * q:4 * q + 4]) for q in range(n_par)], list(outs[4 * n_par:])


def _col_block(weight, px, py):
    return 2 * py + px if weight == "w_ffn_in" else 2 * px + py


def _sum_place(recv, gfull, weight, xyc, name):
    _, r, c = recv.shape
    tr = _pick(r, (128, 176, 160, 64, 32, 16))
    nb = r // tr

    def body(_, own_ref, r_ref, o_ref):
        g = own_ref[...].astype(F32)
        for p in range(7):
            g = g + r_ref[p].astype(F32)
        o_ref[...] = g

    if weight in COL_SHARDED:
        own = pl.BlockSpec((tr, c), lambda i, s: (s[2] * nb + i, _col_block(weight, s[0], s[1])))
    else:
        own = pl.BlockSpec((tr, c), lambda i, s: ((4 * s[0] + 2 * s[1] + s[2]) * nb + i, 0))
    return pl.pallas_call(
        body, name=name,
        grid_spec=pltpu.PrefetchScalarGridSpec(
            num_scalar_prefetch=1, grid=(nb,),
            in_specs=[own, pl.BlockSpec((7, tr, c), lambda i, s: (0, i, 0))],
            out_specs=pl.BlockSpec((tr, c), lambda i, s: (s[2] * nb + i, 0))),
        out_shape=jax.ShapeDtypeStruct((2 * r, c), F32),
        compiler_params=_cparams(("arbitrary",)),
    )(xyc, gfull, recv)


def _place_own(shard, slot, dtype, name):
    r, c = shard.shape
    tr = _pick(r, (256, 352, 320, 128, 32))

    def body(_, x_ref, o_ref):
        o_ref[...] = x_ref[...].astype(dtype)

    return pl.pallas_call(
        body, name=name,
        grid_spec=pltpu.PrefetchScalarGridSpec(
            num_scalar_prefetch=1, grid=(r // tr,),
            in_specs=[pl.BlockSpec((tr, c), lambda i, s: (i, 0))],
            out_specs=pl.BlockSpec((None, tr, c), lambda i, s: (s[0], i, 0))),
        out_shape=jax.ShapeDtypeStruct((4, r, c), dtype),
        compiler_params=_cparams(("arbitrary",)),
    )(slot, shard)


def _place_cols(shard, slot, name):
    r, c = shard.shape
    tr = _pick(r, (256, 128))

    def body(_, x_ref, o_ref):
        o_ref[...] = x_ref[...].astype(BF16)

    return pl.pallas_call(
        body, name=name,
        grid_spec=pltpu.PrefetchScalarGridSpec(
            num_scalar_prefetch=1, grid=(r // tr,),
            in_specs=[pl.BlockSpec((tr, c), lambda i, s: (i, 0))],
            out_specs=pl.BlockSpec((tr, c), lambda i, s: (i, 2 * (s[0] % 2) + s[0] // 2))),
        out_shape=jax.ShapeDtypeStruct((r, 4 * c), BF16),
        compiler_params=_cparams(("arbitrary",)),
    )(slot, shard)


def _place_w_in(shard, slot):
    r, c = shard.shape
    nb, npad = c // 128, COL_PAD // 128
    rot_blocks, in_blocks = ROT // 128, D_IN // 128

    def out_block(j, s):
        nat = s[0] * nb + jnp.minimum(j, nb - 1)
        rot = jnp.where(nat >= rot_blocks, nat - rot_blocks, nat + (D_INP - ROT) // 128)
        return jnp.where(j < nb, rot, (in_blocks - rot_blocks) + (j - nb))

    def body(_, x_ref, o_ref):
        o_ref[...] = jnp.where(pl.program_id(0) < nb, x_ref[...], 0.0).astype(BF16)

    return pl.pallas_call(
        body, name="place_w_in",
        grid_spec=pltpu.PrefetchScalarGridSpec(
            num_scalar_prefetch=1, grid=(nb + npad,),
            in_specs=[pl.BlockSpec((r, 128), lambda j, s: (0, jnp.minimum(j, nb - 1)))],
            out_specs=pl.BlockSpec((r, 128), lambda j, s: (0, out_block(j, s)))),
        out_shape=jax.ShapeDtypeStruct((r, D_INP), BF16),
        compiler_params=_cparams(("arbitrary",)),
    )(slot, shard)


BIG = ("w_in", "w_proj_a", "w_proj_b", "w_out", "w_ffn_in", "w_ffn_down")
COL_SHARDED = ("w_in", "w_ffn_in")


def _coords():
    return lax.axis_index("x"), lax.axis_index("y"), lax.axis_index("c")


W_IN_HEAD = 384


def _shard_views(b, layout, chip, half_of):
    if layout == "slots":
        half = b.shape[1] // 2
        return [b.at[chip, pl.ds(pl.multiple_of(half_of * half, 8), half)]]
    half = b.shape[0] // 2
    rows = pl.ds(pl.multiple_of(half_of * half, 8), half)
    if layout == "cols":
        c = b.shape[1] // 4
        return [b.at[rows, pl.ds(pl.multiple_of((2 * (chip % 2) + chip // 2) * c, 128), c)]]
    c = D_IN // 4
    views = []
    for first, width in ((0, W_IN_HEAD), (W_IN_HEAD, c - W_IN_HEAD)):
        nat = chip * c + first
        rot = jnp.where(nat >= ROT, nat - ROT, nat + D_INP - ROT)
        views.append(b.at[rows, pl.ds(pl.multiple_of(rot, 128), width)])
    return views


def _gather_host(bufs, layouts):
    n = len(bufs)

    def place():
        x, y, c = _coords()
        return x, y, c, [(1 - x, y), (x, 1 - y), (1 - x, 1 - y)]

    def copies(b, sems, w, k, chip, half_of, to):
        return [pltpu.make_async_remote_copy(src_ref=v, dst_ref=v, send_sem=sems[0].at[w, k, p], recv_sem=sems[1].at[w, k, p],
                                             device_id=to, device_id_type=MESH)
                for p, v in enumerate(_shard_views(b, layouts[w], chip, half_of))]

    def start(_, outs, sems):
        x, y, c, chips = place()
        for w, b in enumerate(outs):
            for k, (cx, cy) in enumerate(chips):
                for cp in copies(b, sems, w, k, 2 * x + y, c, (cx, cy, c)):
                    cp.start()

    def mid(_, outs, sems):
        x, y, c, chips = place()
        for w, b in enumerate(outs):
            for k, (cx, cy) in enumerate(chips):
                for cp in copies(b, sems, w, k, 2 * cx + cy, c, (cx, cy, c)):
                    cp.wait_recv()
                for cp in copies(b, sems, w, 3 + k, 2 * cx + cy, c, (x, y, 1 - c)):
                    cp.start()

    def finish(_, outs, sems):
        x, y, c, chips = place()
        for w, b in enumerate(outs):
            for k, (cx, cy) in enumerate(chips):
                for cp in copies(b, sems, w, 3 + k, 2 * cx + cy, 1 - c, (x, y, 1 - c)):
                    cp.wait_recv()
        for w, b in enumerate(outs):
            for k, (cx, cy) in enumerate(chips):
                for cp in copies(b, sems, w, k, 2 * x + y, c, (cx, cy, c)) + copies(b, sems, w, 3 + k, 2 * cx + cy, c, (x, y, 1 - c)):
                    cp.wait_send()

    return _Host(ins=list(bufs), out_shapes=[jax.ShapeDtypeStruct(b.shape, b.dtype) for b in bufs],
                 aliases={i: i for i in range(n)},
                 sems=[pltpu.SemaphoreType.DMA((n, 6, 2)), pltpu.SemaphoreType.DMA((n, 6, 2))],
                 start=start, mid=mid, finish=finish)


def _run_host(host, name):
    n_in, n_out = len(host.ins), len(host.out_shapes)

    def body(*refs):
        ins, outs, sems = refs[:n_in], refs[n_in:n_in + n_out], refs[n_in + n_out:]
        host.start(ins, outs, sems)
        if host.mid is not None:
            host.mid(ins, outs, sems)
        host.finish(ins, outs, sems)

    return pl.pallas_call(
        body, name=name, in_specs=[ANY] * n_in, out_specs=[ANY] * n_out, out_shape=list(host.out_shapes),
        scratch_shapes=list(host.sems), input_output_aliases=dict(host.aliases),
    )(*host.ins)


def _peer(x, y, c, k):
    fx, fy, fc = (k >> 2) & 1, (k >> 1) & 1, k & 1
    return (x ^ fx if fx else x, y ^ fy if fy else y, c ^ fc if fc else c)


def _sub_shape(name, full_shape):
    r, c = full_shape
    return (r // 2, c // 4) if name in COL_SHARDED else (r // 8, c)


def _exchange_host(names, grads):
    n = len(names)
    shapes = [_sub_shape(k, g.shape) for k, g in zip(names, grads)]

    def copy(ins, outs, sems, w, k):
        x, y, c = _coords()
        px, py, pc = _peer(x, y, c, k)
        sr, sc = shapes[w]
        if names[w] in COL_SHARDED:
            col = _col_block(names[w], px, py) * sc
            src = ins[w].at[pl.ds(pl.multiple_of(pc * sr, 16), sr), pl.ds(pl.multiple_of(col, 128), sc)]
        else:
            src = ins[w].at[pl.ds(pl.multiple_of((4 * px + 2 * py + pc) * sr, 16), sr)]
        return pltpu.make_async_remote_copy(
            src_ref=src, dst_ref=outs[w].at[k - 1], send_sem=sems[0].at[w, k - 1], recv_sem=sems[1].at[w, k - 1],
            device_id=(px, py, pc), device_id_type=MESH)

    def start(ins, outs, sems):
        for w in range(n):
            for k in range(1, 8):
                copy(ins, outs, sems, w, k).start()

    def finish(ins, outs, sems):
        for w in range(n):
            for k in range(1, 8):
                copy(ins, outs, sems, w, k).wait_recv()
        for w in range(n):
            for k in range(1, 8):
                copy(ins, outs, sems, w, k).wait_send()

    return _Host(ins=list(grads), out_shapes=[jax.ShapeDtypeStruct((7,) + s, g.dtype) for s, g in zip(shapes, grads)],
                 aliases={}, sems=[pltpu.SemaphoreType.DMA((n, 7)), pltpu.SemaphoreType.DMA((n, 7))],
                 start=start, mid=None, finish=finish)


def _sibling_host(bufs):
    n = len(bufs)

    def copy(outs, sems, w, half_of):
        x, y, c = _coords()
        half = outs[w].shape[0] // 2
        rows = outs[w].at[pl.ds(pl.multiple_of(half_of * half, 8), half)]
        return pltpu.make_async_remote_copy(src_ref=rows, dst_ref=rows, send_sem=sems[0].at[w], recv_sem=sems[1].at[w],
                                            device_id=(x, y, 1 - c), device_id_type=MESH)

    def start(_, outs, sems):
        c = lax.axis_index("c")
        for w in range(n):
            copy(outs, sems, w, c).start()

    def finish(_, outs, sems):
        c = lax.axis_index("c")
        for w in range(n):
            copy(outs, sems, w, 1 - c).wait_recv()
        for w in range(n):
            copy(outs, sems, w, c).wait_send()

    return _Host(ins=list(bufs), out_shapes=[jax.ShapeDtypeStruct(b.shape, b.dtype) for b in bufs],
                 aliases={i: i for i in range(n)},
                 sems=[pltpu.SemaphoreType.DMA((n,)), pltpu.SemaphoreType.DMA((n,))], start=start, mid=None, finish=finish)


def _pack_host(pack):
    def me_of():
        x, y, c = _coords()
        return x, y, c, 4 * x + 2 * y + c

    def copy(ins, outs, sems, k, slot):
        x, y, c, _ = me_of()
        return pltpu.make_async_remote_copy(src_ref=ins[0], dst_ref=outs[0].at[slot], send_sem=sems[0].at[k - 1],
                                            recv_sem=sems[1].at[k - 1], device_id=_peer(x, y, c, k), device_id_type=MESH)

    def start(ins, outs, sems):
        me = me_of()[3]
        pltpu.make_async_copy(ins[0], outs[0].at[me], sems[2]).start()
        for k in range(1, 8):
            copy(ins, outs, sems, k, me).start()

    def finish(ins, outs, sems):
        x, y, c, me = me_of()
        for k in range(1, 8):
            px, py, pc = _peer(x, y, c, k)
            copy(ins, outs, sems, k, 4 * px + 2 * py + pc).wait_recv()
        for k in range(1, 8):
            copy(ins, outs, sems, k, me).wait_send()
        pltpu.make_async_copy(ins[0], outs[0].at[me], sems[2]).wait()

    return _Host(ins=[pack], out_shapes=[jax.ShapeDtypeStruct((8,) + pack.shape, F32)], aliases={},
                 sems=[pltpu.SemaphoreType.DMA((7,)), pltpu.SemaphoreType.DMA((7,)), pltpu.SemaphoreType.DMA],
                 start=start, mid=None, finish=finish)


def _join_hosts(hosts):
    ins, outs, sems, aliases, spans = [], [], [], {}, []
    for h in hosts:
        spans.append((len(ins), len(h.ins), len(outs), len(h.out_shapes), len(sems), len(h.sems)))
        for i_in, i_out in h.aliases.items():
            aliases[len(ins) + i_in] = len(outs) + i_out
        ins, outs, sems = ins + list(h.ins), outs + list(h.out_shapes), sems + list(h.sems)

    def phase(which):
        fns = [getattr(h, which) for h in hosts]
        if all(f is None for f in fns):
            return None

        def run(i_refs, o_refs, s_refs):
            for f, (i0, n_i, o0, n_o, s0, n_s) in zip(fns, spans):
                if f is not None:
                    f(i_refs[i0:i0 + n_i], o_refs[o0:o0 + n_o], s_refs[s0:s0 + n_s])

        return run

    return _Host(ins, outs, aliases, sems, phase("start"), phase("mid"), phase("finish"))


SHARDED_SMALL = ("meta_tokens", "conv_w")

ROWS = {"conv_b": (1, RGW), "rg_ba": (1, RGW), "rg_bx": (1, RGW), "rg_lambda": (1, RGW), "conv_w": (4, RGW),
        "norm2_g": (1, D), "norm_f_g": (1, D), "hg_lb_logits": (2, D),
        "rg_wa": (RGW * RG_BLOCK // 128, 128), "rg_wx": (RGW * RG_BLOCK // 128, 128), "hg_norm_g": (1, 128),
        "norm1_g": (1, D), "meta_tokens": (N_META, D), "loss": (1, D)}
PACKS = {"early_wide": ("conv_b", "rg_ba", "rg_bx", "rg_lambda", "conv_w"),
         "early_model": ("norm2_g", "norm_f_g", "hg_lb_logits"),
         "early_lane": ("rg_wa", "rg_wx", "hg_norm_g"),
         "late": ("norm1_g", "meta_tokens", "loss")}
EARLY = ("early_wide", "early_model", "early_lane")
NO_UPDATE = ("conv_w", "meta_tokens", "loss")


def _pack_rows(vals, names):
    parts, first, row = [], {}, 0
    for k in names:
        first[k] = row
        parts.append(vals[k].reshape(ROWS[k]).astype(F32))
        row += ROWS[k][0]
    if row % 8:
        parts.append(jnp.zeros((-row % 8, ROWS[names[0]][1]), F32))
    return jnp.concatenate(parts, axis=0), first


ORDER = ("meta_tokens", "norm1_g", "w_in", "conv_w", "conv_b", "rg_wa", "rg_ba", "rg_wx", "rg_bx", "rg_lambda",
         "hg_lb_logits", "hg_norm_g", "w_proj_a", "w_proj_b", "w_out", "norm2_g", "w_ffn_in", "w_ffn_down", "norm_f_g")


def kernel(x, meta_tokens, norm1_g, w_in, conv_w, conv_b, rg_wa, rg_ba, rg_wx, rg_bx, rg_lambda, hg_lb_logits, hg_norm_g, w_proj_a, w_proj_b, w_out, norm2_g, w_ffn_in, w_ffn_down, norm_f_g, loss_target, m_meta_tokens, m_norm1_g, m_w_in, m_conv_w, m_conv_b, m_rg_wa, m_rg_ba, m_rg_wx, m_rg_bx, m_rg_lambda, m_hg_lb_logits, m_hg_norm_g, m_w_proj_a, m_w_proj_b, m_w_out, m_norm2_g, m_w_ffn_in, m_w_ffn_down, m_norm_f_g, v_meta_tokens, v_norm1_g, v_w_in, v_conv_w, v_conv_b, v_rg_wa, v_rg_ba, v_rg_wx, v_rg_bx, v_rg_lambda, v_hg_lb_logits, v_hg_norm_g, v_w_proj_a, v_w_proj_b, v_w_out, v_norm2_g, v_w_ffn_in, v_w_ffn_down, v_norm_f_g):
    args = dict(locals())
    w = {k: args[k] for k in ORDER}
    m = {k: args["m_" + k] for k in ORDER}
    v = {k: args["v_" + k] for k in ORDER}
    xi, yi, ci = _coords()
    chip = 2 * xi + yi
    slot = jnp.reshape(chip, (1,)).astype(jnp.int32)
    xyc = jnp.stack([xi, yi, ci]).astype(jnp.int32)

    tiny = jnp.zeros((32, 384), F32)
    tiny = tiny.at[0:N_META, 0:256].set(meta_tokens).at[N_META:N_META + 4, 0:320].set(conv_w[0])
    later = [k for k in BIG if k != "w_in"]
    layouts = {k: "cols" if k in COL_SHARDED else "slots" for k in later}
    slots = {k: _place_cols(w[k][0], slot, "place_" + k) if layouts[k] == "cols"
             else _place_own(w[k][0], slot, BF16, "place_" + k) for k in later}
    w_in_full, tiny_all = _run_host(
        _gather_host([_place_w_in(w["w_in"][0], slot), _place_own(tiny, slot, F32, "place_tiny")], ["w_in", "slots"]),
        "allgather_w_in")

    def full_matrix(k, gth):
        return gth if layouts[k] == "cols" else gth.reshape(-1, gth.shape[2])

    meta_full = jnp.transpose(tiny_all[:, 0:N_META, 0:256], (1, 0, 2)).reshape(N_META, D)
    conv_w_full = jnp.transpose(tiny_all[:, N_META:N_META + 4, 0:320], (1, 0, 2)).reshape(4, RGW)

    small = dict(norm1_g=norm1_g, conv_w=conv_w_full, conv_b=conv_b, rg_wa=rg_wa[0], rg_ba=rg_ba, rg_wx=rg_wx[0],
                 rg_bx=rg_bx, rg_lambda=rg_lambda, hg_lb_logits=hg_lb_logits, hg_norm_g=hg_norm_g,
                 norm2_g=norm2_g, norm_f_g=norm_f_g.reshape(1, D))

    hosts = {
        "mm_proj": lambda g: _gather_host([slots[k] for k in later], [layouts[k] for k in later]),
        "mm_d_act": lambda g: _exchange_host(["w_ffn_down"], [g["w_ffn_down"]]),
        "mm_d_hn2": lambda g: _exchange_host(["w_ffn_in"], [g["w_ffn_in"]]),
        "mm_d_win": lambda g: _join_hosts([
            _exchange_host(["w_out", "w_proj_a", "w_proj_b"], [g["w_out"], g["w_proj_a"], g["w_proj_b"]]),
            *[_pack_host(_pack_rows(g, PACKS[p])[0]) for p in EARLY]]),
        "mm_d_hn1": lambda g: _exchange_host(["w_in"], [_unrotate_g_in(g["w_in"])]),
    }
    h0 = jnp.concatenate([jnp.zeros((PAD, D), F32), meta_full, x[0]], axis=0)
    loss_blk, grad_x, g, carried = _local_step(
        h0, loss_target[0], {"w_in": w_in_full}, small, hosts=hosts,
        finalize=lambda gathered: {k: full_matrix(k, gth) for k, gth in zip(later, gathered)})
    g["loss"] = jnp.broadcast_to(loss_blk[0:1, 0:1], (1, D))

    halves = {}
    sources = {"mm_d_act": ["w_ffn_down"], "mm_d_hn2": ["w_ffn_in"], "mm_d_win": ["w_out", "w_proj_a", "w_proj_b"],
               "mm_d_hn1": ["w_in"]}
    for name, keys in sources.items():
        partials, received = carried[name]
        for k, part, rec in zip(keys, partials, received):
            halves[k] = _sum_place(rec, part, k, xyc, "sum_" + k)
    late, _ = _pack_rows(g, PACKS["late"])
    tail = _run_host(_join_hosts([_sibling_host([halves[k] for k in BIG]), _pack_host(late)]), "swap_and_late_pack")
    g_big = dict(zip(BIG, tail[:len(BIG)]))
    recv_packs = dict(zip(EARLY, carried["mm_d_win"][1][3:]), late=tail[len(BIG)])

    grad, delta, new_m, new_v, summed = {}, {}, {}, {}, {}
    for pack, names in PACKS.items():
        row, params, extras = 0, [], []
        for k in names:
            if k in NO_UPDATE:
                extras.append((row, ROWS[k][0]))
            else:
                params.append((row,) + tuple(d[k].reshape(ROWS[k]) for d in (w, m, v)))
            row += ROWS[k][0]
        updated, rows_only = _adamw_rows(recv_packs[pack], params, extras, "adamw_" + pack)
        for k, res in zip([k for k in names if k not in NO_UPDATE], updated):
            grad[k], delta[k], new_m[k], new_v[k] = (a.reshape(w[k].shape) for a in res)
        summed.update(zip([k for k in names if k in NO_UPDATE], rows_only))
    loss = summed["loss"][0, 0]
    g_shard = {"meta_tokens": lax.dynamic_slice(summed["meta_tokens"], (0, chip * 256), (N_META, 256)),
               "conv_w": lax.dynamic_slice(summed["conv_w"], (0, chip * 320), (4, 320))}
    for k in BIG + SHARDED_SMALL:
        gk = g_big[k] if k in BIG else g_shard[k]
        wk, mk, vk = (d[k].reshape(gk.shape) for d in (w, m, v))
        dk, mk, vk = _adamw(wk, gk, mk, vk, "adamw_" + k)
        grad[k], delta[k], new_m[k], new_v[k] = (a.reshape(w[k].shape) for a in (gk, dk, mk, vk))

    return (loss, grad_x[None], *[grad[k] for k in ORDER], *[delta[k] for k in ORDER],
            *[new_m[k] for k in ORDER], *[new_v[k] for k in ORDER])
```

```python
import functools

import jax
import jax.numpy as jnp
from jax import lax
from jax.experimental import pallas as pl
from jax.experimental.pallas import tpu as pltpu

F32, BF16 = jnp.float32, jnp.bfloat16
D = 1024
N_META = 16
RGW = 1280
RG_BLOCKS, RG_BLOCK = 16, 80
RG_C = 8.0
HEADS, HD = 8, 128
HGW = HEADS * HD
DFF = 2816
D_IN = 2 * RGW + 6 * D
ROT = 2 * RGW
COL_PAD = 256
D_INP = D_IN + COL_PAD
EPS = 1e-6
CH = 64
TM = 256
ROW0 = TM
PAD = ROW0 - N_META
CT = RGW
RS = 16
NCT = RGW // CT
EXP_CLAMP = 80.0
VMEM_LIMIT = 56 * 1024 * 1024

LR, B1, B2, ADAM_EPS, WD, STEP = 0.001, 0.9, 0.999, 1e-08, 0.01, 10
MESH = pl.DeviceIdType.MESH
ANY = pl.BlockSpec(memory_space=pl.ANY)


def _cparams(sem):
    return pltpu.CompilerParams(dimension_semantics=sem, vmem_limit_bytes=VMEM_LIMIT)


def _pick(n, prefs):
    for p in prefs:
        if n % p == 0:
            return p
    return n


def _sig(x):
    return 0.5 * jnp.tanh(0.5 * x) + 0.5


def _dot(a, b, dims):
    return lax.dot_general(a, b, (dims, ((), ())), preferred_element_type=F32)


NN, NT, TN_ = ((1,), (0,)), ((1,), (1,)), ((0,), (0,))


class _Host:
    def __init__(self, ins, out_shapes, aliases, sems, start, mid, finish):
        self.ins, self.out_shapes, self.aliases, self.sems = ins, out_shapes, aliases, sems
        self.start, self.mid, self.finish = start, mid, finish


class _Epi:
    def __init__(self, ins, in_specs, out_shapes, out_specs, fn, sequential=False, split=1):
        self.ins, self.in_specs, self.out_shapes, self.out_specs, self.fn = ins, in_specs, out_shapes, out_specs, fn
        self.sequential = sequential
        self.split = split


class _Pro:
    def __init__(self, ins, in_specs, fn):
        self.ins, self.in_specs, self.fn = ins, in_specs, fn


def _norm_pro(gain):
    def fn(a_ref, ins):
        xh, _ = _rms(a_ref[...])
        return (xh * ins[0][...]).astype(BF16)

    return _Pro([gain], [pl.BlockSpec((1, gain.shape[1]), lambda i, j, k: (0, 0))], fn)


def _mm(a, b, mode, out_dtype, name, resid=None, host=None, epi=None, tiles=None, sparse=None, pro=None):
    if mode == "nn":
        (m, kd), n = a.shape, b.shape[1]
    elif mode == "nt":
        (m, kd), n = a.shape, b.shape[0]
    else:
        (kd, m), n = a.shape, b.shape[1]
    if mode == "tn":
        tm = _pick(m, (1024, 1280, 1408, 640, 512, 256, 128))
        tk = _pick(kd, (1408, 768, 512, 256, 128))
    else:
        tm = _pick(m, (768, 512, 640, 256, 128))
        tk = kd if kd <= 2816 else _pick(kd, (1792, 1408, 1024, 512, 256, 128))
    tn = _pick(n, (1792, 1408, 1280, 1024, 512, 256, 128))
    if tiles is not None:
        tm, tn = tiles
    a_map = (lambda i, j, k: (k, i)) if mode == "tn" else (lambda i, j, k: (i, k))
    b_map = (lambda i, j, k: (j, k)) if mode == "nt" else (lambda i, j, k: (k, j))
    o_map = lambda i, j, k: (i, j)
    if sparse is not None:
        (ni, nj, nk), (tm, tn, tk), (a_map, b_map, o_map) = sparse
    else:
        assert m % tm == 0 and n % tn == 0 and kd % tk == 0, (name, m, n, kd, tm, tn, tk)
        ni, nj, nk = m // tm, n // tn, kd // tk
    dims = {"nn": NN, "nt": NT, "tn": TN_}[mode]

    n_hin = len(host.ins) if host else 0
    n_hout = len(host.out_shapes) if host else 0
    n_res = 0 if resid is None else 1
    n_pro = 0 if pro is None else 1
    n_pin = len(pro.ins) if pro else 0
    n_ein = len(epi.ins) if epi else 0
    n_out = len(epi.out_shapes) if epi else 1
    assert not n_pro or (mode == "nn" and nk == 1)

    def finish(r, r_ref, e_in, o_refs, i, rows=slice(None), first=True):
        if resid is not None:
            r = r + r_ref[rows, :].astype(F32)
        if epi:
            epi.fn(r, e_in, o_refs, i, rows, first)
        else:
            o_refs[0][rows, :] = r.astype(out_dtype)

    def body(*refs):
        a_ref, b_ref = refs[:2]
        r_ref = refs[2] if resid is not None else None
        pos = 2 + n_res
        p_in = refs[pos:pos + n_pin]
        pos += n_pin
        e_in = refs[pos:pos + n_ein]
        pos += n_ein
        h_in = refs[pos:pos + n_hin]
        pos += n_hin
        o_ref = refs[pos:pos + n_out]
        pos += n_out
        hn_ref = refs[pos] if n_pro else None
        pos += n_pro
        h_out = refs[pos:pos + n_hout]
        scratch = refs[pos + n_hout:]
        n_acc = 1 if nk > 1 else 0
        h_sems = scratch[n_acc + n_pro:]
        i, j, k = pl.program_id(0), pl.program_id(1), pl.program_id(2)
        if host:
            @pl.when((i == 0) & (j == 0) & (k == 0))
            def _():
                host.start(h_in, h_out, h_sems)

            if host.mid is not None:
                @pl.when((i == (3 * ni) // 4) & (j == 0) & (k == 0))
                def _():
                    host.mid(h_in, h_out, h_sems)

        if n_pro:
            a_s = scratch[n_acc]

            @pl.when(j == 0)
            def _():
                a_s[...] = pro.fn(a_ref, p_in)
                hn_ref[...] = a_s[...]

            a_val = a_s[...]
        else:
            a_val = a_ref[...]

        if nk == 1 and epi and epi.split > 1 and mode != "tn":
            part = tm // epi.split
            for p in range(epi.split):
                rows = slice(p * part, (p + 1) * part)
                finish(_dot(a_val[rows, :], b_ref[...], dims), r_ref, e_in, o_ref, i, rows, p == 0)
        elif nk == 1:
            finish(_dot(a_val, b_ref[...], dims), r_ref, e_in, o_ref, i)
        else:
            acc = scratch[0]

            @pl.when(k == 0)
            def _():
                acc[...] = jnp.zeros_like(acc)

            acc[...] += _dot(a_val, b_ref[...], dims)

            @pl.when(k == nk - 1)
            def _():
                finish(acc[...], r_ref, e_in, o_ref, i)

        if host:
            @pl.when((i == ni - 1) & (j == nj - 1) & (k == nk - 1))
            def _():
                host.finish(h_in, h_out, h_sems)

    a_spec = pl.BlockSpec((tk, tm) if mode == "tn" else (tm, tk), a_map)
    b_spec = pl.BlockSpec((tn, tk) if mode == "nt" else (tk, tn), b_map)
    o_spec = pl.BlockSpec((tm, tn), o_map)
    in_specs, args = [a_spec, b_spec], [a, b]
    if resid is not None:
        in_specs.append(o_spec)
        args.append(resid)
    if n_pro:
        in_specs += list(pro.in_specs)
        args += list(pro.ins)
    out_shapes, out_specs = [jax.ShapeDtypeStruct((m, n), out_dtype)], [o_spec]
    if epi:
        in_specs += list(epi.in_specs)
        args += list(epi.ins)
        out_shapes, out_specs = list(epi.out_shapes), list(epi.out_specs)
    if n_pro:
        out_shapes.append(jax.ShapeDtypeStruct((m, kd), BF16))
        out_specs.append(pl.BlockSpec((tm, kd), lambda i, j, k: (i, 0)))
    scratch = ([pltpu.VMEM((tm, tn), F32)] if nk > 1 else []) + ([pltpu.VMEM((tm, kd), BF16)] if n_pro else [])
    sequential = host or n_pro or (epi and epi.sequential)
    outs = pl.pallas_call(
        body, name=name, grid=(ni, nj, nk),
        in_specs=in_specs + [ANY] * n_hin, out_specs=out_specs + [ANY] * n_hout,
        out_shape=out_shapes + (list(host.out_shapes) if host else []),
        scratch_shapes=scratch + (list(host.sems) if host else []),
        input_output_aliases=({2 + n_res + n_pin + n_ein + i_in: n_out + n_pro + i_out
                               for i_in, i_out in host.aliases.items()} if host else {}),
        compiler_params=_cparams(("arbitrary",) * 3 if sequential else ("parallel", "parallel", "arbitrary")),
    )(*args, *(host.ins if host else []))
    main = list(outs[:n_out]) if epi else outs[0]
    if n_pro:
        main = (main, outs[n_out])
    return (main, list(outs[n_out + n_pro:])) if host else main


def _rms(x):
    r = lax.rsqrt(jnp.mean(x * x, axis=-1, keepdims=True) + EPS)
    return x * r, r


def _rms_bwd_math(dhn, x, g):
    xh, r = _rms(x)
    dxh = dhn * g
    dx = r * (dxh - xh * jnp.mean(dxh * xh, axis=-1, keepdims=True))
    return dx, jnp.sum(dhn * xh, axis=0, keepdims=True)


def _row_specs(tm):
    return pl.BlockSpec((tm, D), lambda i, j, k: (i, 0)), pl.BlockSpec((1, D), lambda i, j, k: (0, 0))


def _final_loss_epi(gf, target, t, tm):
    parts = tm // TM

    def fn(x, ins, outs, i, rows, first):
        g_ref = ins[0]
        p = rows.start // TM
        t_ref = ins[1 + p]
        dh_ref, dhb_ref, loss_ref, dg_ref = outs

        if first:
            @pl.when(i == 0)
            def _():
                loss_ref[...] = jnp.zeros_like(loss_ref)
                dg_ref[...] = jnp.zeros_like(dg_ref)

        g = g_ref[...]
        xh, _ = _rms(x)
        err = jnp.where(i * parts + p > 0, xh * g - t_ref[...], 0.0)
        loss_ref[...] += 0.5 * jnp.sum(jnp.sum(err * err, axis=-1, keepdims=True) * (1.0 / D))
        dx, dg = _rms_bwd_math(err * (1.0 / D), x, g)
        dh_ref[rows, :] = dx
        dhb_ref[rows, :] = dx.astype(BF16)
        dg_ref[...] += dg

    row, vec = _row_specs(tm)
    t_specs = [pl.BlockSpec((TM, D), lambda i, j, k, p=p: (jnp.maximum(i * parts + p - 1, 0), 0)) for p in range(parts)]
    return _Epi(ins=[gf] + [target] * parts, in_specs=[vec] + t_specs,
                out_shapes=[jax.ShapeDtypeStruct((t, D), F32), jax.ShapeDtypeStruct((t, D), BF16),
                            jax.ShapeDtypeStruct((8, 128), F32), jax.ShapeDtypeStruct((1, D), F32)],
                out_specs=[row, row, pl.BlockSpec((8, 128), lambda i, j, k: (0, 0)), vec], fn=fn, sequential=True,
                split=parts)


def _rms_bwd_epi(h, g, dres, tm):
    t = h.shape[0]

    def fn(dhn, ins, outs, i, rows, first):
        h_ref, g_ref, dres_ref = ins
        dh_ref, dhb_ref, dg_ref = outs

        if first:
            @pl.when(i == 0)
            def _():
                dg_ref[...] = jnp.zeros_like(dg_ref)

        dx, dg = _rms_bwd_math(dhn, h_ref[rows, :], g_ref[...])
        dh = dres_ref[rows, :] + dx
        dh_ref[rows, :] = dh
        dhb_ref[rows, :] = dh.astype(BF16)
        dg_ref[...] += dg

    row, vec = _row_specs(tm)
    return _Epi(ins=[h, g, dres], in_specs=[row, vec, row],
                out_shapes=[jax.ShapeDtypeStruct((t, D), F32), jax.ShapeDtypeStruct((t, D), BF16),
                            jax.ShapeDtypeStruct((1, D), F32)],
                out_specs=[row, row, vec], fn=fn, sequential=True)


def _rmsnorm1_bwd(dhn, h, g, dres):
    t = h.shape[0]

    def body(dhn_ref, h_ref, g_ref, dres_ref, dx_ref, dmeta_ref, dg_ref):
        i = pl.program_id(0)

        @pl.when(i == 0)
        def _():
            dg_ref[...] = jnp.zeros_like(dg_ref)

        dx, dg = _rms_bwd_math(dhn_ref[...].astype(F32), h_ref[...], g_ref[...])
        dh = dres_ref[...] + dx
        dg_ref[...] += dg
        dx_ref[...] = dh

        @pl.when(i == 0)
        def _():
            dmeta_ref[...] = dh[PAD:, :]

    row = pl.BlockSpec((TM, D), lambda i: (i, 0))
    vec = pl.BlockSpec((1, D), lambda i: (0, 0))
    return pl.pallas_call(
        body, name="rmsnorm1_bwd", grid=(t // TM,),
        in_specs=[row, row, vec, row],
        out_specs=[pl.BlockSpec((TM, D), lambda i: (jnp.maximum(i - 1, 0), 0)),
                   pl.BlockSpec((N_META, D), lambda i: (0, 0)), vec],
        out_shape=[jax.ShapeDtypeStruct((t - ROW0, D), F32), jax.ShapeDtypeStruct((N_META, D), F32),
                   jax.ShapeDtypeStruct((1, D), F32)],
        compiler_params=_cparams(("arbitrary",)),
    )(dhn, h, g, dres)


FH = DFF // 2
FM = 384


def _swiglu_epi(t):
    def fn(r, _, outs, i, rows, first):
        g, u = r[:, :FH], r[:, FH:]
        outs[0][rows, :] = r.astype(BF16)
        outs[1][rows, :] = (g * _sig(g) * u).astype(BF16)

    return _Epi(ins=[], in_specs=[],
                out_shapes=[jax.ShapeDtypeStruct((t, 2 * DFF), BF16), jax.ShapeDtypeStruct((t, DFF), BF16)],
                out_specs=[pl.BlockSpec((FM, 2 * FH), lambda i, j, k: (i, j)), pl.BlockSpec((FM, FH), lambda i, j, k: (i, j))],
                fn=fn)


def _swiglu_bwd_epi(gu):
    def fn(d, ins, outs, i, rows, first):
        gu_t = ins[0][rows, :].astype(F32)
        g, u = gu_t[:, :FH], gu_t[:, FH:]
        s = _sig(g)
        outs[0][rows, :FH] = (d * u * (s * (1.0 + g * (1.0 - s)))).astype(BF16)
        outs[0][rows, FH:] = (d * (g * s)).astype(BF16)

    spec = pl.BlockSpec((FM, 2 * FH), lambda i, j, k: (i, j))
    return _Epi(ins=[gu], in_specs=[spec], out_shapes=[jax.ShapeDtypeStruct(gu.shape, BF16)], out_specs=[spec], fn=fn)


COL_GA, COL_GB = 4, 5
COL_AX, COL_AG = (6 * D + COL_PAD) // CT, (6 * D + COL_PAD + RGW) // CT


def _merge_pro(pb, proj, tm):
    def fn(a_ref, ins):
        pa, pb_, ga, gb = (r[...].astype(F32) for r in (a_ref,) + tuple(ins))
        return (_sig(ga) * pa + _sig(gb) * pb_).astype(BF16)

    row, _ = _row_specs(tm)
    return _Pro([pb, proj, proj], [row, pl.BlockSpec((tm, D), lambda i, j, k: (i, COL_GA)),
                                   pl.BlockSpec((tm, D), lambda i, j, k: (i, COL_GB))], fn)


def _merge_bwd_epi(pa, pb, proj, tm):
    t = pa.shape[0]

    def fn(d, ins, outs, i, rows, first):
        pa, pb, ga, gb = (r[rows, :].astype(F32) for r in ins)
        dpa_ref, dpb_ref, dproj_ref = outs
        sa, sb = _sig(ga), _sig(gb)
        dpa_ref[rows, :] = (sa * d).astype(BF16)
        dpb_ref[rows, :] = (sb * d).astype(BF16)
        dproj_ref[rows, :D] = (d * pa * sa * (1.0 - sa)).astype(BF16)
        dproj_ref[rows, D:] = (d * pb * sb * (1.0 - sb)).astype(BF16)

    row, _ = _row_specs(tm)
    return _Epi(ins=[pa, pb, proj, proj],
                in_specs=[row, row, pl.BlockSpec((tm, D), lambda i, j, k: (i, COL_GA)),
                          pl.BlockSpec((tm, D), lambda i, j, k: (i, COL_GB))],
                out_shapes=[jax.ShapeDtypeStruct((t, D), BF16), jax.ShapeDtypeStruct((t, D), BF16),
                            jax.ShapeDtypeStruct((t, D_INP), BF16)],
                out_specs=[row, row, pl.BlockSpec((tm, 2 * D), lambda i, j, k: (i, 2))], fn=fn)


def _zero_pad_cols(dproj):
    t = dproj.shape[0]
    tz = _pick(t, (768, 256))

    def body(_, o_ref):
        o_ref[...] = jnp.zeros_like(o_ref)

    return pl.pallas_call(
        body, name="dproj_pad", grid=(t // tz,), in_specs=[ANY],
        out_specs=pl.BlockSpec((tz, COL_PAD), lambda i: (i, (6 * D) // COL_PAD)),
        out_shape=jax.ShapeDtypeStruct(dproj.shape, BF16), input_output_aliases={0: 0},
        compiler_params=_cparams(("parallel",)),
    )(dproj)


HALO = 16


def _rows_before(cur, before, s):
    n = cur.shape[0]
    return jnp.concatenate([before, cur], axis=0)[n - s:2 * n - s, :]


def _rows_after(cur, after, s):
    n = cur.shape[0]
    return jnp.concatenate([cur, after], axis=0)[s:n + s, :]


def _prev_halo(col0):
    return pl.BlockSpec((HALO, CT), lambda c, i: (jnp.maximum(i * (TM // HALO) - 1, 0), col0 + c))


def _conv_fwd(proj, w, b):
    t = proj.shape[0]

    def body(x_ref, halo_ref, w_ref, b_ref, ob_ref):
        i = pl.program_id(1)
        wv, bv = w_ref[...], b_ref[...]
        first = jnp.where(i > 0, halo_ref[...].astype(F32), 0.0)

        def strip(k, before):
            rows = pl.ds(pl.multiple_of(k * HALO, HALO), HALO)
            x = x_ref[rows, :].astype(F32)
            y = bv + wv[3:4, :] * x
            for j in range(3):
                y = y + wv[j:j + 1, :] * _rows_before(x, before, 3 - j)
            ob_ref[rows, :] = y.astype(BF16)
            return x

        lax.fori_loop(0, TM // HALO, strip, first)

    blk = pl.BlockSpec((TM, CT), lambda c, i: (i, c))
    return pl.pallas_call(
        body, name="conv_fwd", grid=(NCT, t // TM),
        in_specs=[pl.BlockSpec((TM, CT), lambda c, i: (i, COL_AX + c)), _prev_halo(COL_AX),
                  pl.BlockSpec((4, CT), lambda c, i: (0, c)), pl.BlockSpec((1, CT), lambda c, i: (0, c))],
        out_specs=blk,
        out_shape=jax.ShapeDtypeStruct((t, RGW), BF16),
        compiler_params=_cparams(("parallel", "parallel")),
    )(proj, proj, w, b)


def _conv_bwd(dxc, proj, w, dproj):
    t = proj.shape[0]
    nt = t // TM

    def body(d_ref, dn_ref, x_ref, halo_ref, w_ref, _, dx_ref, dw_ref, db_ref, acc):
        i = pl.program_id(1)

        @pl.when(i == 0)
        def _():
            dw_ref[...] = jnp.zeros_like(dw_ref)
            db_ref[...] = jnp.zeros_like(db_ref)

        acc[...] = jnp.zeros_like(acc)
        wv = w_ref[...]
        first = jnp.where(i > 0, halo_ref[...].astype(F32), 0.0)
        last = jnp.where(i < nt - 1, dn_ref[...].astype(F32), 0.0)
        ns = TM // HALO

        def fold(v):
            return v[0:8, :] + v[8:16, :]

        def strip(k, before):
            off = pl.multiple_of(k * HALO, HALO)
            rows = pl.ds(off, HALO)
            x, d = x_ref[rows, :].astype(F32), d_ref[rows, :].astype(F32)
            nxt = d_ref[pl.ds(pl.multiple_of(jnp.minimum(off + HALO, TM - HALO), HALO), HALO), :].astype(F32)
            after = jnp.where(k == ns - 1, last, nxt)
            dx = wv[3:4, :] * d
            for j in range(3):
                dx = dx + wv[j:j + 1, :] * _rows_after(d, after, 3 - j)
                acc[j] += fold(d * _rows_before(x, before, 3 - j))
            acc[3] += fold(d * x)
            acc[4] += fold(d)
            dx_ref[rows, :] = dx.astype(BF16)
            return x

        lax.fori_loop(0, ns, strip, first)
        for j in range(4):
            dw_ref[j:j + 1, :] += jnp.sum(acc[j], axis=0, keepdims=True)
        db_ref[...] += jnp.sum(acc[4], axis=0, keepdims=True)

    return pl.pallas_call(
        body, name="conv_bwd", grid=(NCT, nt),
        in_specs=[pl.BlockSpec((TM, CT), lambda c, i: (i, c)),
                  pl.BlockSpec((HALO, CT), lambda c, i: (jnp.minimum((i + 1) * (TM // HALO), t // HALO - 1), c)),
                  pl.BlockSpec((TM, CT), lambda c, i: (i, COL_AX + c)), _prev_halo(COL_AX),
                  pl.BlockSpec((4, CT), lambda c, i: (0, c)), ANY],
        out_specs=[pl.BlockSpec((TM, CT), lambda c, i: (i, COL_AX + c)),
                   pl.BlockSpec((4, CT), lambda c, i: (0, c)), pl.BlockSpec((1, CT), lambda c, i: (0, c))],
        out_shape=[jax.ShapeDtypeStruct((t, D_INP), BF16), jax.ShapeDtypeStruct((4, RGW), F32),
                   jax.ShapeDtypeStruct((1, RGW), F32)],
        scratch_shapes=[pltpu.VMEM((5, 8, CT), F32)],
        input_output_aliases={5: 0},
        compiler_params=_cparams(("parallel", "arbitrary")),
    )(dxc, dxc, proj, proj, w, dproj)


def _gelu(x):
    c = 0.7978845608028654
    th = jnp.tanh(c * (x + 0.044715 * x * x * x))
    return 0.5 * x * (1.0 + th), th


def _rg_gates(gr, gi, xc, ba, bx, lam, row0):
    r = _sig(gr + ba)
    ig = _sig(gi + bx)
    sp = jax.nn.softplus(-lam)
    a = jnp.exp(-RG_C * r * sp)
    s2 = jnp.maximum(1.0 - a * a, 1e-30)
    rs = lax.rsqrt(s2)
    s = s2 * rs
    rows = row0 + lax.broadcasted_iota(jnp.int32, gr.shape, 0)
    live = rows >= PAD
    u = jnp.where(live, s * ig * xc, 0.0)
    return r, ig, sp, a, s, rs, u, live


def _rg_fwd(gri, xc, proj, ba, bx, lam):
    t = gri.shape[0]

    def body(gr_ref, gi_ref, xc_ref, ag_ref, ba_ref, bx_ref, lam_ref, h_ref, ya_ref, hc):
        i = pl.program_id(1)

        @pl.when(i == 0)
        def _():
            hc[...] = jnp.zeros_like(hc)

        ba, bx, lam = ba_ref[...], bx_ref[...], lam_ref[...]
        sub = lax.broadcasted_iota(jnp.int32, (8, CT), 0)

        def strip(k, h):
            off = pl.multiple_of(k * RS, RS)
            rows = pl.ds(off, RS)
            _, _, _, a, _, _, u, _ = _rg_gates(gr_ref[rows, :], gi_ref[rows, :], xc_ref[rows, :].astype(F32), ba, bx, lam,
                                               i * TM + off)
            outs = []
            for half in range(RS // 8):
                out = jnp.zeros((8, CT), F32)
                for r in range(8):
                    h = a[8 * half + r:8 * half + r + 1, :] * h + u[8 * half + r:8 * half + r + 1, :]
                    out = jnp.where(sub == r, h, out)
                outs.append(out)
            hs = jnp.concatenate(outs, axis=0)
            h_ref[rows, :] = hs
            ge, _ = _gelu(ag_ref[rows, :].astype(F32))
            ya_ref[rows, :] = (hs * ge).astype(BF16)
            return h

        hc[...] = lax.fori_loop(0, TM // RS, strip, hc[...])

    blk_ = pl.BlockSpec((TM, CT), lambda c, i: (i, c))
    vec = pl.BlockSpec((1, CT), lambda c, i: (0, c))
    return pl.pallas_call(
        body, name="rg_fwd", grid=(NCT, t // TM),
        in_specs=[blk_, pl.BlockSpec((TM, CT), lambda c, i: (i, NCT + c)), blk_,
                  pl.BlockSpec((TM, CT), lambda c, i: (i, COL_AG + c)), vec, vec, vec],
        out_specs=[blk_, blk_],
        out_shape=[jax.ShapeDtypeStruct((t, RGW), F32), jax.ShapeDtypeStruct((t, RGW), BF16)],
        scratch_shapes=[pltpu.VMEM((1, CT), F32)],
        compiler_params=_cparams(("parallel", "arbitrary")),
    )(gri, gri, xc, proj, ba, bx, lam)


def _rg_bwd(dya, h, gri, xc, proj, ba, bx, lam, dproj):
    t = gri.shape[0]
    nt = t // TM
    assert NCT == 1

    def body(dya_ref, h_ref, hh_ref, gr_ref, gi_ref, xc_ref, ag_ref, ba_ref, bx_ref, lam_ref, _,
             dgri_ref, dxc_ref, dag_ref, dba_ref, dbx_ref, dlam_ref, hbuf, acc, cc):
        i = pl.program_id(1)
        ri = nt - 1 - i

        @pl.when(i == 0)
        def _():
            cc[...] = jnp.zeros_like(cc)
            dba_ref[...] = jnp.zeros_like(dba_ref)
            dbx_ref[...] = jnp.zeros_like(dbx_ref)
            dlam_ref[...] = jnp.zeros_like(dlam_ref)

        acc[...] = jnp.zeros_like(acc)
        ba, bx, lam = ba_ref[...], bx_ref[...], lam_ref[...]
        halo_last = jnp.where(ri > 0, hh_ref[7:8, :], 0.0)
        sub = lax.broadcasted_iota(jnp.int32, (8, CT), 0)
        c0 = 0.7978845608028654

        def strip(kk, c):
            k = TM // RS - 1 - kk
            off = pl.multiple_of(k * RS, RS)
            rows = pl.ds(off, RS)
            xc, hv = xc_ref[rows, :].astype(F32), h_ref[rows, :]
            ag, dya = ag_ref[rows, :].astype(F32), dya_ref[rows, :].astype(F32)
            r, ig, sp, a, s, rs, _, live = _rg_gates(gr_ref[rows, :], gi_ref[rows, :], xc, ba, bx, lam, ri * TM + off)
            ge, th = _gelu(ag)
            dge = 0.5 * (1.0 + th) + 0.5 * ag * (1.0 - th * th) * c0 * (1.0 + 3.0 * 0.044715 * ag * ag)
            dag_ref[rows, :] = (dya * hv * dge).astype(BF16)
            d = dya * ge
            outs = []
            for half in range(RS // 8 - 1, -1, -1):
                out = jnp.zeros((8, CT), F32)
                for rr in range(7, -1, -1):
                    g = d[8 * half + rr:8 * half + rr + 1, :] + c
                    c = a[8 * half + rr:8 * half + rr + 1, :] * g
                    out = jnp.where(sub == rr, g, out)
                outs.insert(0, out)
            g = jnp.concatenate(outs, axis=0)
            before = h_ref[pl.ds(pl.multiple_of(jnp.maximum(off - RS, 0), RS), RS), :]
            hbuf[7:8, :] = jnp.where(k == 0, halo_last, before[RS - 1:RS, :])
            hbuf[8:, :] = hv
            hprev = hbuf[pl.ds(7, RS), :]
            du = jnp.where(live, g, 0.0)
            ds = du * (ig * xc)
            dm = du * s
            dla = (g * hprev - a * ds * rs) * a
            dr = dla * (-RG_C * sp) * r * (1.0 - r)
            di = dm * xc * ig * (1.0 - ig)
            dgri_ref[rows, :CT] = dr.astype(BF16)
            dgri_ref[rows, CT:] = di.astype(BF16)
            dxc_ref[rows, :] = (dm * ig).astype(BF16)
            dl = dla * (-RG_C * r)
            for half in range(RS // 8):
                part = slice(8 * half, 8 * half + 8)
                acc[0] += dr[part]
                acc[1] += di[part]
                acc[2] += dl[part]
            return c

        cc[...] = lax.fori_loop(0, TM // RS, strip, cc[...])
        dba_ref[...] += jnp.sum(acc[0], axis=0, keepdims=True)
        dbx_ref[...] += jnp.sum(acc[1], axis=0, keepdims=True)
        dlam_ref[...] += jnp.sum(acc[2], axis=0, keepdims=True) * (-_sig(-lam))

    rblk = pl.BlockSpec((TM, CT), lambda c, i: (nt - 1 - i, c))
    vec = pl.BlockSpec((1, CT), lambda c, i: (0, c))
    hh = pl.BlockSpec((8, CT), lambda c, i: (jnp.maximum((nt - 1 - i) * (TM // 8) - 1, 0), c))
    agb = pl.BlockSpec((TM, CT), lambda c, i: (nt - 1 - i, COL_AG + c))
    return pl.pallas_call(
        body, name="rg_bwd", grid=(NCT, nt),
        in_specs=[rblk, rblk, hh, rblk, pl.BlockSpec((TM, CT), lambda c, i: (nt - 1 - i, NCT + c)), rblk, agb,
                  vec, vec, vec, ANY],
        out_specs=[pl.BlockSpec((TM, 2 * CT), lambda c, i: (nt - 1 - i, c)), rblk, agb, vec, vec, vec],
        out_shape=[jax.ShapeDtypeStruct((t, 2 * RGW), BF16),
                   jax.ShapeDtypeStruct((t, RGW), BF16), jax.ShapeDtypeStruct((t, D_INP), BF16),
                   jax.ShapeDtypeStruct((1, RGW), F32), jax.ShapeDtypeStruct((1, RGW), F32),
                   jax.ShapeDtypeStruct((1, RGW), F32)],
        scratch_shapes=[pltpu.VMEM((RS + 8, CT), F32), pltpu.VMEM((3, 8, CT), F32), pltpu.VMEM((1, CT), F32)],
        input_output_aliases={10: 2},
        compiler_params=_cparams(("parallel", "arbitrary")),
    )(dya, h, h, gri, gri, xc, proj, ba, bx, lam, dproj)


NCH = TM // CH

def _tri_dot(tri, x):
    hi = x.astype(BF16)
    r1 = x - hi.astype(F32)
    mid = r1.astype(BF16)
    lo = (r1 - mid.astype(F32)).astype(BF16)
    return _dot(tri, hi, NN) + _dot(tri, mid, NN) + _dot(tri, lo, NN)


def _hg_chunk(qr, fr, lb):
    sf = _sig(fr)
    fg = lb + (1.0 - lb) * sf
    k = (1.0 - lb) * (1.0 - sf)
    sq = _sig(qr)
    q = qr * sq
    ri = lax.broadcasted_iota(jnp.int32, (CH, CH), 0)
    ci = lax.broadcasted_iota(jnp.int32, (CH, CH), 1)
    b = _tri_dot((ri >= ci).astype(BF16), jnp.log(fg))
    bm, bl = b[CH // 2 - 1:CH // 2, :], b[CH - 1:CH, :]
    ebm = jnp.exp(jnp.minimum(b - bm, EXP_CLAMP))
    ekm = jnp.exp(jnp.minimum(bm - b, EXP_CLAMP))
    eb = ebm * jnp.exp(bm)
    ekl = ekm * jnp.exp(bl - bm)
    return dict(sf=sf, fg=fg, k=k, sq=sq, q=q, eb=eb, ebm=ebm, ekm=ekm, ekl=ekl, ebl=jnp.exp(bl),
                qe=q * eb, qh=q * ebm, kh=k * ekm, kd=k * ekl, causal=ri >= ci, upper=(ci >= ri).astype(BF16))


def _hgrn_fwd(proj, lbl, gn):
    t = proj.shape[0]
    nt = t // TM

    def body(q_ref, f_ref, v_ref, g_ref, lbl_ref, gn_ref, yb_ref, o_ref, st_ref, st):
        @pl.when(pl.program_id(0) == 0)
        def _():
            st[...] = jnp.zeros_like(st)

        l = lbl_ref[...]
        lb = _sig(l[0:1, :] - l[1:2, :])
        gnv = gn_ref[...]

        def chunk(c, carry):
            off = pl.multiple_of(c * CH, CH)
            rows = pl.ds(off, CH)
            z = _hg_chunk(q_ref[rows, :].astype(F32), f_ref[rows, :].astype(F32), lb)
            v, gg = v_ref[rows, :], g_ref[rows, :].astype(F32)
            for hh in range(HEADS):
                sl = slice(hh * HD, (hh + 1) * HD)
                s_prev = st[hh]
                st_ref[c, hh] = s_prev
                vb = v[:, sl].astype(BF16)
                att = jnp.where(z["causal"], _dot(z["qh"][:, sl].astype(BF16), z["kh"][:, sl].astype(BF16), NT), 0.0)
                o = _dot(z["qe"][:, sl].astype(BF16), s_prev.astype(BF16), NT) + _dot(att.astype(BF16), vb, NN)
                st[hh] = s_prev * z["ebl"][:, sl] + _dot(vb, z["kd"][:, sl].astype(BF16), TN_)
                xh, _ = _rms(o)
                gh = gg[:, sl]
                o_ref[rows, sl] = o
                yb_ref[rows, sl] = (xh * gnv * gh * _sig(gh)).astype(BF16)
            return carry

        lax.fori_loop(0, NCH, chunk, 0, unroll=True)

    def col(j):
        return pl.BlockSpec((TM, HGW), lambda i, j=j: (i, j))

    return pl.pallas_call(
        body, name="hgrn_fwd", grid=(nt,),
        in_specs=[col(0), col(1), col(2), col(3), pl.BlockSpec((2, HGW), lambda i: (0, 0)),
                  pl.BlockSpec((1, HD), lambda i: (0, 0))],
        out_specs=[col(0), col(0), pl.BlockSpec((NCH, HEADS, HD, HD), lambda i: (i, 0, 0, 0))],
        out_shape=[jax.ShapeDtypeStruct((t, HGW), BF16), jax.ShapeDtypeStruct((t, HGW), F32),
                   jax.ShapeDtypeStruct((t // CH, HEADS, HD, HD), F32)],
        scratch_shapes=[pltpu.VMEM((HEADS, HD, HD), F32)],
        compiler_params=_cparams(("arbitrary",)),
    )(proj, proj, proj, proj, lbl, gn)


def _hgrn_bwd(dyb, proj, o, states, lbl, gn, dproj):
    t = proj.shape[0]
    nt = t // TM

    def body(dy_ref, q_ref, f_ref, v_ref, g_ref, o_ref, st_ref, lbl_ref, gn_ref, _,
             dp_ref, dgn_ref, dl_ref, dst, dlb):
        i = pl.program_id(0)

        @pl.when(i == 0)
        def _():
            dst[...] = jnp.zeros_like(dst)
            dlb[...] = jnp.zeros_like(dlb)
            dgn_ref[...] = jnp.zeros_like(dgn_ref)

        l = lbl_ref[...]
        lb = _sig(l[0:1, :] - l[1:2, :])
        gnv = gn_ref[...]
        last = lax.broadcasted_iota(jnp.int32, (CH, HD), 0) == CH - 1

        def chunk(cc, carry):
            c = NCH - 1 - cc
            off = pl.multiple_of(c * CH, CH)
            rows = pl.ds(off, CH)
            qr, fr = q_ref[rows, :].astype(F32), f_ref[rows, :].astype(F32)
            z = _hg_chunk(qr, fr, lb)
            v, gg, ov, dy = v_ref[rows, :], g_ref[rows, :].astype(F32), o_ref[rows, :], dy_ref[rows, :].astype(F32)
            dqs, dks, dbs, dvs, dgs = [], [], [], [], []
            dgn = jnp.zeros((1, HD), F32)
            for hh in range(HEADS):
                sl = slice(hh * HD, (hh + 1) * HD)
                s_prev, ds_new = st_ref[c, hh], dst[hh]
                qe, qh, kh, kd = z["qe"][:, sl], z["qh"][:, sl], z["kh"][:, sl], z["kd"][:, sl]
                ebl = z["ebl"][:, sl]
                gh, dyh = gg[:, sl], dy[:, sl]
                xh, rr = _rms(ov[:, sl])
                sg = _sig(gh)
                dyn = dyh * (gh * sg)
                dgs.append(dyh * (xh * gnv) * (sg * (1.0 + gh * (1.0 - sg))))
                dgn = dgn + jnp.sum(dyn * xh, axis=0, keepdims=True)
                dxh = dyn * gnv
                do = (rr * (dxh - xh * jnp.mean(dxh * xh, axis=-1, keepdims=True))).astype(BF16)
                vb, dsb = v[:, sl].astype(BF16), ds_new.astype(BF16)
                qeb, qhb, khb, kdb = (a.astype(BF16) for a in (qe, qh, kh, kd))
                att = jnp.where(z["causal"], _dot(qhb, khb, NT), 0.0).astype(BF16)
                datt = jnp.where(z["causal"], _dot(do, vb, NT), 0.0).astype(BF16)
                dvs.append(_dot(att, do, TN_) + _dot(kdb, dsb, NT))
                dqe = _dot(do, s_prev.astype(BF16), NN)
                dqh = _dot(datt, khb, NN)
                dkh = _dot(datt, qhb, TN_)
                dkd = _dot(vb, dsb, NN)
                qe, qh, kh, kd = (a.astype(F32) for a in (qeb, qhb, khb, kdb))
                dbl = (jnp.sum(dkd * kd, axis=0, keepdims=True)
                       + jnp.sum(ds_new * s_prev, axis=0, keepdims=True) * ebl)
                dqs.append(dqe * z["eb"][:, sl] + dqh * z["ebm"][:, sl])
                dks.append(dkh * z["ekm"][:, sl] + dkd * z["ekl"][:, sl])
                dbs.append(dqe * qe + dqh * qh - dkh * kh - dkd * kd + jnp.where(last, dbl, 0.0))
                dst[hh] = _dot(do, qeb, TN_) + ds_new * ebl
            dgn_ref[...] += dgn
            dq, dk, db = (jnp.concatenate(x, axis=1) for x in (dqs, dks, dbs))
            dlf = _tri_dot(z["upper"], db)
            sf, fg, sq = z["sf"], z["fg"], z["sq"]
            dmix = dlf / fg - dk
            dsf = dmix * (1.0 - lb)
            dlb[...] += jnp.sum(dmix * (1.0 - sf), axis=0, keepdims=True)
            dp_ref[rows, 0:HGW] = (dq * (sq * (1.0 + qr * (1.0 - sq)))).astype(BF16)
            dp_ref[rows, HGW:2 * HGW] = (dsf * sf * (1.0 - sf)).astype(BF16)
            dp_ref[rows, 2 * HGW:3 * HGW] = jnp.concatenate(dvs, axis=1).astype(BF16)
            dp_ref[rows, 3 * HGW:4 * HGW] = jnp.concatenate(dgs, axis=1).astype(BF16)
            return carry

        lax.fori_loop(0, NCH, chunk, 0, unroll=True)
        dl0 = dlb[...] * lb * (1.0 - lb)
        dl_ref[0:1, :] = dl0
        dl_ref[1:2, :] = -dl0

    def col(j):
        return pl.BlockSpec((TM, HGW), lambda i, j=j: (nt - 1 - i, j))

    return pl.pallas_call(
        body, name="hgrn_bwd", grid=(nt,),
        in_specs=[col(0), col(0), col(1), col(2), col(3), col(0),
                  pl.BlockSpec((NCH, HEADS, HD, HD), lambda i: (nt - 1 - i, 0, 0, 0)),
                  pl.BlockSpec((2, HGW), lambda i: (0, 0)), pl.BlockSpec((1, HD), lambda i: (0, 0)), ANY],
        out_specs=[pl.BlockSpec((TM, 4 * HGW), lambda i: (nt - 1 - i, 0)),
                   pl.BlockSpec((1, HD), lambda i: (0, 0)), pl.BlockSpec((2, HGW), lambda i: (0, 0))],
        out_shape=[jax.ShapeDtypeStruct((t, D_INP), BF16), jax.ShapeDtypeStruct((1, HD), F32),
                   jax.ShapeDtypeStruct((2, HGW), F32)],
        scratch_shapes=[pltpu.VMEM((HEADS, HD, HD), F32), pltpu.VMEM((1, HGW), F32)],
        input_output_aliases={9: 0},
        compiler_params=_cparams(("arbitrary",)),
    )(dyb, proj, proj, proj, proj, o, states, lbl, gn, dproj)


def _rotate_w_in(w):
    return jnp.concatenate([w[:, ROT:], jnp.zeros((w.shape[0], COL_PAD), w.dtype), w[:, :ROT]], axis=1)


def _unrotate_g_in(g):
    return jnp.concatenate([g[:, D_INP - ROT:], g[:, :D_IN - ROT]], axis=1)


SB = RGW // 2


def _bd_gates(t):
    tm = _pick(t, (768, 256))

    def col(j):
        return 2 * (j % 2) + j // 2

    return ((t // tm, 4, 1), (tm, SB, SB),
            (lambda i, j, k: (i, j // 2), lambda i, j, k: (j // 2, col(j)), lambda i, j, k: (i, col(j))))


def _bd_dw(t):
    tk = _pick(t, (1408, 768))
    return (1, 4, t // tk), (SB, SB, tk), (lambda i, j, k: (k, j % 2), lambda i, j, k: (k, j), lambda i, j, k: (j % 2, j))


def _block_diag(w):
    eye = jnp.eye(RG_BLOCKS, dtype=w.dtype)
    return (w[:, :, None, :] * eye[:, None, :, None]).reshape(RGW, RGW)


def _diag_blocks(wd):
    w4 = wd.reshape(RG_BLOCKS, RG_BLOCK, RG_BLOCKS, RG_BLOCK)
    return jnp.stack([w4[n, :, n, :] for n in range(RG_BLOCKS)])


def _local_step(h0, target, wts, small, hosts=None, finalize=None):
    hosts = hosts or {}
    carried = {}
    g = {}
    t = h0.shape[0]

    def mm(a, b, mode, out_dtype, name, **kw):
        if name not in hosts:
            return _mm(a, b, mode, out_dtype, name, **kw)
        host = hosts[name](g)
        out, res = _mm(a, b, mode, out_dtype, name, host=host, **kw)
        carried[name] = (host.ins, res)
        return out

    w_in = wts["w_in"]
    wax_d = jnp.concatenate([_block_diag(small["rg_wa"]), _block_diag(small["rg_wx"])], axis=1).astype(BF16)
    ba, bx, lam = small["rg_ba"], small["rg_bx"], small["rg_lambda"]
    lbl, gn = small["hg_lb_logits"], small["hg_norm_g"]
    conv_w, conv_b = small["conv_w"], small["conv_b"]

    tmm = _pick(t, (768, 256))
    proj, hn1 = mm(h0, w_in, "nn", BF16, "mm_proj", pro=_norm_pro(small["norm1_g"]))
    if finalize is not None:
        wts = finalize(carried["mm_proj"][1])
    w_pa, w_pb, w_out, w_fi, w_fd = (wts[k] for k in ("w_proj_a", "w_proj_b", "w_out", "w_ffn_in", "w_ffn_down"))
    xcb = _conv_fwd(proj, conv_w, conv_b)
    gri = mm(xcb, wax_d, "nn", F32, "mm_rg_gates", sparse=_bd_gates(t))
    hrg, ya = _rg_fwd(gri, xcb, proj, ba, bx, lam)
    yb, o, states = _hgrn_fwd(proj, lbl, gn)
    pa = mm(ya, w_pa, "nn", BF16, "mm_pa")
    pb = mm(yb, w_pb, "nn", BF16, "mm_pb")
    h1, merged = mm(pa, w_out, "nn", F32, "mm_out", resid=h0, pro=_merge_pro(pb, proj, tmm), tiles=(tmm, D))
    (gu, act), hn2 = mm(h1, w_fi, "nn", BF16, "mm_ffn_in", pro=_norm_pro(small["norm2_g"]), epi=_swiglu_epi(t),
                        tiles=(FM, 2 * FH))
    dh2, dh2b, loss, g["norm_f_g"] = mm(act, w_fd, "nn", F32, "mm_ffn_down", resid=h1,
                                        epi=_final_loss_epi(small["norm_f_g"], target, t, tmm), tiles=(tmm, D))

    g["w_ffn_down"] = mm(act, dh2b, "tn", BF16, "mm_d_wfd")
    (dgu,) = mm(dh2b, w_fd, "nt", BF16, "mm_d_act", epi=_swiglu_bwd_epi(gu), tiles=(FM, FH))
    g["w_ffn_in"] = mm(hn2, dgu, "tn", BF16, "mm_d_wfi")
    dh1, dh1b, g["norm2_g"] = mm(dgu, w_fi, "nt", BF16, "mm_d_hn2", epi=_rms_bwd_epi(h1, small["norm2_g"], dh2, tmm),
                                 tiles=(tmm, D))
    g["w_out"] = mm(merged, dh1b, "tn", BF16, "mm_d_wout")
    dpa, dpb, dproj = mm(dh1b, w_out, "nt", BF16, "mm_d_merged", epi=_merge_bwd_epi(pa, pb, proj, tmm), tiles=(tmm, D))
    dproj = _zero_pad_cols(dproj)
    g["w_proj_a"] = mm(ya, dpa, "tn", BF16, "mm_d_wpa")
    g["w_proj_b"] = mm(yb, dpb, "tn", BF16, "mm_d_wpb")
    dya = mm(dpa, w_pa, "nt", BF16, "mm_d_ya")
    dyb = mm(dpb, w_pb, "nt", BF16, "mm_d_yb")
    dproj, g["hg_norm_g"], g["hg_lb_logits"] = _hgrn_bwd(dyb, proj, o, states, lbl, gn, dproj)
    dgri, dxc, dproj, g["rg_ba"], g["rg_bx"], g["rg_lambda"] = _rg_bwd(dya, hrg, gri, xcb, proj, ba, bx, lam, dproj)
    dwax = mm(xcb, dgri, "tn", F32, "mm_d_wax", sparse=_bd_dw(t))
    g["rg_wa"], g["rg_wx"] = _diag_blocks(dwax[:, :RGW]), _diag_blocks(dwax[:, RGW:])
    dxc = mm(dgri, wax_d, "nt", BF16, "mm_d_xc", resid=dxc)
    dproj, g["conv_w"], g["conv_b"] = _conv_bwd(dxc, proj, conv_w, dproj)
    g["w_in"] = mm(hn1, dproj, "tn", BF16, "mm_d_win")
    dhn1 = mm(dproj, w_in, "nt", BF16, "mm_d_hn1")
    grad_x, g["meta_tokens"], g["norm1_g"] = _rmsnorm1_bwd(dhn1, h0, small["norm1_g"], dh1)
    return loss, grad_x, g, carried


def _adamw_math(w, g, m, v):
    m = B1 * m + (1.0 - B1) * g
    v = B2 * v + (1.0 - B2) * (g * g)
    m_hat = m / (1.0 - B1 ** STEP)
    v_hat = v / (1.0 - B2 ** STEP)
    return -LR * (m_hat / (jnp.sqrt(v_hat) + ADAM_EPS) + WD * w), m, v


def _adamw(w, g, m, v, name):
    r, c = w.shape
    tr = _pick(r, (256, 352, 320, 128, 64, 32, 16, 8))

    def body(w_ref, g_ref, m_ref, v_ref, d_ref, mo_ref, vo_ref):
        d_ref[...], mo_ref[...], vo_ref[...] = _adamw_math(w_ref[...], g_ref[...], m_ref[...], v_ref[...])

    blk = pl.BlockSpec((tr, c), lambda i: (i, 0))
    return pl.pallas_call(
        body, name=name, grid=(r // tr,), in_specs=[blk] * 4, out_specs=[blk] * 3,
        out_shape=[jax.ShapeDtypeStruct((r, c), F32)] * 3,
        compiler_params=_cparams(("parallel",)),
    )(w, g, m, v)


def _adamw_rows(recv, params, extras, name):
    _, _, width = recv.shape
    n_par = len(params)

    def body(*refs):
        r_ref = refs[0]
        ins = refs[1:1 + 3 * n_par]
        outs = refs[1 + 3 * n_par:]
        g_all = r_ref[0]
        for p in range(1, 8):
            g_all = g_all + r_ref[p]
        for q, (row0, w, _, _) in enumerate(params):
            g = g_all[row0:row0 + w.shape[0], :]
            w_ref, m_ref, v_ref = ins[3 * q:3 * q + 3]
            g_ref, d_ref, mo_ref, vo_ref = outs[4 * q:4 * q + 4]
            g_ref[...] = g
            d_ref[...], mo_ref[...], vo_ref[...] = _adamw_math(w_ref[...], g, m_ref[...], v_ref[...])
        for q, (row0, n) in enumerate(extras):
            outs[4 * n_par + q][...] = g_all[row0:row0 + n, :]

    shapes = [jax.ShapeDtypeStruct(w.shape, F32) for _, w, _, _ in params for _ in range(4)]
    shapes += [jax.ShapeDtypeStruct((n, width), F32) for _, n in extras]
    vm = pl.BlockSpec(memory_space=pltpu.VMEM)
    outs = pl.pallas_call(
        body, name=name, in_specs=[vm] * (1 + 3 * n_par), out_specs=[vm] * len(shapes), out_shape=shapes,
        compiler_params=pltpu.CompilerParams(vmem_limit_bytes=VMEM_LIMIT),
    )(recv, *[a for _, w, m, v in params for a in (w, m, v)])
    return [tuple(outs[4 * q:4 * q + 4]) for q in range(n_par)], list(outs[4 * n_par:])


def _col_block(weight, px, py):
    return 2 * py + px if weight == "w_ffn_in" else 2 * px + py


def _sum_place(recv, gfull, weight, xyc, name):
    _, r, c = recv.shape
    tr = _pick(r, (128, 176, 160, 64, 32, 16))
    nb = r // tr

    def body(_, own_ref, r_ref, o_ref):
        g = own_ref[...].astype(F32)
        for p in range(7):
            g = g + r_ref[p].astype(F32)
        o_ref[...] = g

    if weight in COL_SHARDED:
        own = pl.BlockSpec((tr, c), lambda i, s: (s[2] * nb + i, _col_block(weight, s[0], s[1])))
    else:
        own = pl.BlockSpec((tr, c), lambda i, s: ((4 * s[0] + 2 * s[1] + s[2]) * nb + i, 0))
    return pl.pallas_call(
        body, name=name,
        grid_spec=pltpu.PrefetchScalarGridSpec(
            num_scalar_prefetch=1, grid=(nb,),
            in_specs=[own, pl.BlockSpec((7, tr, c), lambda i, s: (0, i, 0))],
            out_specs=pl.BlockSpec((tr, c), lambda i, s: (s[2] * nb + i, 0))),
        out_shape=jax.ShapeDtypeStruct((2 * r, c), F32),
        compiler_params=_cparams(("arbitrary",)),
    )(xyc, gfull, recv)


def _place_own(shard, slot, dtype, name):
    r, c = shard.shape
    tr = _pick(r, (256, 352, 320, 128, 32))

    def body(_, x_ref, o_ref):
        o_ref[...] = x_ref[...].astype(dtype)

    return pl.pallas_call(
        body, name=name,
        grid_spec=pltpu.PrefetchScalarGridSpec(
            num_scalar_prefetch=1, grid=(r // tr,),
            in_specs=[pl.BlockSpec((tr, c), lambda i, s: (i, 0))],
            out_specs=pl.BlockSpec((None, tr, c), lambda i, s: (s[0], i, 0))),
        out_shape=jax.ShapeDtypeStruct((4, r, c), dtype),
        compiler_params=_cparams(("arbitrary",)),
    )(slot, shard)


def _place_cols(shard, slot, name):
    r, c = shard.shape
    tr = _pick(r, (256, 128))

    def body(_, x_ref, o_ref):
        o_ref[...] = x_ref[...].astype(BF16)

    return pl.pallas_call(
        body, name=name,
        grid_spec=pltpu.PrefetchScalarGridSpec(
            num_scalar_prefetch=1, grid=(r // tr,),
            in_specs=[pl.BlockSpec((tr, c), lambda i, s: (i, 0))],
            out_specs=pl.BlockSpec((tr, c), lambda i, s: (i, 2 * (s[0] % 2) + s[0] // 2))),
        out_shape=jax.ShapeDtypeStruct((r, 4 * c), BF16),
        compiler_params=_cparams(("arbitrary",)),
    )(slot, shard)


def _place_w_in(shard, slot):
    r, c = shard.shape
    nb, npad = c // 128, COL_PAD // 128
    rot_blocks, in_blocks = ROT // 128, D_IN // 128

    def out_block(j, s):
        nat = s[0] * nb + jnp.minimum(j, nb - 1)
        rot = jnp.where(nat >= rot_blocks, nat - rot_blocks, nat + (D_INP - ROT) // 128)
        return jnp.where(j < nb, rot, (in_blocks - rot_blocks) + (j - nb))

    def body(_, x_ref, o_ref):
        o_ref[...] = jnp.where(pl.program_id(0) < nb, x_ref[...], 0.0).astype(BF16)

    return pl.pallas_call(
        body, name="place_w_in",
        grid_spec=pltpu.PrefetchScalarGridSpec(
            num_scalar_prefetch=1, grid=(nb + npad,),
            in_specs=[pl.BlockSpec((r, 128), lambda j, s: (0, jnp.minimum(j, nb - 1)))],
            out_specs=pl.BlockSpec((r, 128), lambda j, s: (0, out_block(j, s)))),
        out_shape=jax.ShapeDtypeStruct((r, D_INP), BF16),
        compiler_params=_cparams(("arbitrary",)),
    )(slot, shard)


BIG = ("w_in", "w_proj_a", "w_proj_b", "w_out", "w_ffn_in", "w_ffn_down")
COL_SHARDED = ("w_in", "w_ffn_in")


def _coords():
    return lax.axis_index("x"), lax.axis_index("y"), lax.axis_index("c")


W_IN_HEAD = 384


def _shard_views(b, layout, chip, half_of):
    if layout == "slots":
        half = b.shape[1] // 2
        return [b.at[chip, pl.ds(pl.multiple_of(half_of * half, 8), half)]]
    half = b.shape[0] // 2
    rows = pl.ds(pl.multiple_of(half_of * half, 8), half)
    if layout == "cols":
        c = b.shape[1] // 4
        return [b.at[rows, pl.ds(pl.multiple_of((2 * (chip % 2) + chip // 2) * c, 128), c)]]
    c = D_IN // 4
    views = []
    for first, width in ((0, W_IN_HEAD), (W_IN_HEAD, c - W_IN_HEAD)):
        nat = chip * c + first
        rot = jnp.where(nat >= ROT, nat - ROT, nat + D_INP - ROT)
        views.append(b.at[rows, pl.ds(pl.multiple_of(rot, 128), width)])
    return views


def _gather_host(bufs, layouts):
    n = len(bufs)

    def place():
        x, y, c = _coords()
        return x, y, c, [(1 - x, y), (x, 1 - y), (1 - x, 1 - y)]

    def copies(b, sems, w, k, chip, half_of, to):
        return [pltpu.make_async_remote_copy(src_ref=v, dst_ref=v, send_sem=sems[0].at[w, k, p], recv_sem=sems[1].at[w, k, p],
                                             device_id=to, device_id_type=MESH)
                for p, v in enumerate(_shard_views(b, layouts[w], chip, half_of))]

    def start(_, outs, sems):
        x, y, c, chips = place()
        for w, b in enumerate(outs):
            for k, (cx, cy) in enumerate(chips):
                for cp in copies(b, sems, w, k, 2 * x + y, c, (cx, cy, c)):
                    cp.start()

    def mid(_, outs, sems):
        x, y, c, chips = place()
        for w, b in enumerate(outs):
            for k, (cx, cy) in enumerate(chips):
                for cp in copies(b, sems, w, k, 2 * cx + cy, c, (cx, cy, c)):
                    cp.wait_recv()
                for cp in copies(b, sems, w, 3 + k, 2 * cx + cy, c, (x, y, 1 - c)):
                    cp.start()

    def finish(_, outs, sems):
        x, y, c, chips = place()
        for w, b in enumerate(outs):
            for k, (cx, cy) in enumerate(chips):
                for cp in copies(b, sems, w, 3 + k, 2 * cx + cy, 1 - c, (x, y, 1 - c)):
                    cp.wait_recv()
        for w, b in enumerate(outs):
            for k, (cx, cy) in enumerate(chips):
                for cp in copies(b, sems, w, k, 2 * x + y, c, (cx, cy, c)) + copies(b, sems, w, 3 + k, 2 * cx + cy, c, (x, y, 1 - c)):
                    cp.wait_send()

    return _Host(ins=list(bufs), out_shapes=[jax.ShapeDtypeStruct(b.shape, b.dtype) for b in bufs],
                 aliases={i: i for i in range(n)},
                 sems=[pltpu.SemaphoreType.DMA((n, 6, 2)), pltpu.SemaphoreType.DMA((n, 6, 2))],
                 start=start, mid=mid, finish=finish)


def _run_host(host, name):
    n_in, n_out = len(host.ins), len(host.out_shapes)

    def body(*refs):
        ins, outs, sems = refs[:n_in], refs[n_in:n_in + n_out], refs[n_in + n_out:]
        host.start(ins, outs, sems)
        if host.mid is not None:
            host.mid(ins, outs, sems)
        host.finish(ins, outs, sems)

    return pl.pallas_call(
        body, name=name, in_specs=[ANY] * n_in, out_specs=[ANY] * n_out, out_shape=list(host.out_shapes),
        scratch_shapes=list(host.sems), input_output_aliases=dict(host.aliases),
    )(*host.ins)


def _peer(x, y, c, k):
    fx, fy, fc = (k >> 2) & 1, (k >> 1) & 1, k & 1
    return (x ^ fx if fx else x, y ^ fy if fy else y, c ^ fc if fc else c)


def _sub_shape(name, full_shape):
    r, c = full_shape
    return (r // 2, c // 4) if name in COL_SHARDED else (r // 8, c)


def _exchange_host(names, grads):
    n = len(names)
    shapes = [_sub_shape(k, g.shape) for k, g in zip(names, grads)]

    def copy(ins, outs, sems, w, k):
        x, y, c = _coords()
        px, py, pc = _peer(x, y, c, k)
        sr, sc = shapes[w]
        if names[w] in COL_SHARDED:
            col = _col_block(names[w], px, py) * sc
            src = ins[w].at[pl.ds(pl.multiple_of(pc * sr, 16), sr), pl.ds(pl.multiple_of(col, 128), sc)]
        else:
            src = ins[w].at[pl.ds(pl.multiple_of((4 * px + 2 * py + pc) * sr, 16), sr)]
        return pltpu.make_async_remote_copy(
            src_ref=src, dst_ref=outs[w].at[k - 1], send_sem=sems[0].at[w, k - 1], recv_sem=sems[1].at[w, k - 1],
            device_id=(px, py, pc), device_id_type=MESH)

    def start(ins, outs, sems):
        for w in range(n):
            for k in range(1, 8):
                copy(ins, outs, sems, w, k).start()

    def finish(ins, outs, sems):
        for w in range(n):
            for k in range(1, 8):
                copy(ins, outs, sems, w, k).wait_recv()
        for w in range(n):
            for k in range(1, 8):
                copy(ins, outs, sems, w, k).wait_send()

    return _Host(ins=list(grads), out_shapes=[jax.ShapeDtypeStruct((7,) + s, g.dtype) for s, g in zip(shapes, grads)],
                 aliases={}, sems=[pltpu.SemaphoreType.DMA((n, 7)), pltpu.SemaphoreType.DMA((n, 7))],
                 start=start, mid=None, finish=finish)


def _sibling_host(bufs):
    n = len(bufs)

    def copy(outs, sems, w, half_of):
        x, y, c = _coords()
        half = outs[w].shape[0] // 2
        rows = outs[w].at[pl.ds(pl.multiple_of(half_of * half, 8), half)]
        return pltpu.make_async_remote_copy(src_ref=rows, dst_ref=rows, send_sem=sems[0].at[w], recv_sem=sems[1].at[w],
                                            device_id=(x, y, 1 - c), device_id_type=MESH)

    def start(_, outs, sems):
        c = lax.axis_index("c")
        for w in range(n):
            copy(outs, sems, w, c).start()

    def finish(_, outs, sems):
        c = lax.axis_index("c")
        for w in range(n):
            copy(outs, sems, w, 1 - c).wait_recv()
        for w in range(n):
            copy(outs, sems, w, c).wait_send()

    return _Host(ins=list(bufs), out_shapes=[jax.ShapeDtypeStruct(b.shape, b.dtype) for b in bufs],
                 aliases={i: i for i in range(n)},
                 sems=[pltpu.SemaphoreType.DMA((n,)), pltpu.SemaphoreType.DMA((n,))], start=start, mid=None, finish=finish)


def _pack_host(pack):
    def me_of():
        x, y, c = _coords()
        return x, y, c, 4 * x + 2 * y + c

    def copy(ins, outs, sems, k, slot):
        x, y, c, _ = me_of()
        return pltpu.make_async_remote_copy(src_ref=ins[0], dst_ref=outs[0].at[slot], send_sem=sems[0].at[k - 1],
                                            recv_sem=sems[1].at[k - 1], device_id=_peer(x, y, c, k), device_id_type=MESH)

    def start(ins, outs, sems):
        me = me_of()[3]
        pltpu.make_async_copy(ins[0], outs[0].at[me], sems[2]).start()
        for k in range(1, 8):
            copy(ins, outs, sems, k, me).start()

    def finish(ins, outs, sems):
        x, y, c, me = me_of()
        for k in range(1, 8):
            px, py, pc = _peer(x, y, c, k)
            copy(ins, outs, sems, k, 4 * px + 2 * py + pc).wait_recv()
        for k in range(1, 8):
            copy(ins, outs, sems, k, me).wait_send()
        pltpu.make_async_copy(ins[0], outs[0].at[me], sems[2]).wait()

    return _Host(ins=[pack], out_shapes=[jax.ShapeDtypeStruct((8,) + pack.shape, F32)], aliases={},
                 sems=[pltpu.SemaphoreType.DMA((7,)), pltpu.SemaphoreType.DMA((7,)), pltpu.SemaphoreType.DMA],
                 start=start, mid=None, finish=finish)


def _join_hosts(hosts):
    ins, outs, sems, aliases, spans = [], [], [], {}, []
    for h in hosts:
        spans.append((len(ins), len(h.ins), len(outs), len(h.out_shapes), len(sems), len(h.sems)))
        for i_in, i_out in h.aliases.items():
            aliases[len(ins) + i_in] = len(outs) + i_out
        ins, outs, sems = ins + list(h.ins), outs + list(h.out_shapes), sems + list(h.sems)

    def phase(which):
        fns = [getattr(h, which) for h in hosts]
        if all(f is None for f in fns):
            return None

        def run(i_refs, o_refs, s_refs):
            for f, (i0, n_i, o0, n_o, s0, n_s) in zip(fns, spans):
                if f is not None:
                    f(i_refs[i0:i0 + n_i], o_refs[o0:o0 + n_o], s_refs[s0:s0 + n_s])

        return run

    return _Host(ins, outs, aliases, sems, phase("start"), phase("mid"), phase("finish"))


SHARDED_SMALL = ("meta_tokens", "conv_w")

ROWS = {"conv_b": (1, RGW), "rg_ba": (1, RGW), "rg_bx": (1, RGW), "rg_lambda": (1, RGW), "conv_w": (4, RGW),
        "norm2_g": (1, D), "norm_f_g": (1, D), "hg_lb_logits": (2, D),
        "rg_wa": (RGW * RG_BLOCK // 128, 128), "rg_wx": (RGW * RG_BLOCK // 128, 128), "hg_norm_g": (1, 128),
        "norm1_g": (1, D), "meta_tokens": (N_META, D), "loss": (1, D)}
PACKS = {"early_wide": ("conv_b", "rg_ba", "rg_bx", "rg_lambda", "conv_w"),
         "early_model": ("norm2_g", "norm_f_g", "hg_lb_logits"),
         "early_lane": ("rg_wa", "rg_wx", "hg_norm_g"),
         "late": ("norm1_g", "meta_tokens", "loss")}
EARLY = ("early_wide", "early_model", "early_lane")
NO_UPDATE = ("conv_w", "meta_tokens", "loss")


def _pack_rows(vals, names):
    parts, first, row = [], {}, 0
    for k in names:
        first[k] = row
        parts.append(vals[k].reshape(ROWS[k]).astype(F32))
        row += ROWS[k][0]
    if row % 8:
        parts.append(jnp.zeros((-row % 8, ROWS[names[0]][1]), F32))
    return jnp.concatenate(parts, axis=0), first


ORDER = ("meta_tokens", "norm1_g", "w_in", "conv_w", "conv_b", "rg_wa", "rg_ba", "rg_wx", "rg_bx", "rg_lambda",
         "hg_lb_logits", "hg_norm_g", "w_proj_a", "w_proj_b", "w_out", "norm2_g", "w_ffn_in", "w_ffn_down", "norm_f_g")


def kernel(x, meta_tokens, norm1_g, w_in, conv_w, conv_b, rg_wa, rg_ba, rg_wx, rg_bx, rg_lambda, hg_lb_logits, hg_norm_g, w_proj_a, w_proj_b, w_out, norm2_g, w_ffn_in, w_ffn_down, norm_f_g, loss_target, m_meta_tokens, m_norm1_g, m_w_in, m_conv_w, m_conv_b, m_rg_wa, m_rg_ba, m_rg_wx, m_rg_bx, m_rg_lambda, m_hg_lb_logits, m_hg_norm_g, m_w_proj_a, m_w_proj_b, m_w_out, m_norm2_g, m_w_ffn_in, m_w_ffn_down, m_norm_f_g, v_meta_tokens, v_norm1_g, v_w_in, v_conv_w, v_conv_b, v_rg_wa, v_rg_ba, v_rg_wx, v_rg_bx, v_rg_lambda, v_hg_lb_logits, v_hg_norm_g, v_w_proj_a, v_w_proj_b, v_w_out, v_norm2_g, v_w_ffn_in, v_w_ffn_down, v_norm_f_g):
    args = dict(locals())
    w = {k: args[k] for k in ORDER}
    m = {k: args["m_" + k] for k in ORDER}
    v = {k: args["v_" + k] for k in ORDER}
    xi, yi, ci = _coords()
    chip = 2 * xi + yi
    slot = jnp.reshape(chip, (1,)).astype(jnp.int32)
    xyc = jnp.stack([xi, yi, ci]).astype(jnp.int32)

    tiny = jnp.zeros((32, 384), F32)
    tiny = tiny.at[0:N_META, 0:256].set(meta_tokens).at[N_META:N_META + 4, 0:320].set(conv_w[0])
    later = [k for k in BIG if k != "w_in"]
    layouts = {k: "cols" if k in COL_SHARDED else "slots" for k in later}
    slots = {k: _place_cols(w[k][0], slot, "place_" + k) if layouts[k] == "cols"
             else _place_own(w[k][0], slot, BF16, "place_" + k) for k in later}
    w_in_full, tiny_all = _run_host(
        _gather_host([_place_w_in(w["w_in"][0], slot), _place_own(tiny, slot, F32, "place_tiny")], ["w_in", "slots"]),
        "allgather_w_in")

    def full_matrix(k, gth):
        return gth if layouts[k] == "cols" else gth.reshape(-1, gth.shape[2])

    meta_full = jnp.transpose(tiny_all[:, 0:N_META, 0:256], (1, 0, 2)).reshape(N_META, D)
    conv_w_full = jnp.transpose(tiny_all[:, N_META:N_META + 4, 0:320], (1, 0, 2)).reshape(4, RGW)

    small = dict(norm1_g=norm1_g, conv_w=conv_w_full, conv_b=conv_b, rg_wa=rg_wa[0], rg_ba=rg_ba, rg_wx=rg_wx[0],
                 rg_bx=rg_bx, rg_lambda=rg_lambda, hg_lb_logits=hg_lb_logits, hg_norm_g=hg_norm_g,
                 norm2_g=norm2_g, norm_f_g=norm_f_g.reshape(1, D))

    hosts = {
        "mm_proj": lambda g: _gather_host([slots[k] for k in later], [layouts[k] for k in later]),
        "mm_d_act": lambda g: _exchange_host(["w_ffn_down"], [g["w_ffn_down"]]),
        "mm_d_hn2": lambda g: _exchange_host(["w_ffn_in"], [g["w_ffn_in"]]),
        "mm_d_win": lambda g: _join_hosts([
            _exchange_host(["w_out", "w_proj_a", "w_proj_b"], [g["w_out"], g["w_proj_a"], g["w_proj_b"]]),
            *[_pack_host(_pack_rows(g, PACKS[p])[0]) for p in EARLY]]),
        "mm_d_hn1": lambda g: _exchange_host(["w_in"], [_unrotate_g_in(g["w_in"])]),
    }
    h0 = jnp.concatenate([jnp.zeros((PAD, D), F32), meta_full, x[0]], axis=0)
    loss_blk, grad_x, g, carried = _local_step(
        h0, loss_target[0], {"w_in": w_in_full}, small, hosts=hosts,
        finalize=lambda gathered: {k: full_matrix(k, gth) for k, gth in zip(later, gathered)})
    g["loss"] = jnp.broadcast_to(loss_blk[0:1, 0:1], (1, D))

    halves = {}
    sources = {"mm_d_act": ["w_ffn_down"], "mm_d_hn2": ["w_ffn_in"], "mm_d_win": ["w_out", "w_proj_a", "w_proj_b"],
               "mm_d_hn1": ["w_in"]}
    for name, keys in sources.items():
        partials, received = carried[name]
        for k, part, rec in zip(keys, partials, received):
            halves[k] = _sum_place(rec, part, k, xyc, "sum_" + k)
    late, _ = _pack_rows(g, PACKS["late"])
    tail = _run_host(_join_hosts([_sibling_host([halves[k] for k in BIG]), _pack_host(late)]), "swap_and_late_pack")
    g_big = dict(zip(BIG, tail[:len(BIG)]))
    recv_packs = dict(zip(EARLY, carried["mm_d_win"][1][3:]), late=tail[len(BIG)])

    grad, delta, new_m, new_v, summed = {}, {}, {}, {}, {}
    for pack, names in PACKS.items():
        row, params, extras = 0, [], []
        for k in names:
            if k in NO_UPDATE:
                extras.append((row, ROWS[k][0]))
            else:
                params.append((row,) + tuple(d[k].reshape(ROWS[k]) for d in (w, m, v)))
            row += ROWS[k][0]
        updated, rows_only = _adamw_rows(recv_packs[pack], params, extras, "adamw_" + pack)
        for k, res in zip([k for k in names if k not in NO_UPDATE], updated):
            grad[k], delta[k], new_m[k], new_v[k] = (a.reshape(w[k].shape) for a in res)
        summed.update(zip([k for k in names if k in NO_UPDATE], rows_only))
    loss = summed["loss"][0, 0]
    g_shard = {"meta_tokens": lax.dynamic_slice(summed["meta_tokens"], (0, chip * 256), (N_META, 256)),
               "conv_w": lax.dynamic_slice(summed["conv_w"], (0, chip * 320), (4, 320))}
    for k in BIG + SHARDED_SMALL:
        gk = g_big[k] if k in BIG else g_shard[k]
        wk, mk, vk = (d[k].reshape(gk.shape) for d in (w, m, v))
        dk, mk, vk = _adamw(wk, gk, mk, vk, "adamw_" + k)
        grad[k], delta[k], new_m[k], new_v[k] = (a.reshape(w[k].shape) for a in (gk, dk, mk, vk))

    return (loss, grad_x[None], *[grad[k] for k in ORDER], *[delta[k] for k in ORDER],
            *[new_m[k] for k in ORDER], *[new_v[k] for k in ORDER])
```

```python
import jax
import jax.numpy as jnp
from jax import lax
from jax.experimental import pallas as pl
from jax.experimental.pallas import tpu as pltpu

F32, BF16 = jnp.float32, jnp.bfloat16
D = 1024
N_META = 16
RGW = 1280
RG_BLOCKS, RG_BLOCK = 16, 80
RG_C = 8.0
HEADS, HD = 8, 128
HGW = HEADS * HD
DFF = 2816
D_IN = 2 * RGW + 6 * D
ROT = 2 * RGW
COL_PAD = 256
D_INP = D_IN + COL_PAD
EPS = 1e-6
CH = 64
TM = 256
ROW0 = TM
PAD = ROW0 - N_META
CT = RGW
RS = 16
NCT = RGW // CT
EXP_CLAMP = 80.0
VMEM_LIMIT = 56 * 1024 * 1024

LR, B1, B2, ADAM_EPS, WD, STEP = 0.001, 0.9, 0.999, 1e-08, 0.01, 10
MESH = pl.DeviceIdType.MESH
ANY = pl.BlockSpec(memory_space=pl.ANY)


def _cparams(sem):
    return pltpu.CompilerParams(dimension_semantics=sem, vmem_limit_bytes=VMEM_LIMIT)


def _pick(n, prefs):
    for p in prefs:
        if n % p == 0:
            return p
    return n


def _sig(x):
    return 0.5 * jnp.tanh(0.5 * x) + 0.5


def _dot(a, b, dims):
    return lax.dot_general(a, b, (dims, ((), ())), preferred_element_type=F32)


NN, NT, TN_ = ((1,), (0,)), ((1,), (1,)), ((0,), (0,))


class _Host:
    def __init__(self, ins, out_shapes, aliases, sems, start, mid, finish):
        self.ins, self.out_shapes, self.aliases, self.sems = ins, out_shapes, aliases, sems
        self.start, self.mid, self.finish = start, mid, finish


class _Epi:
    def __init__(self, ins, in_specs, out_shapes, out_specs, fn, sequential=False, split=1):
        self.ins, self.in_specs, self.out_shapes, self.out_specs, self.fn = ins, in_specs, out_shapes, out_specs, fn
        self.sequential = sequential
        self.split = split


class _Pro:
    def __init__(self, ins, in_specs, fn):
        self.ins, self.in_specs, self.fn = ins, in_specs, fn


def _norm_pro(gain):
    def fn(a_ref, ins):
        xh, _ = _rms(a_ref[...])
        return (xh * ins[0][...]).astype(BF16)

    return _Pro([gain], [pl.BlockSpec((1, gain.shape[1]), lambda i, j, k: (0, 0))], fn)


def _mm(a, b, mode, out_dtype, name, resid=None, host=None, epi=None, tiles=None, sparse=None, pro=None):
    if mode == "nn":
        (m, kd), n = a.shape, b.shape[1]
    elif mode == "nt":
        (m, kd), n = a.shape, b.shape[0]
    else:
        (kd, m), n = a.shape, b.shape[1]
    if mode == "tn":
        tm = _pick(m, (1024, 1280, 1408, 640, 512, 256, 128))
        tk = _pick(kd, (1408, 768, 512, 256, 128))
    else:
        tm = _pick(m, (768, 512, 640, 256, 128))
        tk = kd if kd <= 2816 else _pick(kd, (1792, 1408, 1024, 512, 256, 128))
    tn = _pick(n, (1792, 1408, 1280, 1024, 512, 256, 128))
    if tiles is not None:
        tm, tn = tiles
    a_map = (lambda i, j, k: (k, i)) if mode == "tn" else (lambda i, j, k: (i, k))
    b_map = (lambda i, j, k: (j, k)) if mode == "nt" else (lambda i, j, k: (k, j))
    o_map = lambda i, j, k: (i, j)
    if sparse is not None:
        (ni, nj, nk), (tm, tn, tk), (a_map, b_map, o_map) = sparse
    else:
        assert m % tm == 0 and n % tn == 0 and kd % tk == 0, (name, m, n, kd, tm, tn, tk)
        ni, nj, nk = m // tm, n // tn, kd // tk
    dims = {"nn": NN, "nt": NT, "tn": TN_}[mode]

    n_hin = len(host.ins) if host else 0
    n_hout = len(host.out_shapes) if host else 0
    n_res = 0 if resid is None else 1
    n_pro = 0 if pro is None else 1
    n_pin = len(pro.ins) if pro else 0
    n_ein = len(epi.ins) if epi else 0
    n_out = len(epi.out_shapes) if epi else 1
    assert not n_pro or (mode == "nn" and nk == 1)

    def finish(r, r_ref, e_in, o_refs, i, rows=slice(None), first=True):
        if resid is not None:
            r = r + r_ref[rows, :].astype(F32)
        if epi:
            epi.fn(r, e_in, o_refs, i, rows, first)
        else:
            o_refs[0][rows, :] = r.astype(out_dtype)

    def body(*refs):
        a_ref, b_ref = refs[:2]
        r_ref = refs[2] if resid is not None else None
        pos = 2 + n_res
        p_in = refs[pos:pos + n_pin]
        pos += n_pin
        e_in = refs[pos:pos + n_ein]
        pos += n_ein
        h_in = refs[pos:pos + n_hin]
        pos += n_hin
        o_ref = refs[pos:pos + n_out]
        pos += n_out
        hn_ref = refs[pos] if n_pro else None
        pos += n_pro
        h_out = refs[pos:pos + n_hout]
        scratch = refs[pos + n_hout:]
        n_acc = 1 if nk > 1 else 0
        h_sems = scratch[n_acc + n_pro:]
        i, j, k = pl.program_id(0), pl.program_id(1), pl.program_id(2)
        if host:
            @pl.when((i == 0) & (j == 0) & (k == 0))
            def _():
                host.start(h_in, h_out, h_sems)

            if host.mid is not None:
                @pl.when((i == (3 * ni) // 4) & (j == 0) & (k == 0))
                def _():
                    host.mid(h_in, h_out, h_sems)

        if n_pro:
            a_s = scratch[n_acc]

            @pl.when(j == 0)
            def _():
                a_s[...] = pro.fn(a_ref, p_in)
                hn_ref[...] = a_s[...]

            a_val = a_s[...]
        else:
            a_val = a_ref[...]

        if nk == 1 and epi and epi.split > 1 and mode != "tn":
            part = tm // epi.split
            for p in range(epi.split):
                rows = slice(p * part, (p + 1) * part)
                finish(_dot(a_val[rows, :], b_ref[...], dims), r_ref, e_in, o_ref, i, rows, p == 0)
        elif nk == 1:
            finish(_dot(a_val, b_ref[...], dims), r_ref, e_in, o_ref, i)
        else:
            acc = scratch[0]

            @pl.when(k == 0)
            def _():
                acc[...] = jnp.zeros_like(acc)

            acc[...] += _dot(a_val, b_ref[...], dims)

            @pl.when(k == nk - 1)
            def _():
                finish(acc[...], r_ref, e_in, o_ref, i)

        if host:
            @pl.when((i == ni - 1) & (j == nj - 1) & (k == nk - 1))
            def _():
                host.finish(h_in, h_out, h_sems)

    a_spec = pl.BlockSpec((tk, tm) if mode == "tn" else (tm, tk), a_map)
    b_spec = pl.BlockSpec((tn, tk) if mode == "nt" else (tk, tn), b_map)
    o_spec = pl.BlockSpec((tm, tn), o_map)
    in_specs, args = [a_spec, b_spec], [a, b]
    if resid is not None:
        in_specs.append(o_spec)
        args.append(resid)
    if n_pro:
        in_specs += list(pro.in_specs)
        args += list(pro.ins)
    out_shapes, out_specs = [jax.ShapeDtypeStruct((m, n), out_dtype)], [o_spec]
    if epi:
        in_specs += list(epi.in_specs)
        args += list(epi.ins)
        out_shapes, out_specs = list(epi.out_shapes), list(epi.out_specs)
    if n_pro:
        out_shapes.append(jax.ShapeDtypeStruct((m, kd), BF16))
        out_specs.append(pl.BlockSpec((tm, kd), lambda i, j, k: (i, 0)))
    scratch = ([pltpu.VMEM((tm, tn), F32)] if nk > 1 else []) + ([pltpu.VMEM((tm, kd), BF16)] if n_pro else [])
    sequential = host or n_pro or (epi and epi.sequential)
    outs = pl.pallas_call(
        body, name=name, grid=(ni, nj, nk),
        in_specs=in_specs + [ANY] * n_hin, out_specs=out_specs + [ANY] * n_hout,
        out_shape=out_shapes + (list(host.out_shapes) if host else []),
        scratch_shapes=scratch + (list(host.sems) if host else []),
        input_output_aliases=({2 + n_res + n_pin + n_ein + i_in: n_out + n_pro + i_out
                               for i_in, i_out in host.aliases.items()} if host else {}),
        compiler_params=_cparams(("arbitrary",) * 3 if sequential else ("parallel", "parallel", "arbitrary")),
    )(*args, *(host.ins if host else []))
    main = list(outs[:n_out]) if epi else outs[0]
    if n_pro:
        main = (main, outs[n_out])
    return (main, list(outs[n_out + n_pro:])) if host else main


def _rms(x):
    r = lax.rsqrt(jnp.mean(x * x, axis=-1, keepdims=True) + EPS)
    return x * r, r


def _rms_bwd_math(dhn, x, g):
    xh, r = _rms(x)
    dxh = dhn * g
    dx = r * (dxh - xh * jnp.mean(dxh * xh, axis=-1, keepdims=True))
    return dx, jnp.sum(dhn * xh, axis=0, keepdims=True)


def _row_specs(tm):
    return pl.BlockSpec((tm, D), lambda i, j, k: (i, 0)), pl.BlockSpec((1, D), lambda i, j, k: (0, 0))


def _final_loss_epi(gf, target, t, tm):
    parts = tm // TM

    def fn(x, ins, outs, i, rows, first):
        g_ref = ins[0]
        p = rows.start // TM
        t_ref = ins[1 + p]
        dh_ref, dhb_ref, loss_ref, dg_ref = outs

        if first:
            @pl.when(i == 0)
            def _():
                loss_ref[...] = jnp.zeros_like(loss_ref)
                dg_ref[...] = jnp.zeros_like(dg_ref)

        g = g_ref[...]
        xh, _ = _rms(x)
        err = jnp.where(i * parts + p > 0, xh * g - t_ref[...], 0.0)
        loss_ref[...] += 0.5 * jnp.sum(jnp.sum(err * err, axis=-1, keepdims=True) * (1.0 / D))
        dx, dg = _rms_bwd_math(err * (1.0 / D), x, g)
        dh_ref[rows, :] = dx
        dhb_ref[rows, :] = dx.astype(BF16)
        dg_ref[...] += dg

    row, vec = _row_specs(tm)
    t_specs = [pl.BlockSpec((TM, D), lambda i, j, k, p=p: (jnp.maximum(i * parts + p - 1, 0), 0)) for p in range(parts)]
    return _Epi(ins=[gf] + [target] * parts, in_specs=[vec] + t_specs,
                out_shapes=[jax.ShapeDtypeStruct((t, D), F32), jax.ShapeDtypeStruct((t, D), BF16),
                            jax.ShapeDtypeStruct((8, 128), F32), jax.ShapeDtypeStruct((1, D), F32)],
                out_specs=[row, row, pl.BlockSpec((8, 128), lambda i, j, k: (0, 0)), vec], fn=fn, sequential=True,
                split=parts)


def _rms_bwd_epi(h, g, dres, tm):
    t = h.shape[0]

    def fn(dhn, ins, outs, i, rows, first):
        h_ref, g_ref, dres_ref = ins
        dh_ref, dhb_ref, dg_ref = outs

        if first:
            @pl.when(i == 0)
            def _():
                dg_ref[...] = jnp.zeros_like(dg_ref)

        dx, dg = _rms_bwd_math(dhn, h_ref[rows, :], g_ref[...])
        dh = dres_ref[rows, :] + dx
        dh_ref[rows, :] = dh
        dhb_ref[rows, :] = dh.astype(BF16)
        dg_ref[...] += dg

    row, vec = _row_specs(tm)
    return _Epi(ins=[h, g, dres], in_specs=[row, vec, row],
                out_shapes=[jax.ShapeDtypeStruct((t, D), F32), jax.ShapeDtypeStruct((t, D), BF16),
                            jax.ShapeDtypeStruct((1, D), F32)],
                out_specs=[row, row, vec], fn=fn, sequential=True)


def _rmsnorm1_bwd(dhn, h, g, dres):
    t = h.shape[0]

    def body(dhn_ref, h_ref, g_ref, dres_ref, dx_ref, dmeta_ref, dg_ref):
        i = pl.program_id(0)

        @pl.when(i == 0)
        def _():
            dg_ref[...] = jnp.zeros_like(dg_ref)

        dx, dg = _rms_bwd_math(dhn_ref[...].astype(F32), h_ref[...], g_ref[...])
        dh = dres_ref[...] + dx
        dg_ref[...] += dg
        dx_ref[...] = dh

        @pl.when(i == 0)
        def _():
            dmeta_ref[...] = dh[PAD:, :]

    row = pl.BlockSpec((TM, D), lambda i: (i, 0))
    vec = pl.BlockSpec((1, D), lambda i: (0, 0))
    return pl.pallas_call(
        body, name="rmsnorm1_bwd", grid=(t // TM,),
        in_specs=[row, row, vec, row],
        out_specs=[pl.BlockSpec((TM, D), lambda i: (jnp.maximum(i - 1, 0), 0)),
                   pl.BlockSpec((N_META, D), lambda i: (0, 0)), vec],
        out_shape=[jax.ShapeDtypeStruct((t - ROW0, D), F32), jax.ShapeDtypeStruct((N_META, D), F32),
                   jax.ShapeDtypeStruct((1, D), F32)],
        compiler_params=_cparams(("arbitrary",)),
    )(dhn, h, g, dres)


FH = DFF // 2
FM = 384


def _swiglu_epi(t):
    def fn(r, _, outs, i, rows, first):
        g, u = r[:, :FH], r[:, FH:]
        outs[0][rows, :] = r.astype(BF16)
        outs[1][rows, :] = (g * _sig(g) * u).astype(BF16)

    return _Epi(ins=[], in_specs=[],
                out_shapes=[jax.ShapeDtypeStruct((t, 2 * DFF), BF16), jax.ShapeDtypeStruct((t, DFF), BF16)],
                out_specs=[pl.BlockSpec((FM, 2 * FH), lambda i, j, k: (i, j)), pl.BlockSpec((FM, FH), lambda i, j, k: (i, j))],
                fn=fn)


def _swiglu_bwd_epi(gu):
    def fn(d, ins, outs, i, rows, first):
        gu_t = ins[0][rows, :].astype(F32)
        g, u = gu_t[:, :FH], gu_t[:, FH:]
        s = _sig(g)
        outs[0][rows, :FH] = (d * u * (s * (1.0 + g * (1.0 - s)))).astype(BF16)
        outs[0][rows, FH:] = (d * (g * s)).astype(BF16)

    spec = pl.BlockSpec((FM, 2 * FH), lambda i, j, k: (i, j))
    return _Epi(ins=[gu], in_specs=[spec], out_shapes=[jax.ShapeDtypeStruct(gu.shape, BF16)], out_specs=[spec], fn=fn)


COL_GA, COL_GB = 4, 5
COL_AX, COL_AG = (6 * D + COL_PAD) // CT, (6 * D + COL_PAD + RGW) // CT


def _merge_pro(pb, proj, tm):
    def fn(a_ref, ins):
        pa, pb_, ga, gb = (r[...].astype(F32) for r in (a_ref,) + tuple(ins))
        return (_sig(ga) * pa + _sig(gb) * pb_).astype(BF16)

    row, _ = _row_specs(tm)
    return _Pro([pb, proj, proj], [row, pl.BlockSpec((tm, D), lambda i, j, k: (i, COL_GA)),
                                   pl.BlockSpec((tm, D), lambda i, j, k: (i, COL_GB))], fn)


def _merge_bwd_epi(pa, pb, proj, tm):
    t = pa.shape[0]

    def fn(d, ins, outs, i, rows, first):
        pa, pb, ga, gb = (r[rows, :].astype(F32) for r in ins)
        dpa_ref, dpb_ref, dproj_ref = outs
        sa, sb = _sig(ga), _sig(gb)
        dpa_ref[rows, :] = (sa * d).astype(BF16)
        dpb_ref[rows, :] = (sb * d).astype(BF16)
        dproj_ref[rows, :D] = (d * pa * sa * (1.0 - sa)).astype(BF16)
        dproj_ref[rows, D:] = (d * pb * sb * (1.0 - sb)).astype(BF16)

    row, _ = _row_specs(tm)
    return _Epi(ins=[pa, pb, proj, proj],
                in_specs=[row, row, pl.BlockSpec((tm, D), lambda i, j, k: (i, COL_GA)),
                          pl.BlockSpec((tm, D), lambda i, j, k: (i, COL_GB))],
                out_shapes=[jax.ShapeDtypeStruct((t, D), BF16), jax.ShapeDtypeStruct((t, D), BF16),
                            jax.ShapeDtypeStruct((t, D_INP), BF16)],
                out_specs=[row, row, pl.BlockSpec((tm, 2 * D), lambda i, j, k: (i, 2))], fn=fn)


def _zero_pad_cols(dproj):
    t = dproj.shape[0]
    tz = _pick(t, (768, 256))

    def body(_, o_ref):
        o_ref[...] = jnp.zeros_like(o_ref)

    return pl.pallas_call(
        body, name="dproj_pad", grid=(t // tz,), in_specs=[ANY],
        out_specs=pl.BlockSpec((tz, COL_PAD), lambda i: (i, (6 * D) // COL_PAD)),
        out_shape=jax.ShapeDtypeStruct(dproj.shape, BF16), input_output_aliases={0: 0},
        compiler_params=_cparams(("parallel",)),
    )(dproj)


HALO = 16


def _rows_before(cur, before, s):
    n = cur.shape[0]
    return jnp.concatenate([before, cur], axis=0)[n - s:2 * n - s, :]


def _rows_after(cur, after, s):
    n = cur.shape[0]
    return jnp.concatenate([cur, after], axis=0)[s:n + s, :]


def _prev_halo(col0):
    return pl.BlockSpec((HALO, CT), lambda c, i: (jnp.maximum(i * (TM // HALO) - 1, 0), col0 + c))


def _conv_fwd(proj, w, b):
    t = proj.shape[0]

    def body(x_ref, halo_ref, w_ref, b_ref, ob_ref):
        i = pl.program_id(1)
        wv, bv = w_ref[...], b_ref[...]
        first = jnp.where(i > 0, halo_ref[...].astype(F32), 0.0)

        def strip(k, before):
            rows = pl.ds(pl.multiple_of(k * HALO, HALO), HALO)
            x = x_ref[rows, :].astype(F32)
            y = bv + wv[3:4, :] * x
            for j in range(3):
                y = y + wv[j:j + 1, :] * _rows_before(x, before, 3 - j)
            ob_ref[rows, :] = y.astype(BF16)
            return x

        lax.fori_loop(0, TM // HALO, strip, first)

    blk = pl.BlockSpec((TM, CT), lambda c, i: (i, c))
    return pl.pallas_call(
        body, name="conv_fwd", grid=(NCT, t // TM),
        in_specs=[pl.BlockSpec((TM, CT), lambda c, i: (i, COL_AX + c)), _prev_halo(COL_AX),
                  pl.BlockSpec((4, CT), lambda c, i: (0, c)), pl.BlockSpec((1, CT), lambda c, i: (0, c))],
        out_specs=blk,
        out_shape=jax.ShapeDtypeStruct((t, RGW), BF16),
        compiler_params=_cparams(("parallel", "parallel")),
    )(proj, proj, w, b)


def _conv_bwd(dxc, proj, w, dproj):
    t = proj.shape[0]
    nt = t // TM

    def body(d_ref, dn_ref, x_ref, halo_ref, w_ref, _, dx_ref, dw_ref, db_ref, acc):
        i = pl.program_id(1)

        @pl.when(i == 0)
        def _():
            dw_ref[...] = jnp.zeros_like(dw_ref)
            db_ref[...] = jnp.zeros_like(db_ref)

        acc[...] = jnp.zeros_like(acc)
        wv = w_ref[...]
        first = jnp.where(i > 0, halo_ref[...].astype(F32), 0.0)
        last = jnp.where(i < nt - 1, dn_ref[...].astype(F32), 0.0)
        ns = TM // HALO

        def fold(v):
            return v[0:8, :] + v[8:16, :]

        def strip(k, before):
            off = pl.multiple_of(k * HALO, HALO)
            rows = pl.ds(off, HALO)
            x, d = x_ref[rows, :].astype(F32), d_ref[rows, :].astype(F32)
            nxt = d_ref[pl.ds(pl.multiple_of(jnp.minimum(off + HALO, TM - HALO), HALO), HALO), :].astype(F32)
            after = jnp.where(k == ns - 1, last, nxt)
            dx = wv[3:4, :] * d
            for j in range(3):
                dx = dx + wv[j:j + 1, :] * _rows_after(d, after, 3 - j)
                acc[j] += fold(d * _rows_before(x, before, 3 - j))
            acc[3] += fold(d * x)
            acc[4] += fold(d)
            dx_ref[rows, :] = dx.astype(BF16)
            return x

        lax.fori_loop(0, ns, strip, first)
        for j in range(4):
            dw_ref[j:j + 1, :] += jnp.sum(acc[j], axis=0, keepdims=True)
        db_ref[...] += jnp.sum(acc[4], axis=0, keepdims=True)

    return pl.pallas_call(
        body, name="conv_bwd", grid=(NCT, nt),
        in_specs=[pl.BlockSpec((TM, CT), lambda c, i: (i, c)),
                  pl.BlockSpec((HALO, CT), lambda c, i: (jnp.minimum((i + 1) * (TM // HALO), t // HALO - 1), c)),
                  pl.BlockSpec((TM, CT), lambda c, i: (i, COL_AX + c)), _prev_halo(COL_AX),
                  pl.BlockSpec((4, CT), lambda c, i: (0, c)), ANY],
        out_specs=[pl.BlockSpec((TM, CT), lambda c, i: (i, COL_AX + c)),
                   pl.BlockSpec((4, CT), lambda c, i: (0, c)), pl.BlockSpec((1, CT), lambda c, i: (0, c))],
        out_shape=[jax.ShapeDtypeStruct((t, D_INP), BF16), jax.ShapeDtypeStruct((4, RGW), F32),
                   jax.ShapeDtypeStruct((1, RGW), F32)],
        scratch_shapes=[pltpu.VMEM((5, 8, CT), F32)],
        input_output_aliases={5: 0},
        compiler_params=_cparams(("parallel", "arbitrary")),
    )(dxc, dxc, proj, proj, w, dproj)


def _gelu(x):
    c = 0.7978845608028654
    th = jnp.tanh(c * (x + 0.044715 * x * x * x))
    return 0.5 * x * (1.0 + th), th


def _rg_gates(gr, gi, xc, ba, bx, lam, row0):
    r = _sig(gr + ba)
    ig = _sig(gi + bx)
    sp = jax.nn.softplus(-lam)
    a = jnp.exp(-RG_C * r * sp)
    s2 = jnp.maximum(1.0 - a * a, 1e-30)
    rs = lax.rsqrt(s2)
    s = s2 * rs
    rows = row0 + lax.broadcasted_iota(jnp.int32, gr.shape, 0)
    live = rows >= PAD
    u = jnp.where(live, s * ig * xc, 0.0)
    return r, ig, sp, a, s, rs, u, live


def _rg_fwd(gri, xc, proj, ba, bx, lam):
    t = gri.shape[0]

    def body(gr_ref, gi_ref, xc_ref, ag_ref, ba_ref, bx_ref, lam_ref, h_ref, ya_ref, hc):
        i = pl.program_id(1)

        @pl.when(i == 0)
        def _():
            hc[...] = jnp.zeros_like(hc)

        ba, bx, lam = ba_ref[...], bx_ref[...], lam_ref[...]
        sub = lax.broadcasted_iota(jnp.int32, (8, CT), 0)

        def strip(k, h):
            off = pl.multiple_of(k * RS, RS)
            rows = pl.ds(off, RS)
            _, _, _, a, _, _, u, _ = _rg_gates(gr_ref[rows, :], gi_ref[rows, :], xc_ref[rows, :].astype(F32), ba, bx, lam,
                                               i * TM + off)
            outs = []
            for half in range(RS // 8):
                out = jnp.zeros((8, CT), F32)
                for r in range(8):
                    h = a[8 * half + r:8 * half + r + 1, :] * h + u[8 * half + r:8 * half + r + 1, :]
                    out = jnp.where(sub == r, h, out)
                outs.append(out)
            hs = jnp.concatenate(outs, axis=0)
            h_ref[rows, :] = hs
            ge, _ = _gelu(ag_ref[rows, :].astype(F32))
            ya_ref[rows, :] = (hs * ge).astype(BF16)
            return h

        hc[...] = lax.fori_loop(0, TM // RS, strip, hc[...])

    blk_ = pl.BlockSpec((TM, CT), lambda c, i: (i, c))
    vec = pl.BlockSpec((1, CT), lambda c, i: (0, c))
    return pl.pallas_call(
        body, name="rg_fwd", grid=(NCT, t // TM),
        in_specs=[blk_, pl.BlockSpec((TM, CT), lambda c, i: (i, NCT + c)), blk_,
                  pl.BlockSpec((TM, CT), lambda c, i: (i, COL_AG + c)), vec, vec, vec],
        out_specs=[blk_, blk_],
        out_shape=[jax.ShapeDtypeStruct((t, RGW), F32), jax.ShapeDtypeStruct((t, RGW), BF16)],
        scratch_shapes=[pltpu.VMEM((1, CT), F32)],
        compiler_params=_cparams(("parallel", "arbitrary")),
    )(gri, gri, xc, proj, ba, bx, lam)


def _rg_bwd(dya, h, gri, xc, proj, ba, bx, lam, dproj):
    t = gri.shape[0]
    nt = t // TM
    assert NCT == 1

    def body(dya_ref, h_ref, hh_ref, gr_ref, gi_ref, xc_ref, ag_ref, ba_ref, bx_ref, lam_ref, _,
             dgri_ref, dxc_ref, dag_ref, dba_ref, dbx_ref, dlam_ref, hbuf, acc, cc):
        i = pl.program_id(1)
        ri = nt - 1 - i

        @pl.when(i == 0)
        def _():
            cc[...] = jnp.zeros_like(cc)
            dba_ref[...] = jnp.zeros_like(dba_ref)
            dbx_ref[...] = jnp.zeros_like(dbx_ref)
            dlam_ref[...] = jnp.zeros_like(dlam_ref)

        acc[...] = jnp.zeros_like(acc)
        ba, bx, lam = ba_ref[...], bx_ref[...], lam_ref[...]
        halo_last = jnp.where(ri > 0, hh_ref[7:8, :], 0.0)
        sub = lax.broadcasted_iota(jnp.int32, (8, CT), 0)
        c0 = 0.7978845608028654

        def strip(kk, c):
            k = TM // RS - 1 - kk
            off = pl.multiple_of(k * RS, RS)
            rows = pl.ds(off, RS)
            xc, hv = xc_ref[rows, :].astype(F32), h_ref[rows, :]
            ag, dya = ag_ref[rows, :].astype(F32), dya_ref[rows, :].astype(F32)
            r, ig, sp, a, s, rs, _, live = _rg_gates(gr_ref[rows, :], gi_ref[rows, :], xc, ba, bx, lam, ri * TM + off)
            ge, th = _gelu(ag)
            dge = 0.5 * (1.0 + th) + 0.5 * ag * (1.0 - th * th) * c0 * (1.0 + 3.0 * 0.044715 * ag * ag)
            dag_ref[rows, :] = (dya * hv * dge).astype(BF16)
            d = dya * ge
            outs = []
            for half in range(RS // 8 - 1, -1, -1):
                out = jnp.zeros((8, CT), F32)
                for rr in range(7, -1, -1):
                    g = d[8 * half + rr:8 * half + rr + 1, :] + c
                    c = a[8 * half + rr:8 * half + rr + 1, :] * g
                    out = jnp.where(sub == rr, g, out)
                outs.insert(0, out)
            g = jnp.concatenate(outs, axis=0)
            before = h_ref[pl.ds(pl.multiple_of(jnp.maximum(off - RS, 0), RS), RS), :]
            hbuf[7:8, :] = jnp.where(k == 0, halo_last, before[RS - 1:RS, :])
            hbuf[8:, :] = hv
            hprev = hbuf[pl.ds(7, RS), :]
            du = jnp.where(live, g, 0.0)
            ds = du * (ig * xc)
            dm = du * s
            dla = (g * hprev - a * ds * rs) * a
            dr = dla * (-RG_C * sp) * r * (1.0 - r)
            di = dm * xc * ig * (1.0 - ig)
            dgri_ref[rows, :CT] = dr.astype(BF16)
            dgri_ref[rows, CT:] = di.astype(BF16)
            dxc_ref[rows, :] = (dm * ig).astype(BF16)
            dl = dla * (-RG_C * r)
            for half in range(RS // 8):
                part = slice(8 * half, 8 * half + 8)
                acc[0] += dr[part]
                acc[1] += di[part]
                acc[2] += dl[part]
            return c

        cc[...] = lax.fori_loop(0, TM // RS, strip, cc[...])
        dba_ref[...] += jnp.sum(acc[0], axis=0, keepdims=True)
        dbx_ref[...] += jnp.sum(acc[1], axis=0, keepdims=True)
        dlam_ref[...] += jnp.sum(acc[2], axis=0, keepdims=True) * (-_sig(-lam))

    rblk = pl.BlockSpec((TM, CT), lambda c, i: (nt - 1 - i, c))
    vec = pl.BlockSpec((1, CT), lambda c, i: (0, c))
    hh = pl.BlockSpec((8, CT), lambda c, i: (jnp.maximum((nt - 1 - i) * (TM // 8) - 1, 0), c))
    agb = pl.BlockSpec((TM, CT), lambda c, i: (nt - 1 - i, COL_AG + c))
    return pl.pallas_call(
        body, name="rg_bwd", grid=(NCT, nt),
        in_specs=[rblk, rblk, hh, rblk, pl.BlockSpec((TM, CT), lambda c, i: (nt - 1 - i, NCT + c)), rblk, agb,
                  vec, vec, vec, ANY],
        out_specs=[pl.BlockSpec((TM, 2 * CT), lambda c, i: (nt - 1 - i, c)), rblk, agb, vec, vec, vec],
        out_shape=[jax.ShapeDtypeStruct((t, 2 * RGW), BF16),
                   jax.ShapeDtypeStruct((t, RGW), BF16), jax.ShapeDtypeStruct((t, D_INP), BF16),
                   jax.ShapeDtypeStruct((1, RGW), F32), jax.ShapeDtypeStruct((1, RGW), F32),
                   jax.ShapeDtypeStruct((1, RGW), F32)],
        scratch_shapes=[pltpu.VMEM((RS + 8, CT), F32), pltpu.VMEM((3, 8, CT), F32), pltpu.VMEM((1, CT), F32)],
        input_output_aliases={10: 2},
        compiler_params=_cparams(("parallel", "arbitrary")),
    )(dya, h, h, gri, gri, xc, proj, ba, bx, lam, dproj)


NCH = TM // CH

def _tri_dot(tri, x):
    hi = x.astype(BF16)
    r1 = x - hi.astype(F32)
    mid = r1.astype(BF16)
    lo = (r1 - mid.astype(F32)).astype(BF16)
    return _dot(tri, hi, NN) + _dot(tri, mid, NN) + _dot(tri, lo, NN)


def _hg_chunk(qr, fr, lb):
    sf = _sig(fr)
    fg = lb + (1.0 - lb) * sf
    k = (1.0 - lb) * (1.0 - sf)
    sq = _sig(qr)
    q = qr * sq
    ri = lax.broadcasted_iota(jnp.int32, (CH, CH), 0)
    ci = lax.broadcasted_iota(jnp.int32, (CH, CH), 1)
    b = _tri_dot((ri >= ci).astype(BF16), jnp.log(fg))
    bm, bl = b[CH // 2 - 1:CH // 2, :], b[CH - 1:CH, :]
    ebm = jnp.exp(jnp.minimum(b - bm, EXP_CLAMP))
    ekm = jnp.exp(jnp.minimum(bm - b, EXP_CLAMP))
    eb = ebm * jnp.exp(bm)
    ekl = ekm * jnp.exp(bl - bm)
    return dict(sf=sf, fg=fg, k=k, sq=sq, q=q, eb=eb, ebm=ebm, ekm=ekm, ekl=ekl, ebl=jnp.exp(bl),
                qe=q * eb, qh=q * ebm, kh=k * ekm, kd=k * ekl, causal=ri >= ci, upper=(ci >= ri).astype(BF16))


def _hgrn_fwd(proj, lbl, gn):
    t = proj.shape[0]
    nt = t // TM

    def body(q_ref, f_ref, v_ref, g_ref, lbl_ref, gn_ref, yb_ref, o_ref, st_ref, st):
        @pl.when(pl.program_id(0) == 0)
        def _():
            st[...] = jnp.zeros_like(st)

        l = lbl_ref[...]
        lb = _sig(l[0:1, :] - l[1:2, :])
        gnv = gn_ref[...]

        def chunk(c, carry):
            off = pl.multiple_of(c * CH, CH)
            rows = pl.ds(off, CH)
            z = _hg_chunk(q_ref[rows, :].astype(F32), f_ref[rows, :].astype(F32), lb)
            v, gg = v_ref[rows, :], g_ref[rows, :].astype(F32)
            for hh in range(HEADS):
                sl = slice(hh * HD, (hh + 1) * HD)
                s_prev = st[hh]
                st_ref[c, hh] = s_prev
                vb = v[:, sl].astype(BF16)
                att = jnp.where(z["causal"], _dot(z["qh"][:, sl].astype(BF16), z["kh"][:, sl].astype(BF16), NT), 0.0)
                o = _dot(z["qe"][:, sl].astype(BF16), s_prev.astype(BF16), NT) + _dot(att.astype(BF16), vb, NN)
                st[hh] = s_prev * z["ebl"][:, sl] + _dot(vb, z["kd"][:, sl].astype(BF16), TN_)
                xh, _ = _rms(o)
                gh = gg[:, sl]
                o_ref[rows, sl] = o
                yb_ref[rows, sl] = (xh * gnv * gh * _sig(gh)).astype(BF16)
            return carry

        lax.fori_loop(0, NCH, chunk, 0, unroll=True)

    def col(j):
        return pl.BlockSpec((TM, HGW), lambda i, j=j: (i, j))

    return pl.pallas_call(
        body, name="hgrn_fwd", grid=(nt,),
        in_specs=[col(0), col(1), col(2), col(3), pl.BlockSpec((2, HGW), lambda i: (0, 0)),
                  pl.BlockSpec((1, HD), lambda i: (0, 0))],
        out_specs=[col(0), col(0), pl.BlockSpec((NCH, HEADS, HD, HD), lambda i: (i, 0, 0, 0))],
        out_shape=[jax.ShapeDtypeStruct((t, HGW), BF16), jax.ShapeDtypeStruct((t, HGW), F32),
                   jax.ShapeDtypeStruct((t // CH, HEADS, HD, HD), F32)],
        scratch_shapes=[pltpu.VMEM((HEADS, HD, HD), F32)],
        compiler_params=_cparams(("arbitrary",)),
    )(proj, proj, proj, proj, lbl, gn)


def _hgrn_bwd(dyb, proj, o, states, lbl, gn, dproj):
    t = proj.shape[0]
    nt = t // TM

    def body(dy_ref, q_ref, f_ref, v_ref, g_ref, o_ref, st_ref, lbl_ref, gn_ref, _,
             dp_ref, dgn_ref, dl_ref, dst, dlb):
        i = pl.program_id(0)

        @pl.when(i == 0)
        def _():
            dst[...] = jnp.zeros_like(dst)
            dlb[...] = jnp.zeros_like(dlb)
            dgn_ref[...] = jnp.zeros_like(dgn_ref)

        l = lbl_ref[...]
        lb = _sig(l[0:1, :] - l[1:2, :])
        gnv = gn_ref[...]
        last = lax.broadcasted_iota(jnp.int32, (CH, HD), 0) == CH - 1

        def chunk(cc, carry):
            c = NCH - 1 - cc
            off = pl.multiple_of(c * CH, CH)
            rows = pl.ds(off, CH)
            qr, fr = q_ref[rows, :].astype(F32), f_ref[rows, :].astype(F32)
            z = _hg_chunk(qr, fr, lb)
            v, gg, ov, dy = v_ref[rows, :], g_ref[rows, :].astype(F32), o_ref[rows, :], dy_ref[rows, :].astype(F32)
            dqs, dks, dbs, dvs, dgs = [], [], [], [], []
            dgn = jnp.zeros((1, HD), F32)
            for hh in range(HEADS):
                sl = slice(hh * HD, (hh + 1) * HD)
                s_prev, ds_new = st_ref[c, hh], dst[hh]
                qe, qh, kh, kd = z["qe"][:, sl], z["qh"][:, sl], z["kh"][:, sl], z["kd"][:, sl]
                ebl = z["ebl"][:, sl]
                gh, dyh = gg[:, sl], dy[:, sl]
                xh, rr = _rms(ov[:, sl])
                sg = _sig(gh)
                dyn = dyh * (gh * sg)
                dgs.append(dyh * (xh * gnv) * (sg * (1.0 + gh * (1.0 - sg))))
                dgn = dgn + jnp.sum(dyn * xh, axis=0, keepdims=True)
                dxh = dyn * gnv
                do = (rr * (dxh - xh * jnp.mean(dxh * xh, axis=-1, keepdims=True))).astype(BF16)
                vb, dsb = v[:, sl].astype(BF16), ds_new.astype(BF16)
                qeb, qhb, khb, kdb = (a.astype(BF16) for a in (qe, qh, kh, kd))
                att = jnp.where(z["causal"], _dot(qhb, khb, NT), 0.0).astype(BF16)
                datt = jnp.where(z["causal"], _dot(do, vb, NT), 0.0).astype(BF16)
                dvs.append(_dot(att, do, TN_) + _dot(kdb, dsb, NT))
                dqe = _dot(do, s_prev.astype(BF16), NN)
                dqh = _dot(datt, khb, NN)
                dkh = _dot(datt, qhb, TN_)
                dkd = _dot(vb, dsb, NN)
                qe, qh, kh, kd = (a.astype(F32) for a in (qeb, qhb, khb, kdb))
                dbl = (jnp.sum(dkd * kd, axis=0, keepdims=True)
                       + jnp.sum(ds_new * s_prev, axis=0, keepdims=True) * ebl)
                dqs.append(dqe * z["eb"][:, sl] + dqh * z["ebm"][:, sl])
                dks.append(dkh * z["ekm"][:, sl] + dkd * z["ekl"][:, sl])
                dbs.append(dqe * qe + dqh * qh - dkh * kh - dkd * kd + jnp.where(last, dbl, 0.0))
                dst[hh] = _dot(do, qeb, TN_) + ds_new * ebl
            dgn_ref[...] += dgn
            dq, dk, db = (jnp.concatenate(x, axis=1) for x in (dqs, dks, dbs))
            dlf = _tri_dot(z["upper"], db)
            sf, fg, sq = z["sf"], z["fg"], z["sq"]
            dmix = dlf / fg - dk
            dsf = dmix * (1.0 - lb)
            dlb[...] += jnp.sum(dmix * (1.0 - sf), axis=0, keepdims=True)
            dp_ref[rows, 0:HGW] = (dq * (sq * (1.0 + qr * (1.0 - sq)))).astype(BF16)
            dp_ref[rows, HGW:2 * HGW] = (dsf * sf * (1.0 - sf)).astype(BF16)
            dp_ref[rows, 2 * HGW:3 * HGW] = jnp.concatenate(dvs, axis=1).astype(BF16)
            dp_ref[rows, 3 * HGW:4 * HGW] = jnp.concatenate(dgs, axis=1).astype(BF16)
            return carry

        lax.fori_loop(0, NCH, chunk, 0, unroll=True)
        dl0 = dlb[...] * lb * (1.0 - lb)
        dl_ref[0:1, :] = dl0
        dl_ref[1:2, :] = -dl0

    def col(j):
        return pl.BlockSpec((TM, HGW), lambda i, j=j: (nt - 1 - i, j))

    return pl.pallas_call(
        body, name="hgrn_bwd", grid=(nt,),
        in_specs=[col(0), col(0), col(1), col(2), col(3), col(0),
                  pl.BlockSpec((NCH, HEADS, HD, HD), lambda i: (nt - 1 - i, 0, 0, 0)),
                  pl.BlockSpec((2, HGW), lambda i: (0, 0)), pl.BlockSpec((1, HD), lambda i: (0, 0)), ANY],
        out_specs=[pl.BlockSpec((TM, 4 * HGW), lambda i: (nt - 1 - i, 0)),
                   pl.BlockSpec((1, HD), lambda i: (0, 0)), pl.BlockSpec((2, HGW), lambda i: (0, 0))],
        out_shape=[jax.ShapeDtypeStruct((t, D_INP), BF16), jax.ShapeDtypeStruct((1, HD), F32),
                   jax.ShapeDtypeStruct((2, HGW), F32)],
        scratch_shapes=[pltpu.VMEM((HEADS, HD, HD), F32), pltpu.VMEM((1, HGW), F32)],
        input_output_aliases={9: 0},
        compiler_params=_cparams(("arbitrary",)),
    )(dyb, proj, proj, proj, proj, o, states, lbl, gn, dproj)


def _rotate_w_in(w):
    return jnp.concatenate([w[:, ROT:], jnp.zeros((w.shape[0], COL_PAD), w.dtype), w[:, :ROT]], axis=1)


def _unrotate_g_in(g):
    return jnp.concatenate([g[:, D_INP - ROT:], g[:, :D_IN - ROT]], axis=1)


SB = RGW // 2


def _bd_gates(t):
    tm = _pick(t, (768, 256))

    def col(j):
        return 2 * (j % 2) + j // 2

    return ((t // tm, 4, 1), (tm, SB, SB),
            (lambda i, j, k: (i, j // 2), lambda i, j, k: (j // 2, col(j)), lambda i, j, k: (i, col(j))))


def _bd_dw(t):
    tk = _pick(t, (1408, 768))
    return (1, 4, t // tk), (SB, SB, tk), (lambda i, j, k: (k, j % 2), lambda i, j, k: (k, j), lambda i, j, k: (j % 2, j))


def _block_diag(w):
    eye = jnp.eye(RG_BLOCKS, dtype=w.dtype)
    return (w[:, :, None, :] * eye[:, None, :, None]).reshape(RGW, RGW)


def _diag_blocks(wd):
    w4 = wd.reshape(RG_BLOCKS, RG_BLOCK, RG_BLOCKS, RG_BLOCK)
    return jnp.stack([w4[n, :, n, :] for n in range(RG_BLOCKS)])


def _local_step(h0, target, wts, small, hosts=None, finalize=None):
    hosts = hosts or {}
    carried = {}
    g = {}
    t = h0.shape[0]

    def mm(a, b, mode, out_dtype, name, **kw):
        if name not in hosts:
            return _mm(a, b, mode, out_dtype, name, **kw)
        host = hosts[name](g)
        out, res = _mm(a, b, mode, out_dtype, name, host=host, **kw)
        carried[name] = (host.ins, res)
        return out

    w_in = wts["w_in"]
    wax_d = jnp.concatenate([_block_diag(small["rg_wa"].astype(BF16)), _block_diag(small["rg_wx"].astype(BF16))], axis=1)
    ba, bx, lam = small["rg_ba"], small["rg_bx"], small["rg_lambda"]
    lbl, gn = small["hg_lb_logits"], small["hg_norm_g"]
    conv_w, conv_b = small["conv_w"], small["conv_b"]

    tmm = _pick(t, (768, 256))
    proj, hn1 = mm(h0, w_in, "nn", BF16, "mm_proj", pro=_norm_pro(small["norm1_g"]))
    if finalize is not None:
        wts = finalize(carried["mm_proj"][1])
    w_pa, w_pb, w_out, w_fi, w_fd = (wts[k] for k in ("w_proj_a", "w_proj_b", "w_out", "w_ffn_in", "w_ffn_down"))
    xcb = _conv_fwd(proj, conv_w, conv_b)
    gri = mm(xcb, wax_d, "nn", F32, "mm_rg_gates", sparse=_bd_gates(t))
    hrg, ya = _rg_fwd(gri, xcb, proj, ba, bx, lam)
    yb, o, states = _hgrn_fwd(proj, lbl, gn)
    pa = mm(ya, w_pa, "nn", BF16, "mm_pa")
    pb = mm(yb, w_pb, "nn", BF16, "mm_pb")
    h1, merged = mm(pa, w_out, "nn", F32, "mm_out", resid=h0, pro=_merge_pro(pb, proj, tmm), tiles=(tmm, D))
    (gu, act), hn2 = mm(h1, w_fi, "nn", BF16, "mm_ffn_in", pro=_norm_pro(small["norm2_g"]), epi=_swiglu_epi(t),
                        tiles=(FM, 2 * FH))
    dh2, dh2b, loss, g["norm_f_g"] = mm(act, w_fd, "nn", F32, "mm_ffn_down", resid=h1,
                                        epi=_final_loss_epi(small["norm_f_g"], target, t, tmm), tiles=(tmm, D))

    g["w_ffn_down"] = mm(act, dh2b, "tn", BF16, "mm_d_wfd")
    (dgu,) = mm(dh2b, w_fd, "nt", BF16, "mm_d_act", epi=_swiglu_bwd_epi(gu), tiles=(FM, FH))
    g["w_ffn_in"] = mm(hn2, dgu, "tn", BF16, "mm_d_wfi")
    dh1, dh1b, g["norm2_g"] = mm(dgu, w_fi, "nt", BF16, "mm_d_hn2", epi=_rms_bwd_epi(h1, small["norm2_g"], dh2, tmm),
                                 tiles=(tmm, D))
    g["w_out"] = mm(merged, dh1b, "tn", BF16, "mm_d_wout")
    dpa, dpb, dproj = mm(dh1b, w_out, "nt", BF16, "mm_d_merged", epi=_merge_bwd_epi(pa, pb, proj, tmm), tiles=(tmm, D))
    dproj = _zero_pad_cols(dproj)
    g["w_proj_a"] = mm(ya, dpa, "tn", BF16, "mm_d_wpa")
    g["w_proj_b"] = mm(yb, dpb, "tn", BF16, "mm_d_wpb")
    dya = mm(dpa, w_pa, "nt", BF16, "mm_d_ya")
    dyb = mm(dpb, w_pb, "nt", BF16, "mm_d_yb")
    dproj, g["hg_norm_g"], g["hg_lb_logits"] = _hgrn_bwd(dyb, proj, o, states, lbl, gn, dproj)
    dgri, dxc, dproj, g["rg_ba"], g["rg_bx"], g["rg_lambda"] = _rg_bwd(dya, hrg, gri, xcb, proj, ba, bx, lam, dproj)
    dwax = mm(xcb, dgri, "tn", F32, "mm_d_wax", sparse=_bd_dw(t))
    g["rg_wa"], g["rg_wx"] = _diag_blocks(dwax[:, :RGW]), _diag_blocks(dwax[:, RGW:])
    dxc = mm(dgri, wax_d, "nt", BF16, "mm_d_xc", resid=dxc)
    dproj, g["conv_w"], g["conv_b"] = _conv_bwd(dxc, proj, conv_w, dproj)
    g["w_in"] = mm(hn1, dproj, "tn", BF16, "mm_d_win")
    dhn1 = mm(dproj, w_in, "nt", BF16, "mm_d_hn1")
    grad_x, g["meta_tokens"], g["norm1_g"] = _rmsnorm1_bwd(dhn1, h0, small["norm1_g"], dh1)
    return loss, grad_x, g, carried


def _adamw_math(w, g, m, v):
    m = B1 * m + (1.0 - B1) * g
    v = B2 * v + (1.0 - B2) * (g * g)
    m_hat = m / (1.0 - B1 ** STEP)
    v_hat = v / (1.0 - B2 ** STEP)
    return -LR * (m_hat / (jnp.sqrt(v_hat) + ADAM_EPS) + WD * w), m, v


def _adamw(w, g, m, v, name):
    r, c = w.shape
    tr = _pick(r, (256, 352, 320, 128, 64, 32, 16, 8))

    def body(w_ref, g_ref, m_ref, v_ref, d_ref, mo_ref, vo_ref):
        d_ref[...], mo_ref[...], vo_ref[...] = _adamw_math(w_ref[...], g_ref[...], m_ref[...], v_ref[...])

    blk = pl.BlockSpec((tr, c), lambda i: (i, 0))
    return pl.pallas_call(
        body, name=name, grid=(r // tr,), in_specs=[blk] * 4, out_specs=[blk] * 3,
        out_shape=[jax.ShapeDtypeStruct((r, c), F32)] * 3,
        compiler_params=_cparams(("parallel",)),
    )(w, g, m, v)


def _adamw_rows(recv, params, extras, name):
    _, _, width = recv.shape
    n_par = len(params)

    def body(*refs):
        r_ref = refs[0]
        ins = refs[1:1 + 3 * n_par]
        outs = refs[1 + 3 * n_par:]
        g_all = r_ref[0]
        for p in range(1, 8):
            g_all = g_all + r_ref[p]
        for q, (row0, w, _, _) in enumerate(params):
            g = g_all[row0:row0 + w.shape[0], :]
            w_ref, m_ref, v_ref = ins[3 * q:3 * q + 3]
            g_ref, d_ref, mo_ref, vo_ref = outs[4 * q:4 * q + 4]
            g_ref[...] = g
            d_ref[...], mo_ref[...], vo_ref[...] = _adamw_math(w_ref[...], g, m_ref[...], v_ref[...])
        for q, (row0, n) in enumerate(extras):
            outs[4 * n_par + q][...] = g_all[row0:row0 + n, :]

    shapes = [jax.ShapeDtypeStruct(w.shape, F32) for _, w, _, _ in params for _ in range(4)]
    shapes += [jax.ShapeDtypeStruct((n, width), F32) for _, n in extras]
    vm = pl.BlockSpec(memory_space=pltpu.VMEM)
    outs = pl.pallas_call(
        body, name=name, in_specs=[vm] * (1 + 3 * n_par), out_specs=[vm] * len(shapes), out_shape=shapes,
        compiler_params=pltpu.CompilerParams(vmem_limit_bytes=VMEM_LIMIT),
    )(recv, *[a for _, w, m, v in params for a in (w, m, v)])
    return [tuple(outs[4 * q:4 * q + 4]) for q in range(n_par)], list(outs[4 * n_par:])


def _col_block(weight, px, py):
    return 2 * py + px if weight == "w_ffn_in" else 2 * px + py


def _sum_place(recv, gfull, weight, xyc, name):
    _, r, c = recv.shape
    tr = _pick(r, (128, 176, 160, 64, 32, 16))
    nb = r // tr

    def body(_, own_ref, r_ref, o_ref):
        g = own_ref[...].astype(F32)
        for p in range(7):
            g = g + r_ref[p].astype(F32)
        o_ref[...] = g

    if weight in COL_SHARDED:
        own = pl.BlockSpec((tr, c), lambda i, s: (s[2] * nb + i, _col_block(weight, s[0], s[1])))
    else:
        own = pl.BlockSpec((tr, c), lambda i, s: ((4 * s[0] + 2 * s[1] + s[2]) * nb + i, 0))
    return pl.pallas_call(
        body, name=name,
        grid_spec=pltpu.PrefetchScalarGridSpec(
            num_scalar_prefetch=1, grid=(nb,),
            in_specs=[own, pl.BlockSpec((7, tr, c), lambda i, s: (0, i, 0))],
            out_specs=pl.BlockSpec((tr, c), lambda i, s: (s[2] * nb + i, 0))),
        out_shape=jax.ShapeDtypeStruct((2 * r, c), F32),
        compiler_params=_cparams(("arbitrary",)),
    )(xyc, gfull, recv)


def _place_own(shard, slot, dtype, name):
    r, c = shard.shape
    tr = _pick(r, (256, 352, 320, 128, 32))

    def body(_, x_ref, o_ref):
        o_ref[...] = x_ref[...].astype(dtype)

    return pl.pallas_call(
        body, name=name,
        grid_spec=pltpu.PrefetchScalarGridSpec(
            num_scalar_prefetch=1, grid=(r // tr,),
            in_specs=[pl.BlockSpec((tr, c), lambda i, s: (i, 0))],
            out_specs=pl.BlockSpec((None, tr, c), lambda i, s: (s[0], i, 0))),
        out_shape=jax.ShapeDtypeStruct((4, r, c), dtype),
        compiler_params=_cparams(("arbitrary",)),
    )(slot, shard)


def _place_cols(shard, slot, name):
    r, c = shard.shape
    tr = _pick(r, (256, 128))

    def body(_, x_ref, o_ref):
        o_ref[...] = x_ref[...].astype(BF16)

    return pl.pallas_call(
        body, name=name,
        grid_spec=pltpu.PrefetchScalarGridSpec(
            num_scalar_prefetch=1, grid=(r // tr,),
            in_specs=[pl.BlockSpec((tr, c), lambda i, s: (i, 0))],
            out_specs=pl.BlockSpec((tr, c), lambda i, s: (i, 2 * (s[0] % 2) + s[0] // 2))),
        out_shape=jax.ShapeDtypeStruct((r, 4 * c), BF16),
        compiler_params=_cparams(("arbitrary",)),
    )(slot, shard)


def _place_w_in(shard, slot):
    r, c = shard.shape
    nb, npad = c // 128, COL_PAD // 128
    rot_blocks, in_blocks = ROT // 128, D_IN // 128

    def out_block(j, s):
        nat = s[0] * nb + jnp.minimum(j, nb - 1)
        rot = jnp.where(nat >= rot_blocks, nat - rot_blocks, nat + (D_INP - ROT) // 128)
        return jnp.where(j < nb, rot, (in_blocks - rot_blocks) + (j - nb))

    def body(_, x_ref, o_ref):
        o_ref[...] = jnp.where(pl.program_id(0) < nb, x_ref[...], 0.0).astype(BF16)

    return pl.pallas_call(
        body, name="place_w_in",
        grid_spec=pltpu.PrefetchScalarGridSpec(
            num_scalar_prefetch=1, grid=(nb + npad,),
            in_specs=[pl.BlockSpec((r, 128), lambda j, s: (0, jnp.minimum(j, nb - 1)))],
            out_specs=pl.BlockSpec((r, 128), lambda j, s: (0, out_block(j, s)))),
        out_shape=jax.ShapeDtypeStruct((r, D_INP), BF16),
        compiler_params=_cparams(("arbitrary",)),
    )(slot, shard)


BIG = ("w_in", "w_proj_a", "w_proj_b", "w_out", "w_ffn_in", "w_ffn_down")
COL_SHARDED = ("w_in", "w_ffn_in")


def _coords():
    return lax.axis_index("x"), lax.axis_index("y"), lax.axis_index("c")


W_IN_HEAD = 384


def _shard_views(b, layout, chip, half_of):
    if layout == "slots":
        half = b.shape[1] // 2
        return [b.at[chip, pl.ds(pl.multiple_of(half_of * half, 8), half)]]
    half = b.shape[0] // 2
    rows = pl.ds(pl.multiple_of(half_of * half, 8), half)
    if layout == "cols":
        c = b.shape[1] // 4
        return [b.at[rows, pl.ds(pl.multiple_of((2 * (chip % 2) + chip // 2) * c, 128), c)]]
    c = D_IN // 4
    views = []
    for first, width in ((0, W_IN_HEAD), (W_IN_HEAD, c - W_IN_HEAD)):
        nat = chip * c + first
        rot = jnp.where(nat >= ROT, nat - ROT, nat + D_INP - ROT)
        views.append(b.at[rows, pl.ds(pl.multiple_of(rot, 128), width)])
    return views


def _gather_host(bufs, layouts):
    n = len(bufs)

    def place():
        x, y, c = _coords()
        return x, y, c, [(1 - x, y), (x, 1 - y), (1 - x, 1 - y)]

    def copies(b, sems, w, k, chip, half_of, to):
        return [pltpu.make_async_remote_copy(src_ref=v, dst_ref=v, send_sem=sems[0].at[w, k, p], recv_sem=sems[1].at[w, k, p],
                                             device_id=to, device_id_type=MESH)
                for p, v in enumerate(_shard_views(b, layouts[w], chip, half_of))]

    def start(_, outs, sems):
        x, y, c, chips = place()
        for w, b in enumerate(outs):
            for k, (cx, cy) in enumerate(chips):
                for cp in copies(b, sems, w, k, 2 * x + y, c, (cx, cy, c)):
                    cp.start()

    def mid(_, outs, sems):
        x, y, c, chips = place()
        for w, b in enumerate(outs):
            for k, (cx, cy) in enumerate(chips):
                for cp in copies(b, sems, w, k, 2 * cx + cy, c, (cx, cy, c)):
                    cp.wait_recv()
                for cp in copies(b, sems, w, 3 + k, 2 * cx + cy, c, (x, y, 1 - c)):
                    cp.start()

    def finish(_, outs, sems):
        x, y, c, chips = place()
        for w, b in enumerate(outs):
            for k, (cx, cy) in enumerate(chips):
                for cp in copies(b, sems, w, 3 + k, 2 * cx + cy, 1 - c, (x, y, 1 - c)):
                    cp.wait_recv()
        for w, b in enumerate(outs):
            for k, (cx, cy) in enumerate(chips):
                for cp in copies(b, sems, w, k, 2 * x + y, c, (cx, cy, c)) + copies(b, sems, w, 3 + k, 2 * cx + cy, c, (x, y, 1 - c)):
                    cp.wait_send()

    return _Host(ins=list(bufs), out_shapes=[jax.ShapeDtypeStruct(b.shape, b.dtype) for b in bufs],
                 aliases={i: i for i in range(n)},
                 sems=[pltpu.SemaphoreType.DMA((n, 6, 2)), pltpu.SemaphoreType.DMA((n, 6, 2))],
                 start=start, mid=mid, finish=finish)


def _run_host(host, name):
    n_in, n_out = len(host.ins), len(host.out_shapes)

    def body(*refs):
        ins, outs, sems = refs[:n_in], refs[n_in:n_in + n_out], refs[n_in + n_out:]
        host.start(ins, outs, sems)
        if host.mid is not None:
            host.mid(ins, outs, sems)
        host.finish(ins, outs, sems)

    return pl.pallas_call(
        body, name=name, in_specs=[ANY] * n_in, out_specs=[ANY] * n_out, out_shape=list(host.out_shapes),
        scratch_shapes=list(host.sems), input_output_aliases=dict(host.aliases),
    )(*host.ins)


def _peer(x, y, c, k):
    fx, fy, fc = (k >> 2) & 1, (k >> 1) & 1, k & 1
    return (x ^ fx if fx else x, y ^ fy if fy else y, c ^ fc if fc else c)


def _sub_shape(name, full_shape):
    r, c = full_shape
    return (r // 2, c // 4) if name in COL_SHARDED else (r // 8, c)


def _exchange_host(names, grads):
    n = len(names)
    shapes = [_sub_shape(k, g.shape) for k, g in zip(names, grads)]

    def copy(ins, outs, sems, w, k):
        x, y, c = _coords()
        px, py, pc = _peer(x, y, c, k)
        sr, sc = shapes[w]
        if names[w] in COL_SHARDED:
            col = _col_block(names[w], px, py) * sc
            src = ins[w].at[pl.ds(pl.multiple_of(pc * sr, 16), sr), pl.ds(pl.multiple_of(col, 128), sc)]
        else:
            src = ins[w].at[pl.ds(pl.multiple_of((4 * px + 2 * py + pc) * sr, 16), sr)]
        return pltpu.make_async_remote_copy(
            src_ref=src, dst_ref=outs[w].at[k - 1], send_sem=sems[0].at[w, k - 1], recv_sem=sems[1].at[w, k - 1],
            device_id=(px, py, pc), device_id_type=MESH)

    def start(ins, outs, sems):
        for w in range(n):
            for k in range(1, 8):
                copy(ins, outs, sems, w, k).start()

    def finish(ins, outs, sems):
        for w in range(n):
            for k in range(1, 8):
                copy(ins, outs, sems, w, k).wait_recv()
        for w in range(n):
            for k in range(1, 8):
                copy(ins, outs, sems, w, k).wait_send()

    return _Host(ins=list(grads), out_shapes=[jax.ShapeDtypeStruct((7,) + s, g.dtype) for s, g in zip(shapes, grads)],
                 aliases={}, sems=[pltpu.SemaphoreType.DMA((n, 7)), pltpu.SemaphoreType.DMA((n, 7))],
                 start=start, mid=None, finish=finish)


def _sibling_host(bufs):
    n = len(bufs)

    def copy(outs, sems, w, half_of):
        x, y, c = _coords()
        half = outs[w].shape[0] // 2
        rows = outs[w].at[pl.ds(pl.multiple_of(half_of * half, 8), half)]
        return pltpu.make_async_remote_copy(src_ref=rows, dst_ref=rows, send_sem=sems[0].at[w], recv_sem=sems[1].at[w],
                                            device_id=(x, y, 1 - c), device_id_type=MESH)

    def start(_, outs, sems):
        c = lax.axis_index("c")
        for w in range(n):
            copy(outs, sems, w, c).start()

    def finish(_, outs, sems):
        c = lax.axis_index("c")
        for w in range(n):
            copy(outs, sems, w, 1 - c).wait_recv()
        for w in range(n):
            copy(outs, sems, w, c).wait_send()

    return _Host(ins=list(bufs), out_shapes=[jax.ShapeDtypeStruct(b.shape, b.dtype) for b in bufs],
                 aliases={i: i for i in range(n)},
                 sems=[pltpu.SemaphoreType.DMA((n,)), pltpu.SemaphoreType.DMA((n,))], start=start, mid=None, finish=finish)


def _pack_host(pack):
    def me_of():
        x, y, c = _coords()
        return x, y, c, 4 * x + 2 * y + c

    def copy(ins, outs, sems, k, slot):
        x, y, c, _ = me_of()
        return pltpu.make_async_remote_copy(src_ref=ins[0], dst_ref=outs[0].at[slot], send_sem=sems[0].at[k - 1],
                                            recv_sem=sems[1].at[k - 1], device_id=_peer(x, y, c, k), device_id_type=MESH)

    def start(ins, outs, sems):
        me = me_of()[3]
        pltpu.make_async_copy(ins[0], outs[0].at[me], sems[2]).start()
        for k in range(1, 8):
            copy(ins, outs, sems, k, me).start()

    def finish(ins, outs, sems):
        x, y, c, me = me_of()
        for k in range(1, 8):
            px, py, pc = _peer(x, y, c, k)
            copy(ins, outs, sems, k, 4 * px + 2 * py + pc).wait_recv()
        for k in range(1, 8):
            copy(ins, outs, sems, k, me).wait_send()
        pltpu.make_async_copy(ins[0], outs[0].at[me], sems[2]).wait()

    return _Host(ins=[pack], out_shapes=[jax.ShapeDtypeStruct((8,) + pack.shape, F32)], aliases={},
                 sems=[pltpu.SemaphoreType.DMA((7,)), pltpu.SemaphoreType.DMA((7,)), pltpu.SemaphoreType.DMA],
                 start=start, mid=None, finish=finish)


def _join_hosts(hosts):
    ins, outs, sems, aliases, spans = [], [], [], {}, []
    for h in hosts:
        spans.append((len(ins), len(h.ins), len(outs), len(h.out_shapes), len(sems), len(h.sems)))
        for i_in, i_out in h.aliases.items():
            aliases[len(ins) + i_in] = len(outs) + i_out
        ins, outs, sems = ins + list(h.ins), outs + list(h.out_shapes), sems + list(h.sems)

    def phase(which):
        fns = [getattr(h, which) for h in hosts]
        if all(f is None for f in fns):
            return None

        def run(i_refs, o_refs, s_refs):
            for f, (i0, n_i, o0, n_o, s0, n_s) in zip(fns, spans):
                if f is not None:
                    f(i_refs[i0:i0 + n_i], o_refs[o0:o0 + n_o], s_refs[s0:s0 + n_s])

        return run

    return _Host(ins, outs, aliases, sems, phase("start"), phase("mid"), phase("finish"))


SHARDED_SMALL = ("meta_tokens", "conv_w")

ROWS = {"conv_b": (1, RGW), "rg_ba": (1, RGW), "rg_bx": (1, RGW), "rg_lambda": (1, RGW), "conv_w": (4, RGW),
        "norm2_g": (1, D), "norm_f_g": (1, D), "hg_lb_logits": (2, D),
        "rg_wa": (RGW * RG_BLOCK // 128, 128), "rg_wx": (RGW * RG_BLOCK // 128, 128), "hg_norm_g": (1, 128),
        "norm1_g": (1, D), "meta_tokens": (N_META, D), "loss": (1, D)}
PACKS = {"early_wide": ("conv_b", "rg_ba", "rg_bx", "rg_lambda", "conv_w"),
         "early_model": ("norm2_g", "norm_f_g", "hg_lb_logits"),
         "early_lane": ("rg_wa", "rg_wx", "hg_norm_g"),
         "late": ("norm1_g", "meta_tokens", "loss")}
EARLY = ("early_wide", "early_model", "early_lane")
NO_UPDATE = ("conv_w", "meta_tokens", "loss")


def _pack_rows(vals, names):
    parts, first, row = [], {}, 0
    for k in names:
        first[k] = row
        parts.append(vals[k].reshape(ROWS[k]).astype(F32))
        row += ROWS[k][0]
    if row % 8:
        parts.append(jnp.zeros((-row % 8, ROWS[names[0]][1]), F32))
    return jnp.concatenate(parts, axis=0), first


ORDER = ("meta_tokens", "norm1_g", "w_in", "conv_w", "conv_b", "rg_wa", "rg_ba", "rg_wx", "rg_bx", "rg_lambda",
         "hg_lb_logits", "hg_norm_g", "w_proj_a", "w_proj_b", "w_out", "norm2_g", "w_ffn_in", "w_ffn_down", "norm_f_g")


def kernel(x, meta_tokens, norm1_g, w_in, conv_w, conv_b, rg_wa, rg_ba, rg_wx, rg_bx, rg_lambda, hg_lb_logits, hg_norm_g, w_proj_a, w_proj_b, w_out, norm2_g, w_ffn_in, w_ffn_down, norm_f_g, loss_target, m_meta_tokens, m_norm1_g, m_w_in, m_conv_w, m_conv_b, m_rg_wa, m_rg_ba, m_rg_wx, m_rg_bx, m_rg_lambda, m_hg_lb_logits, m_hg_norm_g, m_w_proj_a, m_w_proj_b, m_w_out, m_norm2_g, m_w_ffn_in, m_w_ffn_down, m_norm_f_g, v_meta_tokens, v_norm1_g, v_w_in, v_conv_w, v_conv_b, v_rg_wa, v_rg_ba, v_rg_wx, v_rg_bx, v_rg_lambda, v_hg_lb_logits, v_hg_norm_g, v_w_proj_a, v_w_proj_b, v_w_out, v_norm2_g, v_w_ffn_in, v_w_ffn_down, v_norm_f_g):
    args = dict(locals())
    w = {k: args[k] for k in ORDER}
    m = {k: args["m_" + k] for k in ORDER}
    v = {k: args["v_" + k] for k in ORDER}
    xi, yi, ci = _coords()
    chip = 2 * xi + yi
    slot = jnp.reshape(chip, (1,)).astype(jnp.int32)
    xyc = jnp.stack([xi, yi, ci]).astype(jnp.int32)

    tiny = jnp.zeros((32, 384), F32)
    tiny = tiny.at[0:N_META, 0:256].set(meta_tokens).at[N_META:N_META + 4, 0:320].set(conv_w[0])
    later = [k for k in BIG if k != "w_in"]
    layouts = {k: "cols" if k in COL_SHARDED else "slots" for k in later}
    slots = {k: _place_cols(w[k][0], slot, "place_" + k) if layouts[k] == "cols"
             else _place_own(w[k][0], slot, BF16, "place_" + k) for k in later}
    w_in_full, tiny_all = _run_host(
        _gather_host([_place_w_in(w["w_in"][0], slot), _place_own(tiny, slot, F32, "place_tiny")], ["w_in", "slots"]),
        "allgather_w_in")

    def full_matrix(k, gth):
        return gth if layouts[k] == "cols" else gth.reshape(-1, gth.shape[2])

    meta_full = jnp.transpose(tiny_all[:, 0:N_META, 0:256], (1, 0, 2)).reshape(N_META, D)
    conv_w_full = jnp.transpose(tiny_all[:, N_META:N_META + 4, 0:320], (1, 0, 2)).reshape(4, RGW)

    small = dict(norm1_g=norm1_g, conv_w=conv_w_full, conv_b=conv_b, rg_wa=rg_wa[0], rg_ba=rg_ba, rg_wx=rg_wx[0],
                 rg_bx=rg_bx, rg_lambda=rg_lambda, hg_lb_logits=hg_lb_logits, hg_norm_g=hg_norm_g,
                 norm2_g=norm2_g, norm_f_g=norm_f_g.reshape(1, D))

    hosts = {
        "mm_proj": lambda g: _gather_host([slots[k] for k in later], [layouts[k] for k in later]),
        "mm_d_act": lambda g: _exchange_host(["w_ffn_down"], [g["w_ffn_down"]]),
        "mm_d_hn2": lambda g: _exchange_host(["w_ffn_in"], [g["w_ffn_in"]]),
        "mm_d_win": lambda g: _join_hosts([
            _exchange_host(["w_out", "w_proj_a", "w_proj_b"], [g["w_out"], g["w_proj_a"], g["w_proj_b"]]),
            *[_pack_host(_pack_rows(g, PACKS[p])[0]) for p in EARLY]]),
        "mm_d_hn1": lambda g: _exchange_host(["w_in"], [_unrotate_g_in(g["w_in"])]),
    }
    h0 = jnp.concatenate([jnp.zeros((PAD, D), F32), meta_full, x[0]], axis=0)
    loss_blk, grad_x, g, carried = _local_step(
        h0, loss_target[0], {"w_in": w_in_full}, small, hosts=hosts,
        finalize=lambda gathered: {k: full_matrix(k, gth) for k, gth in zip(later, gathered)})
    g["loss"] = jnp.broadcast_to(loss_blk[0:1, 0:1], (1, D))

    halves = {}
    sources = {"mm_d_act": ["w_ffn_down"], "mm_d_hn2": ["w_ffn_in"], "mm_d_win": ["w_out", "w_proj_a", "w_proj_b"],
               "mm_d_hn1": ["w_in"]}
    for name, keys in sources.items():
        partials, received = carried[name]
        for k, part, rec in zip(keys, partials, received):
            halves[k] = _sum_place(rec, part, k, xyc, "sum_" + k)
    late, _ = _pack_rows(g, PACKS["late"])
    tail = _run_host(_join_hosts([_sibling_host([halves[k] for k in BIG]), _pack_host(late)]), "swap_and_late_pack")
    g_big = dict(zip(BIG, tail[:len(BIG)]))
    recv_packs = dict(zip(EARLY, carried["mm_d_win"][1][3:]), late=tail[len(BIG)])

    grad, delta, new_m, new_v, summed = {}, {}, {}, {}, {}
    for pack, names in PACKS.items():
        row, params, extras = 0, [], []
        for k in names:
            if k in NO_UPDATE:
                extras.append((row, ROWS[k][0]))
            else:
                params.append((row,) + tuple(d[k].reshape(ROWS[k]) for d in (w, m, v)))
            row += ROWS[k][0]
        updated, rows_only = _adamw_rows(recv_packs[pack], params, extras, "adamw_" + pack)
        for k, res in zip([k for k in names if k not in NO_UPDATE], updated):
            grad[k], delta[k], new_m[k], new_v[k] = (a.reshape(w[k].shape) for a in res)
        summed.update(zip([k for k in names if k in NO_UPDATE], rows_only))
    loss = summed["loss"][0, 0]
    g_shard = {"meta_tokens": lax.dynamic_slice(summed["meta_tokens"], (0, chip * 256), (N_META, 256)),
               "conv_w": lax.dynamic_slice(summed["conv_w"], (0, chip * 320), (4, 320))}
    for k in BIG + SHARDED_SMALL:
        gk = g_big[k] if k in BIG else g_shard[k]
        wk, mk, vk = (d[k].reshape(gk.shape) for d in (w, m, v))
        dk, mk, vk = _adamw(wk, gk, mk, vk, "adamw_" + k)
        grad[k], delta[k], new_m[k], new_v[k] = (a.reshape(w[k].shape) for a in (gk, dk, mk, vk))

    return (loss, grad_x[None], *[grad[k] for k in ORDER], *[delta[k] for k in ORDER],
            *[new_m[k] for k in ORDER], *[new_v[k] for k in ORDER])
```

```python
import jax
import jax.numpy as jnp
from jax import lax
from jax.experimental import pallas as pl
from jax.experimental.pallas import tpu as pltpu

F32, BF16 = jnp.float32, jnp.bfloat16
D = 1024
N_META = 16
RGW = 1280
RG_BLOCKS, RG_BLOCK = 16, 80
RG_C = 8.0
HEADS, HD = 8, 128
HGW = HEADS * HD
DFF = 2816
D_IN = 2 * RGW + 6 * D
ROT = 2 * RGW
COL_PAD = 256
D_INP = D_IN + COL_PAD
EPS = 1e-6
CH = 64
TM = 256
ROW0 = TM
PAD = ROW0 - N_META
CT = RGW
RS = 16
NCT = RGW // CT
EXP_CLAMP = 80.0
VMEM_LIMIT = 56 * 1024 * 1024

LR, B1, B2, ADAM_EPS, WD, STEP = 0.001, 0.9, 0.999, 1e-08, 0.01, 10
MESH = pl.DeviceIdType.MESH
ANY = pl.BlockSpec(memory_space=pl.ANY)


def _cparams(sem):
    return pltpu.CompilerParams(dimension_semantics=sem, vmem_limit_bytes=VMEM_LIMIT)


def _pick(n, prefs):
    for p in prefs:
        if n % p == 0:
            return p
    return n


def _sig(x):
    return 0.5 * jnp.tanh(0.5 * x) + 0.5


def _dot(a, b, dims):
    return lax.dot_general(a, b, (dims, ((), ())), preferred_element_type=F32)


NN, NT, TN_ = ((1,), (0,)), ((1,), (1,)), ((0,), (0,))


class _Host:
    def __init__(self, ins, out_shapes, aliases, sems, start, mid, finish):
        self.ins, self.out_shapes, self.aliases, self.sems = ins, out_shapes, aliases, sems
        self.start, self.mid, self.finish = start, mid, finish


class _Epi:
    def __init__(self, ins, in_specs, out_shapes, out_specs, fn, sequential=False, split=1):
        self.ins, self.in_specs, self.out_shapes, self.out_specs, self.fn = ins, in_specs, out_shapes, out_specs, fn
        self.sequential = sequential
        self.split = split


class _Pro:
    def __init__(self, ins, in_specs, fn):
        self.ins, self.in_specs, self.fn = ins, in_specs, fn


def _norm_pro(gain):
    def fn(a_ref, ins):
        xh, _ = _rms(a_ref[...])
        return (xh * ins[0][...]).astype(BF16)

    return _Pro([gain], [pl.BlockSpec((1, gain.shape[1]), lambda i, j, k: (0, 0))], fn)


def _mm(a, b, mode, out_dtype, name, resid=None, host=None, epi=None, tiles=None, sparse=None, pro=None):
    if mode == "nn":
        (m, kd), n = a.shape, b.shape[1]
    elif mode == "nt":
        (m, kd), n = a.shape, b.shape[0]
    else:
        (kd, m), n = a.shape, b.shape[1]
    if mode == "tn":
        tm = _pick(m, (1024, 1280, 1408, 640, 512, 256, 128))
        tk = _pick(kd, (1408, 768, 512, 256, 128))
    else:
        tm = _pick(m, (768, 512, 640, 256, 128))
        tk = kd if kd <= 2816 else _pick(kd, (1792, 1408, 1024, 512, 256, 128))
    tn = _pick(n, (1792, 1408, 1280, 1024, 512, 256, 128))
    if tiles is not None:
        tm, tn = tiles
    a_map = (lambda i, j, k: (k, i)) if mode == "tn" else (lambda i, j, k: (i, k))
    b_map = (lambda i, j, k: (j, k)) if mode == "nt" else (lambda i, j, k: (k, j))
    o_map = lambda i, j, k: (i, j)
    if sparse is not None:
        (ni, nj, nk), (tm, tn, tk), (a_map, b_map, o_map) = sparse
    else:
        assert m % tm == 0 and n % tn == 0 and kd % tk == 0, (name, m, n, kd, tm, tn, tk)
        ni, nj, nk = m // tm, n // tn, kd // tk
    dims = {"nn": NN, "nt": NT, "tn": TN_}[mode]

    n_hin = len(host.ins) if host else 0
    n_hout = len(host.out_shapes) if host else 0
    n_res = 0 if resid is None else 1
    n_pro = 0 if pro is None else 1
    n_pin = len(pro.ins) if pro else 0
    n_ein = len(epi.ins) if epi else 0
    n_out = len(epi.out_shapes) if epi else 1
    assert not n_pro or (mode == "nn" and nk == 1)

    def finish(r, r_ref, e_in, o_refs, i, rows=slice(None), first=True):
        if resid is not None:
            r = r + r_ref[rows, :].astype(F32)
        if epi:
            epi.fn(r, e_in, o_refs, i, rows, first)
        else:
            o_refs[0][rows, :] = r.astype(out_dtype)

    def body(*refs):
        a_ref, b_ref = refs[:2]
        r_ref = refs[2] if resid is not None else None
        pos = 2 + n_res
        p_in = refs[pos:pos + n_pin]
        pos += n_pin
        e_in = refs[pos:pos + n_ein]
        pos += n_ein
        h_in = refs[pos:pos + n_hin]
        pos += n_hin
        o_ref = refs[pos:pos + n_out]
        pos += n_out
        hn_ref = refs[pos] if n_pro else None
        pos += n_pro
        h_out = refs[pos:pos + n_hout]
        scratch = refs[pos + n_hout:]
        n_acc = 1 if nk > 1 else 0
        h_sems = scratch[n_acc + n_pro:]
        i, j, k = pl.program_id(0), pl.program_id(1), pl.program_id(2)
        if host:
            @pl.when((i == 0) & (j == 0) & (k == 0))
            def _():
                host.start(h_in, h_out, h_sems)

            if host.mid is not None:
                @pl.when((i == (3 * ni) // 4) & (j == 0) & (k == 0))
                def _():
                    host.mid(h_in, h_out, h_sems)

        if n_pro:
            a_s = scratch[n_acc]

            @pl.when(j == 0)
            def _():
                a_s[...] = pro.fn(a_ref, p_in)
                hn_ref[...] = a_s[...]

            a_val = a_s[...]
        else:
            a_val = a_ref[...]

        if nk == 1 and epi and epi.split > 1 and mode != "tn":
            part = tm // epi.split
            for p in range(epi.split):
                rows = slice(p * part, (p + 1) * part)
                finish(_dot(a_val[rows, :], b_ref[...], dims), r_ref, e_in, o_ref, i, rows, p == 0)
        elif nk == 1:
            finish(_dot(a_val, b_ref[...], dims), r_ref, e_in, o_ref, i)
        else:
            acc = scratch[0]

            @pl.when(k == 0)
            def _():
                acc[...] = jnp.zeros_like(acc)

            acc[...] += _dot(a_val, b_ref[...], dims)

            @pl.when(k == nk - 1)
            def _():
                finish(acc[...], r_ref, e_in, o_ref, i)

        if host:
            @pl.when((i == ni - 1) & (j == nj - 1) & (k == nk - 1))
            def _():
                host.finish(h_in, h_out, h_sems)

    a_spec = pl.BlockSpec((tk, tm) if mode == "tn" else (tm, tk), a_map)
    b_spec = pl.BlockSpec((tn, tk) if mode == "nt" else (tk, tn), b_map)
    o_spec = pl.BlockSpec((tm, tn), o_map)
    in_specs, args = [a_spec, b_spec], [a, b]
    if resid is not None:
        in_specs.append(o_spec)
        args.append(resid)
    if n_pro:
        in_specs += list(pro.in_specs)
        args += list(pro.ins)
    out_shapes, out_specs = [jax.ShapeDtypeStruct((m, n), out_dtype)], [o_spec]
    if epi:
        in_specs += list(epi.in_specs)
        args += list(epi.ins)
        out_shapes, out_specs = list(epi.out_shapes), list(epi.out_specs)
    if n_pro:
        out_shapes.append(jax.ShapeDtypeStruct((m, kd), BF16))
        out_specs.append(pl.BlockSpec((tm, kd), lambda i, j, k: (i, 0)))
    scratch = ([pltpu.VMEM((tm, tn), F32)] if nk > 1 else []) + ([pltpu.VMEM((tm, kd), BF16)] if n_pro else [])
    sequential = host or n_pro or (epi and epi.sequential)
    outs = pl.pallas_call(
        body, name=name, grid=(ni, nj, nk),
        in_specs=in_specs + [ANY] * n_hin, out_specs=out_specs + [ANY] * n_hout,
        out_shape=out_shapes + (list(host.out_shapes) if host else []),
        scratch_shapes=scratch + (list(host.sems) if host else []),
        input_output_aliases=({2 + n_res + n_pin + n_ein + i_in: n_out + n_pro + i_out
                               for i_in, i_out in host.aliases.items()} if host else {}),
        compiler_params=_cparams(("arbitrary",) * 3 if sequential else ("parallel", "parallel", "arbitrary")),
    )(*args, *(host.ins if host else []))
    main = list(outs[:n_out]) if epi else outs[0]
    if n_pro:
        main = (main, outs[n_out])
    return (main, list(outs[n_out + n_pro:])) if host else main


def _rms(x):
    r = lax.rsqrt(jnp.mean(x * x, axis=-1, keepdims=True) + EPS)
    return x * r, r


def _rms_bwd_math(dhn, x, g):
    xh, r = _rms(x)
    dxh = dhn * g
    dx = r * (dxh - xh * jnp.mean(dxh * xh, axis=-1, keepdims=True))
    return dx, jnp.sum(dhn * xh, axis=0, keepdims=True)


def _row_specs(tm):
    return pl.BlockSpec((tm, D), lambda i, j, k: (i, 0)), pl.BlockSpec((1, D), lambda i, j, k: (0, 0))


def _final_loss_epi(gf, target, t, tm):
    parts = tm // TM

    def fn(x, ins, outs, i, rows, first):
        g_ref = ins[0]
        p = rows.start // TM
        t_ref = ins[1 + p]
        dh_ref, dhb_ref, loss_ref, dg_ref = outs

        if first:
            @pl.when(i == 0)
            def _():
                loss_ref[...] = jnp.zeros_like(loss_ref)
                dg_ref[...] = jnp.zeros_like(dg_ref)

        g = g_ref[...]
        xh, _ = _rms(x)
        err = jnp.where(i * parts + p > 0, xh * g - t_ref[...], 0.0)
        loss_ref[...] += 0.5 * jnp.sum(jnp.sum(err * err, axis=-1, keepdims=True) * (1.0 / D))
        dx, dg = _rms_bwd_math(err * (1.0 / D), x, g)
        dh_ref[rows, :] = dx
        dhb_ref[rows, :] = dx.astype(BF16)
        dg_ref[...] += dg

    row, vec = _row_specs(tm)
    t_specs = [pl.BlockSpec((TM, D), lambda i, j, k, p=p: (jnp.maximum(i * parts + p - 1, 0), 0)) for p in range(parts)]
    return _Epi(ins=[gf] + [target] * parts, in_specs=[vec] + t_specs,
                out_shapes=[jax.ShapeDtypeStruct((t, D), F32), jax.ShapeDtypeStruct((t, D), BF16),
                            jax.ShapeDtypeStruct((8, 128), F32), jax.ShapeDtypeStruct((1, D), F32)],
                out_specs=[row, row, pl.BlockSpec((8, 128), lambda i, j, k: (0, 0)), vec], fn=fn, sequential=True,
                split=parts)


def _rms_bwd_epi(h, g, dres, tm, bf16_copy):
    t = h.shape[0]

    def fn(dhn, ins, outs, i, rows, first):
        h_ref, g_ref, dres_ref = ins
        dg_ref = outs[-1]

        if first:
            @pl.when(i == 0)
            def _():
                dg_ref[...] = jnp.zeros_like(dg_ref)

        dx, dg = _rms_bwd_math(dhn, h_ref[rows, :], g_ref[...])
        dh = dres_ref[rows, :] + dx
        outs[0][rows, :] = dh
        if bf16_copy:
            outs[1][rows, :] = dh.astype(BF16)
        dg_ref[...] += dg

    row, vec = _row_specs(tm)
    copies = [jax.ShapeDtypeStruct((t, D), F32)] + ([jax.ShapeDtypeStruct((t, D), BF16)] if bf16_copy else [])
    return _Epi(ins=[h, g, dres], in_specs=[row, vec, row],
                out_shapes=copies + [jax.ShapeDtypeStruct((1, D), F32)],
                out_specs=[row] * len(copies) + [vec], fn=fn, sequential=True)


FH = DFF // 2
FM = 384


def _swiglu_epi(t):
    def fn(r, _, outs, i, rows, first):
        g, u = r[:, :FH], r[:, FH:]
        outs[0][rows, :] = r.astype(BF16)
        outs[1][rows, :] = (g * _sig(g) * u).astype(BF16)

    return _Epi(ins=[], in_specs=[],
                out_shapes=[jax.ShapeDtypeStruct((t, 2 * DFF), BF16), jax.ShapeDtypeStruct((t, DFF), BF16)],
                out_specs=[pl.BlockSpec((FM, 2 * FH), lambda i, j, k: (i, j)), pl.BlockSpec((FM, FH), lambda i, j, k: (i, j))],
                fn=fn)


def _swiglu_bwd_epi(gu):
    def fn(d, ins, outs, i, rows, first):
        gu_t = ins[0][rows, :].astype(F32)
        g, u = gu_t[:, :FH], gu_t[:, FH:]
        s = _sig(g)
        outs[0][rows, :FH] = (d * u * (s * (1.0 + g * (1.0 - s)))).astype(BF16)
        outs[0][rows, FH:] = (d * (g * s)).astype(BF16)

    spec = pl.BlockSpec((FM, 2 * FH), lambda i, j, k: (i, j))
    return _Epi(ins=[gu], in_specs=[spec], out_shapes=[jax.ShapeDtypeStruct(gu.shape, BF16)], out_specs=[spec], fn=fn)


COL_GA, COL_GB = 4, 5
COL_AX, COL_AG = (6 * D + COL_PAD) // CT, (6 * D + COL_PAD + RGW) // CT


def _merge_pro(pb, proj, tm):
    def fn(a_ref, ins):
        pa, pb_, ga, gb = (r[...].astype(F32) for r in (a_ref,) + tuple(ins))
        return (_sig(ga) * pa + _sig(gb) * pb_).astype(BF16)

    row, _ = _row_specs(tm)
    return _Pro([pb, proj, proj], [row, pl.BlockSpec((tm, D), lambda i, j, k: (i, COL_GA)),
                                   pl.BlockSpec((tm, D), lambda i, j, k: (i, COL_GB))], fn)


def _merge_bwd_epi(pa, pb, proj, tm):
    t = pa.shape[0]

    def fn(d, ins, outs, i, rows, first):
        pa, pb, ga, gb = (r[rows, :].astype(F32) for r in ins)
        dpa_ref, dpb_ref, dproj_ref = outs
        sa, sb = _sig(ga), _sig(gb)
        dpa_ref[rows, :] = (sa * d).astype(BF16)
        dpb_ref[rows, :] = (sb * d).astype(BF16)
        dproj_ref[rows, :D] = (d * pa * sa * (1.0 - sa)).astype(BF16)
        dproj_ref[rows, D:] = (d * pb * sb * (1.0 - sb)).astype(BF16)

    row, _ = _row_specs(tm)
    return _Epi(ins=[pa, pb, proj, proj],
                in_specs=[row, row, pl.BlockSpec((tm, D), lambda i, j, k: (i, COL_GA)),
                          pl.BlockSpec((tm, D), lambda i, j, k: (i, COL_GB))],
                out_shapes=[jax.ShapeDtypeStruct((t, D), BF16), jax.ShapeDtypeStruct((t, D), BF16),
                            jax.ShapeDtypeStruct((t, D_INP), BF16)],
                out_specs=[row, row, pl.BlockSpec((tm, 2 * D), lambda i, j, k: (i, 2))], fn=fn)


def _zero_pad_cols(dproj):
    t = dproj.shape[0]
    tz = _pick(t, (768, 256))

    def body(_, o_ref):
        o_ref[...] = jnp.zeros_like(o_ref)

    return pl.pallas_call(
        body, name="dproj_pad", grid=(t // tz,), in_specs=[ANY],
        out_specs=pl.BlockSpec((tz, COL_PAD), lambda i: (i, (6 * D) // COL_PAD)),
        out_shape=jax.ShapeDtypeStruct(dproj.shape, BF16), input_output_aliases={0: 0},
        compiler_params=_cparams(("parallel",)),
    )(dproj)


HALO = 16


def _rows_before(cur, before, s):
    n = cur.shape[0]
    return jnp.concatenate([before, cur], axis=0)[n - s:2 * n - s, :]


def _rows_after(cur, after, s):
    n = cur.shape[0]
    return jnp.concatenate([cur, after], axis=0)[s:n + s, :]


def _prev_halo(col0):
    return pl.BlockSpec((HALO, CT), lambda c, i: (jnp.maximum(i * (TM // HALO) - 1, 0), col0 + c))


def _conv_fwd(proj, w, b):
    t = proj.shape[0]

    def body(x_ref, halo_ref, w_ref, b_ref, ob_ref):
        i = pl.program_id(1)
        wv, bv = w_ref[...], b_ref[...]
        first = jnp.where(i > 0, halo_ref[...].astype(F32), 0.0)

        def strip(k, before):
            rows = pl.ds(pl.multiple_of(k * HALO, HALO), HALO)
            x = x_ref[rows, :].astype(F32)
            y = bv + wv[3:4, :] * x
            for j in range(3):
                y = y + wv[j:j + 1, :] * _rows_before(x, before, 3 - j)
            ob_ref[rows, :] = y.astype(BF16)
            return x

        lax.fori_loop(0, TM // HALO, strip, first)

    blk = pl.BlockSpec((TM, CT), lambda c, i: (i, c))
    return pl.pallas_call(
        body, name="conv_fwd", grid=(NCT, t // TM),
        in_specs=[pl.BlockSpec((TM, CT), lambda c, i: (i, COL_AX + c)), _prev_halo(COL_AX),
                  pl.BlockSpec((4, CT), lambda c, i: (0, c)), pl.BlockSpec((1, CT), lambda c, i: (0, c))],
        out_specs=blk,
        out_shape=jax.ShapeDtypeStruct((t, RGW), BF16),
        compiler_params=_cparams(("parallel", "parallel")),
    )(proj, proj, w, b)


def _conv_bwd(dxc, proj, w, dproj):
    t = proj.shape[0]
    nt = t // TM

    def body(d_ref, dn_ref, x_ref, halo_ref, w_ref, _, dx_ref, dw_ref, db_ref, acc):
        i = pl.program_id(1)

        @pl.when(i == 0)
        def _():
            dw_ref[...] = jnp.zeros_like(dw_ref)
            db_ref[...] = jnp.zeros_like(db_ref)

        acc[...] = jnp.zeros_like(acc)
        wv = w_ref[...]
        first = jnp.where(i > 0, halo_ref[...].astype(F32), 0.0)
        last = jnp.where(i < nt - 1, dn_ref[...].astype(F32), 0.0)
        ns = TM // HALO

        def fold(v):
            return v[0:8, :] + v[8:16, :]

        def strip(k, before):
            off = pl.multiple_of(k * HALO, HALO)
            rows = pl.ds(off, HALO)
            x, d = x_ref[rows, :].astype(F32), d_ref[rows, :].astype(F32)
            nxt = d_ref[pl.ds(pl.multiple_of(jnp.minimum(off + HALO, TM - HALO), HALO), HALO), :].astype(F32)
            after = jnp.where(k == ns - 1, last, nxt)
            dx = wv[3:4, :] * d
            for j in range(3):
                dx = dx + wv[j:j + 1, :] * _rows_after(d, after, 3 - j)
                acc[j] += fold(d * _rows_before(x, before, 3 - j))
            acc[3] += fold(d * x)
            acc[4] += fold(d)
            dx_ref[rows, :] = dx.astype(BF16)
            return x

        lax.fori_loop(0, ns, strip, first)
        for j in range(4):
            dw_ref[j:j + 1, :] += jnp.sum(acc[j], axis=0, keepdims=True)
        db_ref[...] += jnp.sum(acc[4], axis=0, keepdims=True)

    return pl.pallas_call(
        body, name="conv_bwd", grid=(NCT, nt),
        in_specs=[pl.BlockSpec((TM, CT), lambda c, i: (i, c)),
                  pl.BlockSpec((HALO, CT), lambda c, i: (jnp.minimum((i + 1) * (TM // HALO), t // HALO - 1), c)),
                  pl.BlockSpec((TM, CT), lambda c, i: (i, COL_AX + c)), _prev_halo(COL_AX),
                  pl.BlockSpec((4, CT), lambda c, i: (0, c)), ANY],
        out_specs=[pl.BlockSpec((TM, CT), lambda c, i: (i, COL_AX + c)),
                   pl.BlockSpec((4, CT), lambda c, i: (0, c)), pl.BlockSpec((1, CT), lambda c, i: (0, c))],
        out_shape=[jax.ShapeDtypeStruct((t, D_INP), BF16), jax.ShapeDtypeStruct((4, RGW), F32),
                   jax.ShapeDtypeStruct((1, RGW), F32)],
        scratch_shapes=[pltpu.VMEM((5, 8, CT), F32)],
        input_output_aliases={5: 0},
        compiler_params=_cparams(("parallel", "arbitrary")),
    )(dxc, dxc, proj, proj, w, dproj)


def _gelu(x):
    c = 0.7978845608028654
    th = jnp.tanh(c * (x + 0.044715 * x * x * x))
    return 0.5 * x * (1.0 + th), th


def _rg_gates(gr, gi, xc, ba, bx, lam, row0):
    r = _sig(gr + ba)
    ig = _sig(gi + bx)
    sp = jax.nn.softplus(-lam)
    a = jnp.exp(-RG_C * r * sp)
    s2 = jnp.maximum(1.0 - a * a, 1e-30)
    rs = lax.rsqrt(s2)
    s = s2 * rs
    rows = row0 + lax.broadcasted_iota(jnp.int32, gr.shape, 0)
    live = rows >= PAD
    u = jnp.where(live, s * ig * xc, 0.0)
    return r, ig, sp, a, s, rs, u, live


def _rg_fwd(gri, xc, proj, ba, bx, lam):
    t = gri.shape[0]

    def body(gr_ref, gi_ref, xc_ref, ag_ref, ba_ref, bx_ref, lam_ref, h_ref, ya_ref, hc):
        i = pl.program_id(1)

        @pl.when(i == 0)
        def _():
            hc[...] = jnp.zeros_like(hc)

        ba, bx, lam = ba_ref[...], bx_ref[...], lam_ref[...]
        sub = lax.broadcasted_iota(jnp.int32, (8, CT), 0)

        def strip(k, h):
            off = pl.multiple_of(k * RS, RS)
            rows = pl.ds(off, RS)
            _, _, _, a, _, _, u, _ = _rg_gates(gr_ref[rows, :], gi_ref[rows, :], xc_ref[rows, :].astype(F32), ba, bx, lam,
                                               i * TM + off)
            outs = []
            for half in range(RS // 8):
                out = jnp.zeros((8, CT), F32)
                for r in range(8):
                    h = a[8 * half + r:8 * half + r + 1, :] * h + u[8 * half + r:8 * half + r + 1, :]
                    out = jnp.where(sub == r, h, out)
                outs.append(out)
            hs = jnp.concatenate(outs, axis=0)
            h_ref[rows, :] = hs
            ge, _ = _gelu(ag_ref[rows, :].astype(F32))
            ya_ref[rows, :] = (hs * ge).astype(BF16)
            return h

        hc[...] = lax.fori_loop(0, TM // RS, strip, hc[...])

    blk_ = pl.BlockSpec((TM, CT), lambda c, i: (i, c))
    vec = pl.BlockSpec((1, CT), lambda c, i: (0, c))
    return pl.pallas_call(
        body, name="rg_fwd", grid=(NCT, t // TM),
        in_specs=[blk_, pl.BlockSpec((TM, CT), lambda c, i: (i, NCT + c)), blk_,
                  pl.BlockSpec((TM, CT), lambda c, i: (i, COL_AG + c)), vec, vec, vec],
        out_specs=[blk_, blk_],
        out_shape=[jax.ShapeDtypeStruct((t, RGW), F32), jax.ShapeDtypeStruct((t, RGW), BF16)],
        scratch_shapes=[pltpu.VMEM((1, CT), F32)],
        compiler_params=_cparams(("parallel", "arbitrary")),
    )(gri, gri, xc, proj, ba, bx, lam)


def _rg_bwd(dya, h, gri, xc, proj, ba, bx, lam, dproj):
    t = gri.shape[0]
    nt = t // TM
    assert NCT == 1

    def body(dya_ref, h_ref, hh_ref, gr_ref, gi_ref, xc_ref, ag_ref, ba_ref, bx_ref, lam_ref, _,
             dgri_ref, dxc_ref, dag_ref, dba_ref, dbx_ref, dlam_ref, hbuf, acc, cc):
        i = pl.program_id(1)
        ri = nt - 1 - i

        @pl.when(i == 0)
        def _():
            cc[...] = jnp.zeros_like(cc)
            dba_ref[...] = jnp.zeros_like(dba_ref)
            dbx_ref[...] = jnp.zeros_like(dbx_ref)
            dlam_ref[...] = jnp.zeros_like(dlam_ref)

        acc[...] = jnp.zeros_like(acc)
        ba, bx, lam = ba_ref[...], bx_ref[...], lam_ref[...]
        halo_last = jnp.where(ri > 0, hh_ref[7:8, :], 0.0)
        sub = lax.broadcasted_iota(jnp.int32, (8, CT), 0)
        c0 = 0.7978845608028654

        def strip(kk, c):
            k = TM // RS - 1 - kk
            off = pl.multiple_of(k * RS, RS)
            rows = pl.ds(off, RS)
            xc, hv = xc_ref[rows, :].astype(F32), h_ref[rows, :]
            ag, dya = ag_ref[rows, :].astype(F32), dya_ref[rows, :].astype(F32)
            r, ig, sp, a, s, rs, _, live = _rg_gates(gr_ref[rows, :], gi_ref[rows, :], xc, ba, bx, lam, ri * TM + off)
            ge, th = _gelu(ag)
            dge = 0.5 * (1.0 + th) + 0.5 * ag * (1.0 - th * th) * c0 * (1.0 + 3.0 * 0.044715 * ag * ag)
            dag_ref[rows, :] = (dya * hv * dge).astype(BF16)
            d = dya * ge
            outs = []
            for half in range(RS // 8 - 1, -1, -1):
                out = jnp.zeros((8, CT), F32)
                for rr in range(7, -1, -1):
                    g = d[8 * half + rr:8 * half + rr + 1, :] + c
                    c = a[8 * half + rr:8 * half + rr + 1, :] * g
                    out = jnp.where(sub == rr, g, out)
                outs.insert(0, out)
            g = jnp.concatenate(outs, axis=0)
            before = h_ref[pl.ds(pl.multiple_of(jnp.maximum(off - RS, 0), RS), RS), :]
            hbuf[7:8, :] = jnp.where(k == 0, halo_last, before[RS - 1:RS, :])
            hbuf[8:, :] = hv
            hprev = hbuf[pl.ds(7, RS), :]
            du = jnp.where(live, g, 0.0)
            ds = du * (ig * xc)
            dm = du * s
            dla = (g * hprev - a * ds * rs) * a
            dr = dla * (-RG_C * sp) * r * (1.0 - r)
            di = dm * xc * ig * (1.0 - ig)
            dgri_ref[rows, :CT] = dr.astype(BF16)
            dgri_ref[rows, CT:] = di.astype(BF16)
            dxc_ref[rows, :] = (dm * ig).astype(BF16)
            dl = dla * (-RG_C * r)
            for half in range(RS // 8):
                part = slice(8 * half, 8 * half + 8)
                acc[0] += dr[part]
                acc[1] += di[part]
                acc[2] += dl[part]
            return c

        cc[...] = lax.fori_loop(0, TM // RS, strip, cc[...])
        dba_ref[...] += jnp.sum(acc[0], axis=0, keepdims=True)
        dbx_ref[...] += jnp.sum(acc[1], axis=0, keepdims=True)
        dlam_ref[...] += jnp.sum(acc[2], axis=0, keepdims=True) * (-_sig(-lam))

    rblk = pl.BlockSpec((TM, CT), lambda c, i: (nt - 1 - i, c))
    vec = pl.BlockSpec((1, CT), lambda c, i: (0, c))
    hh = pl.BlockSpec((8, CT), lambda c, i: (jnp.maximum((nt - 1 - i) * (TM // 8) - 1, 0), c))
    agb = pl.BlockSpec((TM, CT), lambda c, i: (nt - 1 - i, COL_AG + c))
    return pl.pallas_call(
        body, name="rg_bwd", grid=(NCT, nt),
        in_specs=[rblk, rblk, hh, rblk, pl.BlockSpec((TM, CT), lambda c, i: (nt - 1 - i, NCT + c)), rblk, agb,
                  vec, vec, vec, ANY],
        out_specs=[pl.BlockSpec((TM, 2 * CT), lambda c, i: (nt - 1 - i, c)), rblk, agb, vec, vec, vec],
        out_shape=[jax.ShapeDtypeStruct((t, 2 * RGW), BF16),
                   jax.ShapeDtypeStruct((t, RGW), BF16), jax.ShapeDtypeStruct((t, D_INP), BF16),
                   jax.ShapeDtypeStruct((1, RGW), F32), jax.ShapeDtypeStruct((1, RGW), F32),
                   jax.ShapeDtypeStruct((1, RGW), F32)],
        scratch_shapes=[pltpu.VMEM((RS + 8, CT), F32), pltpu.VMEM((3, 8, CT), F32), pltpu.VMEM((1, CT), F32)],
        input_output_aliases={10: 2},
        compiler_params=_cparams(("parallel", "arbitrary")),
    )(dya, h, h, gri, gri, xc, proj, ba, bx, lam, dproj)


NCH = TM // CH

def _tri_dot(tri, x):
    hi = x.astype(BF16)
    r1 = x - hi.astype(F32)
    mid = r1.astype(BF16)
    lo = (r1 - mid.astype(F32)).astype(BF16)
    return _dot(tri, hi, NN) + _dot(tri, mid, NN) + _dot(tri, lo, NN)


def _hg_chunk(qr, fr, lb):
    sf = _sig(fr)
    fg = lb + (1.0 - lb) * sf
    k = (1.0 - lb) * (1.0 - sf)
    sq = _sig(qr)
    q = qr * sq
    ri = lax.broadcasted_iota(jnp.int32, (CH, CH), 0)
    ci = lax.broadcasted_iota(jnp.int32, (CH, CH), 1)
    b = _tri_dot((ri >= ci).astype(BF16), jnp.log(fg))
    bm, bl = b[CH // 2 - 1:CH // 2, :], b[CH - 1:CH, :]
    ebm = jnp.exp(jnp.minimum(b - bm, EXP_CLAMP))
    ekm = jnp.exp(jnp.minimum(bm - b, EXP_CLAMP))
    eb = ebm * jnp.exp(bm)
    ekl = ekm * jnp.exp(bl - bm)
    return dict(sf=sf, fg=fg, k=k, sq=sq, q=q, eb=eb, ebm=ebm, ekm=ekm, ekl=ekl, ebl=jnp.exp(bl),
                qe=q * eb, qh=q * ebm, kh=k * ekm, kd=k * ekl, causal=ri >= ci, upper=(ci >= ri).astype(BF16))


def _hgrn_fwd(proj, lbl, gn):
    t = proj.shape[0]
    nt = t // TM

    def body(q_ref, f_ref, v_ref, g_ref, lbl_ref, gn_ref, yb_ref, o_ref, st_ref, st):
        @pl.when(pl.program_id(0) == 0)
        def _():
            st[...] = jnp.zeros_like(st)

        l = lbl_ref[...]
        lb = _sig(l[0:1, :] - l[1:2, :])
        gnv = gn_ref[...]

        def chunk(c, carry):
            off = pl.multiple_of(c * CH, CH)
            rows = pl.ds(off, CH)
            z = _hg_chunk(q_ref[rows, :].astype(F32), f_ref[rows, :].astype(F32), lb)
            v, gg = v_ref[rows, :], g_ref[rows, :].astype(F32)
            for hh in range(HEADS):
                sl = slice(hh * HD, (hh + 1) * HD)
                s_prev = st[hh]
                st_ref[c, hh] = s_prev
                vb = v[:, sl].astype(BF16)
                att = jnp.where(z["causal"], _dot(z["qh"][:, sl].astype(BF16), z["kh"][:, sl].astype(BF16), NT), 0.0)
                o = _dot(z["qe"][:, sl].astype(BF16), s_prev.astype(BF16), NT) + _dot(att.astype(BF16), vb, NN)
                st[hh] = s_prev * z["ebl"][:, sl] + _dot(vb, z["kd"][:, sl].astype(BF16), TN_)
                xh, _ = _rms(o)
                gh = gg[:, sl]
                o_ref[rows, sl] = o
                yb_ref[rows, sl] = (xh * gnv * gh * _sig(gh)).astype(BF16)
            return carry

        lax.fori_loop(0, NCH, chunk, 0, unroll=True)

    def col(j):
        return pl.BlockSpec((TM, HGW), lambda i, j=j: (i, j))

    return pl.pallas_call(
        body, name="hgrn_fwd", grid=(nt,),
        in_specs=[col(0), col(1), col(2), col(3), pl.BlockSpec((2, HGW), lambda i: (0, 0)),
                  pl.BlockSpec((1, HD), lambda i: (0, 0))],
        out_specs=[col(0), col(0), pl.BlockSpec((NCH, HEADS, HD, HD), lambda i: (i, 0, 0, 0))],
        out_shape=[jax.ShapeDtypeStruct((t, HGW), BF16), jax.ShapeDtypeStruct((t, HGW), F32),
                   jax.ShapeDtypeStruct((t // CH, HEADS, HD, HD), F32)],
        scratch_shapes=[pltpu.VMEM((HEADS, HD, HD), F32)],
        compiler_params=_cparams(("arbitrary",)),
    )(proj, proj, proj, proj, lbl, gn)


def _hgrn_bwd(dyb, proj, o, states, lbl, gn, dproj):
    t = proj.shape[0]
    nt = t // TM

    def body(dy_ref, q_ref, f_ref, v_ref, g_ref, o_ref, st_ref, lbl_ref, gn_ref, _,
             dp_ref, dgn_ref, dl_ref, dst, dlb):
        i = pl.program_id(0)

        @pl.when(i == 0)
        def _():
            dst[...] = jnp.zeros_like(dst)
            dlb[...] = jnp.zeros_like(dlb)
            dgn_ref[...] = jnp.zeros_like(dgn_ref)

        l = lbl_ref[...]
        lb = _sig(l[0:1, :] - l[1:2, :])
        gnv = gn_ref[...]
        last = lax.broadcasted_iota(jnp.int32, (CH, HD), 0) == CH - 1

        def chunk(cc, carry):
            c = NCH - 1 - cc
            off = pl.multiple_of(c * CH, CH)
            rows = pl.ds(off, CH)
            qr, fr = q_ref[rows, :].astype(F32), f_ref[rows, :].astype(F32)
            z = _hg_chunk(qr, fr, lb)
            v, gg, ov, dy = v_ref[rows, :], g_ref[rows, :].astype(F32), o_ref[rows, :], dy_ref[rows, :].astype(F32)
            dqs, dks, dbs, dvs, dgs = [], [], [], [], []
            dgn = jnp.zeros((1, HD), F32)
            for hh in range(HEADS):
                sl = slice(hh * HD, (hh + 1) * HD)
                s_prev, ds_new = st_ref[c, hh], dst[hh]
                qe, qh, kh, kd = z["qe"][:, sl], z["qh"][:, sl], z["kh"][:, sl], z["kd"][:, sl]
                ebl = z["ebl"][:, sl]
                gh, dyh = gg[:, sl], dy[:, sl]
                xh, rr = _rms(ov[:, sl])
                sg = _sig(gh)
                dyn = dyh * (gh * sg)
                dgs.append(dyh * (xh * gnv) * (sg * (1.0 + gh * (1.0 - sg))))
                dgn = dgn + jnp.sum(dyn * xh, axis=0, keepdims=True)
                dxh = dyn * gnv
                do = (rr * (dxh - xh * jnp.mean(dxh * xh, axis=-1, keepdims=True))).astype(BF16)
                vb, dsb = v[:, sl].astype(BF16), ds_new.astype(BF16)
                qeb, qhb, khb, kdb = (a.astype(BF16) for a in (qe, qh, kh, kd))
                att = jnp.where(z["causal"], _dot(qhb, khb, NT), 0.0).astype(BF16)
                datt = jnp.where(z["causal"], _dot(do, vb, NT), 0.0).astype(BF16)
                dvs.append(_dot(att, do, TN_) + _dot(kdb, dsb, NT))
                dqe = _dot(do, s_prev.astype(BF16), NN)
                dqh = _dot(datt, khb, NN)
                dkh = _dot(datt, qhb, TN_)
                dkd = _dot(vb, dsb, NN)
                qe, qh, kh, kd = (a.astype(F32) for a in (qeb, qhb, khb, kdb))
                dbl = (jnp.sum(dkd * kd, axis=0, keepdims=True)
                       + jnp.sum(ds_new * s_prev, axis=0, keepdims=True) * ebl)
                dqs.append(dqe * z["eb"][:, sl] + dqh * z["ebm"][:, sl])
                dks.append(dkh * z["ekm"][:, sl] + dkd * z["ekl"][:, sl])
                dbs.append(dqe * qe + dqh * qh - dkh * kh - dkd * kd + jnp.where(last, dbl, 0.0))
                dst[hh] = _dot(do, qeb, TN_) + ds_new * ebl
            dgn_ref[...] += dgn
            dq, dk, db = (jnp.concatenate(x, axis=1) for x in (dqs, dks, dbs))
            dlf = _tri_dot(z["upper"], db)
            sf, fg, sq = z["sf"], z["fg"], z["sq"]
            dmix = dlf / fg - dk
            dsf = dmix * (1.0 - lb)
            dlb[...] += jnp.sum(dmix * (1.0 - sf), axis=0, keepdims=True)
            dp_ref[rows, 0:HGW] = (dq * (sq * (1.0 + qr * (1.0 - sq)))).astype(BF16)
            dp_ref[rows, HGW:2 * HGW] = (dsf * sf * (1.0 - sf)).astype(BF16)
            dp_ref[rows, 2 * HGW:3 * HGW] = jnp.concatenate(dvs, axis=1).astype(BF16)
            dp_ref[rows, 3 * HGW:4 * HGW] = jnp.concatenate(dgs, axis=1).astype(BF16)
            return carry

        lax.fori_loop(0, NCH, chunk, 0, unroll=True)
        dl0 = dlb[...] * lb * (1.0 - lb)
        dl_ref[0:1, :] = dl0
        dl_ref[1:2, :] = -dl0

    def col(j):
        return pl.BlockSpec((TM, HGW), lambda i, j=j: (nt - 1 - i, j))

    return pl.pallas_call(
        body, name="hgrn_bwd", grid=(nt,),
        in_specs=[col(0), col(0), col(1), col(2), col(3), col(0),
                  pl.BlockSpec((NCH, HEADS, HD, HD), lambda i: (nt - 1 - i, 0, 0, 0)),
                  pl.BlockSpec((2, HGW), lambda i: (0, 0)), pl.BlockSpec((1, HD), lambda i: (0, 0)), ANY],
        out_specs=[pl.BlockSpec((TM, 4 * HGW), lambda i: (nt - 1 - i, 0)),
                   pl.BlockSpec((1, HD), lambda i: (0, 0)), pl.BlockSpec((2, HGW), lambda i: (0, 0))],
        out_shape=[jax.ShapeDtypeStruct((t, D_INP), BF16), jax.ShapeDtypeStruct((1, HD), F32),
                   jax.ShapeDtypeStruct((2, HGW), F32)],
        scratch_shapes=[pltpu.VMEM((HEADS, HD, HD), F32), pltpu.VMEM((1, HGW), F32)],
        input_output_aliases={9: 0},
        compiler_params=_cparams(("arbitrary",)),
    )(dyb, proj, proj, proj, proj, o, states, lbl, gn, dproj)


def _rotate_w_in(w):
    return jnp.concatenate([w[:, ROT:], jnp.zeros((w.shape[0], COL_PAD), w.dtype), w[:, :ROT]], axis=1)


def _unrotate_g_in(g):
    return jnp.concatenate([g[:, D_INP - ROT:], g[:, :D_IN - ROT]], axis=1)


SB = RGW // 2


def _bd_gates(t):
    tm = _pick(t, (768, 256))

    def col(j):
        return 2 * (j % 2) + j // 2

    return ((t // tm, 4, 1), (tm, SB, SB),
            (lambda i, j, k: (i, j // 2), lambda i, j, k: (j // 2, col(j)), lambda i, j, k: (i, col(j))))


def _bd_dw(t):
    tk = _pick(t, (1408, 768))
    return (1, 4, t // tk), (SB, SB, tk), (lambda i, j, k: (k, j % 2), lambda i, j, k: (k, j), lambda i, j, k: (j % 2, j))


def _block_diag(w):
    eye = jnp.eye(RG_BLOCKS, dtype=w.dtype)
    return (w[:, :, None, :] * eye[:, None, :, None]).reshape(RGW, RGW)


def _diag_blocks(wd):
    w4 = wd.reshape(RG_BLOCKS, RG_BLOCK, RG_BLOCKS, RG_BLOCK)
    return jnp.stack([w4[n, :, n, :] for n in range(RG_BLOCKS)])


def _local_step(h0, target, wts, small, hosts=None, finalize=None):
    hosts = hosts or {}
    carried = {}
    g = {}
    t = h0.shape[0]

    def mm(a, b, mode, out_dtype, name, **kw):
        if name not in hosts:
            return _mm(a, b, mode, out_dtype, name, **kw)
        host = hosts[name](g)
        out, res = _mm(a, b, mode, out_dtype, name, host=host, **kw)
        carried[name] = (host.ins, res)
        return out

    w_in = wts["w_in"]
    wax_d = jnp.concatenate([_block_diag(small["rg_wa"].astype(BF16)), _block_diag(small["rg_wx"].astype(BF16))], axis=1)
    ba, bx, lam = small["rg_ba"], small["rg_bx"], small["rg_lambda"]
    lbl, gn = small["hg_lb_logits"], small["hg_norm_g"]
    conv_w, conv_b = small["conv_w"], small["conv_b"]

    tmm = _pick(t, (768, 256))
    proj, hn1 = mm(h0, w_in, "nn", BF16, "mm_proj", pro=_norm_pro(small["norm1_g"]))
    if finalize is not None:
        wts = finalize(carried["mm_proj"][1])
    w_pa, w_pb, w_out, w_fi, w_fd = (wts[k] for k in ("w_proj_a", "w_proj_b", "w_out", "w_ffn_in", "w_ffn_down"))
    xcb = _conv_fwd(proj, conv_w, conv_b)
    gri = mm(xcb, wax_d, "nn", F32, "mm_rg_gates", sparse=_bd_gates(t))
    hrg, ya = _rg_fwd(gri, xcb, proj, ba, bx, lam)
    yb, o, states = _hgrn_fwd(proj, lbl, gn)
    pa = mm(ya, w_pa, "nn", BF16, "mm_pa")
    pb = mm(yb, w_pb, "nn", BF16, "mm_pb")
    h1, merged = mm(pa, w_out, "nn", F32, "mm_out", resid=h0, pro=_merge_pro(pb, proj, tmm), tiles=(tmm, D))
    (gu, act), hn2 = mm(h1, w_fi, "nn", BF16, "mm_ffn_in", pro=_norm_pro(small["norm2_g"]), epi=_swiglu_epi(t),
                        tiles=(FM, 2 * FH))
    dh2, dh2b, loss, g["norm_f_g"] = mm(act, w_fd, "nn", F32, "mm_ffn_down", resid=h1,
                                        epi=_final_loss_epi(small["norm_f_g"], target, t, tmm), tiles=(tmm, D))

    g["w_ffn_down"] = mm(act, dh2b, "tn", BF16, "mm_d_wfd")
    (dgu,) = mm(dh2b, w_fd, "nt", BF16, "mm_d_act", epi=_swiglu_bwd_epi(gu), tiles=(FM, FH))
    g["w_ffn_in"] = mm(hn2, dgu, "tn", BF16, "mm_d_wfi")
    dh1, dh1b, g["norm2_g"] = mm(dgu, w_fi, "nt", BF16, "mm_d_hn2", epi=_rms_bwd_epi(h1, small["norm2_g"], dh2, tmm, True),
                                 tiles=(tmm, D))
    g["w_out"] = mm(merged, dh1b, "tn", BF16, "mm_d_wout")
    dpa, dpb, dproj = mm(dh1b, w_out, "nt", BF16, "mm_d_merged", epi=_merge_bwd_epi(pa, pb, proj, tmm), tiles=(tmm, D))
    dproj = _zero_pad_cols(dproj)
    g["w_proj_a"] = mm(ya, dpa, "tn", BF16, "mm_d_wpa")
    g["w_proj_b"] = mm(yb, dpb, "tn", BF16, "mm_d_wpb")
    dya = mm(dpa, w_pa, "nt", BF16, "mm_d_ya")
    dyb = mm(dpb, w_pb, "nt", BF16, "mm_d_yb")
    dproj, g["hg_norm_g"], g["hg_lb_logits"] = _hgrn_bwd(dyb, proj, o, states, lbl, gn, dproj)
    dgri, dxc, dproj, g["rg_ba"], g["rg_bx"], g["rg_lambda"] = _rg_bwd(dya, hrg, gri, xcb, proj, ba, bx, lam, dproj)
    dwax = mm(xcb, dgri, "tn", F32, "mm_d_wax", sparse=_bd_dw(t))
    g["rg_wa"], g["rg_wx"] = _diag_blocks(dwax[:, :RGW]), _diag_blocks(dwax[:, RGW:])
    dxc = mm(dgri, wax_d, "nt", BF16, "mm_d_xc", resid=dxc)
    dproj, g["conv_w"], g["conv_b"] = _conv_bwd(dxc, proj, conv_w, dproj)
    g["w_in"] = mm(hn1, dproj, "tn", BF16, "mm_d_win")
    dh0, g["norm1_g"] = mm(dproj, w_in, "nt", BF16, "mm_d_hn1", epi=_rms_bwd_epi(h0, small["norm1_g"], dh1, tmm, False),
                           tiles=(tmm, D))
    grad_x, g["meta_tokens"] = dh0[ROW0:], dh0[PAD:ROW0]
    return loss, grad_x, g, carried


def _adamw_math(w, g, m, v):
    m = B1 * m + (1.0 - B1) * g
    v = B2 * v + (1.0 - B2) * (g * g)
    m_hat = m / (1.0 - B1 ** STEP)
    v_hat = v / (1.0 - B2 ** STEP)
    return -LR * (m_hat / (jnp.sqrt(v_hat) + ADAM_EPS) + WD * w), m, v


def _adamw(w, g, m, v, name):
    r, c = w.shape
    tr = _pick(r, (256, 352, 320, 128, 64, 32, 16, 8))

    def body(w_ref, g_ref, m_ref, v_ref, d_ref, mo_ref, vo_ref):
        d_ref[...], mo_ref[...], vo_ref[...] = _adamw_math(w_ref[...], g_ref[...], m_ref[...], v_ref[...])

    blk = pl.BlockSpec((tr, c), lambda i: (i, 0))
    return pl.pallas_call(
        body, name=name, grid=(r // tr,), in_specs=[blk] * 4, out_specs=[blk] * 3,
        out_shape=[jax.ShapeDtypeStruct((r, c), F32)] * 3,
        compiler_params=_cparams(("parallel",)),
    )(w, g, m, v)


def _adamw_rows(recv, params, extras, name):
    _, _, width = recv.shape
    n_par = len(params)

    def body(*refs):
        r_ref = refs[0]
        ins = refs[1:1 + 3 * n_par]
        outs = refs[1 + 3 * n_par:]
        g_all = r_ref[0]
        for p in range(1, 8):
            g_all = g_all + r_ref[p]
        for q, (row0, w, _, _) in enumerate(params):
            g = g_all[row0:row0 + w.shape[0], :]
            w_ref, m_ref, v_ref = ins[3 * q:3 * q + 3]
            g_ref, d_ref, mo_ref, vo_ref = outs[4 * q:4 * q + 4]
            g_ref[...] = g
            d_ref[...], mo_ref[...], vo_ref[...] = _adamw_math(w_ref[...], g, m_ref[...], v_ref[...])
        for q, (row0, n) in enumerate(extras):
            outs[4 * n_par + q][...] = g_all[row0:row0 + n, :]

    shapes = [jax.ShapeDtypeStruct(w.shape, F32) for _, w, _, _ in params for _ in range(4)]
    shapes += [jax.ShapeDtypeStruct((n, width), F32) for _, n in extras]
    vm = pl.BlockSpec(memory_space=pltpu.VMEM)
    outs = pl.pallas_call(
        body, name=name, in_specs=[vm] * (1 + 3 * n_par), out_specs=[vm] * len(shapes), out_shape=shapes,
        compiler_params=pltpu.CompilerParams(vmem_limit_bytes=VMEM_LIMIT),
    )(recv, *[a for _, w, m, v in params for a in (w, m, v)])
    return [tuple(outs[4 * q:4 * q + 4]) for q in range(n_par)], list(outs[4 * n_par:])


def _col_block(weight, px, py):
    return 2 * py + px if weight == "w_ffn_in" else 2 * px + py


def _sum_place(recv, gfull, weight, xyc, name):
    _, r, c = recv.shape
    tr = _pick(r, (128, 176, 160, 64, 32, 16))
    nb = r // tr

    def body(_, own_ref, r_ref, o_ref):
        g = own_ref[...].astype(F32)
        for p in range(7):
            g = g + r_ref[p].astype(F32)
        o_ref[...] = g

    if weight in COL_SHARDED:
        own = pl.BlockSpec((tr, c), lambda i, s: (s[2] * nb + i, _col_block(weight, s[0], s[1])))
    else:
        own = pl.BlockSpec((tr, c), lambda i, s: ((4 * s[0] + 2 * s[1] + s[2]) * nb + i, 0))
    return pl.pallas_call(
        body, name=name,
        grid_spec=pltpu.PrefetchScalarGridSpec(
            num_scalar_prefetch=1, grid=(nb,),
            in_specs=[own, pl.BlockSpec((7, tr, c), lambda i, s: (0, i, 0))],
            out_specs=pl.BlockSpec((tr, c), lambda i, s: (s[2] * nb + i, 0))),
        out_shape=jax.ShapeDtypeStruct((2 * r, c), F32),
        compiler_params=_cparams(("arbitrary",)),
    )(xyc, gfull, recv)


def _place_own(shard, slot, dtype, name):
    r, c = shard.shape
    tr = _pick(r, (256, 352, 320, 128, 32))

    def body(_, x_ref, o_ref):
        o_ref[...] = x_ref[...].astype(dtype)

    return pl.pallas_call(
        body, name=name,
        grid_spec=pltpu.PrefetchScalarGridSpec(
            num_scalar_prefetch=1, grid=(r // tr,),
            in_specs=[pl.BlockSpec((tr, c), lambda i, s: (i, 0))],
            out_specs=pl.BlockSpec((None, tr, c), lambda i, s: (s[0], i, 0))),
        out_shape=jax.ShapeDtypeStruct((4, r, c), dtype),
        compiler_params=_cparams(("arbitrary",)),
    )(slot, shard)


def _place_cols(shard, slot, name):
    r, c = shard.shape
    tr = _pick(r, (256, 128))

    def body(_, x_ref, o_ref):
        o_ref[...] = x_ref[...].astype(BF16)

    return pl.pallas_call(
        body, name=name,
        grid_spec=pltpu.PrefetchScalarGridSpec(
            num_scalar_prefetch=1, grid=(r // tr,),
            in_specs=[pl.BlockSpec((tr, c), lambda i, s: (i, 0))],
            out_specs=pl.BlockSpec((tr, c), lambda i, s: (i, 2 * (s[0] % 2) + s[0] // 2))),
        out_shape=jax.ShapeDtypeStruct((r, 4 * c), BF16),
        compiler_params=_cparams(("arbitrary",)),
    )(slot, shard)


def _place_w_in(shard, slot):
    r, c = shard.shape
    nb, npad = c // 128, COL_PAD // 128
    rot_blocks, in_blocks = ROT // 128, D_IN // 128

    def out_block(j, s):
        nat = s[0] * nb + jnp.minimum(j, nb - 1)
        rot = jnp.where(nat >= rot_blocks, nat - rot_blocks, nat + (D_INP - ROT) // 128)
        return jnp.where(j < nb, rot, (in_blocks - rot_blocks) + (j - nb))

    def body(_, x_ref, o_ref):
        o_ref[...] = jnp.where(pl.program_id(0) < nb, x_ref[...], 0.0).astype(BF16)

    return pl.pallas_call(
        body, name="place_w_in",
        grid_spec=pltpu.PrefetchScalarGridSpec(
            num_scalar_prefetch=1, grid=(nb + npad,),
            in_specs=[pl.BlockSpec((r, 128), lambda j, s: (0, jnp.minimum(j, nb - 1)))],
            out_specs=pl.BlockSpec((r, 128), lambda j, s: (0, out_block(j, s)))),
        out_shape=jax.ShapeDtypeStruct((r, D_INP), BF16),
        compiler_params=_cparams(("arbitrary",)),
    )(slot, shard)


BIG = ("w_in", "w_proj_a", "w_proj_b", "w_out", "w_ffn_in", "w_ffn_down")
COL_SHARDED = ("w_in", "w_ffn_in")


def _coords():
    return lax.axis_index("x"), lax.axis_index("y"), lax.axis_index("c")


W_IN_HEAD = 384


def _shard_views(b, layout, chip, half_of):
    if layout == "slots":
        half = b.shape[1] // 2
        return [b.at[chip, pl.ds(pl.multiple_of(half_of * half, 8), half)]]
    half = b.shape[0] // 2
    rows = pl.ds(pl.multiple_of(half_of * half, 8), half)
    if layout == "cols":
        c = b.shape[1] // 4
        return [b.at[rows, pl.ds(pl.multiple_of((2 * (chip % 2) + chip // 2) * c, 128), c)]]
    c = D_IN // 4
    views = []
    for first, width in ((0, W_IN_HEAD), (W_IN_HEAD, c - W_IN_HEAD)):
        nat = chip * c + first
        rot = jnp.where(nat >= ROT, nat - ROT, nat + D_INP - ROT)
        views.append(b.at[rows, pl.ds(pl.multiple_of(rot, 128), width)])
    return views


def _gather_host(bufs, layouts):
    n = len(bufs)

    def place():
        x, y, c = _coords()
        return x, y, c, [(1 - x, y), (x, 1 - y), (1 - x, 1 - y)]

    def copies(b, sems, w, k, chip, half_of, to):
        return [pltpu.make_async_remote_copy(src_ref=v, dst_ref=v, send_sem=sems[0].at[w, k, p], recv_sem=sems[1].at[w, k, p],
                                             device_id=to, device_id_type=MESH)
                for p, v in enumerate(_shard_views(b, layouts[w], chip, half_of))]

    def start(_, outs, sems):
        x, y, c, chips = place()
        for w, b in enumerate(outs):
            for k, (cx, cy) in enumerate(chips):
                for cp in copies(b, sems, w, k, 2 * x + y, c, (cx, cy, c)):
                    cp.start()

    def mid(_, outs, sems):
        x, y, c, chips = place()
        for w, b in enumerate(outs):
            for k, (cx, cy) in enumerate(chips):
                for cp in copies(b, sems, w, k, 2 * cx + cy, c, (cx, cy, c)):
                    cp.wait_recv()
                for cp in copies(b, sems, w, 3 + k, 2 * cx + cy, c, (x, y, 1 - c)):
                    cp.start()

    def finish(_, outs, sems):
        x, y, c, chips = place()
        for w, b in enumerate(outs):
            for k, (cx, cy) in enumerate(chips):
                for cp in copies(b, sems, w, 3 + k, 2 * cx + cy, 1 - c, (x, y, 1 - c)):
                    cp.wait_recv()
        for w, b in enumerate(outs):
            for k, (cx, cy) in enumerate(chips):
                for cp in copies(b, sems, w, k, 2 * x + y, c, (cx, cy, c)) + copies(b, sems, w, 3 + k, 2 * cx + cy, c, (x, y, 1 - c)):
                    cp.wait_send()

    return _Host(ins=list(bufs), out_shapes=[jax.ShapeDtypeStruct(b.shape, b.dtype) for b in bufs],
                 aliases={i: i for i in range(n)},
                 sems=[pltpu.SemaphoreType.DMA((n, 6, 2)), pltpu.SemaphoreType.DMA((n, 6, 2))],
                 start=start, mid=mid, finish=finish)


def _run_host(host, name):
    n_in, n_out = len(host.ins), len(host.out_shapes)

    def body(*refs):
        ins, outs, sems = refs[:n_in], refs[n_in:n_in + n_out], refs[n_in + n_out:]
        host.start(ins, outs, sems)
        if host.mid is not None:
            host.mid(ins, outs, sems)
        host.finish(ins, outs, sems)

    return pl.pallas_call(
        body, name=name, in_specs=[ANY] * n_in, out_specs=[ANY] * n_out, out_shape=list(host.out_shapes),
        scratch_shapes=list(host.sems), input_output_aliases=dict(host.aliases),
    )(*host.ins)


def _peer(x, y, c, k):
    fx, fy, fc = (k >> 2) & 1, (k >> 1) & 1, k & 1
    return (x ^ fx if fx else x, y ^ fy if fy else y, c ^ fc if fc else c)


def _sub_shape(name, full_shape):
    r, c = full_shape
    return (r // 2, c // 4) if name in COL_SHARDED else (r // 8, c)


def _exchange_host(names, grads):
    n = len(names)
    shapes = [_sub_shape(k, g.shape) for k, g in zip(names, grads)]

    def copy(ins, outs, sems, w, k):
        x, y, c = _coords()
        px, py, pc = _peer(x, y, c, k)
        sr, sc = shapes[w]
        if names[w] in COL_SHARDED:
            col = _col_block(names[w], px, py) * sc
            src = ins[w].at[pl.ds(pl.multiple_of(pc * sr, 16), sr), pl.ds(pl.multiple_of(col, 128), sc)]
        else:
            src = ins[w].at[pl.ds(pl.multiple_of((4 * px + 2 * py + pc) * sr, 16), sr)]
        return pltpu.make_async_remote_copy(
            src_ref=src, dst_ref=outs[w].at[k - 1], send_sem=sems[0].at[w, k - 1], recv_sem=sems[1].at[w, k - 1],
            device_id=(px, py, pc), device_id_type=MESH)

    def start(ins, outs, sems):
        for w in range(n):
            for k in range(1, 8):
                copy(ins, outs, sems, w, k).start()

    def finish(ins, outs, sems):
        for w in range(n):
            for k in range(1, 8):
                copy(ins, outs, sems, w, k).wait_recv()
        for w in range(n):
            for k in range(1, 8):
                copy(ins, outs, sems, w, k).wait_send()

    return _Host(ins=list(grads), out_shapes=[jax.ShapeDtypeStruct((7,) + s, g.dtype) for s, g in zip(shapes, grads)],
                 aliases={}, sems=[pltpu.SemaphoreType.DMA((n, 7)), pltpu.SemaphoreType.DMA((n, 7))],
                 start=start, mid=None, finish=finish)


def _sibling_host(bufs):
    n = len(bufs)

    def copy(outs, sems, w, half_of):
        x, y, c = _coords()
        half = outs[w].shape[0] // 2
        rows = outs[w].at[pl.ds(pl.multiple_of(half_of * half, 8), half)]
        return pltpu.make_async_remote_copy(src_ref=rows, dst_ref=rows, send_sem=sems[0].at[w], recv_sem=sems[1].at[w],
                                            device_id=(x, y, 1 - c), device_id_type=MESH)

    def start(_, outs, sems):
        c = lax.axis_index("c")
        for w in range(n):
            copy(outs, sems, w, c).start()

    def finish(_, outs, sems):
        c = lax.axis_index("c")
        for w in range(n):
            copy(outs, sems, w, 1 - c).wait_recv()
        for w in range(n):
            copy(outs, sems, w, c).wait_send()

    return _Host(ins=list(bufs), out_shapes=[jax.ShapeDtypeStruct(b.shape, b.dtype) for b in bufs],
                 aliases={i: i for i in range(n)},
                 sems=[pltpu.SemaphoreType.DMA((n,)), pltpu.SemaphoreType.DMA((n,))], start=start, mid=None, finish=finish)


def _pack_host(pack):
    def me_of():
        x, y, c = _coords()
        return x, y, c, 4 * x + 2 * y + c

    def copy(ins, outs, sems, k, slot):
        x, y, c, _ = me_of()
        return pltpu.make_async_remote_copy(src_ref=ins[0], dst_ref=outs[0].at[slot], send_sem=sems[0].at[k - 1],
                                            recv_sem=sems[1].at[k - 1], device_id=_peer(x, y, c, k), device_id_type=MESH)

    def start(ins, outs, sems):
        me = me_of()[3]
        pltpu.make_async_copy(ins[0], outs[0].at[me], sems[2]).start()
        for k in range(1, 8):
            copy(ins, outs, sems, k, me).start()

    def finish(ins, outs, sems):
        x, y, c, me = me_of()
        for k in range(1, 8):
            px, py, pc = _peer(x, y, c, k)
            copy(ins, outs, sems, k, 4 * px + 2 * py + pc).wait_recv()
        for k in range(1, 8):
            copy(ins, outs, sems, k, me).wait_send()
        pltpu.make_async_copy(ins[0], outs[0].at[me], sems[2]).wait()

    return _Host(ins=[pack], out_shapes=[jax.ShapeDtypeStruct((8,) + pack.shape, F32)], aliases={},
                 sems=[pltpu.SemaphoreType.DMA((7,)), pltpu.SemaphoreType.DMA((7,)), pltpu.SemaphoreType.DMA],
                 start=start, mid=None, finish=finish)


def _join_hosts(hosts):
    ins, outs, sems, aliases, spans = [], [], [], {}, []
    for h in hosts:
        spans.append((len(ins), len(h.ins), len(outs), len(h.out_shapes), len(sems), len(h.sems)))
        for i_in, i_out in h.aliases.items():
            aliases[len(ins) + i_in] = len(outs) + i_out
        ins, outs, sems = ins + list(h.ins), outs + list(h.out_shapes), sems + list(h.sems)

    def phase(which):
        fns = [getattr(h, which) for h in hosts]
        if all(f is None for f in fns):
            return None

        def run(i_refs, o_refs, s_refs):
            for f, (i0, n_i, o0, n_o, s0, n_s) in zip(fns, spans):
                if f is not None:
                    f(i_refs[i0:i0 + n_i], o_refs[o0:o0 + n_o], s_refs[s0:s0 + n_s])

        return run

    return _Host(ins, outs, aliases, sems, phase("start"), phase("mid"), phase("finish"))


SHARDED_SMALL = ("meta_tokens", "conv_w")

ROWS = {"conv_b": (1, RGW), "rg_ba": (1, RGW), "rg_bx": (1, RGW), "rg_lambda": (1, RGW), "conv_w": (4, RGW),
        "norm2_g": (1, D), "norm_f_g": (1, D), "hg_lb_logits": (2, D),
        "rg_wa": (RGW * RG_BLOCK // 128, 128), "rg_wx": (RGW * RG_BLOCK // 128, 128), "hg_norm_g": (1, 128),
        "norm1_g": (1, D), "meta_tokens": (N_META, D), "loss": (1, D)}
PACKS = {"early_wide": ("conv_b", "rg_ba", "rg_bx", "rg_lambda", "conv_w"),
         "early_model": ("norm2_g", "norm_f_g", "hg_lb_logits"),
         "early_lane": ("rg_wa", "rg_wx", "hg_norm_g"),
         "late": ("norm1_g", "meta_tokens", "loss")}
EARLY = ("early_wide", "early_model", "early_lane")
NO_UPDATE = ("conv_w", "meta_tokens", "loss")


def _pack_rows(vals, names):
    parts, first, row = [], {}, 0
    for k in names:
        first[k] = row
        parts.append(vals[k].reshape(ROWS[k]).astype(F32))
        row += ROWS[k][0]
    if row % 8:
        parts.append(jnp.zeros((-row % 8, ROWS[names[0]][1]), F32))
    return jnp.concatenate(parts, axis=0), first


ORDER = ("meta_tokens", "norm1_g", "w_in", "conv_w", "conv_b", "rg_wa", "rg_ba", "rg_wx", "rg_bx", "rg_lambda",
         "hg_lb_logits", "hg_norm_g", "w_proj_a", "w_proj_b", "w_out", "norm2_g", "w_ffn_in", "w_ffn_down", "norm_f_g")


def kernel(x, meta_tokens, norm1_g, w_in, conv_w, conv_b, rg_wa, rg_ba, rg_wx, rg_bx, rg_lambda, hg_lb_logits, hg_norm_g, w_proj_a, w_proj_b, w_out, norm2_g, w_ffn_in, w_ffn_down, norm_f_g, loss_target, m_meta_tokens, m_norm1_g, m_w_in, m_conv_w, m_conv_b, m_rg_wa, m_rg_ba, m_rg_wx, m_rg_bx, m_rg_lambda, m_hg_lb_logits, m_hg_norm_g, m_w_proj_a, m_w_proj_b, m_w_out, m_norm2_g, m_w_ffn_in, m_w_ffn_down, m_norm_f_g, v_meta_tokens, v_norm1_g, v_w_in, v_conv_w, v_conv_b, v_rg_wa, v_rg_ba, v_rg_wx, v_rg_bx, v_rg_lambda, v_hg_lb_logits, v_hg_norm_g, v_w_proj_a, v_w_proj_b, v_w_out, v_norm2_g, v_w_ffn_in, v_w_ffn_down, v_norm_f_g):
    args = dict(locals())
    w = {k: args[k] for k in ORDER}
    m = {k: args["m_" + k] for k in ORDER}
    v = {k: args["v_" + k] for k in ORDER}
    xi, yi, ci = _coords()
    chip = 2 * xi + yi
    slot = jnp.reshape(chip, (1,)).astype(jnp.int32)
    xyc = jnp.stack([xi, yi, ci]).astype(jnp.int32)

    tiny = jnp.zeros((32, 384), F32)
    tiny = tiny.at[0:N_META, 0:256].set(meta_tokens).at[N_META:N_META + 4, 0:320].set(conv_w[0])
    later = [k for k in BIG if k != "w_in"]
    layouts = {k: "cols" if k in COL_SHARDED else "slots" for k in later}
    slots = {k: _place_cols(w[k][0], slot, "place_" + k) if layouts[k] == "cols"
             else _place_own(w[k][0], slot, BF16, "place_" + k) for k in later}
    w_in_full, tiny_all = _run_host(
        _gather_host([_place_w_in(w["w_in"][0], slot), _place_own(tiny, slot, F32, "place_tiny")], ["w_in", "slots"]),
        "allgather_w_in")

    def full_matrix(k, gth):
        return gth if layouts[k] == "cols" else gth.reshape(-1, gth.shape[2])

    meta_full = jnp.transpose(tiny_all[:, 0:N_META, 0:256], (1, 0, 2)).reshape(N_META, D)
    conv_w_full = jnp.transpose(tiny_all[:, N_META:N_META + 4, 0:320], (1, 0, 2)).reshape(4, RGW)

    small = dict(norm1_g=norm1_g, conv_w=conv_w_full, conv_b=conv_b, rg_wa=rg_wa[0], rg_ba=rg_ba, rg_wx=rg_wx[0],
                 rg_bx=rg_bx, rg_lambda=rg_lambda, hg_lb_logits=hg_lb_logits, hg_norm_g=hg_norm_g,
                 norm2_g=norm2_g, norm_f_g=norm_f_g.reshape(1, D))

    hosts = {
        "mm_proj": lambda g: _gather_host([slots[k] for k in later], [layouts[k] for k in later]),
        "mm_d_act": lambda g: _exchange_host(["w_ffn_down"], [g["w_ffn_down"]]),
        "mm_d_hn2": lambda g: _exchange_host(["w_ffn_in"], [g["w_ffn_in"]]),
        "mm_d_win": lambda g: _join_hosts([
            _exchange_host(["w_out", "w_proj_a", "w_proj_b"], [g["w_out"], g["w_proj_a"], g["w_proj_b"]]),
            *[_pack_host(_pack_rows(g, PACKS[p])[0]) for p in EARLY]]),
        "mm_d_hn1": lambda g: _exchange_host(["w_in"], [_unrotate_g_in(g["w_in"])]),
    }
    h0 = jnp.concatenate([jnp.zeros((PAD, D), F32), meta_full, x[0]], axis=0)
    loss_blk, grad_x, g, carried = _local_step(
        h0, loss_target[0], {"w_in": w_in_full}, small, hosts=hosts,
        finalize=lambda gathered: {k: full_matrix(k, gth) for k, gth in zip(later, gathered)})
    g["loss"] = jnp.broadcast_to(loss_blk[0:1, 0:1], (1, D))

    halves = {}
    sources = {"mm_d_act": ["w_ffn_down"], "mm_d_hn2": ["w_ffn_in"], "mm_d_win": ["w_out", "w_proj_a", "w_proj_b"],
               "mm_d_hn1": ["w_in"]}
    for name, keys in sources.items():
        partials, received = carried[name]
        for k, part, rec in zip(keys, partials, received):
            halves[k] = _sum_place(rec, part, k, xyc, "sum_" + k)
    late, _ = _pack_rows(g, PACKS["late"])
    tail = _run_host(_join_hosts([_sibling_host([halves[k] for k in BIG]), _pack_host(late)]), "swap_and_late_pack")
    g_big = dict(zip(BIG, tail[:len(BIG)]))
    recv_packs = dict(zip(EARLY, carried["mm_d_win"][1][3:]), late=tail[len(BIG)])

    grad, delta, new_m, new_v, summed = {}, {}, {}, {}, {}
    for pack, names in PACKS.items():
        row, params, extras = 0, [], []
        for k in names:
            if k in NO_UPDATE:
                extras.append((row, ROWS[k][0]))
            else:
                params.append((row,) + tuple(d[k].reshape(ROWS[k]) for d in (w, m, v)))
            row += ROWS[k][0]
        updated, rows_only = _adamw_rows(recv_packs[pack], params, extras, "adamw_" + pack)
        for k, res in zip([k for k in names if k not in NO_UPDATE], updated):
            grad[k], delta[k], new_m[k], new_v[k] = (a.reshape(w[k].shape) for a in res)
        summed.update(zip([k for k in names if k in NO_UPDATE], rows_only))
    loss = summed["loss"][0, 0]
    g_shard = {"meta_tokens": lax.dynamic_slice(summed["meta_tokens"], (0, chip * 256), (N_META, 256)),
               "conv_w": lax.dynamic_slice(summed["conv_w"], (0, chip * 320), (4, 320))}
    for k in BIG + SHARDED_SMALL:
        gk = g_big[k] if k in BIG else g_shard[k]
        wk, mk, vk = (d[k].reshape(gk.shape) for d in (w, m, v))
        dk, mk, vk = _adamw(wk, gk, mk, vk, "adamw_" + k)
        grad[k], delta[k], new_m[k], new_v[k] = (a.reshape(w[k].shape) for a in (gk, dk, mk, vk))

    return (loss, grad_x[None], *[grad[k] for k in ORDER], *[delta[k] for k in ORDER],
            *[new_m[k] for k in ORDER], *[new_v[k] for k in ORDER])
```

```python
import jax
import jax.numpy as jnp
from jax import lax
from jax.experimental import pallas as pl
from jax.experimental.pallas import tpu as pltpu

F32, BF16 = jnp.float32, jnp.bfloat16
D = 1024
N_META = 16
RGW = 1280
RG_BLOCKS, RG_BLOCK = 16, 80
RG_C = 8.0
HEADS, HD = 8, 128
HGW = HEADS * HD
DFF = 2816
D_IN = 2 * RGW + 6 * D
ROT = 2 * RGW
COL_PAD = 256
D_INP = D_IN + COL_PAD
EPS = 1e-6
CH = 64
TM = 256
ROW0 = TM
PAD = ROW0 - N_META
CT = RGW
RS = 16
NCT = RGW // CT
EXP_CLAMP = 80.0
VMEM_LIMIT = 56 * 1024 * 1024

LR, B1, B2, ADAM_EPS, WD, STEP = 0.001, 0.9, 0.999, 1e-08, 0.01, 10
MESH = pl.DeviceIdType.MESH
ANY = pl.BlockSpec(memory_space=pl.ANY)


def _cparams(sem):
    return pltpu.CompilerParams(dimension_semantics=sem, vmem_limit_bytes=VMEM_LIMIT)


def _pick(n, prefs):
    for p in prefs:
        if n % p == 0:
            return p
    return n


def _sig(x):
    return 0.5 * jnp.tanh(0.5 * x) + 0.5


def _dot(a, b, dims):
    return lax.dot_general(a, b, (dims, ((), ())), preferred_element_type=F32)


NN, NT, TN_ = ((1,), (0,)), ((1,), (1,)), ((0,), (0,))


class _Host:
    def __init__(self, ins, out_shapes, aliases, sems, start, mid, finish):
        self.ins, self.out_shapes, self.aliases, self.sems = ins, out_shapes, aliases, sems
        self.start, self.mid, self.finish = start, mid, finish


class _Epi:
    def __init__(self, ins, in_specs, out_shapes, out_specs, fn, sequential=False, split=1):
        self.ins, self.in_specs, self.out_shapes, self.out_specs, self.fn = ins, in_specs, out_shapes, out_specs, fn
        self.sequential = sequential
        self.split = split


class _Pro:
    def __init__(self, ins, in_specs, fn):
        self.ins, self.in_specs, self.fn = ins, in_specs, fn


def _norm_pro(gain):
    def fn(a_ref, ins):
        xh, _ = _rms(a_ref[...])
        return (xh * ins[0][...]).astype(BF16)

    return _Pro([gain], [pl.BlockSpec((1, gain.shape[1]), lambda i, j, k: (0, 0))], fn)


def _mm(a, b, mode, out_dtype, name, resid=None, host=None, epi=None, tiles=None, sparse=None, pro=None):
    if mode == "nn":
        (m, kd), n = a.shape, b.shape[1]
    elif mode == "nt":
        (m, kd), n = a.shape, b.shape[0]
    else:
        (kd, m), n = a.shape, b.shape[1]
    if mode == "tn":
        tm = _pick(m, (1024, 1280, 1408, 640, 512, 256, 128))
        tk = _pick(kd, (1408, 768, 512, 256, 128))
    else:
        tm = _pick(m, (768, 512, 640, 256, 128))
        tk = kd if kd <= 2816 else _pick(kd, (1792, 1408, 1024, 512, 256, 128))
    tn = _pick(n, (1792, 1408, 1280, 1024, 512, 256, 128))
    if tiles is not None:
        tm, tn = tiles
    a_map = (lambda i, j, k: (k, i)) if mode == "tn" else (lambda i, j, k: (i, k))
    b_map = (lambda i, j, k: (j, k)) if mode == "nt" else (lambda i, j, k: (k, j))
    o_map = lambda i, j, k: (i, j)
    if sparse is not None:
        (ni, nj, nk), (tm, tn, tk), (a_map, b_map, o_map) = sparse
    else:
        assert m % tm == 0 and n % tn == 0 and kd % tk == 0, (name, m, n, kd, tm, tn, tk)
        ni, nj, nk = m // tm, n // tn, kd // tk
    dims = {"nn": NN, "nt": NT, "tn": TN_}[mode]

    n_hin = len(host.ins) if host else 0
    n_hout = len(host.out_shapes) if host else 0
    n_res = 0 if resid is None else 1
    n_pro = 0 if pro is None else 1
    n_pin = len(pro.ins) if pro else 0
    n_ein = len(epi.ins) if epi else 0
    n_out = len(epi.out_shapes) if epi else 1
    assert not n_pro or (mode == "nn" and nk == 1)

    def finish(r, r_ref, e_in, o_refs, i, rows=slice(None), first=True):
        if resid is not None:
            r = r + r_ref[rows, :].astype(F32)
        if epi:
            epi.fn(r, e_in, o_refs, i, rows, first)
        else:
            o_refs[0][rows, :] = r.astype(out_dtype)

    def body(*refs):
        a_ref, b_ref = refs[:2]
        r_ref = refs[2] if resid is not None else None
        pos = 2 + n_res
        p_in = refs[pos:pos + n_pin]
        pos += n_pin
        e_in = refs[pos:pos + n_ein]
        pos += n_ein
        h_in = refs[pos:pos + n_hin]
        pos += n_hin
        o_ref = refs[pos:pos + n_out]
        pos += n_out
        hn_ref = refs[pos] if n_pro else None
        pos += n_pro
        h_out = refs[pos:pos + n_hout]
        scratch = refs[pos + n_hout:]
        n_acc = 1 if nk > 1 else 0
        h_sems = scratch[n_acc + n_pro:]
        i, j, k = pl.program_id(0), pl.program_id(1), pl.program_id(2)
        if host:
            @pl.when((i == 0) & (j == 0) & (k == 0))
            def _():
                host.start(h_in, h_out, h_sems)

            if host.mid is not None:
                @pl.when((i == (3 * ni) // 4) & (j == 0) & (k == 0))
                def _():
                    host.mid(h_in, h_out, h_sems)

        if n_pro:
            a_s = scratch[n_acc]

            @pl.when(j == 0)
            def _():
                a_s[...] = pro.fn(a_ref, p_in)
                hn_ref[...] = a_s[...]

            a_val = a_s[...]
        else:
            a_val = a_ref[...]

        if nk == 1 and epi and epi.split > 1 and mode != "tn":
            part = tm // epi.split
            for p in range(epi.split):
                rows = slice(p * part, (p + 1) * part)
                finish(_dot(a_val[rows, :], b_ref[...], dims), r_ref, e_in, o_ref, i, rows, p == 0)
        elif nk == 1:
            finish(_dot(a_val, b_ref[...], dims), r_ref, e_in, o_ref, i)
        else:
            acc = scratch[0]

            @pl.when(k == 0)
            def _():
                acc[...] = jnp.zeros_like(acc)

            acc[...] += _dot(a_val, b_ref[...], dims)

            @pl.when(k == nk - 1)
            def _():
                finish(acc[...], r_ref, e_in, o_ref, i)

        if host:
            @pl.when((i == ni - 1) & (j == nj - 1) & (k == nk - 1))
            def _():
                host.finish(h_in, h_out, h_sems)

    a_spec = pl.BlockSpec((tk, tm) if mode == "tn" else (tm, tk), a_map)
    b_spec = pl.BlockSpec((tn, tk) if mode == "nt" else (tk, tn), b_map)
    o_spec = pl.BlockSpec((tm, tn), o_map)
    in_specs, args = [a_spec, b_spec], [a, b]
    if resid is not None:
        in_specs.append(o_spec)
        args.append(resid)
    if n_pro:
        in_specs += list(pro.in_specs)
        args += list(pro.ins)
    out_shapes, out_specs = [jax.ShapeDtypeStruct((m, n), out_dtype)], [o_spec]
    if epi:
        in_specs += list(epi.in_specs)
        args += list(epi.ins)
        out_shapes, out_specs = list(epi.out_shapes), list(epi.out_specs)
    if n_pro:
        out_shapes.append(jax.ShapeDtypeStruct((m, kd), BF16))
        out_specs.append(pl.BlockSpec((tm, kd), lambda i, j, k: (i, 0)))
    scratch = ([pltpu.VMEM((tm, tn), F32)] if nk > 1 else []) + ([pltpu.VMEM((tm, kd), BF16)] if n_pro else [])
    sequential = host or n_pro or (epi and epi.sequential)
    outs = pl.pallas_call(
        body, name=name, grid=(ni, nj, nk),
        in_specs=in_specs + [ANY] * n_hin, out_specs=out_specs + [ANY] * n_hout,
        out_shape=out_shapes + (list(host.out_shapes) if host else []),
        scratch_shapes=scratch + (list(host.sems) if host else []),
        input_output_aliases=({2 + n_res + n_pin + n_ein + i_in: n_out + n_pro + i_out
                               for i_in, i_out in host.aliases.items()} if host else {}),
        compiler_params=_cparams(("arbitrary",) * 3 if sequential else ("parallel", "parallel", "arbitrary")),
    )(*args, *(host.ins if host else []))
    main = list(outs[:n_out]) if epi else outs[0]
    if n_pro:
        main = (main, outs[n_out])
    return (main, list(outs[n_out + n_pro:])) if host else main


def _rms(x):
    r = lax.rsqrt(jnp.mean(x * x, axis=-1, keepdims=True) + EPS)
    return x * r, r


def _rms_bwd_math(dhn, x, g):
    xh, r = _rms(x)
    dxh = dhn * g
    dx = r * (dxh - xh * jnp.mean(dxh * xh, axis=-1, keepdims=True))
    return dx, jnp.sum(dhn * xh, axis=0, keepdims=True)


def _row_specs(tm):
    return pl.BlockSpec((tm, D), lambda i, j, k: (i, 0)), pl.BlockSpec((1, D), lambda i, j, k: (0, 0))


def _final_loss_epi(gf, target, t, tm):
    parts = tm // TM

    def fn(x, ins, outs, i, rows, first):
        g_ref = ins[0]
        p = rows.start // TM
        t_ref = ins[1 + p]
        dh_ref, dhb_ref, loss_ref, dg_ref = outs

        if first:
            @pl.when(i == 0)
            def _():
                loss_ref[...] = jnp.zeros_like(loss_ref)
                dg_ref[...] = jnp.zeros_like(dg_ref)

        g = g_ref[...]
        xh, _ = _rms(x)
        err = jnp.where(i * parts + p > 0, xh * g - t_ref[...], 0.0)
        loss_ref[...] += 0.5 * jnp.sum(jnp.sum(err * err, axis=-1, keepdims=True) * (1.0 / D))
        dx, dg = _rms_bwd_math(err * (1.0 / D), x, g)
        dh_ref[rows, :] = dx
        dhb_ref[rows, :] = dx.astype(BF16)
        dg_ref[...] += dg

    row, vec = _row_specs(tm)
    t_specs = [pl.BlockSpec((TM, D), lambda i, j, k, p=p: (jnp.maximum(i * parts + p - 1, 0), 0)) for p in range(parts)]
    return _Epi(ins=[gf] + [target] * parts, in_specs=[vec] + t_specs,
                out_shapes=[jax.ShapeDtypeStruct((t, D), F32), jax.ShapeDtypeStruct((t, D), BF16),
                            jax.ShapeDtypeStruct((8, 128), F32), jax.ShapeDtypeStruct((1, D), F32)],
                out_specs=[row, row, pl.BlockSpec((8, 128), lambda i, j, k: (0, 0)), vec], fn=fn, sequential=True,
                split=parts)


def _rms_bwd_epi(h, g, dres, tm, bf16_copy):
    t = h.shape[0]

    def fn(dhn, ins, outs, i, rows, first):
        h_ref, g_ref, dres_ref = ins
        dg_ref = outs[-1]

        if first:
            @pl.when(i == 0)
            def _():
                dg_ref[...] = jnp.zeros_like(dg_ref)

        dx, dg = _rms_bwd_math(dhn, h_ref[rows, :], g_ref[...])
        dh = dres_ref[rows, :] + dx
        outs[0][rows, :] = dh
        if bf16_copy:
            outs[1][rows, :] = dh.astype(BF16)
        dg_ref[...] += dg

    row, vec = _row_specs(tm)
    copies = [jax.ShapeDtypeStruct((t, D), F32)] + ([jax.ShapeDtypeStruct((t, D), BF16)] if bf16_copy else [])
    return _Epi(ins=[h, g, dres], in_specs=[row, vec, row],
                out_shapes=copies + [jax.ShapeDtypeStruct((1, D), F32)],
                out_specs=[row] * len(copies) + [vec], fn=fn, sequential=True)


FH = DFF // 2
FM = 384


def _swiglu_epi(t):
    def fn(r, _, outs, i, rows, first):
        g, u = r[:, :FH], r[:, FH:]
        outs[0][rows, :] = r.astype(BF16)
        outs[1][rows, :] = (g * _sig(g) * u).astype(BF16)

    return _Epi(ins=[], in_specs=[],
                out_shapes=[jax.ShapeDtypeStruct((t, 2 * DFF), BF16), jax.ShapeDtypeStruct((t, DFF), BF16)],
                out_specs=[pl.BlockSpec((FM, 2 * FH), lambda i, j, k: (i, j)), pl.BlockSpec((FM, FH), lambda i, j, k: (i, j))],
                fn=fn)


def _swiglu_bwd_epi(gu):
    def fn(d, ins, outs, i, rows, first):
        gu_t = ins[0][rows, :].astype(F32)
        g, u = gu_t[:, :FH], gu_t[:, FH:]
        s = _sig(g)
        outs[0][rows, :FH] = (d * u * (s * (1.0 + g * (1.0 - s)))).astype(BF16)
        outs[0][rows, FH:] = (d * (g * s)).astype(BF16)

    spec = pl.BlockSpec((FM, 2 * FH), lambda i, j, k: (i, j))
    return _Epi(ins=[gu], in_specs=[spec], out_shapes=[jax.ShapeDtypeStruct(gu.shape, BF16)], out_specs=[spec], fn=fn)


COL_GA, COL_GB = 4, 5
COL_AX, COL_AG = (6 * D + COL_PAD) // CT, (6 * D + COL_PAD + RGW) // CT


def _merge_pro(pb, proj, tm):
    def fn(a_ref, ins):
        pa, pb_, ga, gb = (r[...].astype(F32) for r in (a_ref,) + tuple(ins))
        return (_sig(ga) * pa + _sig(gb) * pb_).astype(BF16)

    row, _ = _row_specs(tm)
    return _Pro([pb, proj, proj], [row, pl.BlockSpec((tm, D), lambda i, j, k: (i, COL_GA)),
                                   pl.BlockSpec((tm, D), lambda i, j, k: (i, COL_GB))], fn)


def _merge_bwd_epi(pa, pb, proj, tm):
    t = pa.shape[0]

    def fn(d, ins, outs, i, rows, first):
        pa, pb, ga, gb = (r[rows, :].astype(F32) for r in ins)
        dpa_ref, dpb_ref, dproj_ref = outs
        sa, sb = _sig(ga), _sig(gb)
        dpa_ref[rows, :] = (sa * d).astype(BF16)
        dpb_ref[rows, :] = (sb * d).astype(BF16)
        dproj_ref[rows, :D] = (d * pa * sa * (1.0 - sa)).astype(BF16)
        dproj_ref[rows, D:] = (d * pb * sb * (1.0 - sb)).astype(BF16)

    row, _ = _row_specs(tm)
    return _Epi(ins=[pa, pb, proj, proj],
                in_specs=[row, row, pl.BlockSpec((tm, D), lambda i, j, k: (i, COL_GA)),
                          pl.BlockSpec((tm, D), lambda i, j, k: (i, COL_GB))],
                out_shapes=[jax.ShapeDtypeStruct((t, D), BF16), jax.ShapeDtypeStruct((t, D), BF16),
                            jax.ShapeDtypeStruct((t, D_INP), BF16)],
                out_specs=[row, row, pl.BlockSpec((tm, 2 * D), lambda i, j, k: (i, 2))], fn=fn)


def _zero_pad_cols(dproj):
    t = dproj.shape[0]
    tz = _pick(t, (768, 256))

    def body(_, o_ref):
        o_ref[...] = jnp.zeros_like(o_ref)

    return pl.pallas_call(
        body, name="dproj_pad", grid=(t // tz,), in_specs=[ANY],
        out_specs=pl.BlockSpec((tz, COL_PAD), lambda i: (i, (6 * D) // COL_PAD)),
        out_shape=jax.ShapeDtypeStruct(dproj.shape, BF16), input_output_aliases={0: 0},
        compiler_params=_cparams(("parallel",)),
    )(dproj)


HALO = 16


def _rows_before(cur, before, s):
    n = cur.shape[0]
    return jnp.concatenate([before, cur], axis=0)[n - s:2 * n - s, :]


def _rows_after(cur, after, s):
    n = cur.shape[0]
    return jnp.concatenate([cur, after], axis=0)[s:n + s, :]


def _prev_halo(col0):
    return pl.BlockSpec((HALO, CT), lambda c, i: (jnp.maximum(i * (TM // HALO) - 1, 0), col0 + c))


def _conv_fwd(proj, w, b):
    t = proj.shape[0]

    def body(x_ref, halo_ref, w_ref, b_ref, ob_ref):
        i = pl.program_id(1)
        wv, bv = w_ref[...], b_ref[...]
        first = jnp.where(i > 0, halo_ref[...].astype(F32), 0.0)

        def strip(k, before):
            rows = pl.ds(pl.multiple_of(k * HALO, HALO), HALO)
            x = x_ref[rows, :].astype(F32)
            y = bv + wv[3:4, :] * x
            for j in range(3):
                y = y + wv[j:j + 1, :] * _rows_before(x, before, 3 - j)
            ob_ref[rows, :] = y.astype(BF16)
            return x

        lax.fori_loop(0, TM // HALO, strip, first)

    blk = pl.BlockSpec((TM, CT), lambda c, i: (i, c))
    return pl.pallas_call(
        body, name="conv_fwd", grid=(NCT, t // TM),
        in_specs=[pl.BlockSpec((TM, CT), lambda c, i: (i, COL_AX + c)), _prev_halo(COL_AX),
                  pl.BlockSpec((4, CT), lambda c, i: (0, c)), pl.BlockSpec((1, CT), lambda c, i: (0, c))],
        out_specs=blk,
        out_shape=jax.ShapeDtypeStruct((t, RGW), BF16),
        compiler_params=_cparams(("parallel", "parallel")),
    )(proj, proj, w, b)


def _conv_bwd(dxc, proj, w, dproj):
    t = proj.shape[0]
    nt = t // TM

    def body(d_ref, dn_ref, x_ref, halo_ref, w_ref, _, dx_ref, dw_ref, db_ref, acc):
        i = pl.program_id(1)

        @pl.when(i == 0)
        def _():
            dw_ref[...] = jnp.zeros_like(dw_ref)
            db_ref[...] = jnp.zeros_like(db_ref)

        acc[...] = jnp.zeros_like(acc)
        wv = w_ref[...]
        first = jnp.where(i > 0, halo_ref[...].astype(F32), 0.0)
        last = jnp.where(i < nt - 1, dn_ref[...].astype(F32), 0.0)
        ns = TM // HALO

        def fold(v):
            return v[0:8, :] + v[8:16, :]

        def strip(k, before):
            off = pl.multiple_of(k * HALO, HALO)
            rows = pl.ds(off, HALO)
            x, d = x_ref[rows, :].astype(F32), d_ref[rows, :].astype(F32)
            nxt = d_ref[pl.ds(pl.multiple_of(jnp.minimum(off + HALO, TM - HALO), HALO), HALO), :].astype(F32)
            after = jnp.where(k == ns - 1, last, nxt)
            dx = wv[3:4, :] * d
            for j in range(3):
                dx = dx + wv[j:j + 1, :] * _rows_after(d, after, 3 - j)
                acc[j] += fold(d * _rows_before(x, before, 3 - j))
            acc[3] += fold(d * x)
            acc[4] += fold(d)
            dx_ref[rows, :] = dx.astype(BF16)
            return x

        lax.fori_loop(0, ns, strip, first)
        for j in range(4):
            dw_ref[j:j + 1, :] += jnp.sum(acc[j], axis=0, keepdims=True)
        db_ref[...] += jnp.sum(acc[4], axis=0, keepdims=True)

    return pl.pallas_call(
        body, name="conv_bwd", grid=(NCT, nt),
        in_specs=[pl.BlockSpec((TM, CT), lambda c, i: (i, c)),
                  pl.BlockSpec((HALO, CT), lambda c, i: (jnp.minimum((i + 1) * (TM // HALO), t // HALO - 1), c)),
                  pl.BlockSpec((TM, CT), lambda c, i: (i, COL_AX + c)), _prev_halo(COL_AX),
                  pl.BlockSpec((4, CT), lambda c, i: (0, c)), ANY],
        out_specs=[pl.BlockSpec((TM, CT), lambda c, i: (i, COL_AX + c)),
                   pl.BlockSpec((4, CT), lambda c, i: (0, c)), pl.BlockSpec((1, CT), lambda c, i: (0, c))],
        out_shape=[jax.ShapeDtypeStruct((t, D_INP), BF16), jax.ShapeDtypeStruct((4, RGW), F32),
                   jax.ShapeDtypeStruct((1, RGW), F32)],
        scratch_shapes=[pltpu.VMEM((5, 8, CT), F32)],
        input_output_aliases={5: 0},
        compiler_params=_cparams(("parallel", "arbitrary")),
    )(dxc, dxc, proj, proj, w, dproj)


def _gelu(x):
    c = 0.7978845608028654
    th = jnp.tanh(c * (x + 0.044715 * x * x * x))
    return 0.5 * x * (1.0 + th), th


def _rg_gates(gr, gi, xc, ba, bx, lam, row0):
    r = _sig(gr + ba)
    ig = _sig(gi + bx)
    sp = jax.nn.softplus(-lam)
    a = jnp.exp(-RG_C * r * sp)
    s2 = jnp.maximum(1.0 - a * a, 1e-30)
    rs = lax.rsqrt(s2)
    s = s2 * rs
    rows = row0 + lax.broadcasted_iota(jnp.int32, gr.shape, 0)
    live = rows >= PAD
    u = jnp.where(live, s * ig * xc, 0.0)
    return r, ig, sp, a, s, rs, u, live


def _rg_fwd(gri, xc, proj, ba, bx, lam):
    t = gri.shape[0]

    def body(gr_ref, gi_ref, xc_ref, ag_ref, ba_ref, bx_ref, lam_ref, h_ref, ya_ref, hc):
        i = pl.program_id(1)

        @pl.when(i == 0)
        def _():
            hc[...] = jnp.zeros_like(hc)

        ba, bx, lam = ba_ref[...], bx_ref[...], lam_ref[...]
        sub = lax.broadcasted_iota(jnp.int32, (8, CT), 0)

        def strip(k, h):
            off = pl.multiple_of(k * RS, RS)
            rows = pl.ds(off, RS)
            _, _, _, a, _, _, u, _ = _rg_gates(gr_ref[rows, :].astype(F32), gi_ref[rows, :].astype(F32),
                                               xc_ref[rows, :].astype(F32), ba, bx, lam,
                                               i * TM + off)
            outs = []
            for half in range(RS // 8):
                out = jnp.zeros((8, CT), F32)
                for r in range(8):
                    h = a[8 * half + r:8 * half + r + 1, :] * h + u[8 * half + r:8 * half + r + 1, :]
                    out = jnp.where(sub == r, h, out)
                outs.append(out)
            hs = jnp.concatenate(outs, axis=0)
            h_ref[rows, :] = hs
            ge, _ = _gelu(ag_ref[rows, :].astype(F32))
            ya_ref[rows, :] = (hs * ge).astype(BF16)
            return h

        hc[...] = lax.fori_loop(0, TM // RS, strip, hc[...])

    blk_ = pl.BlockSpec((TM, CT), lambda c, i: (i, c))
    vec = pl.BlockSpec((1, CT), lambda c, i: (0, c))
    return pl.pallas_call(
        body, name="rg_fwd", grid=(NCT, t // TM),
        in_specs=[blk_, pl.BlockSpec((TM, CT), lambda c, i: (i, NCT + c)), blk_,
                  pl.BlockSpec((TM, CT), lambda c, i: (i, COL_AG + c)), vec, vec, vec],
        out_specs=[blk_, blk_],
        out_shape=[jax.ShapeDtypeStruct((t, RGW), F32), jax.ShapeDtypeStruct((t, RGW), BF16)],
        scratch_shapes=[pltpu.VMEM((1, CT), F32)],
        compiler_params=_cparams(("parallel", "arbitrary")),
    )(gri, gri, xc, proj, ba, bx, lam)


def _rg_bwd(dya, h, gri, xc, proj, ba, bx, lam, dproj):
    t = gri.shape[0]
    nt = t // TM
    assert NCT == 1

    def body(dya_ref, h_ref, hh_ref, gr_ref, gi_ref, xc_ref, ag_ref, ba_ref, bx_ref, lam_ref, _,
             dgri_ref, dxc_ref, dag_ref, dba_ref, dbx_ref, dlam_ref, hbuf, acc, cc):
        i = pl.program_id(1)
        ri = nt - 1 - i

        @pl.when(i == 0)
        def _():
            cc[...] = jnp.zeros_like(cc)
            dba_ref[...] = jnp.zeros_like(dba_ref)
            dbx_ref[...] = jnp.zeros_like(dbx_ref)
            dlam_ref[...] = jnp.zeros_like(dlam_ref)

        acc[...] = jnp.zeros_like(acc)
        ba, bx, lam = ba_ref[...], bx_ref[...], lam_ref[...]
        halo_last = jnp.where(ri > 0, hh_ref[7:8, :], 0.0)
        sub = lax.broadcasted_iota(jnp.int32, (8, CT), 0)
        c0 = 0.7978845608028654

        def strip(kk, c):
            k = TM // RS - 1 - kk
            off = pl.multiple_of(k * RS, RS)
            rows = pl.ds(off, RS)
            xc, hv = xc_ref[rows, :].astype(F32), h_ref[rows, :]
            ag, dya = ag_ref[rows, :].astype(F32), dya_ref[rows, :].astype(F32)
            r, ig, sp, a, s, rs, _, live = _rg_gates(gr_ref[rows, :].astype(F32), gi_ref[rows, :].astype(F32), xc,
                                                     ba, bx, lam, ri * TM + off)
            ge, th = _gelu(ag)
            dge = 0.5 * (1.0 + th) + 0.5 * ag * (1.0 - th * th) * c0 * (1.0 + 3.0 * 0.044715 * ag * ag)
            dag_ref[rows, :] = (dya * hv * dge).astype(BF16)
            d = dya * ge
            outs = []
            for half in range(RS // 8 - 1, -1, -1):
                out = jnp.zeros((8, CT), F32)
                for rr in range(7, -1, -1):
                    g = d[8 * half + rr:8 * half + rr + 1, :] + c
                    c = a[8 * half + rr:8 * half + rr + 1, :] * g
                    out = jnp.where(sub == rr, g, out)
                outs.insert(0, out)
            g = jnp.concatenate(outs, axis=0)
            before = h_ref[pl.ds(pl.multiple_of(jnp.maximum(off - RS, 0), RS), RS), :]
            hbuf[7:8, :] = jnp.where(k == 0, halo_last, before[RS - 1:RS, :])
            hbuf[8:, :] = hv
            hprev = hbuf[pl.ds(7, RS), :]
            du = jnp.where(live, g, 0.0)
            ds = du * (ig * xc)
            dm = du * s
            dla = (g * hprev - a * ds * rs) * a
            dr = dla * (-RG_C * sp) * r * (1.0 - r)
            di = dm * xc * ig * (1.0 - ig)
            dgri_ref[rows, :CT] = dr.astype(BF16)
            dgri_ref[rows, CT:] = di.astype(BF16)
            dxc_ref[rows, :] = (dm * ig).astype(BF16)
            dl = dla * (-RG_C * r)
            for half in range(RS // 8):
                part = slice(8 * half, 8 * half + 8)
                acc[0] += dr[part]
                acc[1] += di[part]
                acc[2] += dl[part]
            return c

        cc[...] = lax.fori_loop(0, TM // RS, strip, cc[...])
        dba_ref[...] += jnp.sum(acc[0], axis=0, keepdims=True)
        dbx_ref[...] += jnp.sum(acc[1], axis=0, keepdims=True)
        dlam_ref[...] += jnp.sum(acc[2], axis=0, keepdims=True) * (-_sig(-lam))

    rblk = pl.BlockSpec((TM, CT), lambda c, i: (nt - 1 - i, c))
    vec = pl.BlockSpec((1, CT), lambda c, i: (0, c))
    hh = pl.BlockSpec((8, CT), lambda c, i: (jnp.maximum((nt - 1 - i) * (TM // 8) - 1, 0), c))
    agb = pl.BlockSpec((TM, CT), lambda c, i: (nt - 1 - i, COL_AG + c))
    return pl.pallas_call(
        body, name="rg_bwd", grid=(NCT, nt),
        in_specs=[rblk, rblk, hh, rblk, pl.BlockSpec((TM, CT), lambda c, i: (nt - 1 - i, NCT + c)), rblk, agb,
                  vec, vec, vec, ANY],
        out_specs=[pl.BlockSpec((TM, 2 * CT), lambda c, i: (nt - 1 - i, c)), rblk, agb, vec, vec, vec],
        out_shape=[jax.ShapeDtypeStruct((t, 2 * RGW), BF16),
                   jax.ShapeDtypeStruct((t, RGW), BF16), jax.ShapeDtypeStruct((t, D_INP), BF16),
                   jax.ShapeDtypeStruct((1, RGW), F32), jax.ShapeDtypeStruct((1, RGW), F32),
                   jax.ShapeDtypeStruct((1, RGW), F32)],
        scratch_shapes=[pltpu.VMEM((RS + 8, CT), F32), pltpu.VMEM((3, 8, CT), F32), pltpu.VMEM((1, CT), F32)],
        input_output_aliases={10: 2},
        compiler_params=_cparams(("parallel", "arbitrary")),
    )(dya, h, h, gri, gri, xc, proj, ba, bx, lam, dproj)


NCH = TM // CH

def _tri_dot(tri, x):
    hi = x.astype(BF16)
    r1 = x - hi.astype(F32)
    mid = r1.astype(BF16)
    lo = (r1 - mid.astype(F32)).astype(BF16)
    return _dot(tri, hi, NN) + _dot(tri, mid, NN) + _dot(tri, lo, NN)


def _hg_chunk(qr, fr, lb):
    sf = _sig(fr)
    fg = lb + (1.0 - lb) * sf
    k = (1.0 - lb) * (1.0 - sf)
    sq = _sig(qr)
    q = qr * sq
    ri = lax.broadcasted_iota(jnp.int32, (CH, CH), 0)
    ci = lax.broadcasted_iota(jnp.int32, (CH, CH), 1)
    b = _tri_dot((ri >= ci).astype(BF16), jnp.log(fg))
    bm, bl = b[CH // 2 - 1:CH // 2, :], b[CH - 1:CH, :]
    ebm = jnp.exp(jnp.minimum(b - bm, EXP_CLAMP))
    ekm = jnp.exp(jnp.minimum(bm - b, EXP_CLAMP))
    eb = ebm * jnp.exp(bm)
    ekl = ekm * jnp.exp(bl - bm)
    return dict(sf=sf, fg=fg, k=k, sq=sq, q=q, eb=eb, ebm=ebm, ekm=ekm, ekl=ekl, ebl=jnp.exp(bl),
                qe=q * eb, qh=q * ebm, kh=k * ekm, kd=k * ekl, causal=ri >= ci, upper=(ci >= ri).astype(BF16))


def _hgrn_fwd(proj, lbl, gn):
    t = proj.shape[0]
    nt = t // TM

    def body(q_ref, f_ref, v_ref, g_ref, lbl_ref, gn_ref, yb_ref, o_ref, st_ref, st):
        @pl.when(pl.program_id(0) == 0)
        def _():
            st[...] = jnp.zeros_like(st)

        l = lbl_ref[...]
        lb = _sig(l[0:1, :] - l[1:2, :])
        gnv = gn_ref[...]

        def chunk(c, carry):
            off = pl.multiple_of(c * CH, CH)
            rows = pl.ds(off, CH)
            z = _hg_chunk(q_ref[rows, :].astype(F32), f_ref[rows, :].astype(F32), lb)
            v, gg = v_ref[rows, :], g_ref[rows, :].astype(F32)
            for hh in range(HEADS):
                sl = slice(hh * HD, (hh + 1) * HD)
                s_prev = st[hh]
                st_ref[c, hh] = s_prev
                vb = v[:, sl].astype(BF16)
                att = jnp.where(z["causal"], _dot(z["qh"][:, sl].astype(BF16), z["kh"][:, sl].astype(BF16), NT), 0.0)
                o = _dot(z["qe"][:, sl].astype(BF16), s_prev.astype(BF16), NT) + _dot(att.astype(BF16), vb, NN)
                st[hh] = s_prev * z["ebl"][:, sl] + _dot(vb, z["kd"][:, sl].astype(BF16), TN_)
                xh, _ = _rms(o)
                gh = gg[:, sl]
                o_ref[rows, sl] = o
                yb_ref[rows, sl] = (xh * gnv * gh * _sig(gh)).astype(BF16)
            return carry

        lax.fori_loop(0, NCH, chunk, 0, unroll=True)

    def col(j):
        return pl.BlockSpec((TM, HGW), lambda i, j=j: (i, j))

    return pl.pallas_call(
        body, name="hgrn_fwd", grid=(nt,),
        in_specs=[col(0), col(1), col(2), col(3), pl.BlockSpec((2, HGW), lambda i: (0, 0)),
                  pl.BlockSpec((1, HD), lambda i: (0, 0))],
        out_specs=[col(0), col(0), pl.BlockSpec((NCH, HEADS, HD, HD), lambda i: (i, 0, 0, 0))],
        out_shape=[jax.ShapeDtypeStruct((t, HGW), BF16), jax.ShapeDtypeStruct((t, HGW), F32),
                   jax.ShapeDtypeStruct((t // CH, HEADS, HD, HD), F32)],
        scratch_shapes=[pltpu.VMEM((HEADS, HD, HD), F32)],
        compiler_params=_cparams(("arbitrary",)),
    )(proj, proj, proj, proj, lbl, gn)


def _hgrn_bwd(dyb, proj, o, states, lbl, gn, dproj):
    t = proj.shape[0]
    nt = t // TM

    def body(dy_ref, q_ref, f_ref, v_ref, g_ref, o_ref, st_ref, lbl_ref, gn_ref, _,
             dp_ref, dgn_ref, dl_ref, dst, dlb):
        i = pl.program_id(0)

        @pl.when(i == 0)
        def _():
            dst[...] = jnp.zeros_like(dst)
            dlb[...] = jnp.zeros_like(dlb)
            dgn_ref[...] = jnp.zeros_like(dgn_ref)

        l = lbl_ref[...]
        lb = _sig(l[0:1, :] - l[1:2, :])
        gnv = gn_ref[...]
        last = lax.broadcasted_iota(jnp.int32, (CH, HD), 0) == CH - 1

        def chunk(cc, carry):
            c = NCH - 1 - cc
            off = pl.multiple_of(c * CH, CH)
            rows = pl.ds(off, CH)
            qr, fr = q_ref[rows, :].astype(F32), f_ref[rows, :].astype(F32)
            z = _hg_chunk(qr, fr, lb)
            v, gg, ov, dy = v_ref[rows, :], g_ref[rows, :].astype(F32), o_ref[rows, :], dy_ref[rows, :].astype(F32)
            dqs, dks, dbs, dvs, dgs = [], [], [], [], []
            dgn = jnp.zeros((1, HD), F32)
            for hh in range(HEADS):
                sl = slice(hh * HD, (hh + 1) * HD)
                s_prev, ds_new = st_ref[c, hh], dst[hh]
                qe, qh, kh, kd = z["qe"][:, sl], z["qh"][:, sl], z["kh"][:, sl], z["kd"][:, sl]
                ebl = z["ebl"][:, sl]
                gh, dyh = gg[:, sl], dy[:, sl]
                xh, rr = _rms(ov[:, sl])
                sg = _sig(gh)
                dyn = dyh * (gh * sg)
                dgs.append(dyh * (xh * gnv) * (sg * (1.0 + gh * (1.0 - sg))))
                dgn = dgn + jnp.sum(dyn * xh, axis=0, keepdims=True)
                dxh = dyn * gnv
                do = (rr * (dxh - xh * jnp.mean(dxh * xh, axis=-1, keepdims=True))).astype(BF16)
                vb, dsb = v[:, sl].astype(BF16), ds_new.astype(BF16)
                qeb, qhb, khb, kdb = (a.astype(BF16) for a in (qe, qh, kh, kd))
                att = jnp.where(z["causal"], _dot(qhb, khb, NT), 0.0).astype(BF16)
                datt = jnp.where(z["causal"], _dot(do, vb, NT), 0.0).astype(BF16)
                dvs.append(_dot(att, do, TN_) + _dot(kdb, dsb, NT))
                dqe = _dot(do, s_prev.astype(BF16), NN)
                dqh = _dot(datt, khb, NN)
                dkh = _dot(datt, qhb, TN_)
                dkd = _dot(vb, dsb, NN)
                qe, qh, kh, kd = (a.astype(F32) for a in (qeb, qhb, khb, kdb))
                dbl = (jnp.sum(dkd * kd, axis=0, keepdims=True)
                       + jnp.sum(ds_new * s_prev, axis=0, keepdims=True) * ebl)
                dqs.append(dqe * z["eb"][:, sl] + dqh * z["ebm"][:, sl])
                dks.append(dkh * z["ekm"][:, sl] + dkd * z["ekl"][:, sl])
                dbs.append(dqe * qe + dqh * qh - dkh * kh - dkd * kd + jnp.where(last, dbl, 0.0))
                dst[hh] = _dot(do, qeb, TN_) + ds_new * ebl
            dgn_ref[...] += dgn
            dq, dk, db = (jnp.concatenate(x, axis=1) for x in (dqs, dks, dbs))
            dlf = _tri_dot(z["upper"], db)
            sf, fg, sq = z["sf"], z["fg"], z["sq"]
            dmix = dlf / fg - dk
            dsf = dmix * (1.0 - lb)
            dlb[...] += jnp.sum(dmix * (1.0 - sf), axis=0, keepdims=True)
            dp_ref[rows, 0:HGW] = (dq * (sq * (1.0 + qr * (1.0 - sq)))).astype(BF16)
            dp_ref[rows, HGW:2 * HGW] = (dsf * sf * (1.0 - sf)).astype(BF16)
            dp_ref[rows, 2 * HGW:3 * HGW] = jnp.concatenate(dvs, axis=1).astype(BF16)
            dp_ref[rows, 3 * HGW:4 * HGW] = jnp.concatenate(dgs, axis=1).astype(BF16)
            return carry

        lax.fori_loop(0, NCH, chunk, 0, unroll=True)
        dl0 = dlb[...] * lb * (1.0 - lb)
        dl_ref[0:1, :] = dl0
        dl_ref[1:2, :] = -dl0

    def col(j):
        return pl.BlockSpec((TM, HGW), lambda i, j=j: (nt - 1 - i, j))

    return pl.pallas_call(
        body, name="hgrn_bwd", grid=(nt,),
        in_specs=[col(0), col(0), col(1), col(2), col(3), col(0),
                  pl.BlockSpec((NCH, HEADS, HD, HD), lambda i: (nt - 1 - i, 0, 0, 0)),
                  pl.BlockSpec((2, HGW), lambda i: (0, 0)), pl.BlockSpec((1, HD), lambda i: (0, 0)), ANY],
        out_specs=[pl.BlockSpec((TM, 4 * HGW), lambda i: (nt - 1 - i, 0)),
                   pl.BlockSpec((1, HD), lambda i: (0, 0)), pl.BlockSpec((2, HGW), lambda i: (0, 0))],
        out_shape=[jax.ShapeDtypeStruct((t, D_INP), BF16), jax.ShapeDtypeStruct((1, HD), F32),
                   jax.ShapeDtypeStruct((2, HGW), F32)],
        scratch_shapes=[pltpu.VMEM((HEADS, HD, HD), F32), pltpu.VMEM((1, HGW), F32)],
        input_output_aliases={9: 0},
        compiler_params=_cparams(("arbitrary",)),
    )(dyb, proj, proj, proj, proj, o, states, lbl, gn, dproj)


def _rotate_w_in(w):
    return jnp.concatenate([w[:, ROT:], jnp.zeros((w.shape[0], COL_PAD), w.dtype), w[:, :ROT]], axis=1)


def _unrotate_g_in(g):
    return jnp.concatenate([g[:, D_INP - ROT:], g[:, :D_IN - ROT]], axis=1)


SB = RGW // 2


def _bd_gates(t):
    tm = _pick(t, (768, 256))

    def col(j):
        return 2 * (j % 2) + j // 2

    return ((t // tm, 4, 1), (tm, SB, SB),
            (lambda i, j, k: (i, j // 2), lambda i, j, k: (j // 2, col(j)), lambda i, j, k: (i, col(j))))


def _bd_dw(t):
    tk = _pick(t, (1408, 768))
    return (1, 4, t // tk), (SB, SB, tk), (lambda i, j, k: (k, j % 2), lambda i, j, k: (k, j), lambda i, j, k: (j % 2, j))


def _block_diag(w):
    eye = jnp.eye(RG_BLOCKS, dtype=w.dtype)
    return (w[:, :, None, :] * eye[:, None, :, None]).reshape(RGW, RGW)


def _diag_blocks(wd):
    w4 = wd.reshape(RG_BLOCKS, RG_BLOCK, RG_BLOCKS, RG_BLOCK)
    return jnp.stack([w4[n, :, n, :] for n in range(RG_BLOCKS)])


def _local_step(h0, target, wts, small, hosts=None, finalize=None):
    hosts = hosts or {}
    carried = {}
    g = {}
    t = h0.shape[0]

    def mm(a, b, mode, out_dtype, name, **kw):
        if name not in hosts:
            return _mm(a, b, mode, out_dtype, name, **kw)
        host = hosts[name](g)
        out, res = _mm(a, b, mode, out_dtype, name, host=host, **kw)
        carried[name] = (host.ins, res)
        return out

    w_in = wts["w_in"]
    wax_d = jnp.concatenate([_block_diag(small["rg_wa"].astype(BF16)), _block_diag(small["rg_wx"].astype(BF16))], axis=1)
    ba, bx, lam = small["rg_ba"], small["rg_bx"], small["rg_lambda"]
    lbl, gn = small["hg_lb_logits"], small["hg_norm_g"]
    conv_w, conv_b = small["conv_w"], small["conv_b"]

    tmm = _pick(t, (768, 256))
    proj, hn1 = mm(h0, w_in, "nn", BF16, "mm_proj", pro=_norm_pro(small["norm1_g"]))
    if finalize is not None:
        wts = finalize(carried["mm_proj"][1])
    w_pa, w_pb, w_out, w_fi, w_fd = (wts[k] for k in ("w_proj_a", "w_proj_b", "w_out", "w_ffn_in", "w_ffn_down"))
    xcb = _conv_fwd(proj, conv_w, conv_b)
    gri = mm(xcb, wax_d, "nn", BF16, "mm_rg_gates", sparse=_bd_gates(t))
    hrg, ya = _rg_fwd(gri, xcb, proj, ba, bx, lam)
    yb, o, states = _hgrn_fwd(proj, lbl, gn)
    pa = mm(ya, w_pa, "nn", BF16, "mm_pa")
    pb = mm(yb, w_pb, "nn", BF16, "mm_pb")
    h1, merged = mm(pa, w_out, "nn", F32, "mm_out", resid=h0, pro=_merge_pro(pb, proj, tmm), tiles=(tmm, D))
    (gu, act), hn2 = mm(h1, w_fi, "nn", BF16, "mm_ffn_in", pro=_norm_pro(small["norm2_g"]), epi=_swiglu_epi(t),
                        tiles=(FM, 2 * FH))
    dh2, dh2b, loss, g["norm_f_g"] = mm(act, w_fd, "nn", F32, "mm_ffn_down", resid=h1,
                                        epi=_final_loss_epi(small["norm_f_g"], target, t, tmm), tiles=(tmm, D))

    g["w_ffn_down"] = mm(act, dh2b, "tn", BF16, "mm_d_wfd")
    (dgu,) = mm(dh2b, w_fd, "nt", BF16, "mm_d_act", epi=_swiglu_bwd_epi(gu), tiles=(FM, FH))
    g["w_ffn_in"] = mm(hn2, dgu, "tn", BF16, "mm_d_wfi")
    dh1, dh1b, g["norm2_g"] = mm(dgu, w_fi, "nt", BF16, "mm_d_hn2", epi=_rms_bwd_epi(h1, small["norm2_g"], dh2, tmm, True),
                                 tiles=(tmm, D))
    g["w_out"] = mm(merged, dh1b, "tn", BF16, "mm_d_wout")
    dpa, dpb, dproj = mm(dh1b, w_out, "nt", BF16, "mm_d_merged", epi=_merge_bwd_epi(pa, pb, proj, tmm), tiles=(tmm, D))
    dproj = _zero_pad_cols(dproj)
    g["w_proj_a"] = mm(ya, dpa, "tn", BF16, "mm_d_wpa")
    g["w_proj_b"] = mm(yb, dpb, "tn", BF16, "mm_d_wpb")
    dya = mm(dpa, w_pa, "nt", BF16, "mm_d_ya")
    dyb = mm(dpb, w_pb, "nt", BF16, "mm_d_yb")
    dproj, g["hg_norm_g"], g["hg_lb_logits"] = _hgrn_bwd(dyb, proj, o, states, lbl, gn, dproj)
    dgri, dxc, dproj, g["rg_ba"], g["rg_bx"], g["rg_lambda"] = _rg_bwd(dya, hrg, gri, xcb, proj, ba, bx, lam, dproj)
    dwax = mm(xcb, dgri, "tn", F32, "mm_d_wax", sparse=_bd_dw(t))
    g["rg_wa"], g["rg_wx"] = _diag_blocks(dwax[:, :RGW]), _diag_blocks(dwax[:, RGW:])
    dxc = mm(dgri, wax_d, "nt", BF16, "mm_d_xc", resid=dxc)
    dproj, g["conv_w"], g["conv_b"] = _conv_bwd(dxc, proj, conv_w, dproj)
    g["w_in"] = mm(hn1, dproj, "tn", BF16, "mm_d_win")
    dh0, g["norm1_g"] = mm(dproj, w_in, "nt", BF16, "mm_d_hn1", epi=_rms_bwd_epi(h0, small["norm1_g"], dh1, tmm, False),
                           tiles=(tmm, D))
    grad_x, g["meta_tokens"] = dh0[ROW0:], dh0[PAD:ROW0]
    return loss, grad_x, g, carried


def _adamw_math(w, g, m, v):
    m = B1 * m + (1.0 - B1) * g
    v = B2 * v + (1.0 - B2) * (g * g)
    m_hat = m / (1.0 - B1 ** STEP)
    v_hat = v / (1.0 - B2 ** STEP)
    return -LR * (m_hat / (jnp.sqrt(v_hat) + ADAM_EPS) + WD * w), m, v


def _adamw(w, g, m, v, name):
    r, c = w.shape
    tr = _pick(r, (256, 352, 320, 128, 64, 32, 16, 8))

    def body(w_ref, g_ref, m_ref, v_ref, d_ref, mo_ref, vo_ref):
        d_ref[...], mo_ref[...], vo_ref[...] = _adamw_math(w_ref[...], g_ref[...], m_ref[...], v_ref[...])

    blk = pl.BlockSpec((tr, c), lambda i: (i, 0))
    return pl.pallas_call(
        body, name=name, grid=(r // tr,), in_specs=[blk] * 4, out_specs=[blk] * 3,
        out_shape=[jax.ShapeDtypeStruct((r, c), F32)] * 3,
        compiler_params=_cparams(("parallel",)),
    )(w, g, m, v)


def _adamw_rows(recv, params, extras, name):
    _, _, width = recv.shape
    n_par = len(params)

    def body(*refs):
        r_ref = refs[0]
        ins = refs[1:1 + 3 * n_par]
        outs = refs[1 + 3 * n_par:]
        g_all = r_ref[0]
        for p in range(1, 8):
            g_all = g_all + r_ref[p]
        for q, (row0, w, _, _) in enumerate(params):
            g = g_all[row0:row0 + w.shape[0], :]
            w_ref, m_ref, v_ref = ins[3 * q:3 * q + 3]
            g_ref, d_ref, mo_ref, vo_ref = outs[4 * q:4 * q + 4]
            g_ref[...] = g
            d_ref[...], mo_ref[...], vo_ref[...] = _adamw_math(w_ref[...], g, m_ref[...], v_ref[...])
        for q, (row0, n) in enumerate(extras):
            outs[4 * n_par + q][...] = g_all[row0:row0 + n, :]

    shapes = [jax.ShapeDtypeStruct(w.shape, F32) for _, w, _, _ in params for _ in range(4)]
    shapes += [jax.ShapeDtypeStruct((n, width), F32) for _, n in extras]
    vm = pl.BlockSpec(memory_space=pltpu.VMEM)
    outs = pl.pallas_call(
        body, name=name, in_specs=[vm] * (1 + 3 * n_par), out_specs=[vm] * len(shapes), out_shape=shapes,
        compiler_params=pltpu.CompilerParams(vmem_limit_bytes=VMEM_LIMIT),
    )(recv, *[a for _, w, m, v in params for a in (w, m, v)])
    return [tuple(outs[4 * q:4 * q + 4]) for q in range(n_par)], list(outs[4 * n_par:])


def _col_block(weight, px, py):
    return 2 * py + px if weight == "w_ffn_in" else 2 * px + py


def _sum_place(recv, gfull, weight, xyc, name):
    _, r, c = recv.shape
    tr = _pick(r, (128, 176, 160, 64, 32, 16))
    nb = r // tr

    def body(_, own_ref, r_ref, o_ref):
        g = own_ref[...].astype(F32)
        for p in range(7):
            g = g + r_ref[p].astype(F32)
        o_ref[...] = g

    if weight in COL_SHARDED:
        own = pl.BlockSpec((tr, c), lambda i, s: (s[2] * nb + i, _col_block(weight, s[0], s[1])))
    else:
        own = pl.BlockSpec((tr, c), lambda i, s: ((4 * s[0] + 2 * s[1] + s[2]) * nb + i, 0))
    return pl.pallas_call(
        body, name=name,
        grid_spec=pltpu.PrefetchScalarGridSpec(
            num_scalar_prefetch=1, grid=(nb,),
            in_specs=[own, pl.BlockSpec((7, tr, c), lambda i, s: (0, i, 0))],
            out_specs=pl.BlockSpec((tr, c), lambda i, s: (s[2] * nb + i, 0))),
        out_shape=jax.ShapeDtypeStruct((2 * r, c), F32),
        compiler_params=_cparams(("arbitrary",)),
    )(xyc, gfull, recv)


def _place_own(shard, slot, dtype, name):
    r, c = shard.shape
    tr = _pick(r, (256, 352, 320, 128, 32))

    def body(_, x_ref, o_ref):
        o_ref[...] = x_ref[...].astype(dtype)

    return pl.pallas_call(
        body, name=name,
        grid_spec=pltpu.PrefetchScalarGridSpec(
            num_scalar_prefetch=1, grid=(r // tr,),
            in_specs=[pl.BlockSpec((tr, c), lambda i, s: (i, 0))],
            out_specs=pl.BlockSpec((None, tr, c), lambda i, s: (s[0], i, 0))),
        out_shape=jax.ShapeDtypeStruct((4, r, c), dtype),
        compiler_params=_cparams(("arbitrary",)),
    )(slot, shard)


def _place_cols(shard, slot, name):
    r, c = shard.shape
    tr = _pick(r, (256, 128))

    def body(_, x_ref, o_ref):
        o_ref[...] = x_ref[...].astype(BF16)

    return pl.pallas_call(
        body, name=name,
        grid_spec=pltpu.PrefetchScalarGridSpec(
            num_scalar_prefetch=1, grid=(r // tr,),
            in_specs=[pl.BlockSpec((tr, c), lambda i, s: (i, 0))],
            out_specs=pl.BlockSpec((tr, c), lambda i, s: (i, 2 * (s[0] % 2) + s[0] // 2))),
        out_shape=jax.ShapeDtypeStruct((r, 4 * c), BF16),
        compiler_params=_cparams(("arbitrary",)),
    )(slot, shard)


def _place_w_in(shard, slot):
    r, c = shard.shape
    nb, npad = c // 128, COL_PAD // 128
    rot_blocks, in_blocks = ROT // 128, D_IN // 128

    def out_block(j, s):
        nat = s[0] * nb + jnp.minimum(j, nb - 1)
        rot = jnp.where(nat >= rot_blocks, nat - rot_blocks, nat + (D_INP - ROT) // 128)
        return jnp.where(j < nb, rot, (in_blocks - rot_blocks) + (j - nb))

    def body(_, x_ref, o_ref):
        o_ref[...] = jnp.where(pl.program_id(0) < nb, x_ref[...], 0.0).astype(BF16)

    return pl.pallas_call(
        body, name="place_w_in",
        grid_spec=pltpu.PrefetchScalarGridSpec(
            num_scalar_prefetch=1, grid=(nb + npad,),
            in_specs=[pl.BlockSpec((r, 128), lambda j, s: (0, jnp.minimum(j, nb - 1)))],
            out_specs=pl.BlockSpec((r, 128), lambda j, s: (0, out_block(j, s)))),
        out_shape=jax.ShapeDtypeStruct((r, D_INP), BF16),
        compiler_params=_cparams(("arbitrary",)),
    )(slot, shard)


BIG = ("w_in", "w_proj_a", "w_proj_b", "w_out", "w_ffn_in", "w_ffn_down")
COL_SHARDED = ("w_in", "w_ffn_in")


def _coords():
    return lax.axis_index("x"), lax.axis_index("y"), lax.axis_index("c")


W_IN_HEAD = 384


def _shard_views(b, layout, chip, half_of):
    if layout == "slots":
        half = b.shape[1] // 2
        return [b.at[chip, pl.ds(pl.multiple_of(half_of * half, 8), half)]]
    half = b.shape[0] // 2
    rows = pl.ds(pl.multiple_of(half_of * half, 8), half)
    if layout == "cols":
        c = b.shape[1] // 4
        return [b.at[rows, pl.ds(pl.multiple_of((2 * (chip % 2) + chip // 2) * c, 128), c)]]
    c = D_IN // 4
    views = []
    for first, width in ((0, W_IN_HEAD), (W_IN_HEAD, c - W_IN_HEAD)):
        nat = chip * c + first
        rot = jnp.where(nat >= ROT, nat - ROT, nat + D_INP - ROT)
        views.append(b.at[rows, pl.ds(pl.multiple_of(rot, 128), width)])
    return views


def _gather_host(bufs, layouts):
    n = len(bufs)

    def place():
        x, y, c = _coords()
        return x, y, c, [(1 - x, y), (x, 1 - y), (1 - x, 1 - y)]

    def copies(b, sems, w, k, chip, half_of, to):
        return [pltpu.make_async_remote_copy(src_ref=v, dst_ref=v, send_sem=sems[0].at[w, k, p], recv_sem=sems[1].at[w, k, p],
                                             device_id=to, device_id_type=MESH)
                for p, v in enumerate(_shard_views(b, layouts[w], chip, half_of))]

    def start(_, outs, sems):
        x, y, c, chips = place()
        for w, b in enumerate(outs):
            for k, (cx, cy) in enumerate(chips):
                for cp in copies(b, sems, w, k, 2 * x + y, c, (cx, cy, c)):
                    cp.start()

    def mid(_, outs, sems):
        x, y, c, chips = place()
        for w, b in enumerate(outs):
            for k, (cx, cy) in enumerate(chips):
                for cp in copies(b, sems, w, k, 2 * cx + cy, c, (cx, cy, c)):
                    cp.wait_recv()
                for cp in copies(b, sems, w, 3 + k, 2 * cx + cy, c, (x, y, 1 - c)):
                    cp.start()

    def finish(_, outs, sems):
        x, y, c, chips = place()
        for w, b in enumerate(outs):
            for k, (cx, cy) in enumerate(chips):
                for cp in copies(b, sems, w, 3 + k, 2 * cx + cy, 1 - c, (x, y, 1 - c)):
                    cp.wait_recv()
        for w, b in enumerate(outs):
            for k, (cx, cy) in enumerate(chips):
                for cp in copies(b, sems, w, k, 2 * x + y, c, (cx, cy, c)) + copies(b, sems, w, 3 + k, 2 * cx + cy, c, (x, y, 1 - c)):
                    cp.wait_send()

    return _Host(ins=list(bufs), out_shapes=[jax.ShapeDtypeStruct(b.shape, b.dtype) for b in bufs],
                 aliases={i: i for i in range(n)},
                 sems=[pltpu.SemaphoreType.DMA((n, 6, 2)), pltpu.SemaphoreType.DMA((n, 6, 2))],
                 start=start, mid=mid, finish=finish)


def _run_host(host, name):
    n_in, n_out = len(host.ins), len(host.out_shapes)

    def body(*refs):
        ins, outs, sems = refs[:n_in], refs[n_in:n_in + n_out], refs[n_in + n_out:]
        host.start(ins, outs, sems)
        if host.mid is not None:
            host.mid(ins, outs, sems)
        host.finish(ins, outs, sems)

    return pl.pallas_call(
        body, name=name, in_specs=[ANY] * n_in, out_specs=[ANY] * n_out, out_shape=list(host.out_shapes),
        scratch_shapes=list(host.sems), input_output_aliases=dict(host.aliases),
    )(*host.ins)


def _peer(x, y, c, k):
    fx, fy, fc = (k >> 2) & 1, (k >> 1) & 1, k & 1
    return (x ^ fx if fx else x, y ^ fy if fy else y, c ^ fc if fc else c)


def _sub_shape(name, full_shape):
    r, c = full_shape
    return (r // 2, c // 4) if name in COL_SHARDED else (r // 8, c)


def _exchange_host(names, grads):
    n = len(names)
    shapes = [_sub_shape(k, g.shape) for k, g in zip(names, grads)]

    def copy(ins, outs, sems, w, k):
        x, y, c = _coords()
        px, py, pc = _peer(x, y, c, k)
        sr, sc = shapes[w]
        if names[w] in COL_SHARDED:
            col = _col_block(names[w], px, py) * sc
            src = ins[w].at[pl.ds(pl.multiple_of(pc * sr, 16), sr), pl.ds(pl.multiple_of(col, 128), sc)]
        else:
            src = ins[w].at[pl.ds(pl.multiple_of((4 * px + 2 * py + pc) * sr, 16), sr)]
        return pltpu.make_async_remote_copy(
            src_ref=src, dst_ref=outs[w].at[k - 1], send_sem=sems[0].at[w, k - 1], recv_sem=sems[1].at[w, k - 1],
            device_id=(px, py, pc), device_id_type=MESH)

    def start(ins, outs, sems):
        for w in range(n):
            for k in range(1, 8):
                copy(ins, outs, sems, w, k).start()

    def finish(ins, outs, sems):
        for w in range(n):
            for k in range(1, 8):
                copy(ins, outs, sems, w, k).wait_recv()
        for w in range(n):
            for k in range(1, 8):
                copy(ins, outs, sems, w, k).wait_send()

    return _Host(ins=list(grads), out_shapes=[jax.ShapeDtypeStruct((7,) + s, g.dtype) for s, g in zip(shapes, grads)],
                 aliases={}, sems=[pltpu.SemaphoreType.DMA((n, 7)), pltpu.SemaphoreType.DMA((n, 7))],
                 start=start, mid=None, finish=finish)


def _sibling_host(bufs):
    n = len(bufs)

    def copy(outs, sems, w, half_of):
        x, y, c = _coords()
        half = outs[w].shape[0] // 2
        rows = outs[w].at[pl.ds(pl.multiple_of(half_of * half, 8), half)]
        return pltpu.make_async_remote_copy(src_ref=rows, dst_ref=rows, send_sem=sems[0].at[w], recv_sem=sems[1].at[w],
                                            device_id=(x, y, 1 - c), device_id_type=MESH)

    def start(_, outs, sems):
        c = lax.axis_index("c")
        for w in range(n):
            copy(outs, sems, w, c).start()

    def finish(_, outs, sems):
        c = lax.axis_index("c")
        for w in range(n):
            copy(outs, sems, w, 1 - c).wait_recv()
        for w in range(n):
            copy(outs, sems, w, c).wait_send()

    return _Host(ins=list(bufs), out_shapes=[jax.ShapeDtypeStruct(b.shape, b.dtype) for b in bufs],
                 aliases={i: i for i in range(n)},
                 sems=[pltpu.SemaphoreType.DMA((n,)), pltpu.SemaphoreType.DMA((n,))], start=start, mid=None, finish=finish)


def _pack_host(pack):
    def me_of():
        x, y, c = _coords()
        return x, y, c, 4 * x + 2 * y + c

    def copy(ins, outs, sems, k, slot):
        x, y, c, _ = me_of()
        return pltpu.make_async_remote_copy(src_ref=ins[0], dst_ref=outs[0].at[slot], send_sem=sems[0].at[k - 1],
                                            recv_sem=sems[1].at[k - 1], device_id=_peer(x, y, c, k), device_id_type=MESH)

    def start(ins, outs, sems):
        me = me_of()[3]
        pltpu.make_async_copy(ins[0], outs[0].at[me], sems[2]).start()
        for k in range(1, 8):
            copy(ins, outs, sems, k, me).start()

    def finish(ins, outs, sems):
        x, y, c, me = me_of()
        for k in range(1, 8):
            px, py, pc = _peer(x, y, c, k)
            copy(ins, outs, sems, k, 4 * px + 2 * py + pc).wait_recv()
        for k in range(1, 8):
            copy(ins, outs, sems, k, me).wait_send()
        pltpu.make_async_copy(ins[0], outs[0].at[me], sems[2]).wait()

    return _Host(ins=[pack], out_shapes=[jax.ShapeDtypeStruct((8,) + pack.shape, F32)], aliases={},
                 sems=[pltpu.SemaphoreType.DMA((7,)), pltpu.SemaphoreType.DMA((7,)), pltpu.SemaphoreType.DMA],
                 start=start, mid=None, finish=finish)


def _join_hosts(hosts):
    ins, outs, sems, aliases, spans = [], [], [], {}, []
    for h in hosts:
        spans.append((len(ins), len(h.ins), len(outs), len(h.out_shapes), len(sems), len(h.sems)))
        for i_in, i_out in h.aliases.items():
            aliases[len(ins) + i_in] = len(outs) + i_out
        ins, outs, sems = ins + list(h.ins), outs + list(h.out_shapes), sems + list(h.sems)

    def phase(which):
        fns = [getattr(h, which) for h in hosts]
        if all(f is None for f in fns):
            return None

        def run(i_refs, o_refs, s_refs):
            for f, (i0, n_i, o0, n_o, s0, n_s) in zip(fns, spans):
                if f is not None:
                    f(i_refs[i0:i0 + n_i], o_refs[o0:o0 + n_o], s_refs[s0:s0 + n_s])

        return run

    return _Host(ins, outs, aliases, sems, phase("start"), phase("mid"), phase("finish"))


SHARDED_SMALL = ("meta_tokens", "conv_w")

ROWS = {"conv_b": (1, RGW), "rg_ba": (1, RGW), "rg_bx": (1, RGW), "rg_lambda": (1, RGW), "conv_w": (4, RGW),
        "norm2_g": (1, D), "norm_f_g": (1, D), "hg_lb_logits": (2, D),
        "rg_wa": (RGW * RG_BLOCK // 128, 128), "rg_wx": (RGW * RG_BLOCK // 128, 128), "hg_norm_g": (1, 128),
        "norm1_g": (1, D), "meta_tokens": (N_META, D), "loss": (1, D)}
PACKS = {"early_wide": ("conv_b", "rg_ba", "rg_bx", "rg_lambda", "conv_w"),
         "early_model": ("norm2_g", "norm_f_g", "hg_lb_logits"),
         "early_lane": ("rg_wa", "rg_wx", "hg_norm_g"),
         "late": ("norm1_g", "meta_tokens", "loss")}
EARLY = ("early_wide", "early_model", "early_lane")
NO_UPDATE = ("conv_w", "meta_tokens", "loss")


def _pack_rows(vals, names):
    parts, first, row = [], {}, 0
    for k in names:
        first[k] = row
        parts.append(vals[k].reshape(ROWS[k]).astype(F32))
        row += ROWS[k][0]
    if row % 8:
        parts.append(jnp.zeros((-row % 8, ROWS[names[0]][1]), F32))
    return jnp.concatenate(parts, axis=0), first


ORDER = ("meta_tokens", "norm1_g", "w_in", "conv_w", "conv_b", "rg_wa", "rg_ba", "rg_wx", "rg_bx", "rg_lambda",
         "hg_lb_logits", "hg_norm_g", "w_proj_a", "w_proj_b", "w_out", "norm2_g", "w_ffn_in", "w_ffn_down", "norm_f_g")


def kernel(x, meta_tokens, norm1_g, w_in, conv_w, conv_b, rg_wa, rg_ba, rg_wx, rg_bx, rg_lambda, hg_lb_logits, hg_norm_g, w_proj_a, w_proj_b, w_out, norm2_g, w_ffn_in, w_ffn_down, norm_f_g, loss_target, m_meta_tokens, m_norm1_g, m_w_in, m_conv_w, m_conv_b, m_rg_wa, m_rg_ba, m_rg_wx, m_rg_bx, m_rg_lambda, m_hg_lb_logits, m_hg_norm_g, m_w_proj_a, m_w_proj_b, m_w_out, m_norm2_g, m_w_ffn_in, m_w_ffn_down, m_norm_f_g, v_meta_tokens, v_norm1_g, v_w_in, v_conv_w, v_conv_b, v_rg_wa, v_rg_ba, v_rg_wx, v_rg_bx, v_rg_lambda, v_hg_lb_logits, v_hg_norm_g, v_w_proj_a, v_w_proj_b, v_w_out, v_norm2_g, v_w_ffn_in, v_w_ffn_down, v_norm_f_g):
    args = dict(locals())
    w = {k: args[k] for k in ORDER}
    m = {k: args["m_" + k] for k in ORDER}
    v = {k: args["v_" + k] for k in ORDER}
    xi, yi, ci = _coords()
    chip = 2 * xi + yi
    slot = jnp.reshape(chip, (1,)).astype(jnp.int32)
    xyc = jnp.stack([xi, yi, ci]).astype(jnp.int32)

    tiny = jnp.zeros((32, 384), F32)
    tiny = tiny.at[0:N_META, 0:256].set(meta_tokens).at[N_META:N_META + 4, 0:320].set(conv_w[0])
    later = [k for k in BIG if k != "w_in"]
    layouts = {k: "cols" if k in COL_SHARDED else "slots" for k in later}
    slots = {k: _place_cols(w[k][0], slot, "place_" + k) if layouts[k] == "cols"
             else _place_own(w[k][0], slot, BF16, "place_" + k) for k in later}
    w_in_full, tiny_all = _run_host(
        _gather_host([_place_w_in(w["w_in"][0], slot), _place_own(tiny, slot, F32, "place_tiny")], ["w_in", "slots"]),
        "allgather_w_in")

    def full_matrix(k, gth):
        return gth if layouts[k] == "cols" else gth.reshape(-1, gth.shape[2])

    meta_full = jnp.transpose(tiny_all[:, 0:N_META, 0:256], (1, 0, 2)).reshape(N_META, D)
    conv_w_full = jnp.transpose(tiny_all[:, N_META:N_META + 4, 0:320], (1, 0, 2)).reshape(4, RGW)

    small = dict(norm1_g=norm1_g, conv_w=conv_w_full, conv_b=conv_b, rg_wa=rg_wa[0], rg_ba=rg_ba, rg_wx=rg_wx[0],
                 rg_bx=rg_bx, rg_lambda=rg_lambda, hg_lb_logits=hg_lb_logits, hg_norm_g=hg_norm_g,
                 norm2_g=norm2_g, norm_f_g=norm_f_g.reshape(1, D))

    hosts = {
        "mm_proj": lambda g: _gather_host([slots[k] for k in later], [layouts[k] for k in later]),
        "mm_d_act": lambda g: _exchange_host(["w_ffn_down"], [g["w_ffn_down"]]),
        "mm_d_hn2": lambda g: _exchange_host(["w_ffn_in"], [g["w_ffn_in"]]),
        "mm_d_win": lambda g: _join_hosts([
            _exchange_host(["w_out", "w_proj_a", "w_proj_b"], [g["w_out"], g["w_proj_a"], g["w_proj_b"]]),
            *[_pack_host(_pack_rows(g, PACKS[p])[0]) for p in EARLY]]),
        "mm_d_hn1": lambda g: _exchange_host(["w_in"], [_unrotate_g_in(g["w_in"])]),
    }
    h0 = jnp.concatenate([jnp.zeros((PAD, D), F32), meta_full, x[0]], axis=0)
    loss_blk, grad_x, g, carried = _local_step(
        h0, loss_target[0], {"w_in": w_in_full}, small, hosts=hosts,
        finalize=lambda gathered: {k: full_matrix(k, gth) for k, gth in zip(later, gathered)})
    g["loss"] = jnp.broadcast_to(loss_blk[0:1, 0:1], (1, D))

    halves = {}
    sources = {"mm_d_act": ["w_ffn_down"], "mm_d_hn2": ["w_ffn_in"], "mm_d_win": ["w_out", "w_proj_a", "w_proj_b"],
               "mm_d_hn1": ["w_in"]}
    for name, keys in sources.items():
        partials, received = carried[name]
        for k, part, rec in zip(keys, partials, received):
            halves[k] = _sum_place(rec, part, k, xyc, "sum_" + k)
    late, _ = _pack_rows(g, PACKS["late"])
    tail = _run_host(_join_hosts([_sibling_host([halves[k] for k in BIG]), _pack_host(late)]), "swap_and_late_pack")
    g_big = dict(zip(BIG, tail[:len(BIG)]))
    recv_packs = dict(zip(EARLY, carried["mm_d_win"][1][3:]), late=tail[len(BIG)])

    grad, delta, new_m, new_v, summed = {}, {}, {}, {}, {}
    for pack, names in PACKS.items():
        row, params, extras = 0, [], []
        for k in names:
            if k in NO_UPDATE:
                extras.append((row, ROWS[k][0]))
            else:
                params.append((row,) + tuple(d[k].reshape(ROWS[k]) for d in (w, m, v)))
            row += ROWS[k][0]
        updated, rows_only = _adamw_rows(recv_packs[pack], params, extras, "adamw_" + pack)
        for k, res in zip([k for k in names if k not in NO_UPDATE], updated):
            grad[k], delta[k], new_m[k], new_v[k] = (a.reshape(w[k].shape) for a in res)
        summed.update(zip([k for k in names if k in NO_UPDATE], rows_only))
    loss = summed["loss"][0, 0]
    g_shard = {"meta_tokens": lax.dynamic_slice(summed["meta_tokens"], (0, chip * 256), (N_META, 256)),
               "conv_w": lax.dynamic_slice(summed["conv_w"], (0, chip * 320), (4, 320))}
    for k in BIG + SHARDED_SMALL:
        gk = g_big[k] if k in BIG else g_shard[k]
        wk, mk, vk = (d[k].reshape(gk.shape) for d in (w, m, v))
        dk, mk, vk = _adamw(wk, gk, mk, vk, "adamw_" + k)
        grad[k], delta[k], new_m[k], new_v[k] = (a.reshape(w[k].shape) for a in (gk, dk, mk, vk))

    return (loss, grad_x[None], *[grad[k] for k in ORDER], *[delta[k] for k in ORDER],
            *[new_m[k] for k in ORDER], *[new_v[k] for k in ORDER])
```

```python
import jax
import jax.numpy as jnp
from jax import lax
from jax.experimental import pallas as pl
from jax.experimental.pallas import tpu as pltpu

F32, BF16 = jnp.float32, jnp.bfloat16
D = 1024
N_META = 16
RGW = 1280
RG_BLOCKS, RG_BLOCK = 16, 80
RG_C = 8.0
HEADS, HD = 8, 128
HGW = HEADS * HD
DFF = 2816
D_IN = 2 * RGW + 6 * D
ROT = 2 * RGW
COL_PAD = 256
D_INP = D_IN + COL_PAD
EPS = 1e-6
CH = 64
TM = 256
ROW0 = TM
PAD = ROW0 - N_META
CT = RGW
RS = 16
NCT = RGW // CT
EXP_CLAMP = 80.0
VMEM_LIMIT = 56 * 1024 * 1024

LR, B1, B2, ADAM_EPS, WD, STEP = 0.001, 0.9, 0.999, 1e-08, 0.01, 10
MESH = pl.DeviceIdType.MESH
ANY = pl.BlockSpec(memory_space=pl.ANY)


def _cparams(sem):
    return pltpu.CompilerParams(dimension_semantics=sem, vmem_limit_bytes=VMEM_LIMIT)


def _pick(n, prefs):
    for p in prefs:
        if n % p == 0:
            return p
    return n


def _sig(x):
    return 0.5 * jnp.tanh(0.5 * x) + 0.5


def _dot(a, b, dims):
    return lax.dot_general(a, b, (dims, ((), ())), preferred_element_type=F32)


NN, NT, TN_ = ((1,), (0,)), ((1,), (1,)), ((0,), (0,))


class _Host:
    def __init__(self, ins, out_shapes, aliases, sems, start, mid, finish):
        self.ins, self.out_shapes, self.aliases, self.sems = ins, out_shapes, aliases, sems
        self.start, self.mid, self.finish = start, mid, finish


class _Epi:
    def __init__(self, ins, in_specs, out_shapes, out_specs, fn, sequential=False, split=1):
        self.ins, self.in_specs, self.out_shapes, self.out_specs, self.fn = ins, in_specs, out_shapes, out_specs, fn
        self.sequential = sequential
        self.split = split


class _Pro:
    def __init__(self, ins, in_specs, fn):
        self.ins, self.in_specs, self.fn = ins, in_specs, fn


def _norm_pro(gain):
    def fn(a_ref, ins):
        xh, _ = _rms(a_ref[...])
        return (xh * ins[0][...]).astype(BF16)

    return _Pro([gain], [pl.BlockSpec((1, gain.shape[1]), lambda i, j, k: (0, 0))], fn)


def _mm(a, b, mode, out_dtype, name, resid=None, host=None, epi=None, tiles=None, sparse=None, pro=None):
    if mode == "nn":
        (m, kd), n = a.shape, b.shape[1]
    elif mode == "nt":
        (m, kd), n = a.shape, b.shape[0]
    else:
        (kd, m), n = a.shape, b.shape[1]
    if mode == "tn":
        tm = _pick(m, (1024, 1280, 1408, 640, 512, 256, 128))
        tk = _pick(kd, (1408, 768, 512, 256, 128))
    else:
        tm = _pick(m, (768, 512, 640, 256, 128))
        tk = kd if kd <= 2816 else _pick(kd, (1792, 1408, 1024, 512, 256, 128))
    tn = _pick(n, (1792, 1408, 1280, 1024, 512, 256, 128))
    if tiles is not None:
        tm, tn = tiles
    a_map = (lambda i, j, k: (k, i)) if mode == "tn" else (lambda i, j, k: (i, k))
    b_map = (lambda i, j, k: (j, k)) if mode == "nt" else (lambda i, j, k: (k, j))
    o_map = lambda i, j, k: (i, j)
    if sparse is not None:
        (ni, nj, nk), (tm, tn, tk), (a_map, b_map, o_map) = sparse
    else:
        assert m % tm == 0 and n % tn == 0 and kd % tk == 0, (name, m, n, kd, tm, tn, tk)
        ni, nj, nk = m // tm, n // tn, kd // tk
    dims = {"nn": NN, "nt": NT, "tn": TN_}[mode]

    n_hin = len(host.ins) if host else 0
    n_hout = len(host.out_shapes) if host else 0
    n_res = 0 if resid is None else 1
    n_pro = 0 if pro is None else 1
    n_pin = len(pro.ins) if pro else 0
    n_ein = len(epi.ins) if epi else 0
    n_out = len(epi.out_shapes) if epi else 1
    assert not n_pro or (mode == "nn" and nk == 1)

    def finish(r, r_ref, e_in, o_refs, i, rows=slice(None), first=True):
        if resid is not None:
            r = r + r_ref[rows, :].astype(F32)
        if epi:
            epi.fn(r, e_in, o_refs, i, rows, first)
        else:
            o_refs[0][rows, :] = r.astype(out_dtype)

    def body(*refs):
        a_ref, b_ref = refs[:2]
        r_ref = refs[2] if resid is not None else None
        pos = 2 + n_res
        p_in = refs[pos:pos + n_pin]
        pos += n_pin
        e_in = refs[pos:pos + n_ein]
        pos += n_ein
        h_in = refs[pos:pos + n_hin]
        pos += n_hin
        o_ref = refs[pos:pos + n_out]
        pos += n_out
        hn_ref = refs[pos] if n_pro else None
        pos += n_pro
        h_out = refs[pos:pos + n_hout]
        scratch = refs[pos + n_hout:]
        n_acc = 1 if nk > 1 else 0
        h_sems = scratch[n_acc + n_pro:]
        i, j, k = pl.program_id(0), pl.program_id(1), pl.program_id(2)
        if host:
            @pl.when((i == 0) & (j == 0) & (k == 0))
            def _():
                host.start(h_in, h_out, h_sems)

            if host.mid is not None:
                @pl.when((i == (3 * ni) // 4) & (j == 0) & (k == 0))
                def _():
                    host.mid(h_in, h_out, h_sems)

        if n_pro:
            a_s = scratch[n_acc]

            @pl.when(j == 0)
            def _():
                a_s[...] = pro.fn(a_ref, p_in)
                hn_ref[...] = a_s[...]

            a_val = a_s[...]
        else:
            a_val = a_ref[...]

        if nk == 1 and epi and epi.split > 1 and mode != "tn":
            part = tm // epi.split
            for p in range(epi.split):
                rows = slice(p * part, (p + 1) * part)
                finish(_dot(a_val[rows, :], b_ref[...], dims), r_ref, e_in, o_ref, i, rows, p == 0)
        elif nk == 1:
            finish(_dot(a_val, b_ref[...], dims), r_ref, e_in, o_ref, i)
        else:
            acc = scratch[0]

            @pl.when(k == 0)
            def _():
                acc[...] = jnp.zeros_like(acc)

            acc[...] += _dot(a_val, b_ref[...], dims)

            @pl.when(k == nk - 1)
            def _():
                finish(acc[...], r_ref, e_in, o_ref, i)

        if host:
            @pl.when((i == ni - 1) & (j == nj - 1) & (k == nk - 1))
            def _():
                host.finish(h_in, h_out, h_sems)

    a_spec = pl.BlockSpec((tk, tm) if mode == "tn" else (tm, tk), a_map)
    b_spec = pl.BlockSpec((tn, tk) if mode == "nt" else (tk, tn), b_map)
    o_spec = pl.BlockSpec((tm, tn), o_map)
    in_specs, args = [a_spec, b_spec], [a, b]
    if resid is not None:
        in_specs.append(o_spec)
        args.append(resid)
    if n_pro:
        in_specs += list(pro.in_specs)
        args += list(pro.ins)
    out_shapes, out_specs = [jax.ShapeDtypeStruct((m, n), out_dtype)], [o_spec]
    if epi:
        in_specs += list(epi.in_specs)
        args += list(epi.ins)
        out_shapes, out_specs = list(epi.out_shapes), list(epi.out_specs)
    if n_pro:
        out_shapes.append(jax.ShapeDtypeStruct((m, kd), BF16))
        out_specs.append(pl.BlockSpec((tm, kd), lambda i, j, k: (i, 0)))
    scratch = ([pltpu.VMEM((tm, tn), F32)] if nk > 1 else []) + ([pltpu.VMEM((tm, kd), BF16)] if n_pro else [])
    sequential = host or n_pro or (epi and epi.sequential)
    outs = pl.pallas_call(
        body, name=name, grid=(ni, nj, nk),
        in_specs=in_specs + [ANY] * n_hin, out_specs=out_specs + [ANY] * n_hout,
        out_shape=out_shapes + (list(host.out_shapes) if host else []),
        scratch_shapes=scratch + (list(host.sems) if host else []),
        input_output_aliases=({2 + n_res + n_pin + n_ein + i_in: n_out + n_pro + i_out
                               for i_in, i_out in host.aliases.items()} if host else {}),
        compiler_params=_cparams(("arbitrary",) * 3 if sequential else ("parallel", "parallel", "arbitrary")),
    )(*args, *(host.ins if host else []))
    main = list(outs[:n_out]) if epi else outs[0]
    if n_pro:
        main = (main, outs[n_out])
    return (main, list(outs[n_out + n_pro:])) if host else main


def _rms(x):
    r = lax.rsqrt(jnp.mean(x * x, axis=-1, keepdims=True) + EPS)
    return x * r, r


def _rms_bwd_math(dhn, x, g):
    xh, r = _rms(x)
    dxh = dhn * g
    dx = r * (dxh - xh * jnp.mean(dxh * xh, axis=-1, keepdims=True))
    return dx, jnp.sum(dhn * xh, axis=0, keepdims=True)


def _row_specs(tm):
    return pl.BlockSpec((tm, D), lambda i, j, k: (i, 0)), pl.BlockSpec((1, D), lambda i, j, k: (0, 0))


def _final_loss_epi(gf, target, t, tm):
    parts = tm // TM

    def fn(x, ins, outs, i, rows, first):
        g_ref = ins[0]
        p = rows.start // TM
        t_ref = ins[1 + p]
        dh_ref, dhb_ref, loss_ref, dg_ref = outs

        if first:
            @pl.when(i == 0)
            def _():
                loss_ref[...] = jnp.zeros_like(loss_ref)
                dg_ref[...] = jnp.zeros_like(dg_ref)

        g = g_ref[...]
        xh, _ = _rms(x)
        err = jnp.where(i * parts + p > 0, xh * g - t_ref[...], 0.0)
        loss_ref[...] += 0.5 * jnp.sum(jnp.sum(err * err, axis=-1, keepdims=True) * (1.0 / D))
        dx, dg = _rms_bwd_math(err * (1.0 / D), x, g)
        dh_ref[rows, :] = dx
        dhb_ref[rows, :] = dx.astype(BF16)
        dg_ref[...] += dg

    row, vec = _row_specs(tm)
    t_specs = [pl.BlockSpec((TM, D), lambda i, j, k, p=p: (jnp.maximum(i * parts + p - 1, 0), 0)) for p in range(parts)]
    return _Epi(ins=[gf] + [target] * parts, in_specs=[vec] + t_specs,
                out_shapes=[jax.ShapeDtypeStruct((t, D), F32), jax.ShapeDtypeStruct((t, D), BF16),
                            jax.ShapeDtypeStruct((8, 128), F32), jax.ShapeDtypeStruct((1, D), F32)],
                out_specs=[row, row, pl.BlockSpec((8, 128), lambda i, j, k: (0, 0)), vec], fn=fn, sequential=True,
                split=parts)


def _rms_bwd_epi(h, g, dres, tm, bf16_copy):
    t = h.shape[0]

    def fn(dhn, ins, outs, i, rows, first):
        h_ref, g_ref, dres_ref = ins
        dg_ref = outs[-1]

        if first:
            @pl.when(i == 0)
            def _():
                dg_ref[...] = jnp.zeros_like(dg_ref)

        dx, dg = _rms_bwd_math(dhn, h_ref[rows, :], g_ref[...])
        dh = dres_ref[rows, :] + dx
        outs[0][rows, :] = dh
        if bf16_copy:
            outs[1][rows, :] = dh.astype(BF16)
        dg_ref[...] += dg

    row, vec = _row_specs(tm)
    copies = [jax.ShapeDtypeStruct((t, D), F32)] + ([jax.ShapeDtypeStruct((t, D), BF16)] if bf16_copy else [])
    return _Epi(ins=[h, g, dres], in_specs=[row, vec, row],
                out_shapes=copies + [jax.ShapeDtypeStruct((1, D), F32)],
                out_specs=[row] * len(copies) + [vec], fn=fn, sequential=True)


FH = DFF // 2
FM = 384


def _swiglu_epi(t):
    def fn(r, _, outs, i, rows, first):
        g, u = r[:, :FH], r[:, FH:]
        outs[0][rows, :] = r.astype(BF16)
        outs[1][rows, :] = (g * _sig(g) * u).astype(BF16)

    return _Epi(ins=[], in_specs=[],
                out_shapes=[jax.ShapeDtypeStruct((t, 2 * DFF), BF16), jax.ShapeDtypeStruct((t, DFF), BF16)],
                out_specs=[pl.BlockSpec((FM, 2 * FH), lambda i, j, k: (i, j)), pl.BlockSpec((FM, FH), lambda i, j, k: (i, j))],
                fn=fn)


def _swiglu_bwd_epi(gu):
    def fn(d, ins, outs, i, rows, first):
        gu_t = ins[0][rows, :].astype(F32)
        g, u = gu_t[:, :FH], gu_t[:, FH:]
        s = _sig(g)
        outs[0][rows, :FH] = (d * u * (s * (1.0 + g * (1.0 - s)))).astype(BF16)
        outs[0][rows, FH:] = (d * (g * s)).astype(BF16)

    spec = pl.BlockSpec((FM, 2 * FH), lambda i, j, k: (i, j))
    return _Epi(ins=[gu], in_specs=[spec], out_shapes=[jax.ShapeDtypeStruct(gu.shape, BF16)], out_specs=[spec], fn=fn)


COL_GA, COL_GB = 4, 5
COL_AX, COL_AG = (6 * D + COL_PAD) // CT, (6 * D + COL_PAD + RGW) // CT


def _merge_pro(pb, proj, tm):
    def fn(a_ref, ins):
        pa, pb_, ga, gb = (r[...].astype(F32) for r in (a_ref,) + tuple(ins))
        return (_sig(ga) * pa + _sig(gb) * pb_).astype(BF16)

    row, _ = _row_specs(tm)
    return _Pro([pb, proj, proj], [row, pl.BlockSpec((tm, D), lambda i, j, k: (i, COL_GA)),
                                   pl.BlockSpec((tm, D), lambda i, j, k: (i, COL_GB))], fn)


def _merge_bwd_epi(pa, pb, proj, tm):
    t = pa.shape[0]

    def fn(d, ins, outs, i, rows, first):
        pa, pb, ga, gb = (r[rows, :].astype(F32) for r in ins)
        dpa_ref, dpb_ref, dproj_ref = outs
        sa, sb = _sig(ga), _sig(gb)
        dpa_ref[rows, :] = (sa * d).astype(BF16)
        dpb_ref[rows, :] = (sb * d).astype(BF16)
        dproj_ref[rows, :D] = (d * pa * sa * (1.0 - sa)).astype(BF16)
        dproj_ref[rows, D:] = (d * pb * sb * (1.0 - sb)).astype(BF16)

    row, _ = _row_specs(tm)
    return _Epi(ins=[pa, pb, proj, proj],
                in_specs=[row, row, pl.BlockSpec((tm, D), lambda i, j, k: (i, COL_GA)),
                          pl.BlockSpec((tm, D), lambda i, j, k: (i, COL_GB))],
                out_shapes=[jax.ShapeDtypeStruct((t, D), BF16), jax.ShapeDtypeStruct((t, D), BF16),
                            jax.ShapeDtypeStruct((t, D_INP), BF16)],
                out_specs=[row, row, pl.BlockSpec((tm, 2 * D), lambda i, j, k: (i, 2))], fn=fn)


def _zero_pad_cols(dproj):
    t = dproj.shape[0]
    tz = _pick(t, (768, 256))

    def body(_, o_ref):
        o_ref[...] = jnp.zeros_like(o_ref)

    return pl.pallas_call(
        body, name="dproj_pad", grid=(t // tz,), in_specs=[ANY],
        out_specs=pl.BlockSpec((tz, COL_PAD), lambda i: (i, (6 * D) // COL_PAD)),
        out_shape=jax.ShapeDtypeStruct(dproj.shape, BF16), input_output_aliases={0: 0},
        compiler_params=_cparams(("parallel",)),
    )(dproj)


HALO = 16


def _rows_before(cur, before, s):
    n = cur.shape[0]
    return jnp.concatenate([before, cur], axis=0)[n - s:2 * n - s, :]


def _rows_after(cur, after, s):
    n = cur.shape[0]
    return jnp.concatenate([cur, after], axis=0)[s:n + s, :]


def _prev_halo(col0):
    return pl.BlockSpec((HALO, CT), lambda c, i: (jnp.maximum(i * (TM // HALO) - 1, 0), col0 + c))


def _conv_fwd(proj, w, b):
    t = proj.shape[0]

    def body(x_ref, halo_ref, w_ref, b_ref, ob_ref):
        i = pl.program_id(1)
        wv, bv = w_ref[...], b_ref[...]
        first = jnp.where(i > 0, halo_ref[...].astype(F32), 0.0)

        def strip(k, before):
            rows = pl.ds(pl.multiple_of(k * HALO, HALO), HALO)
            x = x_ref[rows, :].astype(F32)
            y = bv + wv[3:4, :] * x
            for j in range(3):
                y = y + wv[j:j + 1, :] * _rows_before(x, before, 3 - j)
            ob_ref[rows, :] = y.astype(BF16)
            return x

        lax.fori_loop(0, TM // HALO, strip, first)

    blk = pl.BlockSpec((TM, CT), lambda c, i: (i, c))
    return pl.pallas_call(
        body, name="conv_fwd", grid=(NCT, t // TM),
        in_specs=[pl.BlockSpec((TM, CT), lambda c, i: (i, COL_AX + c)), _prev_halo(COL_AX),
                  pl.BlockSpec((4, CT), lambda c, i: (0, c)), pl.BlockSpec((1, CT), lambda c, i: (0, c))],
        out_specs=blk,
        out_shape=jax.ShapeDtypeStruct((t, RGW), BF16),
        compiler_params=_cparams(("parallel", "parallel")),
    )(proj, proj, w, b)


def _conv_bwd(dxc, proj, w, dproj):
    t = proj.shape[0]
    nt = t // TM

    def body(d_ref, dn_ref, x_ref, halo_ref, w_ref, _, dx_ref, dw_ref, db_ref, acc):
        i = pl.program_id(1)

        @pl.when(i == 0)
        def _():
            dw_ref[...] = jnp.zeros_like(dw_ref)
            db_ref[...] = jnp.zeros_like(db_ref)

        acc[...] = jnp.zeros_like(acc)
        wv = w_ref[...]
        first = jnp.where(i > 0, halo_ref[...].astype(F32), 0.0)
        last = jnp.where(i < nt - 1, dn_ref[...].astype(F32), 0.0)
        ns = TM // HALO

        def fold(v):
            return v[0:8, :] + v[8:16, :]

        def strip(k, before):
            off = pl.multiple_of(k * HALO, HALO)
            rows = pl.ds(off, HALO)
            x, d = x_ref[rows, :].astype(F32), d_ref[rows, :].astype(F32)
            nxt = d_ref[pl.ds(pl.multiple_of(jnp.minimum(off + HALO, TM - HALO), HALO), HALO), :].astype(F32)
            after = jnp.where(k == ns - 1, last, nxt)
            dx = wv[3:4, :] * d
            for j in range(3):
                dx = dx + wv[j:j + 1, :] * _rows_after(d, after, 3 - j)
                acc[j] += fold(d * _rows_before(x, before, 3 - j))
            acc[3] += fold(d * x)
            acc[4] += fold(d)
            dx_ref[rows, :] = dx.astype(BF16)
            return x

        lax.fori_loop(0, ns, strip, first)
        for j in range(4):
            dw_ref[j:j + 1, :] += jnp.sum(acc[j], axis=0, keepdims=True)
        db_ref[...] += jnp.sum(acc[4], axis=0, keepdims=True)

    return pl.pallas_call(
        body, name="conv_bwd", grid=(NCT, nt),
        in_specs=[pl.BlockSpec((TM, CT), lambda c, i: (i, c)),
                  pl.BlockSpec((HALO, CT), lambda c, i: (jnp.minimum((i + 1) * (TM // HALO), t // HALO - 1), c)),
                  pl.BlockSpec((TM, CT), lambda c, i: (i, COL_AX + c)), _prev_halo(COL_AX),
                  pl.BlockSpec((4, CT), lambda c, i: (0, c)), ANY],
        out_specs=[pl.BlockSpec((TM, CT), lambda c, i: (i, COL_AX + c)),
                   pl.BlockSpec((4, CT), lambda c, i: (0, c)), pl.BlockSpec((1, CT), lambda c, i: (0, c))],
        out_shape=[jax.ShapeDtypeStruct((t, D_INP), BF16), jax.ShapeDtypeStruct((4, RGW), F32),
                   jax.ShapeDtypeStruct((1, RGW), F32)],
        scratch_shapes=[pltpu.VMEM((5, 8, CT), F32)],
        input_output_aliases={5: 0},
        compiler_params=_cparams(("parallel", "arbitrary")),
    )(dxc, dxc, proj, proj, w, dproj)


def _gelu(x):
    c = 0.7978845608028654
    th = jnp.tanh(c * (x + 0.044715 * x * x * x))
    return 0.5 * x * (1.0 + th), th


def _rg_gates(gr, gi, xc, ba, bx, lam, row0):
    r = _sig(gr + ba)
    ig = _sig(gi + bx)
    sp = jax.nn.softplus(-lam)
    a = jnp.exp(-RG_C * r * sp)
    s2 = jnp.maximum(1.0 - a * a, 1e-30)
    rs = lax.rsqrt(s2)
    s = s2 * rs
    rows = row0 + lax.broadcasted_iota(jnp.int32, gr.shape, 0)
    live = rows >= PAD
    u = jnp.where(live, s * ig * xc, 0.0)
    return r, ig, sp, a, s, rs, u, live


def _rg_fwd(gri, xc, proj, ba, bx, lam):
    t = gri.shape[0]

    def body(gr_ref, gi_ref, xc_ref, ag_ref, ba_ref, bx_ref, lam_ref, h_ref, ya_ref, hc):
        i = pl.program_id(1)

        @pl.when(i == 0)
        def _():
            hc[...] = jnp.zeros_like(hc)

        ba, bx, lam = ba_ref[...], bx_ref[...], lam_ref[...]
        sub = lax.broadcasted_iota(jnp.int32, (8, CT), 0)

        def strip(k, h):
            off = pl.multiple_of(k * RS, RS)
            rows = pl.ds(off, RS)
            _, _, _, a, _, _, u, _ = _rg_gates(gr_ref[rows, :].astype(F32), gi_ref[rows, :].astype(F32),
                                               xc_ref[rows, :].astype(F32), ba, bx, lam,
                                               i * TM + off)
            outs = []
            for half in range(RS // 8):
                out = jnp.zeros((8, CT), F32)
                for r in range(8):
                    h = a[8 * half + r:8 * half + r + 1, :] * h + u[8 * half + r:8 * half + r + 1, :]
                    out = jnp.where(sub == r, h, out)
                outs.append(out)
            hs = jnp.concatenate(outs, axis=0)
            h_ref[rows, :] = hs
            ge, _ = _gelu(ag_ref[rows, :].astype(F32))
            ya_ref[rows, :] = (hs * ge).astype(BF16)
            return h

        hc[...] = lax.fori_loop(0, TM // RS, strip, hc[...])

    blk_ = pl.BlockSpec((TM, CT), lambda c, i: (i, c))
    vec = pl.BlockSpec((1, CT), lambda c, i: (0, c))
    return pl.pallas_call(
        body, name="rg_fwd", grid=(NCT, t // TM),
        in_specs=[blk_, pl.BlockSpec((TM, CT), lambda c, i: (i, NCT + c)), blk_,
                  pl.BlockSpec((TM, CT), lambda c, i: (i, COL_AG + c)), vec, vec, vec],
        out_specs=[blk_, blk_],
        out_shape=[jax.ShapeDtypeStruct((t, RGW), F32), jax.ShapeDtypeStruct((t, RGW), BF16)],
        scratch_shapes=[pltpu.VMEM((1, CT), F32)],
        compiler_params=_cparams(("parallel", "arbitrary")),
    )(gri, gri, xc, proj, ba, bx, lam)


def _rg_bwd(dya, h, gri, xc, proj, ba, bx, lam, dproj):
    t = gri.shape[0]
    nt = t // TM
    assert NCT == 1

    def body(dya_ref, h_ref, hh_ref, gr_ref, gi_ref, xc_ref, ag_ref, ba_ref, bx_ref, lam_ref, _,
             dgri_ref, dxc_ref, dag_ref, dba_ref, dbx_ref, dlam_ref, hbuf, acc, cc):
        i = pl.program_id(1)
        ri = nt - 1 - i

        @pl.when(i == 0)
        def _():
            cc[...] = jnp.zeros_like(cc)
            dba_ref[...] = jnp.zeros_like(dba_ref)
            dbx_ref[...] = jnp.zeros_like(dbx_ref)
            dlam_ref[...] = jnp.zeros_like(dlam_ref)

        acc[...] = jnp.zeros_like(acc)
        ba, bx, lam = ba_ref[...], bx_ref[...], lam_ref[...]
        halo_last = jnp.where(ri > 0, hh_ref[7:8, :], 0.0)
        sub = lax.broadcasted_iota(jnp.int32, (8, CT), 0)
        c0 = 0.7978845608028654

        def strip(kk, c):
            k = TM // RS - 1 - kk
            off = pl.multiple_of(k * RS, RS)
            rows = pl.ds(off, RS)
            xc, hv = xc_ref[rows, :].astype(F32), h_ref[rows, :]
            ag, dya = ag_ref[rows, :].astype(F32), dya_ref[rows, :].astype(F32)
            r, ig, sp, a, s, rs, _, live = _rg_gates(gr_ref[rows, :].astype(F32), gi_ref[rows, :].astype(F32), xc,
                                                     ba, bx, lam, ri * TM + off)
            ge, th = _gelu(ag)
            dge = 0.5 * (1.0 + th) + 0.5 * ag * (1.0 - th * th) * c0 * (1.0 + 3.0 * 0.044715 * ag * ag)
            dag_ref[rows, :] = (dya * hv * dge).astype(BF16)
            d = dya * ge
            outs = []
            for half in range(RS // 8 - 1, -1, -1):
                out = jnp.zeros((8, CT), F32)
                for rr in range(7, -1, -1):
                    g = d[8 * half + rr:8 * half + rr + 1, :] + c
                    c = a[8 * half + rr:8 * half + rr + 1, :] * g
                    out = jnp.where(sub == rr, g, out)
                outs.insert(0, out)
            g = jnp.concatenate(outs, axis=0)
            before = h_ref[pl.ds(pl.multiple_of(jnp.maximum(off - RS, 0), RS), RS), :]
            hbuf[7:8, :] = jnp.where(k == 0, halo_last, before[RS - 1:RS, :])
            hbuf[8:, :] = hv
            hprev = hbuf[pl.ds(7, RS), :]
            du = jnp.where(live, g, 0.0)
            ds = du * (ig * xc)
            dm = du * s
            dla = (g * hprev - a * ds * rs) * a
            dr = dla * (-RG_C * sp) * r * (1.0 - r)
            di = dm * xc * ig * (1.0 - ig)
            dgri_ref[rows, :CT] = dr.astype(BF16)
            dgri_ref[rows, CT:] = di.astype(BF16)
            dxc_ref[rows, :] = (dm * ig).astype(BF16)
            dl = dla * (-RG_C * r)
            for half in range(RS // 8):
                part = slice(8 * half, 8 * half + 8)
                acc[0] += dr[part]
                acc[1] += di[part]
                acc[2] += dl[part]
            return c

        cc[...] = lax.fori_loop(0, TM // RS, strip, cc[...])
        dba_ref[...] += jnp.sum(acc[0], axis=0, keepdims=True)
        dbx_ref[...] += jnp.sum(acc[1], axis=0, keepdims=True)
        dlam_ref[...] += jnp.sum(acc[2], axis=0, keepdims=True) * (-_sig(-lam))

    rblk = pl.BlockSpec((TM, CT), lambda c, i: (nt - 1 - i, c))
    vec = pl.BlockSpec((1, CT), lambda c, i: (0, c))
    hh = pl.BlockSpec((8, CT), lambda c, i: (jnp.maximum((nt - 1 - i) * (TM // 8) - 1, 0), c))
    agb = pl.BlockSpec((TM, CT), lambda c, i: (nt - 1 - i, COL_AG + c))
    return pl.pallas_call(
        body, name="rg_bwd", grid=(NCT, nt),
        in_specs=[rblk, rblk, hh, rblk, pl.BlockSpec((TM, CT), lambda c, i: (nt - 1 - i, NCT + c)), rblk, agb,
                  vec, vec, vec, ANY],
        out_specs=[pl.BlockSpec((TM, 2 * CT), lambda c, i: (nt - 1 - i, c)), rblk, agb, vec, vec, vec],
        out_shape=[jax.ShapeDtypeStruct((t, 2 * RGW), BF16),
                   jax.ShapeDtypeStruct((t, RGW), BF16), jax.ShapeDtypeStruct((t, D_INP), BF16),
                   jax.ShapeDtypeStruct((1, RGW), F32), jax.ShapeDtypeStruct((1, RGW), F32),
                   jax.ShapeDtypeStruct((1, RGW), F32)],
        scratch_shapes=[pltpu.VMEM((RS + 8, CT), F32), pltpu.VMEM((3, 8, CT), F32), pltpu.VMEM((1, CT), F32)],
        input_output_aliases={10: 2},
        compiler_params=_cparams(("parallel", "arbitrary")),
    )(dya, h, h, gri, gri, xc, proj, ba, bx, lam, dproj)


NCH = TM // CH
HG = HEADS

def _tri_dot(tri, x):
    hi = x.astype(BF16)
    r1 = x - hi.astype(F32)
    mid = r1.astype(BF16)
    lo = (r1 - mid.astype(F32)).astype(BF16)
    return _dot(tri, hi, NN) + _dot(tri, mid, NN) + _dot(tri, lo, NN)


def _hg_decay(f_ref, lb, sf_s, b_s):
    sf = _sig(f_ref[...].astype(F32))
    ri = lax.broadcasted_iota(jnp.int32, (TM, TM), 0)
    ci = lax.broadcasted_iota(jnp.int32, (TM, TM), 1)
    low = ((ri >= ci) & (ri // CH == ci // CH)).astype(BF16)
    sf_s[...] = sf
    b_s[...] = _tri_dot(low, jnp.log(lb + (1.0 - lb) * sf))


def _hg_chunk(qr, sf, b, lb):
    fg = lb + (1.0 - lb) * sf
    k = (1.0 - lb) * (1.0 - sf)
    sq = _sig(qr)
    q = qr * sq
    ri = lax.broadcasted_iota(jnp.int32, (CH, CH), 0)
    ci = lax.broadcasted_iota(jnp.int32, (CH, CH), 1)
    bm, bl = b[CH // 2 - 1:CH // 2, :], b[CH - 1:CH, :]
    ebm = jnp.exp(jnp.minimum(b - bm, EXP_CLAMP))
    ekm = jnp.exp(jnp.minimum(bm - b, EXP_CLAMP))
    eb = ebm * jnp.exp(bm)
    ekl = ekm * jnp.exp(bl - bm)
    return dict(sf=sf, fg=fg, k=k, sq=sq, q=q, eb=eb, ebm=ebm, ekm=ekm, ekl=ekl, ebl=jnp.exp(bl),
                qe=q * eb, qh=q * ebm, kh=k * ekm, kd=k * ekl, causal=ri >= ci, upper=(ci >= ri).astype(BF16))


def _hgrn_fwd(proj, lbl, gn):
    t = proj.shape[0]
    nt = t // TM

    def body(q_ref, f_ref, v_ref, g_ref, lbl_ref, gn_ref, yb_ref, o_ref, st_ref, st, sf_s, b_s):
        @pl.when(pl.program_id(0) == 0)
        def _():
            st[...] = jnp.zeros_like(st)

        l = lbl_ref[...]
        lb = _sig(l[0:1, :] - l[1:2, :])
        gnv = gn_ref[...]
        _hg_decay(f_ref, lb, sf_s, b_s)

        def chunk(c, carry):
            off = pl.multiple_of(c * CH, CH)
            rows = pl.ds(off, CH)
            for grp in range(HEADS // HG):
                cols = slice(grp * HG * HD, (grp + 1) * HG * HD)
                z = _hg_chunk(q_ref[rows, cols].astype(F32), sf_s[rows, cols], b_s[rows, cols], lb[:, cols])
                v, gg = v_ref[rows, cols], g_ref[rows, cols].astype(F32)
                for k in range(HG):
                    hh = grp * HG + k
                    sl = slice(k * HD, (k + 1) * HD)
                    out = slice(hh * HD, (hh + 1) * HD)
                    s_prev = st[hh]
                    st_ref[c, hh] = s_prev
                    vb = v[:, sl].astype(BF16)
                    att = jnp.where(z["causal"], _dot(z["qh"][:, sl].astype(BF16), z["kh"][:, sl].astype(BF16), NT), 0.0)
                    o = _dot(z["qe"][:, sl].astype(BF16), s_prev.astype(BF16), NT) + _dot(att.astype(BF16), vb, NN)
                    st[hh] = s_prev * z["ebl"][:, sl] + _dot(vb, z["kd"][:, sl].astype(BF16), TN_)
                    xh, _ = _rms(o)
                    gh = gg[:, sl]
                    o_ref[rows, out] = o
                    yb_ref[rows, out] = (xh * gnv * gh * _sig(gh)).astype(BF16)
            return carry

        lax.fori_loop(0, NCH, chunk, 0, unroll=True)

    def col(j):
        return pl.BlockSpec((TM, HGW), lambda i, j=j: (i, j))

    return pl.pallas_call(
        body, name="hgrn_fwd", grid=(nt,),
        in_specs=[col(0), col(1), col(2), col(3), pl.BlockSpec((2, HGW), lambda i: (0, 0)),
                  pl.BlockSpec((1, HD), lambda i: (0, 0))],
        out_specs=[col(0), col(0), pl.BlockSpec((NCH, HEADS, HD, HD), lambda i: (i, 0, 0, 0))],
        out_shape=[jax.ShapeDtypeStruct((t, HGW), BF16), jax.ShapeDtypeStruct((t, HGW), F32),
                   jax.ShapeDtypeStruct((t // CH, HEADS, HD, HD), F32)],
        scratch_shapes=[pltpu.VMEM((HEADS, HD, HD), F32), pltpu.VMEM((TM, HGW), F32), pltpu.VMEM((TM, HGW), F32)],
        compiler_params=_cparams(("arbitrary",)),
    )(proj, proj, proj, proj, lbl, gn)


def _hgrn_bwd(dyb, proj, o, states, lbl, gn, dproj):
    t = proj.shape[0]
    nt = t // TM

    def body(dy_ref, q_ref, f_ref, v_ref, g_ref, o_ref, st_ref, lbl_ref, gn_ref, _,
             dp_ref, dgn_ref, dl_ref, dst, dlb):
        i = pl.program_id(0)

        @pl.when(i == 0)
        def _():
            dst[...] = jnp.zeros_like(dst)
            dlb[...] = jnp.zeros_like(dlb)
            dgn_ref[...] = jnp.zeros_like(dgn_ref)

        l = lbl_ref[...]
        lb = _sig(l[0:1, :] - l[1:2, :])
        gnv = gn_ref[...]
        low_c = (lax.broadcasted_iota(jnp.int32, (CH, CH), 0) >= lax.broadcasted_iota(jnp.int32, (CH, CH), 1)).astype(BF16)
        last = lax.broadcasted_iota(jnp.int32, (CH, HD), 0) == CH - 1

        def chunk(cc, carry):
            c = NCH - 1 - cc
            off = pl.multiple_of(c * CH, CH)
            rows = pl.ds(off, CH)
            dgn = jnp.zeros((1, HD), F32)
            for grp in range(HEADS // HG):
                cols = slice(grp * HG * HD, (grp + 1) * HG * HD)
                lbg = lb[:, cols]
                qr = q_ref[rows, cols].astype(F32)
                sfc = _sig(f_ref[rows, cols].astype(F32))
                z = _hg_chunk(qr, sfc, _tri_dot(low_c, jnp.log(lbg + (1.0 - lbg) * sfc)), lbg)
                v, gg, ov, dy = v_ref[rows, cols], g_ref[rows, cols].astype(F32), o_ref[rows, cols], dy_ref[rows, cols].astype(F32)
                dqs, dks, dbs, dvs, dgs = [], [], [], [], []
                for k in range(HG):
                    hh = grp * HG + k
                    sl = slice(k * HD, (k + 1) * HD)
                    s_prev, ds_new = st_ref[c, hh], dst[hh]
                    qe, qh, kh, kd = z["qe"][:, sl], z["qh"][:, sl], z["kh"][:, sl], z["kd"][:, sl]
                    ebl = z["ebl"][:, sl]
                    gh, dyh = gg[:, sl], dy[:, sl]
                    xh, rr = _rms(ov[:, sl])
                    sg = _sig(gh)
                    dyn = dyh * (gh * sg)
                    dgs.append(dyh * (xh * gnv) * (sg * (1.0 + gh * (1.0 - sg))))
                    dgn = dgn + jnp.sum(dyn * xh, axis=0, keepdims=True)
                    dxh = dyn * gnv
                    do = (rr * (dxh - xh * jnp.mean(dxh * xh, axis=-1, keepdims=True))).astype(BF16)
                    vb, dsb = v[:, sl].astype(BF16), ds_new.astype(BF16)
                    qeb, qhb, khb, kdb = (a.astype(BF16) for a in (qe, qh, kh, kd))
                    att = jnp.where(z["causal"], _dot(qhb, khb, NT), 0.0).astype(BF16)
                    datt = jnp.where(z["causal"], _dot(do, vb, NT), 0.0).astype(BF16)
                    dvs.append(_dot(att, do, TN_) + _dot(kdb, dsb, NT))
                    dqe = _dot(do, s_prev.astype(BF16), NN)
                    dqh = _dot(datt, khb, NN)
                    dkh = _dot(datt, qhb, TN_)
                    dkd = _dot(vb, dsb, NN)
                    qe, qh, kh, kd = (a.astype(F32) for a in (qeb, qhb, khb, kdb))
                    dbl = (jnp.sum(dkd * kd, axis=0, keepdims=True)
                           + jnp.sum(ds_new * s_prev, axis=0, keepdims=True) * ebl)
                    dqs.append(dqe * z["eb"][:, sl] + dqh * z["ebm"][:, sl])
                    dks.append(dkh * z["ekm"][:, sl] + dkd * z["ekl"][:, sl])
                    dbs.append(dqe * qe + dqh * qh - dkh * kh - dkd * kd + jnp.where(last, dbl, 0.0))
                    dst[hh] = _dot(do, qeb, TN_) + ds_new * ebl
                dq, dk, db = (jnp.concatenate(x, axis=1) for x in (dqs, dks, dbs))
                dlf = _tri_dot(z["upper"], db)
                sf, fg, sq = z["sf"], z["fg"], z["sq"]
                dmix = dlf / fg - dk
                dsf = dmix * (1.0 - lbg)
                dlb[:, cols] += jnp.sum(dmix * (1.0 - sf), axis=0, keepdims=True)
                dp_ref[rows, cols] = (dq * (sq * (1.0 + qr * (1.0 - sq)))).astype(BF16)
                dp_ref[rows, 1 * HGW + grp * HG * HD:1 * HGW + (grp + 1) * HG * HD] = (dsf * sf * (1.0 - sf)).astype(BF16)
                dp_ref[rows, 2 * HGW + grp * HG * HD:2 * HGW + (grp + 1) * HG * HD] = jnp.concatenate(dvs, axis=1).astype(BF16)
                dp_ref[rows, 3 * HGW + grp * HG * HD:3 * HGW + (grp + 1) * HG * HD] = jnp.concatenate(dgs, axis=1).astype(BF16)
            dgn_ref[...] += dgn
            return carry

        lax.fori_loop(0, NCH, chunk, 0, unroll=True)
        dl0 = dlb[...] * lb * (1.0 - lb)
        dl_ref[0:1, :] = dl0
        dl_ref[1:2, :] = -dl0

    def col(j):
        return pl.BlockSpec((TM, HGW), lambda i, j=j: (nt - 1 - i, j))

    return pl.pallas_call(
        body, name="hgrn_bwd", grid=(nt,),
        in_specs=[col(0), col(0), col(1), col(2), col(3), col(0),
                  pl.BlockSpec((NCH, HEADS, HD, HD), lambda i: (nt - 1 - i, 0, 0, 0)),
                  pl.BlockSpec((2, HGW), lambda i: (0, 0)), pl.BlockSpec((1, HD), lambda i: (0, 0)), ANY],
        out_specs=[pl.BlockSpec((TM, 4 * HGW), lambda i: (nt - 1 - i, 0)),
                   pl.BlockSpec((1, HD), lambda i: (0, 0)), pl.BlockSpec((2, HGW), lambda i: (0, 0))],
        out_shape=[jax.ShapeDtypeStruct((t, D_INP), BF16), jax.ShapeDtypeStruct((1, HD), F32),
                   jax.ShapeDtypeStruct((2, HGW), F32)],
        scratch_shapes=[pltpu.VMEM((HEADS, HD, HD), F32), pltpu.VMEM((1, HGW), F32)],
        input_output_aliases={9: 0},
        compiler_params=_cparams(("arbitrary",)),
    )(dyb, proj, proj, proj, proj, o, states, lbl, gn, dproj)


def _rotate_w_in(w):
    return jnp.concatenate([w[:, ROT:], jnp.zeros((w.shape[0], COL_PAD), w.dtype), w[:, :ROT]], axis=1)


def _unrotate_g_in(g):
    return jnp.concatenate([g[:, D_INP - ROT:], g[:, :D_IN - ROT]], axis=1)


SB = RGW // 2


def _bd_gates(t):
    tm = _pick(t, (768, 256))

    def col(j):
        return 2 * (j % 2) + j // 2

    return ((t // tm, 4, 1), (tm, SB, SB),
            (lambda i, j, k: (i, j // 2), lambda i, j, k: (j // 2, col(j)), lambda i, j, k: (i, col(j))))


def _bd_dw(t):
    tk = _pick(t, (1408, 768))
    return (1, 4, t // tk), (SB, SB, tk), (lambda i, j, k: (k, j % 2), lambda i, j, k: (k, j), lambda i, j, k: (j % 2, j))


def _block_diag(w):
    eye = jnp.eye(RG_BLOCKS, dtype=w.dtype)
    return (w[:, :, None, :] * eye[:, None, :, None]).reshape(RGW, RGW)


def _diag_blocks(wd):
    w4 = wd.reshape(RG_BLOCKS, RG_BLOCK, RG_BLOCKS, RG_BLOCK)
    return jnp.stack([w4[n, :, n, :] for n in range(RG_BLOCKS)])


def _local_step(h0, target, wts, small, hosts=None, finalize=None):
    hosts = hosts or {}
    carried = {}
    g = {}
    t = h0.shape[0]

    def mm(a, b, mode, out_dtype, name, **kw):
        if name not in hosts:
            return _mm(a, b, mode, out_dtype, name, **kw)
        host = hosts[name](g)
        out, res = _mm(a, b, mode, out_dtype, name, host=host, **kw)
        carried[name] = (host.ins, res)
        return out

    w_in = wts["w_in"]
    wax_d = jnp.concatenate([_block_diag(small["rg_wa"].astype(BF16)), _block_diag(small["rg_wx"].astype(BF16))], axis=1)
    ba, bx, lam = small["rg_ba"], small["rg_bx"], small["rg_lambda"]
    lbl, gn = small["hg_lb_logits"], small["hg_norm_g"]
    conv_w, conv_b = small["conv_w"], small["conv_b"]

    tmm = _pick(t, (768, 256))
    proj, hn1 = mm(h0, w_in, "nn", BF16, "mm_proj", pro=_norm_pro(small["norm1_g"]))
    if finalize is not None:
        wts = finalize(carried["mm_proj"][1])
    w_pa, w_pb, w_out, w_fi, w_fd = (wts[k] for k in ("w_proj_a", "w_proj_b", "w_out", "w_ffn_in", "w_ffn_down"))
    xcb = _conv_fwd(proj, conv_w, conv_b)
    gri = mm(xcb, wax_d, "nn", BF16, "mm_rg_gates", sparse=_bd_gates(t))
    hrg, ya = _rg_fwd(gri, xcb, proj, ba, bx, lam)
    yb, o, states = _hgrn_fwd(proj, lbl, gn)
    pa = mm(ya, w_pa, "nn", BF16, "mm_pa")
    pb = mm(yb, w_pb, "nn", BF16, "mm_pb")
    h1, merged = mm(pa, w_out, "nn", F32, "mm_out", resid=h0, pro=_merge_pro(pb, proj, tmm), tiles=(tmm, D))
    (gu, act), hn2 = mm(h1, w_fi, "nn", BF16, "mm_ffn_in", pro=_norm_pro(small["norm2_g"]), epi=_swiglu_epi(t),
                        tiles=(FM, 2 * FH))
    dh2, dh2b, loss, g["norm_f_g"] = mm(act, w_fd, "nn", F32, "mm_ffn_down", resid=h1,
                                        epi=_final_loss_epi(small["norm_f_g"], target, t, tmm), tiles=(tmm, D))

    g["w_ffn_down"] = mm(act, dh2b, "tn", BF16, "mm_d_wfd")
    (dgu,) = mm(dh2b, w_fd, "nt", BF16, "mm_d_act", epi=_swiglu_bwd_epi(gu), tiles=(FM, FH))
    g["w_ffn_in"] = mm(hn2, dgu, "tn", BF16, "mm_d_wfi")
    dh1, dh1b, g["norm2_g"] = mm(dgu, w_fi, "nt", BF16, "mm_d_hn2", epi=_rms_bwd_epi(h1, small["norm2_g"], dh2, tmm, True),
                                 tiles=(tmm, D))
    g["w_out"] = mm(merged, dh1b, "tn", BF16, "mm_d_wout")
    dpa, dpb, dproj = mm(dh1b, w_out, "nt", BF16, "mm_d_merged", epi=_merge_bwd_epi(pa, pb, proj, tmm), tiles=(tmm, D))
    dproj = _zero_pad_cols(dproj)
    g["w_proj_a"] = mm(ya, dpa, "tn", BF16, "mm_d_wpa")
    g["w_proj_b"] = mm(yb, dpb, "tn", BF16, "mm_d_wpb")
    dya = mm(dpa, w_pa, "nt", BF16, "mm_d_ya")
    dyb = mm(dpb, w_pb, "nt", BF16, "mm_d_yb")
    dproj, g["hg_norm_g"], g["hg_lb_logits"] = _hgrn_bwd(dyb, proj, o, states, lbl, gn, dproj)
    dgri, dxc, dproj, g["rg_ba"], g["rg_bx"], g["rg_lambda"] = _rg_bwd(dya, hrg, gri, xcb, proj, ba, bx, lam, dproj)
    dwax = mm(xcb, dgri, "tn", F32, "mm_d_wax", sparse=_bd_dw(t))
    g["rg_wa"], g["rg_wx"] = _diag_blocks(dwax[:, :RGW]), _diag_blocks(dwax[:, RGW:])
    dxc = mm(dgri, wax_d, "nt", BF16, "mm_d_xc", resid=dxc)
    dproj, g["conv_w"], g["conv_b"] = _conv_bwd(dxc, proj, conv_w, dproj)
    g["w_in"] = mm(hn1, dproj, "tn", BF16, "mm_d_win")
    dh0, g["norm1_g"] = mm(dproj, w_in, "nt", BF16, "mm_d_hn1", epi=_rms_bwd_epi(h0, small["norm1_g"], dh1, tmm, False),
                           tiles=(tmm, D))
    grad_x, g["meta_tokens"] = dh0[ROW0:], dh0[PAD:ROW0]
    return loss, grad_x, g, carried


def _adamw_math(w, g, m, v):
    m = B1 * m + (1.0 - B1) * g
    v = B2 * v + (1.0 - B2) * (g * g)
    m_hat = m / (1.0 - B1 ** STEP)
    v_hat = v / (1.0 - B2 ** STEP)
    return -LR * (m_hat / (jnp.sqrt(v_hat) + ADAM_EPS) + WD * w), m, v


def _adamw(w, g, m, v, name):
    r, c = w.shape
    tr = _pick(r, (256, 352, 320, 128, 64, 32, 16, 8))

    def body(w_ref, g_ref, m_ref, v_ref, d_ref, mo_ref, vo_ref):
        d_ref[...], mo_ref[...], vo_ref[...] = _adamw_math(w_ref[...], g_ref[...], m_ref[...], v_ref[...])

    blk = pl.BlockSpec((tr, c), lambda i: (i, 0))
    return pl.pallas_call(
        body, name=name, grid=(r // tr,), in_specs=[blk] * 4, out_specs=[blk] * 3,
        out_shape=[jax.ShapeDtypeStruct((r, c), F32)] * 3,
        compiler_params=_cparams(("parallel",)),
    )(w, g, m, v)


def _adamw_rows(recv, params, extras, name):
    _, _, width = recv.shape
    n_par = len(params)

    def body(*refs):
        r_ref = refs[0]
        ins = refs[1:1 + 3 * n_par]
        outs = refs[1 + 3 * n_par:]
        g_all = r_ref[0]
        for p in range(1, 8):
            g_all = g_all + r_ref[p]
        for q, (row0, w, _, _) in enumerate(params):
            g = g_all[row0:row0 + w.shape[0], :]
            w_ref, m_ref, v_ref = ins[3 * q:3 * q + 3]
            g_ref, d_ref, mo_ref, vo_ref = outs[4 * q:4 * q + 4]
            g_ref[...] = g
            d_ref[...], mo_ref[...], vo_ref[...] = _adamw_math(w_ref[...], g, m_ref[...], v_ref[...])
        for q, (row0, n) in enumerate(extras):
            outs[4 * n_par + q][...] = g_all[row0:row0 + n, :]

    shapes = [jax.ShapeDtypeStruct(w.shape, F32) for _, w, _, _ in params for _ in range(4)]
    shapes += [jax.ShapeDtypeStruct((n, width), F32) for _, n in extras]
    vm = pl.BlockSpec(memory_space=pltpu.VMEM)
    outs = pl.pallas_call(
        body, name=name, in_specs=[vm] * (1 + 3 * n_par), out_specs=[vm] * len(shapes), out_shape=shapes,
        compiler_params=pltpu.CompilerParams(vmem_limit_bytes=VMEM_LIMIT),
    )(recv, *[a for _, w, m, v in params for a in (w, m, v)])
    return [tuple(outs[4 * q:4 * q + 4]) for q in range(n_par)], list(outs[4 * n_par:])


def _col_block(weight, px, py):
    return 2 * py + px if weight == "w_ffn_in" else 2 * px + py


def _sum_place(recv, gfull, weight, xyc, name):
    _, r, c = recv.shape
    tr = _pick(r, (128, 176, 160, 64, 32, 16))
    nb = r // tr

    def body(_, own_ref, r_ref, o_ref):
        g = own_ref[...].astype(F32)
        for p in range(7):
            g = g + r_ref[p].astype(F32)
        o_ref[...] = g

    if weight in COL_SHARDED:
        own = pl.BlockSpec((tr, c), lambda i, s: (s[2] * nb + i, _col_block(weight, s[0], s[1])))
    else:
        own = pl.BlockSpec((tr, c), lambda i, s: ((4 * s[0] + 2 * s[1] + s[2]) * nb + i, 0))
    return pl.pallas_call(
        body, name=name,
        grid_spec=pltpu.PrefetchScalarGridSpec(
            num_scalar_prefetch=1, grid=(nb,),
            in_specs=[own, pl.BlockSpec((7, tr, c), lambda i, s: (0, i, 0))],
            out_specs=pl.BlockSpec((tr, c), lambda i, s: (s[2] * nb + i, 0))),
        out_shape=jax.ShapeDtypeStruct((2 * r, c), F32),
        compiler_params=_cparams(("arbitrary",)),
    )(xyc, gfull, recv)


def _place_own(shard, slot, dtype, name):
    r, c = shard.shape
    tr = _pick(r, (256, 352, 320, 128, 32))

    def body(_, x_ref, o_ref):
        o_ref[...] = x_ref[...].astype(dtype)

    return pl.pallas_call(
        body, name=name,
        grid_spec=pltpu.PrefetchScalarGridSpec(
            num_scalar_prefetch=1, grid=(r // tr,),
            in_specs=[pl.BlockSpec((tr, c), lambda i, s: (i, 0))],
            out_specs=pl.BlockSpec((None, tr, c), lambda i, s: (s[0], i, 0))),
        out_shape=jax.ShapeDtypeStruct((4, r, c), dtype),
        compiler_params=_cparams(("arbitrary",)),
    )(slot, shard)


def _place_cols(shard, slot, name):
    r, c = shard.shape
    tr = _pick(r, (256, 128))

    def body(_, x_ref, o_ref):
        o_ref[...] = x_ref[...].astype(BF16)

    return pl.pallas_call(
        body, name=name,
        grid_spec=pltpu.PrefetchScalarGridSpec(
            num_scalar_prefetch=1, grid=(r // tr,),
            in_specs=[pl.BlockSpec((tr, c), lambda i, s: (i, 0))],
            out_specs=pl.BlockSpec((tr, c), lambda i, s: (i, 2 * (s[0] % 2) + s[0] // 2))),
        out_shape=jax.ShapeDtypeStruct((r, 4 * c), BF16),
        compiler_params=_cparams(("arbitrary",)),
    )(slot, shard)


def _place_w_in(shard, slot):
    r, c = shard.shape
    nb, npad = c // 128, COL_PAD // 128
    rot_blocks, in_blocks = ROT // 128, D_IN // 128

    def out_block(j, s):
        nat = s[0] * nb + jnp.minimum(j, nb - 1)
        rot = jnp.where(nat >= rot_blocks, nat - rot_blocks, nat + (D_INP - ROT) // 128)
        return jnp.where(j < nb, rot, (in_blocks - rot_blocks) + (j - nb))

    def body(_, x_ref, o_ref):
        o_ref[...] = jnp.where(pl.program_id(0) < nb, x_ref[...], 0.0).astype(BF16)

    return pl.pallas_call(
        body, name="place_w_in",
        grid_spec=pltpu.PrefetchScalarGridSpec(
            num_scalar_prefetch=1, grid=(nb + npad,),
            in_specs=[pl.BlockSpec((r, 128), lambda j, s: (0, jnp.minimum(j, nb - 1)))],
            out_specs=pl.BlockSpec((r, 128), lambda j, s: (0, out_block(j, s)))),
        out_shape=jax.ShapeDtypeStruct((r, D_INP), BF16),
        compiler_params=_cparams(("arbitrary",)),
    )(slot, shard)


BIG = ("w_in", "w_proj_a", "w_proj_b", "w_out", "w_ffn_in", "w_ffn_down")
COL_SHARDED = ("w_in", "w_ffn_in")


def _coords():
    return lax.axis_index("x"), lax.axis_index("y"), lax.axis_index("c")


W_IN_HEAD = 384


def _shard_views(b, layout, chip, half_of):
    if layout == "slots":
        half = b.shape[1] // 2
        return [b.at[chip, pl.ds(pl.multiple_of(half_of * half, 8), half)]]
    half = b.shape[0] // 2
    rows = pl.ds(pl.multiple_of(half_of * half, 8), half)
    if layout == "cols":
        c = b.shape[1] // 4
        return [b.at[rows, pl.ds(pl.multiple_of((2 * (chip % 2) + chip // 2) * c, 128), c)]]
    c = D_IN // 4
    views = []
    for first, width in ((0, W_IN_HEAD), (W_IN_HEAD, c - W_IN_HEAD)):
        nat = chip * c + first
        rot = jnp.where(nat >= ROT, nat - ROT, nat + D_INP - ROT)
        views.append(b.at[rows, pl.ds(pl.multiple_of(rot, 128), width)])
    return views


def _gather_host(bufs, layouts):
    n = len(bufs)

    def place():
        x, y, c = _coords()
        return x, y, c, [(1 - x, y), (x, 1 - y), (1 - x, 1 - y)]

    def copies(b, sems, w, k, chip, half_of, to):
        return [pltpu.make_async_remote_copy(src_ref=v, dst_ref=v, send_sem=sems[0].at[w, k, p], recv_sem=sems[1].at[w, k, p],
                                             device_id=to, device_id_type=MESH)
                for p, v in enumerate(_shard_views(b, layouts[w], chip, half_of))]

    def start(_, outs, sems):
        x, y, c, chips = place()
        for w, b in enumerate(outs):
            for k, (cx, cy) in enumerate(chips):
                for cp in copies(b, sems, w, k, 2 * x + y, c, (cx, cy, c)):
                    cp.start()

    def mid(_, outs, sems):
        x, y, c, chips = place()
        for w, b in enumerate(outs):
            for k, (cx, cy) in enumerate(chips):
                for cp in copies(b, sems, w, k, 2 * cx + cy, c, (cx, cy, c)):
                    cp.wait_recv()
                for cp in copies(b, sems, w, 3 + k, 2 * cx + cy, c, (x, y, 1 - c)):
                    cp.start()

    def finish(_, outs, sems):
        x, y, c, chips = place()
        for w, b in enumerate(outs):
            for k, (cx, cy) in enumerate(chips):
                for cp in copies(b, sems, w, 3 + k, 2 * cx + cy, 1 - c, (x, y, 1 - c)):
                    cp.wait_recv()
        for w, b in enumerate(outs):
            for k, (cx, cy) in enumerate(chips):
                for cp in copies(b, sems, w, k, 2 * x + y, c, (cx, cy, c)) + copies(b, sems, w, 3 + k, 2 * cx + cy, c, (x, y, 1 - c)):
                    cp.wait_send()

    return _Host(ins=list(bufs), out_shapes=[jax.ShapeDtypeStruct(b.shape, b.dtype) for b in bufs],
                 aliases={i: i for i in range(n)},
                 sems=[pltpu.SemaphoreType.DMA((n, 6, 2)), pltpu.SemaphoreType.DMA((n, 6, 2))],
                 start=start, mid=mid, finish=finish)


def _run_host(host, name):
    n_in, n_out = len(host.ins), len(host.out_shapes)

    def body(*refs):
        ins, outs, sems = refs[:n_in], refs[n_in:n_in + n_out], refs[n_in + n_out:]
        host.start(ins, outs, sems)
        if host.mid is not None:
            host.mid(ins, outs, sems)
        host.finish(ins, outs, sems)

    return pl.pallas_call(
        body, name=name, in_specs=[ANY] * n_in, out_specs=[ANY] * n_out, out_shape=list(host.out_shapes),
        scratch_shapes=list(host.sems), input_output_aliases=dict(host.aliases),
    )(*host.ins)


def _peer(x, y, c, k):
    fx, fy, fc = (k >> 2) & 1, (k >> 1) & 1, k & 1
    return (x ^ fx if fx else x, y ^ fy if fy else y, c ^ fc if fc else c)


def _sub_shape(name, full_shape):
    r, c = full_shape
    return (r // 2, c // 4) if name in COL_SHARDED else (r // 8, c)


def _exchange_host(names, grads):
    n = len(names)
    shapes = [_sub_shape(k, g.shape) for k, g in zip(names, grads)]

    def copy(ins, outs, sems, w, k):
        x, y, c = _coords()
        px, py, pc = _peer(x, y, c, k)
        sr, sc = shapes[w]
        if names[w] in COL_SHARDED:
            col = _col_block(names[w], px, py) * sc
            src = ins[w].at[pl.ds(pl.multiple_of(pc * sr, 16), sr), pl.ds(pl.multiple_of(col, 128), sc)]
        else:
            src = ins[w].at[pl.ds(pl.multiple_of((4 * px + 2 * py + pc) * sr, 16), sr)]
        return pltpu.make_async_remote_copy(
            src_ref=src, dst_ref=outs[w].at[k - 1], send_sem=sems[0].at[w, k - 1], recv_sem=sems[1].at[w, k - 1],
            device_id=(px, py, pc), device_id_type=MESH)

    def start(ins, outs, sems):
        for w in range(n):
            for k in range(1, 8):
                copy(ins, outs, sems, w, k).start()

    def finish(ins, outs, sems):
        for w in range(n):
            for k in range(1, 8):
                copy(ins, outs, sems, w, k).wait_recv()
        for w in range(n):
            for k in range(1, 8):
                copy(ins, outs, sems, w, k).wait_send()

    return _Host(ins=list(grads), out_shapes=[jax.ShapeDtypeStruct((7,) + s, g.dtype) for s, g in zip(shapes, grads)],
                 aliases={}, sems=[pltpu.SemaphoreType.DMA((n, 7)), pltpu.SemaphoreType.DMA((n, 7))],
                 start=start, mid=None, finish=finish)


def _sibling_host(bufs):
    n = len(bufs)

    def copy(outs, sems, w, half_of):
        x, y, c = _coords()
        half = outs[w].shape[0] // 2
        rows = outs[w].at[pl.ds(pl.multiple_of(half_of * half, 8), half)]
        return pltpu.make_async_remote_copy(src_ref=rows, dst_ref=rows, send_sem=sems[0].at[w], recv_sem=sems[1].at[w],
                                            device_id=(x, y, 1 - c), device_id_type=MESH)

    def start(_, outs, sems):
        c = lax.axis_index("c")
        for w in range(n):
            copy(outs, sems, w, c).start()

    def finish(_, outs, sems):
        c = lax.axis_index("c")
        for w in range(n):
            copy(outs, sems, w, 1 - c).wait_recv()
        for w in range(n):
            copy(outs, sems, w, c).wait_send()

    return _Host(ins=list(bufs), out_shapes=[jax.ShapeDtypeStruct(b.shape, b.dtype) for b in bufs],
                 aliases={i: i for i in range(n)},
                 sems=[pltpu.SemaphoreType.DMA((n,)), pltpu.SemaphoreType.DMA((n,))], start=start, mid=None, finish=finish)


def _pack_host(pack):
    def me_of():
        x, y, c = _coords()
        return x, y, c, 4 * x + 2 * y + c

    def copy(ins, outs, sems, k, slot):
        x, y, c, _ = me_of()
        return pltpu.make_async_remote_copy(src_ref=ins[0], dst_ref=outs[0].at[slot], send_sem=sems[0].at[k - 1],
                                            recv_sem=sems[1].at[k - 1], device_id=_peer(x, y, c, k), device_id_type=MESH)

    def start(ins, outs, sems):
        me = me_of()[3]
        pltpu.make_async_copy(ins[0], outs[0].at[me], sems[2]).start()
        for k in range(1, 8):
            copy(ins, outs, sems, k, me).start()

    def finish(ins, outs, sems):
        x, y, c, me = me_of()
        for k in range(1, 8):
            px, py, pc = _peer(x, y, c, k)
            copy(ins, outs, sems, k, 4 * px + 2 * py + pc).wait_recv()
        for k in range(1, 8):
            copy(ins, outs, sems, k, me).wait_send()
        pltpu.make_async_copy(ins[0], outs[0].at[me], sems[2]).wait()

    return _Host(ins=[pack], out_shapes=[jax.ShapeDtypeStruct((8,) + pack.shape, F32)], aliases={},
                 sems=[pltpu.SemaphoreType.DMA((7,)), pltpu.SemaphoreType.DMA((7,)), pltpu.SemaphoreType.DMA],
                 start=start, mid=None, finish=finish)


def _join_hosts(hosts):
    ins, outs, sems, aliases, spans = [], [], [], {}, []
    for h in hosts:
        spans.append((len(ins), len(h.ins), len(outs), len(h.out_shapes), len(sems), len(h.sems)))
        for i_in, i_out in h.aliases.items():
            aliases[len(ins) + i_in] = len(outs) + i_out
        ins, outs, sems = ins + list(h.ins), outs + list(h.out_shapes), sems + list(h.sems)

    def phase(which):
        fns = [getattr(h, which) for h in hosts]
        if all(f is None for f in fns):
            return None

        def run(i_refs, o_refs, s_refs):
            for f, (i0, n_i, o0, n_o, s0, n_s) in zip(fns, spans):
                if f is not None:
                    f(i_refs[i0:i0 + n_i], o_refs[o0:o0 + n_o], s_refs[s0:s0 + n_s])

        return run

    return _Host(ins, outs, aliases, sems, phase("start"), phase("mid"), phase("finish"))


SHARDED_SMALL = ("meta_tokens", "conv_w")

ROWS = {"conv_b": (1, RGW), "rg_ba": (1, RGW), "rg_bx": (1, RGW), "rg_lambda": (1, RGW), "conv_w": (4, RGW),
        "norm2_g": (1, D), "norm_f_g": (1, D), "hg_lb_logits": (2, D),
        "rg_wa": (RGW * RG_BLOCK // 128, 128), "rg_wx": (RGW * RG_BLOCK // 128, 128), "hg_norm_g": (1, 128),
        "norm1_g": (1, D), "meta_tokens": (N_META, D), "loss": (1, D)}
PACKS = {"early_wide": ("conv_b", "rg_ba", "rg_bx", "rg_lambda", "conv_w"),
         "early_model": ("norm2_g", "norm_f_g", "hg_lb_logits"),
         "early_lane": ("rg_wa", "rg_wx", "hg_norm_g"),
         "late": ("norm1_g", "meta_tokens", "loss")}
EARLY = ("early_wide", "early_model", "early_lane")
NO_UPDATE = ("conv_w", "meta_tokens", "loss")


def _pack_rows(vals, names):
    parts, first, row = [], {}, 0
    for k in names:
        first[k] = row
        parts.append(vals[k].reshape(ROWS[k]).astype(F32))
        row += ROWS[k][0]
    if row % 8:
        parts.append(jnp.zeros((-row % 8, ROWS[names[0]][1]), F32))
    return jnp.concatenate(parts, axis=0), first


ORDER = ("meta_tokens", "norm1_g", "w_in", "conv_w", "conv_b", "rg_wa", "rg_ba", "rg_wx", "rg_bx", "rg_lambda",
         "hg_lb_logits", "hg_norm_g", "w_proj_a", "w_proj_b", "w_out", "norm2_g", "w_ffn_in", "w_ffn_down", "norm_f_g")


def kernel(x, meta_tokens, norm1_g, w_in, conv_w, conv_b, rg_wa, rg_ba, rg_wx, rg_bx, rg_lambda, hg_lb_logits, hg_norm_g, w_proj_a, w_proj_b, w_out, norm2_g, w_ffn_in, w_ffn_down, norm_f_g, loss_target, m_meta_tokens, m_norm1_g, m_w_in, m_conv_w, m_conv_b, m_rg_wa, m_rg_ba, m_rg_wx, m_rg_bx, m_rg_lambda, m_hg_lb_logits, m_hg_norm_g, m_w_proj_a, m_w_proj_b, m_w_out, m_norm2_g, m_w_ffn_in, m_w_ffn_down, m_norm_f_g, v_meta_tokens, v_norm1_g, v_w_in, v_conv_w, v_conv_b, v_rg_wa, v_rg_ba, v_rg_wx, v_rg_bx, v_rg_lambda, v_hg_lb_logits, v_hg_norm_g, v_w_proj_a, v_w_proj_b, v_w_out, v_norm2_g, v_w_ffn_in, v_w_ffn_down, v_norm_f_g):
    args = dict(locals())
    w = {k: args[k] for k in ORDER}
    m = {k: args["m_" + k] for k in ORDER}
    v = {k: args["v_" + k] for k in ORDER}
    xi, yi, ci = _coords()
    chip = 2 * xi + yi
    slot = jnp.reshape(chip, (1,)).astype(jnp.int32)
    xyc = jnp.stack([xi, yi, ci]).astype(jnp.int32)

    tiny = jnp.zeros((32, 384), F32)
    tiny = tiny.at[0:N_META, 0:256].set(meta_tokens).at[N_META:N_META + 4, 0:320].set(conv_w[0])
    later = [k for k in BIG if k != "w_in"]
    layouts = {k: "cols" if k in COL_SHARDED else "slots" for k in later}
    slots = {k: _place_cols(w[k][0], slot, "place_" + k) if layouts[k] == "cols"
             else _place_own(w[k][0], slot, BF16, "place_" + k) for k in later}
    w_in_full, tiny_all = _run_host(
        _gather_host([_place_w_in(w["w_in"][0], slot), _place_own(tiny, slot, F32, "place_tiny")], ["w_in", "slots"]),
        "allgather_w_in")

    def full_matrix(k, gth):
        return gth if layouts[k] == "cols" else gth.reshape(-1, gth.shape[2])

    meta_full = jnp.transpose(tiny_all[:, 0:N_META, 0:256], (1, 0, 2)).reshape(N_META, D)
    conv_w_full = jnp.transpose(tiny_all[:, N_META:N_META + 4, 0:320], (1, 0, 2)).reshape(4, RGW)

    small = dict(norm1_g=norm1_g, conv_w=conv_w_full, conv_b=conv_b, rg_wa=rg_wa[0], rg_ba=rg_ba, rg_wx=rg_wx[0],
                 rg_bx=rg_bx, rg_lambda=rg_lambda, hg_lb_logits=hg_lb_logits, hg_norm_g=hg_norm_g,
                 norm2_g=norm2_g, norm_f_g=norm_f_g.reshape(1, D))

    hosts = {
        "mm_proj": lambda g: _gather_host([slots[k] for k in later], [layouts[k] for k in later]),
        "mm_d_act": lambda g: _exchange_host(["w_ffn_down"], [g["w_ffn_down"]]),
        "mm_d_hn2": lambda g: _exchange_host(["w_ffn_in"], [g["w_ffn_in"]]),
        "mm_d_win": lambda g: _join_hosts([
            _exchange_host(["w_out", "w_proj_a", "w_proj_b"], [g["w_out"], g["w_proj_a"], g["w_proj_b"]]),
            *[_pack_host(_pack_rows(g, PACKS[p])[0]) for p in EARLY]]),
        "mm_d_hn1": lambda g: _exchange_host(["w_in"], [_unrotate_g_in(g["w_in"])]),
    }
    h0 = jnp.concatenate([jnp.zeros((PAD, D), F32), meta_full, x[0]], axis=0)
    loss_blk, grad_x, g, carried = _local_step(
        h0, loss_target[0], {"w_in": w_in_full}, small, hosts=hosts,
        finalize=lambda gathered: {k: full_matrix(k, gth) for k, gth in zip(later, gathered)})
    g["loss"] = jnp.broadcast_to(loss_blk[0:1, 0:1], (1, D))

    halves = {}
    sources = {"mm_d_act": ["w_ffn_down"], "mm_d_hn2": ["w_ffn_in"], "mm_d_win": ["w_out", "w_proj_a", "w_proj_b"],
               "mm_d_hn1": ["w_in"]}
    for name, keys in sources.items():
        partials, received = carried[name]
        for k, part, rec in zip(keys, partials, received):
            halves[k] = _sum_place(rec, part, k, xyc, "sum_" + k)
    late, _ = _pack_rows(g, PACKS["late"])
    tail = _run_host(_join_hosts([_sibling_host([halves[k] for k in BIG]), _pack_host(late)]), "swap_and_late_pack")
    g_big = dict(zip(BIG, tail[:len(BIG)]))
    recv_packs = dict(zip(EARLY, carried["mm_d_win"][1][3:]), late=tail[len(BIG)])

    grad, delta, new_m, new_v, summed = {}, {}, {}, {}, {}
    for pack, names in PACKS.items():
        row, params, extras = 0, [], []
        for k in names:
            if k in NO_UPDATE:
                extras.append((row, ROWS[k][0]))
            else:
                params.append((row,) + tuple(d[k].reshape(ROWS[k]) for d in (w, m, v)))
            row += ROWS[k][0]
        updated, rows_only = _adamw_rows(recv_packs[pack], params, extras, "adamw_" + pack)
        for k, res in zip([k for k in names if k not in NO_UPDATE], updated):
            grad[k], delta[k], new_m[k], new_v[k] = (a.reshape(w[k].shape) for a in res)
        summed.update(zip([k for k in names if k in NO_UPDATE], rows_only))
    loss = summed["loss"][0, 0]
    g_shard = {"meta_tokens": lax.dynamic_slice(summed["meta_tokens"], (0, chip * 256), (N_META, 256)),
               "conv_w": lax.dynamic_slice(summed["conv_w"], (0, chip * 320), (4, 320))}
    for k in BIG + SHARDED_SMALL:
        gk = g_big[k] if k in BIG else g_shard[k]
        wk, mk, vk = (d[k].reshape(gk.shape) for d in (w, m, v))
        dk, mk, vk = _adamw(wk, gk, mk, vk, "adamw_" + k)
        grad[k], delta[k], new_m[k], new_v[k] = (a.reshape(w[k].shape) for a in (gk, dk, mk, vk))

    return (loss, grad_x[None], *[grad[k] for k in ORDER], *[delta[k] for k in ORDER],
            *[new_m[k] for k in ORDER], *[new_v[k] for k in ORDER])
```
